```python
import jax, jax.numpy as jnp
from jax import lax
import numpy as np

D_MODEL = 2048
BATCH = 8
SEQ = 4096
DEPTH = 1

N_HEADS = 16
N_KV_HEADS = 4
HEAD_DIM = 64
GROUP = N_HEADS // N_KV_HEADS
ROT_DIM = HEAD_DIM // 4
ROPE_THETA = 500000.0
WINDOW = 128
BLOCK = 128
ATTN_WIDTH = N_HEADS * HEAD_DIM
KV_WIDTH = N_KV_HEADS * HEAD_DIM
POOL_WINDOWS = (2, 4, 8, 16)
N_POOL_GROUPS = len(POOL_WINDOWS)
POOL_WIDTH = D_MODEL // 2
POOL_GROUP_DIM = POOL_WIDTH // N_POOL_GROUPS
MIX_WIDTH = ATTN_WIDTH + POOL_WIDTH
IN_WIDTH = ATTN_WIDTH + 2 * KV_WIDTH + POOL_WIDTH
N_EXPERT_GROUPS = 4
EXPERTS_PER_GROUP = 8
N_EXPERTS = N_EXPERT_GROUPS * EXPERTS_PER_GROUP
TOP_K = 2
D_EXPERT = 512
ROW_BLOCK = 256
EPS = 1e-6

kernel_name = "hymba_swa_sink_pool_hmoe_block"


def rms_norm(t, gain):
    tf = t.astype(jnp.float32)
    tf = tf * lax.rsqrt(jnp.mean(tf * tf, axis=-1, keepdims=True) + EPS)
    return (tf * gain.astype(jnp.float32)).astype(t.dtype)


def partial_rope(t, cos, sin):
    half = ROT_DIM // 2
    x1 = t[..., :half]
    x2 = t[..., half:ROT_DIM]
    c = cos[None, :, None, :].astype(t.dtype)
    s = sin[None, :, None, :].astype(t.dtype)
    return jnp.concatenate([x1 * c - x2 * s, x2 * c + x1 * s, t[..., ROT_DIM:]], axis=-1)


def sliding_window_attention(q, k, v, sinks):
    B, S, _, _ = q.shape
    nb = S // BLOCK
    qb = q.reshape(B, nb, BLOCK, N_KV_HEADS, GROUP, HEAD_DIM)

    def band(t):
        tb = t.reshape(B, nb, BLOCK, N_KV_HEADS, HEAD_DIM)
        prev = jnp.pad(tb, ((0, 0), (1, 0), (0, 0), (0, 0), (0, 0)))[:, :nb]
        return jnp.concatenate([prev, tb], axis=2)

    kb, vb = band(k), band(v)
    scale = HEAD_DIM ** -0.5
    s = jnp.einsum('bnqkgd,bnskd->bnkgqs', qb, kb).astype(jnp.float32) * scale
    qi = jnp.arange(BLOCK)[:, None] + BLOCK
    si = jnp.arange(2 * BLOCK)[None, :]
    diff = qi - si
    local = (diff >= 0) & (diff < WINDOW)
    blk = jnp.arange(nb)[:, None, None]
    valid = local[None] & ((si[None] >= BLOCK) | (blk > 0))
    s = jnp.where(valid[None, :, None, None], s, -1e30)
    sink = sinks.astype(jnp.float32).reshape(N_KV_HEADS, GROUP)[:, :, None]
    m = jnp.maximum(jnp.max(s, axis=-1), sink)
    p = jnp.exp(s - m[..., None])
    denom = jnp.sum(p, axis=-1) + jnp.exp(sink - m)
    probs = (p / denom[..., None]).astype(v.dtype)
    o = jnp.einsum('bnkgqs,bnskd->bnqkgd', probs, vb)
    return o.reshape(B, S, ATTN_WIDTH)


def multiscale_pool(u, w_pool, pool_scale):
    B, S, _ = u.shape
    uf = u.astype(jnp.float32)
    cs = jnp.cumsum(uf, axis=1)
    pos = jnp.arange(S)
    outs = []
    for g, w in enumerate(POOL_WINDOWS):
        sl = slice(g * POOL_GROUP_DIM, (g + 1) * POOL_GROUP_DIM)
        cg = cs[..., sl]
        lag = jnp.pad(cg, ((0, 0), (w, 0), (0, 0)))[:, :S]
        cnt = jnp.minimum(pos + 1, w).astype(jnp.float32)[None, :, None]
        outs.append((cg - lag) / cnt - uf[..., sl])
    d = jnp.stack(outs, axis=2).astype(u.dtype)
    y = jnp.einsum('bsgc,gcd->bsgd', d, w_pool).reshape(B, S, POOL_WIDTH)
    return y * pool_scale


def hierarchical_moe(h, w_coarse, b_coarse, w_fine, b_fine, w_gate, w_up, w_down):
    B, S, D = h.shape
    N = B * S
    hf = h.reshape(N, D)
    coarse = (hf @ w_coarse).astype(jnp.float32) + b_coarse.astype(jnp.float32)
    p_coarse = jax.nn.softmax(coarse, axis=-1)
    grp = jnp.argmax(coarse, axis=-1)
    p_grp = jnp.take_along_axis(p_coarse, grp[:, None], axis=1)[:, 0]
    fine = ((hf @ w_fine).astype(jnp.float32) + b_fine.astype(jnp.float32)).reshape(
        N, N_EXPERT_GROUPS, EXPERTS_PER_GROUP)
    fine_sel = jnp.take_along_axis(fine, grp[:, None, None], axis=1)[:, 0]
    p_fine = jax.nn.softmax(fine_sel, axis=-1)
    top_p, top_i = lax.top_k(p_fine, TOP_K)
    gates = p_grp[:, None] * top_p / jnp.sum(top_p, axis=-1, keepdims=True)

    A = N * TOP_K
    eid = (grp[:, None] * EXPERTS_PER_GROUP + top_i).reshape(A)
    tok = jnp.repeat(jnp.arange(N), TOP_K)
    gate = gates.reshape(A)
    order = jnp.argsort(eid)
    s_eid, s_tok, s_gate = eid[order], tok[order], gate[order]
    counts = jnp.bincount(eid, length=N_EXPERTS)
    padded = ((counts + ROW_BLOCK - 1) // ROW_BLOCK) * ROW_BLOCK
    start = jnp.cumsum(counts) - counts
    pad_end = jnp.cumsum(padded)
    pad_start = pad_end - padded
    dest = pad_start[s_eid] + jnp.arange(A) - start[s_eid]
    n_blocks = -(-A // ROW_BLOCK) + N_EXPERTS
    n_rows = n_blocks * ROW_BLOCK
    row_tok = jnp.full((n_rows,), N, dtype=jnp.int32).at[dest].set(s_tok.astype(jnp.int32))
    row_gate = jnp.zeros((n_rows,), jnp.float32).at[dest].set(s_gate)
    blk_eid = jnp.clip(jnp.searchsorted(pad_end, jnp.arange(n_blocks) * ROW_BLOCK, side='right'),
                       0, N_EXPERTS - 1)
    h_ext = jnp.concatenate([hf, jnp.zeros((1, D), hf.dtype)], axis=0)
    xs = h_ext[row_tok].reshape(n_blocks, ROW_BLOCK, D)

    def expert_block(args):
        xb, e = args
        a = xb @ w_gate[e]
        b = xb @ w_up[e]
        return (jax.nn.silu(a) * b) @ w_down[e]

    ys = lax.map(expert_block, (xs, blk_eid)).reshape(n_rows, D)
    ys = ys * row_gate[:, None].astype(ys.dtype)
    out = jnp.zeros((N, D), ys.dtype).at[row_tok].add(ys, mode='drop')
    return out.reshape(B, S, D)


def setup_inputs(seed: int = 0) -> dict:
    key = jax.random.key(seed)
    ks = jax.random.split(key, 18)
    f32 = jnp.float32
    nrm = lambda k, shape, scale: jax.random.normal(k, shape, f32) * scale
    L = DEPTH
    return {
        "x": jax.random.normal(ks[0], (BATCH, SEQ, D_MODEL), f32),
        "norm_mix": 1.0 + nrm(ks[1], (L, D_MODEL), 0.02),
        "w_in": nrm(ks[2], (L, D_MODEL, IN_WIDTH), D_MODEL ** -0.5),
        "q_norm": 1.0 + nrm(ks[3], (L, HEAD_DIM), 0.02),
        "k_norm": 1.0 + nrm(ks[4], (L, HEAD_DIM), 0.02),
        "sinks": nrm(ks[5], (L, N_HEADS), 1.0),
        "w_pool": nrm(ks[6], (L, N_POOL_GROUPS, POOL_GROUP_DIM, POOL_GROUP_DIM), POOL_GROUP_DIM ** -0.5),
        "pool_scale": 1.0 + nrm(ks[7], (L, POOL_WIDTH), 0.1),
        "w_out": nrm(ks[8], (L, MIX_WIDTH, D_MODEL), MIX_WIDTH ** -0.5),
        "norm_ffn": 1.0 + nrm(ks[9], (L, D_MODEL), 0.02),
        "w_coarse": nrm(ks[10], (L, D_MODEL, N_EXPERT_GROUPS), D_MODEL ** -0.5),
        "b_coarse": nrm(ks[11], (L, N_EXPERT_GROUPS), 0.01),
        "w_fine": nrm(ks[12], (L, D_MODEL, N_EXPERTS), D_MODEL ** -0.5),
        "b_fine": nrm(ks[13], (L, N_EXPERTS), 0.01),
        "w_gate": nrm(ks[14], (L, N_EXPERTS, D_MODEL, D_EXPERT), D_MODEL ** -0.5),
        "w_up": nrm(ks[15], (L, N_EXPERTS, D_MODEL, D_EXPERT), D_MODEL ** -0.5),
        "w_down": nrm(ks[16], (L, N_EXPERTS, D_EXPERT, D_MODEL), D_EXPERT ** -0.5),
    }


def reference(x, norm_mix, w_in, q_norm, k_norm, sinks, w_pool, pool_scale, w_out,
              norm_ffn, w_coarse, b_coarse, w_fine, b_fine, w_gate, w_up, w_down):
    B, S, _ = x.shape
    pos = jnp.arange(S, dtype=jnp.float32)
    inv_freq = ROPE_THETA ** (-jnp.arange(0, ROT_DIM, 2, dtype=jnp.float32) / ROT_DIM)
    ang = pos[:, None] * inv_freq[None, :]
    cos, sin = jnp.cos(ang), jnp.sin(ang)
    h = x
    for l in range(DEPTH):
        hn = rms_norm(h, norm_mix[l])
        proj = hn @ w_in[l]
        q = proj[..., :ATTN_WIDTH].reshape(B, S, N_HEADS, HEAD_DIM)
        k = proj[..., ATTN_WIDTH:ATTN_WIDTH + KV_WIDTH].reshape(B, S, N_KV_HEADS, HEAD_DIM)
        v = proj[..., ATTN_WIDTH + KV_WIDTH:ATTN_WIDTH + 2 * KV_WIDTH].reshape(B, S, N_KV_HEADS, HEAD_DIM)
        u = proj[..., ATTN_WIDTH + 2 * KV_WIDTH:]
        q = partial_rope(rms_norm(q, q_norm[l]), cos, sin)
        k = partial_rope(rms_norm(k, k_norm[l]), cos, sin)
        attn = sliding_window_attention(q, k, v, sinks[l])
        pool = multiscale_pool(u, w_pool[l], pool_scale[l])
        mixed = jnp.concatenate([attn, pool.astype(attn.dtype)], axis=-1)
        h = h + mixed @ w_out[l]
        hn = rms_norm(h, norm_ffn[l])
        h = h + hierarchical_moe(hn, w_coarse[l], b_coarse[l], w_fine[l], b_fine[l],
                                 w_gate[l], w_up[l], w_down[l])
    return h
```

```python
import functools

import jax
import jax.numpy as jnp
from jax import lax
from jax.experimental import pallas as pl
from jax.experimental.pallas import tpu as pltpu

F32 = jnp.float32
BF16 = jnp.bfloat16

D_MODEL = 2048
N_HEADS = 16
N_KV_HEADS = 4
HEAD_DIM = 64
GROUP = N_HEADS // N_KV_HEADS
ROT_DIM = HEAD_DIM // 4
ROPE_THETA = 500000.0
WINDOW = 128
BLOCK = 128
ATTN_WIDTH = N_HEADS * HEAD_DIM
KV_WIDTH = N_KV_HEADS * HEAD_DIM
POOL_WINDOWS = (2, 4, 8, 16)
POOL_WIDTH = D_MODEL // 2
POOL_GROUP_DIM = POOL_WIDTH // len(POOL_WINDOWS)
N_EXPERT_GROUPS = 4
EXPERTS_PER_GROUP = 8
N_EXPERTS = N_EXPERT_GROUPS * EXPERTS_PER_GROUP
TOP_K = 2
D_EXPERT = 512
EPS = 1e-6

LANES = 128
MXU_DIM = 256
HALO = 16
NEG_BIG = -1e30
HALF = D_MODEL // 2

ROW_TILE = 256
EXPERT_TILE = 256
COMBINE_TILE = 256
VMEM_LIMIT = 56 * 1024 * 1024


def _cparams(n_axes=1):
    return pltpu.CompilerParams(
        dimension_semantics=("arbitrary",) * n_axes,
        vmem_limit_bytes=VMEM_LIMIT,
    )


def _head_sumsq(t, e_ref):
    t2 = t * t
    hi = t2.astype(BF16)
    lo = (t2 - hi.astype(F32)).astype(BF16)
    e = e_ref[...]
    outs = []
    for c in range(t.shape[1] // MXU_DIM):
        sl = slice(c * MXU_DIM, (c + 1) * MXU_DIM)
        outs.append(jnp.dot(hi[:, sl], e, preferred_element_type=F32)
                    + jnp.dot(lo[:, sl], e, preferred_element_type=F32))
    return outs[0] if len(outs) == 1 else jnp.concatenate(outs, axis=1)


def _norm_rope(t, gain, e_ref, cc, s_up, s_dn, out_scale):
    w = t.shape[1]
    ss = _head_sumsq(t, e_ref)
    tn = t * lax.rsqrt(ss * (1.0 / HEAD_DIM) + EPS) * gain
    reps = w // LANES
    up = pltpu.roll(tn, w - ROT_DIM // 2, axis=1)
    dn = pltpu.roll(tn, ROT_DIM // 2, axis=1)
    c = jnp.concatenate([cc] * reps, axis=1)
    su = jnp.concatenate([s_up] * reps, axis=1)
    sd = jnp.concatenate([s_dn] * reps, axis=1)
    return (tn * c + up * su + dn * sd) * out_scale


def _inproj_kernel(x_ref, g_ref, wq_ref, wk_ref, wv_ref, wu_ref, qn_ref, kn_ref,
                   cc_ref, su_ref, sd_ref, e_ref, wpool_ref, pscale_ref,
                   q_out, k_out, v_out, y_out, ubuf, *, tiles_per_seq):
    i = pl.program_id(0)
    tm = x_ref.shape[0]
    x = x_ref[...]
    ms = jnp.mean(x * x, axis=-1, keepdims=True)
    hn = (x * lax.rsqrt(ms + EPS) * g_ref[...]).astype(BF16)

    cc, s_up, s_dn = cc_ref[...], su_ref[...], sd_ref[...]
    q = jnp.dot(hn, wq_ref[...], preferred_element_type=F32)
    q_out[...] = _norm_rope(q, qn_ref[...], e_ref, cc, s_up, s_dn,
                            HEAD_DIM ** -0.5).astype(BF16)
    k = jnp.dot(hn, wk_ref[...], preferred_element_type=F32)
    k_out[...] = _norm_rope(k, kn_ref[...], e_ref, cc, s_up, s_dn, 1.0).astype(BF16)
    v_out[...] = jnp.dot(hn, wv_ref[...], preferred_element_type=F32).astype(BF16)

    @pl.when(i % tiles_per_seq == 0)
    def _():
        ubuf[0:HALO, :] = jnp.zeros((HALO, POOL_WIDTH), F32)

    ubuf[HALO:HALO + tm, :] = jnp.dot(hn, wu_ref[...], preferred_element_type=F32)
    pos = (i % tiles_per_seq) * tm + lax.broadcasted_iota(
        jnp.int32, (tm, POOL_GROUP_DIM), 0)
    for g, w in enumerate(POOL_WINDOWS):
        cols = slice(g * POOL_GROUP_DIM, (g + 1) * POOL_GROUP_DIM)
        u_g = ubuf[HALO:HALO + tm, cols]
        acc = u_g
        for j in range(1, w):
            acc = acc + ubuf[HALO - j:HALO - j + tm, cols]
        cnt = jnp.minimum(pos + 1, w).astype(F32)
        d = (acc / cnt - u_g).astype(BF16)
        y = jnp.dot(d, wpool_ref[g], preferred_element_type=F32)
        y_out[:, cols] = (y * pscale_ref[:, cols]).astype(BF16)
    ubuf[0:HALO, :] = ubuf[tm:tm + HALO, :]


def _attn_kernel(sinks_ref, q_ref, kc_ref, kp_ref, vc_ref, vp_ref, o_ref):
    j = pl.program_id(1)
    kband = jnp.concatenate([kp_ref[...], kc_ref[...]], axis=0)
    vband = jnp.concatenate([vp_ref[...], vc_ref[...]], axis=0)
    row = lax.broadcasted_iota(jnp.int32, (BLOCK, 2 * BLOCK), 0)
    col = lax.broadcasted_iota(jnp.int32, (BLOCK, 2 * BLOCK), 1)
    diff = row + BLOCK - col
    first = jnp.where(j > 0, 0, BLOCK)
    valid = (diff >= 0) & (diff < WINDOW) & (col >= first)
    for g in range(N_KV_HEADS):
        kg = kband[:, g * HEAD_DIM:(g + 1) * HEAD_DIM]
        vg = vband[:, g * HEAD_DIM:(g + 1) * HEAD_DIM]
        for hh in range(GROUP):
            h = g * GROUP + hh
            qh = q_ref[:, h * HEAD_DIM:(h + 1) * HEAD_DIM]
            s = lax.dot_general(qh, kg, (((1,), (1,)), ((), ())),
                                preferred_element_type=F32)
            s = jnp.where(valid, s, NEG_BIG)
            sink = sinks_ref[h]
            m = jnp.maximum(jnp.max(s, axis=-1, keepdims=True), sink)
            p = jnp.exp(s - m)
            denom = jnp.sum(p, axis=-1, keepdims=True) + jnp.exp(sink - m)
            o = jnp.dot(p.astype(BF16), vg, preferred_element_type=F32)
            o_ref[:, h * HEAD_DIM:(h + 1) * HEAD_DIM] = (o / denom).astype(BF16)


def _outproj_router_kernel(o_ref, y_ref, x_ref, woa_ref, wop_ref, g_ref, wr_ref,
                           wrl_ref, br_ref, h_out, hpk_out, route_out):
    tm = x_ref.shape[0]
    h = (x_ref[...]
         + jnp.dot(o_ref[...], woa_ref[...], preferred_element_type=F32)
         + jnp.dot(y_ref[...], wop_ref[...], preferred_element_type=F32))
    h_out[...] = h
    ms = jnp.mean(h * h, axis=-1, keepdims=True)
    hn = h * lax.rsqrt(ms + EPS) * g_ref[...]
    hb = hn.astype(BF16)
    hb32 = hb.astype(F32)
    lo_bits = lax.shift_right_logical(
        pltpu.bitcast(hb32[:, :HALF], jnp.uint32), jnp.uint32(16))
    hi_bits = pltpu.bitcast(hb32[:, HALF:], jnp.uint32) & jnp.uint32(0xFFFF0000)
    hpk_out[...] = lo_bits | hi_bits

    hl = (hn - hb32).astype(BF16)
    both = jnp.dot(hb, wr_ref[...], preferred_element_type=F32)
    logits = (both[:, :LANES] + both[:, LANES:]
              + jnp.dot(hl, wr_ref[:, :LANES], preferred_element_type=F32)
              + br_ref[...])
    lane = lax.broadcasted_iota(jnp.int32, (tm, LANES), 1)
    lane_f = lane.astype(F32)

    def first_lane_of(mask):
        return jnp.min(jnp.where(mask, lane_f, float(LANES)), axis=-1, keepdims=True)

    coarse = jnp.where(lane < N_EXPERT_GROUPS, logits, NEG_BIG)
    cmax = jnp.max(coarse, axis=-1, keepdims=True)
    grp = first_lane_of(coarse == cmax)
    p_grp = 1.0 / jnp.sum(jnp.exp(coarse - cmax), axis=-1, keepdims=True)

    flo = N_EXPERT_GROUPS + EXPERTS_PER_GROUP * grp
    fmask = (lane_f >= flo) & (lane_f < flo + EXPERTS_PER_GROUP)
    fsel = jnp.where(fmask, logits, NEG_BIG)
    fmax = jnp.max(fsel, axis=-1, keepdims=True)
    fe = jnp.exp(fsel - fmax)
    p = fe / jnp.sum(fe, axis=-1, keepdims=True)
    pc = jnp.where(fmask, p, -1.0)
    t1 = jnp.max(pc, axis=-1, keepdims=True)
    i1 = first_lane_of(pc == t1)
    pc2 = jnp.where(lane_f == i1, -1.0, pc)
    t2 = jnp.max(pc2, axis=-1, keepdims=True)
    i2 = first_lane_of(pc2 == t2)
    tsum = t1 + t2
    g1 = p_grp * t1 / tsum
    g2 = p_grp * t2 / tsum
    route_out[...] = jnp.where(
        lane == 0, i1 - N_EXPERT_GROUPS,
        jnp.where(lane == 1, i2 - N_EXPERT_GROUPS,
                  jnp.where(lane == 2, g1, jnp.where(lane == 3, g2, 0.0))))


def _expert_kernel(eid_ref, first_ref, nact_ref, tok_ref,
                   hpk_hbm, wg_ref, wu_ref, wd_ref, ys_out,
                   xbuf, wg_s, wu_s, wd_s, sems):
    i = pl.program_id(0)
    tm = ys_out.shape[0]
    n_act = nact_ref[0]

    def gather(blk, slot):
        def body(r, carry):
            tok = tok_ref[blk * tm + r]
            pltpu.make_async_copy(hpk_hbm.at[pl.ds(tok, 1)],
                                  xbuf.at[slot, pl.ds(r, 1)],
                                  sems.at[slot]).start()
            return carry
        lax.fori_loop(0, tm, body, 0)

    @pl.when(i == 0)
    def _():
        gather(0, 0)

    @pl.when(i + 1 < n_act)
    def _():
        gather(i + 1, (i + 1) % 2)

    @pl.when(i < n_act)
    def _():
        slot = i % 2

        @pl.when(first_ref[i] == 1)
        def _():
            wg_s[...] = wg_ref[0].astype(BF16)
            wu_s[...] = wu_ref[0].astype(BF16)
            wd_s[...] = wd_ref[0].astype(BF16)

        pltpu.make_async_copy(hpk_hbm.at[pl.ds(0, tm)], xbuf.at[slot],
                              sems.at[slot]).wait()
        xu = xbuf[slot]
        x_lo = pltpu.bitcast(lax.shift_left(xu, jnp.uint32(16)), F32).astype(BF16)
        x_hi = pltpu.bitcast(xu & jnp.uint32(0xFFFF0000), F32).astype(BF16)
        a = (jnp.dot(x_lo, wg_s[0:HALF, :], preferred_element_type=F32)
             + jnp.dot(x_hi, wg_s[HALF:, :], preferred_element_type=F32))
        b = (jnp.dot(x_lo, wu_s[0:HALF, :], preferred_element_type=F32)
             + jnp.dot(x_hi, wu_s[HALF:, :], preferred_element_type=F32))
        mid = (a * jax.nn.sigmoid(a) * b).astype(BF16)
        ys_out[...] = jnp.dot(mid, wd_s[...], preferred_element_type=F32)

    @pl.when(i >= n_act)
    def _():
        ys_out[...] = jnp.zeros(ys_out.shape, F32)


def _combine_kernel(pos_ref, h_ref, route_ref, ys_hbm, out_ref, ybuf, sems):
    i = pl.program_id(0)
    n = pl.num_programs(0)
    tt = h_ref.shape[0]

    def gather(blk, slot):
        def body(r, carry):
            base = (blk * tt + r) * TOP_K
            for k in range(TOP_K):
                pltpu.make_async_copy(ys_hbm.at[pl.ds(pos_ref[base + k], 1)],
                                      ybuf.at[slot, k, pl.ds(r, 1)],
                                      sems.at[slot]).start()
            return carry
        lax.fori_loop(0, tt, body, 0)

    @pl.when(i == 0)
    def _():
        gather(0, 0)

    @pl.when(i + 1 < n)
    def _():
        gather(i + 1, (i + 1) % 2)

    slot = i % 2
    for k in range(TOP_K):
        pltpu.make_async_copy(ys_hbm.at[pl.ds(0, tt)], ybuf.at[slot, k],
                              sems.at[slot]).wait()
    g0 = route_ref[:, 2:3]
    g1 = route_ref[:, 3:4]
    out_ref[...] = h_ref[...] + ybuf[slot, 0] * g0 + ybuf[slot, 1] * g1


def _rope_tables(seq):
    pos = jnp.arange(seq, dtype=F32)
    inv_freq = ROPE_THETA ** (-jnp.arange(0, ROT_DIM, 2, dtype=F32) / ROT_DIM)
    ang = pos[:, None] * inv_freq[None, :]
    cos, sin = jnp.cos(ang), jnp.sin(ang)
    half = ROT_DIM // 2
    ones = jnp.ones((seq, HEAD_DIM - ROT_DIM), F32)
    zeros = jnp.zeros((seq, HEAD_DIM - ROT_DIM), F32)
    zh = jnp.zeros((seq, half), F32)
    cc = jnp.concatenate([cos, cos, ones], axis=1)
    s_up = jnp.concatenate([-sin, zh, zeros], axis=1)
    s_dn = jnp.concatenate([zh, sin, zeros], axis=1)
    rep = LANES // HEAD_DIM
    return tuple(jnp.tile(t, (1, rep)) for t in (cc, s_up, s_dn))


def _dispatch(eid, n_tokens, tile):
    a = n_tokens * TOP_K
    eid_flat = eid.reshape(a)
    onehot = (eid_flat[:, None] == jnp.arange(N_EXPERTS, dtype=jnp.int32)[None, :])
    csum = jnp.cumsum(onehot.astype(jnp.int32), axis=0)
    rank = jnp.take_along_axis(csum, eid_flat[:, None], axis=1)[:, 0] - 1
    counts = csum[-1]
    padded = ((counts + tile - 1) // tile) * tile
    pad_end = jnp.cumsum(padded)
    pad_start = pad_end - padded
    dest = (pad_start[eid_flat] + rank).astype(jnp.int32)
    n_rows = a + N_EXPERTS * tile
    n_blocks = n_rows // tile
    tok = jnp.arange(a, dtype=jnp.int32) // TOP_K
    row_tok = jnp.zeros((n_rows,), jnp.int32).at[dest].set(tok)
    blk_start = jnp.arange(n_blocks, dtype=jnp.int32) * tile
    blk_eid = jnp.clip(jnp.searchsorted(pad_end, blk_start, side="right"),
                       0, N_EXPERTS - 1).astype(jnp.int32)
    prev = jnp.concatenate([jnp.full((1,), -1, jnp.int32), blk_eid[:-1]])
    first = (blk_eid != prev).astype(jnp.int32)
    n_act = (pad_end[-1] // tile).astype(jnp.int32).reshape(1)
    return dest, row_tok, blk_eid, first, n_act, n_rows, n_blocks


def kernel(x, norm_mix, w_in, q_norm, k_norm, sinks, w_pool, pool_scale, w_out,
           norm_ffn, w_coarse, b_coarse, w_fine, b_fine, w_gate, w_up, w_down):
    bsz, seq, d = x.shape
    n = bsz * seq
    assert d == D_MODEL and seq % ROW_TILE == 0 and seq % BLOCK == 0
    assert norm_mix.shape[0] == 1, "single-layer problem"
    xf = x.reshape(n, d)

    w_in_b = w_in[0].astype(BF16)
    wq = w_in_b[:, :ATTN_WIDTH]
    wk = w_in_b[:, ATTN_WIDTH:ATTN_WIDTH + KV_WIDTH]
    wv = w_in_b[:, ATTN_WIDTH + KV_WIDTH:ATTN_WIDTH + 2 * KV_WIDTH]
    wu = w_in_b[:, ATTN_WIDTH + 2 * KV_WIDTH:]
    qn = jnp.tile(q_norm[0], N_HEADS).reshape(1, ATTN_WIDTH)
    kn = jnp.tile(k_norm[0], N_KV_HEADS).reshape(1, KV_WIDTH)
    cc, s_up, s_dn = _rope_tables(seq)
    lane_head = jnp.arange(MXU_DIM) // HEAD_DIM
    e_mat = (lane_head[:, None] == lane_head[None, :]).astype(BF16)
    w_pool_b = w_pool[0].astype(BF16)
    pscale = pool_scale[0].reshape(1, POOL_WIDTH)
    w_out_b = w_out[0].astype(BF16)
    wo_attn, wo_pool = w_out_b[:ATTN_WIDTH], w_out_b[ATTN_WIDTH:]
    n_router = N_EXPERT_GROUPS + N_EXPERTS
    w_r = jnp.concatenate([w_coarse[0], w_fine[0]], axis=1)
    w_r = jnp.pad(w_r, ((0, 0), (0, LANES - n_router)))
    w_r_hi = w_r.astype(BF16)
    w_r_lo = (w_r - w_r_hi.astype(F32)).astype(BF16)
    w_r2 = jnp.concatenate([w_r_hi, w_r_lo], axis=1)
    b_r = jnp.pad(jnp.concatenate([b_coarse[0], b_fine[0]]),
                  (0, LANES - n_router)).reshape(1, LANES)

    tm = ROW_TILE
    tiles_per_seq = seq // tm
    const = lambda *_: (0, 0)
    row_blk = lambda i: (i, 0)

    q, k, v, y = pl.pallas_call(
        functools.partial(_inproj_kernel, tiles_per_seq=tiles_per_seq),
        grid=(n // tm,),
        in_specs=[
            pl.BlockSpec((tm, d), row_blk),
            pl.BlockSpec((1, d), const),
            pl.BlockSpec((d, ATTN_WIDTH), const),
            pl.BlockSpec((d, KV_WIDTH), const),
            pl.BlockSpec((d, KV_WIDTH), const),
            pl.BlockSpec((d, POOL_WIDTH), const),
            pl.BlockSpec((1, ATTN_WIDTH), const),
            pl.BlockSpec((1, KV_WIDTH), const),
            pl.BlockSpec((tm, LANES), lambda i: (i % tiles_per_seq, 0)),
            pl.BlockSpec((tm, LANES), lambda i: (i % tiles_per_seq, 0)),
            pl.BlockSpec((tm, LANES), lambda i: (i % tiles_per_seq, 0)),
            pl.BlockSpec((MXU_DIM, MXU_DIM), const),
            pl.BlockSpec((len(POOL_WINDOWS), POOL_GROUP_DIM, POOL_GROUP_DIM),
                         lambda i: (0, 0, 0)),
            pl.BlockSpec((1, POOL_WIDTH), const),
        ],
        out_specs=[
            pl.BlockSpec((tm, ATTN_WIDTH), row_blk),
            pl.BlockSpec((tm, KV_WIDTH), row_blk),
            pl.BlockSpec((tm, KV_WIDTH), row_blk),
            pl.BlockSpec((tm, POOL_WIDTH), row_blk),
        ],
        out_shape=[
            jax.ShapeDtypeStruct((n, ATTN_WIDTH), BF16),
            jax.ShapeDtypeStruct((n, KV_WIDTH), BF16),
            jax.ShapeDtypeStruct((n, KV_WIDTH), BF16),
            jax.ShapeDtypeStruct((n, POOL_WIDTH), BF16),
        ],
        scratch_shapes=[pltpu.VMEM((HALO + tm, POOL_WIDTH), F32)],
        compiler_params=_cparams(),
        name="inproj",
    )(xf, norm_mix[0].reshape(1, d), wq, wk, wv, wu, qn, kn, cc, s_up, s_dn,
      e_mat, w_pool_b, pscale)

    nb = seq // BLOCK
    cur = lambda b, j, *_: (b * nb + j, 0)
    prv = lambda b, j, *_: (b * nb + jnp.maximum(j - 1, 0), 0)
    attn = pl.pallas_call(
        _attn_kernel,
        grid_spec=pltpu.PrefetchScalarGridSpec(
            num_scalar_prefetch=1,
            grid=(bsz, nb),
            in_specs=[
                pl.BlockSpec((BLOCK, ATTN_WIDTH), cur),
                pl.BlockSpec((BLOCK, KV_WIDTH), cur),
                pl.BlockSpec((BLOCK, KV_WIDTH), prv),
                pl.BlockSpec((BLOCK, KV_WIDTH), cur),
                pl.BlockSpec((BLOCK, KV_WIDTH), prv),
            ],
            out_specs=pl.BlockSpec((BLOCK, ATTN_WIDTH), cur),
        ),
        out_shape=jax.ShapeDtypeStruct((n, ATTN_WIDTH), BF16),
        compiler_params=_cparams(2),
        name="swa_attn",
    )(sinks[0].astype(F32), q, k, k, v, v)

    h, hpk, route = pl.pallas_call(
        _outproj_router_kernel,
        grid=(n // tm,),
        in_specs=[
            pl.BlockSpec((tm, ATTN_WIDTH), row_blk),
            pl.BlockSpec((tm, POOL_WIDTH), row_blk),
            pl.BlockSpec((tm, d), row_blk),
            pl.BlockSpec((ATTN_WIDTH, d), const),
            pl.BlockSpec((POOL_WIDTH, d), const),
            pl.BlockSpec((1, d), const),
            pl.BlockSpec((d, 2 * LANES), const),
            pl.BlockSpec((d, LANES), const),
            pl.BlockSpec((1, LANES), const),
        ],
        out_specs=[
            pl.BlockSpec((tm, d), row_blk),
            pl.BlockSpec((tm, HALF), row_blk),
            pl.BlockSpec((tm, LANES), row_blk),
        ],
        out_shape=[
            jax.ShapeDtypeStruct((n, d), F32),
            jax.ShapeDtypeStruct((n, HALF), jnp.uint32),
            jax.ShapeDtypeStruct((n, LANES), F32),
        ],
        compiler_params=_cparams(),
        name="outproj_router",
    )(attn, y, xf, wo_attn, wo_pool, norm_ffn[0].reshape(1, d), w_r2, w_r_lo, b_r)

    eid = route[:, :TOP_K].astype(jnp.int32)
    te = EXPERT_TILE
    dest, row_tok, blk_eid, first, n_act, n_rows, n_blocks = _dispatch(eid, n, te)

    w_idx = lambda i, eid_r, *_: (eid_r[i], 0, 0)
    ys = pl.pallas_call(
        _expert_kernel,
        grid_spec=pltpu.PrefetchScalarGridSpec(
            num_scalar_prefetch=4,
            grid=(n_blocks,),
            in_specs=[
                pl.BlockSpec(memory_space=pl.ANY),
                pl.BlockSpec((1, d, D_EXPERT), w_idx),
                pl.BlockSpec((1, d, D_EXPERT), w_idx),
                pl.BlockSpec((1, D_EXPERT, d), w_idx),
            ],
            out_specs=pl.BlockSpec((te, d), lambda i, *_: (i, 0)),
            scratch_shapes=[
                pltpu.VMEM((2, te, HALF), jnp.uint32),
                pltpu.VMEM((d, D_EXPERT), BF16),
                pltpu.VMEM((d, D_EXPERT), BF16),
                pltpu.VMEM((D_EXPERT, d), BF16),
                pltpu.SemaphoreType.DMA((2,)),
            ],
        ),
        out_shape=jax.ShapeDtypeStruct((n_rows, d), F32),
        compiler_params=_cparams(),
        name="experts",
    )(blk_eid, first, n_act, row_tok, hpk, w_gate[0], w_up[0], w_down[0])

    tt = COMBINE_TILE
    out = pl.pallas_call(
        _combine_kernel,
        grid_spec=pltpu.PrefetchScalarGridSpec(
            num_scalar_prefetch=1,
            grid=(n // tt,),
            in_specs=[
                pl.BlockSpec((tt, d), lambda i, *_: (i, 0)),
                pl.BlockSpec((tt, LANES), lambda i, *_: (i, 0)),
                pl.BlockSpec(memory_space=pl.ANY),
            ],
            out_specs=pl.BlockSpec((tt, d), lambda i, *_: (i, 0)),
            scratch_shapes=[
                pltpu.VMEM((2, TOP_K, tt, d), F32),
                pltpu.SemaphoreType.DMA((2,)),
            ],
        ),
        out_shape=jax.ShapeDtypeStruct((n, d), F32),
        compiler_params=_cparams(),
        name="combine",
    )(dest, h, route, ys)
    return out.reshape(bsz, seq, d)
```

```python
import functools

import jax
import jax.numpy as jnp
from jax import lax
from jax.experimental import pallas as pl
from jax.experimental.pallas import tpu as pltpu

F32 = jnp.float32
BF16 = jnp.bfloat16

D_MODEL = 2048
N_HEADS = 16
N_KV_HEADS = 4
HEAD_DIM = 64
GROUP = N_HEADS // N_KV_HEADS
ROT_DIM = HEAD_DIM // 4
ROPE_THETA = 500000.0
WINDOW = 128
BLOCK = 128
ATTN_WIDTH = N_HEADS * HEAD_DIM
KV_WIDTH = N_KV_HEADS * HEAD_DIM
POOL_WINDOWS = (2, 4, 8, 16)
POOL_WIDTH = D_MODEL // 2
POOL_GROUP_DIM = POOL_WIDTH // len(POOL_WINDOWS)
N_EXPERT_GROUPS = 4
EXPERTS_PER_GROUP = 8
N_EXPERTS = N_EXPERT_GROUPS * EXPERTS_PER_GROUP
TOP_K = 2
D_EXPERT = 512
EPS = 1e-6

LANES = 128
SUBLANES = 8
MXU_DIM = 256
HALO = 16
NEG_BIG = -1e30
HALF = D_MODEL // 2
PK_ROWS = HALF // LANES
Y_ROWS = D_MODEL // LANES

ROW_TILE = 256
EXPERT_TILE = 256
COMBINE_TILE = 256
VMEM_LIMIT = 56 * 1024 * 1024


def _cparams(n_axes=1):
    return pltpu.CompilerParams(
        dimension_semantics=("arbitrary",) * n_axes,
        vmem_limit_bytes=VMEM_LIMIT,
    )


def _head_sumsq(t, e_ref):
    t2 = t * t
    hi = t2.astype(BF16)
    lo = (t2 - hi.astype(F32)).astype(BF16)
    e = e_ref[...]
    outs = []
    for c in range(t.shape[1] // MXU_DIM):
        sl = slice(c * MXU_DIM, (c + 1) * MXU_DIM)
        outs.append(jnp.dot(hi[:, sl], e, preferred_element_type=F32)
                    + jnp.dot(lo[:, sl], e, preferred_element_type=F32))
    return outs[0] if len(outs) == 1 else jnp.concatenate(outs, axis=1)


def _norm_rope(t, gain, e_ref, cc, s_up, s_dn, out_scale):
    w = t.shape[1]
    ss = _head_sumsq(t, e_ref)
    tn = t * lax.rsqrt(ss * (1.0 / HEAD_DIM) + EPS) * gain
    reps = w // LANES
    up = pltpu.roll(tn, w - ROT_DIM // 2, axis=1)
    dn = pltpu.roll(tn, ROT_DIM // 2, axis=1)
    c = jnp.concatenate([cc] * reps, axis=1)
    su = jnp.concatenate([s_up] * reps, axis=1)
    sd = jnp.concatenate([s_dn] * reps, axis=1)
    return (tn * c + up * su + dn * sd) * out_scale


def _inproj_kernel(x_ref, g_ref, wq_ref, wk_ref, wv_ref, wu_ref, qn_ref, kn_ref,
                   cc_ref, su_ref, sd_ref, e_ref, wpool_ref, pscale_ref,
                   q_out, k_out, v_out, y_out, ubuf, *, tiles_per_seq):
    i = pl.program_id(0)
    tm = x_ref.shape[0]
    x = x_ref[...]
    ms = jnp.mean(x * x, axis=-1, keepdims=True)
    hn = (x * lax.rsqrt(ms + EPS) * g_ref[...]).astype(BF16)

    cc, s_up, s_dn = cc_ref[...], su_ref[...], sd_ref[...]
    q = jnp.dot(hn, wq_ref[...], preferred_element_type=F32)
    q_out[...] = _norm_rope(q, qn_ref[...], e_ref, cc, s_up, s_dn,
                            HEAD_DIM ** -0.5).astype(BF16)
    k = jnp.dot(hn, wk_ref[...], preferred_element_type=F32)
    k_out[...] = _norm_rope(k, kn_ref[...], e_ref, cc, s_up, s_dn, 1.0).astype(BF16)
    v_out[...] = jnp.dot(hn, wv_ref[...], preferred_element_type=F32).astype(BF16)

    @pl.when(i % tiles_per_seq == 0)
    def _():
        ubuf[0:HALO, :] = jnp.zeros((HALO, POOL_WIDTH), F32)

    ubuf[HALO:HALO + tm, :] = jnp.dot(hn, wu_ref[...], preferred_element_type=F32)
    pos = (i % tiles_per_seq) * tm + lax.broadcasted_iota(
        jnp.int32, (tm, POOL_GROUP_DIM), 0)
    for g, w in enumerate(POOL_WINDOWS):
        cols = slice(g * POOL_GROUP_DIM, (g + 1) * POOL_GROUP_DIM)
        u_g = ubuf[HALO:HALO + tm, cols]
        acc = u_g
        for j in range(1, w):
            acc = acc + ubuf[HALO - j:HALO - j + tm, cols]
        cnt = jnp.minimum(pos + 1, w).astype(F32)
        d = (acc / cnt - u_g).astype(BF16)
        y = jnp.dot(d, wpool_ref[g], preferred_element_type=F32)
        y_out[:, cols] = (y * pscale_ref[:, cols]).astype(BF16)
    ubuf[0:HALO, :] = ubuf[tm:tm + HALO, :]


def _attn_kernel(sinks_ref, q_ref, kc_ref, kp_ref, vc_ref, vp_ref, o_ref):
    j = pl.program_id(1)
    kband = jnp.concatenate([kp_ref[...], kc_ref[...]], axis=0)
    vband = jnp.concatenate([vp_ref[...], vc_ref[...]], axis=0)
    row = lax.broadcasted_iota(jnp.int32, (BLOCK, 2 * BLOCK), 0)
    col = lax.broadcasted_iota(jnp.int32, (BLOCK, 2 * BLOCK), 1)
    diff = row + BLOCK - col
    first = jnp.where(j > 0, 0, BLOCK)
    valid = (diff >= 0) & (diff < WINDOW) & (col >= first)
    for g in range(N_KV_HEADS):
        kg = kband[:, g * HEAD_DIM:(g + 1) * HEAD_DIM]
        vg = vband[:, g * HEAD_DIM:(g + 1) * HEAD_DIM]
        for hh in range(GROUP):
            h = g * GROUP + hh
            qh = q_ref[:, h * HEAD_DIM:(h + 1) * HEAD_DIM]
            s = lax.dot_general(qh, kg, (((1,), (1,)), ((), ())),
                                preferred_element_type=F32)
            s = jnp.where(valid, s, NEG_BIG)
            sink = sinks_ref[h]
            m = jnp.maximum(jnp.max(s, axis=-1, keepdims=True), sink)
            p = jnp.exp(s - m)
            denom = jnp.sum(p, axis=-1, keepdims=True) + jnp.exp(sink - m)
            o = jnp.dot(p.astype(BF16), vg, preferred_element_type=F32)
            o_ref[:, h * HEAD_DIM:(h + 1) * HEAD_DIM] = (o / denom).astype(BF16)


def _outproj_router_kernel(o_ref, y_ref, x_ref, woa_ref, wop_ref, g_ref, wr_ref,
                           br_ref, tri_ref, h_out, hpk_out, route_out, cnt_out):
    tm = x_ref.shape[0]
    h = (x_ref[...]
         + jnp.dot(o_ref[...], woa_ref[...], preferred_element_type=F32)
         + jnp.dot(y_ref[...], wop_ref[...], preferred_element_type=F32))
    h_out[...] = h
    ms = jnp.mean(h * h, axis=-1, keepdims=True)
    hn = h * lax.rsqrt(ms + EPS) * g_ref[...]
    hb = hn.astype(BF16)
    hb32 = hb.astype(F32)
    lo_bits = lax.shift_right_logical(
        pltpu.bitcast(hb32[:, :HALF], jnp.uint32), jnp.uint32(16))
    hi_bits = pltpu.bitcast(hb32[:, HALF:], jnp.uint32) & jnp.uint32(0xFFFF0000)
    packed = lo_bits | hi_bits
    for s in range(PK_ROWS):
        hpk_out[pl.ds(s, tm, stride=PK_ROWS), :] = packed[:, s * LANES:(s + 1) * LANES]

    hl = (hn - hb32).astype(BF16)
    both = jnp.dot(hb, wr_ref[...], preferred_element_type=F32)
    logits = (both[:, :LANES] + both[:, LANES:]
              + jnp.dot(hl, wr_ref[:, :LANES], preferred_element_type=F32)
              + br_ref[...])
    lane = lax.broadcasted_iota(jnp.int32, (tm, LANES), 1)
    lane_f = lane.astype(F32)

    def first_lane_of(mask):
        return jnp.min(jnp.where(mask, lane_f, float(LANES)), axis=-1, keepdims=True)

    coarse = jnp.where(lane < N_EXPERT_GROUPS, logits, NEG_BIG)
    cmax = jnp.max(coarse, axis=-1, keepdims=True)
    grp = first_lane_of(coarse == cmax)
    p_grp = 1.0 / jnp.sum(jnp.exp(coarse - cmax), axis=-1, keepdims=True)

    flo = N_EXPERT_GROUPS + EXPERTS_PER_GROUP * grp
    fmask = (lane_f >= flo) & (lane_f < flo + EXPERTS_PER_GROUP)
    fsel = jnp.where(fmask, logits, NEG_BIG)
    fmax = jnp.max(fsel, axis=-1, keepdims=True)
    fe = jnp.exp(fsel - fmax)
    p = fe / jnp.sum(fe, axis=-1, keepdims=True)
    pc = jnp.where(fmask, p, -1.0)
    t1 = jnp.max(pc, axis=-1, keepdims=True)
    i1 = first_lane_of(pc == t1)
    pc2 = jnp.where(lane_f == i1, -1.0, pc)
    t2 = jnp.max(pc2, axis=-1, keepdims=True)
    i2 = first_lane_of(pc2 == t2)
    tsum = t1 + t2
    g1 = p_grp * t1 / tsum
    g2 = p_grp * t2 / tsum
    e1 = i1 - N_EXPERT_GROUPS
    e2 = i2 - N_EXPERT_GROUPS

    oh1 = lane_f == e1
    oh2 = lane_f == e2
    oh = jnp.where(oh1, 1.0, 0.0) + jnp.where(oh2, 1.0, 0.0)
    prefix = jnp.dot(tri_ref[...], oh.astype(BF16), preferred_element_type=F32)
    r1 = jnp.sum(jnp.where(oh1, prefix, 0.0), axis=-1, keepdims=True)
    r2 = jnp.sum(jnp.where(oh2, prefix, 0.0), axis=-1, keepdims=True)
    route_out[...] = jnp.where(
        lane == 0, e1,
        jnp.where(lane == 1, e2,
                  jnp.where(lane == 2, g1,
                            jnp.where(lane == 3, g2,
                                      jnp.where(lane == 4, r1,
                                                jnp.where(lane == 5, r2, 0.0))))))
    cnt_out[...] = jnp.broadcast_to(jnp.sum(oh, axis=0, keepdims=True), cnt_out.shape)


def _expert_kernel(eid_ref, first_ref, nact_ref, off_ref,
                   hpk_hbm, wg_ref, wu_ref, wd_ref, ys_out,
                   xbuf, wg_s, wu_s, wd_s, sems):
    i = pl.program_id(0)
    tm = ys_out.shape[0] // Y_ROWS
    n_act = nact_ref[0]

    def row_copy(blk, slot, r, priority):
        src = pl.multiple_of(off_ref[blk * tm + r], PK_ROWS)
        dst = pl.multiple_of(r * PK_ROWS, PK_ROWS)
        pltpu.make_async_copy(hpk_hbm.at[pl.ds(src, PK_ROWS)],
                              xbuf.at[slot, pl.ds(dst, PK_ROWS)],
                              sems.at[slot]).start(priority=priority)

    def gather(blk, slot):
        def body(r2, carry):
            row_copy(blk, slot, 2 * r2, 0)
            row_copy(blk, slot, 2 * r2 + 1, 1)
            return carry
        lax.fori_loop(0, tm // 2, body, 0)

    @pl.when(i == 0)
    def _():
        gather(0, 0)

    @pl.when(i + 1 < n_act)
    def _():
        gather(i + 1, (i + 1) % 2)

    @pl.when(i < n_act)
    def _():
        slot = i % 2

        @pl.when(first_ref[i] == 1)
        def _():
            wg_s[...] = wg_ref[0].astype(BF16)
            wu_s[...] = wu_ref[0].astype(BF16)
            wd_s[...] = wd_ref[0].astype(BF16)

        pltpu.make_async_copy(hpk_hbm.at[pl.ds(0, tm * PK_ROWS)], xbuf.at[slot],
                              sems.at[slot]).wait()
        xu = jnp.concatenate(
            [xbuf[slot, pl.ds(s, tm, stride=PK_ROWS), :] for s in range(PK_ROWS)],
            axis=1)
        x_lo = pltpu.bitcast(lax.shift_left(xu, jnp.uint32(16)), F32).astype(BF16)
        x_hi = pltpu.bitcast(xu & jnp.uint32(0xFFFF0000), F32).astype(BF16)
        a = (jnp.dot(x_lo, wg_s[0:HALF, :], preferred_element_type=F32)
             + jnp.dot(x_hi, wg_s[HALF:, :], preferred_element_type=F32))
        b = (jnp.dot(x_lo, wu_s[0:HALF, :], preferred_element_type=F32)
             + jnp.dot(x_hi, wu_s[HALF:, :], preferred_element_type=F32))
        mid = (a * jax.nn.sigmoid(a) * b).astype(BF16)
        y = jnp.dot(mid, wd_s[...], preferred_element_type=F32)
        for s in range(Y_ROWS):
            ys_out[pl.ds(s, tm, stride=Y_ROWS), :] = y[:, s * LANES:(s + 1) * LANES]

    @pl.when(i >= n_act)
    def _():
        ys_out[...] = jnp.zeros(ys_out.shape, F32)


def _combine_kernel(pos_ref, h_ref, route_ref, ys_hbm, out_ref, ybuf, sems):
    i = pl.program_id(0)
    n = pl.num_programs(0)
    tt = h_ref.shape[0]

    def gather(blk, slot):
        def body(r, carry):
            base = (blk * tt + r) * TOP_K
            dst = pl.multiple_of(r * Y_ROWS, Y_ROWS)
            for k in range(TOP_K):
                src = pl.multiple_of(pos_ref[base + k], Y_ROWS)
                pltpu.make_async_copy(ys_hbm.at[pl.ds(src, Y_ROWS)],
                                      ybuf.at[slot, k, pl.ds(dst, Y_ROWS)],
                                      sems.at[slot]).start(priority=k)
            return carry
        lax.fori_loop(0, tt, body, 0)

    @pl.when(i == 0)
    def _():
        gather(0, 0)

    @pl.when(i + 1 < n)
    def _():
        gather(i + 1, (i + 1) % 2)

    slot = i % 2
    for k in range(TOP_K):
        pltpu.make_async_copy(ys_hbm.at[pl.ds(0, tt * Y_ROWS)], ybuf.at[slot, k],
                              sems.at[slot]).wait()

    def rows_of(k):
        return jnp.concatenate(
            [ybuf[slot, k, pl.ds(s, tt, stride=Y_ROWS), :] for s in range(Y_ROWS)],
            axis=1)

    g0 = route_ref[:, 2:3]
    g1 = route_ref[:, 3:4]
    out_ref[...] = h_ref[...] + rows_of(0) * g0 + rows_of(1) * g1


def _rope_tables(seq):
    pos = jnp.arange(seq, dtype=F32)
    inv_freq = ROPE_THETA ** (-jnp.arange(0, ROT_DIM, 2, dtype=F32) / ROT_DIM)
    ang = pos[:, None] * inv_freq[None, :]
    cos, sin = jnp.cos(ang), jnp.sin(ang)
    half = ROT_DIM // 2
    ones = jnp.ones((seq, HEAD_DIM - ROT_DIM), F32)
    zeros = jnp.zeros((seq, HEAD_DIM - ROT_DIM), F32)
    zh = jnp.zeros((seq, half), F32)
    cc = jnp.concatenate([cos, cos, ones], axis=1)
    s_up = jnp.concatenate([-sin, zh, zeros], axis=1)
    s_dn = jnp.concatenate([zh, sin, zeros], axis=1)
    rep = LANES // HEAD_DIM
    return tuple(jnp.tile(t, (1, rep)) for t in (cc, s_up, s_dn))


def _dispatch(eid, rank, tile_cnt, n_tokens, tok_tile, tile):
    a = n_tokens * TOP_K
    n_tiles = n_tokens // tok_tile
    tile_off = jnp.cumsum(tile_cnt, axis=0) - tile_cnt
    counts = jnp.sum(tile_cnt, axis=0)
    padded = ((counts + tile - 1) // tile) * tile
    pad_end = jnp.cumsum(padded)
    pad_start = pad_end - padded
    start = jnp.cumsum(counts) - counts
    base = pad_start[None, :] + tile_off
    base_tok = jnp.take_along_axis(base, eid.reshape(n_tiles, tok_tile * TOP_K), axis=1)
    dest = (base_tok.reshape(a) + rank.reshape(a)).astype(jnp.int32)

    n_rows = a + N_EXPERTS * tile
    n_blocks = n_rows // tile
    blk_start = jnp.arange(n_blocks, dtype=jnp.int32) * tile
    blk_eid = jnp.clip(jnp.searchsorted(pad_end, blk_start, side="right"),
                       0, N_EXPERTS - 1).astype(jnp.int32)
    prev = jnp.concatenate([jnp.full((1,), -1, jnp.int32), blk_eid[:-1]])
    first = (blk_eid != prev).astype(jnp.int32)
    n_act = (pad_end[-1] // tile).astype(jnp.int32).reshape(1)

    order = jnp.argsort(dest).astype(jnp.int32)
    row_e = jnp.repeat(blk_eid, tile)
    u = jnp.arange(n_rows, dtype=jnp.int32) + (start - pad_start)[row_e]
    row_tok = order[jnp.clip(u, 0, a - 1)] // TOP_K
    return dest, row_tok.astype(jnp.int32), blk_eid, first, n_act, n_rows, n_blocks


def kernel(x, norm_mix, w_in, q_norm, k_norm, sinks, w_pool, pool_scale, w_out,
           norm_ffn, w_coarse, b_coarse, w_fine, b_fine, w_gate, w_up, w_down):
    bsz, seq, d = x.shape
    n = bsz * seq
    assert d == D_MODEL and seq % ROW_TILE == 0 and seq % BLOCK == 0
    assert norm_mix.shape[0] == 1, "single-layer problem"
    xf = x.reshape(n, d)

    w_in_b = w_in[0].astype(BF16)
    wq = w_in_b[:, :ATTN_WIDTH]
    wk = w_in_b[:, ATTN_WIDTH:ATTN_WIDTH + KV_WIDTH]
    wv = w_in_b[:, ATTN_WIDTH + KV_WIDTH:ATTN_WIDTH + 2 * KV_WIDTH]
    wu = w_in_b[:, ATTN_WIDTH + 2 * KV_WIDTH:]
    qn = jnp.tile(q_norm[0], N_HEADS).reshape(1, ATTN_WIDTH)
    kn = jnp.tile(k_norm[0], N_KV_HEADS).reshape(1, KV_WIDTH)
    cc, s_up, s_dn = _rope_tables(seq)
    lane_head = jnp.arange(MXU_DIM) // HEAD_DIM
    e_mat = (lane_head[:, None] == lane_head[None, :]).astype(BF16)
    w_pool_b = w_pool[0].astype(BF16)
    pscale = pool_scale[0].reshape(1, POOL_WIDTH)
    w_out_b = w_out[0].astype(BF16)
    wo_attn, wo_pool = w_out_b[:ATTN_WIDTH], w_out_b[ATTN_WIDTH:]
    n_router = N_EXPERT_GROUPS + N_EXPERTS
    w_r = jnp.concatenate([w_coarse[0], w_fine[0]], axis=1)
    w_r = jnp.pad(w_r, ((0, 0), (0, LANES - n_router)))
    w_r_hi = w_r.astype(BF16)
    w_r_lo = (w_r - w_r_hi.astype(F32)).astype(BF16)
    w_r2 = jnp.concatenate([w_r_hi, w_r_lo], axis=1)
    b_r = jnp.pad(jnp.concatenate([b_coarse[0], b_fine[0]]),
                  (0, LANES - n_router)).reshape(1, LANES)

    tm = ROW_TILE
    tiles_per_seq = seq // tm
    n_tiles = n // tm
    idx = jnp.arange(tm)
    tri = (idx[None, :] < idx[:, None]).astype(BF16)
    const = lambda *_: (0, 0)
    row_blk = lambda i: (i, 0)

    q, k, v, y = pl.pallas_call(
        functools.partial(_inproj_kernel, tiles_per_seq=tiles_per_seq),
        grid=(n_tiles,),
        in_specs=[
            pl.BlockSpec((tm, d), row_blk),
            pl.BlockSpec((1, d), const),
            pl.BlockSpec((d, ATTN_WIDTH), const),
            pl.BlockSpec((d, KV_WIDTH), const),
            pl.BlockSpec((d, KV_WIDTH), const),
            pl.BlockSpec((d, POOL_WIDTH), const),
            pl.BlockSpec((1, ATTN_WIDTH), const),
            pl.BlockSpec((1, KV_WIDTH), const),
            pl.BlockSpec((tm, LANES), lambda i: (i % tiles_per_seq, 0)),
            pl.BlockSpec((tm, LANES), lambda i: (i % tiles_per_seq, 0)),
            pl.BlockSpec((tm, LANES), lambda i: (i % tiles_per_seq, 0)),
            pl.BlockSpec((MXU_DIM, MXU_DIM), const),
            pl.BlockSpec((len(POOL_WINDOWS), POOL_GROUP_DIM, POOL_GROUP_DIM),
                         lambda i: (0, 0, 0)),
            pl.BlockSpec((1, POOL_WIDTH), const),
        ],
        out_specs=[
            pl.BlockSpec((tm, ATTN_WIDTH), row_blk),
            pl.BlockSpec((tm, KV_WIDTH), row_blk),
            pl.BlockSpec((tm, KV_WIDTH), row_blk),
            pl.BlockSpec((tm, POOL_WIDTH), row_blk),
        ],
        out_shape=[
            jax.ShapeDtypeStruct((n, ATTN_WIDTH), BF16),
            jax.ShapeDtypeStruct((n, KV_WIDTH), BF16),
            jax.ShapeDtypeStruct((n, KV_WIDTH), BF16),
            jax.ShapeDtypeStruct((n, POOL_WIDTH), BF16),
        ],
        scratch_shapes=[pltpu.VMEM((HALO + tm, POOL_WIDTH), F32)],
        compiler_params=_cparams(),
        name="inproj",
    )(xf, norm_mix[0].reshape(1, d), wq, wk, wv, wu, qn, kn, cc, s_up, s_dn,
      e_mat, w_pool_b, pscale)

    nb = seq // BLOCK
    cur = lambda b, j, *_: (b * nb + j, 0)
    prv = lambda b, j, *_: (b * nb + jnp.maximum(j - 1, 0), 0)
    attn = pl.pallas_call(
        _attn_kernel,
        grid_spec=pltpu.PrefetchScalarGridSpec(
            num_scalar_prefetch=1,
            grid=(bsz, nb),
            in_specs=[
                pl.BlockSpec((BLOCK, ATTN_WIDTH), cur),
                pl.BlockSpec((BLOCK, KV_WIDTH), cur),
                pl.BlockSpec((BLOCK, KV_WIDTH), prv),
                pl.BlockSpec((BLOCK, KV_WIDTH), cur),
                pl.BlockSpec((BLOCK, KV_WIDTH), prv),
            ],
            out_specs=pl.BlockSpec((BLOCK, ATTN_WIDTH), cur),
        ),
        out_shape=jax.ShapeDtypeStruct((n, ATTN_WIDTH), BF16),
        compiler_params=_cparams(2),
        name="swa_attn",
    )(sinks[0].astype(F32), q, k, k, v, v)

    h, hpk, route, cnt = pl.pallas_call(
        _outproj_router_kernel,
        grid=(n_tiles,),
        in_specs=[
            pl.BlockSpec((tm, ATTN_WIDTH), row_blk),
            pl.BlockSpec((tm, POOL_WIDTH), row_blk),
            pl.BlockSpec((tm, d), row_blk),
            pl.BlockSpec((ATTN_WIDTH, d), const),
            pl.BlockSpec((POOL_WIDTH, d), const),
            pl.BlockSpec((1, d), const),
            pl.BlockSpec((d, 2 * LANES), const),
            pl.BlockSpec((1, LANES), const),
            pl.BlockSpec((tm, tm), const),
        ],
        out_specs=[
            pl.BlockSpec((tm, d), row_blk),
            pl.BlockSpec((tm * PK_ROWS, LANES), row_blk),
            pl.BlockSpec((tm, LANES), row_blk),
            pl.BlockSpec((SUBLANES, LANES), row_blk),
        ],
        out_shape=[
            jax.ShapeDtypeStruct((n, d), F32),
            jax.ShapeDtypeStruct((n * PK_ROWS, LANES), jnp.uint32),
            jax.ShapeDtypeStruct((n, LANES), F32),
            jax.ShapeDtypeStruct((n_tiles * SUBLANES, LANES), F32),
        ],
        compiler_params=_cparams(),
        name="outproj_router",
    )(attn, y, xf, wo_attn, wo_pool, norm_ffn[0].reshape(1, d), w_r2, b_r, tri)

    eid = route[:, :TOP_K].astype(jnp.int32)
    rank = route[:, 4:4 + TOP_K].astype(jnp.int32)
    tile_cnt = cnt.reshape(n_tiles, SUBLANES, LANES)[:, 0, :N_EXPERTS].astype(jnp.int32)
    te = EXPERT_TILE
    dest, row_tok, blk_eid, first, n_act, n_rows, n_blocks = _dispatch(
        eid, rank, tile_cnt, n, tm, te)

    w_idx = lambda i, eid_r, *_: (eid_r[i], 0, 0)
    ys = pl.pallas_call(
        _expert_kernel,
        grid_spec=pltpu.PrefetchScalarGridSpec(
            num_scalar_prefetch=4,
            grid=(n_blocks,),
            in_specs=[
                pl.BlockSpec(memory_space=pl.ANY),
                pl.BlockSpec((1, d, D_EXPERT), w_idx),
                pl.BlockSpec((1, d, D_EXPERT), w_idx),
                pl.BlockSpec((1, D_EXPERT, d), w_idx),
            ],
            out_specs=pl.BlockSpec((te * Y_ROWS, LANES), lambda i, *_: (i, 0)),
            scratch_shapes=[
                pltpu.VMEM((2, te * PK_ROWS, LANES), jnp.uint32),
                pltpu.VMEM((d, D_EXPERT), BF16),
                pltpu.VMEM((d, D_EXPERT), BF16),
                pltpu.VMEM((D_EXPERT, d), BF16),
                pltpu.SemaphoreType.DMA((2,)),
            ],
        ),
        out_shape=jax.ShapeDtypeStruct((n_rows * Y_ROWS, LANES), F32),
        compiler_params=_cparams(),
        name="experts",
    )(blk_eid, first, n_act, row_tok * PK_ROWS, hpk, w_gate[0], w_up[0], w_down[0])

    tt = COMBINE_TILE
    out = pl.pallas_call(
        _combine_kernel,
        grid_spec=pltpu.PrefetchScalarGridSpec(
            num_scalar_prefetch=1,
            grid=(n // tt,),
            in_specs=[
                pl.BlockSpec((tt, d), lambda i, *_: (i, 0)),
                pl.BlockSpec((tt, LANES), lambda i, *_: (i, 0)),
                pl.BlockSpec(memory_space=pl.ANY),
            ],
            out_specs=pl.BlockSpec((tt, d), lambda i, *_: (i, 0)),
            scratch_shapes=[
                pltpu.VMEM((2, TOP_K, tt * Y_ROWS, LANES), F32),
                pltpu.SemaphoreType.DMA((2,)),
            ],
        ),
        out_shape=jax.ShapeDtypeStruct((n, d), F32),
        compiler_params=_cparams(),
        name="combine",
    )(dest * Y_ROWS, h, route, ys)
    return out.reshape(bsz, seq, d)
```

```python
import functools

import jax
import jax.numpy as jnp
from jax import lax
from jax.experimental import pallas as pl
from jax.experimental.pallas import tpu as pltpu

F32 = jnp.float32
BF16 = jnp.bfloat16

D_MODEL = 2048
N_HEADS = 16
N_KV_HEADS = 4
HEAD_DIM = 64
GROUP = N_HEADS // N_KV_HEADS
ROT_DIM = HEAD_DIM // 4
ROPE_THETA = 500000.0
WINDOW = 128
BLOCK = 128
ATTN_WIDTH = N_HEADS * HEAD_DIM
KV_WIDTH = N_KV_HEADS * HEAD_DIM
POOL_WINDOWS = (2, 4, 8, 16)
POOL_WIDTH = D_MODEL // 2
POOL_GROUP_DIM = POOL_WIDTH // len(POOL_WINDOWS)
N_EXPERT_GROUPS = 4
EXPERTS_PER_GROUP = 8
N_EXPERTS = N_EXPERT_GROUPS * EXPERTS_PER_GROUP
TOP_K = 2
D_EXPERT = 512
EPS = 1e-6

LANES = 128
SUBLANES = 8
MXU_DIM = 256
HALO = 16
NEG_BIG = -1e30
HALF = D_MODEL // 2
PK_ROWS = HALF // LANES
Y_ROWS = D_MODEL // LANES

ROW_TILE = 256
EXPERT_TILE = 256
COMBINE_TILE = 256
VMEM_LIMIT = 56 * 1024 * 1024


def _cparams(n_axes=1):
    return pltpu.CompilerParams(
        dimension_semantics=("arbitrary",) * n_axes,
        vmem_limit_bytes=VMEM_LIMIT,
    )


def _head_sumsq(t, e_ref):
    t2 = t * t
    hi = t2.astype(BF16)
    lo = (t2 - hi.astype(F32)).astype(BF16)
    e = e_ref[...]
    outs = []
    for c in range(t.shape[1] // MXU_DIM):
        sl = slice(c * MXU_DIM, (c + 1) * MXU_DIM)
        outs.append(jnp.dot(hi[:, sl], e, preferred_element_type=F32)
                    + jnp.dot(lo[:, sl], e, preferred_element_type=F32))
    return outs[0] if len(outs) == 1 else jnp.concatenate(outs, axis=1)


def _norm_rope(t, gain, e_ref, cc, s_up, s_dn, out_scale):
    w = t.shape[1]
    ss = _head_sumsq(t, e_ref)
    tn = t * lax.rsqrt(ss * (1.0 / HEAD_DIM) + EPS) * gain
    reps = w // LANES
    up = pltpu.roll(tn, w - ROT_DIM // 2, axis=1)
    dn = pltpu.roll(tn, ROT_DIM // 2, axis=1)
    c = jnp.concatenate([cc] * reps, axis=1)
    su = jnp.concatenate([s_up] * reps, axis=1)
    sd = jnp.concatenate([s_dn] * reps, axis=1)
    return (tn * c + up * su + dn * sd) * out_scale


def _inproj_kernel(x_ref, g_ref, wq_ref, wk_ref, wv_ref, wu_ref, qn_ref, kn_ref,
                   cc_ref, su_ref, sd_ref, e_ref, wpool_ref, pscale_ref,
                   q_out, k_out, v_out, y_out, ubuf, *, tiles_per_seq):
    i = pl.program_id(0)
    tm = x_ref.shape[0]
    x = x_ref[...]
    ms = jnp.mean(x * x, axis=-1, keepdims=True)
    hn = (x * lax.rsqrt(ms + EPS) * g_ref[...]).astype(BF16)

    cc, s_up, s_dn = cc_ref[...], su_ref[...], sd_ref[...]
    q = jnp.dot(hn, wq_ref[...], preferred_element_type=F32)
    q_out[...] = _norm_rope(q, qn_ref[...], e_ref, cc, s_up, s_dn,
                            HEAD_DIM ** -0.5).astype(BF16)
    k = jnp.dot(hn, wk_ref[...], preferred_element_type=F32)
    k_out[...] = _norm_rope(k, kn_ref[...], e_ref, cc, s_up, s_dn, 1.0).astype(BF16)
    v_out[...] = jnp.dot(hn, wv_ref[...], preferred_element_type=F32).astype(BF16)

    @pl.when(i % tiles_per_seq == 0)
    def _():
        ubuf[0:HALO, :] = jnp.zeros((HALO, POOL_WIDTH), F32)

    ubuf[HALO:HALO + tm, :] = jnp.dot(hn, wu_ref[...], preferred_element_type=F32)
    pos = (i % tiles_per_seq) * tm + lax.broadcasted_iota(
        jnp.int32, (tm, POOL_GROUP_DIM), 0)
    for g, w in enumerate(POOL_WINDOWS):
        cols = slice(g * POOL_GROUP_DIM, (g + 1) * POOL_GROUP_DIM)
        u_g = ubuf[HALO:HALO + tm, cols]
        acc = u_g
        for j in range(1, w):
            acc = acc + ubuf[HALO - j:HALO - j + tm, cols]
        cnt = jnp.minimum(pos + 1, w).astype(F32)
        d = (acc / cnt - u_g).astype(BF16)
        y = jnp.dot(d, wpool_ref[g], preferred_element_type=F32)
        y_out[:, cols] = (y * pscale_ref[:, cols]).astype(BF16)
    ubuf[0:HALO, :] = ubuf[tm:tm + HALO, :]


def _attn_kernel(sinks_ref, q_ref, kc_ref, kp_ref, vc_ref, vp_ref, o_ref):
    j = pl.program_id(1)
    kband = jnp.concatenate([kp_ref[...], kc_ref[...]], axis=0)
    vband = jnp.concatenate([vp_ref[...], vc_ref[...]], axis=0)
    row = lax.broadcasted_iota(jnp.int32, (BLOCK, 2 * BLOCK), 0)
    col = lax.broadcasted_iota(jnp.int32, (BLOCK, 2 * BLOCK), 1)
    diff = row + BLOCK - col
    first = jnp.where(j > 0, 0, BLOCK)
    valid = (diff >= 0) & (diff < WINDOW) & (col >= first)
    for g in range(N_KV_HEADS):
        kg = kband[:, g * HEAD_DIM:(g + 1) * HEAD_DIM]
        vg = vband[:, g * HEAD_DIM:(g + 1) * HEAD_DIM]
        for hh in range(GROUP):
            h = g * GROUP + hh
            qh = q_ref[:, h * HEAD_DIM:(h + 1) * HEAD_DIM]
            s = lax.dot_general(qh, kg, (((1,), (1,)), ((), ())),
                                preferred_element_type=F32)
            s = jnp.where(valid, s, NEG_BIG)
            sink = sinks_ref[h]
            m = jnp.maximum(jnp.max(s, axis=-1, keepdims=True), sink)
            p = jnp.exp(s - m)
            denom = jnp.sum(p, axis=-1, keepdims=True) + jnp.exp(sink - m)
            o = jnp.dot(p.astype(BF16), vg, preferred_element_type=F32)
            o_ref[:, h * HEAD_DIM:(h + 1) * HEAD_DIM] = (o / denom).astype(BF16)


def _outproj_router_kernel(o_ref, y_ref, x_ref, woa_ref, wop_ref, g_ref, wr_ref,
                           br_ref, tri_ref, h_out, hpk_out, route_out, cnt_out):
    tm = x_ref.shape[0]
    h = (x_ref[...]
         + jnp.dot(o_ref[...], woa_ref[...], preferred_element_type=F32)
         + jnp.dot(y_ref[...], wop_ref[...], preferred_element_type=F32))
    h_out[...] = h
    ms = jnp.mean(h * h, axis=-1, keepdims=True)
    hn = h * lax.rsqrt(ms + EPS) * g_ref[...]
    hb = hn.astype(BF16)
    hb32 = hb.astype(F32)
    lo_bits = lax.shift_right_logical(
        pltpu.bitcast(hb32[:, :HALF], jnp.uint32), jnp.uint32(16))
    hi_bits = pltpu.bitcast(hb32[:, HALF:], jnp.uint32) & jnp.uint32(0xFFFF0000)
    packed = lo_bits | hi_bits
    for s in range(PK_ROWS):
        hpk_out[pl.ds(s, tm, stride=PK_ROWS), :] = packed[:, s * LANES:(s + 1) * LANES]

    hl = (hn - hb32).astype(BF16)
    both = jnp.dot(hb, wr_ref[...], preferred_element_type=F32)
    logits = (both[:, :LANES] + both[:, LANES:]
              + jnp.dot(hl, wr_ref[:, :LANES], preferred_element_type=F32)
              + br_ref[...])
    lane = lax.broadcasted_iota(jnp.int32, (tm, LANES), 1)
    lane_f = lane.astype(F32)

    def first_lane_of(mask):
        return jnp.min(jnp.where(mask, lane_f, float(LANES)), axis=-1, keepdims=True)

    coarse = jnp.where(lane < N_EXPERT_GROUPS, logits, NEG_BIG)
    cmax = jnp.max(coarse, axis=-1, keepdims=True)
    grp = first_lane_of(coarse == cmax)
    p_grp = 1.0 / jnp.sum(jnp.exp(coarse - cmax), axis=-1, keepdims=True)

    flo = N_EXPERT_GROUPS + EXPERTS_PER_GROUP * grp
    fmask = (lane_f >= flo) & (lane_f < flo + EXPERTS_PER_GROUP)
    fsel = jnp.where(fmask, logits, NEG_BIG)
    fmax = jnp.max(fsel, axis=-1, keepdims=True)
    fe = jnp.exp(fsel - fmax)
    p = fe / jnp.sum(fe, axis=-1, keepdims=True)
    pc = jnp.where(fmask, p, -1.0)
    t1 = jnp.max(pc, axis=-1, keepdims=True)
    i1 = first_lane_of(pc == t1)
    pc2 = jnp.where(lane_f == i1, -1.0, pc)
    t2 = jnp.max(pc2, axis=-1, keepdims=True)
    i2 = first_lane_of(pc2 == t2)
    tsum = t1 + t2
    g1 = p_grp * t1 / tsum
    g2 = p_grp * t2 / tsum
    e1 = i1 - N_EXPERT_GROUPS
    e2 = i2 - N_EXPERT_GROUPS

    oh1 = lane_f == e1
    oh2 = lane_f == e2
    oh = jnp.where(oh1, 1.0, 0.0) + jnp.where(oh2, 1.0, 0.0)
    prefix = jnp.dot(tri_ref[...], oh.astype(BF16), preferred_element_type=F32)
    r1 = jnp.sum(jnp.where(oh1, prefix, 0.0), axis=-1, keepdims=True)
    r2 = jnp.sum(jnp.where(oh2, prefix, 0.0), axis=-1, keepdims=True)
    route_out[...] = jnp.where(
        lane == 0, e1,
        jnp.where(lane == 1, e2,
                  jnp.where(lane == 2, g1,
                            jnp.where(lane == 3, g2,
                                      jnp.where(lane == 4, r1,
                                                jnp.where(lane == 5, r2, 0.0))))))
    cnt_out[...] = jnp.broadcast_to(jnp.sum(oh, axis=0, keepdims=True), cnt_out.shape)


def _expert_kernel(eid_ref, first_ref, nact_ref, off_ref,
                   hpk_hbm, wg_ref, wu_ref, wd_ref, ys_out,
                   xbuf, wg_s, wu_s, wd_s, sems):
    i = pl.program_id(0)
    tm = ys_out.shape[0] // Y_ROWS
    n_act = nact_ref[0]

    def row_copy(blk, slot, r, priority):
        src = pl.multiple_of(off_ref[blk * tm + r], PK_ROWS)
        dst = pl.multiple_of(r * PK_ROWS, PK_ROWS)
        pltpu.make_async_copy(hpk_hbm.at[pl.ds(src, PK_ROWS)],
                              xbuf.at[slot, pl.ds(dst, PK_ROWS)],
                              sems.at[slot]).start(priority=priority)

    def gather(blk, slot):
        def body(r2, carry):
            row_copy(blk, slot, 2 * r2, 0)
            row_copy(blk, slot, 2 * r2 + 1, 1)
            return carry
        lax.fori_loop(0, tm // 2, body, 0)

    @pl.when(i == 0)
    def _():
        gather(0, 0)

    @pl.when(i + 1 < n_act)
    def _():
        gather(i + 1, (i + 1) % 2)

    @pl.when(i < n_act)
    def _():
        slot = i % 2

        @pl.when(first_ref[i] == 1)
        def _():
            wg_s[...] = wg_ref[0].astype(BF16)
            wu_s[...] = wu_ref[0].astype(BF16)
            wd_s[...] = wd_ref[0].astype(BF16)

        pltpu.make_async_copy(hpk_hbm.at[pl.ds(0, tm * PK_ROWS)], xbuf.at[slot],
                              sems.at[slot]).wait()
        xu = jnp.concatenate(
            [xbuf[slot, pl.ds(s, tm, stride=PK_ROWS), :] for s in range(PK_ROWS)],
            axis=1)
        x_lo = pltpu.bitcast(lax.shift_left(xu, jnp.uint32(16)), F32).astype(BF16)
        x_hi = pltpu.bitcast(xu & jnp.uint32(0xFFFF0000), F32).astype(BF16)
        a = (jnp.dot(x_lo, wg_s[0:HALF, :], preferred_element_type=F32)
             + jnp.dot(x_hi, wg_s[HALF:, :], preferred_element_type=F32))
        b = (jnp.dot(x_lo, wu_s[0:HALF, :], preferred_element_type=F32)
             + jnp.dot(x_hi, wu_s[HALF:, :], preferred_element_type=F32))
        mid = (a * jax.nn.sigmoid(a) * b).astype(BF16)
        y = jnp.dot(mid, wd_s[...], preferred_element_type=F32)
        for s in range(Y_ROWS):
            ys_out[pl.ds(s, tm, stride=Y_ROWS), :] = y[:, s * LANES:(s + 1) * LANES]

    @pl.when(i >= n_act)
    def _():
        ys_out[...] = jnp.zeros(ys_out.shape, F32)


def _combine_kernel(pos_ref, h_ref, route_ref, ys_hbm, out_ref, ybuf, sems):
    i = pl.program_id(0)
    n = pl.num_programs(0)
    tt = h_ref.shape[0]

    def gather(blk, slot):
        def body(r, carry):
            base = (blk * tt + r) * TOP_K
            dst = pl.multiple_of(r * Y_ROWS, Y_ROWS)
            for k in range(TOP_K):
                src = pl.multiple_of(pos_ref[base + k], Y_ROWS)
                pltpu.make_async_copy(ys_hbm.at[pl.ds(src, Y_ROWS)],
                                      ybuf.at[slot, k, pl.ds(dst, Y_ROWS)],
                                      sems.at[slot]).start(priority=k)
            return carry
        lax.fori_loop(0, tt, body, 0)

    @pl.when(i == 0)
    def _():
        gather(0, 0)

    @pl.when(i + 1 < n)
    def _():
        gather(i + 1, (i + 1) % 2)

    slot = i % 2
    for k in range(TOP_K):
        pltpu.make_async_copy(ys_hbm.at[pl.ds(0, tt * Y_ROWS)], ybuf.at[slot, k],
                              sems.at[slot]).wait()

    g0 = jnp.broadcast_to(route_ref[:, 2:3], (tt, LANES))
    g1 = jnp.broadcast_to(route_ref[:, 3:4], (tt, LANES))
    for s in range(Y_ROWS):
        cols = slice(s * LANES, (s + 1) * LANES)
        y0 = ybuf[slot, 0, pl.ds(s, tt, stride=Y_ROWS), :]
        y1 = ybuf[slot, 1, pl.ds(s, tt, stride=Y_ROWS), :]
        out_ref[:, cols] = h_ref[:, cols] + y0 * g0 + y1 * g1


def _rope_tables(seq):
    pos = jnp.arange(seq, dtype=F32)
    inv_freq = ROPE_THETA ** (-jnp.arange(0, ROT_DIM, 2, dtype=F32) / ROT_DIM)
    ang = pos[:, None] * inv_freq[None, :]
    cos, sin = jnp.cos(ang), jnp.sin(ang)
    half = ROT_DIM // 2
    ones = jnp.ones((seq, HEAD_DIM - ROT_DIM), F32)
    zeros = jnp.zeros((seq, HEAD_DIM - ROT_DIM), F32)
    zh = jnp.zeros((seq, half), F32)
    cc = jnp.concatenate([cos, cos, ones], axis=1)
    s_up = jnp.concatenate([-sin, zh, zeros], axis=1)
    s_dn = jnp.concatenate([zh, sin, zeros], axis=1)
    rep = LANES // HEAD_DIM
    return tuple(jnp.tile(t, (1, rep)) for t in (cc, s_up, s_dn))


def _dispatch(eid, rank, tile_cnt, n_tokens, tok_tile, tile):
    a = n_tokens * TOP_K
    n_tiles = n_tokens // tok_tile
    experts = jnp.arange(N_EXPERTS, dtype=jnp.int32)
    tile_off = jnp.cumsum(tile_cnt, axis=0) - tile_cnt
    counts = jnp.sum(tile_cnt, axis=0)
    padded = ((counts + tile - 1) // tile) * tile
    pad_end = jnp.cumsum(padded)
    pad_start = pad_end - padded
    base = pad_start[None, :] + tile_off
    onehot = eid.reshape(n_tiles, tok_tile * TOP_K, 1) == experts
    base_tok = jnp.sum(jnp.where(onehot, base[:, None, :], 0), axis=-1)
    dest = (base_tok.reshape(a) + rank.reshape(a)).astype(jnp.int32)

    n_rows = a + N_EXPERTS * tile
    n_blocks = n_rows // tile
    blk_start = jnp.arange(n_blocks, dtype=jnp.int32) * tile
    blk_eid = jnp.minimum(
        jnp.sum((blk_start[:, None] >= pad_end[None, :]).astype(jnp.int32), axis=-1),
        N_EXPERTS - 1)
    prev = jnp.concatenate([jnp.full((1,), -1, jnp.int32), blk_eid[:-1]])
    first = (blk_eid != prev).astype(jnp.int32)
    n_act = (pad_end[-1] // tile).astype(jnp.int32).reshape(1)

    gap = padded - counts
    cgap = jnp.cumsum(gap)
    j = jnp.arange(n_rows - a, dtype=jnp.int32)
    e_j = jnp.sum((j[:, None] >= cgap[None, :]).astype(jnp.int32), axis=-1)
    shift = pad_start + counts - (cgap - gap)
    key_in = j + jnp.sum(jnp.where(e_j[:, None] == experts, shift[None, :], 0), axis=-1)
    key_tail = pad_end[-1] + (j - cgap[-1])
    fill_key = jnp.where(e_j < N_EXPERTS, key_in, key_tail)
    keys = jnp.concatenate([dest, fill_key.astype(jnp.int32)])
    toks = jnp.concatenate([jnp.arange(a, dtype=jnp.int32) // TOP_K,
                            jnp.zeros((n_rows - a,), jnp.int32)])
    _, row_tok = lax.sort((keys, toks), num_keys=1)
    return dest, row_tok, blk_eid, first, n_act, n_rows, n_blocks


def kernel(x, norm_mix, w_in, q_norm, k_norm, sinks, w_pool, pool_scale, w_out,
           norm_ffn, w_coarse, b_coarse, w_fine, b_fine, w_gate, w_up, w_down):
    bsz, seq, d = x.shape
    n = bsz * seq
    assert d == D_MODEL and seq % ROW_TILE == 0 and seq % BLOCK == 0
    assert norm_mix.shape[0] == 1, "single-layer problem"
    xf = x.reshape(n, d)

    w_in_b = w_in[0].astype(BF16)
    wq = w_in_b[:, :ATTN_WIDTH]
    wk = w_in_b[:, ATTN_WIDTH:ATTN_WIDTH + KV_WIDTH]
    wv = w_in_b[:, ATTN_WIDTH + KV_WIDTH:ATTN_WIDTH + 2 * KV_WIDTH]
    wu = w_in_b[:, ATTN_WIDTH + 2 * KV_WIDTH:]
    qn = jnp.tile(q_norm[0], N_HEADS).reshape(1, ATTN_WIDTH)
    kn = jnp.tile(k_norm[0], N_KV_HEADS).reshape(1, KV_WIDTH)
    cc, s_up, s_dn = _rope_tables(seq)
    lane_head = jnp.arange(MXU_DIM) // HEAD_DIM
    e_mat = (lane_head[:, None] == lane_head[None, :]).astype(BF16)
    w_pool_b = w_pool[0].astype(BF16)
    pscale = pool_scale[0].reshape(1, POOL_WIDTH)
    w_out_b = w_out[0].astype(BF16)
    wo_attn, wo_pool = w_out_b[:ATTN_WIDTH], w_out_b[ATTN_WIDTH:]
    n_router = N_EXPERT_GROUPS + N_EXPERTS
    w_r = jnp.concatenate([w_coarse[0], w_fine[0]], axis=1)
    w_r = jnp.pad(w_r, ((0, 0), (0, LANES - n_router)))
    w_r_hi = w_r.astype(BF16)
    w_r_lo = (w_r - w_r_hi.astype(F32)).astype(BF16)
    w_r2 = jnp.concatenate([w_r_hi, w_r_lo], axis=1)
    b_r = jnp.pad(jnp.concatenate([b_coarse[0], b_fine[0]]),
                  (0, LANES - n_router)).reshape(1, LANES)

    tm = ROW_TILE
    tiles_per_seq = seq // tm
    n_tiles = n // tm
    idx = jnp.arange(tm)
    tri = (idx[None, :] < idx[:, None]).astype(BF16)
    const = lambda *_: (0, 0)
    row_blk = lambda i: (i, 0)

    q, k, v, y = pl.pallas_call(
        functools.partial(_inproj_kernel, tiles_per_seq=tiles_per_seq),
        grid=(n_tiles,),
        in_specs=[
            pl.BlockSpec((tm, d), row_blk),
            pl.BlockSpec((1, d), const),
            pl.BlockSpec((d, ATTN_WIDTH), const),
            pl.BlockSpec((d, KV_WIDTH), const),
            pl.BlockSpec((d, KV_WIDTH), const),
            pl.BlockSpec((d, POOL_WIDTH), const),
            pl.BlockSpec((1, ATTN_WIDTH), const),
            pl.BlockSpec((1, KV_WIDTH), const),
            pl.BlockSpec((tm, LANES), lambda i: (i % tiles_per_seq, 0)),
            pl.BlockSpec((tm, LANES), lambda i: (i % tiles_per_seq, 0)),
            pl.BlockSpec((tm, LANES), lambda i: (i % tiles_per_seq, 0)),
            pl.BlockSpec((MXU_DIM, MXU_DIM), const),
            pl.BlockSpec((len(POOL_WINDOWS), POOL_GROUP_DIM, POOL_GROUP_DIM),
                         lambda i: (0, 0, 0)),
            pl.BlockSpec((1, POOL_WIDTH), const),
        ],
        out_specs=[
            pl.BlockSpec((tm, ATTN_WIDTH), row_blk),
            pl.BlockSpec((tm, KV_WIDTH), row_blk),
            pl.BlockSpec((tm, KV_WIDTH), row_blk),
            pl.BlockSpec((tm, POOL_WIDTH), row_blk),
        ],
        out_shape=[
            jax.ShapeDtypeStruct((n, ATTN_WIDTH), BF16),
            jax.ShapeDtypeStruct((n, KV_WIDTH), BF16),
            jax.ShapeDtypeStruct((n, KV_WIDTH), BF16),
            jax.ShapeDtypeStruct((n, POOL_WIDTH), BF16),
        ],
        scratch_shapes=[pltpu.VMEM((HALO + tm, POOL_WIDTH), F32)],
        compiler_params=_cparams(),
        name="inproj",
    )(xf, norm_mix[0].reshape(1, d), wq, wk, wv, wu, qn, kn, cc, s_up, s_dn,
      e_mat, w_pool_b, pscale)

    nb = seq // BLOCK
    cur = lambda b, j, *_: (b * nb + j, 0)
    prv = lambda b, j, *_: (b * nb + jnp.maximum(j - 1, 0), 0)
    attn = pl.pallas_call(
        _attn_kernel,
        grid_spec=pltpu.PrefetchScalarGridSpec(
            num_scalar_prefetch=1,
            grid=(bsz, nb),
            in_specs=[
                pl.BlockSpec((BLOCK, ATTN_WIDTH), cur),
                pl.BlockSpec((BLOCK, KV_WIDTH), cur),
                pl.BlockSpec((BLOCK, KV_WIDTH), prv),
                pl.BlockSpec((BLOCK, KV_WIDTH), cur),
                pl.BlockSpec((BLOCK, KV_WIDTH), prv),
            ],
            out_specs=pl.BlockSpec((BLOCK, ATTN_WIDTH), cur),
        ),
        out_shape=jax.ShapeDtypeStruct((n, ATTN_WIDTH), BF16),
        compiler_params=_cparams(2),
        name="swa_attn",
    )(sinks[0].astype(F32), q, k, k, v, v)

    h, hpk, route, cnt = pl.pallas_call(
        _outproj_router_kernel,
        grid=(n_tiles,),
        in_specs=[
            pl.BlockSpec((tm, ATTN_WIDTH), row_blk),
            pl.BlockSpec((tm, POOL_WIDTH), row_blk),
            pl.BlockSpec((tm, d), row_blk),
            pl.BlockSpec((ATTN_WIDTH, d), const),
            pl.BlockSpec((POOL_WIDTH, d), const),
            pl.BlockSpec((1, d), const),
            pl.BlockSpec((d, 2 * LANES), const),
            pl.BlockSpec((1, LANES), const),
            pl.BlockSpec((tm, tm), const),
        ],
        out_specs=[
            pl.BlockSpec((tm, d), row_blk),
            pl.BlockSpec((tm * PK_ROWS, LANES), row_blk),
            pl.BlockSpec((tm, LANES), row_blk),
            pl.BlockSpec((SUBLANES, LANES), row_blk),
        ],
        out_shape=[
            jax.ShapeDtypeStruct((n, d), F32),
            jax.ShapeDtypeStruct((n * PK_ROWS, LANES), jnp.uint32),
            jax.ShapeDtypeStruct((n, LANES), F32),
            jax.ShapeDtypeStruct((n_tiles * SUBLANES, LANES), F32),
        ],
        compiler_params=_cparams(),
        name="outproj_router",
    )(attn, y, xf, wo_attn, wo_pool, norm_ffn[0].reshape(1, d), w_r2, b_r, tri)

    eid = route[:, :TOP_K].astype(jnp.int32)
    rank = route[:, 4:4 + TOP_K].astype(jnp.int32)
    tile_cnt = cnt.reshape(n_tiles, SUBLANES, LANES)[:, 0, :N_EXPERTS].astype(jnp.int32)
    te = EXPERT_TILE
    dest, row_tok, blk_eid, first, n_act, n_rows, n_blocks = _dispatch(
        eid, rank, tile_cnt, n, tm, te)

    w_idx = lambda i, eid_r, *_: (eid_r[i], 0, 0)
    ys = pl.pallas_call(
        _expert_kernel,
        grid_spec=pltpu.PrefetchScalarGridSpec(
            num_scalar_prefetch=4,
            grid=(n_blocks,),
            in_specs=[
                pl.BlockSpec(memory_space=pl.ANY),
                pl.BlockSpec((1, d, D_EXPERT), w_idx),
                pl.BlockSpec((1, d, D_EXPERT), w_idx),
                pl.BlockSpec((1, D_EXPERT, d), w_idx),
            ],
            out_specs=pl.BlockSpec((te * Y_ROWS, LANES), lambda i, *_: (i, 0)),
            scratch_shapes=[
                pltpu.VMEM((2, te * PK_ROWS, LANES), jnp.uint32),
                pltpu.VMEM((d, D_EXPERT), BF16),
                pltpu.VMEM((d, D_EXPERT), BF16),
                pltpu.VMEM((D_EXPERT, d), BF16),
                pltpu.SemaphoreType.DMA((2,)),
            ],
        ),
        out_shape=jax.ShapeDtypeStruct((n_rows * Y_ROWS, LANES), F32),
        compiler_params=_cparams(),
        name="experts",
    )(blk_eid, first, n_act, row_tok * PK_ROWS, hpk, w_gate[0], w_up[0], w_down[0])

    tt = COMBINE_TILE
    out = pl.pallas_call(
        _combine_kernel,
        grid_spec=pltpu.PrefetchScalarGridSpec(
            num_scalar_prefetch=1,
            grid=(n // tt,),
            in_specs=[
                pl.BlockSpec((tt, d), lambda i, *_: (i, 0)),
                pl.BlockSpec((tt, LANES), lambda i, *_: (i, 0)),
                pl.BlockSpec(memory_space=pl.ANY),
            ],
            out_specs=pl.BlockSpec((tt, d), lambda i, *_: (i, 0)),
            scratch_shapes=[
                pltpu.VMEM((2, TOP_K, tt * Y_ROWS, LANES), F32),
                pltpu.SemaphoreType.DMA((2,)),
            ],
        ),
        out_shape=jax.ShapeDtypeStruct((n, d), F32),
        compiler_params=_cparams(),
        name="combine",
    )(dest * Y_ROWS, h, route, ys)
    return out.reshape(bsz, seq, d)
```

```python
import functools

import jax
import jax.numpy as jnp
from jax import lax
from jax.experimental import pallas as pl
from jax.experimental.pallas import tpu as pltpu

F32 = jnp.float32
BF16 = jnp.bfloat16

D_MODEL = 2048
N_HEADS = 16
N_KV_HEADS = 4
HEAD_DIM = 64
GROUP = N_HEADS // N_KV_HEADS
ROT_DIM = HEAD_DIM // 4
ROPE_THETA = 500000.0
WINDOW = 128
BLOCK = 128
ATTN_WIDTH = N_HEADS * HEAD_DIM
KV_WIDTH = N_KV_HEADS * HEAD_DIM
POOL_WINDOWS = (2, 4, 8, 16)
POOL_WIDTH = D_MODEL // 2
POOL_GROUP_DIM = POOL_WIDTH // len(POOL_WINDOWS)
N_EXPERT_GROUPS = 4
EXPERTS_PER_GROUP = 8
N_EXPERTS = N_EXPERT_GROUPS * EXPERTS_PER_GROUP
TOP_K = 2
D_EXPERT = 512
EPS = 1e-6

LANES = 128
SUBLANES = 8
MXU_DIM = 256
HALO = 16
NEG_BIG = -1e30
HALF = D_MODEL // 2
PK_ROWS = HALF // LANES
Y_ROWS = D_MODEL // LANES

ROW_TILE = 256
EXPERT_TILE = 256
LOCAL_TILE = 512
CHUNK = 8
LT_MAX = TOP_K * LOCAL_TILE + N_EXPERTS * CHUNK
VMEM_LIMIT = 56 * 1024 * 1024


def _cparams(n_axes=1):
    return pltpu.CompilerParams(
        dimension_semantics=("arbitrary",) * n_axes,
        vmem_limit_bytes=VMEM_LIMIT,
    )


def _head_sumsq(t, e_ref):
    t2 = t * t
    hi = t2.astype(BF16)
    lo = (t2 - hi.astype(F32)).astype(BF16)
    e = e_ref[...]
    outs = []
    for c in range(t.shape[1] // MXU_DIM):
        sl = slice(c * MXU_DIM, (c + 1) * MXU_DIM)
        outs.append(jnp.dot(hi[:, sl], e, preferred_element_type=F32)
                    + jnp.dot(lo[:, sl], e, preferred_element_type=F32))
    return outs[0] if len(outs) == 1 else jnp.concatenate(outs, axis=1)


def _norm_rope(t, gain, e_ref, cc, s_up, s_dn, out_scale):
    w = t.shape[1]
    ss = _head_sumsq(t, e_ref)
    tn = t * lax.rsqrt(ss * (1.0 / HEAD_DIM) + EPS) * gain
    reps = w // LANES
    up = pltpu.roll(tn, w - ROT_DIM // 2, axis=1)
    dn = pltpu.roll(tn, ROT_DIM // 2, axis=1)
    c = jnp.concatenate([cc] * reps, axis=1)
    su = jnp.concatenate([s_up] * reps, axis=1)
    sd = jnp.concatenate([s_dn] * reps, axis=1)
    return (tn * c + up * su + dn * sd) * out_scale


def _inproj_kernel(x_ref, g_ref, wq_ref, wk_ref, wv_ref, wu_ref, qn_ref, kn_ref,
                   cc_ref, su_ref, sd_ref, e_ref, wpool_ref, pscale_ref,
                   q_out, k_out, v_out, y_out, ubuf, *, tiles_per_seq):
    i = pl.program_id(0)
    tm = x_ref.shape[0]
    x = x_ref[...]
    ms = jnp.mean(x * x, axis=-1, keepdims=True)
    hn = (x * lax.rsqrt(ms + EPS) * g_ref[...]).astype(BF16)

    cc, s_up, s_dn = cc_ref[...], su_ref[...], sd_ref[...]
    q = jnp.dot(hn, wq_ref[...], preferred_element_type=F32)
    q_out[...] = _norm_rope(q, qn_ref[...], e_ref, cc, s_up, s_dn,
                            HEAD_DIM ** -0.5).astype(BF16)
    k = jnp.dot(hn, wk_ref[...], preferred_element_type=F32)
    k_out[...] = _norm_rope(k, kn_ref[...], e_ref, cc, s_up, s_dn, 1.0).astype(BF16)
    v_out[...] = jnp.dot(hn, wv_ref[...], preferred_element_type=F32).astype(BF16)

    @pl.when(i % tiles_per_seq == 0)
    def _():
        ubuf[0:HALO, :] = jnp.zeros((HALO, POOL_WIDTH), F32)

    ubuf[HALO:HALO + tm, :] = jnp.dot(hn, wu_ref[...], preferred_element_type=F32)
    pos = (i % tiles_per_seq) * tm + lax.broadcasted_iota(
        jnp.int32, (tm, POOL_GROUP_DIM), 0)
    for g, w in enumerate(POOL_WINDOWS):
        cols = slice(g * POOL_GROUP_DIM, (g + 1) * POOL_GROUP_DIM)
        u_g = ubuf[HALO:HALO + tm, cols]
        acc = u_g
        for j in range(1, w):
            acc = acc + ubuf[HALO - j:HALO - j + tm, cols]
        cnt = jnp.minimum(pos + 1, w).astype(F32)
        d = (acc / cnt - u_g).astype(BF16)
        y = jnp.dot(d, wpool_ref[g], preferred_element_type=F32)
        y_out[:, cols] = (y * pscale_ref[:, cols]).astype(BF16)
    ubuf[0:HALO, :] = ubuf[tm:tm + HALO, :]


def _attn_kernel(sinks_ref, q_ref, kc_ref, kp_ref, vc_ref, vp_ref, o_ref):
    j = pl.program_id(1)
    kband = jnp.concatenate([kp_ref[...], kc_ref[...]], axis=0)
    vband = jnp.concatenate([vp_ref[...], vc_ref[...]], axis=0)
    row = lax.broadcasted_iota(jnp.int32, (BLOCK, 2 * BLOCK), 0)
    col = lax.broadcasted_iota(jnp.int32, (BLOCK, 2 * BLOCK), 1)
    diff = row + BLOCK - col
    first = jnp.where(j > 0, 0, BLOCK)
    valid = (diff >= 0) & (diff < WINDOW) & (col >= first)
    for g in range(N_KV_HEADS):
        kg = kband[:, g * HEAD_DIM:(g + 1) * HEAD_DIM]
        vg = vband[:, g * HEAD_DIM:(g + 1) * HEAD_DIM]
        for hh in range(GROUP):
            h = g * GROUP + hh
            qh = q_ref[:, h * HEAD_DIM:(h + 1) * HEAD_DIM]
            s = lax.dot_general(qh, kg, (((1,), (1,)), ((), ())),
                                preferred_element_type=F32)
            s = jnp.where(valid, s, NEG_BIG)
            sink = sinks_ref[h]
            m = jnp.maximum(jnp.max(s, axis=-1, keepdims=True), sink)
            p = jnp.exp(s - m)
            denom = jnp.sum(p, axis=-1, keepdims=True) + jnp.exp(sink - m)
            o = jnp.dot(p.astype(BF16), vg, preferred_element_type=F32)
            o_ref[:, h * HEAD_DIM:(h + 1) * HEAD_DIM] = (o / denom).astype(BF16)


def _outproj_router_kernel(o_ref, y_ref, x_ref, woa_ref, wop_ref, g_ref, wr_ref,
                           br_ref, tri_ref, h_out, hpk_out, route_out, cnt_out):
    tm = x_ref.shape[0]
    h = (x_ref[...]
         + jnp.dot(o_ref[...], woa_ref[...], preferred_element_type=F32)
         + jnp.dot(y_ref[...], wop_ref[...], preferred_element_type=F32))
    h_out[...] = h
    ms = jnp.mean(h * h, axis=-1, keepdims=True)
    hn = h * lax.rsqrt(ms + EPS) * g_ref[...]
    hb = hn.astype(BF16)
    hb32 = hb.astype(F32)
    lo_bits = lax.shift_right_logical(
        pltpu.bitcast(hb32[:, :HALF], jnp.uint32), jnp.uint32(16))
    hi_bits = pltpu.bitcast(hb32[:, HALF:], jnp.uint32) & jnp.uint32(0xFFFF0000)
    packed = lo_bits | hi_bits
    for s in range(PK_ROWS):
        hpk_out[pl.ds(s, tm, stride=PK_ROWS), :] = packed[:, s * LANES:(s + 1) * LANES]

    hl = (hn - hb32).astype(BF16)
    both = jnp.dot(hb, wr_ref[...], preferred_element_type=F32)
    logits = (both[:, :LANES] + both[:, LANES:]
              + jnp.dot(hl, wr_ref[:, :LANES], preferred_element_type=F32)
              + br_ref[...])
    lane = lax.broadcasted_iota(jnp.int32, (tm, LANES), 1)
    lane_f = lane.astype(F32)

    def first_lane_of(mask):
        return jnp.min(jnp.where(mask, lane_f, float(LANES)), axis=-1, keepdims=True)

    coarse = jnp.where(lane < N_EXPERT_GROUPS, logits, NEG_BIG)
    cmax = jnp.max(coarse, axis=-1, keepdims=True)
    grp = first_lane_of(coarse == cmax)
    p_grp = 1.0 / jnp.sum(jnp.exp(coarse - cmax), axis=-1, keepdims=True)

    flo = N_EXPERT_GROUPS + EXPERTS_PER_GROUP * grp
    fmask = (lane_f >= flo) & (lane_f < flo + EXPERTS_PER_GROUP)
    fsel = jnp.where(fmask, logits, NEG_BIG)
    fmax = jnp.max(fsel, axis=-1, keepdims=True)
    fe = jnp.exp(fsel - fmax)
    p = fe / jnp.sum(fe, axis=-1, keepdims=True)
    pc = jnp.where(fmask, p, -1.0)
    t1 = jnp.max(pc, axis=-1, keepdims=True)
    i1 = first_lane_of(pc == t1)
    pc2 = jnp.where(lane_f == i1, -1.0, pc)
    t2 = jnp.max(pc2, axis=-1, keepdims=True)
    i2 = first_lane_of(pc2 == t2)
    tsum = t1 + t2
    g1 = p_grp * t1 / tsum
    g2 = p_grp * t2 / tsum
    e1 = i1 - N_EXPERT_GROUPS
    e2 = i2 - N_EXPERT_GROUPS

    oh1 = lane_f == e1
    oh2 = lane_f == e2
    oh = jnp.where(oh1, 1.0, 0.0) + jnp.where(oh2, 1.0, 0.0)
    prefix = jnp.dot(tri_ref[...], oh.astype(BF16), preferred_element_type=F32)
    r1 = jnp.sum(jnp.where(oh1, prefix, 0.0), axis=-1, keepdims=True)
    r2 = jnp.sum(jnp.where(oh2, prefix, 0.0), axis=-1, keepdims=True)
    route_out[...] = jnp.where(
        lane == 0, e1,
        jnp.where(lane == 1, e2,
                  jnp.where(lane == 2, g1,
                            jnp.where(lane == 3, g2,
                                      jnp.where(lane == 4, r1,
                                                jnp.where(lane == 5, r2, 0.0))))))
    cnt_out[...] = jnp.broadcast_to(jnp.sum(oh, axis=0, keepdims=True), cnt_out.shape)


def _expert_kernel(eid_ref, first_ref, nact_ref, off_ref,
                   hpk_hbm, wg_ref, wu_ref, wd_ref, ys_out,
                   xbuf, wg_s, wu_s, wd_s, sems):
    i = pl.program_id(0)
    tm = ys_out.shape[0] // Y_ROWS
    n_act = nact_ref[0]

    ch_rows = CHUNK * PK_ROWS
    n_ch = tm // CHUNK

    def gather(blk, slot):
        for c in range(n_ch):
            src = pl.multiple_of(off_ref[blk * n_ch + c], ch_rows)
            pltpu.make_async_copy(hpk_hbm.at[pl.ds(src, ch_rows)],
                                  xbuf.at[slot, pl.ds(c * ch_rows, ch_rows)],
                                  sems.at[slot]).start()

    @pl.when(i == 0)
    def _():
        gather(0, 0)

    @pl.when(i + 1 < n_act)
    def _():
        gather(i + 1, (i + 1) % 2)

    @pl.when(i < n_act)
    def _():
        slot = i % 2

        @pl.when(first_ref[i] == 1)
        def _():
            wg_s[...] = wg_ref[0].astype(BF16)
            wu_s[...] = wu_ref[0].astype(BF16)
            wd_s[...] = wd_ref[0].astype(BF16)

        pltpu.make_async_copy(hpk_hbm.at[pl.ds(0, tm * PK_ROWS)], xbuf.at[slot],
                              sems.at[slot]).wait()
        xu = jnp.concatenate(
            [xbuf[slot, pl.ds(s, tm, stride=PK_ROWS), :] for s in range(PK_ROWS)],
            axis=1)
        x_lo = pltpu.bitcast(lax.shift_left(xu, jnp.uint32(16)), F32).astype(BF16)
        x_hi = pltpu.bitcast(xu & jnp.uint32(0xFFFF0000), F32).astype(BF16)
        a = (jnp.dot(x_lo, wg_s[0:HALF, :], preferred_element_type=F32)
             + jnp.dot(x_hi, wg_s[HALF:, :], preferred_element_type=F32))
        b = (jnp.dot(x_lo, wu_s[0:HALF, :], preferred_element_type=F32)
             + jnp.dot(x_hi, wu_s[HALF:, :], preferred_element_type=F32))
        mid = (a * jax.nn.sigmoid(a) * b).astype(BF16)
        y = jnp.dot(mid, wd_s[...], preferred_element_type=F32)
        for s in range(Y_ROWS):
            ys_out[pl.ds(s, tm, stride=Y_ROWS), :] = y[:, s * LANES:(s + 1) * LANES]

    @pl.when(i >= n_act)
    def _():
        ys_out[...] = jnp.zeros(ys_out.shape, F32)


def _local_sort_kernel(lp_ref, hpk_ref, xs_out):
    i = pl.program_id(0)
    t_tile = hpk_ref.shape[0] // PK_ROWS
    xs_out[...] = jnp.zeros(xs_out.shape, xs_out.dtype)

    def body(q, carry):
        for u in range(SUBLANES):
            t = q * SUBLANES + u
            row = hpk_ref[pl.ds(pl.multiple_of(t * PK_ROWS, PK_ROWS), PK_ROWS), :]
            for k in range(TOP_K):
                dst = pl.multiple_of(lp_ref[(i * t_tile + t) * TOP_K + k], PK_ROWS)
                xs_out[pl.ds(dst, PK_ROWS), :] = row
        return carry
    lax.fori_loop(0, t_tile // SUBLANES, body, 0)


def _combine_kernel(gch_ref, lp_ref, gate_ref, h_ref, ys_hbm, out_ref, ylocal, acc, sems):
    i = pl.program_id(0)
    n = pl.num_programs(0)
    t_tile = h_ref.shape[0]
    ch_rows = CHUNK * Y_ROWS
    n_ch = ylocal.shape[1] // ch_rows

    def gather(tile, slot):
        def body(q, carry):
            for u in range(SUBLANES):
                lc = q * SUBLANES + u
                src = pl.multiple_of(gch_ref[tile * n_ch + lc], ch_rows)
                dst = pl.multiple_of(lc * ch_rows, ch_rows)
                pltpu.make_async_copy(ys_hbm.at[pl.ds(src, ch_rows)],
                                      ylocal.at[slot, pl.ds(dst, ch_rows)],
                                      sems.at[slot]).start()
            return carry
        lax.fori_loop(0, n_ch // SUBLANES, body, 0)

    @pl.when(i == 0)
    def _():
        gather(0, 0)

    @pl.when(i + 1 < n)
    def _():
        gather(i + 1, (i + 1) % 2)

    slot = i % 2
    pltpu.make_async_copy(ys_hbm.at[pl.ds(0, n_ch * ch_rows)], ylocal.at[slot],
                          sems.at[slot]).wait()

    def tok_body(q, carry):
        for u in range(4):
            t = q * 4 + u
            a0 = (i * t_tile + t) * TOP_K
            p0 = pl.multiple_of(lp_ref[a0], Y_ROWS)
            p1 = pl.multiple_of(lp_ref[a0 + 1], Y_ROWS)
            y0 = ylocal[slot, pl.ds(p0, Y_ROWS), :]
            y1 = ylocal[slot, pl.ds(p1, Y_ROWS), :]
            dst = pl.multiple_of(t * Y_ROWS, Y_ROWS)
            acc[pl.ds(dst, Y_ROWS), :] = gate_ref[a0] * y0 + gate_ref[a0 + 1] * y1
        return carry
    lax.fori_loop(0, t_tile // 4, tok_body, 0)

    for s in range(Y_ROWS):
        cols = slice(s * LANES, (s + 1) * LANES)
        out_ref[:, cols] = h_ref[:, cols] + acc[pl.ds(s, t_tile, stride=Y_ROWS), :]


def _rope_tables(seq):
    pos = jnp.arange(seq, dtype=F32)
    inv_freq = ROPE_THETA ** (-jnp.arange(0, ROT_DIM, 2, dtype=F32) / ROT_DIM)
    ang = pos[:, None] * inv_freq[None, :]
    cos, sin = jnp.cos(ang), jnp.sin(ang)
    half = ROT_DIM // 2
    ones = jnp.ones((seq, HEAD_DIM - ROT_DIM), F32)
    zeros = jnp.zeros((seq, HEAD_DIM - ROT_DIM), F32)
    zh = jnp.zeros((seq, half), F32)
    cc = jnp.concatenate([cos, cos, ones], axis=1)
    s_up = jnp.concatenate([-sin, zh, zeros], axis=1)
    s_dn = jnp.concatenate([zh, sin, zeros], axis=1)
    rep = LANES // HEAD_DIM
    return tuple(jnp.tile(t, (1, rep)) for t in (cc, s_up, s_dn))


def _plan(eid, rank, cnt_rt, n_tokens):
    sub = LOCAL_TILE // ROW_TILE
    nt = n_tokens // LOCAL_TILE
    experts = jnp.arange(N_EXPERTS, dtype=jnp.int32)
    cnt_sub = cnt_rt.reshape(nt, sub, N_EXPERTS)
    sub_off = jnp.cumsum(cnt_sub, axis=1) - cnt_sub
    cnt = jnp.sum(cnt_sub, axis=1)
    cntp = ((cnt + CHUNK - 1) // CHUNK) * CHUNK
    loff_end = jnp.cumsum(cntp, axis=1)
    loff = loff_end - cntp
    rows_e = jnp.sum(cntp, axis=0)
    rows_pad = ((rows_e + EXPERT_TILE - 1) // EXPERT_TILE) * EXPERT_TILE
    g_end = jnp.cumsum(rows_pad)
    g_start = g_end - rows_pad
    gpos = g_start[None, :] + jnp.cumsum(cntp, axis=0) - cntp

    onehot = eid.reshape(nt, sub, ROW_TILE * TOP_K, 1) == experts
    table = (loff[:, None, :] + sub_off)[:, :, None, :]
    lp = (jnp.sum(jnp.where(onehot, table, 0), axis=-1).reshape(-1)
          + rank.reshape(-1)).astype(jnp.int32)

    lrow = jnp.arange(LT_MAX // CHUNK, dtype=jnp.int32) * CHUNK
    e_lc = jnp.sum((lrow[None, :, None] >= loff_end[:, None, :]).astype(jnp.int32), axis=-1)
    sel = e_lc[:, :, None] == experts
    delta = jnp.sum(jnp.where(sel, (gpos - loff)[:, None, :], 0), axis=-1)
    gch = jnp.where(e_lc < N_EXPERTS, (delta + lrow[None, :]) // CHUNK, 0).reshape(-1)

    n_rows = TOP_K * n_tokens + nt * N_EXPERTS * (CHUNK - 1) + N_EXPERTS * (EXPERT_TILE - CHUNK)
    n_blocks = -(-n_rows // EXPERT_TILE)
    n_rows = n_blocks * EXPERT_TILE
    run_start = (gpos.T.reshape(-1) // CHUNK).astype(F32)
    run_len = (cntp.T.reshape(-1) // CHUNK).astype(F32)
    tile_base = jnp.arange(nt, dtype=jnp.int32)[:, None] * (LT_MAX // CHUNK)
    run_src = ((tile_base + loff // CHUNK).T.reshape(-1)).astype(F32)
    tabs = jnp.stack([run_start, run_len, run_src], axis=1)
    dtabs = tabs - jnp.concatenate([jnp.zeros((1, 3), F32), tabs[:-1]], axis=0)
    gc = jnp.arange(n_rows // CHUNK, dtype=jnp.int32)
    started = (run_start[None, :] <= gc[:, None].astype(F32)).astype(F32)
    picked = jnp.dot(started, dtabs, precision=lax.Precision.HIGHEST).astype(jnp.int32)
    within = gc - picked[:, 0]
    csrc = jnp.where(within < picked[:, 1], picked[:, 2] + within, 0)

    blk_start = jnp.arange(n_blocks, dtype=jnp.int32) * EXPERT_TILE
    blk_eid = jnp.minimum(
        jnp.sum((blk_start[:, None] >= g_end[None, :]).astype(jnp.int32), axis=-1),
        N_EXPERTS - 1)
    prev = jnp.concatenate([jnp.full((1,), -1, jnp.int32), blk_eid[:-1]])
    first = (blk_eid != prev).astype(jnp.int32)
    n_act = (g_end[-1] // EXPERT_TILE).astype(jnp.int32).reshape(1)
    return lp, gch.astype(jnp.int32), csrc.astype(jnp.int32), blk_eid, first, n_act, n_rows


def kernel(x, norm_mix, w_in, q_norm, k_norm, sinks, w_pool, pool_scale, w_out,
           norm_ffn, w_coarse, b_coarse, w_fine, b_fine, w_gate, w_up, w_down):
    bsz, seq, d = x.shape
    n = bsz * seq
    assert d == D_MODEL and seq % ROW_TILE == 0 and seq % BLOCK == 0
    assert norm_mix.shape[0] == 1, "single-layer problem"
    xf = x.reshape(n, d)

    w_in_b = w_in[0].astype(BF16)
    wq = w_in_b[:, :ATTN_WIDTH]
    wk = w_in_b[:, ATTN_WIDTH:ATTN_WIDTH + KV_WIDTH]
    wv = w_in_b[:, ATTN_WIDTH + KV_WIDTH:ATTN_WIDTH + 2 * KV_WIDTH]
    wu = w_in_b[:, ATTN_WIDTH + 2 * KV_WIDTH:]
    qn = jnp.tile(q_norm[0], N_HEADS).reshape(1, ATTN_WIDTH)
    kn = jnp.tile(k_norm[0], N_KV_HEADS).reshape(1, KV_WIDTH)
    cc, s_up, s_dn = _rope_tables(seq)
    lane_head = jnp.arange(MXU_DIM) // HEAD_DIM
    e_mat = (lane_head[:, None] == lane_head[None, :]).astype(BF16)
    w_pool_b = w_pool[0].astype(BF16)
    pscale = pool_scale[0].reshape(1, POOL_WIDTH)
    w_out_b = w_out[0].astype(BF16)
    wo_attn, wo_pool = w_out_b[:ATTN_WIDTH], w_out_b[ATTN_WIDTH:]
    n_router = N_EXPERT_GROUPS + N_EXPERTS
    w_r = jnp.concatenate([w_coarse[0], w_fine[0]], axis=1)
    w_r = jnp.pad(w_r, ((0, 0), (0, LANES - n_router)))
    w_r_hi = w_r.astype(BF16)
    w_r_lo = (w_r - w_r_hi.astype(F32)).astype(BF16)
    w_r2 = jnp.concatenate([w_r_hi, w_r_lo], axis=1)
    b_r = jnp.pad(jnp.concatenate([b_coarse[0], b_fine[0]]),
                  (0, LANES - n_router)).reshape(1, LANES)

    tm = ROW_TILE
    tiles_per_seq = seq // tm
    n_tiles = n // tm
    idx = jnp.arange(tm)
    tri = (idx[None, :] < idx[:, None]).astype(BF16)
    const = lambda *_: (0, 0)
    row_blk = lambda i: (i, 0)

    q, k, v, y = pl.pallas_call(
        functools.partial(_inproj_kernel, tiles_per_seq=tiles_per_seq),
        grid=(n_tiles,),
        in_specs=[
            pl.BlockSpec((tm, d), row_blk),
            pl.BlockSpec((1, d), const),
            pl.BlockSpec((d, ATTN_WIDTH), const),
            pl.BlockSpec((d, KV_WIDTH), const),
            pl.BlockSpec((d, KV_WIDTH), const),
            pl.BlockSpec((d, POOL_WIDTH), const),
            pl.BlockSpec((1, ATTN_WIDTH), const),
            pl.BlockSpec((1, KV_WIDTH), const),
            pl.BlockSpec((tm, LANES), lambda i: (i % tiles_per_seq, 0)),
            pl.BlockSpec((tm, LANES), lambda i: (i % tiles_per_seq, 0)),
            pl.BlockSpec((tm, LANES), lambda i: (i % tiles_per_seq, 0)),
            pl.BlockSpec((MXU_DIM, MXU_DIM), const),
            pl.BlockSpec((len(POOL_WINDOWS), POOL_GROUP_DIM, POOL_GROUP_DIM),
                         lambda i: (0, 0, 0)),
            pl.BlockSpec((1, POOL_WIDTH), const),
        ],
        out_specs=[
            pl.BlockSpec((tm, ATTN_WIDTH), row_blk),
            pl.BlockSpec((tm, KV_WIDTH), row_blk),
            pl.BlockSpec((tm, KV_WIDTH), row_blk),
            pl.BlockSpec((tm, POOL_WIDTH), row_blk),
        ],
        out_shape=[
            jax.ShapeDtypeStruct((n, ATTN_WIDTH), BF16),
            jax.ShapeDtypeStruct((n, KV_WIDTH), BF16),
            jax.ShapeDtypeStruct((n, KV_WIDTH), BF16),
            jax.ShapeDtypeStruct((n, POOL_WIDTH), BF16),
        ],
        scratch_shapes=[pltpu.VMEM((HALO + tm, POOL_WIDTH), F32)],
        compiler_params=_cparams(),
        name="inproj",
    )(xf, norm_mix[0].reshape(1, d), wq, wk, wv, wu, qn, kn, cc, s_up, s_dn,
      e_mat, w_pool_b, pscale)

    nb = seq // BLOCK
    cur = lambda b, j, *_: (b * nb + j, 0)
    prv = lambda b, j, *_: (b * nb + jnp.maximum(j - 1, 0), 0)
    attn = pl.pallas_call(
        _attn_kernel,
        grid_spec=pltpu.PrefetchScalarGridSpec(
            num_scalar_prefetch=1,
            grid=(bsz, nb),
            in_specs=[
                pl.BlockSpec((BLOCK, ATTN_WIDTH), cur),
                pl.BlockSpec((BLOCK, KV_WIDTH), cur),
                pl.BlockSpec((BLOCK, KV_WIDTH), prv),
                pl.BlockSpec((BLOCK, KV_WIDTH), cur),
                pl.BlockSpec((BLOCK, KV_WIDTH), prv),
            ],
            out_specs=pl.BlockSpec((BLOCK, ATTN_WIDTH), cur),
        ),
        out_shape=jax.ShapeDtypeStruct((n, ATTN_WIDTH), BF16),
        compiler_params=_cparams(2),
        name="swa_attn",
    )(sinks[0].astype(F32), q, k, k, v, v)

    h, hpk, route, cnt = pl.pallas_call(
        _outproj_router_kernel,
        grid=(n_tiles,),
        in_specs=[
            pl.BlockSpec((tm, ATTN_WIDTH), row_blk),
            pl.BlockSpec((tm, POOL_WIDTH), row_blk),
            pl.BlockSpec((tm, d), row_blk),
            pl.BlockSpec((ATTN_WIDTH, d), const),
            pl.BlockSpec((POOL_WIDTH, d), const),
            pl.BlockSpec((1, d), const),
            pl.BlockSpec((d, 2 * LANES), const),
            pl.BlockSpec((1, LANES), const),
            pl.BlockSpec((tm, tm), const),
        ],
        out_specs=[
            pl.BlockSpec((tm, d), row_blk),
            pl.BlockSpec((tm * PK_ROWS, LANES), row_blk),
            pl.BlockSpec((tm, LANES), row_blk),
            pl.BlockSpec((SUBLANES, LANES), row_blk),
        ],
        out_shape=[
            jax.ShapeDtypeStruct((n, d), F32),
            jax.ShapeDtypeStruct((n * PK_ROWS, LANES), jnp.uint32),
            jax.ShapeDtypeStruct((n, LANES), F32),
            jax.ShapeDtypeStruct((n_tiles * SUBLANES, LANES), F32),
        ],
        compiler_params=_cparams(),
        name="outproj_router",
    )(attn, y, xf, wo_attn, wo_pool, norm_ffn[0].reshape(1, d), w_r2, b_r, tri)

    eid = route[:, :TOP_K].astype(jnp.int32)
    rank = route[:, 4:4 + TOP_K].astype(jnp.int32)
    tile_cnt = cnt.reshape(n_tiles, SUBLANES, LANES)[:, 0, :N_EXPERTS].astype(jnp.int32)
    te = EXPERT_TILE
    lp, gch, csrc, blk_eid, first, n_act, n_rows = _plan(eid, rank, tile_cnt, n)
    n_blocks = n_rows // te
    n_local = n // LOCAL_TILE
    gates = route[:, 2:2 + TOP_K].reshape(n * TOP_K)

    xs = pl.pallas_call(
        _local_sort_kernel,
        grid_spec=pltpu.PrefetchScalarGridSpec(
            num_scalar_prefetch=1,
            grid=(n_local,),
            in_specs=[pl.BlockSpec((LOCAL_TILE * PK_ROWS, LANES), lambda i, *_: (i, 0))],
            out_specs=pl.BlockSpec((LT_MAX * PK_ROWS, LANES), lambda i, *_: (i, 0)),
        ),
        out_shape=jax.ShapeDtypeStruct((n_local * LT_MAX * PK_ROWS, LANES), jnp.uint32),
        compiler_params=_cparams(),
        name="local_sort",
    )(lp * PK_ROWS, hpk)

    w_idx = lambda i, eid_r, *_: (eid_r[i], 0, 0)
    ys = pl.pallas_call(
        _expert_kernel,
        grid_spec=pltpu.PrefetchScalarGridSpec(
            num_scalar_prefetch=4,
            grid=(n_blocks,),
            in_specs=[
                pl.BlockSpec(memory_space=pl.ANY),
                pl.BlockSpec((1, d, D_EXPERT), w_idx),
                pl.BlockSpec((1, d, D_EXPERT), w_idx),
                pl.BlockSpec((1, D_EXPERT, d), w_idx),
            ],
            out_specs=pl.BlockSpec((te * Y_ROWS, LANES), lambda i, *_: (i, 0)),
            scratch_shapes=[
                pltpu.VMEM((2, te * PK_ROWS, LANES), jnp.uint32),
                pltpu.VMEM((d, D_EXPERT), BF16),
                pltpu.VMEM((d, D_EXPERT), BF16),
                pltpu.VMEM((D_EXPERT, d), BF16),
                pltpu.SemaphoreType.DMA((2,)),
            ],
        ),
        out_shape=jax.ShapeDtypeStruct((n_rows * Y_ROWS, LANES), F32),
        compiler_params=_cparams(),
        name="experts",
    )(blk_eid, first, n_act, csrc * (CHUNK * PK_ROWS), xs, w_gate[0], w_up[0], w_down[0])

    tt = LOCAL_TILE
    out = pl.pallas_call(
        _combine_kernel,
        grid_spec=pltpu.PrefetchScalarGridSpec(
            num_scalar_prefetch=3,
            grid=(n_local,),
            in_specs=[
                pl.BlockSpec((tt, d), lambda i, *_: (i, 0)),
                pl.BlockSpec(memory_space=pl.ANY),
            ],
            out_specs=pl.BlockSpec((tt, d), lambda i, *_: (i, 0)),
            scratch_shapes=[
                pltpu.VMEM((2, LT_MAX * Y_ROWS, LANES), F32),
                pltpu.VMEM((tt * Y_ROWS, LANES), F32),
                pltpu.SemaphoreType.DMA((2,)),
            ],
        ),
        out_shape=jax.ShapeDtypeStruct((n, d), F32),
        compiler_params=_cparams(),
        name="combine",
    )(gch * (CHUNK * Y_ROWS), lp * Y_ROWS, gates, h, ys)
    return out.reshape(bsz, seq, d)
```

```python
import functools

import jax
import jax.numpy as jnp
from jax import lax
from jax.experimental import pallas as pl
from jax.experimental.pallas import tpu as pltpu

F32 = jnp.float32
BF16 = jnp.bfloat16

D_MODEL = 2048
N_HEADS = 16
N_KV_HEADS = 4
HEAD_DIM = 64
GROUP = N_HEADS // N_KV_HEADS
ROT_DIM = HEAD_DIM // 4
ROPE_THETA = 500000.0
WINDOW = 128
BLOCK = 128
ATTN_WIDTH = N_HEADS * HEAD_DIM
KV_WIDTH = N_KV_HEADS * HEAD_DIM
POOL_WINDOWS = (2, 4, 8, 16)
POOL_WIDTH = D_MODEL // 2
POOL_GROUP_DIM = POOL_WIDTH // len(POOL_WINDOWS)
N_EXPERT_GROUPS = 4
EXPERTS_PER_GROUP = 8
N_EXPERTS = N_EXPERT_GROUPS * EXPERTS_PER_GROUP
TOP_K = 2
D_EXPERT = 512
EPS = 1e-6

LANES = 128
SUBLANES = 8
MXU_DIM = 256
HALO = 16
NEG_BIG = -1e30
HALF = D_MODEL // 2
PK_ROWS = HALF // LANES
Y_ROWS = D_MODEL // LANES

ROW_TILE = 512
SUB_TILE = 256
EXPERT_TILE = 256
LOCAL_TILE = 512
CHUNK = 8
LT_MAX = TOP_K * LOCAL_TILE + N_EXPERTS * CHUNK
VMEM_LIMIT = 56 * 1024 * 1024


def _cparams(n_axes=1):
    return pltpu.CompilerParams(
        dimension_semantics=("arbitrary",) * n_axes,
        vmem_limit_bytes=VMEM_LIMIT,
    )


def _pack_halves(t):
    t32 = t.astype(BF16).astype(F32)
    lo_bits = lax.shift_right_logical(
        pltpu.bitcast(t32[:, :HALF], jnp.uint32), jnp.uint32(16))
    hi_bits = pltpu.bitcast(t32[:, HALF:], jnp.uint32) & jnp.uint32(0xFFFF0000)
    return lo_bits | hi_bits


def _unpack_halves(w):
    lo = pltpu.bitcast(lax.shift_left(w, jnp.uint32(16)), F32)
    hi = pltpu.bitcast(w & jnp.uint32(0xFFFF0000), F32)
    return lo, hi


def _head_sumsq(t, e_ref):
    t2 = t * t
    hi = t2.astype(BF16)
    lo = (t2 - hi.astype(F32)).astype(BF16)
    e = e_ref[...]
    outs = []
    for c in range(t.shape[1] // MXU_DIM):
        sl = slice(c * MXU_DIM, (c + 1) * MXU_DIM)
        outs.append(jnp.dot(hi[:, sl], e, preferred_element_type=F32)
                    + jnp.dot(lo[:, sl], e, preferred_element_type=F32))
    return outs[0] if len(outs) == 1 else jnp.concatenate(outs, axis=1)


def _norm_rope(t, gain, e_ref, cc, s_up, s_dn, out_scale):
    w = t.shape[1]
    ss = _head_sumsq(t, e_ref)
    tn = t * lax.rsqrt(ss * (1.0 / HEAD_DIM) + EPS) * gain
    reps = w // LANES
    up = pltpu.roll(tn, w - ROT_DIM // 2, axis=1)
    dn = pltpu.roll(tn, ROT_DIM // 2, axis=1)
    c = jnp.concatenate([cc] * reps, axis=1)
    su = jnp.concatenate([s_up] * reps, axis=1)
    sd = jnp.concatenate([s_dn] * reps, axis=1)
    return (tn * c + up * su + dn * sd) * out_scale


def _inproj_kernel(x_ref, g_ref, wq_ref, wk_ref, wv_ref, wu_ref, qn_ref, kn_ref,
                   cc_ref, su_ref, sd_ref, e_ref, wpool_ref, pscale_ref,
                   q_out, k_out, v_out, y_out, ubuf, *, tiles_per_seq):
    i = pl.program_id(0)
    tm = x_ref.shape[0]

    @pl.when(i % tiles_per_seq == 0)
    def _():
        ubuf[0:HALO, :] = jnp.zeros((HALO, POOL_WIDTH), F32)

    for r0 in range(0, tm, SUB_TILE):
        rows = slice(r0, r0 + SUB_TILE)
        x = x_ref[rows, :]
        ms = jnp.mean(x * x, axis=-1, keepdims=True)
        hn = (x * lax.rsqrt(ms + EPS) * g_ref[...]).astype(BF16)

        cc, s_up, s_dn = cc_ref[rows, :], su_ref[rows, :], sd_ref[rows, :]
        q = jnp.dot(hn, wq_ref[...], preferred_element_type=F32)
        q_out[rows, :] = _norm_rope(q, qn_ref[...], e_ref, cc, s_up, s_dn,
                                    HEAD_DIM ** -0.5).astype(BF16)
        k = jnp.dot(hn, wk_ref[...], preferred_element_type=F32)
        k_out[rows, :] = _norm_rope(k, kn_ref[...], e_ref, cc, s_up, s_dn,
                                    1.0).astype(BF16)
        v_out[rows, :] = jnp.dot(hn, wv_ref[...], preferred_element_type=F32).astype(BF16)

        base = HALO + r0
        ubuf[base:base + SUB_TILE, :] = jnp.dot(hn, wu_ref[...], preferred_element_type=F32)
        pos = (i % tiles_per_seq) * tm + r0 + lax.broadcasted_iota(
            jnp.int32, (SUB_TILE, POOL_GROUP_DIM), 0)
        for g, w in enumerate(POOL_WINDOWS):
            cols = slice(g * POOL_GROUP_DIM, (g + 1) * POOL_GROUP_DIM)
            u_g = ubuf[base:base + SUB_TILE, cols]
            acc = u_g
            for j in range(1, w):
                acc = acc + ubuf[base - j:base - j + SUB_TILE, cols]
            cnt = jnp.minimum(pos + 1, w).astype(F32)
            d = (acc / cnt - u_g).astype(BF16)
            y = jnp.dot(d, wpool_ref[g], preferred_element_type=F32)
            y_out[rows, cols] = (y * pscale_ref[:, cols]).astype(BF16)
    ubuf[0:HALO, :] = ubuf[tm:tm + HALO, :]


def _attn_kernel(sinks_ref, q_ref, kc_ref, kp_ref, vc_ref, vp_ref, o_ref):
    j = pl.program_id(1)
    kband = jnp.concatenate([kp_ref[...], kc_ref[...]], axis=0)
    vband = jnp.concatenate([vp_ref[...], vc_ref[...]], axis=0)
    row = lax.broadcasted_iota(jnp.int32, (BLOCK, 2 * BLOCK), 0)
    col = lax.broadcasted_iota(jnp.int32, (BLOCK, 2 * BLOCK), 1)
    diff = row + BLOCK - col
    first = jnp.where(j > 0, 0, BLOCK)
    valid = (diff >= 0) & (diff < WINDOW) & (col >= first)
    for g in range(N_KV_HEADS):
        kg = kband[:, g * HEAD_DIM:(g + 1) * HEAD_DIM]
        vg = vband[:, g * HEAD_DIM:(g + 1) * HEAD_DIM]
        for hh in range(GROUP):
            h = g * GROUP + hh
            qh = q_ref[:, h * HEAD_DIM:(h + 1) * HEAD_DIM]
            s = lax.dot_general(qh, kg, (((1,), (1,)), ((), ())),
                                preferred_element_type=F32)
            s = jnp.where(valid, s, NEG_BIG)
            sink = sinks_ref[h]
            m = jnp.maximum(jnp.max(s, axis=-1, keepdims=True), sink)
            p = jnp.exp(s - m)
            denom = jnp.sum(p, axis=-1, keepdims=True) + jnp.exp(sink - m)
            o = jnp.dot(p.astype(BF16), vg, preferred_element_type=F32)
            o_ref[:, h * HEAD_DIM:(h + 1) * HEAD_DIM] = (o / denom).astype(BF16)


def _outproj_router_kernel(o_ref, y_ref, x_ref, woa_ref, wop_ref, g_ref, wr_ref,
                           br_ref, tri_ref, h_out, hpk_out, route_out, cnt_out):
    for sub in range(x_ref.shape[0] // SUB_TILE):
        _outproj_router_subtile(sub, o_ref, y_ref, x_ref, woa_ref, wop_ref, g_ref, wr_ref,
                                br_ref, tri_ref, h_out, hpk_out, route_out, cnt_out)


def _outproj_router_subtile(sub, o_ref, y_ref, x_ref, woa_ref, wop_ref, g_ref, wr_ref,
                            br_ref, tri_ref, h_out, hpk_out, route_out, cnt_out):
    tm = SUB_TILE
    r0 = sub * SUB_TILE
    rows = slice(r0, r0 + SUB_TILE)
    h = (x_ref[rows, :]
         + jnp.dot(o_ref[rows, :], woa_ref[...], preferred_element_type=F32)
         + jnp.dot(y_ref[rows, :], wop_ref[...], preferred_element_type=F32))
    h_out[rows, :] = h
    ms = jnp.mean(h * h, axis=-1, keepdims=True)
    hn = h * lax.rsqrt(ms + EPS) * g_ref[...]
    hb = hn.astype(BF16)
    hb32 = hb.astype(F32)
    packed = _pack_halves(hn)
    for s in range(PK_ROWS):
        hpk_out[pl.ds(r0 * PK_ROWS + s, tm, stride=PK_ROWS), :] = (
            packed[:, s * LANES:(s + 1) * LANES])

    hl = (hn - hb32).astype(BF16)
    both = jnp.dot(hb, wr_ref[...], preferred_element_type=F32)
    logits = (both[:, :LANES] + both[:, LANES:]
              + jnp.dot(hl, wr_ref[:, :LANES], preferred_element_type=F32)
              + br_ref[...])
    lane = lax.broadcasted_iota(jnp.int32, (tm, LANES), 1)
    lane_f = lane.astype(F32)

    def first_lane_of(mask):
        return jnp.min(jnp.where(mask, lane_f, float(LANES)), axis=-1, keepdims=True)

    coarse = jnp.where(lane < N_EXPERT_GROUPS, logits, NEG_BIG)
    cmax = jnp.max(coarse, axis=-1, keepdims=True)
    grp = first_lane_of(coarse == cmax)
    p_grp = 1.0 / jnp.sum(jnp.exp(coarse - cmax), axis=-1, keepdims=True)

    flo = N_EXPERT_GROUPS + EXPERTS_PER_GROUP * grp
    fmask = (lane_f >= flo) & (lane_f < flo + EXPERTS_PER_GROUP)
    fsel = jnp.where(fmask, logits, NEG_BIG)
    fmax = jnp.max(fsel, axis=-1, keepdims=True)
    fe = jnp.exp(fsel - fmax)
    p = fe / jnp.sum(fe, axis=-1, keepdims=True)
    pc = jnp.where(fmask, p, -1.0)
    t1 = jnp.max(pc, axis=-1, keepdims=True)
    i1 = first_lane_of(pc == t1)
    pc2 = jnp.where(lane_f == i1, -1.0, pc)
    t2 = jnp.max(pc2, axis=-1, keepdims=True)
    i2 = first_lane_of(pc2 == t2)
    tsum = t1 + t2
    g1 = p_grp * t1 / tsum
    g2 = p_grp * t2 / tsum
    e1 = i1 - N_EXPERT_GROUPS
    e2 = i2 - N_EXPERT_GROUPS

    oh1 = lane_f == e1
    oh2 = lane_f == e2
    oh = jnp.where(oh1, 1.0, 0.0) + jnp.where(oh2, 1.0, 0.0)
    prefix = jnp.dot(tri_ref[...], oh.astype(BF16), preferred_element_type=F32)
    r1 = jnp.sum(jnp.where(oh1, prefix, 0.0), axis=-1, keepdims=True)
    r2 = jnp.sum(jnp.where(oh2, prefix, 0.0), axis=-1, keepdims=True)
    route_out[rows, :] = jnp.where(
        lane == 0, e1,
        jnp.where(lane == 1, e2,
                  jnp.where(lane == 2, g1,
                            jnp.where(lane == 3, g2,
                                      jnp.where(lane == 4, r1,
                                                jnp.where(lane == 5, r2, 0.0))))))
    cnt_out[sub * SUBLANES:(sub + 1) * SUBLANES, :] = jnp.broadcast_to(
        jnp.sum(oh, axis=0, keepdims=True), (SUBLANES, LANES))


def _expert_kernel(eid_ref, first_ref, nact_ref, off_ref,
                   hpk_hbm, wg_ref, wu_ref, wd_ref, ys_out,
                   xbuf, wg_s, wu_s, wd_s, sems):
    i = pl.program_id(0)
    tm = ys_out.shape[0] // PK_ROWS
    n_act = nact_ref[0]

    ch_rows = CHUNK * PK_ROWS
    n_ch = tm // CHUNK

    def gather(blk, slot):
        for c in range(n_ch):
            src = pl.multiple_of(off_ref[blk * n_ch + c], ch_rows)
            pltpu.make_async_copy(hpk_hbm.at[pl.ds(src, ch_rows)],
                                  xbuf.at[slot, pl.ds(c * ch_rows, ch_rows)],
                                  sems.at[slot]).start()

    @pl.when(i == 0)
    def _():
        gather(0, 0)

    @pl.when(i + 1 < n_act)
    def _():
        gather(i + 1, (i + 1) % 2)

    @pl.when(i < n_act)
    def _():
        slot = i % 2

        @pl.when(first_ref[i] == 1)
        def _():
            wg_s[...] = wg_ref[0].astype(BF16)
            wu_s[...] = wu_ref[0].astype(BF16)
            wd_s[...] = wd_ref[0].astype(BF16)

        pltpu.make_async_copy(hpk_hbm.at[pl.ds(0, tm * PK_ROWS)], xbuf.at[slot],
                              sems.at[slot]).wait()
        xu = jnp.concatenate(
            [xbuf[slot, pl.ds(s, tm, stride=PK_ROWS), :] for s in range(PK_ROWS)],
            axis=1)
        x_lo, x_hi = (v.astype(BF16) for v in _unpack_halves(xu))
        a = (jnp.dot(x_lo, wg_s[0:HALF, :], preferred_element_type=F32)
             + jnp.dot(x_hi, wg_s[HALF:, :], preferred_element_type=F32))
        b = (jnp.dot(x_lo, wu_s[0:HALF, :], preferred_element_type=F32)
             + jnp.dot(x_hi, wu_s[HALF:, :], preferred_element_type=F32))
        mid = (a * jax.nn.sigmoid(a) * b).astype(BF16)
        y = jnp.dot(mid, wd_s[...], preferred_element_type=F32)
        yw = _pack_halves(y)
        for s in range(PK_ROWS):
            ys_out[pl.ds(s, tm, stride=PK_ROWS), :] = yw[:, s * LANES:(s + 1) * LANES]

    @pl.when(i >= n_act)
    def _():
        ys_out[...] = jnp.zeros(ys_out.shape, ys_out.dtype)


def _local_sort_kernel(lp_ref, hpk_ref, xs_out):
    i = pl.program_id(0)
    t_tile = hpk_ref.shape[0] // PK_ROWS
    xs_out[...] = jnp.zeros(xs_out.shape, xs_out.dtype)

    def body(q, carry):
        for u in range(SUBLANES):
            t = q * SUBLANES + u
            row = hpk_ref[pl.ds(pl.multiple_of(t * PK_ROWS, PK_ROWS), PK_ROWS), :]
            for k in range(TOP_K):
                dst = pl.multiple_of(lp_ref[(i * t_tile + t) * TOP_K + k], PK_ROWS)
                xs_out[pl.ds(dst, PK_ROWS), :] = row
        return carry
    lax.fori_loop(0, t_tile // SUBLANES, body, 0)


def _combine_kernel(gch_ref, lp_ref, gate_ref, h_ref, ys_hbm, out_ref, ylocal, acc, sems):
    i = pl.program_id(0)
    n = pl.num_programs(0)
    t_tile = h_ref.shape[0]
    ch_rows = CHUNK * PK_ROWS
    n_ch = ylocal.shape[1] // ch_rows

    def gather(tile, slot):
        def body(q, carry):
            for u in range(SUBLANES):
                lc = q * SUBLANES + u
                src = pl.multiple_of(gch_ref[tile * n_ch + lc], ch_rows)
                dst = pl.multiple_of(lc * ch_rows, ch_rows)
                pltpu.make_async_copy(ys_hbm.at[pl.ds(src, ch_rows)],
                                      ylocal.at[slot, pl.ds(dst, ch_rows)],
                                      sems.at[slot]).start()
            return carry
        lax.fori_loop(0, n_ch // SUBLANES, body, 0)

    @pl.when(i == 0)
    def _():
        gather(0, 0)

    @pl.when(i + 1 < n)
    def _():
        gather(i + 1, (i + 1) % 2)

    slot = i % 2
    pltpu.make_async_copy(ys_hbm.at[pl.ds(0, n_ch * ch_rows)], ylocal.at[slot],
                          sems.at[slot]).wait()

    def tok_body(q, carry):
        for u in range(4):
            t = q * 4 + u
            a0 = (i * t_tile + t) * TOP_K
            p0 = pl.multiple_of(lp_ref[a0], PK_ROWS)
            p1 = pl.multiple_of(lp_ref[a0 + 1], PK_ROWS)
            lo0, hi0 = _unpack_halves(ylocal[slot, pl.ds(p0, PK_ROWS), :])
            lo1, hi1 = _unpack_halves(ylocal[slot, pl.ds(p1, PK_ROWS), :])
            g0 = gate_ref[a0]
            g1 = gate_ref[a0 + 1]
            dst = pl.multiple_of(t * Y_ROWS, Y_ROWS)
            acc[pl.ds(dst, PK_ROWS), :] = g0 * lo0 + g1 * lo1
            acc[pl.ds(dst + PK_ROWS, PK_ROWS), :] = g0 * hi0 + g1 * hi1
        return carry
    lax.fori_loop(0, t_tile // 4, tok_body, 0)

    for s in range(Y_ROWS):
        cols = slice(s * LANES, (s + 1) * LANES)
        out_ref[:, cols] = h_ref[:, cols] + acc[pl.ds(s, t_tile, stride=Y_ROWS), :]


def _rope_tables(seq):
    pos = jnp.arange(seq, dtype=F32)
    inv_freq = ROPE_THETA ** (-jnp.arange(0, ROT_DIM, 2, dtype=F32) / ROT_DIM)
    ang = pos[:, None] * inv_freq[None, :]
    cos, sin = jnp.cos(ang), jnp.sin(ang)
    half = ROT_DIM // 2
    ones = jnp.ones((seq, HEAD_DIM - ROT_DIM), F32)
    zeros = jnp.zeros((seq, HEAD_DIM - ROT_DIM), F32)
    zh = jnp.zeros((seq, half), F32)
    cc = jnp.concatenate([cos, cos, ones], axis=1)
    s_up = jnp.concatenate([-sin, zh, zeros], axis=1)
    s_dn = jnp.concatenate([zh, sin, zeros], axis=1)
    rep = LANES // HEAD_DIM
    return tuple(jnp.tile(t, (1, rep)) for t in (cc, s_up, s_dn))


def _plan(eid, rank, cnt_rt, n_tokens):
    sub = LOCAL_TILE // SUB_TILE
    nt = n_tokens // LOCAL_TILE
    experts = jnp.arange(N_EXPERTS, dtype=jnp.int32)
    cnt_sub = cnt_rt.reshape(nt, sub, N_EXPERTS)
    sub_off = jnp.cumsum(cnt_sub, axis=1) - cnt_sub
    cnt = jnp.sum(cnt_sub, axis=1)
    cntp = ((cnt + CHUNK - 1) // CHUNK) * CHUNK
    loff_end = jnp.cumsum(cntp, axis=1)
    loff = loff_end - cntp
    rows_e = jnp.sum(cntp, axis=0)
    rows_pad = ((rows_e + EXPERT_TILE - 1) // EXPERT_TILE) * EXPERT_TILE
    g_end = jnp.cumsum(rows_pad)
    g_start = g_end - rows_pad
    gpos = g_start[None, :] + jnp.cumsum(cntp, axis=0) - cntp

    onehot = eid.reshape(nt, sub, SUB_TILE * TOP_K, 1) == experts
    table = (loff[:, None, :] + sub_off)[:, :, None, :]
    lp = (jnp.sum(jnp.where(onehot, table, 0), axis=-1).reshape(-1)
          + rank.reshape(-1)).astype(jnp.int32)

    lrow = jnp.arange(LT_MAX // CHUNK, dtype=jnp.int32) * CHUNK
    e_lc = jnp.sum((lrow[None, :, None] >= loff_end[:, None, :]).astype(jnp.int32), axis=-1)
    sel = e_lc[:, :, None] == experts
    delta = jnp.sum(jnp.where(sel, (gpos - loff)[:, None, :], 0), axis=-1)
    gch = jnp.where(e_lc < N_EXPERTS, (delta + lrow[None, :]) // CHUNK, 0).reshape(-1)

    n_rows = TOP_K * n_tokens + nt * N_EXPERTS * (CHUNK - 1) + N_EXPERTS * (EXPERT_TILE - CHUNK)
    n_blocks = -(-n_rows // EXPERT_TILE)
    n_rows = n_blocks * EXPERT_TILE
    run_start = (gpos.T.reshape(-1) // CHUNK).astype(F32)
    run_len = (cntp.T.reshape(-1) // CHUNK).astype(F32)
    tile_base = jnp.arange(nt, dtype=jnp.int32)[:, None] * (LT_MAX // CHUNK)
    run_src = ((tile_base + loff // CHUNK).T.reshape(-1)).astype(F32)
    tabs = jnp.stack([run_start, run_len, run_src], axis=1)
    dtabs = tabs - jnp.concatenate([jnp.zeros((1, 3), F32), tabs[:-1]], axis=0)
    gc = jnp.arange(n_rows // CHUNK, dtype=jnp.int32)
    started = (run_start[None, :] <= gc[:, None].astype(F32)).astype(F32)
    picked = jnp.dot(started, dtabs, precision=lax.Precision.HIGHEST).astype(jnp.int32)
    within = gc - picked[:, 0]
    csrc = jnp.where(within < picked[:, 1], picked[:, 2] + within, 0)

    blk_start = jnp.arange(n_blocks, dtype=jnp.int32) * EXPERT_TILE
    blk_eid = jnp.minimum(
        jnp.sum((blk_start[:, None] >= g_end[None, :]).astype(jnp.int32), axis=-1),
        N_EXPERTS - 1)
    prev = jnp.concatenate([jnp.full((1,), -1, jnp.int32), blk_eid[:-1]])
    first = (blk_eid != prev).astype(jnp.int32)
    n_act = (g_end[-1] // EXPERT_TILE).astype(jnp.int32).reshape(1)
    return lp, gch.astype(jnp.int32), csrc.astype(jnp.int32), blk_eid, first, n_act, n_rows


def kernel(x, norm_mix, w_in, q_norm, k_norm, sinks, w_pool, pool_scale, w_out,
           norm_ffn, w_coarse, b_coarse, w_fine, b_fine, w_gate, w_up, w_down):
    bsz, seq, d = x.shape
    n = bsz * seq
    assert d == D_MODEL and seq % ROW_TILE == 0 and seq % BLOCK == 0
    assert norm_mix.shape[0] == 1, "single-layer problem"
    xf = x.reshape(n, d)

    w_in_b = w_in[0].astype(BF16)
    wq = w_in_b[:, :ATTN_WIDTH]
    wk = w_in_b[:, ATTN_WIDTH:ATTN_WIDTH + KV_WIDTH]
    wv = w_in_b[:, ATTN_WIDTH + KV_WIDTH:ATTN_WIDTH + 2 * KV_WIDTH]
    wu = w_in_b[:, ATTN_WIDTH + 2 * KV_WIDTH:]
    qn = jnp.tile(q_norm[0], N_HEADS).reshape(1, ATTN_WIDTH)
    kn = jnp.tile(k_norm[0], N_KV_HEADS).reshape(1, KV_WIDTH)
    cc, s_up, s_dn = _rope_tables(seq)
    lane_head = jnp.arange(MXU_DIM) // HEAD_DIM
    e_mat = (lane_head[:, None] == lane_head[None, :]).astype(BF16)
    w_pool_b = w_pool[0].astype(BF16)
    pscale = pool_scale[0].reshape(1, POOL_WIDTH)
    w_out_b = w_out[0].astype(BF16)
    wo_attn, wo_pool = w_out_b[:ATTN_WIDTH], w_out_b[ATTN_WIDTH:]
    n_router = N_EXPERT_GROUPS + N_EXPERTS
    w_r = jnp.concatenate([w_coarse[0], w_fine[0]], axis=1)
    w_r = jnp.pad(w_r, ((0, 0), (0, LANES - n_router)))
    w_r_hi = w_r.astype(BF16)
    w_r_lo = (w_r - w_r_hi.astype(F32)).astype(BF16)
    w_r2 = jnp.concatenate([w_r_hi, w_r_lo], axis=1)
    b_r = jnp.pad(jnp.concatenate([b_coarse[0], b_fine[0]]),
                  (0, LANES - n_router)).reshape(1, LANES)

    tm = ROW_TILE
    tiles_per_seq = seq // tm
    n_tiles = n // tm
    n_sub = n // SUB_TILE
    idx = jnp.arange(SUB_TILE)
    tri = (idx[None, :] < idx[:, None]).astype(BF16)
    const = lambda *_: (0, 0)
    row_blk = lambda i: (i, 0)

    q, k, v, y = pl.pallas_call(
        functools.partial(_inproj_kernel, tiles_per_seq=tiles_per_seq),
        grid=(n_tiles,),
        in_specs=[
            pl.BlockSpec((tm, d), row_blk),
            pl.BlockSpec((1, d), const),
            pl.BlockSpec((d, ATTN_WIDTH), const),
            pl.BlockSpec((d, KV_WIDTH), const),
            pl.BlockSpec((d, KV_WIDTH), const),
            pl.BlockSpec((d, POOL_WIDTH), const),
            pl.BlockSpec((1, ATTN_WIDTH), const),
            pl.BlockSpec((1, KV_WIDTH), const),
            pl.BlockSpec((tm, LANES), lambda i: (i % tiles_per_seq, 0)),
            pl.BlockSpec((tm, LANES), lambda i: (i % tiles_per_seq, 0)),
            pl.BlockSpec((tm, LANES), lambda i: (i % tiles_per_seq, 0)),
            pl.BlockSpec((MXU_DIM, MXU_DIM), const),
            pl.BlockSpec((len(POOL_WINDOWS), POOL_GROUP_DIM, POOL_GROUP_DIM),
                         lambda i: (0, 0, 0)),
            pl.BlockSpec((1, POOL_WIDTH), const),
        ],
        out_specs=[
            pl.BlockSpec((tm, ATTN_WIDTH), row_blk),
            pl.BlockSpec((tm, KV_WIDTH), row_blk),
            pl.BlockSpec((tm, KV_WIDTH), row_blk),
            pl.BlockSpec((tm, POOL_WIDTH), row_blk),
        ],
        out_shape=[
            jax.ShapeDtypeStruct((n, ATTN_WIDTH), BF16),
            jax.ShapeDtypeStruct((n, KV_WIDTH), BF16),
            jax.ShapeDtypeStruct((n, KV_WIDTH), BF16),
            jax.ShapeDtypeStruct((n, POOL_WIDTH), BF16),
        ],
        scratch_shapes=[pltpu.VMEM((HALO + tm, POOL_WIDTH), F32)],
        compiler_params=_cparams(),
        name="inproj",
    )(xf, norm_mix[0].reshape(1, d), wq, wk, wv, wu, qn, kn, cc, s_up, s_dn,
      e_mat, w_pool_b, pscale)

    nb = seq // BLOCK
    cur = lambda b, j, *_: (b * nb + j, 0)
    prv = lambda b, j, *_: (b * nb + jnp.maximum(j - 1, 0), 0)
    attn = pl.pallas_call(
        _attn_kernel,
        grid_spec=pltpu.PrefetchScalarGridSpec(
            num_scalar_prefetch=1,
            grid=(bsz, nb),
            in_specs=[
                pl.BlockSpec((BLOCK, ATTN_WIDTH), cur),
                pl.BlockSpec((BLOCK, KV_WIDTH), cur),
                pl.BlockSpec((BLOCK, KV_WIDTH), prv),
                pl.BlockSpec((BLOCK, KV_WIDTH), cur),
                pl.BlockSpec((BLOCK, KV_WIDTH), prv),
            ],
            out_specs=pl.BlockSpec((BLOCK, ATTN_WIDTH), cur),
        ),
        out_shape=jax.ShapeDtypeStruct((n, ATTN_WIDTH), BF16),
        compiler_params=_cparams(2),
        name="swa_attn",
    )(sinks[0].astype(F32), q, k, k, v, v)

    h, hpk, route, cnt = pl.pallas_call(
        _outproj_router_kernel,
        grid=(n_tiles,),
        in_specs=[
            pl.BlockSpec((tm, ATTN_WIDTH), row_blk),
            pl.BlockSpec((tm, POOL_WIDTH), row_blk),
            pl.BlockSpec((tm, d), row_blk),
            pl.BlockSpec((ATTN_WIDTH, d), const),
            pl.BlockSpec((POOL_WIDTH, d), const),
            pl.BlockSpec((1, d), const),
            pl.BlockSpec((d, 2 * LANES), const),
            pl.BlockSpec((1, LANES), const),
            pl.BlockSpec((SUB_TILE, SUB_TILE), const),
        ],
        out_specs=[
            pl.BlockSpec((tm, d), row_blk),
            pl.BlockSpec((tm * PK_ROWS, LANES), row_blk),
            pl.BlockSpec((tm, LANES), row_blk),
            pl.BlockSpec((tm // SUB_TILE * SUBLANES, LANES), row_blk),
        ],
        out_shape=[
            jax.ShapeDtypeStruct((n, d), F32),
            jax.ShapeDtypeStruct((n * PK_ROWS, LANES), jnp.uint32),
            jax.ShapeDtypeStruct((n, LANES), F32),
            jax.ShapeDtypeStruct((n_sub * SUBLANES, LANES), F32),
        ],
        compiler_params=_cparams(),
        name="outproj_router",
    )(attn, y, xf, wo_attn, wo_pool, norm_ffn[0].reshape(1, d), w_r2, b_r, tri)

    eid = route[:, :TOP_K].astype(jnp.int32)
    rank = route[:, 4:4 + TOP_K].astype(jnp.int32)
    tile_cnt = cnt.reshape(n_sub, SUBLANES, LANES)[:, 0, :N_EXPERTS].astype(jnp.int32)
    te = EXPERT_TILE
    lp, gch, csrc, blk_eid, first, n_act, n_rows = _plan(eid, rank, tile_cnt, n)
    n_blocks = n_rows // te
    n_local = n // LOCAL_TILE
    gates = route[:, 2:2 + TOP_K].reshape(n * TOP_K)

    xs = pl.pallas_call(
        _local_sort_kernel,
        grid_spec=pltpu.PrefetchScalarGridSpec(
            num_scalar_prefetch=1,
            grid=(n_local,),
            in_specs=[pl.BlockSpec((LOCAL_TILE * PK_ROWS, LANES), lambda i, *_: (i, 0))],
            out_specs=pl.BlockSpec((LT_MAX * PK_ROWS, LANES), lambda i, *_: (i, 0)),
        ),
        out_shape=jax.ShapeDtypeStruct((n_local * LT_MAX * PK_ROWS, LANES), jnp.uint32),
        compiler_params=_cparams(),
        name="local_sort",
    )(lp * PK_ROWS, hpk)

    w_idx = lambda i, eid_r, *_: (eid_r[i], 0, 0)
    ys = pl.pallas_call(
        _expert_kernel,
        grid_spec=pltpu.PrefetchScalarGridSpec(
            num_scalar_prefetch=4,
            grid=(n_blocks,),
            in_specs=[
                pl.BlockSpec(memory_space=pl.ANY),
                pl.BlockSpec((1, d, D_EXPERT), w_idx),
                pl.BlockSpec((1, d, D_EXPERT), w_idx),
                pl.BlockSpec((1, D_EXPERT, d), w_idx),
            ],
            out_specs=pl.BlockSpec((te * PK_ROWS, LANES), lambda i, *_: (i, 0)),
            scratch_shapes=[
                pltpu.VMEM((2, te * PK_ROWS, LANES), jnp.uint32),
                pltpu.VMEM((d, D_EXPERT), BF16),
                pltpu.VMEM((d, D_EXPERT), BF16),
                pltpu.VMEM((D_EXPERT, d), BF16),
                pltpu.SemaphoreType.DMA((2,)),
            ],
        ),
        out_shape=jax.ShapeDtypeStruct((n_rows * PK_ROWS, LANES), jnp.uint32),
        compiler_params=_cparams(),
        name="experts",
    )(blk_eid, first, n_act, csrc * (CHUNK * PK_ROWS), xs, w_gate[0], w_up[0], w_down[0])

    tt = LOCAL_TILE
    out = pl.pallas_call(
        _combine_kernel,
        grid_spec=pltpu.PrefetchScalarGridSpec(
            num_scalar_prefetch=3,
            grid=(n_local,),
            in_specs=[
                pl.BlockSpec((tt, d), lambda i, *_: (i, 0)),
                pl.BlockSpec(memory_space=pl.ANY),
            ],
            out_specs=pl.BlockSpec((tt, d), lambda i, *_: (i, 0)),
            scratch_shapes=[
                pltpu.VMEM((2, LT_MAX * PK_ROWS, LANES), jnp.uint32),
                pltpu.VMEM((tt * Y_ROWS, LANES), F32),
                pltpu.SemaphoreType.DMA((2,)),
            ],
        ),
        out_shape=jax.ShapeDtypeStruct((n, d), F32),
        compiler_params=_cparams(),
        name="combine",
    )(gch * (CHUNK * PK_ROWS), lp * PK_ROWS, gates, h, ys)
    return out.reshape(bsz, seq, d)
```

```python
import functools

import jax
import jax.numpy as jnp
from jax import lax
from jax.experimental import pallas as pl
from jax.experimental.pallas import tpu as pltpu

F32 = jnp.float32
BF16 = jnp.bfloat16

D_MODEL = 2048
N_HEADS = 16
N_KV_HEADS = 4
HEAD_DIM = 64
GROUP = N_HEADS // N_KV_HEADS
ROT_DIM = HEAD_DIM // 4
ROPE_THETA = 500000.0
WINDOW = 128
BLOCK = 128
Q_BLOCKS = 2
ATTN_WIDTH = N_HEADS * HEAD_DIM
KV_WIDTH = N_KV_HEADS * HEAD_DIM
POOL_WINDOWS = (2, 4, 8, 16)
POOL_WIDTH = D_MODEL // 2
POOL_GROUP_DIM = POOL_WIDTH // len(POOL_WINDOWS)
N_EXPERT_GROUPS = 4
EXPERTS_PER_GROUP = 8
N_EXPERTS = N_EXPERT_GROUPS * EXPERTS_PER_GROUP
TOP_K = 2
D_EXPERT = 512
EPS = 1e-6

LANES = 128
SUBLANES = 8
MXU_DIM = 256
HALO = 16
NEG_BIG = -1e30
HALF = D_MODEL // 2
PK_ROWS = HALF // LANES
Y_ROWS = D_MODEL // LANES

ROW_TILE = 512
SUB_TILE = 256
EXPERT_TILE = 256
LOCAL_TILE = 512
CHUNK = 8
LT_MAX = TOP_K * LOCAL_TILE + N_EXPERTS * CHUNK
VMEM_LIMIT = 56 * 1024 * 1024


def _cparams(n_axes=1):
    return pltpu.CompilerParams(
        dimension_semantics=("arbitrary",) * n_axes,
        vmem_limit_bytes=VMEM_LIMIT,
    )


def _pack_halves(t):
    t32 = t.astype(BF16).astype(F32)
    lo_bits = lax.shift_right_logical(
        pltpu.bitcast(t32[:, :HALF], jnp.uint32), jnp.uint32(16))
    hi_bits = pltpu.bitcast(t32[:, HALF:], jnp.uint32) & jnp.uint32(0xFFFF0000)
    return lo_bits | hi_bits


def _unpack_halves(w):
    lo = pltpu.bitcast(lax.shift_left(w, jnp.uint32(16)), F32)
    hi = pltpu.bitcast(w & jnp.uint32(0xFFFF0000), F32)
    return lo, hi


def _head_sumsq(t, e_ref):
    t2 = t * t
    hi = t2.astype(BF16)
    lo = (t2 - hi.astype(F32)).astype(BF16)
    e = e_ref[...]
    outs = []
    for c in range(t.shape[1] // MXU_DIM):
        sl = slice(c * MXU_DIM, (c + 1) * MXU_DIM)
        outs.append(jnp.dot(hi[:, sl], e, preferred_element_type=F32)
                    + jnp.dot(lo[:, sl], e, preferred_element_type=F32))
    return outs[0] if len(outs) == 1 else jnp.concatenate(outs, axis=1)


def _norm_rope(t, gain, e_ref, cc, s_up, s_dn, out_scale):
    w = t.shape[1]
    ss = _head_sumsq(t, e_ref)
    tn = t * lax.rsqrt(ss * (1.0 / HEAD_DIM) + EPS) * gain
    reps = w // LANES
    up = pltpu.roll(tn, w - ROT_DIM // 2, axis=1)
    dn = pltpu.roll(tn, ROT_DIM // 2, axis=1)
    c = jnp.concatenate([cc] * reps, axis=1)
    su = jnp.concatenate([s_up] * reps, axis=1)
    sd = jnp.concatenate([s_dn] * reps, axis=1)
    return (tn * c + up * su + dn * sd) * out_scale


def _swap_head_pairs(t):
    w = t.shape[1]
    lane = lax.broadcasted_iota(jnp.int32, t.shape, 1)
    from_up = pltpu.roll(t, w - HEAD_DIM, axis=1)
    from_dn = pltpu.roll(t, HEAD_DIM, axis=1)
    return jnp.where((lane & (LANES - 1)) < HEAD_DIM, from_up, from_dn)


def _inproj_kernel(x_ref, g_ref, wq_ref, wk_ref, wv_ref, wu_ref, qn_ref, kn_ref,
                   cc_ref, su_ref, sd_ref, e_ref, wpool_ref, pscale_ref,
                   q_out, k_out, v_out, ksw_out, vsw_out, y_out, ubuf, *, tiles_per_seq):
    i = pl.program_id(0)
    tm = x_ref.shape[0]

    @pl.when(i % tiles_per_seq == 0)
    def _():
        ubuf[0:HALO, :] = jnp.zeros((HALO, POOL_WIDTH), F32)

    for r0 in range(0, tm, SUB_TILE):
        rows = slice(r0, r0 + SUB_TILE)
        x = x_ref[rows, :]
        ms = jnp.mean(x * x, axis=-1, keepdims=True)
        hn = (x * lax.rsqrt(ms + EPS) * g_ref[...]).astype(BF16)

        cc, s_up, s_dn = cc_ref[rows, :], su_ref[rows, :], sd_ref[rows, :]
        q = jnp.dot(hn, wq_ref[...], preferred_element_type=F32)
        q_out[rows, :] = _norm_rope(q, qn_ref[...], e_ref, cc, s_up, s_dn,
                                    HEAD_DIM ** -0.5).astype(BF16)
        k = jnp.dot(hn, wk_ref[...], preferred_element_type=F32)
        k = _norm_rope(k, kn_ref[...], e_ref, cc, s_up, s_dn, 1.0)
        v = jnp.dot(hn, wv_ref[...], preferred_element_type=F32)
        k_out[rows, :] = k.astype(BF16)
        v_out[rows, :] = v.astype(BF16)
        ksw_out[rows, :] = _swap_head_pairs(k).astype(BF16)
        vsw_out[rows, :] = _swap_head_pairs(v).astype(BF16)

        base = HALO + r0
        ubuf[base:base + SUB_TILE, :] = jnp.dot(hn, wu_ref[...], preferred_element_type=F32)
        pos = (i % tiles_per_seq) * tm + r0 + lax.broadcasted_iota(
            jnp.int32, (SUB_TILE, POOL_GROUP_DIM), 0)
        for g, w in enumerate(POOL_WINDOWS):
            cols = slice(g * POOL_GROUP_DIM, (g + 1) * POOL_GROUP_DIM)
            u_g = ubuf[base:base + SUB_TILE, cols]
            acc = u_g
            for j in range(1, w):
                acc = acc + ubuf[base - j:base - j + SUB_TILE, cols]
            cnt = jnp.minimum(pos + 1, w).astype(F32)
            d = (acc / cnt - u_g).astype(BF16)
            y = jnp.dot(d, wpool_ref[g], preferred_element_type=F32)
            y_out[rows, cols] = (y * pscale_ref[:, cols]).astype(BF16)
    ubuf[0:HALO, :] = ubuf[tm:tm + HALO, :]


def _attn_kernel(sinks_ref, q_ref, kc_ref, kp_ref, kswc_ref, kswp_ref,
                 vc_ref, vp_ref, vswc_ref, vswp_ref, o_ref):
    j = pl.program_id(1)
    two = 2 * BLOCK
    row = lax.broadcasted_iota(jnp.int32, (two, two), 0) & (BLOCK - 1)
    col = lax.broadcasted_iota(jnp.int32, (two, two), 1)
    diff = row + BLOCK - col
    local = (diff >= 0) & (diff < WINDOW)
    lo_lanes = lax.broadcasted_iota(jnp.int32, (two, LANES), 1) < HEAD_DIM
    top_rows = lax.broadcasted_iota(jnp.int32, (two, 1), 0) < BLOCK
    zero = jnp.zeros((two, LANES), BF16)

    for sb in range(q_ref.shape[0] // BLOCK):
        rows = slice(sb * BLOCK, (sb + 1) * BLOCK)
        if sb == 0:
            def band(cur_ref, prev_ref):
                return jnp.concatenate([prev_ref[...], cur_ref[0:BLOCK, :]], axis=0)
            valid = local & ((col >= BLOCK) | (j > 0))
        else:
            def band(cur_ref, prev_ref, sb=sb):
                return cur_ref[(sb - 1) * BLOCK:(sb + 1) * BLOCK, :]
            valid = local
        k_nat, k_swp = band(kc_ref, kp_ref), band(kswc_ref, kswp_ref)
        v_nat, v_swp = band(vc_ref, vp_ref), band(vswc_ref, vswp_ref)

        for g in range(N_KV_HEADS):
            c0 = (g // 2) * LANES
            kcols = (k_nat[:, c0:c0 + LANES], k_swp[:, c0:c0 + LANES])
            vcols = (v_nat[:, c0:c0 + LANES], v_swp[:, c0:c0 + LANES])
            in_lo, in_hi = (0, 1) if g % 2 == 0 else (1, 0)
            q0 = g * GROUP * HEAD_DIM
            qq = jnp.concatenate([q_ref[rows, q0:q0 + LANES],
                                  q_ref[rows, q0 + LANES:q0 + 2 * LANES]], axis=0)
            halves = []
            for half, src in ((0, in_lo), (1, in_hi)):
                qm = jnp.where(lo_lanes, qq, zero) if half == 0 else jnp.where(lo_lanes, zero, qq)
                s = lax.dot_general(qm, kcols[src], (((1,), (1,)), ((), ())),
                                    preferred_element_type=F32)
                s = jnp.where(valid, s, NEG_BIG)
                h_top = g * GROUP + half
                sink = jnp.where(top_rows, sinks_ref[h_top], sinks_ref[h_top + 2])
                m = jnp.maximum(jnp.max(s, axis=-1, keepdims=True), sink)
                p = jnp.exp(s - m)
                denom = jnp.sum(p, axis=-1, keepdims=True) + jnp.exp(sink - m)
                o = jnp.dot(p.astype(BF16), vcols[src], preferred_element_type=F32)
                halves.append(o / denom)
            o_pair = jnp.where(lo_lanes, halves[0], halves[1]).astype(BF16)
            o_ref[rows, q0:q0 + LANES] = o_pair[0:BLOCK, :]
            o_ref[rows, q0 + LANES:q0 + 2 * LANES] = o_pair[BLOCK:, :]


def _outproj_router_kernel(o_ref, y_ref, x_ref, woa_ref, wop_ref, g_ref, wr_ref,
                           br_ref, tri_ref, h_out, hpk_out, route_out, cnt_out):
    for sub in range(x_ref.shape[0] // SUB_TILE):
        _outproj_router_subtile(sub, o_ref, y_ref, x_ref, woa_ref, wop_ref, g_ref, wr_ref,
                                br_ref, tri_ref, h_out, hpk_out, route_out, cnt_out)


def _outproj_router_subtile(sub, o_ref, y_ref, x_ref, woa_ref, wop_ref, g_ref, wr_ref,
                            br_ref, tri_ref, h_out, hpk_out, route_out, cnt_out):
    tm = SUB_TILE
    r0 = sub * SUB_TILE
    rows = slice(r0, r0 + SUB_TILE)
    h = (x_ref[rows, :]
         + jnp.dot(o_ref[rows, :], woa_ref[...], preferred_element_type=F32)
         + jnp.dot(y_ref[rows, :], wop_ref[...], preferred_element_type=F32))
    h_out[rows, :] = h
    ms = jnp.mean(h * h, axis=-1, keepdims=True)
    hn = h * lax.rsqrt(ms + EPS) * g_ref[...]
    hb = hn.astype(BF16)
    hb32 = hb.astype(F32)
    packed = _pack_halves(hn)
    for s in range(PK_ROWS):
        hpk_out[pl.ds(r0 * PK_ROWS + s, tm, stride=PK_ROWS), :] = (
            packed[:, s * LANES:(s + 1) * LANES])

    hl = (hn - hb32).astype(BF16)
    both = jnp.dot(hb, wr_ref[...], preferred_element_type=F32)
    logits = (both[:, :LANES] + both[:, LANES:]
              + jnp.dot(hl, wr_ref[:, :LANES], preferred_element_type=F32)
              + br_ref[...])
    lane = lax.broadcasted_iota(jnp.int32, (tm, LANES), 1)
    lane_f = lane.astype(F32)

    def first_lane_of(mask):
        return jnp.min(jnp.where(mask, lane_f, float(LANES)), axis=-1, keepdims=True)

    coarse = jnp.where(lane < N_EXPERT_GROUPS, logits, NEG_BIG)
    cmax = jnp.max(coarse, axis=-1, keepdims=True)
    grp = first_lane_of(coarse == cmax)
    p_grp = 1.0 / jnp.sum(jnp.exp(coarse - cmax), axis=-1, keepdims=True)

    flo = N_EXPERT_GROUPS + EXPERTS_PER_GROUP * grp
    fmask = (lane_f >= flo) & (lane_f < flo + EXPERTS_PER_GROUP)
    fsel = jnp.where(fmask, logits, NEG_BIG)
    fmax = jnp.max(fsel, axis=-1, keepdims=True)
    fe = jnp.exp(fsel - fmax)
    p = fe / jnp.sum(fe, axis=-1, keepdims=True)
    pc = jnp.where(fmask, p, -1.0)
    t1 = jnp.max(pc, axis=-1, keepdims=True)
    i1 = first_lane_of(pc == t1)
    pc2 = jnp.where(lane_f == i1, -1.0, pc)
    t2 = jnp.max(pc2, axis=-1, keepdims=True)
    i2 = first_lane_of(pc2 == t2)
    tsum = t1 + t2
    g1 = p_grp * t1 / tsum
    g2 = p_grp * t2 / tsum
    e1 = i1 - N_EXPERT_GROUPS
    e2 = i2 - N_EXPERT_GROUPS

    oh1 = lane_f == e1
    oh2 = lane_f == e2
    oh = jnp.where(oh1, 1.0, 0.0) + jnp.where(oh2, 1.0, 0.0)
    prefix = jnp.dot(tri_ref[...], oh.astype(BF16), preferred_element_type=F32)
    r1 = jnp.sum(jnp.where(oh1, prefix, 0.0), axis=-1, keepdims=True)
    r2 = jnp.sum(jnp.where(oh2, prefix, 0.0), axis=-1, keepdims=True)
    route_out[rows, :] = jnp.where(
        lane == 0, e1,
        jnp.where(lane == 1, e2,
                  jnp.where(lane == 2, g1,
                            jnp.where(lane == 3, g2,
                                      jnp.where(lane == 4, r1,
                                                jnp.where(lane == 5, r2, 0.0))))))
    cnt_out[sub * SUBLANES:(sub + 1) * SUBLANES, :] = jnp.broadcast_to(
        jnp.sum(oh, axis=0, keepdims=True), (SUBLANES, LANES))


def _expert_kernel(eid_ref, first_ref, nact_ref, off_ref,
                   hpk_hbm, wg_ref, wu_ref, wd_ref, ys_out,
                   xbuf, wg_s, wu_s, wd_s, sems):
    i = pl.program_id(0)
    tm = ys_out.shape[0] // PK_ROWS
    n_act = nact_ref[0]

    ch_rows = CHUNK * PK_ROWS
    n_ch = tm // CHUNK

    def gather(blk, slot):
        for c in range(n_ch):
            src = pl.multiple_of(off_ref[blk * n_ch + c], ch_rows)
            pltpu.make_async_copy(hpk_hbm.at[pl.ds(src, ch_rows)],
                                  xbuf.at[slot, pl.ds(c * ch_rows, ch_rows)],
                                  sems.at[slot]).start()

    @pl.when(i == 0)
    def _():
        gather(0, 0)

    @pl.when(i + 1 < n_act)
    def _():
        gather(i + 1, (i + 1) % 2)

    @pl.when(i < n_act)
    def _():
        slot = i % 2

        @pl.when(first_ref[i] == 1)
        def _():
            wg_s[...] = wg_ref[0].astype(BF16)
            wu_s[...] = wu_ref[0].astype(BF16)
            wd_s[...] = wd_ref[0].astype(BF16)

        pltpu.make_async_copy(hpk_hbm.at[pl.ds(0, tm * PK_ROWS)], xbuf.at[slot],
                              sems.at[slot]).wait()
        xu = jnp.concatenate(
            [xbuf[slot, pl.ds(s, tm, stride=PK_ROWS), :] for s in range(PK_ROWS)],
            axis=1)
        x_lo, x_hi = (v.astype(BF16) for v in _unpack_halves(xu))
        a = (jnp.dot(x_lo, wg_s[0:HALF, :], preferred_element_type=F32)
             + jnp.dot(x_hi, wg_s[HALF:, :], preferred_element_type=F32))
        b = (jnp.dot(x_lo, wu_s[0:HALF, :], preferred_element_type=F32)
             + jnp.dot(x_hi, wu_s[HALF:, :], preferred_element_type=F32))
        mid = (a * jax.nn.sigmoid(a) * b).astype(BF16)
        y = jnp.dot(mid, wd_s[...], preferred_element_type=F32)
        yw = _pack_halves(y)
        for s in range(PK_ROWS):
            ys_out[pl.ds(s, tm, stride=PK_ROWS), :] = yw[:, s * LANES:(s + 1) * LANES]

    @pl.when(i >= n_act)
    def _():
        ys_out[...] = jnp.zeros(ys_out.shape, ys_out.dtype)


def _local_sort_kernel(lp_ref, hpk_ref, xs_out):
    i = pl.program_id(0)
    t_tile = hpk_ref.shape[0] // PK_ROWS
    xs_out[...] = jnp.zeros(xs_out.shape, xs_out.dtype)

    def body(q, carry):
        for u in range(SUBLANES):
            t = q * SUBLANES + u
            row = hpk_ref[pl.ds(pl.multiple_of(t * PK_ROWS, PK_ROWS), PK_ROWS), :]
            for k in range(TOP_K):
                dst = pl.multiple_of(lp_ref[(i * t_tile + t) * TOP_K + k], PK_ROWS)
                xs_out[pl.ds(dst, PK_ROWS), :] = row
        return carry
    lax.fori_loop(0, t_tile // SUBLANES, body, 0)


def _combine_kernel(gch_ref, lp_ref, gate_ref, h_ref, ys_hbm, out_ref, ylocal, acc, sems):
    i = pl.program_id(0)
    n = pl.num_programs(0)
    t_tile = h_ref.shape[0]
    ch_rows = CHUNK * PK_ROWS
    n_ch = ylocal.shape[1] // ch_rows

    def gather(tile, slot):
        def body(q, carry):
            for u in range(SUBLANES):
                lc = q * SUBLANES + u
                src = pl.multiple_of(gch_ref[tile * n_ch + lc], ch_rows)
                dst = pl.multiple_of(lc * ch_rows, ch_rows)
                pltpu.make_async_copy(ys_hbm.at[pl.ds(src, ch_rows)],
                                      ylocal.at[slot, pl.ds(dst, ch_rows)],
                                      sems.at[slot]).start()
            return carry
        lax.fori_loop(0, n_ch // SUBLANES, body, 0)

    @pl.when(i == 0)
    def _():
        gather(0, 0)

    @pl.when(i + 1 < n)
    def _():
        gather(i + 1, (i + 1) % 2)

    slot = i % 2
    pltpu.make_async_copy(ys_hbm.at[pl.ds(0, n_ch * ch_rows)], ylocal.at[slot],
                          sems.at[slot]).wait()

    def tok_body(q, carry):
        for u in range(4):
            t = q * 4 + u
            a0 = (i * t_tile + t) * TOP_K
            p0 = pl.multiple_of(lp_ref[a0], PK_ROWS)
            p1 = pl.multiple_of(lp_ref[a0 + 1], PK_ROWS)
            lo0, hi0 = _unpack_halves(ylocal[slot, pl.ds(p0, PK_ROWS), :])
            lo1, hi1 = _unpack_halves(ylocal[slot, pl.ds(p1, PK_ROWS), :])
            g0 = gate_ref[a0]
            g1 = gate_ref[a0 + 1]
            dst = pl.multiple_of(t * Y_ROWS, Y_ROWS)
            acc[pl.ds(dst, PK_ROWS), :] = g0 * lo0 + g1 * lo1
            acc[pl.ds(dst + PK_ROWS, PK_ROWS), :] = g0 * hi0 + g1 * hi1
        return carry
    lax.fori_loop(0, t_tile // 4, tok_body, 0)

    for s in range(Y_ROWS):
        cols = slice(s * LANES, (s + 1) * LANES)
        out_ref[:, cols] = h_ref[:, cols] + acc[pl.ds(s, t_tile, stride=Y_ROWS), :]


def _rope_tables(seq):
    pos = jnp.arange(seq, dtype=F32)
    inv_freq = ROPE_THETA ** (-jnp.arange(0, ROT_DIM, 2, dtype=F32) / ROT_DIM)
    ang = pos[:, None] * inv_freq[None, :]
    cos, sin = jnp.cos(ang), jnp.sin(ang)
    half = ROT_DIM // 2
    ones = jnp.ones((seq, HEAD_DIM - ROT_DIM), F32)
    zeros = jnp.zeros((seq, HEAD_DIM - ROT_DIM), F32)
    zh = jnp.zeros((seq, half), F32)
    cc = jnp.concatenate([cos, cos, ones], axis=1)
    s_up = jnp.concatenate([-sin, zh, zeros], axis=1)
    s_dn = jnp.concatenate([zh, sin, zeros], axis=1)
    rep = LANES // HEAD_DIM
    return tuple(jnp.tile(t, (1, rep)) for t in (cc, s_up, s_dn))


def _plan(eid, rank, cnt_rt, n_tokens):
    sub = LOCAL_TILE // SUB_TILE
    nt = n_tokens // LOCAL_TILE
    experts = jnp.arange(N_EXPERTS, dtype=jnp.int32)
    cnt_sub = cnt_rt.reshape(nt, sub, N_EXPERTS)
    sub_off = jnp.cumsum(cnt_sub, axis=1) - cnt_sub
    cnt = jnp.sum(cnt_sub, axis=1)
    cntp = ((cnt + CHUNK - 1) // CHUNK) * CHUNK
    loff_end = jnp.cumsum(cntp, axis=1)
    loff = loff_end - cntp
    rows_e = jnp.sum(cntp, axis=0)
    rows_pad = ((rows_e + EXPERT_TILE - 1) // EXPERT_TILE) * EXPERT_TILE
    g_end = jnp.cumsum(rows_pad)
    g_start = g_end - rows_pad
    gpos = g_start[None, :] + jnp.cumsum(cntp, axis=0) - cntp

    onehot = eid.reshape(nt, sub, SUB_TILE * TOP_K, 1) == experts
    table = (loff[:, None, :] + sub_off)[:, :, None, :]
    lp = (jnp.sum(jnp.where(onehot, table, 0), axis=-1).reshape(-1)
          + rank.reshape(-1)).astype(jnp.int32)

    lrow = jnp.arange(LT_MAX // CHUNK, dtype=jnp.int32) * CHUNK
    e_lc = jnp.sum((lrow[None, :, None] >= loff_end[:, None, :]).astype(jnp.int32), axis=-1)
    sel = e_lc[:, :, None] == experts
    delta = jnp.sum(jnp.where(sel, (gpos - loff)[:, None, :], 0), axis=-1)
    gch = jnp.where(e_lc < N_EXPERTS, (delta + lrow[None, :]) // CHUNK, 0).reshape(-1)

    n_rows = TOP_K * n_tokens + nt * N_EXPERTS * (CHUNK - 1) + N_EXPERTS * (EXPERT_TILE - CHUNK)
    n_blocks = -(-n_rows // EXPERT_TILE)
    n_rows = n_blocks * EXPERT_TILE
    run_start = (gpos.T.reshape(-1) // CHUNK).astype(F32)
    run_len = (cntp.T.reshape(-1) // CHUNK).astype(F32)
    tile_base = jnp.arange(nt, dtype=jnp.int32)[:, None] * (LT_MAX // CHUNK)
    run_src = ((tile_base + loff // CHUNK).T.reshape(-1)).astype(F32)
    tabs = jnp.stack([run_start, run_len, run_src], axis=1)
    dtabs = tabs - jnp.concatenate([jnp.zeros((1, 3), F32), tabs[:-1]], axis=0)
    gc = jnp.arange(n_rows // CHUNK, dtype=jnp.int32)
    started = (run_start[None, :] <= gc[:, None].astype(F32)).astype(F32)
    picked = jnp.dot(started, dtabs, precision=lax.Precision.HIGHEST).astype(jnp.int32)
    within = gc - picked[:, 0]
    csrc = jnp.where(within < picked[:, 1], picked[:, 2] + within, 0)

    blk_start = jnp.arange(n_blocks, dtype=jnp.int32) * EXPERT_TILE
    blk_eid = jnp.minimum(
        jnp.sum((blk_start[:, None] >= g_end[None, :]).astype(jnp.int32), axis=-1),
        N_EXPERTS - 1)
    prev = jnp.concatenate([jnp.full((1,), -1, jnp.int32), blk_eid[:-1]])
    first = (blk_eid != prev).astype(jnp.int32)
    n_act = (g_end[-1] // EXPERT_TILE).astype(jnp.int32).reshape(1)
    return lp, gch.astype(jnp.int32), csrc.astype(jnp.int32), blk_eid, first, n_act, n_rows


def kernel(x, norm_mix, w_in, q_norm, k_norm, sinks, w_pool, pool_scale, w_out,
           norm_ffn, w_coarse, b_coarse, w_fine, b_fine, w_gate, w_up, w_down):
    bsz, seq, d = x.shape
    n = bsz * seq
    assert d == D_MODEL and seq % ROW_TILE == 0 and seq % BLOCK == 0
    assert norm_mix.shape[0] == 1, "single-layer problem"
    xf = x.reshape(n, d)

    w_in_b = w_in[0].astype(BF16)
    wq = w_in_b[:, :ATTN_WIDTH]
    wk = w_in_b[:, ATTN_WIDTH:ATTN_WIDTH + KV_WIDTH]
    wv = w_in_b[:, ATTN_WIDTH + KV_WIDTH:ATTN_WIDTH + 2 * KV_WIDTH]
    wu = w_in_b[:, ATTN_WIDTH + 2 * KV_WIDTH:]
    qn = jnp.tile(q_norm[0], N_HEADS).reshape(1, ATTN_WIDTH)
    kn = jnp.tile(k_norm[0], N_KV_HEADS).reshape(1, KV_WIDTH)
    cc, s_up, s_dn = _rope_tables(seq)
    lane_head = jnp.arange(MXU_DIM) // HEAD_DIM
    e_mat = (lane_head[:, None] == lane_head[None, :]).astype(BF16)
    w_pool_b = w_pool[0].astype(BF16)
    pscale = pool_scale[0].reshape(1, POOL_WIDTH)
    w_out_b = w_out[0].astype(BF16)
    wo_attn, wo_pool = w_out_b[:ATTN_WIDTH], w_out_b[ATTN_WIDTH:]
    n_router = N_EXPERT_GROUPS + N_EXPERTS
    w_r = jnp.concatenate([w_coarse[0], w_fine[0]], axis=1)
    w_r = jnp.pad(w_r, ((0, 0), (0, LANES - n_router)))
    w_r_hi = w_r.astype(BF16)
    w_r_lo = (w_r - w_r_hi.astype(F32)).astype(BF16)
    w_r2 = jnp.concatenate([w_r_hi, w_r_lo], axis=1)
    b_r = jnp.pad(jnp.concatenate([b_coarse[0], b_fine[0]]),
                  (0, LANES - n_router)).reshape(1, LANES)

    tm = ROW_TILE
    tiles_per_seq = seq // tm
    n_tiles = n // tm
    n_sub = n // SUB_TILE
    idx = jnp.arange(SUB_TILE)
    tri = (idx[None, :] < idx[:, None]).astype(BF16)
    const = lambda *_: (0, 0)
    row_blk = lambda i: (i, 0)

    q, k, v, ksw, vsw, y = pl.pallas_call(
        functools.partial(_inproj_kernel, tiles_per_seq=tiles_per_seq),
        grid=(n_tiles,),
        in_specs=[
            pl.BlockSpec((tm, d), row_blk),
            pl.BlockSpec((1, d), const),
            pl.BlockSpec((d, ATTN_WIDTH), const),
            pl.BlockSpec((d, KV_WIDTH), const),
            pl.BlockSpec((d, KV_WIDTH), const),
            pl.BlockSpec((d, POOL_WIDTH), const),
            pl.BlockSpec((1, ATTN_WIDTH), const),
            pl.BlockSpec((1, KV_WIDTH), const),
            pl.BlockSpec((tm, LANES), lambda i: (i % tiles_per_seq, 0)),
            pl.BlockSpec((tm, LANES), lambda i: (i % tiles_per_seq, 0)),
            pl.BlockSpec((tm, LANES), lambda i: (i % tiles_per_seq, 0)),
            pl.BlockSpec((MXU_DIM, MXU_DIM), const),
            pl.BlockSpec((len(POOL_WINDOWS), POOL_GROUP_DIM, POOL_GROUP_DIM),
                         lambda i: (0, 0, 0)),
            pl.BlockSpec((1, POOL_WIDTH), const),
        ],
        out_specs=[
            pl.BlockSpec((tm, ATTN_WIDTH), row_blk),
            pl.BlockSpec((tm, KV_WIDTH), row_blk),
            pl.BlockSpec((tm, KV_WIDTH), row_blk),
            pl.BlockSpec((tm, KV_WIDTH), row_blk),
            pl.BlockSpec((tm, KV_WIDTH), row_blk),
            pl.BlockSpec((tm, POOL_WIDTH), row_blk),
        ],
        out_shape=[
            jax.ShapeDtypeStruct((n, ATTN_WIDTH), BF16),
            jax.ShapeDtypeStruct((n, KV_WIDTH), BF16),
            jax.ShapeDtypeStruct((n, KV_WIDTH), BF16),
            jax.ShapeDtypeStruct((n, KV_WIDTH), BF16),
            jax.ShapeDtypeStruct((n, KV_WIDTH), BF16),
            jax.ShapeDtypeStruct((n, POOL_WIDTH), BF16),
        ],
        scratch_shapes=[pltpu.VMEM((HALO + tm, POOL_WIDTH), F32)],
        compiler_params=_cparams(),
        name="inproj",
    )(xf, norm_mix[0].reshape(1, d), wq, wk, wv, wu, qn, kn, cc, s_up, s_dn,
      e_mat, w_pool_b, pscale)

    nb = seq // BLOCK
    nq = nb // Q_BLOCKS
    qrows = Q_BLOCKS * BLOCK
    cur = lambda b, j, *_: (b * nq + j, 0)
    prv = lambda b, j, *_: (b * nb + jnp.maximum(Q_BLOCKS * j - 1, 0), 0)
    kv_specs = [pl.BlockSpec((qrows, KV_WIDTH), cur), pl.BlockSpec((BLOCK, KV_WIDTH), prv)]
    attn = pl.pallas_call(
        _attn_kernel,
        grid_spec=pltpu.PrefetchScalarGridSpec(
            num_scalar_prefetch=1,
            grid=(bsz, nq),
            in_specs=[pl.BlockSpec((qrows, ATTN_WIDTH), cur)] + kv_specs * 4,
            out_specs=pl.BlockSpec((qrows, ATTN_WIDTH), cur),
        ),
        out_shape=jax.ShapeDtypeStruct((n, ATTN_WIDTH), BF16),
        compiler_params=_cparams(2),
        name="swa_attn",
    )(sinks[0].astype(F32), q, k, k, ksw, ksw, v, v, vsw, vsw)

    h, hpk, route, cnt = pl.pallas_call(
        _outproj_router_kernel,
        grid=(n_tiles,),
        in_specs=[
            pl.BlockSpec((tm, ATTN_WIDTH), row_blk),
            pl.BlockSpec((tm, POOL_WIDTH), row_blk),
            pl.BlockSpec((tm, d), row_blk),
            pl.BlockSpec((ATTN_WIDTH, d), const),
            pl.BlockSpec((POOL_WIDTH, d), const),
            pl.BlockSpec((1, d), const),
            pl.BlockSpec((d, 2 * LANES), const),
            pl.BlockSpec((1, LANES), const),
            pl.BlockSpec((SUB_TILE, SUB_TILE), const),
        ],
        out_specs=[
            pl.BlockSpec((tm, d), row_blk),
            pl.BlockSpec((tm * PK_ROWS, LANES), row_blk),
            pl.BlockSpec((tm, LANES), row_blk),
            pl.BlockSpec((tm // SUB_TILE * SUBLANES, LANES), row_blk),
        ],
        out_shape=[
            jax.ShapeDtypeStruct((n, d), F32),
            jax.ShapeDtypeStruct((n * PK_ROWS, LANES), jnp.uint32),
            jax.ShapeDtypeStruct((n, LANES), F32),
            jax.ShapeDtypeStruct((n_sub * SUBLANES, LANES), F32),
        ],
        compiler_params=_cparams(),
        name="outproj_router",
    )(attn, y, xf, wo_attn, wo_pool, norm_ffn[0].reshape(1, d), w_r2, b_r, tri)

    eid = route[:, :TOP_K].astype(jnp.int32)
    rank = route[:, 4:4 + TOP_K].astype(jnp.int32)
    tile_cnt = cnt.reshape(n_sub, SUBLANES, LANES)[:, 0, :N_EXPERTS].astype(jnp.int32)
    te = EXPERT_TILE
    lp, gch, csrc, blk_eid, first, n_act, n_rows = _plan(eid, rank, tile_cnt, n)
    n_blocks = n_rows // te
    n_local = n // LOCAL_TILE
    gates = route[:, 2:2 + TOP_K].reshape(n * TOP_K)

    xs = pl.pallas_call(
        _local_sort_kernel,
        grid_spec=pltpu.PrefetchScalarGridSpec(
            num_scalar_prefetch=1,
            grid=(n_local,),
            in_specs=[pl.BlockSpec((LOCAL_TILE * PK_ROWS, LANES), lambda i, *_: (i, 0))],
            out_specs=pl.BlockSpec((LT_MAX * PK_ROWS, LANES), lambda i, *_: (i, 0)),
        ),
        out_shape=jax.ShapeDtypeStruct((n_local * LT_MAX * PK_ROWS, LANES), jnp.uint32),
        compiler_params=_cparams(),
        name="local_sort",
    )(lp * PK_ROWS, hpk)

    w_idx = lambda i, eid_r, *_: (eid_r[i], 0, 0)
    ys = pl.pallas_call(
        _expert_kernel,
        grid_spec=pltpu.PrefetchScalarGridSpec(
            num_scalar_prefetch=4,
            grid=(n_blocks,),
            in_specs=[
                pl.BlockSpec(memory_space=pl.ANY),
                pl.BlockSpec((1, d, D_EXPERT), w_idx),
                pl.BlockSpec((1, d, D_EXPERT), w_idx),
                pl.BlockSpec((1, D_EXPERT, d), w_idx),
            ],
            out_specs=pl.BlockSpec((te * PK_ROWS, LANES), lambda i, *_: (i, 0)),
            scratch_shapes=[
                pltpu.VMEM((2, te * PK_ROWS, LANES), jnp.uint32),
                pltpu.VMEM((d, D_EXPERT), BF16),
                pltpu.VMEM((d, D_EXPERT), BF16),
                pltpu.VMEM((D_EXPERT, d), BF16),
                pltpu.SemaphoreType.DMA((2,)),
            ],
        ),
        out_shape=jax.ShapeDtypeStruct((n_rows * PK_ROWS, LANES), jnp.uint32),
        compiler_params=_cparams(),
        name="experts",
    )(blk_eid, first, n_act, csrc * (CHUNK * PK_ROWS), xs, w_gate[0], w_up[0], w_down[0])

    tt = LOCAL_TILE
    out = pl.pallas_call(
        _combine_kernel,
        grid_spec=pltpu.PrefetchScalarGridSpec(
            num_scalar_prefetch=3,
            grid=(n_local,),
            in_specs=[
                pl.BlockSpec((tt, d), lambda i, *_: (i, 0)),
                pl.BlockSpec(memory_space=pl.ANY),
            ],
            out_specs=pl.BlockSpec((tt, d), lambda i, *_: (i, 0)),
            scratch_shapes=[
                pltpu.VMEM((2, LT_MAX * PK_ROWS, LANES), jnp.uint32),
                pltpu.VMEM((tt * Y_ROWS, LANES), F32),
                pltpu.SemaphoreType.DMA((2,)),
            ],
        ),
        out_shape=jax.ShapeDtypeStruct((n, d), F32),
        compiler_params=_cparams(),
        name="combine",
    )(gch * (CHUNK * PK_ROWS), lp * PK_ROWS, gates, h, ys)
    return out.reshape(bsz, seq, d)
```

```python
import functools

import jax
import jax.numpy as jnp
from jax import lax
from jax.experimental import pallas as pl
from jax.experimental.pallas import tpu as pltpu

F32 = jnp.float32
BF16 = jnp.bfloat16

D_MODEL = 2048
N_HEADS = 16
N_KV_HEADS = 4
HEAD_DIM = 64
GROUP = N_HEADS // N_KV_HEADS
ROT_DIM = HEAD_DIM // 4
ROPE_THETA = 500000.0
WINDOW = 128
BLOCK = 128
Q_BLOCKS = 2
ATTN_WIDTH = N_HEADS * HEAD_DIM
KV_WIDTH = N_KV_HEADS * HEAD_DIM
POOL_WINDOWS = (2, 4, 8, 16)
POOL_WIDTH = D_MODEL // 2
POOL_GROUP_DIM = POOL_WIDTH // len(POOL_WINDOWS)
N_EXPERT_GROUPS = 4
EXPERTS_PER_GROUP = 8
N_EXPERTS = N_EXPERT_GROUPS * EXPERTS_PER_GROUP
TOP_K = 2
D_EXPERT = 512
EPS = 1e-6

LANES = 128
SUBLANES = 8
MXU_DIM = 256
HALO = 16
NEG_BIG = -1e30
HALF = D_MODEL // 2
PK_ROWS = HALF // LANES
Y_ROWS = D_MODEL // LANES

ROW_TILE = 512
SUB_TILE = 256
EXPERT_TILE = 512
LOCAL_TILE = 512
CHUNK = 8
LT_MAX = TOP_K * LOCAL_TILE + N_EXPERTS * CHUNK
VMEM_LIMIT = 56 * 1024 * 1024
VMEM_LIMIT_EXPERTS = 60 * 1024 * 1024


def _cparams(n_axes=1, vmem_limit=VMEM_LIMIT):
    return pltpu.CompilerParams(
        dimension_semantics=("arbitrary",) * n_axes,
        vmem_limit_bytes=vmem_limit,
    )


def _pack_halves(t):
    t32 = t.astype(BF16).astype(F32)
    lo_bits = lax.shift_right_logical(
        pltpu.bitcast(t32[:, :HALF], jnp.uint32), jnp.uint32(16))
    hi_bits = pltpu.bitcast(t32[:, HALF:], jnp.uint32) & jnp.uint32(0xFFFF0000)
    return lo_bits | hi_bits


def _unpack_halves(w):
    lo = pltpu.bitcast(lax.shift_left(w, jnp.uint32(16)), F32)
    hi = pltpu.bitcast(w & jnp.uint32(0xFFFF0000), F32)
    return lo, hi


def _head_sumsq(t, e_ref):
    t2 = t * t
    hi = t2.astype(BF16)
    lo = (t2 - hi.astype(F32)).astype(BF16)
    e = e_ref[...]
    outs = []
    for c in range(t.shape[1] // MXU_DIM):
        sl = slice(c * MXU_DIM, (c + 1) * MXU_DIM)
        outs.append(jnp.dot(hi[:, sl], e, preferred_element_type=F32)
                    + jnp.dot(lo[:, sl], e, preferred_element_type=F32))
    return outs[0] if len(outs) == 1 else jnp.concatenate(outs, axis=1)


def _norm_rope(t, gain, e_ref, cc, s_up, s_dn, out_scale):
    w = t.shape[1]
    ss = _head_sumsq(t, e_ref)
    tn = t * lax.rsqrt(ss * (1.0 / HEAD_DIM) + EPS) * gain
    reps = w // LANES
    up = pltpu.roll(tn, w - ROT_DIM // 2, axis=1)
    dn = pltpu.roll(tn, ROT_DIM // 2, axis=1)
    c = jnp.concatenate([cc] * reps, axis=1)
    su = jnp.concatenate([s_up] * reps, axis=1)
    sd = jnp.concatenate([s_dn] * reps, axis=1)
    return (tn * c + up * su + dn * sd) * out_scale


def _swap_head_pairs(t):
    w = t.shape[1]
    lane = lax.broadcasted_iota(jnp.int32, t.shape, 1)
    from_up = pltpu.roll(t, w - HEAD_DIM, axis=1)
    from_dn = pltpu.roll(t, HEAD_DIM, axis=1)
    return jnp.where((lane & (LANES - 1)) < HEAD_DIM, from_up, from_dn)


def _inproj_kernel(x_ref, g_ref, wq_ref, wk_ref, wv_ref, wu_ref, qn_ref, kn_ref,
                   cc_ref, su_ref, sd_ref, e_ref, wpool_ref, pscale_ref,
                   q_out, k_out, v_out, ksw_out, vsw_out, y_out, ubuf, *, tiles_per_seq):
    i = pl.program_id(0)
    tm = x_ref.shape[0]

    @pl.when(i % tiles_per_seq == 0)
    def _():
        ubuf[0:HALO, :] = jnp.zeros((HALO, POOL_WIDTH), F32)

    for r0 in range(0, tm, SUB_TILE):
        rows = slice(r0, r0 + SUB_TILE)
        x = x_ref[rows, :]
        ms = jnp.mean(x * x, axis=-1, keepdims=True)
        hn = (x * lax.rsqrt(ms + EPS) * g_ref[...]).astype(BF16)

        cc, s_up, s_dn = cc_ref[rows, :], su_ref[rows, :], sd_ref[rows, :]
        q = jnp.dot(hn, wq_ref[...], preferred_element_type=F32)
        q_out[rows, :] = _norm_rope(q, qn_ref[...], e_ref, cc, s_up, s_dn,
                                    HEAD_DIM ** -0.5).astype(BF16)
        k = jnp.dot(hn, wk_ref[...], preferred_element_type=F32)
        k = _norm_rope(k, kn_ref[...], e_ref, cc, s_up, s_dn, 1.0)
        v = jnp.dot(hn, wv_ref[...], preferred_element_type=F32)
        k_out[rows, :] = k.astype(BF16)
        v_out[rows, :] = v.astype(BF16)
        ksw_out[rows, :] = _swap_head_pairs(k).astype(BF16)
        vsw_out[rows, :] = _swap_head_pairs(v).astype(BF16)

        base = HALO + r0
        ubuf[base:base + SUB_TILE, :] = jnp.dot(hn, wu_ref[...], preferred_element_type=F32)
        pos = (i % tiles_per_seq) * tm + r0 + lax.broadcasted_iota(
            jnp.int32, (SUB_TILE, POOL_GROUP_DIM), 0)
        for g, w in enumerate(POOL_WINDOWS):
            cols = slice(g * POOL_GROUP_DIM, (g + 1) * POOL_GROUP_DIM)
            u_g = ubuf[base:base + SUB_TILE, cols]
            acc = u_g
            for j in range(1, w):
                acc = acc + ubuf[base - j:base - j + SUB_TILE, cols]
            cnt = jnp.minimum(pos + 1, w).astype(F32)
            d = (acc / cnt - u_g).astype(BF16)
            y = jnp.dot(d, wpool_ref[g], preferred_element_type=F32)
            y_out[rows, cols] = (y * pscale_ref[:, cols]).astype(BF16)
    ubuf[0:HALO, :] = ubuf[tm:tm + HALO, :]


def _attn_kernel(sinks_ref, q_ref, kc_ref, kp_ref, kswc_ref, kswp_ref,
                 vc_ref, vp_ref, vswc_ref, vswp_ref, o_ref):
    j = pl.program_id(1)
    two = 2 * BLOCK
    row = lax.broadcasted_iota(jnp.int32, (two, two), 0) & (BLOCK - 1)
    col = lax.broadcasted_iota(jnp.int32, (two, two), 1)
    diff = row + BLOCK - col
    local = (diff >= 0) & (diff < WINDOW)
    lo_lanes = lax.broadcasted_iota(jnp.int32, (two, LANES), 1) < HEAD_DIM
    top_rows = lax.broadcasted_iota(jnp.int32, (two, 1), 0) < BLOCK
    zero = jnp.zeros((two, LANES), BF16)

    for sb in range(q_ref.shape[0] // BLOCK):
        rows = slice(sb * BLOCK, (sb + 1) * BLOCK)
        if sb == 0:
            def band(cur_ref, prev_ref):
                return jnp.concatenate([prev_ref[...], cur_ref[0:BLOCK, :]], axis=0)
            valid = local & ((col >= BLOCK) | (j > 0))
        else:
            def band(cur_ref, prev_ref, sb=sb):
                return cur_ref[(sb - 1) * BLOCK:(sb + 1) * BLOCK, :]
            valid = local
        k_nat, k_swp = band(kc_ref, kp_ref), band(kswc_ref, kswp_ref)
        v_nat, v_swp = band(vc_ref, vp_ref), band(vswc_ref, vswp_ref)

        for g in range(N_KV_HEADS):
            c0 = (g // 2) * LANES
            kcols = (k_nat[:, c0:c0 + LANES], k_swp[:, c0:c0 + LANES])
            vcols = (v_nat[:, c0:c0 + LANES], v_swp[:, c0:c0 + LANES])
            in_lo, in_hi = (0, 1) if g % 2 == 0 else (1, 0)
            q0 = g * GROUP * HEAD_DIM
            qq = jnp.concatenate([q_ref[rows, q0:q0 + LANES],
                                  q_ref[rows, q0 + LANES:q0 + 2 * LANES]], axis=0)
            halves = []
            for half, src in ((0, in_lo), (1, in_hi)):
                qm = jnp.where(lo_lanes, qq, zero) if half == 0 else jnp.where(lo_lanes, zero, qq)
                s = lax.dot_general(qm, kcols[src], (((1,), (1,)), ((), ())),
                                    preferred_element_type=F32)
                s = jnp.where(valid, s, NEG_BIG)
                h_top = g * GROUP + half
                sink = jnp.where(top_rows, sinks_ref[h_top], sinks_ref[h_top + 2])
                m = jnp.maximum(jnp.max(s, axis=-1, keepdims=True), sink)
                p = jnp.exp(s - m)
                denom = jnp.sum(p, axis=-1, keepdims=True) + jnp.exp(sink - m)
                o = jnp.dot(p.astype(BF16), vcols[src], preferred_element_type=F32)
                halves.append(o / denom)
            o_pair = jnp.where(lo_lanes, halves[0], halves[1]).astype(BF16)
            o_ref[rows, q0:q0 + LANES] = o_pair[0:BLOCK, :]
            o_ref[rows, q0 + LANES:q0 + 2 * LANES] = o_pair[BLOCK:, :]


def _outproj_router_kernel(o_ref, y_ref, x_ref, woa_ref, wop_ref, g_ref, wr_ref,
                           br_ref, tri_ref, h_out, hpk_out, route_out, cnt_out):
    for sub in range(x_ref.shape[0] // SUB_TILE):
        _outproj_router_subtile(sub, o_ref, y_ref, x_ref, woa_ref, wop_ref, g_ref, wr_ref,
                                br_ref, tri_ref, h_out, hpk_out, route_out, cnt_out)


def _outproj_router_subtile(sub, o_ref, y_ref, x_ref, woa_ref, wop_ref, g_ref, wr_ref,
                            br_ref, tri_ref, h_out, hpk_out, route_out, cnt_out):
    tm = SUB_TILE
    r0 = sub * SUB_TILE
    rows = slice(r0, r0 + SUB_TILE)
    h = (x_ref[rows, :]
         + jnp.dot(o_ref[rows, :], woa_ref[...], preferred_element_type=F32)
         + jnp.dot(y_ref[rows, :], wop_ref[...], preferred_element_type=F32))
    h_out[rows, :] = h
    ms = jnp.mean(h * h, axis=-1, keepdims=True)
    hn = h * lax.rsqrt(ms + EPS) * g_ref[...]
    hb = hn.astype(BF16)
    hb32 = hb.astype(F32)
    packed = _pack_halves(hn)
    for s in range(PK_ROWS):
        hpk_out[pl.ds(r0 * PK_ROWS + s, tm, stride=PK_ROWS), :] = (
            packed[:, s * LANES:(s + 1) * LANES])

    hl = (hn - hb32).astype(BF16)
    both = jnp.dot(hb, wr_ref[...], preferred_element_type=F32)
    logits = (both[:, :LANES] + both[:, LANES:]
              + jnp.dot(hl, wr_ref[:, :LANES], preferred_element_type=F32)
              + br_ref[...])
    lane = lax.broadcasted_iota(jnp.int32, (tm, LANES), 1)
    lane_f = lane.astype(F32)

    def first_lane_of(mask):
        return jnp.min(jnp.where(mask, lane_f, float(LANES)), axis=-1, keepdims=True)

    coarse = jnp.where(lane < N_EXPERT_GROUPS, logits, NEG_BIG)
    cmax = jnp.max(coarse, axis=-1, keepdims=True)
    grp = first_lane_of(coarse == cmax)
    p_grp = 1.0 / jnp.sum(jnp.exp(coarse - cmax), axis=-1, keepdims=True)

    flo = N_EXPERT_GROUPS + EXPERTS_PER_GROUP * grp
    fmask = (lane_f >= flo) & (lane_f < flo + EXPERTS_PER_GROUP)
    fsel = jnp.where(fmask, logits, NEG_BIG)
    fmax = jnp.max(fsel, axis=-1, keepdims=True)
    fe = jnp.exp(fsel - fmax)
    p = fe / jnp.sum(fe, axis=-1, keepdims=True)
    pc = jnp.where(fmask, p, -1.0)
    t1 = jnp.max(pc, axis=-1, keepdims=True)
    i1 = first_lane_of(pc == t1)
    pc2 = jnp.where(lane_f == i1, -1.0, pc)
    t2 = jnp.max(pc2, axis=-1, keepdims=True)
    i2 = first_lane_of(pc2 == t2)
    tsum = t1 + t2
    g1 = p_grp * t1 / tsum
    g2 = p_grp * t2 / tsum
    e1 = i1 - N_EXPERT_GROUPS
    e2 = i2 - N_EXPERT_GROUPS

    oh1 = lane_f == e1
    oh2 = lane_f == e2
    oh = jnp.where(oh1, 1.0, 0.0) + jnp.where(oh2, 1.0, 0.0)
    prefix = jnp.dot(tri_ref[...], oh.astype(BF16), preferred_element_type=F32)
    r1 = jnp.sum(jnp.where(oh1, prefix, 0.0), axis=-1, keepdims=True)
    r2 = jnp.sum(jnp.where(oh2, prefix, 0.0), axis=-1, keepdims=True)
    route_out[rows, :] = jnp.where(
        lane == 0, e1,
        jnp.where(lane == 1, e2,
                  jnp.where(lane == 2, g1,
                            jnp.where(lane == 3, g2,
                                      jnp.where(lane == 4, r1,
                                                jnp.where(lane == 5, r2, 0.0))))))
    cnt_out[sub * SUBLANES:(sub + 1) * SUBLANES, :] = jnp.broadcast_to(
        jnp.sum(oh, axis=0, keepdims=True), (SUBLANES, LANES))


def _expert_kernel(eid_ref, first_ref, nact_ref, off_ref,
                   hpk_hbm, wg_ref, wu_ref, wd_ref, ys_out,
                   xbuf, wg_s, wu_s, wd_s, sems):
    i = pl.program_id(0)
    tm = ys_out.shape[0] // PK_ROWS
    n_act = nact_ref[0]

    ch_rows = CHUNK * PK_ROWS
    n_ch = tm // CHUNK

    def gather(blk, slot):
        for c in range(n_ch):
            src = pl.multiple_of(off_ref[blk * n_ch + c], ch_rows)
            pltpu.make_async_copy(hpk_hbm.at[pl.ds(src, ch_rows)],
                                  xbuf.at[slot, pl.ds(c * ch_rows, ch_rows)],
                                  sems.at[slot]).start()

    @pl.when(i == 0)
    def _():
        gather(0, 0)

    @pl.when(i + 1 < n_act)
    def _():
        gather(i + 1, (i + 1) % 2)

    @pl.when(i < n_act)
    def _():
        slot = i % 2

        @pl.when(first_ref[i] == 1)
        def _():
            wg_s[...] = wg_ref[0].astype(BF16)
            wu_s[...] = wu_ref[0].astype(BF16)
            wd_s[...] = wd_ref[0].astype(BF16)

        pltpu.make_async_copy(hpk_hbm.at[pl.ds(0, tm * PK_ROWS)], xbuf.at[slot],
                              sems.at[slot]).wait()
        xu = jnp.concatenate(
            [xbuf[slot, pl.ds(s, tm, stride=PK_ROWS), :] for s in range(PK_ROWS)],
            axis=1)
        x_lo, x_hi = (v.astype(BF16) for v in _unpack_halves(xu))
        a = (jnp.dot(x_lo, wg_s[0:HALF, :], preferred_element_type=F32)
             + jnp.dot(x_hi, wg_s[HALF:, :], preferred_element_type=F32))
        b = (jnp.dot(x_lo, wu_s[0:HALF, :], preferred_element_type=F32)
             + jnp.dot(x_hi, wu_s[HALF:, :], preferred_element_type=F32))
        mid = (a * jax.nn.sigmoid(a) * b).astype(BF16)
        y = jnp.dot(mid, wd_s[...], preferred_element_type=F32)
        yw = _pack_halves(y)
        for s in range(PK_ROWS):
            ys_out[pl.ds(s, tm, stride=PK_ROWS), :] = yw[:, s * LANES:(s + 1) * LANES]

    @pl.when(i >= n_act)
    def _():
        ys_out[...] = jnp.zeros(ys_out.shape, ys_out.dtype)


def _local_sort_kernel(lp_ref, hpk_ref, xs_out):
    i = pl.program_id(0)
    t_tile = hpk_ref.shape[0] // PK_ROWS
    xs_out[...] = jnp.zeros(xs_out.shape, xs_out.dtype)

    def body(q, carry):
        for u in range(SUBLANES):
            t = q * SUBLANES + u
            row = hpk_ref[pl.ds(pl.multiple_of(t * PK_ROWS, PK_ROWS), PK_ROWS), :]
            for k in range(TOP_K):
                dst = pl.multiple_of(lp_ref[(i * t_tile + t) * TOP_K + k], PK_ROWS)
                xs_out[pl.ds(dst, PK_ROWS), :] = row
        return carry
    lax.fori_loop(0, t_tile // SUBLANES, body, 0)


def _combine_kernel(gch_ref, lp_ref, gate_ref, h_ref, ys_hbm, out_ref, ylocal, acc, sems):
    i = pl.program_id(0)
    n = pl.num_programs(0)
    t_tile = h_ref.shape[0]
    ch_rows = CHUNK * PK_ROWS
    n_ch = ylocal.shape[1] // ch_rows

    def gather(tile, slot):
        def body(q, carry):
            for u in range(SUBLANES):
                lc = q * SUBLANES + u
                src = pl.multiple_of(gch_ref[tile * n_ch + lc], ch_rows)
                dst = pl.multiple_of(lc * ch_rows, ch_rows)
                pltpu.make_async_copy(ys_hbm.at[pl.ds(src, ch_rows)],
                                      ylocal.at[slot, pl.ds(dst, ch_rows)],
                                      sems.at[slot]).start()
            return carry
        lax.fori_loop(0, n_ch // SUBLANES, body, 0)

    @pl.when(i == 0)
    def _():
        gather(0, 0)

    @pl.when(i + 1 < n)
    def _():
        gather(i + 1, (i + 1) % 2)

    slot = i % 2
    pltpu.make_async_copy(ys_hbm.at[pl.ds(0, n_ch * ch_rows)], ylocal.at[slot],
                          sems.at[slot]).wait()

    def tok_body(q, carry):
        for u in range(4):
            t = q * 4 + u
            a0 = (i * t_tile + t) * TOP_K
            p0 = pl.multiple_of(lp_ref[a0], PK_ROWS)
            p1 = pl.multiple_of(lp_ref[a0 + 1], PK_ROWS)
            lo0, hi0 = _unpack_halves(ylocal[slot, pl.ds(p0, PK_ROWS), :])
            lo1, hi1 = _unpack_halves(ylocal[slot, pl.ds(p1, PK_ROWS), :])
            g0 = gate_ref[a0]
            g1 = gate_ref[a0 + 1]
            dst = pl.multiple_of(t * Y_ROWS, Y_ROWS)
            acc[pl.ds(dst, PK_ROWS), :] = g0 * lo0 + g1 * lo1
            acc[pl.ds(dst + PK_ROWS, PK_ROWS), :] = g0 * hi0 + g1 * hi1
        return carry
    lax.fori_loop(0, t_tile // 4, tok_body, 0)

    for s in range(Y_ROWS):
        cols = slice(s * LANES, (s + 1) * LANES)
        out_ref[:, cols] = h_ref[:, cols] + acc[pl.ds(s, t_tile, stride=Y_ROWS), :]


def _rope_tables(seq):
    pos = jnp.arange(seq, dtype=F32)
    inv_freq = ROPE_THETA ** (-jnp.arange(0, ROT_DIM, 2, dtype=F32) / ROT_DIM)
    ang = pos[:, None] * inv_freq[None, :]
    cos, sin = jnp.cos(ang), jnp.sin(ang)
    half = ROT_DIM // 2
    ones = jnp.ones((seq, HEAD_DIM - ROT_DIM), F32)
    zeros = jnp.zeros((seq, HEAD_DIM - ROT_DIM), F32)
    zh = jnp.zeros((seq, half), F32)
    cc = jnp.concatenate([cos, cos, ones], axis=1)
    s_up = jnp.concatenate([-sin, zh, zeros], axis=1)
    s_dn = jnp.concatenate([zh, sin, zeros], axis=1)
    rep = LANES // HEAD_DIM
    return tuple(jnp.tile(t, (1, rep)) for t in (cc, s_up, s_dn))


def _plan(eid, rank, cnt_rt, n_tokens):
    sub = LOCAL_TILE // SUB_TILE
    nt = n_tokens // LOCAL_TILE
    experts = jnp.arange(N_EXPERTS, dtype=jnp.int32)
    cnt_sub = cnt_rt.reshape(nt, sub, N_EXPERTS)
    sub_off = jnp.cumsum(cnt_sub, axis=1) - cnt_sub
    cnt = jnp.sum(cnt_sub, axis=1)
    cntp = ((cnt + CHUNK - 1) // CHUNK) * CHUNK
    loff_end = jnp.cumsum(cntp, axis=1)
    loff = loff_end - cntp
    rows_e = jnp.sum(cntp, axis=0)
    rows_pad = ((rows_e + EXPERT_TILE - 1) // EXPERT_TILE) * EXPERT_TILE
    g_end = jnp.cumsum(rows_pad)
    g_start = g_end - rows_pad
    gpos = g_start[None, :] + jnp.cumsum(cntp, axis=0) - cntp

    onehot = eid.reshape(nt, sub, SUB_TILE * TOP_K, 1) == experts
    table = (loff[:, None, :] + sub_off)[:, :, None, :]
    lp = (jnp.sum(jnp.where(onehot, table, 0), axis=-1).reshape(-1)
          + rank.reshape(-1)).astype(jnp.int32)

    lrow = jnp.arange(LT_MAX // CHUNK, dtype=jnp.int32) * CHUNK
    e_lc = jnp.sum((lrow[None, :, None] >= loff_end[:, None, :]).astype(jnp.int32), axis=-1)
    sel = e_lc[:, :, None] == experts
    delta = jnp.sum(jnp.where(sel, (gpos - loff)[:, None, :], 0), axis=-1)
    gch = jnp.where(e_lc < N_EXPERTS, (delta + lrow[None, :]) // CHUNK, 0).reshape(-1)

    n_rows = TOP_K * n_tokens + nt * N_EXPERTS * (CHUNK - 1) + N_EXPERTS * (EXPERT_TILE - CHUNK)
    n_blocks = -(-n_rows // EXPERT_TILE)
    n_rows = n_blocks * EXPERT_TILE
    run_start = (gpos.T.reshape(-1) // CHUNK).astype(F32)
    run_len = (cntp.T.reshape(-1) // CHUNK).astype(F32)
    tile_base = jnp.arange(nt, dtype=jnp.int32)[:, None] * (LT_MAX // CHUNK)
    run_src = ((tile_base + loff // CHUNK).T.reshape(-1)).astype(F32)
    tabs = jnp.stack([run_start, run_len, run_src], axis=1)
    dtabs = tabs - jnp.concatenate([jnp.zeros((1, 3), F32), tabs[:-1]], axis=0)
    gc = jnp.arange(n_rows // CHUNK, dtype=jnp.int32)
    started = (run_start[None, :] <= gc[:, None].astype(F32)).astype(F32)
    picked = jnp.dot(started, dtabs, precision=lax.Precision.HIGHEST).astype(jnp.int32)
    within = gc - picked[:, 0]
    csrc = jnp.where(within < picked[:, 1], picked[:, 2] + within, 0)

    blk_start = jnp.arange(n_blocks, dtype=jnp.int32) * EXPERT_TILE
    blk_eid = jnp.minimum(
        jnp.sum((blk_start[:, None] >= g_end[None, :]).astype(jnp.int32), axis=-1),
        N_EXPERTS - 1)
    prev = jnp.concatenate([jnp.full((1,), -1, jnp.int32), blk_eid[:-1]])
    first = (blk_eid != prev).astype(jnp.int32)
    n_act = (g_end[-1] // EXPERT_TILE).astype(jnp.int32).reshape(1)
    return lp, gch.astype(jnp.int32), csrc.astype(jnp.int32), blk_eid, first, n_act, n_rows


def kernel(x, norm_mix, w_in, q_norm, k_norm, sinks, w_pool, pool_scale, w_out,
           norm_ffn, w_coarse, b_coarse, w_fine, b_fine, w_gate, w_up, w_down):
    bsz, seq, d = x.shape
    n = bsz * seq
    assert d == D_MODEL and seq % ROW_TILE == 0 and seq % BLOCK == 0
    assert norm_mix.shape[0] == 1, "single-layer problem"
    xf = x.reshape(n, d)

    w_in_b = w_in[0].astype(BF16)
    wq = w_in_b[:, :ATTN_WIDTH]
    wk = w_in_b[:, ATTN_WIDTH:ATTN_WIDTH + KV_WIDTH]
    wv = w_in_b[:, ATTN_WIDTH + KV_WIDTH:ATTN_WIDTH + 2 * KV_WIDTH]
    wu = w_in_b[:, ATTN_WIDTH + 2 * KV_WIDTH:]
    qn = jnp.tile(q_norm[0], N_HEADS).reshape(1, ATTN_WIDTH)
    kn = jnp.tile(k_norm[0], N_KV_HEADS).reshape(1, KV_WIDTH)
    cc, s_up, s_dn = _rope_tables(seq)
    lane_head = jnp.arange(MXU_DIM) // HEAD_DIM
    e_mat = (lane_head[:, None] == lane_head[None, :]).astype(BF16)
    w_pool_b = w_pool[0].astype(BF16)
    pscale = pool_scale[0].reshape(1, POOL_WIDTH)
    w_out_b = w_out[0].astype(BF16)
    wo_attn, wo_pool = w_out_b[:ATTN_WIDTH], w_out_b[ATTN_WIDTH:]
    n_router = N_EXPERT_GROUPS + N_EXPERTS
    w_r = jnp.concatenate([w_coarse[0], w_fine[0]], axis=1)
    w_r = jnp.pad(w_r, ((0, 0), (0, LANES - n_router)))
    w_r_hi = w_r.astype(BF16)
    w_r_lo = (w_r - w_r_hi.astype(F32)).astype(BF16)
    w_r2 = jnp.concatenate([w_r_hi, w_r_lo], axis=1)
    b_r = jnp.pad(jnp.concatenate([b_coarse[0], b_fine[0]]),
                  (0, LANES - n_router)).reshape(1, LANES)

    tm = ROW_TILE
    tiles_per_seq = seq // tm
    n_tiles = n // tm
    n_sub = n // SUB_TILE
    idx = jnp.arange(SUB_TILE)
    tri = (idx[None, :] < idx[:, None]).astype(BF16)
    const = lambda *_: (0, 0)
    row_blk = lambda i: (i, 0)

    q, k, v, ksw, vsw, y = pl.pallas_call(
        functools.partial(_inproj_kernel, tiles_per_seq=tiles_per_seq),
        grid=(n_tiles,),
        in_specs=[
            pl.BlockSpec((tm, d), row_blk),
            pl.BlockSpec((1, d), const),
            pl.BlockSpec((d, ATTN_WIDTH), const),
            pl.BlockSpec((d, KV_WIDTH), const),
            pl.BlockSpec((d, KV_WIDTH), const),
            pl.BlockSpec((d, POOL_WIDTH), const),
            pl.BlockSpec((1, ATTN_WIDTH), const),
            pl.BlockSpec((1, KV_WIDTH), const),
            pl.BlockSpec((tm, LANES), lambda i: (i % tiles_per_seq, 0)),
            pl.BlockSpec((tm, LANES), lambda i: (i % tiles_per_seq, 0)),
            pl.BlockSpec((tm, LANES), lambda i: (i % tiles_per_seq, 0)),
            pl.BlockSpec((MXU_DIM, MXU_DIM), const),
            pl.BlockSpec((len(POOL_WINDOWS), POOL_GROUP_DIM, POOL_GROUP_DIM),
                         lambda i: (0, 0, 0)),
            pl.BlockSpec((1, POOL_WIDTH), const),
        ],
        out_specs=[
            pl.BlockSpec((tm, ATTN_WIDTH), row_blk),
            pl.BlockSpec((tm, KV_WIDTH), row_blk),
            pl.BlockSpec((tm, KV_WIDTH), row_blk),
            pl.BlockSpec((tm, KV_WIDTH), row_blk),
            pl.BlockSpec((tm, KV_WIDTH), row_blk),
            pl.BlockSpec((tm, POOL_WIDTH), row_blk),
        ],
        out_shape=[
            jax.ShapeDtypeStruct((n, ATTN_WIDTH), BF16),
            jax.ShapeDtypeStruct((n, KV_WIDTH), BF16),
            jax.ShapeDtypeStruct((n, KV_WIDTH), BF16),
            jax.ShapeDtypeStruct((n, KV_WIDTH), BF16),
            jax.ShapeDtypeStruct((n, KV_WIDTH), BF16),
            jax.ShapeDtypeStruct((n, POOL_WIDTH), BF16),
        ],
        scratch_shapes=[pltpu.VMEM((HALO + tm, POOL_WIDTH), F32)],
        compiler_params=_cparams(),
        name="inproj",
    )(xf, norm_mix[0].reshape(1, d), wq, wk, wv, wu, qn, kn, cc, s_up, s_dn,
      e_mat, w_pool_b, pscale)

    nb = seq // BLOCK
    nq = nb // Q_BLOCKS
    qrows = Q_BLOCKS * BLOCK
    cur = lambda b, j, *_: (b * nq + j, 0)
    prv = lambda b, j, *_: (b * nb + jnp.maximum(Q_BLOCKS * j - 1, 0), 0)
    kv_specs = [pl.BlockSpec((qrows, KV_WIDTH), cur), pl.BlockSpec((BLOCK, KV_WIDTH), prv)]
    attn = pl.pallas_call(
        _attn_kernel,
        grid_spec=pltpu.PrefetchScalarGridSpec(
            num_scalar_prefetch=1,
            grid=(bsz, nq),
            in_specs=[pl.BlockSpec((qrows, ATTN_WIDTH), cur)] + kv_specs * 4,
            out_specs=pl.BlockSpec((qrows, ATTN_WIDTH), cur),
        ),
        out_shape=jax.ShapeDtypeStruct((n, ATTN_WIDTH), BF16),
        compiler_params=_cparams(2),
        name="swa_attn",
    )(sinks[0].astype(F32), q, k, k, ksw, ksw, v, v, vsw, vsw)

    h, hpk, route, cnt = pl.pallas_call(
        _outproj_router_kernel,
        grid=(n_tiles,),
        in_specs=[
            pl.BlockSpec((tm, ATTN_WIDTH), row_blk),
            pl.BlockSpec((tm, POOL_WIDTH), row_blk),
            pl.BlockSpec((tm, d), row_blk),
            pl.BlockSpec((ATTN_WIDTH, d), const),
            pl.BlockSpec((POOL_WIDTH, d), const),
            pl.BlockSpec((1, d), const),
            pl.BlockSpec((d, 2 * LANES), const),
            pl.BlockSpec((1, LANES), const),
            pl.BlockSpec((SUB_TILE, SUB_TILE), const),
        ],
        out_specs=[
            pl.BlockSpec((tm, d), row_blk),
            pl.BlockSpec((tm * PK_ROWS, LANES), row_blk),
            pl.BlockSpec((tm, LANES), row_blk),
            pl.BlockSpec((tm // SUB_TILE * SUBLANES, LANES), row_blk),
        ],
        out_shape=[
            jax.ShapeDtypeStruct((n, d), F32),
            jax.ShapeDtypeStruct((n * PK_ROWS, LANES), jnp.uint32),
            jax.ShapeDtypeStruct((n, LANES), F32),
            jax.ShapeDtypeStruct((n_sub * SUBLANES, LANES), F32),
        ],
        compiler_params=_cparams(),
        name="outproj_router",
    )(attn, y, xf, wo_attn, wo_pool, norm_ffn[0].reshape(1, d), w_r2, b_r, tri)

    eid = route[:, :TOP_K].astype(jnp.int32)
    rank = route[:, 4:4 + TOP_K].astype(jnp.int32)
    tile_cnt = cnt.reshape(n_sub, SUBLANES, LANES)[:, 0, :N_EXPERTS].astype(jnp.int32)
    te = EXPERT_TILE
    lp, gch, csrc, blk_eid, first, n_act, n_rows = _plan(eid, rank, tile_cnt, n)
    n_blocks = n_rows // te
    n_local = n // LOCAL_TILE
    gates = route[:, 2:2 + TOP_K].reshape(n * TOP_K)

    xs = pl.pallas_call(
        _local_sort_kernel,
        grid_spec=pltpu.PrefetchScalarGridSpec(
            num_scalar_prefetch=1,
            grid=(n_local,),
            in_specs=[pl.BlockSpec((LOCAL_TILE * PK_ROWS, LANES), lambda i, *_: (i, 0))],
            out_specs=pl.BlockSpec((LT_MAX * PK_ROWS, LANES), lambda i, *_: (i, 0)),
        ),
        out_shape=jax.ShapeDtypeStruct((n_local * LT_MAX * PK_ROWS, LANES), jnp.uint32),
        compiler_params=_cparams(),
        name="local_sort",
    )(lp * PK_ROWS, hpk)

    w_idx = lambda i, eid_r, *_: (eid_r[i], 0, 0)
    ys = pl.pallas_call(
        _expert_kernel,
        grid_spec=pltpu.PrefetchScalarGridSpec(
            num_scalar_prefetch=4,
            grid=(n_blocks,),
            in_specs=[
                pl.BlockSpec(memory_space=pl.ANY),
                pl.BlockSpec((1, d, D_EXPERT), w_idx),
                pl.BlockSpec((1, d, D_EXPERT), w_idx),
                pl.BlockSpec((1, D_EXPERT, d), w_idx),
            ],
            out_specs=pl.BlockSpec((te * PK_ROWS, LANES), lambda i, *_: (i, 0)),
            scratch_shapes=[
                pltpu.VMEM((2, te * PK_ROWS, LANES), jnp.uint32),
                pltpu.VMEM((d, D_EXPERT), BF16),
                pltpu.VMEM((d, D_EXPERT), BF16),
                pltpu.VMEM((D_EXPERT, d), BF16),
                pltpu.SemaphoreType.DMA((2,)),
            ],
        ),
        out_shape=jax.ShapeDtypeStruct((n_rows * PK_ROWS, LANES), jnp.uint32),
        compiler_params=_cparams(vmem_limit=VMEM_LIMIT_EXPERTS),
        name="experts",
    )(blk_eid, first, n_act, csrc * (CHUNK * PK_ROWS), xs, w_gate[0], w_up[0], w_down[0])

    tt = LOCAL_TILE
    out = pl.pallas_call(
        _combine_kernel,
        grid_spec=pltpu.PrefetchScalarGridSpec(
            num_scalar_prefetch=3,
            grid=(n_local,),
            in_specs=[
                pl.BlockSpec((tt, d), lambda i, *_: (i, 0)),
                pl.BlockSpec(memory_space=pl.ANY),
            ],
            out_specs=pl.BlockSpec((tt, d), lambda i, *_: (i, 0)),
            scratch_shapes=[
                pltpu.VMEM((2, LT_MAX * PK_ROWS, LANES), jnp.uint32),
                pltpu.VMEM((tt * Y_ROWS, LANES), F32),
                pltpu.SemaphoreType.DMA((2,)),
            ],
        ),
        out_shape=jax.ShapeDtypeStruct((n, d), F32),
        compiler_params=_cparams(),
        name="combine",
    )(gch * (CHUNK * PK_ROWS), lp * PK_ROWS, gates, h, ys)
    return out.reshape(bsz, seq, d)
```

```python
import functools

import jax
import jax.numpy as jnp
from jax import lax
from jax.experimental import pallas as pl
from jax.experimental.pallas import tpu as pltpu

F32 = jnp.float32
BF16 = jnp.bfloat16

D_MODEL = 2048
N_HEADS = 16
N_KV_HEADS = 4
HEAD_DIM = 64
GROUP = N_HEADS // N_KV_HEADS
ROT_DIM = HEAD_DIM // 4
ROPE_THETA = 500000.0
WINDOW = 128
BLOCK = 128
Q_BLOCKS = 2
ATTN_WIDTH = N_HEADS * HEAD_DIM
KV_WIDTH = N_KV_HEADS * HEAD_DIM
POOL_WINDOWS = (2, 4, 8, 16)
POOL_WIDTH = D_MODEL // 2
POOL_GROUP_DIM = POOL_WIDTH // len(POOL_WINDOWS)
N_EXPERT_GROUPS = 4
EXPERTS_PER_GROUP = 8
N_EXPERTS = N_EXPERT_GROUPS * EXPERTS_PER_GROUP
TOP_K = 2
D_EXPERT = 512
EPS = 1e-6

LANES = 128
SUBLANES = 8
MXU_DIM = 256
HALO = 16
NEG_BIG = -1e30
HALF = D_MODEL // 2
PK_ROWS = HALF // LANES
Y_ROWS = D_MODEL // LANES

ROW_TILE = 512
SUB_TILE = 256
EXPERT_TILE = 512
LOCAL_TILE = 512
CHUNK = 8
LT_MAX = TOP_K * LOCAL_TILE + N_EXPERTS * CHUNK
VMEM_LIMIT = 56 * 1024 * 1024
VMEM_LIMIT_EXPERTS = 60 * 1024 * 1024


def _cparams(n_axes=1, vmem_limit=VMEM_LIMIT):
    return pltpu.CompilerParams(
        dimension_semantics=("arbitrary",) * n_axes,
        vmem_limit_bytes=vmem_limit,
    )


def _pack_halves(t):
    t32 = t.astype(BF16).astype(F32)
    lo_bits = lax.shift_right_logical(
        pltpu.bitcast(t32[:, :HALF], jnp.uint32), jnp.uint32(16))
    hi_bits = pltpu.bitcast(t32[:, HALF:], jnp.uint32) & jnp.uint32(0xFFFF0000)
    return lo_bits | hi_bits


def _unpack_halves(w):
    lo = pltpu.bitcast(lax.shift_left(w, jnp.uint32(16)), F32)
    hi = pltpu.bitcast(w & jnp.uint32(0xFFFF0000), F32)
    return lo, hi


def _head_sumsq(t, e_ref):
    t2 = (t * t).astype(BF16)
    e = e_ref[...]
    outs = []
    for c in range(t.shape[1] // MXU_DIM):
        sl = slice(c * MXU_DIM, (c + 1) * MXU_DIM)
        outs.append(jnp.dot(t2[:, sl], e, preferred_element_type=F32))
    return outs[0] if len(outs) == 1 else jnp.concatenate(outs, axis=1)


def _norm_rope(t, gain, e_ref, cc, s_up, s_dn, out_scale):
    w = t.shape[1]
    ss = _head_sumsq(t, e_ref)
    tn = t * lax.rsqrt(ss * (1.0 / HEAD_DIM) + EPS) * gain
    reps = w // LANES
    up = pltpu.roll(tn, w - ROT_DIM // 2, axis=1)
    dn = pltpu.roll(tn, ROT_DIM // 2, axis=1)
    c = jnp.concatenate([cc] * reps, axis=1)
    su = jnp.concatenate([s_up] * reps, axis=1)
    sd = jnp.concatenate([s_dn] * reps, axis=1)
    return (tn * c + up * su + dn * sd) * out_scale


def _swap_head_pairs(t):
    w = t.shape[1]
    lane = lax.broadcasted_iota(jnp.int32, t.shape, 1)
    from_up = pltpu.roll(t, w - HEAD_DIM, axis=1)
    from_dn = pltpu.roll(t, HEAD_DIM, axis=1)
    return jnp.where((lane & (LANES - 1)) < HEAD_DIM, from_up, from_dn)


def _inproj_kernel(x_ref, g_ref, wq_ref, wk_ref, wv_ref, wu_ref, qn_ref, kn_ref,
                   cc_ref, su_ref, sd_ref, e_ref, wpool_ref, pscale_ref,
                   q_out, k_out, v_out, ksw_out, vsw_out, y_out, ubuf, *, tiles_per_seq):
    i = pl.program_id(0)
    tm = x_ref.shape[0]

    @pl.when(i % tiles_per_seq == 0)
    def _():
        ubuf[0:HALO, :] = jnp.zeros((HALO, POOL_WIDTH), F32)

    for r0 in range(0, tm, SUB_TILE):
        rows = slice(r0, r0 + SUB_TILE)
        x = x_ref[rows, :]
        ms = jnp.mean(x * x, axis=-1, keepdims=True)
        hn = (x * lax.rsqrt(ms + EPS) * g_ref[...]).astype(BF16)

        cc, s_up, s_dn = cc_ref[rows, :], su_ref[rows, :], sd_ref[rows, :]
        q = jnp.dot(hn, wq_ref[...], preferred_element_type=F32)
        q_out[rows, :] = _norm_rope(q, qn_ref[...], e_ref, cc, s_up, s_dn,
                                    HEAD_DIM ** -0.5).astype(BF16)
        k = jnp.dot(hn, wk_ref[...], preferred_element_type=F32)
        k = _norm_rope(k, kn_ref[...], e_ref, cc, s_up, s_dn, 1.0)
        v = jnp.dot(hn, wv_ref[...], preferred_element_type=F32)
        k_out[rows, :] = k.astype(BF16)
        v_out[rows, :] = v.astype(BF16)
        ksw_out[rows, :] = _swap_head_pairs(k).astype(BF16)
        vsw_out[rows, :] = _swap_head_pairs(v).astype(BF16)

        base = HALO + r0
        ubuf[base:base + SUB_TILE, :] = jnp.dot(hn, wu_ref[...], preferred_element_type=F32)
        pos = (i % tiles_per_seq) * tm + r0 + lax.broadcasted_iota(
            jnp.int32, (SUB_TILE, POOL_GROUP_DIM), 0)
        for g, w in enumerate(POOL_WINDOWS):
            cols = slice(g * POOL_GROUP_DIM, (g + 1) * POOL_GROUP_DIM)
            u_g = ubuf[base:base + SUB_TILE, cols]
            acc = u_g
            for j in range(1, w):
                acc = acc + ubuf[base - j:base - j + SUB_TILE, cols]
            cnt = jnp.minimum(pos + 1, w).astype(F32)
            d = (acc / cnt - u_g).astype(BF16)
            y = jnp.dot(d, wpool_ref[g], preferred_element_type=F32)
            y_out[rows, cols] = (y * pscale_ref[:, cols]).astype(BF16)
    ubuf[0:HALO, :] = ubuf[tm:tm + HALO, :]


def _attn_kernel(sinks_ref, q_ref, kc_ref, kp_ref, kswc_ref, kswp_ref,
                 vc_ref, vp_ref, vswc_ref, vswp_ref, o_ref):
    j = pl.program_id(1)
    two = 2 * BLOCK
    row = lax.broadcasted_iota(jnp.int32, (two, two), 0) & (BLOCK - 1)
    col = lax.broadcasted_iota(jnp.int32, (two, two), 1)
    diff = row + BLOCK - col
    local = (diff >= 0) & (diff < WINDOW)
    lo_lanes = lax.broadcasted_iota(jnp.int32, (two, LANES), 1) < HEAD_DIM
    top_rows = lax.broadcasted_iota(jnp.int32, (two, 1), 0) < BLOCK
    zero = jnp.zeros((two, LANES), BF16)

    for sb in range(q_ref.shape[0] // BLOCK):
        rows = slice(sb * BLOCK, (sb + 1) * BLOCK)
        if sb == 0:
            def band(cur_ref, prev_ref):
                return jnp.concatenate([prev_ref[...], cur_ref[0:BLOCK, :]], axis=0)
            valid = local & ((col >= BLOCK) | (j > 0))
        else:
            def band(cur_ref, prev_ref, sb=sb):
                return cur_ref[(sb - 1) * BLOCK:(sb + 1) * BLOCK, :]
            valid = local
        k_nat, k_swp = band(kc_ref, kp_ref), band(kswc_ref, kswp_ref)
        v_nat, v_swp = band(vc_ref, vp_ref), band(vswc_ref, vswp_ref)

        for g in range(N_KV_HEADS):
            c0 = (g // 2) * LANES
            kcols = (k_nat[:, c0:c0 + LANES], k_swp[:, c0:c0 + LANES])
            vcols = (v_nat[:, c0:c0 + LANES], v_swp[:, c0:c0 + LANES])
            in_lo, in_hi = (0, 1) if g % 2 == 0 else (1, 0)
            q0 = g * GROUP * HEAD_DIM
            qq = jnp.concatenate([q_ref[rows, q0:q0 + LANES],
                                  q_ref[rows, q0 + LANES:q0 + 2 * LANES]], axis=0)
            halves = []
            for half, src in ((0, in_lo), (1, in_hi)):
                qm = jnp.where(lo_lanes, qq, zero) if half == 0 else jnp.where(lo_lanes, zero, qq)
                s = lax.dot_general(qm, kcols[src], (((1,), (1,)), ((), ())),
                                    preferred_element_type=F32)
                s = jnp.where(valid, s, NEG_BIG)
                h_top = g * GROUP + half
                sink = jnp.where(top_rows, sinks_ref[h_top], sinks_ref[h_top + 2])
                m = jnp.maximum(jnp.max(s, axis=-1, keepdims=True), sink)
                p = jnp.exp(s - m)
                denom = jnp.sum(p, axis=-1, keepdims=True) + jnp.exp(sink - m)
                o = jnp.dot(p.astype(BF16), vcols[src], preferred_element_type=F32)
                halves.append(o / denom)
            o_pair = jnp.where(lo_lanes, halves[0], halves[1]).astype(BF16)
            o_ref[rows, q0:q0 + LANES] = o_pair[0:BLOCK, :]
            o_ref[rows, q0 + LANES:q0 + 2 * LANES] = o_pair[BLOCK:, :]


def _outproj_router_kernel(o_ref, y_ref, x_ref, woa_ref, wop_ref, g_ref, wr_ref,
                           br_ref, tri_ref, h_out, hpk_out, route_out, cnt_out):
    n_sub = x_ref.shape[0] // SUB_TILE
    picks = [_outproj_router_subtile(sub, o_ref, y_ref, x_ref, woa_ref, wop_ref, g_ref,
                                     wr_ref, br_ref, tri_ref, h_out, hpk_out)
             for sub in range(n_sub)]

    lane = lax.broadcasted_iota(jnp.int32, (SUB_TILE, LANES), 1)
    lane_f = lane.astype(F32)
    cnt_tile = sum(pk[6] for pk in picks)
    cnt_pad = jnp.floor((cnt_tile + (CHUNK - 1)) * (1.0 / CHUNK)) * CHUNK
    run_start = lax.dot_general(
        jnp.broadcast_to(cnt_pad, (SUBLANES, LANES)).astype(BF16),
        tri_ref[0:LANES, 0:LANES], (((1,), (1,)), ((), ())),
        preferred_element_type=F32)[0:1, :]
    before = jnp.zeros((1, LANES), F32)
    for sub, (e1, e2, g1, g2, r1, r2, cnt_sub) in enumerate(picks):
        table = run_start + before
        lp1 = jnp.sum(jnp.where(lane_f == e1, table, 0.0), axis=-1, keepdims=True) + r1
        lp2 = jnp.sum(jnp.where(lane_f == e2, table, 0.0), axis=-1, keepdims=True) + r2
        route_out[sub * SUB_TILE:(sub + 1) * SUB_TILE, :] = jnp.where(
            lane == 0, e1,
            jnp.where(lane == 1, e2,
                      jnp.where(lane == 2, g1,
                                jnp.where(lane == 3, g2,
                                          jnp.where(lane == 4, lp1,
                                                    jnp.where(lane == 5, lp2, 0.0))))))
        before = before + cnt_sub
    cnt_out[...] = jnp.broadcast_to(cnt_tile, cnt_out.shape)


def _outproj_router_subtile(sub, o_ref, y_ref, x_ref, woa_ref, wop_ref, g_ref, wr_ref,
                            br_ref, tri_ref, h_out, hpk_out):
    tm = SUB_TILE
    r0 = sub * SUB_TILE
    rows = slice(r0, r0 + SUB_TILE)
    h = (x_ref[rows, :]
         + jnp.dot(o_ref[rows, :], woa_ref[...], preferred_element_type=F32)
         + jnp.dot(y_ref[rows, :], wop_ref[...], preferred_element_type=F32))
    h_out[rows, :] = h
    ms = jnp.mean(h * h, axis=-1, keepdims=True)
    hn = h * lax.rsqrt(ms + EPS) * g_ref[...]
    hb = hn.astype(BF16)
    packed = _pack_halves(hn)
    for s in range(PK_ROWS):
        hpk_out[pl.ds(r0 * PK_ROWS + s, tm, stride=PK_ROWS), :] = (
            packed[:, s * LANES:(s + 1) * LANES])

    both = jnp.dot(hb, wr_ref[...], preferred_element_type=F32)
    logits = both[:, :LANES] + both[:, LANES:] + br_ref[...]
    lane = lax.broadcasted_iota(jnp.int32, (tm, LANES), 1)
    lane_f = lane.astype(F32)

    def first_lane_of(mask):
        return jnp.min(jnp.where(mask, lane_f, float(LANES)), axis=-1, keepdims=True)

    def row_max(t):
        return jnp.max(t, axis=-1, keepdims=True)

    coarse = jnp.where(lane < N_EXPERT_GROUPS, logits, NEG_BIG)
    fine = []
    for g in range(N_EXPERT_GROUPS):
        lo = N_EXPERT_GROUPS + g * EXPERTS_PER_GROUP
        fine.append(jnp.where((lane >= lo) & (lane < lo + EXPERTS_PER_GROUP), logits, NEG_BIG))
    cmax = row_max(coarse)
    f1s = [row_max(f) for f in fine]
    grp = first_lane_of(coarse == cmax)
    csum = jnp.sum(jnp.exp(coarse - cmax), axis=-1, keepdims=True)
    i1s = [first_lane_of(f == f1) for f, f1 in zip(fine, f1s)]
    rest = [jnp.where(lane_f == i1, NEG_BIG, f) for f, i1 in zip(fine, i1s)]
    f2s = [row_max(f) for f in rest]
    i2s = [first_lane_of(f == f2) for f, f2 in zip(rest, f2s)]

    def of_group(vals):
        out = vals[N_EXPERT_GROUPS - 1]
        for g in range(N_EXPERT_GROUPS - 2, -1, -1):
            out = jnp.where(grp == float(g), vals[g], out)
        return out

    f1, i1, f2, i2 = of_group(f1s), of_group(i1s), of_group(f2s), of_group(i2s)
    ratio = jnp.exp(f2 - f1)
    scale = (1.0 / csum) / (1.0 + ratio)
    g1 = scale
    g2 = scale * ratio
    e1 = i1 - N_EXPERT_GROUPS
    e2 = i2 - N_EXPERT_GROUPS

    oh1 = lane_f == e1
    oh2 = lane_f == e2
    oh = jnp.where(oh1, 1.0, 0.0) + jnp.where(oh2, 1.0, 0.0)
    prefix = jnp.dot(tri_ref[...], oh.astype(BF16), preferred_element_type=F32)
    r1 = jnp.sum(jnp.where(oh1, prefix, 0.0), axis=-1, keepdims=True)
    r2 = jnp.sum(jnp.where(oh2, prefix, 0.0), axis=-1, keepdims=True)
    return e1, e2, g1, g2, r1, r2, jnp.sum(oh, axis=0, keepdims=True)


def _expert_kernel(eid_ref, first_ref, nact_ref, off_ref,
                   hpk_hbm, wg_ref, wu_ref, wd_ref, ys_out,
                   xbuf, wg_s, wu_s, wd_s, sems):
    i = pl.program_id(0)
    tm = ys_out.shape[0] // PK_ROWS
    n_act = nact_ref[0]

    ch_rows = CHUNK * PK_ROWS
    n_ch = tm // CHUNK

    def gather(blk, slot):
        for c in range(n_ch):
            src = pl.multiple_of(off_ref[blk * n_ch + c], ch_rows)
            pltpu.make_async_copy(hpk_hbm.at[pl.ds(src, ch_rows)],
                                  xbuf.at[slot, pl.ds(c * ch_rows, ch_rows)],
                                  sems.at[slot]).start()

    @pl.when(i == 0)
    def _():
        gather(0, 0)

    @pl.when(i + 1 < n_act)
    def _():
        gather(i + 1, (i + 1) % 2)

    @pl.when(i < n_act)
    def _():
        slot = i % 2

        @pl.when(first_ref[i] == 1)
        def _():
            wg_s[...] = wg_ref[0].astype(BF16)
            wu_s[...] = wu_ref[0].astype(BF16)
            wd_s[...] = wd_ref[0].astype(BF16)

        pltpu.make_async_copy(hpk_hbm.at[pl.ds(0, tm * PK_ROWS)], xbuf.at[slot],
                              sems.at[slot]).wait()
        xu = jnp.concatenate(
            [xbuf[slot, pl.ds(s, tm, stride=PK_ROWS), :] for s in range(PK_ROWS)],
            axis=1)
        x_lo, x_hi = (v.astype(BF16) for v in _unpack_halves(xu))
        a = (jnp.dot(x_lo, wg_s[0:HALF, :], preferred_element_type=F32)
             + jnp.dot(x_hi, wg_s[HALF:, :], preferred_element_type=F32))
        b = (jnp.dot(x_lo, wu_s[0:HALF, :], preferred_element_type=F32)
             + jnp.dot(x_hi, wu_s[HALF:, :], preferred_element_type=F32))
        mid = (a * jax.nn.sigmoid(a) * b).astype(BF16)
        y = jnp.dot(mid, wd_s[...], preferred_element_type=F32)
        yw = _pack_halves(y)
        for s in range(PK_ROWS):
            ys_out[pl.ds(s, tm, stride=PK_ROWS), :] = yw[:, s * LANES:(s + 1) * LANES]

    @pl.when(i >= n_act)
    def _():
        ys_out[...] = jnp.zeros(ys_out.shape, ys_out.dtype)


def _local_sort_kernel(lp_ref, hpk_ref, xs_out):
    i = pl.program_id(0)
    t_tile = hpk_ref.shape[0] // PK_ROWS
    xs_out[...] = jnp.zeros(xs_out.shape, xs_out.dtype)

    def body(q, carry):
        for u in range(SUBLANES):
            t = q * SUBLANES + u
            row = hpk_ref[pl.ds(pl.multiple_of(t * PK_ROWS, PK_ROWS), PK_ROWS), :]
            for k in range(TOP_K):
                dst = pl.multiple_of(lp_ref[(i * t_tile + t) * TOP_K + k], PK_ROWS)
                xs_out[pl.ds(dst, PK_ROWS), :] = row
        return carry
    lax.fori_loop(0, t_tile // SUBLANES, body, 0)


def _combine_kernel(gch_ref, lp_ref, gate_ref, h_ref, ys_hbm, out_ref, ylocal, acc, sems):
    i = pl.program_id(0)
    n = pl.num_programs(0)
    t_tile = h_ref.shape[0]
    ch_rows = CHUNK * PK_ROWS
    n_ch = ylocal.shape[1] // ch_rows

    def gather(tile, slot):
        def body(q, carry):
            for u in range(SUBLANES):
                lc = q * SUBLANES + u
                src = pl.multiple_of(gch_ref[tile * n_ch + lc], ch_rows)
                dst = pl.multiple_of(lc * ch_rows, ch_rows)
                pltpu.make_async_copy(ys_hbm.at[pl.ds(src, ch_rows)],
                                      ylocal.at[slot, pl.ds(dst, ch_rows)],
                                      sems.at[slot]).start()
            return carry
        lax.fori_loop(0, n_ch // SUBLANES, body, 0)

    @pl.when(i == 0)
    def _():
        gather(0, 0)

    @pl.when(i + 1 < n)
    def _():
        gather(i + 1, (i + 1) % 2)

    slot = i % 2
    pltpu.make_async_copy(ys_hbm.at[pl.ds(0, n_ch * ch_rows)], ylocal.at[slot],
                          sems.at[slot]).wait()

    def tok_body(q, carry):
        for u in range(4):
            t = q * 4 + u
            a0 = (i * t_tile + t) * TOP_K
            p0 = pl.multiple_of(lp_ref[a0], PK_ROWS)
            p1 = pl.multiple_of(lp_ref[a0 + 1], PK_ROWS)
            lo0, hi0 = _unpack_halves(ylocal[slot, pl.ds(p0, PK_ROWS), :])
            lo1, hi1 = _unpack_halves(ylocal[slot, pl.ds(p1, PK_ROWS), :])
            g0 = gate_ref[a0]
            g1 = gate_ref[a0 + 1]
            dst = pl.multiple_of(t * Y_ROWS, Y_ROWS)
            acc[pl.ds(dst, PK_ROWS), :] = g0 * lo0 + g1 * lo1
            acc[pl.ds(dst + PK_ROWS, PK_ROWS), :] = g0 * hi0 + g1 * hi1
        return carry
    lax.fori_loop(0, t_tile // 4, tok_body, 0)

    for s in range(Y_ROWS):
        cols = slice(s * LANES, (s + 1) * LANES)
        out_ref[:, cols] = h_ref[:, cols] + acc[pl.ds(s, t_tile, stride=Y_ROWS), :]


def _rope_tables(seq):
    pos = jnp.arange(seq, dtype=F32)
    inv_freq = ROPE_THETA ** (-jnp.arange(0, ROT_DIM, 2, dtype=F32) / ROT_DIM)
    ang = pos[:, None] * inv_freq[None, :]
    cos, sin = jnp.cos(ang), jnp.sin(ang)
    half = ROT_DIM // 2
    ones = jnp.ones((seq, HEAD_DIM - ROT_DIM), F32)
    zeros = jnp.zeros((seq, HEAD_DIM - ROT_DIM), F32)
    zh = jnp.zeros((seq, half), F32)
    cc = jnp.concatenate([cos, cos, ones], axis=1)
    s_up = jnp.concatenate([-sin, zh, zeros], axis=1)
    s_dn = jnp.concatenate([zh, sin, zeros], axis=1)
    rep = LANES // HEAD_DIM
    return tuple(jnp.tile(t, (1, rep)) for t in (cc, s_up, s_dn))


def _plan(cnt, n_tokens):
    nt = n_tokens // LOCAL_TILE
    experts = jnp.arange(N_EXPERTS, dtype=jnp.int32)
    cntp = ((cnt + CHUNK - 1) // CHUNK) * CHUNK
    loff_end = jnp.cumsum(cntp, axis=1)
    loff = loff_end - cntp
    rows_e = jnp.sum(cntp, axis=0)
    rows_pad = ((rows_e + EXPERT_TILE - 1) // EXPERT_TILE) * EXPERT_TILE
    g_end = jnp.cumsum(rows_pad)
    g_start = g_end - rows_pad
    gpos = g_start[None, :] + jnp.cumsum(cntp, axis=0) - cntp

    lrow = jnp.arange(LT_MAX // CHUNK, dtype=jnp.int32) * CHUNK
    e_lc = jnp.sum((lrow[None, :, None] >= loff_end[:, None, :]).astype(jnp.int32), axis=-1)
    sel = e_lc[:, :, None] == experts
    delta = jnp.sum(jnp.where(sel, (gpos - loff)[:, None, :], 0), axis=-1)
    gch = jnp.where(e_lc < N_EXPERTS, (delta + lrow[None, :]) // CHUNK, 0).reshape(-1)

    n_rows = TOP_K * n_tokens + nt * N_EXPERTS * (CHUNK - 1) + N_EXPERTS * (EXPERT_TILE - CHUNK)
    n_blocks = -(-n_rows // EXPERT_TILE)
    n_rows = n_blocks * EXPERT_TILE
    blk_start = jnp.arange(n_blocks, dtype=jnp.int32) * EXPERT_TILE
    blk_eid = jnp.minimum(
        jnp.sum((blk_start[:, None] >= g_end[None, :]).astype(jnp.int32), axis=-1),
        N_EXPERTS - 1)
    prev = jnp.concatenate([jnp.full((1,), -1, jnp.int32), blk_eid[:-1]])
    first = (blk_eid != prev).astype(jnp.int32)
    n_act = (g_end[-1] // EXPERT_TILE).astype(jnp.int32).reshape(1)

    cpb = EXPERT_TILE // CHUNK
    run_start = gpos.T // CHUNK
    run_len = cntp.T // CHUNK
    run_src = (jnp.arange(nt, dtype=jnp.int32)[:, None] * (LT_MAX // CHUNK) + loff // CHUNK).T
    tabs = jnp.stack([run_start, run_len, run_src], axis=0)
    dtabs = tabs - jnp.concatenate(
        [jnp.zeros((3, N_EXPERTS, 1), jnp.int32), tabs[:, :, :-1]], axis=2)
    blk_sel = (blk_eid[:, None] == experts)[:, :, None]
    blk_start_t = jnp.sum(jnp.where(blk_sel, run_start[None], 0), axis=1)
    blk_dtabs = jnp.sum(jnp.where(blk_sel[None], dtabs[:, None], 0), axis=2)
    gc = jnp.arange(n_rows // CHUNK, dtype=jnp.int32).reshape(n_blocks, cpb)
    started = blk_start_t[:, None, :] <= gc[:, :, None]
    picked = jnp.sum(jnp.where(started[None], blk_dtabs[:, :, None, :], 0), axis=-1)
    within = gc - picked[0]
    csrc = jnp.where(within < picked[1], picked[2] + within, 0).reshape(-1)
    return gch.astype(jnp.int32), csrc.astype(jnp.int32), blk_eid, first, n_act, n_rows


def kernel(x, norm_mix, w_in, q_norm, k_norm, sinks, w_pool, pool_scale, w_out,
           norm_ffn, w_coarse, b_coarse, w_fine, b_fine, w_gate, w_up, w_down):
    bsz, seq, d = x.shape
    n = bsz * seq
    assert d == D_MODEL and seq % ROW_TILE == 0 and seq % BLOCK == 0
    assert norm_mix.shape[0] == 1, "single-layer problem"
    xf = x.reshape(n, d)

    w_in_b = w_in[0].astype(BF16)
    wq = w_in_b[:, :ATTN_WIDTH]
    wk = w_in_b[:, ATTN_WIDTH:ATTN_WIDTH + KV_WIDTH]
    wv = w_in_b[:, ATTN_WIDTH + KV_WIDTH:ATTN_WIDTH + 2 * KV_WIDTH]
    wu = w_in_b[:, ATTN_WIDTH + 2 * KV_WIDTH:]
    qn = jnp.tile(q_norm[0], N_HEADS).reshape(1, ATTN_WIDTH)
    kn = jnp.tile(k_norm[0], N_KV_HEADS).reshape(1, KV_WIDTH)
    cc, s_up, s_dn = _rope_tables(seq)
    lane_head = jnp.arange(MXU_DIM) // HEAD_DIM
    e_mat = (lane_head[:, None] == lane_head[None, :]).astype(BF16)
    w_pool_b = w_pool[0].astype(BF16)
    pscale = pool_scale[0].reshape(1, POOL_WIDTH)
    w_out_b = w_out[0].astype(BF16)
    wo_attn, wo_pool = w_out_b[:ATTN_WIDTH], w_out_b[ATTN_WIDTH:]
    n_router = N_EXPERT_GROUPS + N_EXPERTS
    w_r = jnp.concatenate([w_coarse[0], w_fine[0]], axis=1)
    w_r = jnp.pad(w_r, ((0, 0), (0, LANES - n_router)))
    w_r_hi = w_r.astype(BF16)
    w_r_lo = (w_r - w_r_hi.astype(F32)).astype(BF16)
    w_r2 = jnp.concatenate([w_r_hi, w_r_lo], axis=1)
    b_r = jnp.pad(jnp.concatenate([b_coarse[0], b_fine[0]]),
                  (0, LANES - n_router)).reshape(1, LANES)

    tm = ROW_TILE
    tiles_per_seq = seq // tm
    n_tiles = n // tm
    n_sub = n // SUB_TILE
    idx = jnp.arange(SUB_TILE)
    tri = (idx[None, :] < idx[:, None]).astype(BF16)
    const = lambda *_: (0, 0)
    row_blk = lambda i: (i, 0)

    q, k, v, ksw, vsw, y = pl.pallas_call(
        functools.partial(_inproj_kernel, tiles_per_seq=tiles_per_seq),
        grid=(n_tiles,),
        in_specs=[
            pl.BlockSpec((tm, d), row_blk),
            pl.BlockSpec((1, d), const),
            pl.BlockSpec((d, ATTN_WIDTH), const),
            pl.BlockSpec((d, KV_WIDTH), const),
            pl.BlockSpec((d, KV_WIDTH), const),
            pl.BlockSpec((d, POOL_WIDTH), const),
            pl.BlockSpec((1, ATTN_WIDTH), const),
            pl.BlockSpec((1, KV_WIDTH), const),
            pl.BlockSpec((tm, LANES), lambda i: (i % tiles_per_seq, 0)),
            pl.BlockSpec((tm, LANES), lambda i: (i % tiles_per_seq, 0)),
            pl.BlockSpec((tm, LANES), lambda i: (i % tiles_per_seq, 0)),
            pl.BlockSpec((MXU_DIM, MXU_DIM), const),
            pl.BlockSpec((len(POOL_WINDOWS), POOL_GROUP_DIM, POOL_GROUP_DIM),
                         lambda i: (0, 0, 0)),
            pl.BlockSpec((1, POOL_WIDTH), const),
        ],
        out_specs=[
            pl.BlockSpec((tm, ATTN_WIDTH), row_blk),
            pl.BlockSpec((tm, KV_WIDTH), row_blk),
            pl.BlockSpec((tm, KV_WIDTH), row_blk),
            pl.BlockSpec((tm, KV_WIDTH), row_blk),
            pl.BlockSpec((tm, KV_WIDTH), row_blk),
            pl.BlockSpec((tm, POOL_WIDTH), row_blk),
        ],
        out_shape=[
            jax.ShapeDtypeStruct((n, ATTN_WIDTH), BF16),
            jax.ShapeDtypeStruct((n, KV_WIDTH), BF16),
            jax.ShapeDtypeStruct((n, KV_WIDTH), BF16),
            jax.ShapeDtypeStruct((n, KV_WIDTH), BF16),
            jax.ShapeDtypeStruct((n, KV_WIDTH), BF16),
            jax.ShapeDtypeStruct((n, POOL_WIDTH), BF16),
        ],
        scratch_shapes=[pltpu.VMEM((HALO + tm, POOL_WIDTH), F32)],
        compiler_params=_cparams(),
        name="inproj",
    )(xf, norm_mix[0].reshape(1, d), wq, wk, wv, wu, qn, kn, cc, s_up, s_dn,
      e_mat, w_pool_b, pscale)

    nb = seq // BLOCK
    nq = nb // Q_BLOCKS
    qrows = Q_BLOCKS * BLOCK
    cur = lambda b, j, *_: (b * nq + j, 0)
    prv = lambda b, j, *_: (b * nb + jnp.maximum(Q_BLOCKS * j - 1, 0), 0)
    kv_specs = [pl.BlockSpec((qrows, KV_WIDTH), cur), pl.BlockSpec((BLOCK, KV_WIDTH), prv)]
    attn = pl.pallas_call(
        _attn_kernel,
        grid_spec=pltpu.PrefetchScalarGridSpec(
            num_scalar_prefetch=1,
            grid=(bsz, nq),
            in_specs=[pl.BlockSpec((qrows, ATTN_WIDTH), cur)] + kv_specs * 4,
            out_specs=pl.BlockSpec((qrows, ATTN_WIDTH), cur),
        ),
        out_shape=jax.ShapeDtypeStruct((n, ATTN_WIDTH), BF16),
        compiler_params=_cparams(2),
        name="swa_attn",
    )(sinks[0].astype(F32), q, k, k, ksw, ksw, v, v, vsw, vsw)

    h, hpk, route, cnt = pl.pallas_call(
        _outproj_router_kernel,
        grid=(n_tiles,),
        in_specs=[
            pl.BlockSpec((tm, ATTN_WIDTH), row_blk),
            pl.BlockSpec((tm, POOL_WIDTH), row_blk),
            pl.BlockSpec((tm, d), row_blk),
            pl.BlockSpec((ATTN_WIDTH, d), const),
            pl.BlockSpec((POOL_WIDTH, d), const),
            pl.BlockSpec((1, d), const),
            pl.BlockSpec((d, 2 * LANES), const),
            pl.BlockSpec((1, LANES), const),
            pl.BlockSpec((SUB_TILE, SUB_TILE), const),
        ],
        out_specs=[
            pl.BlockSpec((tm, d), row_blk),
            pl.BlockSpec((tm * PK_ROWS, LANES), row_blk),
            pl.BlockSpec((tm, LANES), row_blk),
            pl.BlockSpec((SUBLANES, LANES), row_blk),
        ],
        out_shape=[
            jax.ShapeDtypeStruct((n, d), F32),
            jax.ShapeDtypeStruct((n * PK_ROWS, LANES), jnp.uint32),
            jax.ShapeDtypeStruct((n, LANES), F32),
            jax.ShapeDtypeStruct((n_tiles * SUBLANES, LANES), F32),
        ],
        compiler_params=_cparams(),
        name="outproj_router",
    )(attn, y, xf, wo_attn, wo_pool, norm_ffn[0].reshape(1, d), w_r2, b_r, tri)

    assert tm == LOCAL_TILE
    lp = route[:, 4:4 + TOP_K].astype(jnp.int32).reshape(n * TOP_K)
    tile_cnt = cnt.reshape(n_tiles, SUBLANES, LANES)[:, 0, :N_EXPERTS].astype(jnp.int32)
    te = EXPERT_TILE
    gch, csrc, blk_eid, first, n_act, n_rows = _plan(tile_cnt, n)
    n_blocks = n_rows // te
    n_local = n // LOCAL_TILE
    gates = route[:, 2:2 + TOP_K].reshape(n * TOP_K)

    xs = pl.pallas_call(
        _local_sort_kernel,
        grid_spec=pltpu.PrefetchScalarGridSpec(
            num_scalar_prefetch=1,
            grid=(n_local,),
            in_specs=[pl.BlockSpec((LOCAL_TILE * PK_ROWS, LANES), lambda i, *_: (i, 0))],
            out_specs=pl.BlockSpec((LT_MAX * PK_ROWS, LANES), lambda i, *_: (i, 0)),
        ),
        out_shape=jax.ShapeDtypeStruct((n_local * LT_MAX * PK_ROWS, LANES), jnp.uint32),
        compiler_params=_cparams(),
        name="local_sort",
    )(lp * PK_ROWS, hpk)

    w_idx = lambda i, eid_r, *_: (eid_r[i], 0, 0)
    ys = pl.pallas_call(
        _expert_kernel,
        grid_spec=pltpu.PrefetchScalarGridSpec(
            num_scalar_prefetch=4,
            grid=(n_blocks,),
            in_specs=[
                pl.BlockSpec(memory_space=pl.ANY),
                pl.BlockSpec((1, d, D_EXPERT), w_idx),
                pl.BlockSpec((1, d, D_EXPERT), w_idx),
                pl.BlockSpec((1, D_EXPERT, d), w_idx),
            ],
            out_specs=pl.BlockSpec((te * PK_ROWS, LANES), lambda i, *_: (i, 0)),
            scratch_shapes=[
                pltpu.VMEM((2, te * PK_ROWS, LANES), jnp.uint32),
                pltpu.VMEM((d, D_EXPERT), BF16),
                pltpu.VMEM((d, D_EXPERT), BF16),
                pltpu.VMEM((D_EXPERT, d), BF16),
                pltpu.SemaphoreType.DMA((2,)),
            ],
        ),
        out_shape=jax.ShapeDtypeStruct((n_rows * PK_ROWS, LANES), jnp.uint32),
        compiler_params=_cparams(vmem_limit=VMEM_LIMIT_EXPERTS),
        name="experts",
    )(blk_eid, first, n_act, csrc * (CHUNK * PK_ROWS), xs, w_gate[0], w_up[0], w_down[0])

    tt = LOCAL_TILE
    out = pl.pallas_call(
        _combine_kernel,
        grid_spec=pltpu.PrefetchScalarGridSpec(
            num_scalar_prefetch=3,
            grid=(n_local,),
            in_specs=[
                pl.BlockSpec((tt, d), lambda i, *_: (i, 0)),
                pl.BlockSpec(memory_space=pl.ANY),
            ],
            out_specs=pl.BlockSpec((tt, d), lambda i, *_: (i, 0)),
            scratch_shapes=[
                pltpu.VMEM((2, LT_MAX * PK_ROWS, LANES), jnp.uint32),
                pltpu.VMEM((tt * Y_ROWS, LANES), F32),
                pltpu.SemaphoreType.DMA((2,)),
            ],
        ),
        out_shape=jax.ShapeDtypeStruct((n, d), F32),
        compiler_params=_cparams(),
        name="combine",
    )(gch * (CHUNK * PK_ROWS), lp * PK_ROWS, gates, h, ys)
    return out.reshape(bsz, seq, d)
```

```python
import functools

import jax
import jax.numpy as jnp
from jax import lax
from jax.experimental import pallas as pl
from jax.experimental.pallas import tpu as pltpu

F32 = jnp.float32
BF16 = jnp.bfloat16

D_MODEL = 2048
N_HEADS = 16
N_KV_HEADS = 4
HEAD_DIM = 64
GROUP = N_HEADS // N_KV_HEADS
ROT_DIM = HEAD_DIM // 4
ROPE_THETA = 500000.0
WINDOW = 128
BLOCK = 128
Q_BLOCKS = 2
ATTN_WIDTH = N_HEADS * HEAD_DIM
KV_WIDTH = N_KV_HEADS * HEAD_DIM
POOL_WINDOWS = (2, 4, 8, 16)
POOL_WIDTH = D_MODEL // 2
POOL_GROUP_DIM = POOL_WIDTH // len(POOL_WINDOWS)
N_EXPERT_GROUPS = 4
EXPERTS_PER_GROUP = 8
N_EXPERTS = N_EXPERT_GROUPS * EXPERTS_PER_GROUP
TOP_K = 2
D_EXPERT = 512
EPS = 1e-6

LANES = 128
SUBLANES = 8
MXU_DIM = 256
HALO = 16
NEG_BIG = -1e30
HALF = D_MODEL // 2
PK_ROWS = HALF // LANES
Y_ROWS = D_MODEL // LANES

ROW_TILE = 512
SUB_TILE = 512
EXPERT_TILE = 512
LOCAL_TILE = 512
CHUNK = 8
LT_MAX = TOP_K * LOCAL_TILE + N_EXPERTS * CHUNK
VMEM_LIMIT = 56 * 1024 * 1024
VMEM_LIMIT_EXPERTS = 60 * 1024 * 1024


def _cparams(n_axes=1, vmem_limit=VMEM_LIMIT):
    return pltpu.CompilerParams(
        dimension_semantics=("arbitrary",) * n_axes,
        vmem_limit_bytes=vmem_limit,
    )


def _pack_halves(t):
    t32 = t.astype(BF16).astype(F32)
    lo_bits = lax.shift_right_logical(
        pltpu.bitcast(t32[:, :HALF], jnp.uint32), jnp.uint32(16))
    hi_bits = pltpu.bitcast(t32[:, HALF:], jnp.uint32) & jnp.uint32(0xFFFF0000)
    return lo_bits | hi_bits


def _unpack_halves(w):
    lo = pltpu.bitcast(lax.shift_left(w, jnp.uint32(16)), F32)
    hi = pltpu.bitcast(w & jnp.uint32(0xFFFF0000), F32)
    return lo, hi


def _head_sumsq(t, e_ref):
    t2 = (t * t).astype(BF16)
    e = e_ref[...]
    outs = []
    for c in range(t.shape[1] // MXU_DIM):
        sl = slice(c * MXU_DIM, (c + 1) * MXU_DIM)
        outs.append(jnp.dot(t2[:, sl], e, preferred_element_type=F32))
    return outs[0] if len(outs) == 1 else jnp.concatenate(outs, axis=1)


def _norm_rope(t, gain, e_ref, cc, s_up, s_dn, out_scale):
    w = t.shape[1]
    ss = _head_sumsq(t, e_ref)
    tn = t * lax.rsqrt(ss * (1.0 / HEAD_DIM) + EPS) * gain
    reps = w // LANES
    up = pltpu.roll(tn, w - ROT_DIM // 2, axis=1)
    dn = pltpu.roll(tn, ROT_DIM // 2, axis=1)
    c = jnp.concatenate([cc] * reps, axis=1)
    su = jnp.concatenate([s_up] * reps, axis=1)
    sd = jnp.concatenate([s_dn] * reps, axis=1)
    return (tn * c + up * su + dn * sd) * out_scale


def _swap_head_pairs(t):
    w = t.shape[1]
    lane = lax.broadcasted_iota(jnp.int32, t.shape, 1)
    from_up = pltpu.roll(t, w - HEAD_DIM, axis=1)
    from_dn = pltpu.roll(t, HEAD_DIM, axis=1)
    return jnp.where((lane & (LANES - 1)) < HEAD_DIM, from_up, from_dn)


def _inproj_kernel(x_ref, g_ref, wq_ref, wk_ref, wv_ref, wu_ref, qn_ref, kn_ref,
                   cc_ref, su_ref, sd_ref, e_ref, wpool_ref, pscale_ref,
                   q_out, k_out, v_out, ksw_out, vsw_out, y_out, ubuf, *, tiles_per_seq):
    i = pl.program_id(0)
    tm = x_ref.shape[0]

    @pl.when(i % tiles_per_seq == 0)
    def _():
        ubuf[0:HALO, :] = jnp.zeros((HALO, POOL_WIDTH), F32)

    for r0 in range(0, tm, SUB_TILE):
        rows = slice(r0, r0 + SUB_TILE)
        x = x_ref[rows, :]
        ms = jnp.mean(x * x, axis=-1, keepdims=True)
        hn = (x * lax.rsqrt(ms + EPS) * g_ref[...]).astype(BF16)

        cc, s_up, s_dn = cc_ref[rows, :], su_ref[rows, :], sd_ref[rows, :]
        q = jnp.dot(hn, wq_ref[...], preferred_element_type=F32)
        q_out[rows, :] = _norm_rope(q, qn_ref[...], e_ref, cc, s_up, s_dn,
                                    HEAD_DIM ** -0.5).astype(BF16)
        k = jnp.dot(hn, wk_ref[...], preferred_element_type=F32)
        k = _norm_rope(k, kn_ref[...], e_ref, cc, s_up, s_dn, 1.0)
        v = jnp.dot(hn, wv_ref[...], preferred_element_type=F32)
        k_out[rows, :] = k.astype(BF16)
        v_out[rows, :] = v.astype(BF16)
        ksw_out[rows, :] = _swap_head_pairs(k).astype(BF16)
        vsw_out[rows, :] = _swap_head_pairs(v).astype(BF16)

        base = HALO + r0
        ubuf[base:base + SUB_TILE, :] = jnp.dot(hn, wu_ref[...], preferred_element_type=F32)
        pos = (i % tiles_per_seq) * tm + r0 + lax.broadcasted_iota(
            jnp.int32, (SUB_TILE, POOL_GROUP_DIM), 0)
        for g, w in enumerate(POOL_WINDOWS):
            cols = slice(g * POOL_GROUP_DIM, (g + 1) * POOL_GROUP_DIM)
            u_g = ubuf[base:base + SUB_TILE, cols]
            acc = u_g
            for j in range(1, w):
                acc = acc + ubuf[base - j:base - j + SUB_TILE, cols]
            cnt = jnp.minimum(pos + 1, w).astype(F32)
            d = (acc / cnt - u_g).astype(BF16)
            y = jnp.dot(d, wpool_ref[g], preferred_element_type=F32)
            y_out[rows, cols] = (y * pscale_ref[:, cols]).astype(BF16)
    ubuf[0:HALO, :] = ubuf[tm:tm + HALO, :]


def _attn_kernel(sinks_ref, q_ref, kc_ref, kp_ref, kswc_ref, kswp_ref,
                 vc_ref, vp_ref, vswc_ref, vswp_ref, o_ref):
    j = pl.program_id(1)
    two = 2 * BLOCK
    row = lax.broadcasted_iota(jnp.int32, (two, two), 0) & (BLOCK - 1)
    col = lax.broadcasted_iota(jnp.int32, (two, two), 1)
    diff = row + BLOCK - col
    local = (diff >= 0) & (diff < WINDOW)
    lo_lanes = lax.broadcasted_iota(jnp.int32, (two, LANES), 1) < HEAD_DIM
    top_rows = lax.broadcasted_iota(jnp.int32, (two, 1), 0) < BLOCK
    zero = jnp.zeros((two, LANES), BF16)

    for sb in range(q_ref.shape[0] // BLOCK):
        rows = slice(sb * BLOCK, (sb + 1) * BLOCK)
        if sb == 0:
            def band(cur_ref, prev_ref):
                return jnp.concatenate([prev_ref[...], cur_ref[0:BLOCK, :]], axis=0)
            valid = local & ((col >= BLOCK) | (j > 0))
        else:
            def band(cur_ref, prev_ref, sb=sb):
                return cur_ref[(sb - 1) * BLOCK:(sb + 1) * BLOCK, :]
            valid = local
        k_nat, k_swp = band(kc_ref, kp_ref), band(kswc_ref, kswp_ref)
        v_nat, v_swp = band(vc_ref, vp_ref), band(vswc_ref, vswp_ref)

        for g in range(N_KV_HEADS):
            c0 = (g // 2) * LANES
            kcols = (k_nat[:, c0:c0 + LANES], k_swp[:, c0:c0 + LANES])
            vcols = (v_nat[:, c0:c0 + LANES], v_swp[:, c0:c0 + LANES])
            in_lo, in_hi = (0, 1) if g % 2 == 0 else (1, 0)
            q0 = g * GROUP * HEAD_DIM
            qq = jnp.concatenate([q_ref[rows, q0:q0 + LANES],
                                  q_ref[rows, q0 + LANES:q0 + 2 * LANES]], axis=0)
            halves = []
            for half, src in ((0, in_lo), (1, in_hi)):
                qm = jnp.where(lo_lanes, qq, zero) if half == 0 else jnp.where(lo_lanes, zero, qq)
                s = lax.dot_general(qm, kcols[src], (((1,), (1,)), ((), ())),
                                    preferred_element_type=F32)
                s = jnp.where(valid, s, NEG_BIG)
                h_top = g * GROUP + half
                sink = jnp.where(top_rows, sinks_ref[h_top], sinks_ref[h_top + 2])
                m = jnp.maximum(jnp.max(s, axis=-1, keepdims=True), sink)
                p = jnp.exp(s - m)
                denom = jnp.sum(p, axis=-1, keepdims=True) + jnp.exp(sink - m)
                o = jnp.dot(p.astype(BF16), vcols[src], preferred_element_type=F32)
                halves.append(o / denom)
            o_pair = jnp.where(lo_lanes, halves[0], halves[1]).astype(BF16)
            o_ref[rows, q0:q0 + LANES] = o_pair[0:BLOCK, :]
            o_ref[rows, q0 + LANES:q0 + 2 * LANES] = o_pair[BLOCK:, :]


def _outproj_router_kernel(o_ref, y_ref, x_ref, woa_ref, wop_ref, g_ref, wr_ref,
                           br_ref, tri_ref, h_out, hpk_out, route_out, cnt_out):
    n_sub = x_ref.shape[0] // SUB_TILE
    picks = [_outproj_router_subtile(sub, o_ref, y_ref, x_ref, woa_ref, wop_ref, g_ref,
                                     wr_ref, br_ref, tri_ref, h_out, hpk_out)
             for sub in range(n_sub)]

    lane = lax.broadcasted_iota(jnp.int32, (SUB_TILE, LANES), 1)
    lane_f = lane.astype(F32)
    cnt_tile = sum(pk[6] for pk in picks)
    cnt_pad = jnp.floor((cnt_tile + (CHUNK - 1)) * (1.0 / CHUNK)) * CHUNK
    run_start = lax.dot_general(
        jnp.broadcast_to(cnt_pad, (SUBLANES, LANES)).astype(BF16),
        tri_ref[0:LANES, 0:LANES], (((1,), (1,)), ((), ())),
        preferred_element_type=F32)[0:1, :]
    before = jnp.zeros((1, LANES), F32)
    for sub, (e1, e2, g1, g2, r1, r2, cnt_sub) in enumerate(picks):
        table = run_start + before
        lp1 = jnp.sum(jnp.where(lane_f == e1, table, 0.0), axis=-1, keepdims=True) + r1
        lp2 = jnp.sum(jnp.where(lane_f == e2, table, 0.0), axis=-1, keepdims=True) + r2
        route_out[sub * SUB_TILE:(sub + 1) * SUB_TILE, :] = jnp.where(
            lane == 0, e1,
            jnp.where(lane == 1, e2,
                      jnp.where(lane == 2, g1,
                                jnp.where(lane == 3, g2,
                                          jnp.where(lane == 4, lp1,
                                                    jnp.where(lane == 5, lp2, 0.0))))))
        before = before + cnt_sub
    cnt_out[...] = jnp.broadcast_to(cnt_tile, cnt_out.shape)


def _outproj_router_subtile(sub, o_ref, y_ref, x_ref, woa_ref, wop_ref, g_ref, wr_ref,
                            br_ref, tri_ref, h_out, hpk_out):
    tm = SUB_TILE
    r0 = sub * SUB_TILE
    rows = slice(r0, r0 + SUB_TILE)
    h = (x_ref[rows, :]
         + jnp.dot(o_ref[rows, :], woa_ref[...], preferred_element_type=F32)
         + jnp.dot(y_ref[rows, :], wop_ref[...], preferred_element_type=F32))
    h_out[rows, :] = h
    ms = jnp.mean(h * h, axis=-1, keepdims=True)
    hn = h * lax.rsqrt(ms + EPS) * g_ref[...]
    hb = hn.astype(BF16)
    packed = _pack_halves(hn)
    for s in range(PK_ROWS):
        hpk_out[pl.ds(r0 * PK_ROWS + s, tm, stride=PK_ROWS), :] = (
            packed[:, s * LANES:(s + 1) * LANES])

    both = jnp.dot(hb, wr_ref[...], preferred_element_type=F32)
    logits = both[:, :LANES] + both[:, LANES:] + br_ref[...]
    lane = lax.broadcasted_iota(jnp.int32, (tm, LANES), 1)
    lane_f = lane.astype(F32)

    def first_lane_of(mask):
        return jnp.min(jnp.where(mask, lane_f, float(LANES)), axis=-1, keepdims=True)

    def row_max(t):
        return jnp.max(t, axis=-1, keepdims=True)

    coarse = jnp.where(lane < N_EXPERT_GROUPS, logits, NEG_BIG)
    fine = []
    for g in range(N_EXPERT_GROUPS):
        lo = N_EXPERT_GROUPS + g * EXPERTS_PER_GROUP
        fine.append(jnp.where((lane >= lo) & (lane < lo + EXPERTS_PER_GROUP), logits, NEG_BIG))
    cmax = row_max(coarse)
    f1s = [row_max(f) for f in fine]
    grp = first_lane_of(coarse == cmax)
    csum = jnp.sum(jnp.exp(coarse - cmax), axis=-1, keepdims=True)
    i1s = [first_lane_of(f == f1) for f, f1 in zip(fine, f1s)]
    rest = [jnp.where(lane_f == i1, NEG_BIG, f) for f, i1 in zip(fine, i1s)]
    f2s = [row_max(f) for f in rest]
    i2s = [first_lane_of(f == f2) for f, f2 in zip(rest, f2s)]

    def of_group(vals):
        out = vals[N_EXPERT_GROUPS - 1]
        for g in range(N_EXPERT_GROUPS - 2, -1, -1):
            out = jnp.where(grp == float(g), vals[g], out)
        return out

    f1, i1, f2, i2 = of_group(f1s), of_group(i1s), of_group(f2s), of_group(i2s)
    ratio = jnp.exp(f2 - f1)
    scale = (1.0 / csum) / (1.0 + ratio)
    g1 = scale
    g2 = scale * ratio
    e1 = i1 - N_EXPERT_GROUPS
    e2 = i2 - N_EXPERT_GROUPS

    oh1 = lane_f == e1
    oh2 = lane_f == e2
    oh = jnp.where(oh1, 1.0, 0.0) + jnp.where(oh2, 1.0, 0.0)
    prefix = jnp.dot(tri_ref[...], oh.astype(BF16), preferred_element_type=F32)
    r1 = jnp.sum(jnp.where(oh1, prefix, 0.0), axis=-1, keepdims=True)
    r2 = jnp.sum(jnp.where(oh2, prefix, 0.0), axis=-1, keepdims=True)
    return e1, e2, g1, g2, r1, r2, jnp.sum(oh, axis=0, keepdims=True)


def _expert_kernel(eid_ref, first_ref, nact_ref, off_ref,
                   hpk_hbm, wg_ref, wu_ref, wd_ref, ys_out,
                   xbuf, wg_s, wu_s, wd_s, sems):
    i = pl.program_id(0)
    tm = ys_out.shape[0] // PK_ROWS
    n_act = nact_ref[0]

    ch_rows = CHUNK * PK_ROWS
    n_ch = tm // CHUNK

    def gather(blk, slot):
        for c in range(n_ch):
            src = pl.multiple_of(off_ref[blk * n_ch + c], ch_rows)
            pltpu.make_async_copy(hpk_hbm.at[pl.ds(src, ch_rows)],
                                  xbuf.at[slot, pl.ds(c * ch_rows, ch_rows)],
                                  sems.at[slot]).start()

    @pl.when(i == 0)
    def _():
        gather(0, 0)

    @pl.when(i + 1 < n_act)
    def _():
        gather(i + 1, (i + 1) % 2)

    @pl.when(i < n_act)
    def _():
        slot = i % 2

        @pl.when(first_ref[i] == 1)
        def _():
            wg_s[...] = wg_ref[0].astype(BF16)
            wu_s[...] = wu_ref[0].astype(BF16)
            wd_s[...] = wd_ref[0].astype(BF16)

        pltpu.make_async_copy(hpk_hbm.at[pl.ds(0, tm * PK_ROWS)], xbuf.at[slot],
                              sems.at[slot]).wait()
        xu = jnp.concatenate(
            [xbuf[slot, pl.ds(s, tm, stride=PK_ROWS), :] for s in range(PK_ROWS)],
            axis=1)
        x_lo, x_hi = (v.astype(BF16) for v in _unpack_halves(xu))
        a = (jnp.dot(x_lo, wg_s[0:HALF, :], preferred_element_type=F32)
             + jnp.dot(x_hi, wg_s[HALF:, :], preferred_element_type=F32))
        b = (jnp.dot(x_lo, wu_s[0:HALF, :], preferred_element_type=F32)
             + jnp.dot(x_hi, wu_s[HALF:, :], preferred_element_type=F32))
        mid = (a * jax.nn.sigmoid(a) * b).astype(BF16)
        y = jnp.dot(mid, wd_s[...], preferred_element_type=F32)
        yw = _pack_halves(y)
        for s in range(PK_ROWS):
            ys_out[pl.ds(s, tm, stride=PK_ROWS), :] = yw[:, s * LANES:(s + 1) * LANES]

    @pl.when(i >= n_act)
    def _():
        ys_out[...] = jnp.zeros(ys_out.shape, ys_out.dtype)


def _local_sort_kernel(lp_ref, hpk_ref, xs_out):
    i = pl.program_id(0)
    t_tile = hpk_ref.shape[0] // PK_ROWS
    xs_out[...] = jnp.zeros(xs_out.shape, xs_out.dtype)

    def body(q, carry):
        for u in range(SUBLANES):
            t = q * SUBLANES + u
            row = hpk_ref[pl.ds(pl.multiple_of(t * PK_ROWS, PK_ROWS), PK_ROWS), :]
            for k in range(TOP_K):
                dst = pl.multiple_of(lp_ref[(i * t_tile + t) * TOP_K + k], PK_ROWS)
                xs_out[pl.ds(dst, PK_ROWS), :] = row
        return carry
    lax.fori_loop(0, t_tile // SUBLANES, body, 0)


def _combine_kernel(gch_ref, lp_ref, gate_ref, h_ref, ys_hbm, out_ref, ylocal, acc, sems):
    i = pl.program_id(0)
    n = pl.num_programs(0)
    t_tile = h_ref.shape[0]
    ch_rows = CHUNK * PK_ROWS
    n_ch = ylocal.shape[1] // ch_rows

    def gather(tile, slot):
        def body(q, carry):
            for u in range(SUBLANES):
                lc = q * SUBLANES + u
                src = pl.multiple_of(gch_ref[tile * n_ch + lc], ch_rows)
                dst = pl.multiple_of(lc * ch_rows, ch_rows)
                pltpu.make_async_copy(ys_hbm.at[pl.ds(src, ch_rows)],
                                      ylocal.at[slot, pl.ds(dst, ch_rows)],
                                      sems.at[slot]).start()
            return carry
        lax.fori_loop(0, n_ch // SUBLANES, body, 0)

    @pl.when(i == 0)
    def _():
        gather(0, 0)

    @pl.when(i + 1 < n)
    def _():
        gather(i + 1, (i + 1) % 2)

    slot = i % 2
    pltpu.make_async_copy(ys_hbm.at[pl.ds(0, n_ch * ch_rows)], ylocal.at[slot],
                          sems.at[slot]).wait()

    def tok_body(q, carry):
        for u in range(4):
            t = q * 4 + u
            a0 = (i * t_tile + t) * TOP_K
            p0 = pl.multiple_of(lp_ref[a0], PK_ROWS)
            p1 = pl.multiple_of(lp_ref[a0 + 1], PK_ROWS)
            lo0, hi0 = _unpack_halves(ylocal[slot, pl.ds(p0, PK_ROWS), :])
            lo1, hi1 = _unpack_halves(ylocal[slot, pl.ds(p1, PK_ROWS), :])
            g0 = gate_ref[a0]
            g1 = gate_ref[a0 + 1]
            dst = pl.multiple_of(t * Y_ROWS, Y_ROWS)
            acc[pl.ds(dst, PK_ROWS), :] = g0 * lo0 + g1 * lo1
            acc[pl.ds(dst + PK_ROWS, PK_ROWS), :] = g0 * hi0 + g1 * hi1
        return carry
    lax.fori_loop(0, t_tile // 4, tok_body, 0)

    for s in range(Y_ROWS):
        cols = slice(s * LANES, (s + 1) * LANES)
        out_ref[:, cols] = h_ref[:, cols] + acc[pl.ds(s, t_tile, stride=Y_ROWS), :]


def _rope_tables(seq):
    pos = jnp.arange(seq, dtype=F32)
    inv_freq = ROPE_THETA ** (-jnp.arange(0, ROT_DIM, 2, dtype=F32) / ROT_DIM)
    ang = pos[:, None] * inv_freq[None, :]
    cos, sin = jnp.cos(ang), jnp.sin(ang)
    half = ROT_DIM // 2
    ones = jnp.ones((seq, HEAD_DIM - ROT_DIM), F32)
    zeros = jnp.zeros((seq, HEAD_DIM - ROT_DIM), F32)
    zh = jnp.zeros((seq, half), F32)
    cc = jnp.concatenate([cos, cos, ones], axis=1)
    s_up = jnp.concatenate([-sin, zh, zeros], axis=1)
    s_dn = jnp.concatenate([zh, sin, zeros], axis=1)
    rep = LANES // HEAD_DIM
    return tuple(jnp.tile(t, (1, rep)) for t in (cc, s_up, s_dn))


def _plan(cnt, n_tokens):
    nt = n_tokens // LOCAL_TILE
    experts = jnp.arange(N_EXPERTS, dtype=jnp.int32)
    cntp = ((cnt + CHUNK - 1) // CHUNK) * CHUNK
    loff_end = jnp.cumsum(cntp, axis=1)
    loff = loff_end - cntp
    rows_e = jnp.sum(cntp, axis=0)
    rows_pad = ((rows_e + EXPERT_TILE - 1) // EXPERT_TILE) * EXPERT_TILE
    g_end = jnp.cumsum(rows_pad)
    g_start = g_end - rows_pad
    gpos = g_start[None, :] + jnp.cumsum(cntp, axis=0) - cntp

    lrow = jnp.arange(LT_MAX // CHUNK, dtype=jnp.int32) * CHUNK
    e_lc = jnp.sum((lrow[None, :, None] >= loff_end[:, None, :]).astype(jnp.int32), axis=-1)
    sel = e_lc[:, :, None] == experts
    delta = jnp.sum(jnp.where(sel, (gpos - loff)[:, None, :], 0), axis=-1)
    gch = jnp.where(e_lc < N_EXPERTS, (delta + lrow[None, :]) // CHUNK, 0).reshape(-1)

    n_rows = TOP_K * n_tokens + nt * N_EXPERTS * (CHUNK - 1) + N_EXPERTS * (EXPERT_TILE - CHUNK)
    n_blocks = -(-n_rows // EXPERT_TILE)
    n_rows = n_blocks * EXPERT_TILE
    blk_start = jnp.arange(n_blocks, dtype=jnp.int32) * EXPERT_TILE
    blk_eid = jnp.minimum(
        jnp.sum((blk_start[:, None] >= g_end[None, :]).astype(jnp.int32), axis=-1),
        N_EXPERTS - 1)
    prev = jnp.concatenate([jnp.full((1,), -1, jnp.int32), blk_eid[:-1]])
    first = (blk_eid != prev).astype(jnp.int32)
    n_act = (g_end[-1] // EXPERT_TILE).astype(jnp.int32).reshape(1)

    cpb = EXPERT_TILE // CHUNK
    run_start = gpos.T // CHUNK
    run_len = cntp.T // CHUNK
    run_src = (jnp.arange(nt, dtype=jnp.int32)[:, None] * (LT_MAX // CHUNK) + loff // CHUNK).T
    tabs = jnp.stack([run_start, run_len, run_src], axis=0)
    dtabs = tabs - jnp.concatenate(
        [jnp.zeros((3, N_EXPERTS, 1), jnp.int32), tabs[:, :, :-1]], axis=2)
    blk_sel = (blk_eid[:, None] == experts)[:, :, None]
    blk_start_t = jnp.sum(jnp.where(blk_sel, run_start[None], 0), axis=1)
    blk_dtabs = jnp.sum(jnp.where(blk_sel[None], dtabs[:, None], 0), axis=2)
    gc = jnp.arange(n_rows // CHUNK, dtype=jnp.int32).reshape(n_blocks, cpb)
    started = blk_start_t[:, None, :] <= gc[:, :, None]
    picked = jnp.sum(jnp.where(started[None], blk_dtabs[:, :, None, :], 0), axis=-1)
    within = gc - picked[0]
    csrc = jnp.where(within < picked[1], picked[2] + within, 0).reshape(-1)
    return gch.astype(jnp.int32), csrc.astype(jnp.int32), blk_eid, first, n_act, n_rows


def kernel(x, norm_mix, w_in, q_norm, k_norm, sinks, w_pool, pool_scale, w_out,
           norm_ffn, w_coarse, b_coarse, w_fine, b_fine, w_gate, w_up, w_down):
    bsz, seq, d = x.shape
    n = bsz * seq
    assert d == D_MODEL and seq % ROW_TILE == 0 and seq % BLOCK == 0
    assert norm_mix.shape[0] == 1, "single-layer problem"
    xf = x.reshape(n, d)

    w_in_b = w_in[0].astype(BF16)
    wq = w_in_b[:, :ATTN_WIDTH]
    wk = w_in_b[:, ATTN_WIDTH:ATTN_WIDTH + KV_WIDTH]
    wv = w_in_b[:, ATTN_WIDTH + KV_WIDTH:ATTN_WIDTH + 2 * KV_WIDTH]
    wu = w_in_b[:, ATTN_WIDTH + 2 * KV_WIDTH:]
    qn = jnp.tile(q_norm[0], N_HEADS).reshape(1, ATTN_WIDTH)
    kn = jnp.tile(k_norm[0], N_KV_HEADS).reshape(1, KV_WIDTH)
    cc, s_up, s_dn = _rope_tables(seq)
    lane_head = jnp.arange(MXU_DIM) // HEAD_DIM
    e_mat = (lane_head[:, None] == lane_head[None, :]).astype(BF16)
    w_pool_b = w_pool[0].astype(BF16)
    pscale = pool_scale[0].reshape(1, POOL_WIDTH)
    w_out_b = w_out[0].astype(BF16)
    wo_attn, wo_pool = w_out_b[:ATTN_WIDTH], w_out_b[ATTN_WIDTH:]
    n_router = N_EXPERT_GROUPS + N_EXPERTS
    w_r = jnp.concatenate([w_coarse[0], w_fine[0]], axis=1)
    w_r = jnp.pad(w_r, ((0, 0), (0, LANES - n_router)))
    w_r_hi = w_r.astype(BF16)
    w_r_lo = (w_r - w_r_hi.astype(F32)).astype(BF16)
    w_r2 = jnp.concatenate([w_r_hi, w_r_lo], axis=1)
    b_r = jnp.pad(jnp.concatenate([b_coarse[0], b_fine[0]]),
                  (0, LANES - n_router)).reshape(1, LANES)

    tm = ROW_TILE
    tiles_per_seq = seq // tm
    n_tiles = n // tm
    n_sub = n // SUB_TILE
    idx = jnp.arange(SUB_TILE)
    tri = (idx[None, :] < idx[:, None]).astype(BF16)
    const = lambda *_: (0, 0)
    row_blk = lambda i: (i, 0)

    q, k, v, ksw, vsw, y = pl.pallas_call(
        functools.partial(_inproj_kernel, tiles_per_seq=tiles_per_seq),
        grid=(n_tiles,),
        in_specs=[
            pl.BlockSpec((tm, d), row_blk),
            pl.BlockSpec((1, d), const),
            pl.BlockSpec((d, ATTN_WIDTH), const),
            pl.BlockSpec((d, KV_WIDTH), const),
            pl.BlockSpec((d, KV_WIDTH), const),
            pl.BlockSpec((d, POOL_WIDTH), const),
            pl.BlockSpec((1, ATTN_WIDTH), const),
            pl.BlockSpec((1, KV_WIDTH), const),
            pl.BlockSpec((tm, LANES), lambda i: (i % tiles_per_seq, 0)),
            pl.BlockSpec((tm, LANES), lambda i: (i % tiles_per_seq, 0)),
            pl.BlockSpec((tm, LANES), lambda i: (i % tiles_per_seq, 0)),
            pl.BlockSpec((MXU_DIM, MXU_DIM), const),
            pl.BlockSpec((len(POOL_WINDOWS), POOL_GROUP_DIM, POOL_GROUP_DIM),
                         lambda i: (0, 0, 0)),
            pl.BlockSpec((1, POOL_WIDTH), const),
        ],
        out_specs=[
            pl.BlockSpec((tm, ATTN_WIDTH), row_blk),
            pl.BlockSpec((tm, KV_WIDTH), row_blk),
            pl.BlockSpec((tm, KV_WIDTH), row_blk),
            pl.BlockSpec((tm, KV_WIDTH), row_blk),
            pl.BlockSpec((tm, KV_WIDTH), row_blk),
            pl.BlockSpec((tm, POOL_WIDTH), row_blk),
        ],
        out_shape=[
            jax.ShapeDtypeStruct((n, ATTN_WIDTH), BF16),
            jax.ShapeDtypeStruct((n, KV_WIDTH), BF16),
            jax.ShapeDtypeStruct((n, KV_WIDTH), BF16),
            jax.ShapeDtypeStruct((n, KV_WIDTH), BF16),
            jax.ShapeDtypeStruct((n, KV_WIDTH), BF16),
            jax.ShapeDtypeStruct((n, POOL_WIDTH), BF16),
        ],
        scratch_shapes=[pltpu.VMEM((HALO + tm, POOL_WIDTH), F32)],
        compiler_params=_cparams(),
        name="inproj",
    )(xf, norm_mix[0].reshape(1, d), wq, wk, wv, wu, qn, kn, cc, s_up, s_dn,
      e_mat, w_pool_b, pscale)

    nb = seq // BLOCK
    nq = nb // Q_BLOCKS
    qrows = Q_BLOCKS * BLOCK
    cur = lambda b, j, *_: (b * nq + j, 0)
    prv = lambda b, j, *_: (b * nb + jnp.maximum(Q_BLOCKS * j - 1, 0), 0)
    kv_specs = [pl.BlockSpec((qrows, KV_WIDTH), cur), pl.BlockSpec((BLOCK, KV_WIDTH), prv)]
    attn = pl.pallas_call(
        _attn_kernel,
        grid_spec=pltpu.PrefetchScalarGridSpec(
            num_scalar_prefetch=1,
            grid=(bsz, nq),
            in_specs=[pl.BlockSpec((qrows, ATTN_WIDTH), cur)] + kv_specs * 4,
            out_specs=pl.BlockSpec((qrows, ATTN_WIDTH), cur),
        ),
        out_shape=jax.ShapeDtypeStruct((n, ATTN_WIDTH), BF16),
        compiler_params=_cparams(2),
        name="swa_attn",
    )(sinks[0].astype(F32), q, k, k, ksw, ksw, v, v, vsw, vsw)

    h, hpk, route, cnt = pl.pallas_call(
        _outproj_router_kernel,
        grid=(n_tiles,),
        in_specs=[
            pl.BlockSpec((tm, ATTN_WIDTH), row_blk),
            pl.BlockSpec((tm, POOL_WIDTH), row_blk),
            pl.BlockSpec((tm, d), row_blk),
            pl.BlockSpec((ATTN_WIDTH, d), const),
            pl.BlockSpec((POOL_WIDTH, d), const),
            pl.BlockSpec((1, d), const),
            pl.BlockSpec((d, 2 * LANES), const),
            pl.BlockSpec((1, LANES), const),
            pl.BlockSpec((SUB_TILE, SUB_TILE), const),
        ],
        out_specs=[
            pl.BlockSpec((tm, d), row_blk),
            pl.BlockSpec((tm * PK_ROWS, LANES), row_blk),
            pl.BlockSpec((tm, LANES), row_blk),
            pl.BlockSpec((SUBLANES, LANES), row_blk),
        ],
        out_shape=[
            jax.ShapeDtypeStruct((n, d), F32),
            jax.ShapeDtypeStruct((n * PK_ROWS, LANES), jnp.uint32),
            jax.ShapeDtypeStruct((n, LANES), F32),
            jax.ShapeDtypeStruct((n_tiles * SUBLANES, LANES), F32),
        ],
        compiler_params=_cparams(),
        name="outproj_router",
    )(attn, y, xf, wo_attn, wo_pool, norm_ffn[0].reshape(1, d), w_r2, b_r, tri)

    assert tm == LOCAL_TILE
    lp = route[:, 4:4 + TOP_K].astype(jnp.int32).reshape(n * TOP_K)
    tile_cnt = cnt.reshape(n_tiles, SUBLANES, LANES)[:, 0, :N_EXPERTS].astype(jnp.int32)
    te = EXPERT_TILE
    gch, csrc, blk_eid, first, n_act, n_rows = _plan(tile_cnt, n)
    n_blocks = n_rows // te
    n_local = n // LOCAL_TILE
    gates = route[:, 2:2 + TOP_K].reshape(n * TOP_K)

    xs = pl.pallas_call(
        _local_sort_kernel,
        grid_spec=pltpu.PrefetchScalarGridSpec(
            num_scalar_prefetch=1,
            grid=(n_local,),
            in_specs=[pl.BlockSpec((LOCAL_TILE * PK_ROWS, LANES), lambda i, *_: (i, 0))],
            out_specs=pl.BlockSpec((LT_MAX * PK_ROWS, LANES), lambda i, *_: (i, 0)),
        ),
        out_shape=jax.ShapeDtypeStruct((n_local * LT_MAX * PK_ROWS, LANES), jnp.uint32),
        compiler_params=_cparams(),
        name="local_sort",
    )(lp * PK_ROWS, hpk)

    w_idx = lambda i, eid_r, *_: (eid_r[i], 0, 0)
    ys = pl.pallas_call(
        _expert_kernel,
        grid_spec=pltpu.PrefetchScalarGridSpec(
            num_scalar_prefetch=4,
            grid=(n_blocks,),
            in_specs=[
                pl.BlockSpec(memory_space=pl.ANY),
                pl.BlockSpec((1, d, D_EXPERT), w_idx),
                pl.BlockSpec((1, d, D_EXPERT), w_idx),
                pl.BlockSpec((1, D_EXPERT, d), w_idx),
            ],
            out_specs=pl.BlockSpec((te * PK_ROWS, LANES), lambda i, *_: (i, 0)),
            scratch_shapes=[
                pltpu.VMEM((2, te * PK_ROWS, LANES), jnp.uint32),
                pltpu.VMEM((d, D_EXPERT), BF16),
                pltpu.VMEM((d, D_EXPERT), BF16),
                pltpu.VMEM((D_EXPERT, d), BF16),
                pltpu.SemaphoreType.DMA((2,)),
            ],
        ),
        out_shape=jax.ShapeDtypeStruct((n_rows * PK_ROWS, LANES), jnp.uint32),
        compiler_params=_cparams(vmem_limit=VMEM_LIMIT_EXPERTS),
        name="experts",
    )(blk_eid, first, n_act, csrc * (CHUNK * PK_ROWS), xs, w_gate[0], w_up[0], w_down[0])

    tt = LOCAL_TILE
    out = pl.pallas_call(
        _combine_kernel,
        grid_spec=pltpu.PrefetchScalarGridSpec(
            num_scalar_prefetch=3,
            grid=(n_local,),
            in_specs=[
                pl.BlockSpec((tt, d), lambda i, *_: (i, 0)),
                pl.BlockSpec(memory_space=pl.ANY),
            ],
            out_specs=pl.BlockSpec((tt, d), lambda i, *_: (i, 0)),
            scratch_shapes=[
                pltpu.VMEM((2, LT_MAX * PK_ROWS, LANES), jnp.uint32),
                pltpu.VMEM((tt * Y_ROWS, LANES), F32),
                pltpu.SemaphoreType.DMA((2,)),
            ],
        ),
        out_shape=jax.ShapeDtypeStruct((n, d), F32),
        compiler_params=_cparams(),
        name="combine",
    )(gch * (CHUNK * PK_ROWS), lp * PK_ROWS, gates, h, ys)
    return out.reshape(bsz, seq, d)
```

```python
import functools

import jax
import jax.numpy as jnp
from jax import lax
from jax.experimental import pallas as pl
from jax.experimental.pallas import tpu as pltpu

F32 = jnp.float32
BF16 = jnp.bfloat16

D_MODEL = 2048
N_HEADS = 16
N_KV_HEADS = 4
HEAD_DIM = 64
GROUP = N_HEADS // N_KV_HEADS
ROT_DIM = HEAD_DIM // 4
ROPE_THETA = 500000.0
WINDOW = 128
BLOCK = 128
Q_BLOCKS = 2
ATTN_WIDTH = N_HEADS * HEAD_DIM
KV_WIDTH = N_KV_HEADS * HEAD_DIM
POOL_WINDOWS = (2, 4, 8, 16)
POOL_WIDTH = D_MODEL // 2
POOL_GROUP_DIM = POOL_WIDTH // len(POOL_WINDOWS)
N_EXPERT_GROUPS = 4
EXPERTS_PER_GROUP = 8
N_EXPERTS = N_EXPERT_GROUPS * EXPERTS_PER_GROUP
TOP_K = 2
D_EXPERT = 512
EPS = 1e-6

LANES = 128
SUBLANES = 8
MXU_DIM = 256
HALO = 16
NEG_BIG = -1e30
HALF = D_MODEL // 2
PK_ROWS = HALF // LANES
Y_ROWS = D_MODEL // LANES

ROW_TILE = 512
SUB_TILE = 512
EXPERT_TILE = 512
LOCAL_TILE = 512
CHUNK = 8
LT_MAX = TOP_K * LOCAL_TILE + N_EXPERTS * CHUNK
VMEM_LIMIT = 56 * 1024 * 1024
VMEM_LIMIT_EXPERTS = 60 * 1024 * 1024


def _cparams(n_axes=1, vmem_limit=VMEM_LIMIT):
    return pltpu.CompilerParams(
        dimension_semantics=("arbitrary",) * n_axes,
        vmem_limit_bytes=vmem_limit,
    )


def _pack_halves(t):
    return pltpu.pack_elementwise([t[:, :HALF], t[:, HALF:]], packed_dtype=BF16)


def _unpack_halves(w):
    return tuple(pltpu.unpack_elementwise(w, index=i, packed_dtype=BF16, unpacked_dtype=F32)
                 for i in range(2))


def _head_sumsq(t, e_ref):
    t2 = (t * t).astype(BF16)
    e = e_ref[...]
    outs = []
    for c in range(t.shape[1] // MXU_DIM):
        sl = slice(c * MXU_DIM, (c + 1) * MXU_DIM)
        outs.append(jnp.dot(t2[:, sl], e, preferred_element_type=F32))
    return outs[0] if len(outs) == 1 else jnp.concatenate(outs, axis=1)


def _norm_rope(t, gain, e_ref, cc, s_up, s_dn, out_scale):
    w = t.shape[1]
    ss = _head_sumsq(t, e_ref)
    tn = t * lax.rsqrt(ss * (1.0 / HEAD_DIM) + EPS) * gain
    reps = w // LANES
    up = pltpu.roll(tn, w - ROT_DIM // 2, axis=1)
    dn = pltpu.roll(tn, ROT_DIM // 2, axis=1)
    c = jnp.concatenate([cc] * reps, axis=1)
    su = jnp.concatenate([s_up] * reps, axis=1)
    sd = jnp.concatenate([s_dn] * reps, axis=1)
    return (tn * c + up * su + dn * sd) * out_scale


def _swap_head_pairs(t):
    w = t.shape[1]
    lane = lax.broadcasted_iota(jnp.int32, t.shape, 1)
    from_up = pltpu.roll(t, w - HEAD_DIM, axis=1)
    from_dn = pltpu.roll(t, HEAD_DIM, axis=1)
    return jnp.where((lane & (LANES - 1)) < HEAD_DIM, from_up, from_dn)


def _inproj_kernel(x_ref, g_ref, wq_ref, wk_ref, wv_ref, wu_ref, qn_ref, kn_ref,
                   cc_ref, su_ref, sd_ref, e_ref, wpool_ref, pscale_ref,
                   q_out, k_out, v_out, ksw_out, vsw_out, y_out, ubuf, *, tiles_per_seq):
    i = pl.program_id(0)
    tm = x_ref.shape[0]

    @pl.when(i % tiles_per_seq == 0)
    def _():
        ubuf[0:HALO, :] = jnp.zeros((HALO, POOL_WIDTH), F32)

    for r0 in range(0, tm, SUB_TILE):
        rows = slice(r0, r0 + SUB_TILE)
        x = x_ref[rows, :]
        ms = jnp.mean(x * x, axis=-1, keepdims=True)
        hn = (x * lax.rsqrt(ms + EPS) * g_ref[...]).astype(BF16)

        cc, s_up, s_dn = cc_ref[rows, :], su_ref[rows, :], sd_ref[rows, :]
        q = jnp.dot(hn, wq_ref[...], preferred_element_type=F32)
        q_out[rows, :] = _norm_rope(q, qn_ref[...], e_ref, cc, s_up, s_dn,
                                    HEAD_DIM ** -0.5).astype(BF16)
        k = jnp.dot(hn, wk_ref[...], preferred_element_type=F32)
        k = _norm_rope(k, kn_ref[...], e_ref, cc, s_up, s_dn, 1.0)
        v = jnp.dot(hn, wv_ref[...], preferred_element_type=F32)
        k_out[rows, :] = k.astype(BF16)
        v_out[rows, :] = v.astype(BF16)
        ksw_out[rows, :] = _swap_head_pairs(k).astype(BF16)
        vsw_out[rows, :] = _swap_head_pairs(v).astype(BF16)

        base = HALO + r0
        ubuf[base:base + SUB_TILE, :] = jnp.dot(hn, wu_ref[...], preferred_element_type=F32)
        pos = (i % tiles_per_seq) * tm + r0 + lax.broadcasted_iota(
            jnp.int32, (SUB_TILE, POOL_GROUP_DIM), 0)
        for g, w in enumerate(POOL_WINDOWS):
            cols = slice(g * POOL_GROUP_DIM, (g + 1) * POOL_GROUP_DIM)
            u_g = ubuf[base:base + SUB_TILE, cols]
            acc = u_g
            for j in range(1, w):
                acc = acc + ubuf[base - j:base - j + SUB_TILE, cols]
            cnt = jnp.minimum(pos + 1, w).astype(F32)
            d = (acc / cnt - u_g).astype(BF16)
            y = jnp.dot(d, wpool_ref[g], preferred_element_type=F32)
            y_out[rows, cols] = (y * pscale_ref[:, cols]).astype(BF16)
    ubuf[0:HALO, :] = ubuf[tm:tm + HALO, :]


def _attn_kernel(sinks_ref, q_ref, kc_ref, kp_ref, kswc_ref, kswp_ref,
                 vc_ref, vp_ref, vswc_ref, vswp_ref, o_ref):
    j = pl.program_id(1)
    two = 2 * BLOCK
    row = lax.broadcasted_iota(jnp.int32, (two, two), 0) & (BLOCK - 1)
    col = lax.broadcasted_iota(jnp.int32, (two, two), 1)
    diff = row + BLOCK - col
    local = (diff >= 0) & (diff < WINDOW)
    lo_lanes = lax.broadcasted_iota(jnp.int32, (two, LANES), 1) < HEAD_DIM
    top_rows = lax.broadcasted_iota(jnp.int32, (two, 1), 0) < BLOCK
    zero = jnp.zeros((two, LANES), BF16)

    for sb in range(q_ref.shape[0] // BLOCK):
        rows = slice(sb * BLOCK, (sb + 1) * BLOCK)
        if sb == 0:
            def band(cur_ref, prev_ref):
                return jnp.concatenate([prev_ref[...], cur_ref[0:BLOCK, :]], axis=0)
            valid = local & ((col >= BLOCK) | (j > 0))
        else:
            def band(cur_ref, prev_ref, sb=sb):
                return cur_ref[(sb - 1) * BLOCK:(sb + 1) * BLOCK, :]
            valid = local
        k_nat, k_swp = band(kc_ref, kp_ref), band(kswc_ref, kswp_ref)
        v_nat, v_swp = band(vc_ref, vp_ref), band(vswc_ref, vswp_ref)

        for g in range(N_KV_HEADS):
            c0 = (g // 2) * LANES
            kcols = (k_nat[:, c0:c0 + LANES], k_swp[:, c0:c0 + LANES])
            vcols = (v_nat[:, c0:c0 + LANES], v_swp[:, c0:c0 + LANES])
            in_lo, in_hi = (0, 1) if g % 2 == 0 else (1, 0)
            q0 = g * GROUP * HEAD_DIM
            qq = jnp.concatenate([q_ref[rows, q0:q0 + LANES],
                                  q_ref[rows, q0 + LANES:q0 + 2 * LANES]], axis=0)
            halves = []
            for half, src in ((0, in_lo), (1, in_hi)):
                qm = jnp.where(lo_lanes, qq, zero) if half == 0 else jnp.where(lo_lanes, zero, qq)
                s = lax.dot_general(qm, kcols[src], (((1,), (1,)), ((), ())),
                                    preferred_element_type=F32)
                s = jnp.where(valid, s, NEG_BIG)
                h_top = g * GROUP + half
                sink = jnp.where(top_rows, sinks_ref[h_top], sinks_ref[h_top + 2])
                m = jnp.maximum(jnp.max(s, axis=-1, keepdims=True), sink)
                p = jnp.exp(s - m)
                denom = jnp.sum(p, axis=-1, keepdims=True) + jnp.exp(sink - m)
                o = jnp.dot(p.astype(BF16), vcols[src], preferred_element_type=F32)
                halves.append(o / denom)
            o_pair = jnp.where(lo_lanes, halves[0], halves[1]).astype(BF16)
            o_ref[rows, q0:q0 + LANES] = o_pair[0:BLOCK, :]
            o_ref[rows, q0 + LANES:q0 + 2 * LANES] = o_pair[BLOCK:, :]


def _outproj_router_kernel(o_ref, y_ref, x_ref, woa_ref, wop_ref, g_ref, wr_ref,
                           br_ref, tri_ref, h_out, hpk_out, route_out, cnt_out):
    n_sub = x_ref.shape[0] // SUB_TILE
    picks = [_outproj_router_subtile(sub, o_ref, y_ref, x_ref, woa_ref, wop_ref, g_ref,
                                     wr_ref, br_ref, tri_ref, h_out, hpk_out)
             for sub in range(n_sub)]

    lane = lax.broadcasted_iota(jnp.int32, (SUB_TILE, LANES), 1)
    lane_f = lane.astype(F32)
    cnt_tile = sum(pk[6] for pk in picks)
    cnt_pad = jnp.floor((cnt_tile + (CHUNK - 1)) * (1.0 / CHUNK)) * CHUNK
    run_start = lax.dot_general(
        jnp.broadcast_to(cnt_pad, (SUBLANES, LANES)).astype(BF16),
        tri_ref[0:LANES, 0:LANES], (((1,), (1,)), ((), ())),
        preferred_element_type=F32)[0:1, :]
    before = jnp.zeros((1, LANES), F32)
    for sub, (e1, e2, g1, g2, r1, r2, cnt_sub) in enumerate(picks):
        table = run_start + before
        lp1 = jnp.sum(jnp.where(lane_f == e1, table, 0.0), axis=-1, keepdims=True) + r1
        lp2 = jnp.sum(jnp.where(lane_f == e2, table, 0.0), axis=-1, keepdims=True) + r2
        route_out[sub * SUB_TILE:(sub + 1) * SUB_TILE, :] = jnp.where(
            lane == 0, e1,
            jnp.where(lane == 1, e2,
                      jnp.where(lane == 2, g1,
                                jnp.where(lane == 3, g2,
                                          jnp.where(lane == 4, lp1,
                                                    jnp.where(lane == 5, lp2, 0.0))))))
        before = before + cnt_sub
    cnt_out[...] = jnp.broadcast_to(cnt_tile, cnt_out.shape)


def _outproj_router_subtile(sub, o_ref, y_ref, x_ref, woa_ref, wop_ref, g_ref, wr_ref,
                            br_ref, tri_ref, h_out, hpk_out):
    tm = SUB_TILE
    r0 = sub * SUB_TILE
    rows = slice(r0, r0 + SUB_TILE)
    h = (x_ref[rows, :]
         + jnp.dot(o_ref[rows, :], woa_ref[...], preferred_element_type=F32)
         + jnp.dot(y_ref[rows, :], wop_ref[...], preferred_element_type=F32))
    h_out[rows, :] = h
    ms = jnp.mean(h * h, axis=-1, keepdims=True)
    hn = h * lax.rsqrt(ms + EPS) * g_ref[...]
    hb = hn.astype(BF16)
    packed = _pack_halves(hn)
    for s in range(PK_ROWS):
        hpk_out[pl.ds(r0 * PK_ROWS + s, tm, stride=PK_ROWS), :] = (
            packed[:, s * LANES:(s + 1) * LANES])

    both = jnp.dot(hb, wr_ref[...], preferred_element_type=F32)
    logits = both[:, :LANES] + both[:, LANES:] + br_ref[...]
    lane = lax.broadcasted_iota(jnp.int32, (tm, LANES), 1)
    lane_f = lane.astype(F32)

    def first_lane_of(mask):
        return jnp.min(jnp.where(mask, lane_f, float(LANES)), axis=-1, keepdims=True)

    def row_max(t):
        return jnp.max(t, axis=-1, keepdims=True)

    coarse = jnp.where(lane < N_EXPERT_GROUPS, logits, NEG_BIG)
    fine = []
    for g in range(N_EXPERT_GROUPS):
        lo = N_EXPERT_GROUPS + g * EXPERTS_PER_GROUP
        fine.append(jnp.where((lane >= lo) & (lane < lo + EXPERTS_PER_GROUP), logits, NEG_BIG))
    cmax = row_max(coarse)
    f1s = [row_max(f) for f in fine]
    grp = first_lane_of(coarse == cmax)
    csum = jnp.sum(jnp.exp(coarse - cmax), axis=-1, keepdims=True)
    i1s = [first_lane_of(f == f1) for f, f1 in zip(fine, f1s)]
    rest = [jnp.where(lane_f == i1, NEG_BIG, f) for f, i1 in zip(fine, i1s)]
    f2s = [row_max(f) for f in rest]
    i2s = [first_lane_of(f == f2) for f, f2 in zip(rest, f2s)]

    def of_group(vals):
        out = vals[N_EXPERT_GROUPS - 1]
        for g in range(N_EXPERT_GROUPS - 2, -1, -1):
            out = jnp.where(grp == float(g), vals[g], out)
        return out

    f1, i1, f2, i2 = of_group(f1s), of_group(i1s), of_group(f2s), of_group(i2s)
    ratio = jnp.exp(f2 - f1)
    scale = (1.0 / csum) / (1.0 + ratio)
    g1 = scale
    g2 = scale * ratio
    e1 = i1 - N_EXPERT_GROUPS
    e2 = i2 - N_EXPERT_GROUPS

    oh1 = lane_f == e1
    oh2 = lane_f == e2
    oh = jnp.where(oh1, 1.0, 0.0) + jnp.where(oh2, 1.0, 0.0)
    prefix = jnp.dot(tri_ref[...], oh.astype(BF16), preferred_element_type=F32)
    r1 = jnp.sum(jnp.where(oh1, prefix, 0.0), axis=-1, keepdims=True)
    r2 = jnp.sum(jnp.where(oh2, prefix, 0.0), axis=-1, keepdims=True)
    return e1, e2, g1, g2, r1, r2, jnp.sum(oh, axis=0, keepdims=True)


def _expert_kernel(eid_ref, first_ref, nact_ref, off_ref,
                   hpk_hbm, wg_ref, wu_ref, wd_ref, ys_out,
                   xbuf, wg_s, wu_s, wd_s, sems):
    i = pl.program_id(0)
    tm = ys_out.shape[0] // PK_ROWS
    n_act = nact_ref[0]

    ch_rows = CHUNK * PK_ROWS
    n_ch = tm // CHUNK

    def gather(blk, slot):
        for c in range(n_ch):
            src = pl.multiple_of(off_ref[blk * n_ch + c], ch_rows)
            pltpu.make_async_copy(hpk_hbm.at[pl.ds(src, ch_rows)],
                                  xbuf.at[slot, pl.ds(c * ch_rows, ch_rows)],
                                  sems.at[slot]).start()

    @pl.when(i == 0)
    def _():
        gather(0, 0)

    @pl.when(i + 1 < n_act)
    def _():
        gather(i + 1, (i + 1) % 2)

    @pl.when(i < n_act)
    def _():
        slot = i % 2

        @pl.when(first_ref[i] == 1)
        def _():
            wg_s[...] = wg_ref[0].astype(BF16)
            wu_s[...] = wu_ref[0].astype(BF16)
            wd_s[...] = wd_ref[0].astype(BF16)

        pltpu.make_async_copy(hpk_hbm.at[pl.ds(0, tm * PK_ROWS)], xbuf.at[slot],
                              sems.at[slot]).wait()
        xu = jnp.concatenate(
            [xbuf[slot, pl.ds(s, tm, stride=PK_ROWS), :] for s in range(PK_ROWS)],
            axis=1)
        x_lo, x_hi = (v.astype(BF16) for v in _unpack_halves(xu))
        a = (jnp.dot(x_lo, wg_s[0:HALF, :], preferred_element_type=F32)
             + jnp.dot(x_hi, wg_s[HALF:, :], preferred_element_type=F32))
        b = (jnp.dot(x_lo, wu_s[0:HALF, :], preferred_element_type=F32)
             + jnp.dot(x_hi, wu_s[HALF:, :], preferred_element_type=F32))
        mid = (a * jax.nn.sigmoid(a) * b).astype(BF16)
        y = jnp.dot(mid, wd_s[...], preferred_element_type=F32)
        yw = _pack_halves(y)
        for s in range(PK_ROWS):
            ys_out[pl.ds(s, tm, stride=PK_ROWS), :] = yw[:, s * LANES:(s + 1) * LANES]

    @pl.when(i >= n_act)
    def _():
        ys_out[...] = jnp.zeros(ys_out.shape, ys_out.dtype)


def _local_sort_kernel(lp_ref, hpk_ref, xs_out):
    i = pl.program_id(0)
    t_tile = hpk_ref.shape[0] // PK_ROWS
    xs_out[...] = jnp.zeros(xs_out.shape, xs_out.dtype)

    def body(q, carry):
        for u in range(SUBLANES):
            t = q * SUBLANES + u
            row = hpk_ref[pl.ds(pl.multiple_of(t * PK_ROWS, PK_ROWS), PK_ROWS), :]
            for k in range(TOP_K):
                dst = pl.multiple_of(lp_ref[(i * t_tile + t) * TOP_K + k], PK_ROWS)
                xs_out[pl.ds(dst, PK_ROWS), :] = row
        return carry
    lax.fori_loop(0, t_tile // SUBLANES, body, 0)


def _combine_kernel(gch_ref, lp_ref, gate_ref, h_ref, ys_hbm, out_ref, ylocal, acc, sems):
    i = pl.program_id(0)
    n = pl.num_programs(0)
    t_tile = h_ref.shape[0]
    ch_rows = CHUNK * PK_ROWS
    n_ch = ylocal.shape[1] // ch_rows

    def gather(tile, slot):
        def body(q, carry):
            for u in range(SUBLANES):
                lc = q * SUBLANES + u
                src = pl.multiple_of(gch_ref[tile * n_ch + lc], ch_rows)
                dst = pl.multiple_of(lc * ch_rows, ch_rows)
                pltpu.make_async_copy(ys_hbm.at[pl.ds(src, ch_rows)],
                                      ylocal.at[slot, pl.ds(dst, ch_rows)],
                                      sems.at[slot]).start()
            return carry
        lax.fori_loop(0, n_ch // SUBLANES, body, 0)

    @pl.when(i == 0)
    def _():
        gather(0, 0)

    @pl.when(i + 1 < n)
    def _():
        gather(i + 1, (i + 1) % 2)

    slot = i % 2
    pltpu.make_async_copy(ys_hbm.at[pl.ds(0, n_ch * ch_rows)], ylocal.at[slot],
                          sems.at[slot]).wait()

    def tok_body(q, carry):
        for u in range(4):
            t = q * 4 + u
            a0 = (i * t_tile + t) * TOP_K
            p0 = pl.multiple_of(lp_ref[a0], PK_ROWS)
            p1 = pl.multiple_of(lp_ref[a0 + 1], PK_ROWS)
            lo0, hi0 = _unpack_halves(ylocal[slot, pl.ds(p0, PK_ROWS), :])
            lo1, hi1 = _unpack_halves(ylocal[slot, pl.ds(p1, PK_ROWS), :])
            g0 = gate_ref[a0]
            g1 = gate_ref[a0 + 1]
            dst = pl.multiple_of(t * PK_ROWS, PK_ROWS)
            acc[0, pl.ds(dst, PK_ROWS), :] = g0 * lo0 + g1 * lo1
            acc[1, pl.ds(dst, PK_ROWS), :] = g0 * hi0 + g1 * hi1
        return carry
    lax.fori_loop(0, t_tile // 4, tok_body, 0)

    for half in range(2):
        for s in range(PK_ROWS):
            c0 = half * HALF + s * LANES
            out_ref[:, c0:c0 + LANES] = (h_ref[:, c0:c0 + LANES]
                                         + acc[half, pl.ds(s, t_tile, stride=PK_ROWS), :])


def _rope_tables(seq):
    pos = jnp.arange(seq, dtype=F32)
    inv_freq = ROPE_THETA ** (-jnp.arange(0, ROT_DIM, 2, dtype=F32) / ROT_DIM)
    ang = pos[:, None] * inv_freq[None, :]
    cos, sin = jnp.cos(ang), jnp.sin(ang)
    half = ROT_DIM // 2
    ones = jnp.ones((seq, HEAD_DIM - ROT_DIM), F32)
    zeros = jnp.zeros((seq, HEAD_DIM - ROT_DIM), F32)
    zh = jnp.zeros((seq, half), F32)
    cc = jnp.concatenate([cos, cos, ones], axis=1)
    s_up = jnp.concatenate([-sin, zh, zeros], axis=1)
    s_dn = jnp.concatenate([zh, sin, zeros], axis=1)
    rep = LANES // HEAD_DIM
    return tuple(jnp.tile(t, (1, rep)) for t in (cc, s_up, s_dn))


def _plan(cnt, n_tokens):
    nt = n_tokens // LOCAL_TILE
    experts = jnp.arange(N_EXPERTS, dtype=jnp.int32)
    cntp = ((cnt + CHUNK - 1) // CHUNK) * CHUNK
    loff_end = jnp.cumsum(cntp, axis=1)
    loff = loff_end - cntp
    rows_e = jnp.sum(cntp, axis=0)
    rows_pad = ((rows_e + EXPERT_TILE - 1) // EXPERT_TILE) * EXPERT_TILE
    g_end = jnp.cumsum(rows_pad)
    g_start = g_end - rows_pad
    gpos = g_start[None, :] + jnp.cumsum(cntp, axis=0) - cntp

    lrow = jnp.arange(LT_MAX // CHUNK, dtype=jnp.int32) * CHUNK
    e_lc = jnp.sum((lrow[None, :, None] >= loff_end[:, None, :]).astype(jnp.int32), axis=-1)
    sel = e_lc[:, :, None] == experts
    delta = jnp.sum(jnp.where(sel, (gpos - loff)[:, None, :], 0), axis=-1)
    gch = jnp.where(e_lc < N_EXPERTS, (delta + lrow[None, :]) // CHUNK, 0).reshape(-1)

    n_rows = TOP_K * n_tokens + nt * N_EXPERTS * (CHUNK - 1) + N_EXPERTS * (EXPERT_TILE - CHUNK)
    n_blocks = -(-n_rows // EXPERT_TILE)
    n_rows = n_blocks * EXPERT_TILE
    blk_start = jnp.arange(n_blocks, dtype=jnp.int32) * EXPERT_TILE
    blk_eid = jnp.minimum(
        jnp.sum((blk_start[:, None] >= g_end[None, :]).astype(jnp.int32), axis=-1),
        N_EXPERTS - 1)
    prev = jnp.concatenate([jnp.full((1,), -1, jnp.int32), blk_eid[:-1]])
    first = (blk_eid != prev).astype(jnp.int32)
    n_act = (g_end[-1] // EXPERT_TILE).astype(jnp.int32).reshape(1)

    cpb = EXPERT_TILE // CHUNK
    run_start = gpos.T // CHUNK
    run_len = cntp.T // CHUNK
    run_src = (jnp.arange(nt, dtype=jnp.int32)[:, None] * (LT_MAX // CHUNK) + loff // CHUNK).T
    tabs = jnp.stack([run_start, run_len, run_src], axis=0)
    dtabs = tabs - jnp.concatenate(
        [jnp.zeros((3, N_EXPERTS, 1), jnp.int32), tabs[:, :, :-1]], axis=2)
    blk_sel = (blk_eid[:, None] == experts)[:, :, None]
    blk_start_t = jnp.sum(jnp.where(blk_sel, run_start[None], 0), axis=1)
    blk_dtabs = jnp.sum(jnp.where(blk_sel[None], dtabs[:, None], 0), axis=2)
    gc = jnp.arange(n_rows // CHUNK, dtype=jnp.int32).reshape(n_blocks, cpb)
    started = blk_start_t[:, None, :] <= gc[:, :, None]
    picked = jnp.sum(jnp.where(started[None], blk_dtabs[:, :, None, :], 0), axis=-1)
    within = gc - picked[0]
    csrc = jnp.where(within < picked[1], picked[2] + within, 0).reshape(-1)
    return gch.astype(jnp.int32), csrc.astype(jnp.int32), blk_eid, first, n_act, n_rows


def kernel(x, norm_mix, w_in, q_norm, k_norm, sinks, w_pool, pool_scale, w_out,
           norm_ffn, w_coarse, b_coarse, w_fine, b_fine, w_gate, w_up, w_down):
    bsz, seq, d = x.shape
    n = bsz * seq
    assert d == D_MODEL and seq % ROW_TILE == 0 and seq % BLOCK == 0
    assert norm_mix.shape[0] == 1, "single-layer problem"
    xf = x.reshape(n, d)

    w_in_b = w_in[0].astype(BF16)
    wq = w_in_b[:, :ATTN_WIDTH]
    wk = w_in_b[:, ATTN_WIDTH:ATTN_WIDTH + KV_WIDTH]
    wv = w_in_b[:, ATTN_WIDTH + KV_WIDTH:ATTN_WIDTH + 2 * KV_WIDTH]
    wu = w_in_b[:, ATTN_WIDTH + 2 * KV_WIDTH:]
    qn = jnp.tile(q_norm[0], N_HEADS).reshape(1, ATTN_WIDTH)
    kn = jnp.tile(k_norm[0], N_KV_HEADS).reshape(1, KV_WIDTH)
    cc, s_up, s_dn = _rope_tables(seq)
    lane_head = jnp.arange(MXU_DIM) // HEAD_DIM
    e_mat = (lane_head[:, None] == lane_head[None, :]).astype(BF16)
    w_pool_b = w_pool[0].astype(BF16)
    pscale = pool_scale[0].reshape(1, POOL_WIDTH)
    w_out_b = w_out[0].astype(BF16)
    wo_attn, wo_pool = w_out_b[:ATTN_WIDTH], w_out_b[ATTN_WIDTH:]
    n_router = N_EXPERT_GROUPS + N_EXPERTS
    w_r = jnp.concatenate([w_coarse[0], w_fine[0]], axis=1)
    w_r = jnp.pad(w_r, ((0, 0), (0, LANES - n_router)))
    w_r_hi = w_r.astype(BF16)
    w_r_lo = (w_r - w_r_hi.astype(F32)).astype(BF16)
    w_r2 = jnp.concatenate([w_r_hi, w_r_lo], axis=1)
    b_r = jnp.pad(jnp.concatenate([b_coarse[0], b_fine[0]]),
                  (0, LANES - n_router)).reshape(1, LANES)

    tm = ROW_TILE
    tiles_per_seq = seq // tm
    n_tiles = n // tm
    n_sub = n // SUB_TILE
    idx = jnp.arange(SUB_TILE)
    tri = (idx[None, :] < idx[:, None]).astype(BF16)
    const = lambda *_: (0, 0)
    row_blk = lambda i: (i, 0)

    q, k, v, ksw, vsw, y = pl.pallas_call(
        functools.partial(_inproj_kernel, tiles_per_seq=tiles_per_seq),
        grid=(n_tiles,),
        in_specs=[
            pl.BlockSpec((tm, d), row_blk),
            pl.BlockSpec((1, d), const),
            pl.BlockSpec((d, ATTN_WIDTH), const),
            pl.BlockSpec((d, KV_WIDTH), const),
            pl.BlockSpec((d, KV_WIDTH), const),
            pl.BlockSpec((d, POOL_WIDTH), const),
            pl.BlockSpec((1, ATTN_WIDTH), const),
            pl.BlockSpec((1, KV_WIDTH), const),
            pl.BlockSpec((tm, LANES), lambda i: (i % tiles_per_seq, 0)),
            pl.BlockSpec((tm, LANES), lambda i: (i % tiles_per_seq, 0)),
            pl.BlockSpec((tm, LANES), lambda i: (i % tiles_per_seq, 0)),
            pl.BlockSpec((MXU_DIM, MXU_DIM), const),
            pl.BlockSpec((len(POOL_WINDOWS), POOL_GROUP_DIM, POOL_GROUP_DIM),
                         lambda i: (0, 0, 0)),
            pl.BlockSpec((1, POOL_WIDTH), const),
        ],
        out_specs=[
            pl.BlockSpec((tm, ATTN_WIDTH), row_blk),
            pl.BlockSpec((tm, KV_WIDTH), row_blk),
            pl.BlockSpec((tm, KV_WIDTH), row_blk),
            pl.BlockSpec((tm, KV_WIDTH), row_blk),
            pl.BlockSpec((tm, KV_WIDTH), row_blk),
            pl.BlockSpec((tm, POOL_WIDTH), row_blk),
        ],
        out_shape=[
            jax.ShapeDtypeStruct((n, ATTN_WIDTH), BF16),
            jax.ShapeDtypeStruct((n, KV_WIDTH), BF16),
            jax.ShapeDtypeStruct((n, KV_WIDTH), BF16),
            jax.ShapeDtypeStruct((n, KV_WIDTH), BF16),
            jax.ShapeDtypeStruct((n, KV_WIDTH), BF16),
            jax.ShapeDtypeStruct((n, POOL_WIDTH), BF16),
        ],
        scratch_shapes=[pltpu.VMEM((HALO + tm, POOL_WIDTH), F32)],
        compiler_params=_cparams(),
        name="inproj",
    )(xf, norm_mix[0].reshape(1, d), wq, wk, wv, wu, qn, kn, cc, s_up, s_dn,
      e_mat, w_pool_b, pscale)

    nb = seq // BLOCK
    nq = nb // Q_BLOCKS
    qrows = Q_BLOCKS * BLOCK
    cur = lambda b, j, *_: (b * nq + j, 0)
    prv = lambda b, j, *_: (b * nb + jnp.maximum(Q_BLOCKS * j - 1, 0), 0)
    kv_specs = [pl.BlockSpec((qrows, KV_WIDTH), cur), pl.BlockSpec((BLOCK, KV_WIDTH), prv)]
    attn = pl.pallas_call(
        _attn_kernel,
        grid_spec=pltpu.PrefetchScalarGridSpec(
            num_scalar_prefetch=1,
            grid=(bsz, nq),
            in_specs=[pl.BlockSpec((qrows, ATTN_WIDTH), cur)] + kv_specs * 4,
            out_specs=pl.BlockSpec((qrows, ATTN_WIDTH), cur),
        ),
        out_shape=jax.ShapeDtypeStruct((n, ATTN_WIDTH), BF16),
        compiler_params=_cparams(2),
        name="swa_attn",
    )(sinks[0].astype(F32), q, k, k, ksw, ksw, v, v, vsw, vsw)

    h, hpk, route, cnt = pl.pallas_call(
        _outproj_router_kernel,
        grid=(n_tiles,),
        in_specs=[
            pl.BlockSpec((tm, ATTN_WIDTH), row_blk),
            pl.BlockSpec((tm, POOL_WIDTH), row_blk),
            pl.BlockSpec((tm, d), row_blk),
            pl.BlockSpec((ATTN_WIDTH, d), const),
            pl.BlockSpec((POOL_WIDTH, d), const),
            pl.BlockSpec((1, d), const),
            pl.BlockSpec((d, 2 * LANES), const),
            pl.BlockSpec((1, LANES), const),
            pl.BlockSpec((SUB_TILE, SUB_TILE), const),
        ],
        out_specs=[
            pl.BlockSpec((tm, d), row_blk),
            pl.BlockSpec((tm * PK_ROWS, LANES), row_blk),
            pl.BlockSpec((tm, LANES), row_blk),
            pl.BlockSpec((SUBLANES, LANES), row_blk),
        ],
        out_shape=[
            jax.ShapeDtypeStruct((n, d), F32),
            jax.ShapeDtypeStruct((n * PK_ROWS, LANES), jnp.int32),
            jax.ShapeDtypeStruct((n, LANES), F32),
            jax.ShapeDtypeStruct((n_tiles * SUBLANES, LANES), F32),
        ],
        compiler_params=_cparams(),
        name="outproj_router",
    )(attn, y, xf, wo_attn, wo_pool, norm_ffn[0].reshape(1, d), w_r2, b_r, tri)

    assert tm == LOCAL_TILE
    lp = route[:, 4:4 + TOP_K].astype(jnp.int32).reshape(n * TOP_K)
    tile_cnt = cnt.reshape(n_tiles, SUBLANES, LANES)[:, 0, :N_EXPERTS].astype(jnp.int32)
    te = EXPERT_TILE
    gch, csrc, blk_eid, first, n_act, n_rows = _plan(tile_cnt, n)
    n_blocks = n_rows // te
    n_local = n // LOCAL_TILE
    gates = route[:, 2:2 + TOP_K].reshape(n * TOP_K)

    xs = pl.pallas_call(
        _local_sort_kernel,
        grid_spec=pltpu.PrefetchScalarGridSpec(
            num_scalar_prefetch=1,
            grid=(n_local,),
            in_specs=[pl.BlockSpec((LOCAL_TILE * PK_ROWS, LANES), lambda i, *_: (i, 0))],
            out_specs=pl.BlockSpec((LT_MAX * PK_ROWS, LANES), lambda i, *_: (i, 0)),
        ),
        out_shape=jax.ShapeDtypeStruct((n_local * LT_MAX * PK_ROWS, LANES), jnp.int32),
        compiler_params=_cparams(),
        name="local_sort",
    )(lp * PK_ROWS, hpk)

    w_idx = lambda i, eid_r, *_: (eid_r[i], 0, 0)
    ys = pl.pallas_call(
        _expert_kernel,
        grid_spec=pltpu.PrefetchScalarGridSpec(
            num_scalar_prefetch=4,
            grid=(n_blocks,),
            in_specs=[
                pl.BlockSpec(memory_space=pl.ANY),
                pl.BlockSpec((1, d, D_EXPERT), w_idx),
                pl.BlockSpec((1, d, D_EXPERT), w_idx),
                pl.BlockSpec((1, D_EXPERT, d), w_idx),
            ],
            out_specs=pl.BlockSpec((te * PK_ROWS, LANES), lambda i, *_: (i, 0)),
            scratch_shapes=[
                pltpu.VMEM((2, te * PK_ROWS, LANES), jnp.int32),
                pltpu.VMEM((d, D_EXPERT), BF16),
                pltpu.VMEM((d, D_EXPERT), BF16),
                pltpu.VMEM((D_EXPERT, d), BF16),
                pltpu.SemaphoreType.DMA((2,)),
            ],
        ),
        out_shape=jax.ShapeDtypeStruct((n_rows * PK_ROWS, LANES), jnp.int32),
        compiler_params=_cparams(vmem_limit=VMEM_LIMIT_EXPERTS),
        name="experts",
    )(blk_eid, first, n_act, csrc * (CHUNK * PK_ROWS), xs, w_gate[0], w_up[0], w_down[0])

    tt = LOCAL_TILE
    out = pl.pallas_call(
        _combine_kernel,
        grid_spec=pltpu.PrefetchScalarGridSpec(
            num_scalar_prefetch=3,
            grid=(n_local,),
            in_specs=[
                pl.BlockSpec((tt, d), lambda i, *_: (i, 0)),
                pl.BlockSpec(memory_space=pl.ANY),
            ],
            out_specs=pl.BlockSpec((tt, d), lambda i, *_: (i, 0)),
            scratch_shapes=[
                pltpu.VMEM((2, LT_MAX * PK_ROWS, LANES), jnp.int32),
                pltpu.VMEM((2, tt * PK_ROWS, LANES), F32),
                pltpu.SemaphoreType.DMA((2,)),
            ],
        ),
        out_shape=jax.ShapeDtypeStruct((n, d), F32),
        compiler_params=_cparams(),
        name="combine",
    )(gch * (CHUNK * PK_ROWS), lp * PK_ROWS, gates, h, ys)
    return out.reshape(bsz, seq, d)
```

```python
import functools

import jax
import jax.numpy as jnp
from jax import lax
from jax.experimental import pallas as pl
from jax.experimental.pallas import tpu as pltpu

F32 = jnp.float32
BF16 = jnp.bfloat16

D_MODEL = 2048
N_HEADS = 16
N_KV_HEADS = 4
HEAD_DIM = 64
GROUP = N_HEADS // N_KV_HEADS
ROT_DIM = HEAD_DIM // 4
ROPE_THETA = 500000.0
WINDOW = 128
BLOCK = 128
Q_BLOCKS = 2
ATTN_WIDTH = N_HEADS * HEAD_DIM
KV_WIDTH = N_KV_HEADS * HEAD_DIM
POOL_WINDOWS = (2, 4, 8, 16)
POOL_WIDTH = D_MODEL // 2
POOL_GROUP_DIM = POOL_WIDTH // len(POOL_WINDOWS)
N_EXPERT_GROUPS = 4
EXPERTS_PER_GROUP = 8
N_EXPERTS = N_EXPERT_GROUPS * EXPERTS_PER_GROUP
TOP_K = 2
D_EXPERT = 512
EPS = 1e-6

LANES = 128
SUBLANES = 8
MXU_DIM = 256
HALO = 16
NEG_BIG = -1e30
HALF = D_MODEL // 2
PK_ROWS = HALF // LANES
Y_ROWS = D_MODEL // LANES

ROW_TILE = 512
SUB_TILE = 512
EXPERT_TILE = 512
LOCAL_TILE = 512
CHUNK = 8
LT_MAX = TOP_K * LOCAL_TILE + N_EXPERTS * CHUNK
VMEM_LIMIT = 56 * 1024 * 1024
VMEM_LIMIT_EXPERTS = 60 * 1024 * 1024


def _cparams(n_axes=1, vmem_limit=VMEM_LIMIT, flags=None):
    return pltpu.CompilerParams(
        dimension_semantics=("arbitrary",) * n_axes,
        vmem_limit_bytes=vmem_limit,
        flags=flags,
    )


def _pack_halves(t):
    return pltpu.pack_elementwise([t[:, :HALF], t[:, HALF:]], packed_dtype=BF16)


def _unpack_halves(w):
    return tuple(pltpu.unpack_elementwise(w, index=i, packed_dtype=BF16, unpacked_dtype=F32)
                 for i in range(2))


def _head_sumsq(t, e_ref):
    t2 = (t * t).astype(BF16)
    e = e_ref[...]
    outs = []
    for c in range(t.shape[1] // MXU_DIM):
        sl = slice(c * MXU_DIM, (c + 1) * MXU_DIM)
        outs.append(jnp.dot(t2[:, sl], e, preferred_element_type=F32))
    return outs[0] if len(outs) == 1 else jnp.concatenate(outs, axis=1)


def _norm_rope(t, gain, e_ref, cc, s_up, s_dn, out_scale):
    w = t.shape[1]
    ss = _head_sumsq(t, e_ref)
    tn = t * lax.rsqrt(ss * (1.0 / HEAD_DIM) + EPS) * gain
    reps = w // LANES
    up = pltpu.roll(tn, w - ROT_DIM // 2, axis=1)
    dn = pltpu.roll(tn, ROT_DIM // 2, axis=1)
    c = jnp.concatenate([cc] * reps, axis=1)
    su = jnp.concatenate([s_up] * reps, axis=1)
    sd = jnp.concatenate([s_dn] * reps, axis=1)
    return (tn * c + up * su + dn * sd) * out_scale


def _swap_head_pairs(t):
    w = t.shape[1]
    lane = lax.broadcasted_iota(jnp.int32, t.shape, 1)
    from_up = pltpu.roll(t, w - HEAD_DIM, axis=1)
    from_dn = pltpu.roll(t, HEAD_DIM, axis=1)
    return jnp.where((lane & (LANES - 1)) < HEAD_DIM, from_up, from_dn)


def _inproj_kernel(x_ref, g_ref, wq_ref, wk_ref, wv_ref, wu_ref, qn_ref, kn_ref,
                   cc_ref, su_ref, sd_ref, e_ref, wpool_ref, pscale_ref,
                   q_out, k_out, v_out, ksw_out, vsw_out, y_out, ubuf, *, tiles_per_seq):
    i = pl.program_id(0)
    tm = x_ref.shape[0]

    @pl.when(i % tiles_per_seq == 0)
    def _():
        ubuf[0:HALO, :] = jnp.zeros((HALO, POOL_WIDTH), F32)

    for r0 in range(0, tm, SUB_TILE):
        rows = slice(r0, r0 + SUB_TILE)
        x = x_ref[rows, :]
        ms = jnp.mean(x * x, axis=-1, keepdims=True)
        hn = (x * lax.rsqrt(ms + EPS) * g_ref[...]).astype(BF16)

        cc, s_up, s_dn = cc_ref[rows, :], su_ref[rows, :], sd_ref[rows, :]
        q = jnp.dot(hn, wq_ref[...], preferred_element_type=F32)
        q_out[rows, :] = _norm_rope(q, qn_ref[...], e_ref, cc, s_up, s_dn,
                                    HEAD_DIM ** -0.5).astype(BF16)
        k = jnp.dot(hn, wk_ref[...], preferred_element_type=F32)
        k = _norm_rope(k, kn_ref[...], e_ref, cc, s_up, s_dn, 1.0)
        v = jnp.dot(hn, wv_ref[...], preferred_element_type=F32)
        k_out[rows, :] = k.astype(BF16)
        v_out[rows, :] = v.astype(BF16)
        ksw_out[rows, :] = _swap_head_pairs(k).astype(BF16)
        vsw_out[rows, :] = _swap_head_pairs(v).astype(BF16)

        base = HALO + r0
        ubuf[base:base + SUB_TILE, :] = jnp.dot(hn, wu_ref[...], preferred_element_type=F32)
        pos = (i % tiles_per_seq) * tm + r0 + lax.broadcasted_iota(
            jnp.int32, (SUB_TILE, POOL_GROUP_DIM), 0)
        for g, w in enumerate(POOL_WINDOWS):
            cols = slice(g * POOL_GROUP_DIM, (g + 1) * POOL_GROUP_DIM)
            assert w & (w - 1) == 0 and w - 1 < HALO
            acc = ubuf[base - HALO:base + SUB_TILE, cols]
            shift = 1
            while shift < w:
                acc = acc + pltpu.roll(acc, shift, axis=0)
                shift *= 2
            acc = acc[HALO:, :]
            u_g = ubuf[base:base + SUB_TILE, cols]
            cnt = jnp.minimum(pos + 1, w).astype(F32)
            d = (acc / cnt - u_g).astype(BF16)
            y = jnp.dot(d, wpool_ref[g], preferred_element_type=F32)
            y_out[rows, cols] = (y * pscale_ref[:, cols]).astype(BF16)
    ubuf[0:HALO, :] = ubuf[tm:tm + HALO, :]


def _attn_kernel(sinks_ref, q_ref, kc_ref, kp_ref, kswc_ref, kswp_ref,
                 vc_ref, vp_ref, vswc_ref, vswp_ref, o_ref):
    j = pl.program_id(1)
    two = 2 * BLOCK
    row = lax.broadcasted_iota(jnp.int32, (two, two), 0) & (BLOCK - 1)
    col = lax.broadcasted_iota(jnp.int32, (two, two), 1)
    diff = row + BLOCK - col
    local = (diff >= 0) & (diff < WINDOW)
    lo_lanes = lax.broadcasted_iota(jnp.int32, (two, LANES), 1) < HEAD_DIM
    top_rows = lax.broadcasted_iota(jnp.int32, (two, 1), 0) < BLOCK
    zero = jnp.zeros((two, LANES), BF16)

    for sb in range(q_ref.shape[0] // BLOCK):
        rows = slice(sb * BLOCK, (sb + 1) * BLOCK)
        if sb == 0:
            def band(cur_ref, prev_ref):
                return jnp.concatenate([prev_ref[...], cur_ref[0:BLOCK, :]], axis=0)
            valid = local & ((col >= BLOCK) | (j > 0))
        else:
            def band(cur_ref, prev_ref, sb=sb):
                return cur_ref[(sb - 1) * BLOCK:(sb + 1) * BLOCK, :]
            valid = local
        k_nat, k_swp = band(kc_ref, kp_ref), band(kswc_ref, kswp_ref)
        v_nat, v_swp = band(vc_ref, vp_ref), band(vswc_ref, vswp_ref)

        for g in range(N_KV_HEADS):
            c0 = (g // 2) * LANES
            kcols = (k_nat[:, c0:c0 + LANES], k_swp[:, c0:c0 + LANES])
            vcols = (v_nat[:, c0:c0 + LANES], v_swp[:, c0:c0 + LANES])
            in_lo, in_hi = (0, 1) if g % 2 == 0 else (1, 0)
            q0 = g * GROUP * HEAD_DIM
            qq = jnp.concatenate([q_ref[rows, q0:q0 + LANES],
                                  q_ref[rows, q0 + LANES:q0 + 2 * LANES]], axis=0)
            halves = []
            for half, src in ((0, in_lo), (1, in_hi)):
                qm = jnp.where(lo_lanes, qq, zero) if half == 0 else jnp.where(lo_lanes, zero, qq)
                s = lax.dot_general(qm, kcols[src], (((1,), (1,)), ((), ())),
                                    preferred_element_type=F32)
                s = jnp.where(valid, s, NEG_BIG)
                h_top = g * GROUP + half
                sink = jnp.where(top_rows, sinks_ref[h_top], sinks_ref[h_top + 2])
                m = jnp.maximum(jnp.max(s, axis=-1, keepdims=True), sink)
                p = jnp.exp(s - m)
                denom = jnp.sum(p, axis=-1, keepdims=True) + jnp.exp(sink - m)
                o = jnp.dot(p.astype(BF16), vcols[src], preferred_element_type=F32)
                halves.append(o / denom)
            o_pair = jnp.where(lo_lanes, halves[0], halves[1]).astype(BF16)
            o_ref[rows, q0:q0 + LANES] = o_pair[0:BLOCK, :]
            o_ref[rows, q0 + LANES:q0 + 2 * LANES] = o_pair[BLOCK:, :]


def _outproj_router_kernel(o_ref, y_ref, x_ref, woa_ref, wop_ref, g_ref, wr_ref,
                           br_ref, tri_ref, h_out, hpk_out, route_out, cnt_out):
    n_sub = x_ref.shape[0] // SUB_TILE
    picks = [_outproj_router_subtile(sub, o_ref, y_ref, x_ref, woa_ref, wop_ref, g_ref,
                                     wr_ref, br_ref, tri_ref, h_out, hpk_out)
             for sub in range(n_sub)]

    lane = lax.broadcasted_iota(jnp.int32, (SUB_TILE, LANES), 1)
    lane_f = lane.astype(F32)
    cnt_tile = sum(pk[6] for pk in picks)
    cnt_pad = jnp.floor((cnt_tile + (CHUNK - 1)) * (1.0 / CHUNK)) * CHUNK
    run_start = lax.dot_general(
        jnp.broadcast_to(cnt_pad, (SUBLANES, LANES)).astype(BF16),
        tri_ref[0:LANES, 0:LANES], (((1,), (1,)), ((), ())),
        preferred_element_type=F32)[0:1, :]
    before = jnp.zeros((1, LANES), F32)
    for sub, (e1, e2, g1, g2, r1, r2, cnt_sub) in enumerate(picks):
        table = run_start + before
        lp1 = jnp.sum(jnp.where(lane_f == e1, table, 0.0), axis=-1, keepdims=True) + r1
        lp2 = jnp.sum(jnp.where(lane_f == e2, table, 0.0), axis=-1, keepdims=True) + r2
        route_out[sub * SUB_TILE:(sub + 1) * SUB_TILE, :] = jnp.where(
            lane == 0, e1,
            jnp.where(lane == 1, e2,
                      jnp.where(lane == 2, g1,
                                jnp.where(lane == 3, g2,
                                          jnp.where(lane == 4, lp1,
                                                    jnp.where(lane == 5, lp2, 0.0))))))
        before = before + cnt_sub
    cnt_out[...] = jnp.broadcast_to(cnt_tile, cnt_out.shape)


def _outproj_router_subtile(sub, o_ref, y_ref, x_ref, woa_ref, wop_ref, g_ref, wr_ref,
                            br_ref, tri_ref, h_out, hpk_out):
    tm = SUB_TILE
    r0 = sub * SUB_TILE
    rows = slice(r0, r0 + SUB_TILE)
    h = (x_ref[rows, :]
         + jnp.dot(o_ref[rows, :], woa_ref[...], preferred_element_type=F32)
         + jnp.dot(y_ref[rows, :], wop_ref[...], preferred_element_type=F32))
    h_out[rows, :] = h
    ms = jnp.mean(h * h, axis=-1, keepdims=True)
    hn = h * lax.rsqrt(ms + EPS) * g_ref[...]
    hb = hn.astype(BF16)
    packed = _pack_halves(hn)
    for s in range(PK_ROWS):
        hpk_out[pl.ds(r0 * PK_ROWS + s, tm, stride=PK_ROWS), :] = (
            packed[:, s * LANES:(s + 1) * LANES])

    both = jnp.dot(hb, wr_ref[...], preferred_element_type=F32)
    logits = both[:, :LANES] + both[:, LANES:] + br_ref[...]
    lane = lax.broadcasted_iota(jnp.int32, (tm, LANES), 1)
    lane_f = lane.astype(F32)

    def first_lane_of(mask):
        return jnp.min(jnp.where(mask, lane_f, float(LANES)), axis=-1, keepdims=True)

    def row_max(t):
        return jnp.max(t, axis=-1, keepdims=True)

    coarse = jnp.where(lane < N_EXPERT_GROUPS, logits, NEG_BIG)
    fine = []
    for g in range(N_EXPERT_GROUPS):
        lo = N_EXPERT_GROUPS + g * EXPERTS_PER_GROUP
        fine.append(jnp.where((lane >= lo) & (lane < lo + EXPERTS_PER_GROUP), logits, NEG_BIG))
    cmax = row_max(coarse)
    f1s = [row_max(f) for f in fine]
    grp = first_lane_of(coarse == cmax)
    csum = jnp.sum(jnp.exp(coarse - cmax), axis=-1, keepdims=True)
    i1s = [first_lane_of(f == f1) for f, f1 in zip(fine, f1s)]
    rest = [jnp.where(lane_f == i1, NEG_BIG, f) for f, i1 in zip(fine, i1s)]
    f2s = [row_max(f) for f in rest]
    i2s = [first_lane_of(f == f2) for f, f2 in zip(rest, f2s)]

    def of_group(vals):
        out = vals[N_EXPERT_GROUPS - 1]
        for g in range(N_EXPERT_GROUPS - 2, -1, -1):
            out = jnp.where(grp == float(g), vals[g], out)
        return out

    f1, i1, f2, i2 = of_group(f1s), of_group(i1s), of_group(f2s), of_group(i2s)
    ratio = jnp.exp(f2 - f1)
    scale = (1.0 / csum) / (1.0 + ratio)
    g1 = scale
    g2 = scale * ratio
    e1 = i1 - N_EXPERT_GROUPS
    e2 = i2 - N_EXPERT_GROUPS

    oh1 = lane_f == e1
    oh2 = lane_f == e2
    oh = jnp.where(oh1, 1.0, 0.0) + jnp.where(oh2, 1.0, 0.0)
    prefix = jnp.dot(tri_ref[...], oh.astype(BF16), preferred_element_type=F32)
    r1 = jnp.sum(jnp.where(oh1, prefix, 0.0), axis=-1, keepdims=True)
    r2 = jnp.sum(jnp.where(oh2, prefix, 0.0), axis=-1, keepdims=True)
    return e1, e2, g1, g2, r1, r2, jnp.sum(oh, axis=0, keepdims=True)


def _expert_kernel(eid_ref, first_ref, nact_ref, off_ref,
                   hpk_hbm, wg_ref, wu_ref, wd_ref, ys_out,
                   xbuf, wg_s, wu_s, wd_s, sems):
    i = pl.program_id(0)
    tm = ys_out.shape[0] // PK_ROWS
    n_act = nact_ref[0]

    ch_rows = CHUNK * PK_ROWS
    n_ch = tm // CHUNK

    def gather(blk, slot):
        for c in range(n_ch):
            src = pl.multiple_of(off_ref[blk * n_ch + c], ch_rows)
            pltpu.make_async_copy(hpk_hbm.at[pl.ds(src, ch_rows)],
                                  xbuf.at[slot, pl.ds(c * ch_rows, ch_rows)],
                                  sems.at[slot]).start()

    @pl.when(i == 0)
    def _():
        gather(0, 0)

    @pl.when(i + 1 < n_act)
    def _():
        gather(i + 1, (i + 1) % 2)

    @pl.when(i < n_act)
    def _():
        slot = i % 2

        @pl.when(first_ref[i] == 1)
        def _():
            wg_s[...] = wg_ref[0].astype(BF16)
            wu_s[...] = wu_ref[0].astype(BF16)
            wd_s[...] = wd_ref[0].astype(BF16)

        pltpu.make_async_copy(hpk_hbm.at[pl.ds(0, tm * PK_ROWS)], xbuf.at[slot],
                              sems.at[slot]).wait()
        xu = jnp.concatenate(
            [xbuf[slot, pl.ds(s, tm, stride=PK_ROWS), :] for s in range(PK_ROWS)],
            axis=1)
        x_lo, x_hi = (v.astype(BF16) for v in _unpack_halves(xu))
        a = (jnp.dot(x_lo, wg_s[0:HALF, :], preferred_element_type=F32)
             + jnp.dot(x_hi, wg_s[HALF:, :], preferred_element_type=F32))
        b = (jnp.dot(x_lo, wu_s[0:HALF, :], preferred_element_type=F32)
             + jnp.dot(x_hi, wu_s[HALF:, :], preferred_element_type=F32))
        mid = (a * jax.nn.sigmoid(a) * b).astype(BF16)
        y = jnp.dot(mid, wd_s[...], preferred_element_type=F32)
        yw = _pack_halves(y)
        for s in range(PK_ROWS):
            ys_out[pl.ds(s, tm, stride=PK_ROWS), :] = yw[:, s * LANES:(s + 1) * LANES]

    @pl.when(i >= n_act)
    def _():
        ys_out[...] = jnp.zeros(ys_out.shape, ys_out.dtype)


def _local_sort_kernel(lp_ref, hpk_ref, xs_out):
    i = pl.program_id(0)
    t_tile = hpk_ref.shape[0] // PK_ROWS
    xs_out[...] = jnp.zeros(xs_out.shape, xs_out.dtype)

    def body(q, carry):
        for u in range(SUBLANES):
            t = q * SUBLANES + u
            row = hpk_ref[pl.ds(pl.multiple_of(t * PK_ROWS, PK_ROWS), PK_ROWS), :]
            for k in range(TOP_K):
                dst = pl.multiple_of(lp_ref[(i * t_tile + t) * TOP_K + k], PK_ROWS)
                xs_out[pl.ds(dst, PK_ROWS), :] = row
        return carry
    lax.fori_loop(0, t_tile // SUBLANES, body, 0)


def _combine_kernel(gch_ref, lp_ref, gate_ref, h_ref, ys_hbm, out_ref, ylocal, acc, sems):
    i = pl.program_id(0)
    n = pl.num_programs(0)
    t_tile = h_ref.shape[0]
    ch_rows = CHUNK * PK_ROWS
    n_ch = ylocal.shape[1] // ch_rows

    def chunk_copy(tile, slot, lc):
        src = pl.multiple_of(gch_ref[tile * n_ch + lc], ch_rows)
        dst = pl.multiple_of(lc * ch_rows, ch_rows)
        pltpu.make_async_copy(ys_hbm.at[pl.ds(src, ch_rows)],
                              ylocal.at[slot, pl.ds(dst, ch_rows)],
                              sems.at[slot]).start()

    def wait_slot(slot):
        pltpu.make_async_copy(ys_hbm.at[pl.ds(0, n_ch * ch_rows)], ylocal.at[slot],
                              sems.at[slot]).wait()

    @pl.when(i == 0)
    def _():
        def body(lc, carry):
            chunk_copy(0, 0, lc)
            return carry
        lax.fori_loop(0, n_ch, body, 0)

    slot = i % 2
    wait_slot(slot)

    nxt = jnp.minimum(i + 1, n - 1)
    tok_unroll = 4
    n_groups = t_tile // tok_unroll
    per_group = -(-n_ch // n_groups)
    issue_groups = n_ch // per_group
    assert issue_groups * per_group == n_ch and issue_groups <= n_groups

    def tok_body(q, carry, issue):
        if issue:
            for u in range(per_group):
                chunk_copy(nxt, 1 - slot, q * per_group + u)
        for u in range(tok_unroll):
            t = q * tok_unroll + u
            a0 = (i * t_tile + t) * TOP_K
            p0 = pl.multiple_of(lp_ref[a0], PK_ROWS)
            p1 = pl.multiple_of(lp_ref[a0 + 1], PK_ROWS)
            lo0, hi0 = _unpack_halves(ylocal[slot, pl.ds(p0, PK_ROWS), :])
            lo1, hi1 = _unpack_halves(ylocal[slot, pl.ds(p1, PK_ROWS), :])
            g0 = gate_ref[a0]
            g1 = gate_ref[a0 + 1]
            dst = pl.multiple_of(t * PK_ROWS, PK_ROWS)
            acc[0, pl.ds(dst, PK_ROWS), :] = g0 * lo0 + g1 * lo1
            acc[1, pl.ds(dst, PK_ROWS), :] = g0 * hi0 + g1 * hi1
        return carry
    lax.fori_loop(0, issue_groups, functools.partial(tok_body, issue=True), 0)
    lax.fori_loop(issue_groups, n_groups, functools.partial(tok_body, issue=False), 0)

    @pl.when(i == n - 1)
    def _():
        wait_slot(1 - slot)

    for half in range(2):
        for s in range(PK_ROWS):
            c0 = half * HALF + s * LANES
            out_ref[:, c0:c0 + LANES] = (h_ref[:, c0:c0 + LANES]
                                         + acc[half, pl.ds(s, t_tile, stride=PK_ROWS), :])


def _rope_tables(seq):
    pos = jnp.arange(seq, dtype=F32)
    inv_freq = ROPE_THETA ** (-jnp.arange(0, ROT_DIM, 2, dtype=F32) / ROT_DIM)
    ang = pos[:, None] * inv_freq[None, :]
    cos, sin = jnp.cos(ang), jnp.sin(ang)
    half = ROT_DIM // 2
    ones = jnp.ones((seq, HEAD_DIM - ROT_DIM), F32)
    zeros = jnp.zeros((seq, HEAD_DIM - ROT_DIM), F32)
    zh = jnp.zeros((seq, half), F32)
    cc = jnp.concatenate([cos, cos, ones], axis=1)
    s_up = jnp.concatenate([-sin, zh, zeros], axis=1)
    s_dn = jnp.concatenate([zh, sin, zeros], axis=1)
    rep = LANES // HEAD_DIM
    return tuple(jnp.tile(t, (1, rep)) for t in (cc, s_up, s_dn))


def _plan(cnt, n_tokens):
    nt = n_tokens // LOCAL_TILE
    experts = jnp.arange(N_EXPERTS, dtype=jnp.int32)
    cntp = ((cnt + CHUNK - 1) // CHUNK) * CHUNK
    loff_end = jnp.cumsum(cntp, axis=1)
    loff = loff_end - cntp
    rows_e = jnp.sum(cntp, axis=0)
    rows_pad = ((rows_e + EXPERT_TILE - 1) // EXPERT_TILE) * EXPERT_TILE
    g_end = jnp.cumsum(rows_pad)
    g_start = g_end - rows_pad
    gpos = g_start[None, :] + jnp.cumsum(cntp, axis=0) - cntp

    lrow = jnp.arange(LT_MAX // CHUNK, dtype=jnp.int32) * CHUNK
    e_lc = jnp.sum((lrow[None, :, None] >= loff_end[:, None, :]).astype(jnp.int32), axis=-1)
    sel = e_lc[:, :, None] == experts
    delta = jnp.sum(jnp.where(sel, (gpos - loff)[:, None, :], 0), axis=-1)
    gch = jnp.where(e_lc < N_EXPERTS, (delta + lrow[None, :]) // CHUNK, 0).reshape(-1)

    n_rows = TOP_K * n_tokens + nt * N_EXPERTS * (CHUNK - 1) + N_EXPERTS * (EXPERT_TILE - CHUNK)
    n_blocks = -(-n_rows // EXPERT_TILE)
    n_rows = n_blocks * EXPERT_TILE
    blk_start = jnp.arange(n_blocks, dtype=jnp.int32) * EXPERT_TILE
    blk_eid = jnp.minimum(
        jnp.sum((blk_start[:, None] >= g_end[None, :]).astype(jnp.int32), axis=-1),
        N_EXPERTS - 1)
    prev = jnp.concatenate([jnp.full((1,), -1, jnp.int32), blk_eid[:-1]])
    first = (blk_eid != prev).astype(jnp.int32)
    n_act = (g_end[-1] // EXPERT_TILE).astype(jnp.int32).reshape(1)

    cpb = EXPERT_TILE // CHUNK
    run_start = gpos.T // CHUNK
    run_len = cntp.T // CHUNK
    run_src = (jnp.arange(nt, dtype=jnp.int32)[:, None] * (LT_MAX // CHUNK) + loff // CHUNK).T
    tabs = jnp.stack([run_start, run_len, run_src], axis=0)
    dtabs = tabs - jnp.concatenate(
        [jnp.zeros((3, N_EXPERTS, 1), jnp.int32), tabs[:, :, :-1]], axis=2)
    blk_sel = (blk_eid[:, None] == experts)[:, :, None]
    blk_start_t = jnp.sum(jnp.where(blk_sel, run_start[None], 0), axis=1)
    blk_dtabs = jnp.sum(jnp.where(blk_sel[None], dtabs[:, None], 0), axis=2)
    gc = jnp.arange(n_rows // CHUNK, dtype=jnp.int32).reshape(n_blocks, cpb)
    started = blk_start_t[:, None, :] <= gc[:, :, None]
    picked = jnp.sum(jnp.where(started[None], blk_dtabs[:, :, None, :], 0), axis=-1)
    within = gc - picked[0]
    csrc = jnp.where(within < picked[1], picked[2] + within, 0).reshape(-1)
    return gch.astype(jnp.int32), csrc.astype(jnp.int32), blk_eid, first, n_act, n_rows


def kernel(x, norm_mix, w_in, q_norm, k_norm, sinks, w_pool, pool_scale, w_out,
           norm_ffn, w_coarse, b_coarse, w_fine, b_fine, w_gate, w_up, w_down):
    bsz, seq, d = x.shape
    n = bsz * seq
    assert d == D_MODEL and seq % ROW_TILE == 0 and seq % BLOCK == 0
    assert norm_mix.shape[0] == 1, "single-layer problem"
    xf = x.reshape(n, d)

    w_in_b = w_in[0].astype(BF16)
    wq = w_in_b[:, :ATTN_WIDTH]
    wk = w_in_b[:, ATTN_WIDTH:ATTN_WIDTH + KV_WIDTH]
    wv = w_in_b[:, ATTN_WIDTH + KV_WIDTH:ATTN_WIDTH + 2 * KV_WIDTH]
    wu = w_in_b[:, ATTN_WIDTH + 2 * KV_WIDTH:]
    qn = jnp.tile(q_norm[0], N_HEADS).reshape(1, ATTN_WIDTH)
    kn = jnp.tile(k_norm[0], N_KV_HEADS).reshape(1, KV_WIDTH)
    cc, s_up, s_dn = _rope_tables(seq)
    lane_head = jnp.arange(MXU_DIM) // HEAD_DIM
    e_mat = (lane_head[:, None] == lane_head[None, :]).astype(BF16)
    w_pool_b = w_pool[0].astype(BF16)
    pscale = pool_scale[0].reshape(1, POOL_WIDTH)
    w_out_b = w_out[0].astype(BF16)
    wo_attn, wo_pool = w_out_b[:ATTN_WIDTH], w_out_b[ATTN_WIDTH:]
    n_router = N_EXPERT_GROUPS + N_EXPERTS
    w_r = jnp.concatenate([w_coarse[0], w_fine[0]], axis=1)
    w_r = jnp.pad(w_r, ((0, 0), (0, LANES - n_router)))
    w_r_hi = w_r.astype(BF16)
    w_r_lo = (w_r - w_r_hi.astype(F32)).astype(BF16)
    w_r2 = jnp.concatenate([w_r_hi, w_r_lo], axis=1)
    b_r = jnp.pad(jnp.concatenate([b_coarse[0], b_fine[0]]),
                  (0, LANES - n_router)).reshape(1, LANES)

    tm = ROW_TILE
    tiles_per_seq = seq // tm
    n_tiles = n // tm
    n_sub = n // SUB_TILE
    idx = jnp.arange(SUB_TILE)
    tri = (idx[None, :] < idx[:, None]).astype(BF16)
    const = lambda *_: (0, 0)
    row_blk = lambda i: (i, 0)

    q, k, v, ksw, vsw, y = pl.pallas_call(
        functools.partial(_inproj_kernel, tiles_per_seq=tiles_per_seq),
        grid=(n_tiles,),
        in_specs=[
            pl.BlockSpec((tm, d), row_blk),
            pl.BlockSpec((1, d), const),
            pl.BlockSpec((d, ATTN_WIDTH), const),
            pl.BlockSpec((d, KV_WIDTH), const),
            pl.BlockSpec((d, KV_WIDTH), const),
            pl.BlockSpec((d, POOL_WIDTH), const),
            pl.BlockSpec((1, ATTN_WIDTH), const),
            pl.BlockSpec((1, KV_WIDTH), const),
            pl.BlockSpec((tm, LANES), lambda i: (i % tiles_per_seq, 0)),
            pl.BlockSpec((tm, LANES), lambda i: (i % tiles_per_seq, 0)),
            pl.BlockSpec((tm, LANES), lambda i: (i % tiles_per_seq, 0)),
            pl.BlockSpec((MXU_DIM, MXU_DIM), const),
            pl.BlockSpec((len(POOL_WINDOWS), POOL_GROUP_DIM, POOL_GROUP_DIM),
                         lambda i: (0, 0, 0)),
            pl.BlockSpec((1, POOL_WIDTH), const),
        ],
        out_specs=[
            pl.BlockSpec((tm, ATTN_WIDTH), row_blk),
            pl.BlockSpec((tm, KV_WIDTH), row_blk),
            pl.BlockSpec((tm, KV_WIDTH), row_blk),
            pl.BlockSpec((tm, KV_WIDTH), row_blk),
            pl.BlockSpec((tm, KV_WIDTH), row_blk),
            pl.BlockSpec((tm, POOL_WIDTH), row_blk),
        ],
        out_shape=[
            jax.ShapeDtypeStruct((n, ATTN_WIDTH), BF16),
            jax.ShapeDtypeStruct((n, KV_WIDTH), BF16),
            jax.ShapeDtypeStruct((n, KV_WIDTH), BF16),
            jax.ShapeDtypeStruct((n, KV_WIDTH), BF16),
            jax.ShapeDtypeStruct((n, KV_WIDTH), BF16),
            jax.ShapeDtypeStruct((n, POOL_WIDTH), BF16),
        ],
        scratch_shapes=[pltpu.VMEM((HALO + tm, POOL_WIDTH), F32)],
        compiler_params=_cparams(),
        name="inproj",
    )(xf, norm_mix[0].reshape(1, d), wq, wk, wv, wu, qn, kn, cc, s_up, s_dn,
      e_mat, w_pool_b, pscale)

    nb = seq // BLOCK
    nq = nb // Q_BLOCKS
    qrows = Q_BLOCKS * BLOCK
    cur = lambda b, j, *_: (b * nq + j, 0)
    prv = lambda b, j, *_: (b * nb + jnp.maximum(Q_BLOCKS * j - 1, 0), 0)
    kv_specs = [pl.BlockSpec((qrows, KV_WIDTH), cur), pl.BlockSpec((BLOCK, KV_WIDTH), prv)]
    attn = pl.pallas_call(
        _attn_kernel,
        grid_spec=pltpu.PrefetchScalarGridSpec(
            num_scalar_prefetch=1,
            grid=(bsz, nq),
            in_specs=[pl.BlockSpec((qrows, ATTN_WIDTH), cur)] + kv_specs * 4,
            out_specs=pl.BlockSpec((qrows, ATTN_WIDTH), cur),
        ),
        out_shape=jax.ShapeDtypeStruct((n, ATTN_WIDTH), BF16),
        compiler_params=_cparams(2),
        name="swa_attn",
    )(sinks[0].astype(F32), q, k, k, ksw, ksw, v, v, vsw, vsw)

    h, hpk, route, cnt = pl.pallas_call(
        _outproj_router_kernel,
        grid=(n_tiles,),
        in_specs=[
            pl.BlockSpec((tm, ATTN_WIDTH), row_blk),
            pl.BlockSpec((tm, POOL_WIDTH), row_blk),
            pl.BlockSpec((tm, d), row_blk),
            pl.BlockSpec((ATTN_WIDTH, d), const),
            pl.BlockSpec((POOL_WIDTH, d), const),
            pl.BlockSpec((1, d), const),
            pl.BlockSpec((d, 2 * LANES), const),
            pl.BlockSpec((1, LANES), const),
            pl.BlockSpec((SUB_TILE, SUB_TILE), const),
        ],
        out_specs=[
            pl.BlockSpec((tm, d), row_blk),
            pl.BlockSpec((tm * PK_ROWS, LANES), row_blk),
            pl.BlockSpec((tm, LANES), row_blk),
            pl.BlockSpec((SUBLANES, LANES), row_blk),
        ],
        out_shape=[
            jax.ShapeDtypeStruct((n, d), F32),
            jax.ShapeDtypeStruct((n * PK_ROWS, LANES), jnp.int32),
            jax.ShapeDtypeStruct((n, LANES), F32),
            jax.ShapeDtypeStruct((n_tiles * SUBLANES, LANES), F32),
        ],
        compiler_params=_cparams(),
        name="outproj_router",
    )(attn, y, xf, wo_attn, wo_pool, norm_ffn[0].reshape(1, d), w_r2, b_r, tri)

    assert tm == LOCAL_TILE
    lp = route[:, 4:4 + TOP_K].astype(jnp.int32).reshape(n * TOP_K)
    tile_cnt = cnt.reshape(n_tiles, SUBLANES, LANES)[:, 0, :N_EXPERTS].astype(jnp.int32)
    te = EXPERT_TILE
    gch, csrc, blk_eid, first, n_act, n_rows = _plan(tile_cnt, n)
    n_blocks = n_rows // te
    n_local = n // LOCAL_TILE
    gates = route[:, 2:2 + TOP_K].reshape(n * TOP_K)

    xs = pl.pallas_call(
        _local_sort_kernel,
        grid_spec=pltpu.PrefetchScalarGridSpec(
            num_scalar_prefetch=1,
            grid=(n_local,),
            in_specs=[pl.BlockSpec((LOCAL_TILE * PK_ROWS, LANES), lambda i, *_: (i, 0))],
            out_specs=pl.BlockSpec((LT_MAX * PK_ROWS, LANES), lambda i, *_: (i, 0)),
        ),
        out_shape=jax.ShapeDtypeStruct((n_local * LT_MAX * PK_ROWS, LANES), jnp.int32),
        compiler_params=_cparams(),
        name="local_sort",
    )(lp * PK_ROWS, hpk)

    w_idx = lambda i, eid_r, *_: (eid_r[i], 0, 0)
    ys = pl.pallas_call(
        _expert_kernel,
        grid_spec=pltpu.PrefetchScalarGridSpec(
            num_scalar_prefetch=4,
            grid=(n_blocks,),
            in_specs=[
                pl.BlockSpec(memory_space=pl.ANY),
                pl.BlockSpec((1, d, D_EXPERT), w_idx),
                pl.BlockSpec((1, d, D_EXPERT), w_idx),
                pl.BlockSpec((1, D_EXPERT, d), w_idx),
            ],
            out_specs=pl.BlockSpec((te * PK_ROWS, LANES), lambda i, *_: (i, 0)),
            scratch_shapes=[
                pltpu.VMEM((2, te * PK_ROWS, LANES), jnp.int32),
                pltpu.VMEM((d, D_EXPERT), BF16),
                pltpu.VMEM((d, D_EXPERT), BF16),
                pltpu.VMEM((D_EXPERT, d), BF16),
                pltpu.SemaphoreType.DMA((2,)),
            ],
        ),
        out_shape=jax.ShapeDtypeStruct((n_rows * PK_ROWS, LANES), jnp.int32),
        compiler_params=_cparams(vmem_limit=VMEM_LIMIT_EXPERTS),
        name="experts",
    )(blk_eid, first, n_act, csrc * (CHUNK * PK_ROWS), xs, w_gate[0], w_up[0], w_down[0])

    tt = LOCAL_TILE
    out = pl.pallas_call(
        _combine_kernel,
        grid_spec=pltpu.PrefetchScalarGridSpec(
            num_scalar_prefetch=3,
            grid=(n_local,),
            in_specs=[
                pl.BlockSpec((tt, d), lambda i, *_: (i, 0)),
                pl.BlockSpec(memory_space=pl.ANY),
            ],
            out_specs=pl.BlockSpec((tt, d), lambda i, *_: (i, 0)),
            scratch_shapes=[
                pltpu.VMEM((2, LT_MAX * PK_ROWS, LANES), jnp.int32),
                pltpu.VMEM((2, tt * PK_ROWS, LANES), F32),
                pltpu.SemaphoreType.DMA((2,)),
            ],
        ),
        out_shape=jax.ShapeDtypeStruct((n, d), F32),
        compiler_params=_cparams(),
        name="combine",
    )(gch * (CHUNK * PK_ROWS), lp * PK_ROWS, gates, h, ys)
    return out.reshape(bsz, seq, d)
```

```python
import functools

import jax
import jax.numpy as jnp
from jax import lax
from jax.experimental import pallas as pl
from jax.experimental.pallas import tpu as pltpu

F32 = jnp.float32
BF16 = jnp.bfloat16

D_MODEL = 2048
N_HEADS = 16
N_KV_HEADS = 4
HEAD_DIM = 64
GROUP = N_HEADS // N_KV_HEADS
ROT_DIM = HEAD_DIM // 4
ROPE_THETA = 500000.0
WINDOW = 128
BLOCK = 128
Q_BLOCKS = 2
ATTN_WIDTH = N_HEADS * HEAD_DIM
KV_WIDTH = N_KV_HEADS * HEAD_DIM
POOL_WINDOWS = (2, 4, 8, 16)
POOL_WIDTH = D_MODEL // 2
POOL_GROUP_DIM = POOL_WIDTH // len(POOL_WINDOWS)
N_EXPERT_GROUPS = 4
EXPERTS_PER_GROUP = 8
N_EXPERTS = N_EXPERT_GROUPS * EXPERTS_PER_GROUP
TOP_K = 2
D_EXPERT = 512
EPS = 1e-6

LANES = 128
SUBLANES = 8
MXU_DIM = 256
HALO = 16
NEG_BIG = -1e30
HALF = D_MODEL // 2
PK_ROWS = HALF // LANES
Y_ROWS = D_MODEL // LANES

ROW_TILE = 512
SUB_TILE = 512
EXPERT_TILE = 512
LOCAL_TILE = 512
CHUNK = 8
LT_MAX = TOP_K * LOCAL_TILE + N_EXPERTS * CHUNK
VMEM_LIMIT = 56 * 1024 * 1024
VMEM_LIMIT_EXPERTS = 60 * 1024 * 1024


def _cparams(n_axes=1, vmem_limit=VMEM_LIMIT, flags=None):
    return pltpu.CompilerParams(
        dimension_semantics=("arbitrary",) * n_axes,
        vmem_limit_bytes=vmem_limit,
        flags=flags,
    )


def _pack_halves(t):
    return pltpu.pack_elementwise([t[:, :HALF], t[:, HALF:]], packed_dtype=BF16)


def _unpack_halves(w):
    return tuple(pltpu.unpack_elementwise(w, index=i, packed_dtype=BF16, unpacked_dtype=F32)
                 for i in range(2))


def _head_sumsq(t, e_ref):
    t2 = (t * t).astype(BF16)
    e = e_ref[...]
    outs = []
    for c in range(t.shape[1] // MXU_DIM):
        sl = slice(c * MXU_DIM, (c + 1) * MXU_DIM)
        outs.append(jnp.dot(t2[:, sl], e, preferred_element_type=F32))
    return outs[0] if len(outs) == 1 else jnp.concatenate(outs, axis=1)


def _norm_rope(t, gain, e_ref, cc, s_up, s_dn, out_scale):
    w = t.shape[1]
    ss = _head_sumsq(t, e_ref)
    tn = t * lax.rsqrt(ss * (1.0 / HEAD_DIM) + EPS) * gain
    reps = w // LANES
    up = pltpu.roll(tn, w - ROT_DIM // 2, axis=1)
    dn = pltpu.roll(tn, ROT_DIM // 2, axis=1)
    c = jnp.concatenate([cc] * reps, axis=1)
    su = jnp.concatenate([s_up] * reps, axis=1)
    sd = jnp.concatenate([s_dn] * reps, axis=1)
    return (tn * c + up * su + dn * sd) * out_scale


def _swap_head_pairs(t):
    w = t.shape[1]
    lane = lax.broadcasted_iota(jnp.int32, t.shape, 1)
    from_up = pltpu.roll(t, w - HEAD_DIM, axis=1)
    from_dn = pltpu.roll(t, HEAD_DIM, axis=1)
    return jnp.where((lane & (LANES - 1)) < HEAD_DIM, from_up, from_dn)


def _inproj_kernel(x_ref, g_ref, wq_ref, wk_ref, wv_ref, wu_ref, qn_ref, kn_ref,
                   cc_ref, su_ref, sd_ref, e_ref, wpool_ref, pscale_ref,
                   q_out, k_out, v_out, ksw_out, vsw_out, y_out, ubuf, *, tiles_per_seq):
    i = pl.program_id(0)
    tm = x_ref.shape[0]

    @pl.when(i % tiles_per_seq == 0)
    def _():
        ubuf[0:HALO, :] = jnp.zeros((HALO, POOL_WIDTH), F32)

    for r0 in range(0, tm, SUB_TILE):
        rows = slice(r0, r0 + SUB_TILE)
        x = x_ref[rows, :]
        ms = jnp.mean(x * x, axis=-1, keepdims=True)
        hn = (x * lax.rsqrt(ms + EPS) * g_ref[...]).astype(BF16)

        cc, s_up, s_dn = cc_ref[rows, :], su_ref[rows, :], sd_ref[rows, :]
        q = jnp.dot(hn, wq_ref[...], preferred_element_type=F32)
        q_out[rows, :] = _norm_rope(q, qn_ref[...], e_ref, cc, s_up, s_dn,
                                    HEAD_DIM ** -0.5).astype(BF16)
        k = jnp.dot(hn, wk_ref[...], preferred_element_type=F32)
        k = _norm_rope(k, kn_ref[...], e_ref, cc, s_up, s_dn, 1.0)
        v = jnp.dot(hn, wv_ref[...], preferred_element_type=F32)
        k_out[rows, :] = k.astype(BF16)
        v_out[rows, :] = v.astype(BF16)
        ksw_out[rows, :] = _swap_head_pairs(k).astype(BF16)
        vsw_out[rows, :] = _swap_head_pairs(v).astype(BF16)

        base = HALO + r0
        ubuf[base:base + SUB_TILE, :] = jnp.dot(hn, wu_ref[...], preferred_element_type=F32)
        pos = (i % tiles_per_seq) * tm + r0 + lax.broadcasted_iota(
            jnp.int32, (SUB_TILE, POOL_GROUP_DIM), 0)
        for g, w in enumerate(POOL_WINDOWS):
            cols = slice(g * POOL_GROUP_DIM, (g + 1) * POOL_GROUP_DIM)
            assert w & (w - 1) == 0 and w - 1 < HALO
            acc = ubuf[base - HALO:base + SUB_TILE, cols]
            shift = 1
            while shift < w:
                acc = acc + pltpu.roll(acc, shift, axis=0)
                shift *= 2
            acc = acc[HALO:, :]
            u_g = ubuf[base:base + SUB_TILE, cols]
            cnt = jnp.minimum(pos + 1, w).astype(F32)
            d = (acc / cnt - u_g).astype(BF16)
            y = jnp.dot(d, wpool_ref[g], preferred_element_type=F32)
            y_out[rows, cols] = (y * pscale_ref[:, cols]).astype(BF16)
    ubuf[0:HALO, :] = ubuf[tm:tm + HALO, :]


def _attn_kernel(sinks_ref, q_ref, kc_ref, kp_ref, kswc_ref, kswp_ref,
                 vc_ref, vp_ref, vswc_ref, vswp_ref, o_ref):
    j = pl.program_id(1)
    two = 2 * BLOCK
    row = lax.broadcasted_iota(jnp.int32, (two, two), 0) & (BLOCK - 1)
    col = lax.broadcasted_iota(jnp.int32, (two, two), 1)
    diff = row + BLOCK - col
    local = (diff >= 0) & (diff < WINDOW)
    lo_lanes = lax.broadcasted_iota(jnp.int32, (two, LANES), 1) < HEAD_DIM
    top_rows = lax.broadcasted_iota(jnp.int32, (two, 1), 0) < BLOCK
    zero = jnp.zeros((two, LANES), BF16)

    for sb in range(q_ref.shape[0] // BLOCK):
        rows = slice(sb * BLOCK, (sb + 1) * BLOCK)
        if sb == 0:
            def band(cur_ref, prev_ref):
                return jnp.concatenate([prev_ref[...], cur_ref[0:BLOCK, :]], axis=0)
            valid = local & ((col >= BLOCK) | (j > 0))
        else:
            def band(cur_ref, prev_ref, sb=sb):
                return cur_ref[(sb - 1) * BLOCK:(sb + 1) * BLOCK, :]
            valid = local
        k_nat, k_swp = band(kc_ref, kp_ref), band(kswc_ref, kswp_ref)
        v_nat, v_swp = band(vc_ref, vp_ref), band(vswc_ref, vswp_ref)

        for g in range(N_KV_HEADS):
            c0 = (g // 2) * LANES
            kcols = (k_nat[:, c0:c0 + LANES], k_swp[:, c0:c0 + LANES])
            vcols = (v_nat[:, c0:c0 + LANES], v_swp[:, c0:c0 + LANES])
            in_lo, in_hi = (0, 1) if g % 2 == 0 else (1, 0)
            q0 = g * GROUP * HEAD_DIM
            qq = jnp.concatenate([q_ref[rows, q0:q0 + LANES],
                                  q_ref[rows, q0 + LANES:q0 + 2 * LANES]], axis=0)
            halves = []
            for half, src in ((0, in_lo), (1, in_hi)):
                qm = jnp.where(lo_lanes, qq, zero) if half == 0 else jnp.where(lo_lanes, zero, qq)
                s = lax.dot_general(qm, kcols[src], (((1,), (1,)), ((), ())),
                                    preferred_element_type=F32)
                s = jnp.where(valid, s, NEG_BIG)
                h_top = g * GROUP + half
                sink = jnp.where(top_rows, sinks_ref[h_top], sinks_ref[h_top + 2])
                m = jnp.maximum(jnp.max(s, axis=-1, keepdims=True), sink)
                p = jnp.exp(s - m)
                denom = jnp.sum(p, axis=-1, keepdims=True) + jnp.exp(sink - m)
                o = jnp.dot(p.astype(BF16), vcols[src], preferred_element_type=F32)
                halves.append(o / denom)
            o_pair = jnp.where(lo_lanes, halves[0], halves[1]).astype(BF16)
            o_ref[rows, q0:q0 + LANES] = o_pair[0:BLOCK, :]
            o_ref[rows, q0 + LANES:q0 + 2 * LANES] = o_pair[BLOCK:, :]


def _outproj_router_kernel(o_ref, y_ref, x_ref, woa_ref, wop_ref, g_ref, wr_ref,
                           br_ref, tri_ref, h_out, hpk_out, route_out, cnt_out):
    n_sub = x_ref.shape[0] // SUB_TILE
    picks = [_outproj_router_subtile(sub, o_ref, y_ref, x_ref, woa_ref, wop_ref, g_ref,
                                     wr_ref, br_ref, tri_ref, h_out, hpk_out)
             for sub in range(n_sub)]

    lane = lax.broadcasted_iota(jnp.int32, (SUB_TILE, LANES), 1)
    lane_f = lane.astype(F32)
    cnt_tile = sum(pk[5] for pk in picks)
    cnt_pad = jnp.floor((cnt_tile + (CHUNK - 1)) * (1.0 / CHUNK)) * CHUNK
    run_start = lax.dot_general(
        jnp.broadcast_to(cnt_pad, (SUBLANES, LANES)).astype(BF16),
        tri_ref[0:LANES, 0:LANES], (((1,), (1,)), ((), ())),
        preferred_element_type=F32)[0:1, :]
    before = jnp.zeros((1, LANES), F32)
    for sub, (e1, e2, g1, g2, prefix, cnt_sub) in enumerate(picks):
        table = prefix + (run_start + before)
        lp1 = jnp.sum(jnp.where(lane_f == e1, table, 0.0), axis=-1, keepdims=True)
        lp2 = jnp.sum(jnp.where(lane_f == e2, table, 0.0), axis=-1, keepdims=True)
        cols = jnp.where(lane == 0, g1,
                         jnp.where(lane == 1, g2,
                                   jnp.where(lane == 2, lp1,
                                             jnp.where(lane == 3, lp2, 0.0))))
        pick = (lax.broadcasted_iota(jnp.int32, (SUBLANES, LANES), 0)
                == lax.broadcasted_iota(jnp.int32, (SUBLANES, LANES), 1)).astype(F32)
        route_out[:, sub * SUB_TILE:(sub + 1) * SUB_TILE] = lax.dot_general(
            pick, cols, (((1,), (1,)), ((), ())), precision=lax.Precision.HIGHEST,
            preferred_element_type=F32)
        before = before + cnt_sub
    cnt_out[...] = jnp.broadcast_to(cnt_tile, cnt_out.shape)


def _outproj_router_subtile(sub, o_ref, y_ref, x_ref, woa_ref, wop_ref, g_ref, wr_ref,
                            br_ref, tri_ref, h_out, hpk_out):
    tm = SUB_TILE
    r0 = sub * SUB_TILE
    rows = slice(r0, r0 + SUB_TILE)
    h = (x_ref[rows, :]
         + jnp.dot(o_ref[rows, :], woa_ref[...], preferred_element_type=F32)
         + jnp.dot(y_ref[rows, :], wop_ref[...], preferred_element_type=F32))
    h_out[rows, :] = h
    ms = jnp.mean(h * h, axis=-1, keepdims=True)
    hn = h * lax.rsqrt(ms + EPS) * g_ref[...]
    hb = hn.astype(BF16)
    packed = _pack_halves(hn)
    for s in range(PK_ROWS):
        hpk_out[pl.ds(r0 * PK_ROWS + s, tm, stride=PK_ROWS), :] = (
            packed[:, s * LANES:(s + 1) * LANES])

    both = jnp.dot(hb, wr_ref[...], preferred_element_type=F32)
    logits = both[:, :LANES] + both[:, LANES:] + br_ref[...]
    lane = lax.broadcasted_iota(jnp.int32, (tm, LANES), 1)
    lane_f = lane.astype(F32)

    def first_lane_of(mask):
        return jnp.min(jnp.where(mask, lane_f, float(LANES)), axis=-1, keepdims=True)

    def row_max(t):
        return jnp.max(t, axis=-1, keepdims=True)

    coarse = jnp.where(lane < N_EXPERT_GROUPS, logits, NEG_BIG)
    cmax = row_max(coarse)
    grp = first_lane_of(coarse == cmax)
    csum = jnp.sum(jnp.exp(coarse - cmax), axis=-1, keepdims=True)
    flo = N_EXPERT_GROUPS + EXPERTS_PER_GROUP * grp
    fine = jnp.where((lane_f >= flo) & (lane_f < flo + EXPERTS_PER_GROUP), logits, NEG_BIG)
    f1 = row_max(fine)
    i1 = first_lane_of(fine == f1)
    rest = jnp.where(lane_f == i1, NEG_BIG, fine)
    f2 = row_max(rest)
    i2 = first_lane_of(rest == f2)
    ratio = jnp.exp(f2 - f1)
    scale = (1.0 / csum) / (1.0 + ratio)
    g1 = scale
    g2 = scale * ratio
    e1 = i1 - N_EXPERT_GROUPS
    e2 = i2 - N_EXPERT_GROUPS

    oh = jnp.where(lane_f == e1, 1.0, 0.0) + jnp.where(lane_f == e2, 1.0, 0.0)
    prefix = jnp.dot(tri_ref[...], oh.astype(BF16), preferred_element_type=F32)
    return e1, e2, g1, g2, prefix, jnp.sum(oh, axis=0, keepdims=True)


def _expert_kernel(eid_ref, first_ref, nact_ref, off_ref,
                   hpk_hbm, wg_ref, wu_ref, wd_ref, ys_out,
                   xbuf, wg_s, wu_s, wd_s, sems):
    i = pl.program_id(0)
    tm = ys_out.shape[0] // PK_ROWS
    n_act = nact_ref[0]

    ch_rows = CHUNK * PK_ROWS
    n_ch = tm // CHUNK

    def gather(blk, slot):
        for c in range(n_ch):
            src = pl.multiple_of(off_ref[blk * n_ch + c], ch_rows)
            pltpu.make_async_copy(hpk_hbm.at[pl.ds(src, ch_rows)],
                                  xbuf.at[slot, pl.ds(c * ch_rows, ch_rows)],
                                  sems.at[slot]).start()

    @pl.when(i == 0)
    def _():
        gather(0, 0)

    @pl.when(i + 1 < n_act)
    def _():
        gather(i + 1, (i + 1) % 2)

    @pl.when(i < n_act)
    def _():
        slot = i % 2

        @pl.when(first_ref[i] == 1)
        def _():
            wg_s[...] = wg_ref[0].astype(BF16)
            wu_s[...] = wu_ref[0].astype(BF16)
            wd_s[...] = wd_ref[0].astype(BF16)

        pltpu.make_async_copy(hpk_hbm.at[pl.ds(0, tm * PK_ROWS)], xbuf.at[slot],
                              sems.at[slot]).wait()
        xu = jnp.concatenate(
            [xbuf[slot, pl.ds(s, tm, stride=PK_ROWS), :] for s in range(PK_ROWS)],
            axis=1)
        x_lo, x_hi = (v.astype(BF16) for v in _unpack_halves(xu))
        a = (jnp.dot(x_lo, wg_s[0:HALF, :], preferred_element_type=F32)
             + jnp.dot(x_hi, wg_s[HALF:, :], preferred_element_type=F32))
        b = (jnp.dot(x_lo, wu_s[0:HALF, :], preferred_element_type=F32)
             + jnp.dot(x_hi, wu_s[HALF:, :], preferred_element_type=F32))
        mid = (a * jax.nn.sigmoid(a) * b).astype(BF16)
        y = jnp.dot(mid, wd_s[...], preferred_element_type=F32)
        yw = _pack_halves(y)
        for s in range(PK_ROWS):
            ys_out[pl.ds(s, tm, stride=PK_ROWS), :] = yw[:, s * LANES:(s + 1) * LANES]

    @pl.when(i >= n_act)
    def _():
        ys_out[...] = jnp.zeros(ys_out.shape, ys_out.dtype)


def _local_sort_kernel(lp_ref, hpk_ref, xs_out):
    i = pl.program_id(0)
    t_tile = hpk_ref.shape[0] // PK_ROWS
    n_tok = pl.num_programs(0) * t_tile
    xs_out[...] = jnp.zeros(xs_out.shape, xs_out.dtype)

    def body(q, carry):
        for u in range(SUBLANES):
            t = q * SUBLANES + u
            row = hpk_ref[pl.ds(pl.multiple_of(t * PK_ROWS, PK_ROWS), PK_ROWS), :]
            for k in range(TOP_K):
                dst = pl.multiple_of(lp_ref[k * n_tok + i * t_tile + t], PK_ROWS)
                xs_out[pl.ds(dst, PK_ROWS), :] = row
        return carry
    lax.fori_loop(0, t_tile // SUBLANES, body, 0)


def _combine_kernel(gch_ref, lp_ref, gate_ref, h_ref, ys_hbm, out_ref, ylocal, acc, sems):
    i = pl.program_id(0)
    n = pl.num_programs(0)
    t_tile = h_ref.shape[0]
    n_tok = n * t_tile
    ch_rows = CHUNK * PK_ROWS
    n_ch = ylocal.shape[1] // ch_rows

    def chunk_copy(tile, slot, lc):
        src = pl.multiple_of(gch_ref[tile * n_ch + lc], ch_rows)
        dst = pl.multiple_of(lc * ch_rows, ch_rows)
        pltpu.make_async_copy(ys_hbm.at[pl.ds(src, ch_rows)],
                              ylocal.at[slot, pl.ds(dst, ch_rows)],
                              sems.at[slot]).start()

    def wait_slot(slot):
        pltpu.make_async_copy(ys_hbm.at[pl.ds(0, n_ch * ch_rows)], ylocal.at[slot],
                              sems.at[slot]).wait()

    @pl.when(i == 0)
    def _():
        def body(lc, carry):
            chunk_copy(0, 0, lc)
            return carry
        lax.fori_loop(0, n_ch, body, 0)

    slot = i % 2
    wait_slot(slot)

    nxt = jnp.minimum(i + 1, n - 1)
    tok_unroll = 4
    n_groups = t_tile // tok_unroll
    per_group = -(-n_ch // n_groups)
    issue_groups = n_ch // per_group
    assert issue_groups * per_group == n_ch and issue_groups <= n_groups

    def tok_body(q, carry, issue):
        if issue:
            for u in range(per_group):
                chunk_copy(nxt, 1 - slot, q * per_group + u)
        for u in range(tok_unroll):
            t = q * tok_unroll + u
            a0 = i * t_tile + t
            a1 = n_tok + a0
            p0 = pl.multiple_of(lp_ref[a0], PK_ROWS)
            p1 = pl.multiple_of(lp_ref[a1], PK_ROWS)
            lo0, hi0 = _unpack_halves(ylocal[slot, pl.ds(p0, PK_ROWS), :])
            lo1, hi1 = _unpack_halves(ylocal[slot, pl.ds(p1, PK_ROWS), :])
            g0 = gate_ref[a0]
            g1 = gate_ref[a1]
            dst = pl.multiple_of(t * PK_ROWS, PK_ROWS)
            acc[0, pl.ds(dst, PK_ROWS), :] = g0 * lo0 + g1 * lo1
            acc[1, pl.ds(dst, PK_ROWS), :] = g0 * hi0 + g1 * hi1
        return carry
    lax.fori_loop(0, issue_groups, functools.partial(tok_body, issue=True), 0)
    lax.fori_loop(issue_groups, n_groups, functools.partial(tok_body, issue=False), 0)

    @pl.when(i == n - 1)
    def _():
        wait_slot(1 - slot)

    for half in range(2):
        for s in range(PK_ROWS):
            c0 = half * HALF + s * LANES
            out_ref[:, c0:c0 + LANES] = (h_ref[:, c0:c0 + LANES]
                                         + acc[half, pl.ds(s, t_tile, stride=PK_ROWS), :])


def _rope_tables(seq):
    pos = jnp.arange(seq, dtype=F32)
    inv_freq = ROPE_THETA ** (-jnp.arange(0, ROT_DIM, 2, dtype=F32) / ROT_DIM)
    ang = pos[:, None] * inv_freq[None, :]
    cos, sin = jnp.cos(ang), jnp.sin(ang)
    half = ROT_DIM // 2
    ones = jnp.ones((seq, HEAD_DIM - ROT_DIM), F32)
    zeros = jnp.zeros((seq, HEAD_DIM - ROT_DIM), F32)
    zh = jnp.zeros((seq, half), F32)
    cc = jnp.concatenate([cos, cos, ones], axis=1)
    s_up = jnp.concatenate([-sin, zh, zeros], axis=1)
    s_dn = jnp.concatenate([zh, sin, zeros], axis=1)
    rep = LANES // HEAD_DIM
    return tuple(jnp.tile(t, (1, rep)) for t in (cc, s_up, s_dn))


def _plan(cnt, n_tokens):
    nt = n_tokens // LOCAL_TILE
    experts = jnp.arange(N_EXPERTS, dtype=jnp.int32)
    cntp = ((cnt + CHUNK - 1) // CHUNK) * CHUNK
    loff_end = jnp.cumsum(cntp, axis=1)
    loff = loff_end - cntp
    rows_e = jnp.sum(cntp, axis=0)
    rows_pad = ((rows_e + EXPERT_TILE - 1) // EXPERT_TILE) * EXPERT_TILE
    g_end = jnp.cumsum(rows_pad)
    g_start = g_end - rows_pad
    gpos = g_start[None, :] + jnp.cumsum(cntp, axis=0) - cntp

    lrow = jnp.arange(LT_MAX // CHUNK, dtype=jnp.int32) * CHUNK
    e_lc = jnp.sum((lrow[None, :, None] >= loff_end[:, None, :]).astype(jnp.int32), axis=-1)
    sel = e_lc[:, :, None] == experts
    delta = jnp.sum(jnp.where(sel, (gpos - loff)[:, None, :], 0), axis=-1)
    gch = jnp.where(e_lc < N_EXPERTS, (delta + lrow[None, :]) // CHUNK, 0).reshape(-1)

    n_rows = TOP_K * n_tokens + nt * N_EXPERTS * (CHUNK - 1) + N_EXPERTS * (EXPERT_TILE - CHUNK)
    n_blocks = -(-n_rows // EXPERT_TILE)
    n_rows = n_blocks * EXPERT_TILE
    blk_start = jnp.arange(n_blocks, dtype=jnp.int32) * EXPERT_TILE
    blk_eid = jnp.minimum(
        jnp.sum((blk_start[:, None] >= g_end[None, :]).astype(jnp.int32), axis=-1),
        N_EXPERTS - 1)
    prev = jnp.concatenate([jnp.full((1,), -1, jnp.int32), blk_eid[:-1]])
    first = (blk_eid != prev).astype(jnp.int32)
    n_act = (g_end[-1] // EXPERT_TILE).astype(jnp.int32).reshape(1)

    cpb = EXPERT_TILE // CHUNK
    run_start = gpos.T // CHUNK
    run_len = cntp.T // CHUNK
    run_src = (jnp.arange(nt, dtype=jnp.int32)[:, None] * (LT_MAX // CHUNK) + loff // CHUNK).T
    tabs = jnp.stack([run_start, run_len, run_src], axis=0)
    dtabs = tabs - jnp.concatenate(
        [jnp.zeros((3, N_EXPERTS, 1), jnp.int32), tabs[:, :, :-1]], axis=2)
    blk_sel = (blk_eid[:, None] == experts)[:, :, None]
    blk_start_t = jnp.sum(jnp.where(blk_sel, run_start[None], 0), axis=1)
    blk_dtabs = jnp.sum(jnp.where(blk_sel[None], dtabs[:, None], 0), axis=2)
    gc = jnp.arange(n_rows // CHUNK, dtype=jnp.int32).reshape(n_blocks, cpb)
    started = blk_start_t[:, None, :] <= gc[:, :, None]
    picked = jnp.sum(jnp.where(started[None], blk_dtabs[:, :, None, :], 0), axis=-1)
    within = gc - picked[0]
    csrc = jnp.where(within < picked[1], picked[2] + within, 0).reshape(-1)
    return gch.astype(jnp.int32), csrc.astype(jnp.int32), blk_eid, first, n_act, n_rows


def kernel(x, norm_mix, w_in, q_norm, k_norm, sinks, w_pool, pool_scale, w_out,
           norm_ffn, w_coarse, b_coarse, w_fine, b_fine, w_gate, w_up, w_down):
    bsz, seq, d = x.shape
    n = bsz * seq
    assert d == D_MODEL and seq % ROW_TILE == 0 and seq % BLOCK == 0
    assert norm_mix.shape[0] == 1, "single-layer problem"
    xf = x.reshape(n, d)

    w_in_b = w_in[0].astype(BF16)
    wq = w_in_b[:, :ATTN_WIDTH]
    wk = w_in_b[:, ATTN_WIDTH:ATTN_WIDTH + KV_WIDTH]
    wv = w_in_b[:, ATTN_WIDTH + KV_WIDTH:ATTN_WIDTH + 2 * KV_WIDTH]
    wu = w_in_b[:, ATTN_WIDTH + 2 * KV_WIDTH:]
    qn = jnp.tile(q_norm[0], N_HEADS).reshape(1, ATTN_WIDTH)
    kn = jnp.tile(k_norm[0], N_KV_HEADS).reshape(1, KV_WIDTH)
    cc, s_up, s_dn = _rope_tables(seq)
    lane_head = jnp.arange(MXU_DIM) // HEAD_DIM
    e_mat = (lane_head[:, None] == lane_head[None, :]).astype(BF16)
    w_pool_b = w_pool[0].astype(BF16)
    pscale = pool_scale[0].reshape(1, POOL_WIDTH)
    w_out_b = w_out[0].astype(BF16)
    wo_attn, wo_pool = w_out_b[:ATTN_WIDTH], w_out_b[ATTN_WIDTH:]
    n_router = N_EXPERT_GROUPS + N_EXPERTS
    w_r = jnp.concatenate([w_coarse[0], w_fine[0]], axis=1)
    w_r = jnp.pad(w_r, ((0, 0), (0, LANES - n_router)))
    w_r_hi = w_r.astype(BF16)
    w_r_lo = (w_r - w_r_hi.astype(F32)).astype(BF16)
    w_r2 = jnp.concatenate([w_r_hi, w_r_lo], axis=1)
    b_r = jnp.pad(jnp.concatenate([b_coarse[0], b_fine[0]]),
                  (0, LANES - n_router)).reshape(1, LANES)

    tm = ROW_TILE
    tiles_per_seq = seq // tm
    n_tiles = n // tm
    n_sub = n // SUB_TILE
    idx = jnp.arange(SUB_TILE)
    tri = (idx[None, :] < idx[:, None]).astype(BF16)
    const = lambda *_: (0, 0)
    row_blk = lambda i: (i, 0)

    q, k, v, ksw, vsw, y = pl.pallas_call(
        functools.partial(_inproj_kernel, tiles_per_seq=tiles_per_seq),
        grid=(n_tiles,),
        in_specs=[
            pl.BlockSpec((tm, d), row_blk),
            pl.BlockSpec((1, d), const),
            pl.BlockSpec((d, ATTN_WIDTH), const),
            pl.BlockSpec((d, KV_WIDTH), const),
            pl.BlockSpec((d, KV_WIDTH), const),
            pl.BlockSpec((d, POOL_WIDTH), const),
            pl.BlockSpec((1, ATTN_WIDTH), const),
            pl.BlockSpec((1, KV_WIDTH), const),
            pl.BlockSpec((tm, LANES), lambda i: (i % tiles_per_seq, 0)),
            pl.BlockSpec((tm, LANES), lambda i: (i % tiles_per_seq, 0)),
            pl.BlockSpec((tm, LANES), lambda i: (i % tiles_per_seq, 0)),
            pl.BlockSpec((MXU_DIM, MXU_DIM), const),
            pl.BlockSpec((len(POOL_WINDOWS), POOL_GROUP_DIM, POOL_GROUP_DIM),
                         lambda i: (0, 0, 0)),
            pl.BlockSpec((1, POOL_WIDTH), const),
        ],
        out_specs=[
            pl.BlockSpec((tm, ATTN_WIDTH), row_blk),
            pl.BlockSpec((tm, KV_WIDTH), row_blk),
            pl.BlockSpec((tm, KV_WIDTH), row_blk),
            pl.BlockSpec((tm, KV_WIDTH), row_blk),
            pl.BlockSpec((tm, KV_WIDTH), row_blk),
            pl.BlockSpec((tm, POOL_WIDTH), row_blk),
        ],
        out_shape=[
            jax.ShapeDtypeStruct((n, ATTN_WIDTH), BF16),
            jax.ShapeDtypeStruct((n, KV_WIDTH), BF16),
            jax.ShapeDtypeStruct((n, KV_WIDTH), BF16),
            jax.ShapeDtypeStruct((n, KV_WIDTH), BF16),
            jax.ShapeDtypeStruct((n, KV_WIDTH), BF16),
            jax.ShapeDtypeStruct((n, POOL_WIDTH), BF16),
        ],
        scratch_shapes=[pltpu.VMEM((HALO + tm, POOL_WIDTH), F32)],
        compiler_params=_cparams(),
        name="inproj",
    )(xf, norm_mix[0].reshape(1, d), wq, wk, wv, wu, qn, kn, cc, s_up, s_dn,
      e_mat, w_pool_b, pscale)

    nb = seq // BLOCK
    nq = nb // Q_BLOCKS
    qrows = Q_BLOCKS * BLOCK
    cur = lambda b, j, *_: (b * nq + j, 0)
    prv = lambda b, j, *_: (b * nb + jnp.maximum(Q_BLOCKS * j - 1, 0), 0)
    kv_specs = [pl.BlockSpec((qrows, KV_WIDTH), cur), pl.BlockSpec((BLOCK, KV_WIDTH), prv)]
    attn = pl.pallas_call(
        _attn_kernel,
        grid_spec=pltpu.PrefetchScalarGridSpec(
            num_scalar_prefetch=1,
            grid=(bsz, nq),
            in_specs=[pl.BlockSpec((qrows, ATTN_WIDTH), cur)] + kv_specs * 4,
            out_specs=pl.BlockSpec((qrows, ATTN_WIDTH), cur),
        ),
        out_shape=jax.ShapeDtypeStruct((n, ATTN_WIDTH), BF16),
        compiler_params=_cparams(2),
        name="swa_attn",
    )(sinks[0].astype(F32), q, k, k, ksw, ksw, v, v, vsw, vsw)

    h, hpk, route, cnt = pl.pallas_call(
        _outproj_router_kernel,
        grid=(n_tiles,),
        in_specs=[
            pl.BlockSpec((tm, ATTN_WIDTH), row_blk),
            pl.BlockSpec((tm, POOL_WIDTH), row_blk),
            pl.BlockSpec((tm, d), row_blk),
            pl.BlockSpec((ATTN_WIDTH, d), const),
            pl.BlockSpec((POOL_WIDTH, d), const),
            pl.BlockSpec((1, d), const),
            pl.BlockSpec((d, 2 * LANES), const),
            pl.BlockSpec((1, LANES), const),
            pl.BlockSpec((SUB_TILE, SUB_TILE), const),
        ],
        out_specs=[
            pl.BlockSpec((tm, d), row_blk),
            pl.BlockSpec((tm * PK_ROWS, LANES), row_blk),
            pl.BlockSpec((SUBLANES, tm), lambda i: (0, i)),
            pl.BlockSpec((SUBLANES, LANES), row_blk),
        ],
        out_shape=[
            jax.ShapeDtypeStruct((n, d), F32),
            jax.ShapeDtypeStruct((n * PK_ROWS, LANES), jnp.int32),
            jax.ShapeDtypeStruct((SUBLANES, n), F32),
            jax.ShapeDtypeStruct((n_tiles * SUBLANES, LANES), F32),
        ],
        compiler_params=_cparams(),
        name="outproj_router",
    )(attn, y, xf, wo_attn, wo_pool, norm_ffn[0].reshape(1, d), w_r2, b_r, tri)

    assert tm == LOCAL_TILE
    gates = route[0:TOP_K].reshape(TOP_K * n)
    lp = route[TOP_K:2 * TOP_K].astype(jnp.int32).reshape(TOP_K * n)
    tile_cnt = cnt.reshape(n_tiles, SUBLANES, LANES)[:, 0, :N_EXPERTS].astype(jnp.int32)
    te = EXPERT_TILE
    gch, csrc, blk_eid, first, n_act, n_rows = _plan(tile_cnt, n)
    n_blocks = n_rows // te
    n_local = n // LOCAL_TILE

    xs = pl.pallas_call(
        _local_sort_kernel,
        grid_spec=pltpu.PrefetchScalarGridSpec(
            num_scalar_prefetch=1,
            grid=(n_local,),
            in_specs=[pl.BlockSpec((LOCAL_TILE * PK_ROWS, LANES), lambda i, *_: (i, 0))],
            out_specs=pl.BlockSpec((LT_MAX * PK_ROWS, LANES), lambda i, *_: (i, 0)),
        ),
        out_shape=jax.ShapeDtypeStruct((n_local * LT_MAX * PK_ROWS, LANES), jnp.int32),
        compiler_params=_cparams(),
        name="local_sort",
    )(lp * PK_ROWS, hpk)

    w_idx = lambda i, eid_r, *_: (eid_r[i], 0, 0)
    ys = pl.pallas_call(
        _expert_kernel,
        grid_spec=pltpu.PrefetchScalarGridSpec(
            num_scalar_prefetch=4,
            grid=(n_blocks,),
            in_specs=[
                pl.BlockSpec(memory_space=pl.ANY),
                pl.BlockSpec((1, d, D_EXPERT), w_idx),
                pl.BlockSpec((1, d, D_EXPERT), w_idx),
                pl.BlockSpec((1, D_EXPERT, d), w_idx),
            ],
            out_specs=pl.BlockSpec((te * PK_ROWS, LANES), lambda i, *_: (i, 0)),
            scratch_shapes=[
                pltpu.VMEM((2, te * PK_ROWS, LANES), jnp.int32),
                pltpu.VMEM((d, D_EXPERT), BF16),
                pltpu.VMEM((d, D_EXPERT), BF16),
                pltpu.VMEM((D_EXPERT, d), BF16),
                pltpu.SemaphoreType.DMA((2,)),
            ],
        ),
        out_shape=jax.ShapeDtypeStruct((n_rows * PK_ROWS, LANES), jnp.int32),
        compiler_params=_cparams(vmem_limit=VMEM_LIMIT_EXPERTS),
        name="experts",
    )(blk_eid, first, n_act, csrc * (CHUNK * PK_ROWS), xs, w_gate[0], w_up[0], w_down[0])

    tt = LOCAL_TILE
    out = pl.pallas_call(
        _combine_kernel,
        grid_spec=pltpu.PrefetchScalarGridSpec(
            num_scalar_prefetch=3,
            grid=(n_local,),
            in_specs=[
                pl.BlockSpec((tt, d), lambda i, *_: (i, 0)),
                pl.BlockSpec(memory_space=pl.ANY),
            ],
            out_specs=pl.BlockSpec((tt, d), lambda i, *_: (i, 0)),
            scratch_shapes=[
                pltpu.VMEM((2, LT_MAX * PK_ROWS, LANES), jnp.int32),
                pltpu.VMEM((2, tt * PK_ROWS, LANES), F32),
                pltpu.SemaphoreType.DMA((2,)),
            ],
        ),
        out_shape=jax.ShapeDtypeStruct((n, d), F32),
        compiler_params=_cparams(),
        name="combine",
    )(gch * (CHUNK * PK_ROWS), lp * PK_ROWS, gates, h, ys)
    return out.reshape(bsz, seq, d)
```

```python
import functools

import jax
import jax.numpy as jnp
from jax import lax
from jax.experimental import pallas as pl
from jax.experimental.pallas import tpu as pltpu

F32 = jnp.float32
BF16 = jnp.bfloat16

D_MODEL = 2048
N_HEADS = 16
N_KV_HEADS = 4
HEAD_DIM = 64
GROUP = N_HEADS // N_KV_HEADS
ROT_DIM = HEAD_DIM // 4
ROPE_THETA = 500000.0
WINDOW = 128
BLOCK = 128
Q_BLOCKS = 2
ATTN_WIDTH = N_HEADS * HEAD_DIM
KV_WIDTH = N_KV_HEADS * HEAD_DIM
POOL_WINDOWS = (2, 4, 8, 16)
POOL_WIDTH = D_MODEL // 2
POOL_GROUP_DIM = POOL_WIDTH // len(POOL_WINDOWS)
N_EXPERT_GROUPS = 4
EXPERTS_PER_GROUP = 8
N_EXPERTS = N_EXPERT_GROUPS * EXPERTS_PER_GROUP
TOP_K = 2
D_EXPERT = 512
EPS = 1e-6

LANES = 128
SUBLANES = 8
MXU_DIM = 256
HALO = 16
NEG_BIG = -1e30
HALF = D_MODEL // 2
PK_ROWS = HALF // LANES
Y_ROWS = D_MODEL // LANES

ROW_TILE = 512
SUB_TILE = 512
EXPERT_TILE = 512
LOCAL_TILE = 512
CHUNK = 8
LT_MAX = TOP_K * LOCAL_TILE + N_EXPERTS * CHUNK
VMEM_LIMIT = 56 * 1024 * 1024
VMEM_LIMIT_EXPERTS = 60 * 1024 * 1024


def _cparams(n_axes=1, vmem_limit=VMEM_LIMIT, flags=None):
    return pltpu.CompilerParams(
        dimension_semantics=("arbitrary",) * n_axes,
        vmem_limit_bytes=vmem_limit,
        flags=flags,
    )


def _pack_halves(t):
    return pltpu.pack_elementwise([t[:, :HALF], t[:, HALF:]], packed_dtype=BF16)


def _unpack_halves(w):
    return tuple(pltpu.unpack_elementwise(w, index=i, packed_dtype=BF16, unpacked_dtype=F32)
                 for i in range(2))


def _head_sumsq(t, e_ref):
    t2 = (t * t).astype(BF16)
    e = e_ref[...]
    outs = []
    for c in range(t.shape[1] // MXU_DIM):
        sl = slice(c * MXU_DIM, (c + 1) * MXU_DIM)
        outs.append(jnp.dot(t2[:, sl], e, preferred_element_type=F32))
    return outs[0] if len(outs) == 1 else jnp.concatenate(outs, axis=1)


def _norm_rope(t, gain, e_ref, cc, s_up, s_dn, out_scale):
    w = t.shape[1]
    ss = _head_sumsq(t, e_ref)
    tn = t * lax.rsqrt(ss * (1.0 / HEAD_DIM) + EPS) * gain
    reps = w // LANES
    up = pltpu.roll(tn, w - ROT_DIM // 2, axis=1)
    dn = pltpu.roll(tn, ROT_DIM // 2, axis=1)
    c = jnp.concatenate([cc] * reps, axis=1)
    su = jnp.concatenate([s_up] * reps, axis=1)
    sd = jnp.concatenate([s_dn] * reps, axis=1)
    return (tn * c + up * su + dn * sd) * out_scale


def _swap_head_pairs(t):
    w = t.shape[1]
    lane = lax.broadcasted_iota(jnp.int32, t.shape, 1)
    from_up = pltpu.roll(t, w - HEAD_DIM, axis=1)
    from_dn = pltpu.roll(t, HEAD_DIM, axis=1)
    return jnp.where((lane & (LANES - 1)) < HEAD_DIM, from_up, from_dn)


def _inproj_kernel(x_ref, g_ref, wq_ref, wk_ref, wv_ref, wu_ref, qn_ref, kn_ref,
                   cc_ref, su_ref, sd_ref, e_ref, wpool_ref, pscale_ref,
                   q_out, k_out, v_out, ksw_out, vsw_out, y_out, ubuf, *, tiles_per_seq):
    i = pl.program_id(0)
    tm = x_ref.shape[0]

    @pl.when(i % tiles_per_seq == 0)
    def _():
        ubuf[0:HALO, :] = jnp.zeros((HALO, POOL_WIDTH), F32)

    for r0 in range(0, tm, SUB_TILE):
        rows = slice(r0, r0 + SUB_TILE)
        x = x_ref[rows, :]
        ms = jnp.mean(x * x, axis=-1, keepdims=True)
        hn = (x * lax.rsqrt(ms + EPS) * g_ref[...]).astype(BF16)

        cc, s_up, s_dn = cc_ref[rows, :], su_ref[rows, :], sd_ref[rows, :]
        q = jnp.dot(hn, wq_ref[...], preferred_element_type=F32)
        q_out[rows, :] = _norm_rope(q, qn_ref[...], e_ref, cc, s_up, s_dn,
                                    HEAD_DIM ** -0.5).astype(BF16)
        k = jnp.dot(hn, wk_ref[...], preferred_element_type=F32)
        k = _norm_rope(k, kn_ref[...], e_ref, cc, s_up, s_dn, 1.0)
        v = jnp.dot(hn, wv_ref[...], preferred_element_type=F32)
        k_out[rows, :] = k.astype(BF16)
        v_out[rows, :] = v.astype(BF16)
        ksw_out[rows, :] = _swap_head_pairs(k).astype(BF16)
        vsw_out[rows, :] = _swap_head_pairs(v).astype(BF16)

        base = HALO + r0
        ubuf[base:base + SUB_TILE, :] = jnp.dot(hn, wu_ref[...], preferred_element_type=F32)
        pos = (i % tiles_per_seq) * tm + r0 + lax.broadcasted_iota(
            jnp.int32, (SUB_TILE, POOL_GROUP_DIM), 0)
        for g, w in enumerate(POOL_WINDOWS):
            cols = slice(g * POOL_GROUP_DIM, (g + 1) * POOL_GROUP_DIM)
            assert w & (w - 1) == 0 and w - 1 < HALO
            acc = ubuf[base - HALO:base + SUB_TILE, cols]
            shift = 1
            while shift < w:
                acc = acc + pltpu.roll(acc, shift, axis=0)
                shift *= 2
            acc = acc[HALO:, :]
            u_g = ubuf[base:base + SUB_TILE, cols]
            cnt = jnp.minimum(pos + 1, w).astype(F32)
            d = (acc / cnt - u_g).astype(BF16)
            y = jnp.dot(d, wpool_ref[g], preferred_element_type=F32)
            y_out[rows, cols] = (y * pscale_ref[:, cols]).astype(BF16)
    ubuf[0:HALO, :] = ubuf[tm:tm + HALO, :]


def _attn_kernel(sinks_ref, q_ref, kc_ref, kp_ref, kswc_ref, kswp_ref,
                 vc_ref, vp_ref, vswc_ref, vswp_ref, o_ref):
    j = pl.program_id(1)
    two = 2 * BLOCK
    row = lax.broadcasted_iota(jnp.int32, (two, two), 0) & (BLOCK - 1)
    col = lax.broadcasted_iota(jnp.int32, (two, two), 1)
    diff = row + BLOCK - col
    local = (diff >= 0) & (diff < WINDOW)
    lo_lanes = lax.broadcasted_iota(jnp.int32, (two, LANES), 1) < HEAD_DIM
    top_rows = lax.broadcasted_iota(jnp.int32, (two, 1), 0) < BLOCK
    zero = jnp.zeros((two, LANES), BF16)

    for sb in range(q_ref.shape[0] // BLOCK):
        rows = slice(sb * BLOCK, (sb + 1) * BLOCK)
        if sb == 0:
            def band(cur_ref, prev_ref):
                return jnp.concatenate([prev_ref[...], cur_ref[0:BLOCK, :]], axis=0)
            valid = local & ((col >= BLOCK) | (j > 0))
        else:
            def band(cur_ref, prev_ref, sb=sb):
                return cur_ref[(sb - 1) * BLOCK:(sb + 1) * BLOCK, :]
            valid = local
        k_nat, k_swp = band(kc_ref, kp_ref), band(kswc_ref, kswp_ref)
        v_nat, v_swp = band(vc_ref, vp_ref), band(vswc_ref, vswp_ref)

        for g in range(N_KV_HEADS):
            c0 = (g // 2) * LANES
            kcols = (k_nat[:, c0:c0 + LANES], k_swp[:, c0:c0 + LANES])
            vcols = (v_nat[:, c0:c0 + LANES], v_swp[:, c0:c0 + LANES])
            in_lo, in_hi = (0, 1) if g % 2 == 0 else (1, 0)
            q0 = g * GROUP * HEAD_DIM
            qq = jnp.concatenate([q_ref[rows, q0:q0 + LANES],
                                  q_ref[rows, q0 + LANES:q0 + 2 * LANES]], axis=0)
            halves = []
            for half, src in ((0, in_lo), (1, in_hi)):
                qm = jnp.where(lo_lanes, qq, zero) if half == 0 else jnp.where(lo_lanes, zero, qq)
                s = lax.dot_general(qm, kcols[src], (((1,), (1,)), ((), ())),
                                    preferred_element_type=F32)
                s = jnp.where(valid, s, NEG_BIG)
                h_top = g * GROUP + half
                sink = jnp.where(top_rows, sinks_ref[h_top], sinks_ref[h_top + 2])
                m = jnp.maximum(jnp.max(s, axis=-1, keepdims=True), sink)
                p = jnp.exp(s - m)
                denom = jnp.sum(p, axis=-1, keepdims=True) + jnp.exp(sink - m)
                o = jnp.dot(p.astype(BF16), vcols[src], preferred_element_type=F32)
                halves.append(o / denom)
            o_pair = jnp.where(lo_lanes, halves[0], halves[1]).astype(BF16)
            o_ref[rows, q0:q0 + LANES] = o_pair[0:BLOCK, :]
            o_ref[rows, q0 + LANES:q0 + 2 * LANES] = o_pair[BLOCK:, :]


def _outproj_router_kernel(o_ref, y_ref, x_ref, woa_ref, wop_ref, g_ref, wr_ref,
                           br_ref, tri_ref, h_out, hpk_out, route_out, cnt_out):
    n_sub = x_ref.shape[0] // SUB_TILE
    picks = [_outproj_router_subtile(sub, o_ref, y_ref, x_ref, woa_ref, wop_ref, g_ref,
                                     wr_ref, br_ref, tri_ref, h_out, hpk_out)
             for sub in range(n_sub)]

    lane = lax.broadcasted_iota(jnp.int32, (SUB_TILE, LANES), 1)
    lane_f = lane.astype(F32)
    cnt_tile = sum(pk[5] for pk in picks)
    cnt_pad = jnp.floor((cnt_tile + (CHUNK - 1)) * (1.0 / CHUNK)) * CHUNK
    run_start = lax.dot_general(
        jnp.broadcast_to(cnt_pad, (SUBLANES, LANES)).astype(BF16),
        tri_ref[0:LANES, 0:LANES], (((1,), (1,)), ((), ())),
        preferred_element_type=F32)[0:1, :]
    before = jnp.zeros((1, LANES), F32)
    for sub, (e1, e2, g1, g2, prefix, cnt_sub) in enumerate(picks):
        table = prefix + (run_start + before)
        lp1 = jnp.sum(jnp.where(lane_f == e1, table, 0.0), axis=-1, keepdims=True)
        lp2 = jnp.sum(jnp.where(lane_f == e2, table, 0.0), axis=-1, keepdims=True)
        cols = jnp.where(lane == 0, g1,
                         jnp.where(lane == 1, g2,
                                   jnp.where(lane == 2, lp1,
                                             jnp.where(lane == 3, lp2, 0.0))))
        pick = (lax.broadcasted_iota(jnp.int32, (SUBLANES, LANES), 0)
                == lax.broadcasted_iota(jnp.int32, (SUBLANES, LANES), 1)).astype(F32)
        route_out[:, sub * SUB_TILE:(sub + 1) * SUB_TILE] = lax.dot_general(
            pick, cols, (((1,), (1,)), ((), ())), precision=lax.Precision.HIGHEST,
            preferred_element_type=F32)
        before = before + cnt_sub
    cnt_out[...] = jnp.broadcast_to(cnt_tile, cnt_out.shape)


def _outproj_router_subtile(sub, o_ref, y_ref, x_ref, woa_ref, wop_ref, g_ref, wr_ref,
                            br_ref, tri_ref, h_out, hpk_out):
    tm = SUB_TILE
    r0 = sub * SUB_TILE
    rows = slice(r0, r0 + SUB_TILE)
    h = (x_ref[rows, :]
         + jnp.dot(o_ref[rows, :], woa_ref[...], preferred_element_type=F32)
         + jnp.dot(y_ref[rows, :], wop_ref[...], preferred_element_type=F32))
    h_out[rows, :] = h
    ms = jnp.mean(h * h, axis=-1, keepdims=True)
    hn = h * lax.rsqrt(ms + EPS) * g_ref[...]
    hb = hn.astype(BF16)
    packed = _pack_halves(hn)
    for s in range(PK_ROWS):
        hpk_out[pl.ds(r0 * PK_ROWS + s, tm, stride=PK_ROWS), :] = (
            packed[:, s * LANES:(s + 1) * LANES])

    both = jnp.dot(hb, wr_ref[...], preferred_element_type=F32)
    logits = both[:, :LANES] + both[:, LANES:] + br_ref[...]
    lane = lax.broadcasted_iota(jnp.int32, (tm, LANES), 1)
    lane_f = lane.astype(F32)

    def first_lane_of(mask):
        return jnp.min(jnp.where(mask, lane_f, float(LANES)), axis=-1, keepdims=True)

    def row_max(t):
        return jnp.max(t, axis=-1, keepdims=True)

    coarse = jnp.where(lane < N_EXPERT_GROUPS, logits, NEG_BIG)
    cmax = row_max(coarse)
    grp = first_lane_of(coarse == cmax)
    csum = jnp.sum(jnp.exp(coarse - cmax), axis=-1, keepdims=True)
    flo = N_EXPERT_GROUPS + EXPERTS_PER_GROUP * grp
    fine = jnp.where((lane_f >= flo) & (lane_f < flo + EXPERTS_PER_GROUP), logits, NEG_BIG)
    f1 = row_max(fine)
    i1 = first_lane_of(fine == f1)
    rest = jnp.where(lane_f == i1, NEG_BIG, fine)
    f2 = row_max(rest)
    i2 = first_lane_of(rest == f2)
    ratio = jnp.exp(f2 - f1)
    scale = (1.0 / csum) / (1.0 + ratio)
    g1 = scale
    g2 = scale * ratio
    e1 = i1 - N_EXPERT_GROUPS
    e2 = i2 - N_EXPERT_GROUPS

    oh = jnp.where(lane_f == e1, 1.0, 0.0) + jnp.where(lane_f == e2, 1.0, 0.0)
    prefix = jnp.dot(tri_ref[...], oh.astype(BF16), preferred_element_type=F32)
    return e1, e2, g1, g2, prefix, jnp.sum(oh, axis=0, keepdims=True)


def _expert_kernel(eid_ref, first_ref, nact_ref,
                   xs_ref, wg_ref, wu_ref, wd_ref, ys_out, wg_s, wu_s, wd_s):
    i = pl.program_id(0)
    tm = ys_out.shape[0] // PK_ROWS
    n_act = nact_ref[0]

    @pl.when(i < n_act)
    def _():
        @pl.when(first_ref[i] == 1)
        def _():
            wg_s[...] = wg_ref[0].astype(BF16)
            wu_s[...] = wu_ref[0].astype(BF16)
            wd_s[...] = wd_ref[0].astype(BF16)

        xu = jnp.concatenate(
            [xs_ref[pl.ds(s, tm, stride=PK_ROWS), :] for s in range(PK_ROWS)],
            axis=1)
        x_lo, x_hi = (v.astype(BF16) for v in _unpack_halves(xu))
        a = (jnp.dot(x_lo, wg_s[0:HALF, :], preferred_element_type=F32)
             + jnp.dot(x_hi, wg_s[HALF:, :], preferred_element_type=F32))
        b = (jnp.dot(x_lo, wu_s[0:HALF, :], preferred_element_type=F32)
             + jnp.dot(x_hi, wu_s[HALF:, :], preferred_element_type=F32))
        mid = (a * jax.nn.sigmoid(a) * b).astype(BF16)
        y = jnp.dot(mid, wd_s[...], preferred_element_type=F32)
        yw = _pack_halves(y)
        for s in range(PK_ROWS):
            ys_out[pl.ds(s, tm, stride=PK_ROWS), :] = yw[:, s * LANES:(s + 1) * LANES]

    @pl.when(i >= n_act)
    def _():
        ys_out[...] = jnp.zeros(ys_out.shape, ys_out.dtype)


def _local_sort_kernel(lp_ref, gch_ref, nv_ref, gap_ref, hpk_ref, xs_hbm,
                       stile, zbuf, sems, zsem):
    i = pl.program_id(0)
    n = pl.num_programs(0)
    t_tile = hpk_ref.shape[0] // PK_ROWS
    n_tok = n * t_tile
    ch_rows = CHUNK * PK_ROWS
    n_ch = stile.shape[1] // ch_rows
    blk_rows = zbuf.shape[0]
    slot = i % 2

    def zero_gaps(act):
        for e in range(N_EXPERTS):
            def body(c, carry, e=e):
                dst = pl.multiple_of(gap_ref[2 * e] + c * ch_rows, ch_rows)
                act(pltpu.make_async_copy(zbuf.at[pl.ds(0, ch_rows)],
                                          xs_hbm.at[pl.ds(dst, ch_rows)], zsem.at[0]))
                return carry
            lax.fori_loop(0, gap_ref[2 * e + 1], body, 0)

        def tail_body(b, carry):
            dst = pl.multiple_of(gap_ref[2 * N_EXPERTS] + b * blk_rows, ch_rows)
            act(pltpu.make_async_copy(zbuf, xs_hbm.at[pl.ds(dst, blk_rows)], zsem.at[0]))
            return carry
        lax.fori_loop(0, gap_ref[2 * N_EXPERTS + 1], tail_body, 0)

    @pl.when(i == 0)
    def _():
        zbuf[...] = jnp.zeros(zbuf.shape, zbuf.dtype)
        zero_gaps(lambda cp: cp.start())

    def chunk_copy(tile, sl, lc):
        dst = pl.multiple_of(gch_ref[tile * n_ch + lc], ch_rows)
        src = pl.multiple_of(lc * ch_rows, ch_rows)
        return pltpu.make_async_copy(stile.at[sl, pl.ds(src, ch_rows)],
                                     xs_hbm.at[pl.ds(dst, ch_rows)], sems.at[sl])

    def for_valid_chunks(tile, fn):
        def body(lc, carry):
            fn(lc)
            return carry
        lax.fori_loop(0, nv_ref[tile], body, 0)

    @pl.when(i >= 2)
    def _():
        for_valid_chunks(i - 2, lambda lc: chunk_copy(i - 2, slot, lc).wait())

    stile[slot] = jnp.zeros(stile.shape[1:], stile.dtype)

    def body(q, carry):
        for u in range(SUBLANES):
            t = q * SUBLANES + u
            row = hpk_ref[pl.ds(pl.multiple_of(t * PK_ROWS, PK_ROWS), PK_ROWS), :]
            for k in range(TOP_K):
                dst = pl.multiple_of(lp_ref[k * n_tok + i * t_tile + t], PK_ROWS)
                stile[slot, pl.ds(dst, PK_ROWS), :] = row
        return carry
    lax.fori_loop(0, t_tile // SUBLANES, body, 0)

    for_valid_chunks(i, lambda lc: chunk_copy(i, slot, lc).start())

    @pl.when(i == n - 1)
    def _():
        @pl.when(n >= 2)
        def _():
            for_valid_chunks(i - 1, lambda lc: chunk_copy(i - 1, 1 - slot, lc).wait())
        for_valid_chunks(i, lambda lc: chunk_copy(i, slot, lc).wait())
        zero_gaps(lambda cp: cp.wait())


def _combine_kernel(gch_ref, lp_ref, gate_ref, h_ref, ys_hbm, out_ref, ylocal, acc, sems):
    i = pl.program_id(0)
    n = pl.num_programs(0)
    t_tile = h_ref.shape[0]
    n_tok = n * t_tile
    ch_rows = CHUNK * PK_ROWS
    n_ch = ylocal.shape[1] // ch_rows

    def chunk_copy(tile, slot, lc):
        src = pl.multiple_of(gch_ref[tile * n_ch + lc], ch_rows)
        dst = pl.multiple_of(lc * ch_rows, ch_rows)
        pltpu.make_async_copy(ys_hbm.at[pl.ds(src, ch_rows)],
                              ylocal.at[slot, pl.ds(dst, ch_rows)],
                              sems.at[slot]).start()

    def wait_slot(slot):
        pltpu.make_async_copy(ys_hbm.at[pl.ds(0, n_ch * ch_rows)], ylocal.at[slot],
                              sems.at[slot]).wait()

    @pl.when(i == 0)
    def _():
        def body(lc, carry):
            chunk_copy(0, 0, lc)
            return carry
        lax.fori_loop(0, n_ch, body, 0)

    slot = i % 2
    wait_slot(slot)

    nxt = jnp.minimum(i + 1, n - 1)
    tok_unroll = 4
    n_groups = t_tile // tok_unroll
    per_group = -(-n_ch // n_groups)
    issue_groups = n_ch // per_group
    assert issue_groups * per_group == n_ch and issue_groups <= n_groups

    def tok_body(q, carry, issue):
        if issue:
            for u in range(per_group):
                chunk_copy(nxt, 1 - slot, q * per_group + u)
        for u in range(tok_unroll):
            t = q * tok_unroll + u
            a0 = i * t_tile + t
            a1 = n_tok + a0
            p0 = pl.multiple_of(lp_ref[a0], PK_ROWS)
            p1 = pl.multiple_of(lp_ref[a1], PK_ROWS)
            lo0, hi0 = _unpack_halves(ylocal[slot, pl.ds(p0, PK_ROWS), :])
            lo1, hi1 = _unpack_halves(ylocal[slot, pl.ds(p1, PK_ROWS), :])
            g0 = gate_ref[a0]
            g1 = gate_ref[a1]
            dst = pl.multiple_of(t * PK_ROWS, PK_ROWS)
            acc[0, pl.ds(dst, PK_ROWS), :] = g0 * lo0 + g1 * lo1
            acc[1, pl.ds(dst, PK_ROWS), :] = g0 * hi0 + g1 * hi1
        return carry
    lax.fori_loop(0, issue_groups, functools.partial(tok_body, issue=True), 0)
    lax.fori_loop(issue_groups, n_groups, functools.partial(tok_body, issue=False), 0)

    @pl.when(i == n - 1)
    def _():
        wait_slot(1 - slot)

    for half in range(2):
        for s in range(PK_ROWS):
            c0 = half * HALF + s * LANES
            out_ref[:, c0:c0 + LANES] = (h_ref[:, c0:c0 + LANES]
                                         + acc[half, pl.ds(s, t_tile, stride=PK_ROWS), :])


def _rope_tables(seq):
    pos = jnp.arange(seq, dtype=F32)
    inv_freq = ROPE_THETA ** (-jnp.arange(0, ROT_DIM, 2, dtype=F32) / ROT_DIM)
    ang = pos[:, None] * inv_freq[None, :]
    cos, sin = jnp.cos(ang), jnp.sin(ang)
    half = ROT_DIM // 2
    ones = jnp.ones((seq, HEAD_DIM - ROT_DIM), F32)
    zeros = jnp.zeros((seq, HEAD_DIM - ROT_DIM), F32)
    zh = jnp.zeros((seq, half), F32)
    cc = jnp.concatenate([cos, cos, ones], axis=1)
    s_up = jnp.concatenate([-sin, zh, zeros], axis=1)
    s_dn = jnp.concatenate([zh, sin, zeros], axis=1)
    rep = LANES // HEAD_DIM
    return tuple(jnp.tile(t, (1, rep)) for t in (cc, s_up, s_dn))


def _plan(cnt, n_tokens):
    nt = n_tokens // LOCAL_TILE
    experts = jnp.arange(N_EXPERTS, dtype=jnp.int32)
    cntp = ((cnt + CHUNK - 1) // CHUNK) * CHUNK
    loff_end = jnp.cumsum(cntp, axis=1)
    loff = loff_end - cntp
    rows_e = jnp.sum(cntp, axis=0)
    rows_pad = ((rows_e + EXPERT_TILE - 1) // EXPERT_TILE) * EXPERT_TILE
    g_end = jnp.cumsum(rows_pad)
    g_start = g_end - rows_pad
    gpos = g_start[None, :] + jnp.cumsum(cntp, axis=0) - cntp

    lrow = jnp.arange(LT_MAX // CHUNK, dtype=jnp.int32) * CHUNK
    e_lc = jnp.sum((lrow[None, :, None] >= loff_end[:, None, :]).astype(jnp.int32), axis=-1)
    sel = e_lc[:, :, None] == experts
    delta = jnp.sum(jnp.where(sel, (gpos - loff)[:, None, :], 0), axis=-1)
    gch = jnp.where(e_lc < N_EXPERTS, (delta + lrow[None, :]) // CHUNK, 0).reshape(-1)

    n_rows = TOP_K * n_tokens + nt * N_EXPERTS * (CHUNK - 1) + N_EXPERTS * (EXPERT_TILE - CHUNK)
    n_blocks = -(-n_rows // EXPERT_TILE)
    n_rows = n_blocks * EXPERT_TILE
    blk_start = jnp.arange(n_blocks, dtype=jnp.int32) * EXPERT_TILE
    blk_eid = jnp.minimum(
        jnp.sum((blk_start[:, None] >= g_end[None, :]).astype(jnp.int32), axis=-1),
        N_EXPERTS - 1)
    prev = jnp.concatenate([jnp.full((1,), -1, jnp.int32), blk_eid[:-1]])
    first = (blk_eid != prev).astype(jnp.int32)
    n_act = (g_end[-1] // EXPERT_TILE).astype(jnp.int32).reshape(1)

    n_valid = (loff_end[:, -1] // CHUNK).astype(jnp.int32)
    gaps = jnp.stack([
        jnp.concatenate([g_start + rows_e, g_end[-1:]]) * PK_ROWS,
        jnp.concatenate([(rows_pad - rows_e) // CHUNK, n_blocks - n_act]),
    ], axis=1).reshape(-1).astype(jnp.int32)
    return gch.astype(jnp.int32), n_valid, gaps, blk_eid, first, n_act, n_rows


def kernel(x, norm_mix, w_in, q_norm, k_norm, sinks, w_pool, pool_scale, w_out,
           norm_ffn, w_coarse, b_coarse, w_fine, b_fine, w_gate, w_up, w_down):
    bsz, seq, d = x.shape
    n = bsz * seq
    assert d == D_MODEL and seq % ROW_TILE == 0 and seq % BLOCK == 0
    assert norm_mix.shape[0] == 1, "single-layer problem"
    xf = x.reshape(n, d)

    w_in_b = w_in[0].astype(BF16)
    wq = w_in_b[:, :ATTN_WIDTH]
    wk = w_in_b[:, ATTN_WIDTH:ATTN_WIDTH + KV_WIDTH]
    wv = w_in_b[:, ATTN_WIDTH + KV_WIDTH:ATTN_WIDTH + 2 * KV_WIDTH]
    wu = w_in_b[:, ATTN_WIDTH + 2 * KV_WIDTH:]
    qn = jnp.tile(q_norm[0], N_HEADS).reshape(1, ATTN_WIDTH)
    kn = jnp.tile(k_norm[0], N_KV_HEADS).reshape(1, KV_WIDTH)
    cc, s_up, s_dn = _rope_tables(seq)
    lane_head = jnp.arange(MXU_DIM) // HEAD_DIM
    e_mat = (lane_head[:, None] == lane_head[None, :]).astype(BF16)
    w_pool_b = w_pool[0].astype(BF16)
    pscale = pool_scale[0].reshape(1, POOL_WIDTH)
    w_out_b = w_out[0].astype(BF16)
    wo_attn, wo_pool = w_out_b[:ATTN_WIDTH], w_out_b[ATTN_WIDTH:]
    n_router = N_EXPERT_GROUPS + N_EXPERTS
    w_r = jnp.concatenate([w_coarse[0], w_fine[0]], axis=1)
    w_r = jnp.pad(w_r, ((0, 0), (0, LANES - n_router)))
    w_r_hi = w_r.astype(BF16)
    w_r_lo = (w_r - w_r_hi.astype(F32)).astype(BF16)
    w_r2 = jnp.concatenate([w_r_hi, w_r_lo], axis=1)
    b_r = jnp.pad(jnp.concatenate([b_coarse[0], b_fine[0]]),
                  (0, LANES - n_router)).reshape(1, LANES)

    tm = ROW_TILE
    tiles_per_seq = seq // tm
    n_tiles = n // tm
    n_sub = n // SUB_TILE
    idx = jnp.arange(SUB_TILE)
    tri = (idx[None, :] < idx[:, None]).astype(BF16)
    const = lambda *_: (0, 0)
    row_blk = lambda i: (i, 0)

    q, k, v, ksw, vsw, y = pl.pallas_call(
        functools.partial(_inproj_kernel, tiles_per_seq=tiles_per_seq),
        grid=(n_tiles,),
        in_specs=[
            pl.BlockSpec((tm, d), row_blk),
            pl.BlockSpec((1, d), const),
            pl.BlockSpec((d, ATTN_WIDTH), const),
            pl.BlockSpec((d, KV_WIDTH), const),
            pl.BlockSpec((d, KV_WIDTH), const),
            pl.BlockSpec((d, POOL_WIDTH), const),
            pl.BlockSpec((1, ATTN_WIDTH), const),
            pl.BlockSpec((1, KV_WIDTH), const),
            pl.BlockSpec((tm, LANES), lambda i: (i % tiles_per_seq, 0)),
            pl.BlockSpec((tm, LANES), lambda i: (i % tiles_per_seq, 0)),
            pl.BlockSpec((tm, LANES), lambda i: (i % tiles_per_seq, 0)),
            pl.BlockSpec((MXU_DIM, MXU_DIM), const),
            pl.BlockSpec((len(POOL_WINDOWS), POOL_GROUP_DIM, POOL_GROUP_DIM),
                         lambda i: (0, 0, 0)),
            pl.BlockSpec((1, POOL_WIDTH), const),
        ],
        out_specs=[
            pl.BlockSpec((tm, ATTN_WIDTH), row_blk),
            pl.BlockSpec((tm, KV_WIDTH), row_blk),
            pl.BlockSpec((tm, KV_WIDTH), row_blk),
            pl.BlockSpec((tm, KV_WIDTH), row_blk),
            pl.BlockSpec((tm, KV_WIDTH), row_blk),
            pl.BlockSpec((tm, POOL_WIDTH), row_blk),
        ],
        out_shape=[
            jax.ShapeDtypeStruct((n, ATTN_WIDTH), BF16),
            jax.ShapeDtypeStruct((n, KV_WIDTH), BF16),
            jax.ShapeDtypeStruct((n, KV_WIDTH), BF16),
            jax.ShapeDtypeStruct((n, KV_WIDTH), BF16),
            jax.ShapeDtypeStruct((n, KV_WIDTH), BF16),
            jax.ShapeDtypeStruct((n, POOL_WIDTH), BF16),
        ],
        scratch_shapes=[pltpu.VMEM((HALO + tm, POOL_WIDTH), F32)],
        compiler_params=_cparams(),
        name="inproj",
    )(xf, norm_mix[0].reshape(1, d), wq, wk, wv, wu, qn, kn, cc, s_up, s_dn,
      e_mat, w_pool_b, pscale)

    nb = seq // BLOCK
    nq = nb // Q_BLOCKS
    qrows = Q_BLOCKS * BLOCK
    cur = lambda b, j, *_: (b * nq + j, 0)
    prv = lambda b, j, *_: (b * nb + jnp.maximum(Q_BLOCKS * j - 1, 0), 0)
    kv_specs = [pl.BlockSpec((qrows, KV_WIDTH), cur), pl.BlockSpec((BLOCK, KV_WIDTH), prv)]
    attn = pl.pallas_call(
        _attn_kernel,
        grid_spec=pltpu.PrefetchScalarGridSpec(
            num_scalar_prefetch=1,
            grid=(bsz, nq),
            in_specs=[pl.BlockSpec((qrows, ATTN_WIDTH), cur)] + kv_specs * 4,
            out_specs=pl.BlockSpec((qrows, ATTN_WIDTH), cur),
        ),
        out_shape=jax.ShapeDtypeStruct((n, ATTN_WIDTH), BF16),
        compiler_params=_cparams(2),
        name="swa_attn",
    )(sinks[0].astype(F32), q, k, k, ksw, ksw, v, v, vsw, vsw)

    h, hpk, route, cnt = pl.pallas_call(
        _outproj_router_kernel,
        grid=(n_tiles,),
        in_specs=[
            pl.BlockSpec((tm, ATTN_WIDTH), row_blk),
            pl.BlockSpec((tm, POOL_WIDTH), row_blk),
            pl.BlockSpec((tm, d), row_blk),
            pl.BlockSpec((ATTN_WIDTH, d), const),
            pl.BlockSpec((POOL_WIDTH, d), const),
            pl.BlockSpec((1, d), const),
            pl.BlockSpec((d, 2 * LANES), const),
            pl.BlockSpec((1, LANES), const),
            pl.BlockSpec((SUB_TILE, SUB_TILE), const),
        ],
        out_specs=[
            pl.BlockSpec((tm, d), row_blk),
            pl.BlockSpec((tm * PK_ROWS, LANES), row_blk),
            pl.BlockSpec((SUBLANES, tm), lambda i: (0, i)),
            pl.BlockSpec((SUBLANES, LANES), row_blk),
        ],
        out_shape=[
            jax.ShapeDtypeStruct((n, d), F32),
            jax.ShapeDtypeStruct((n * PK_ROWS, LANES), jnp.int32),
            jax.ShapeDtypeStruct((SUBLANES, n), F32),
            jax.ShapeDtypeStruct((n_tiles * SUBLANES, LANES), F32),
        ],
        compiler_params=_cparams(),
        name="outproj_router",
    )(attn, y, xf, wo_attn, wo_pool, norm_ffn[0].reshape(1, d), w_r2, b_r, tri)

    assert tm == LOCAL_TILE
    gates = route[0:TOP_K].reshape(TOP_K * n)
    lp = route[TOP_K:2 * TOP_K].astype(jnp.int32).reshape(TOP_K * n)
    tile_cnt = cnt.reshape(n_tiles, SUBLANES, LANES)[:, 0, :N_EXPERTS].astype(jnp.int32)
    te = EXPERT_TILE
    gch, n_valid, gaps, blk_eid, first, n_act, n_rows = _plan(tile_cnt, n)
    n_blocks = n_rows // te
    n_local = n // LOCAL_TILE

    gch_rows = gch * (CHUNK * PK_ROWS)
    xs = pl.pallas_call(
        _local_sort_kernel,
        grid_spec=pltpu.PrefetchScalarGridSpec(
            num_scalar_prefetch=4,
            grid=(n_local,),
            in_specs=[pl.BlockSpec((LOCAL_TILE * PK_ROWS, LANES), lambda i, *_: (i, 0))],
            out_specs=pl.BlockSpec(memory_space=pl.ANY),
            scratch_shapes=[
                pltpu.VMEM((2, LT_MAX * PK_ROWS, LANES), jnp.int32),
                pltpu.VMEM((te * PK_ROWS, LANES), jnp.int32),
                pltpu.SemaphoreType.DMA((2,)),
                pltpu.SemaphoreType.DMA((1,)),
            ],
        ),
        out_shape=jax.ShapeDtypeStruct((n_rows * PK_ROWS, LANES), jnp.int32),
        compiler_params=_cparams(),
        name="local_sort",
    )(lp * PK_ROWS, gch_rows, n_valid, gaps, hpk)

    w_idx = lambda i, eid_r, *_: (eid_r[i], 0, 0)
    x_idx = lambda i, eid_r, first_r, nact_r: (jnp.minimum(i, nact_r[0] - 1), 0)
    ys = pl.pallas_call(
        _expert_kernel,
        grid_spec=pltpu.PrefetchScalarGridSpec(
            num_scalar_prefetch=3,
            grid=(n_blocks,),
            in_specs=[
                pl.BlockSpec((te * PK_ROWS, LANES), x_idx),
                pl.BlockSpec((1, d, D_EXPERT), w_idx),
                pl.BlockSpec((1, d, D_EXPERT), w_idx),
                pl.BlockSpec((1, D_EXPERT, d), w_idx),
            ],
            out_specs=pl.BlockSpec((te * PK_ROWS, LANES), lambda i, *_: (i, 0)),
            scratch_shapes=[
                pltpu.VMEM((d, D_EXPERT), BF16),
                pltpu.VMEM((d, D_EXPERT), BF16),
                pltpu.VMEM((D_EXPERT, d), BF16),
            ],
        ),
        out_shape=jax.ShapeDtypeStruct((n_rows * PK_ROWS, LANES), jnp.int32),
        compiler_params=_cparams(vmem_limit=VMEM_LIMIT_EXPERTS),
        name="experts",
    )(blk_eid, first, n_act, xs, w_gate[0], w_up[0], w_down[0])

    tt = LOCAL_TILE
    out = pl.pallas_call(
        _combine_kernel,
        grid_spec=pltpu.PrefetchScalarGridSpec(
            num_scalar_prefetch=3,
            grid=(n_local,),
            in_specs=[
                pl.BlockSpec((tt, d), lambda i, *_: (i, 0)),
                pl.BlockSpec(memory_space=pl.ANY),
            ],
            out_specs=pl.BlockSpec((tt, d), lambda i, *_: (i, 0)),
            scratch_shapes=[
                pltpu.VMEM((2, LT_MAX * PK_ROWS, LANES), jnp.int32),
                pltpu.VMEM((2, tt * PK_ROWS, LANES), F32),
                pltpu.SemaphoreType.DMA((2,)),
            ],
        ),
        out_shape=jax.ShapeDtypeStruct((n, d), F32),
        compiler_params=_cparams(),
        name="combine",
    )(gch_rows, lp * PK_ROWS, gates, h, ys)
    return out.reshape(bsz, seq, d)
```

```python
import functools

import jax
import jax.numpy as jnp
from jax import lax
from jax.experimental import pallas as pl
from jax.experimental.pallas import tpu as pltpu

F32 = jnp.float32
BF16 = jnp.bfloat16

D_MODEL = 2048
N_HEADS = 16
N_KV_HEADS = 4
HEAD_DIM = 64
GROUP = N_HEADS // N_KV_HEADS
ROT_DIM = HEAD_DIM // 4
ROPE_THETA = 500000.0
WINDOW = 128
BLOCK = 128
Q_BLOCKS = 2
ATTN_WIDTH = N_HEADS * HEAD_DIM
KV_WIDTH = N_KV_HEADS * HEAD_DIM
POOL_WINDOWS = (2, 4, 8, 16)
POOL_WIDTH = D_MODEL // 2
POOL_GROUP_DIM = POOL_WIDTH // len(POOL_WINDOWS)
N_EXPERT_GROUPS = 4
EXPERTS_PER_GROUP = 8
N_EXPERTS = N_EXPERT_GROUPS * EXPERTS_PER_GROUP
TOP_K = 2
D_EXPERT = 512
EPS = 1e-6

LANES = 128
SUBLANES = 8
MXU_DIM = 256
HALO = 16
NEG_BIG = -1e30
HALF = D_MODEL // 2
PK_ROWS = HALF // LANES
Y_ROWS = D_MODEL // LANES

ROW_TILE = 512
SUB_TILE = 512
EXPERT_TILE = 512
LOCAL_TILE = 512
CHUNK = 8
LT_MAX = TOP_K * LOCAL_TILE + N_EXPERTS * (CHUNK - 1)
VMEM_LIMIT = 56 * 1024 * 1024


def _cparams(n_axes=1):
    return pltpu.CompilerParams(
        dimension_semantics=("arbitrary",) * n_axes,
        vmem_limit_bytes=VMEM_LIMIT,
    )


def _pack_halves(t):
    return pltpu.pack_elementwise([t[:, :HALF], t[:, HALF:]], packed_dtype=BF16)


def _unpack_halves(w):
    return tuple(pltpu.unpack_elementwise(w, index=i, packed_dtype=BF16, unpacked_dtype=F32)
                 for i in range(2))


def _head_sumsq(t, e_ref):
    t2 = (t * t).astype(BF16)
    e = e_ref[...]
    outs = []
    for c in range(t.shape[1] // MXU_DIM):
        sl = slice(c * MXU_DIM, (c + 1) * MXU_DIM)
        outs.append(jnp.dot(t2[:, sl], e, preferred_element_type=F32))
    return outs[0] if len(outs) == 1 else jnp.concatenate(outs, axis=1)


def _norm_rope(t, gain, e_ref, cc, s_up, s_dn, out_scale):
    w = t.shape[1]
    ss = _head_sumsq(t, e_ref)
    tn = t * lax.rsqrt(ss * (1.0 / HEAD_DIM) + EPS) * gain
    reps = w // LANES
    up = pltpu.roll(tn, w - ROT_DIM // 2, axis=1)
    dn = pltpu.roll(tn, ROT_DIM // 2, axis=1)
    c = jnp.concatenate([cc] * reps, axis=1)
    su = jnp.concatenate([s_up] * reps, axis=1)
    sd = jnp.concatenate([s_dn] * reps, axis=1)
    return (tn * c + up * su + dn * sd) * out_scale


def _swap_head_pairs(t):
    w = t.shape[1]
    lane = lax.broadcasted_iota(jnp.int32, t.shape, 1)
    from_up = pltpu.roll(t, w - HEAD_DIM, axis=1)
    from_dn = pltpu.roll(t, HEAD_DIM, axis=1)
    return jnp.where((lane & (LANES - 1)) < HEAD_DIM, from_up, from_dn)


def _inproj_kernel(x_ref, g_ref, wq_ref, wk_ref, wv_ref, wu_ref, qn_ref, kn_ref,
                   cc_ref, su_ref, sd_ref, e_ref, wpool_ref, pscale_ref,
                   q_out, k_out, v_out, ksw_out, vsw_out, y_out, ubuf, *, tiles_per_seq):
    i = pl.program_id(0)
    tm = x_ref.shape[0]

    @pl.when(i % tiles_per_seq == 0)
    def _():
        ubuf[0:HALO, :] = jnp.zeros((HALO, POOL_WIDTH), F32)

    for r0 in range(0, tm, SUB_TILE):
        rows = slice(r0, r0 + SUB_TILE)
        x = x_ref[rows, :]
        ms = jnp.mean(x * x, axis=-1, keepdims=True)
        hn = (x * lax.rsqrt(ms + EPS) * g_ref[...]).astype(BF16)

        cc, s_up, s_dn = cc_ref[rows, :], su_ref[rows, :], sd_ref[rows, :]
        q = jnp.dot(hn, wq_ref[...], preferred_element_type=F32)
        q_out[rows, :] = _norm_rope(q, qn_ref[...], e_ref, cc, s_up, s_dn,
                                    HEAD_DIM ** -0.5).astype(BF16)
        k = jnp.dot(hn, wk_ref[...], preferred_element_type=F32)
        k = _norm_rope(k, kn_ref[...], e_ref, cc, s_up, s_dn, 1.0)
        v = jnp.dot(hn, wv_ref[...], preferred_element_type=F32)
        k_out[rows, :] = k.astype(BF16)
        v_out[rows, :] = v.astype(BF16)
        ksw_out[rows, :] = _swap_head_pairs(k).astype(BF16)
        vsw_out[rows, :] = _swap_head_pairs(v).astype(BF16)

        base = HALO + r0
        ubuf[base:base + SUB_TILE, :] = jnp.dot(hn, wu_ref[...], preferred_element_type=F32)
        pos = (i % tiles_per_seq) * tm + r0 + lax.broadcasted_iota(
            jnp.int32, (SUB_TILE, POOL_GROUP_DIM), 0)
        for g, w in enumerate(POOL_WINDOWS):
            cols = slice(g * POOL_GROUP_DIM, (g + 1) * POOL_GROUP_DIM)
            assert w & (w - 1) == 0 and w - 1 < HALO
            acc = ubuf[base - HALO:base + SUB_TILE, cols]
            shift = 1
            while shift < w:
                acc = acc + pltpu.roll(acc, shift, axis=0)
                shift *= 2
            acc = acc[HALO:, :]
            u_g = ubuf[base:base + SUB_TILE, cols]
            cnt = jnp.minimum(pos + 1, w).astype(F32)
            d = (acc / cnt - u_g).astype(BF16)
            y = jnp.dot(d, wpool_ref[g], preferred_element_type=F32)
            y_out[rows, cols] = (y * pscale_ref[:, cols]).astype(BF16)
    ubuf[0:HALO, :] = ubuf[tm:tm + HALO, :]


def _attn_kernel(sinks_ref, q_ref, kc_ref, kp_ref, kswc_ref, kswp_ref,
                 vc_ref, vp_ref, vswc_ref, vswp_ref, o_ref):
    j = pl.program_id(1)
    two = 2 * BLOCK
    row = lax.broadcasted_iota(jnp.int32, (two, two), 0) & (BLOCK - 1)
    col = lax.broadcasted_iota(jnp.int32, (two, two), 1)
    diff = row + BLOCK - col
    local = (diff >= 0) & (diff < WINDOW)
    lo_lanes = lax.broadcasted_iota(jnp.int32, (two, LANES), 1) < HEAD_DIM
    top_rows = lax.broadcasted_iota(jnp.int32, (two, 1), 0) < BLOCK
    zero = jnp.zeros((two, LANES), BF16)

    for sb in range(q_ref.shape[0] // BLOCK):
        rows = slice(sb * BLOCK, (sb + 1) * BLOCK)
        if sb == 0:
            def band(cur_ref, prev_ref):
                return jnp.concatenate([prev_ref[...], cur_ref[0:BLOCK, :]], axis=0)
            valid = local & ((col >= BLOCK) | (j > 0))
        else:
            def band(cur_ref, prev_ref, sb=sb):
                return cur_ref[(sb - 1) * BLOCK:(sb + 1) * BLOCK, :]
            valid = local
        k_nat, k_swp = band(kc_ref, kp_ref), band(kswc_ref, kswp_ref)
        v_nat, v_swp = band(vc_ref, vp_ref), band(vswc_ref, vswp_ref)

        for g in range(N_KV_HEADS):
            c0 = (g // 2) * LANES
            kcols = (k_nat[:, c0:c0 + LANES], k_swp[:, c0:c0 + LANES])
            vcols = (v_nat[:, c0:c0 + LANES], v_swp[:, c0:c0 + LANES])
            in_lo, in_hi = (0, 1) if g % 2 == 0 else (1, 0)
            q0 = g * GROUP * HEAD_DIM
            qq = jnp.concatenate([q_ref[rows, q0:q0 + LANES],
                                  q_ref[rows, q0 + LANES:q0 + 2 * LANES]], axis=0)
            halves = []
            for half, src in ((0, in_lo), (1, in_hi)):
                qm = jnp.where(lo_lanes, qq, zero) if half == 0 else jnp.where(lo_lanes, zero, qq)
                s = lax.dot_general(qm, kcols[src], (((1,), (1,)), ((), ())),
                                    preferred_element_type=F32)
                s = jnp.where(valid, s, NEG_BIG)
                h_top = g * GROUP + half
                sink = jnp.where(top_rows, sinks_ref[h_top], sinks_ref[h_top + 2])
                m = jnp.maximum(jnp.max(s, axis=-1, keepdims=True), sink)
                p = jnp.exp(s - m)
                denom = jnp.sum(p, axis=-1, keepdims=True) + jnp.exp(sink - m)
                o = jnp.dot(p.astype(BF16), vcols[src], preferred_element_type=F32)
                halves.append(o / denom)
            o_pair = jnp.where(lo_lanes, halves[0], halves[1]).astype(BF16)
            o_ref[rows, q0:q0 + LANES] = o_pair[0:BLOCK, :]
            o_ref[rows, q0 + LANES:q0 + 2 * LANES] = o_pair[BLOCK:, :]


def _outproj_router_kernel(o_ref, y_ref, x_ref, woa_ref, wop_ref, g_ref, wr_ref,
                           br_ref, tri_ref, h_out, hpk_out, route_out, cnt_out):
    n_sub = x_ref.shape[0] // SUB_TILE
    picks = [_outproj_router_subtile(sub, o_ref, y_ref, x_ref, woa_ref, wop_ref, g_ref,
                                     wr_ref, br_ref, tri_ref, h_out, hpk_out)
             for sub in range(n_sub)]

    lane = lax.broadcasted_iota(jnp.int32, (SUB_TILE, LANES), 1)
    lane_f = lane.astype(F32)
    cnt_tile = sum(pk[5] for pk in picks)
    cnt_pad = jnp.floor((cnt_tile + (CHUNK - 1)) * (1.0 / CHUNK)) * CHUNK
    run_start = lax.dot_general(
        jnp.broadcast_to(cnt_pad, (SUBLANES, LANES)).astype(BF16),
        tri_ref[0:LANES, 0:LANES], (((1,), (1,)), ((), ())),
        preferred_element_type=F32)[0:1, :]
    before = jnp.zeros((1, LANES), F32)
    for sub, (e1, e2, g1, g2, prefix, cnt_sub) in enumerate(picks):
        table = prefix + (run_start + before)
        lp1 = jnp.sum(jnp.where(lane_f == e1, table, 0.0), axis=-1, keepdims=True)
        lp2 = jnp.sum(jnp.where(lane_f == e2, table, 0.0), axis=-1, keepdims=True)
        cols = jnp.where(lane == 0, g1,
                         jnp.where(lane == 1, g2,
                                   jnp.where(lane == 2, lp1,
                                             jnp.where(lane == 3, lp2, 0.0))))
        pick = (lax.broadcasted_iota(jnp.int32, (SUBLANES, LANES), 0)
                == lax.broadcasted_iota(jnp.int32, (SUBLANES, LANES), 1)).astype(F32)
        route_out[:, sub * SUB_TILE:(sub + 1) * SUB_TILE] = lax.dot_general(
            pick, cols, (((1,), (1,)), ((), ())), precision=lax.Precision.HIGHEST,
            preferred_element_type=F32)
        before = before + cnt_sub
    cnt_out[...] = jnp.broadcast_to(cnt_tile, cnt_out.shape)


def _outproj_router_subtile(sub, o_ref, y_ref, x_ref, woa_ref, wop_ref, g_ref, wr_ref,
                            br_ref, tri_ref, h_out, hpk_out):
    tm = SUB_TILE
    r0 = sub * SUB_TILE
    rows = slice(r0, r0 + SUB_TILE)
    h = (x_ref[rows, :]
         + jnp.dot(o_ref[rows, :], woa_ref[...], preferred_element_type=F32)
         + jnp.dot(y_ref[rows, :], wop_ref[...], preferred_element_type=F32))
    h_out[rows, :] = h
    ms = jnp.mean(h * h, axis=-1, keepdims=True)
    hn = h * lax.rsqrt(ms + EPS) * g_ref[...]
    hb = hn.astype(BF16)
    packed = _pack_halves(hn)
    for s in range(PK_ROWS):
        hpk_out[pl.ds(r0 * PK_ROWS + s, tm, stride=PK_ROWS), :] = (
            packed[:, s * LANES:(s + 1) * LANES])

    both = jnp.dot(hb, wr_ref[...], preferred_element_type=F32)
    logits = both[:, :LANES] + both[:, LANES:] + br_ref[...]
    lane = lax.broadcasted_iota(jnp.int32, (tm, LANES), 1)
    lane_f = lane.astype(F32)

    def first_lane_of(mask):
        return jnp.min(jnp.where(mask, lane_f, float(LANES)), axis=-1, keepdims=True)

    def row_max(t):
        return jnp.max(t, axis=-1, keepdims=True)

    coarse = jnp.where(lane < N_EXPERT_GROUPS, logits, NEG_BIG)
    cmax = row_max(coarse)
    grp = first_lane_of(coarse == cmax)
    csum = jnp.sum(jnp.exp(coarse - cmax), axis=-1, keepdims=True)
    flo = N_EXPERT_GROUPS + EXPERTS_PER_GROUP * grp
    fine = jnp.where((lane_f >= flo) & (lane_f < flo + EXPERTS_PER_GROUP), logits, NEG_BIG)
    f1 = row_max(fine)
    i1 = first_lane_of(fine == f1)
    rest = jnp.where(lane_f == i1, NEG_BIG, fine)
    f2 = row_max(rest)
    i2 = first_lane_of(rest == f2)
    ratio = jnp.exp(f2 - f1)
    scale = (1.0 / csum) / (1.0 + ratio)
    g1 = scale
    g2 = scale * ratio
    e1 = i1 - N_EXPERT_GROUPS
    e2 = i2 - N_EXPERT_GROUPS

    oh = jnp.where(lane_f == e1, 1.0, 0.0) + jnp.where(lane_f == e2, 1.0, 0.0)
    prefix = jnp.dot(tri_ref[...], oh.astype(BF16), preferred_element_type=F32)
    return e1, e2, g1, g2, prefix, jnp.sum(oh, axis=0, keepdims=True)


def _expert_kernel(eid_ref, first_ref, nact_ref, off_ref, next_ref,
                   hpk_hbm, wg_hbm, wu_hbm, wd_hbm, ys_out,
                   xbuf, wg_f, wu_f, wd_f, wg_s, wu_s, wd_s, sems, wsems):
    i = pl.program_id(0)
    tm = ys_out.shape[0] // PK_ROWS
    n_act = nact_ref[0]

    def weight_copies(e):
        return (pltpu.make_async_copy(wg_hbm.at[e], wg_f, wsems.at[0]),
                pltpu.make_async_copy(wu_hbm.at[e], wu_f, wsems.at[1]),
                pltpu.make_async_copy(wd_hbm.at[e], wd_f, wsems.at[2]))

    ch_rows = CHUNK * PK_ROWS
    n_ch = tm // CHUNK

    def gather(blk, slot):
        for c in range(n_ch):
            src = pl.multiple_of(off_ref[blk * n_ch + c], ch_rows)
            pltpu.make_async_copy(hpk_hbm.at[pl.ds(src, ch_rows)],
                                  xbuf.at[slot, pl.ds(c * ch_rows, ch_rows)],
                                  sems.at[slot]).start()

    @pl.when(i == 0)
    def _():
        for cp in weight_copies(eid_ref[0]):
            cp.start()
        gather(0, 0)

    @pl.when(i + 1 < n_act)
    def _():
        gather(i + 1, (i + 1) % 2)

    @pl.when(i < n_act)
    def _():
        slot = i % 2

        @pl.when(first_ref[i] == 1)
        def _():
            for cp in weight_copies(eid_ref[i]):
                cp.wait()
            wg_s[...] = wg_f[...].astype(BF16)
            wu_s[...] = wu_f[...].astype(BF16)
            wd_s[...] = wd_f[...].astype(BF16)

            @pl.when(next_ref[i] >= 0)
            def _():
                for cp in weight_copies(next_ref[i]):
                    cp.start()

        pltpu.make_async_copy(hpk_hbm.at[pl.ds(0, tm * PK_ROWS)], xbuf.at[slot],
                              sems.at[slot]).wait()
        xu = jnp.concatenate(
            [xbuf[slot, pl.ds(s, tm, stride=PK_ROWS), :] for s in range(PK_ROWS)],
            axis=1)
        x_lo, x_hi = (v.astype(BF16) for v in _unpack_halves(xu))
        a = (jnp.dot(x_lo, wg_s[0:HALF, :], preferred_element_type=F32)
             + jnp.dot(x_hi, wg_s[HALF:, :], preferred_element_type=F32))
        b = (jnp.dot(x_lo, wu_s[0:HALF, :], preferred_element_type=F32)
             + jnp.dot(x_hi, wu_s[HALF:, :], preferred_element_type=F32))
        mid = (a * jax.nn.sigmoid(a) * b).astype(BF16)
        y = jnp.dot(mid, wd_s[...], preferred_element_type=F32)
        yw = _pack_halves(y)
        for s in range(PK_ROWS):
            ys_out[pl.ds(s, tm, stride=PK_ROWS), :] = yw[:, s * LANES:(s + 1) * LANES]

    @pl.when(i >= n_act)
    def _():
        ys_out[...] = jnp.zeros(ys_out.shape, ys_out.dtype)


def _local_sort_kernel(lp_ref, hpk_ref, xs_out):
    i = pl.program_id(0)
    t_tile = hpk_ref.shape[0] // PK_ROWS
    n_tok = pl.num_programs(0) * t_tile
    xs_out[...] = jnp.zeros(xs_out.shape, xs_out.dtype)

    def body(q, carry):
        for u in range(SUBLANES):
            t = q * SUBLANES + u
            row = hpk_ref[pl.ds(pl.multiple_of(t * PK_ROWS, PK_ROWS), PK_ROWS), :]
            for k in range(TOP_K):
                dst = pl.multiple_of(lp_ref[k * n_tok + i * t_tile + t], PK_ROWS)
                xs_out[pl.ds(dst, PK_ROWS), :] = row
        return carry
    lax.fori_loop(0, t_tile // SUBLANES, body, 0)


def _combine_kernel(gch_ref, lp_ref, gate_ref, h_ref, ys_hbm, out_ref, ylocal, acc, sems):
    i = pl.program_id(0)
    n = pl.num_programs(0)
    t_tile = h_ref.shape[0]
    n_tok = n * t_tile
    ch_rows = CHUNK * PK_ROWS
    n_ch = ylocal.shape[1] // ch_rows

    def chunk_copy(tile, slot, lc):
        src = pl.multiple_of(gch_ref[tile * n_ch + lc], ch_rows)
        dst = pl.multiple_of(lc * ch_rows, ch_rows)
        pltpu.make_async_copy(ys_hbm.at[pl.ds(src, ch_rows)],
                              ylocal.at[slot, pl.ds(dst, ch_rows)],
                              sems.at[slot]).start()

    def wait_slot(slot):
        pltpu.make_async_copy(ys_hbm.at[pl.ds(0, n_ch * ch_rows)], ylocal.at[slot],
                              sems.at[slot]).wait()

    @pl.when(i == 0)
    def _():
        def body(lc, carry):
            chunk_copy(0, 0, lc)
            return carry
        lax.fori_loop(0, n_ch, body, 0)

    slot = i % 2
    wait_slot(slot)

    nxt = jnp.minimum(i + 1, n - 1)
    tok_unroll = 4
    n_groups = t_tile // tok_unroll
    per_group = -(-n_ch // n_groups)
    issue_groups = n_ch // per_group
    assert issue_groups * per_group == n_ch and issue_groups <= n_groups

    def tok_body(q, carry, issue):
        if issue:
            for u in range(per_group):
                chunk_copy(nxt, 1 - slot, q * per_group + u)
        for u in range(tok_unroll):
            t = q * tok_unroll + u
            a0 = i * t_tile + t
            a1 = n_tok + a0
            p0 = pl.multiple_of(lp_ref[a0], PK_ROWS)
            p1 = pl.multiple_of(lp_ref[a1], PK_ROWS)
            lo0, hi0 = _unpack_halves(ylocal[slot, pl.ds(p0, PK_ROWS), :])
            lo1, hi1 = _unpack_halves(ylocal[slot, pl.ds(p1, PK_ROWS), :])
            g0 = gate_ref[a0]
            g1 = gate_ref[a1]
            dst = pl.multiple_of(t * PK_ROWS, PK_ROWS)
            acc[0, pl.ds(dst, PK_ROWS), :] = g0 * lo0 + g1 * lo1
            acc[1, pl.ds(dst, PK_ROWS), :] = g0 * hi0 + g1 * hi1
        return carry
    lax.fori_loop(0, issue_groups, functools.partial(tok_body, issue=True), 0)
    lax.fori_loop(issue_groups, n_groups, functools.partial(tok_body, issue=False), 0)

    @pl.when(i == n - 1)
    def _():
        wait_slot(1 - slot)

    for half in range(2):
        for s in range(PK_ROWS):
            c0 = half * HALF + s * LANES
            out_ref[:, c0:c0 + LANES] = (h_ref[:, c0:c0 + LANES]
                                         + acc[half, pl.ds(s, t_tile, stride=PK_ROWS), :])


def _rope_tables(seq):
    pos = jnp.arange(seq, dtype=F32)
    inv_freq = ROPE_THETA ** (-jnp.arange(0, ROT_DIM, 2, dtype=F32) / ROT_DIM)
    ang = pos[:, None] * inv_freq[None, :]
    cos, sin = jnp.cos(ang), jnp.sin(ang)
    half = ROT_DIM // 2
    ones = jnp.ones((seq, HEAD_DIM - ROT_DIM), F32)
    zeros = jnp.zeros((seq, HEAD_DIM - ROT_DIM), F32)
    zh = jnp.zeros((seq, half), F32)
    cc = jnp.concatenate([cos, cos, ones], axis=1)
    s_up = jnp.concatenate([-sin, zh, zeros], axis=1)
    s_dn = jnp.concatenate([zh, sin, zeros], axis=1)
    rep = LANES // HEAD_DIM
    return tuple(jnp.tile(t, (1, rep)) for t in (cc, s_up, s_dn))


def _plan(cnt, n_tokens):
    nt = n_tokens // LOCAL_TILE
    experts = jnp.arange(N_EXPERTS, dtype=jnp.int32)
    cntp = ((cnt + CHUNK - 1) // CHUNK) * CHUNK
    loff_end = jnp.cumsum(cntp, axis=1)
    loff = loff_end - cntp
    rows_e = jnp.sum(cntp, axis=0)
    rows_pad = ((rows_e + EXPERT_TILE - 1) // EXPERT_TILE) * EXPERT_TILE
    g_end = jnp.cumsum(rows_pad)
    g_start = g_end - rows_pad
    gpos = g_start[None, :] + jnp.cumsum(cntp, axis=0) - cntp

    lrow = jnp.arange(LT_MAX // CHUNK, dtype=jnp.int32) * CHUNK
    e_lc = jnp.sum((lrow[None, :, None] >= loff_end[:, None, :]).astype(jnp.int32), axis=-1)
    sel = e_lc[:, :, None] == experts
    delta = jnp.sum(jnp.where(sel, (gpos - loff)[:, None, :], 0), axis=-1)
    gch = jnp.where(e_lc < N_EXPERTS, (delta + lrow[None, :]) // CHUNK, 0).reshape(-1)

    n_rows = TOP_K * n_tokens + nt * N_EXPERTS * (CHUNK - 1) + N_EXPERTS * (EXPERT_TILE - CHUNK)
    n_blocks = -(-n_rows // EXPERT_TILE)
    n_rows = n_blocks * EXPERT_TILE
    blk_start = jnp.arange(n_blocks, dtype=jnp.int32) * EXPERT_TILE
    blk_eid = jnp.minimum(
        jnp.sum((blk_start[:, None] >= g_end[None, :]).astype(jnp.int32), axis=-1),
        N_EXPERTS - 1)
    prev = jnp.concatenate([jnp.full((1,), -1, jnp.int32), blk_eid[:-1]])
    first = (blk_eid != prev).astype(jnp.int32)
    n_act = (g_end[-1] // EXPERT_TILE).astype(jnp.int32).reshape(1)
    later = (blk_eid[None, :] > blk_eid[:, None]) & (jnp.arange(n_blocks)[None, :] < n_act)
    next_eid = jnp.min(jnp.where(later, blk_eid[None, :], N_EXPERTS), axis=1)
    next_eid = jnp.where(next_eid < N_EXPERTS, next_eid, -1).astype(jnp.int32)

    cpb = EXPERT_TILE // CHUNK
    run_start = gpos.T // CHUNK
    run_len = cntp.T // CHUNK
    run_src = (jnp.arange(nt, dtype=jnp.int32)[:, None] * (LT_MAX // CHUNK) + loff // CHUNK).T
    tabs = jnp.stack([run_start, run_len, run_src], axis=0)
    dtabs = tabs - jnp.concatenate(
        [jnp.zeros((3, N_EXPERTS, 1), jnp.int32), tabs[:, :, :-1]], axis=2)
    blk_sel = (blk_eid[:, None] == experts)[:, :, None]
    blk_start_t = jnp.sum(jnp.where(blk_sel, run_start[None], 0), axis=1)
    blk_dtabs = jnp.sum(jnp.where(blk_sel[None], dtabs[:, None], 0), axis=2)
    gc = jnp.arange(n_rows // CHUNK, dtype=jnp.int32).reshape(n_blocks, cpb)
    started = blk_start_t[:, None, :] <= gc[:, :, None]
    picked = jnp.sum(jnp.where(started[None], blk_dtabs[:, :, None, :], 0), axis=-1)
    within = gc - picked[0]
    csrc = jnp.where(within < picked[1], picked[2] + within, 0).reshape(-1)
    return gch.astype(jnp.int32), csrc.astype(jnp.int32), blk_eid, first, next_eid, n_act, n_rows


def kernel(x, norm_mix, w_in, q_norm, k_norm, sinks, w_pool, pool_scale, w_out,
           norm_ffn, w_coarse, b_coarse, w_fine, b_fine, w_gate, w_up, w_down):
    bsz, seq, d = x.shape
    n = bsz * seq
    assert d == D_MODEL and seq % ROW_TILE == 0 and seq % BLOCK == 0
    assert norm_mix.shape[0] == 1, "single-layer problem"
    xf = x.reshape(n, d)

    w_in_b = w_in[0].astype(BF16)
    wq = w_in_b[:, :ATTN_WIDTH]
    wk = w_in_b[:, ATTN_WIDTH:ATTN_WIDTH + KV_WIDTH]
    wv = w_in_b[:, ATTN_WIDTH + KV_WIDTH:ATTN_WIDTH + 2 * KV_WIDTH]
    wu = w_in_b[:, ATTN_WIDTH + 2 * KV_WIDTH:]
    qn = jnp.tile(q_norm[0], N_HEADS).reshape(1, ATTN_WIDTH)
    kn = jnp.tile(k_norm[0], N_KV_HEADS).reshape(1, KV_WIDTH)
    cc, s_up, s_dn = _rope_tables(seq)
    lane_head = jnp.arange(MXU_DIM) // HEAD_DIM
    e_mat = (lane_head[:, None] == lane_head[None, :]).astype(BF16)
    w_pool_b = w_pool[0].astype(BF16)
    pscale = pool_scale[0].reshape(1, POOL_WIDTH)
    w_out_b = w_out[0].astype(BF16)
    wo_attn, wo_pool = w_out_b[:ATTN_WIDTH], w_out_b[ATTN_WIDTH:]
    n_router = N_EXPERT_GROUPS + N_EXPERTS
    w_r = jnp.concatenate([w_coarse[0], w_fine[0]], axis=1)
    w_r = jnp.pad(w_r, ((0, 0), (0, LANES - n_router)))
    w_r_hi = w_r.astype(BF16)
    w_r_lo = (w_r - w_r_hi.astype(F32)).astype(BF16)
    w_r2 = jnp.concatenate([w_r_hi, w_r_lo], axis=1)
    b_r = jnp.pad(jnp.concatenate([b_coarse[0], b_fine[0]]),
                  (0, LANES - n_router)).reshape(1, LANES)

    tm = ROW_TILE
    tiles_per_seq = seq // tm
    n_tiles = n // tm
    idx = jnp.arange(SUB_TILE)
    tri = (idx[None, :] < idx[:, None]).astype(BF16)
    const = lambda *_: (0, 0)
    row_blk = lambda i: (i, 0)

    q, k, v, ksw, vsw, y = pl.pallas_call(
        functools.partial(_inproj_kernel, tiles_per_seq=tiles_per_seq),
        grid=(n_tiles,),
        in_specs=[
            pl.BlockSpec((tm, d), row_blk),
            pl.BlockSpec((1, d), const),
            pl.BlockSpec((d, ATTN_WIDTH), const),
            pl.BlockSpec((d, KV_WIDTH), const),
            pl.BlockSpec((d, KV_WIDTH), const),
            pl.BlockSpec((d, POOL_WIDTH), const),
            pl.BlockSpec((1, ATTN_WIDTH), const),
            pl.BlockSpec((1, KV_WIDTH), const),
            pl.BlockSpec((tm, LANES), lambda i: (i % tiles_per_seq, 0)),
            pl.BlockSpec((tm, LANES), lambda i: (i % tiles_per_seq, 0)),
            pl.BlockSpec((tm, LANES), lambda i: (i % tiles_per_seq, 0)),
            pl.BlockSpec((MXU_DIM, MXU_DIM), const),
            pl.BlockSpec((len(POOL_WINDOWS), POOL_GROUP_DIM, POOL_GROUP_DIM),
                         lambda i: (0, 0, 0)),
            pl.BlockSpec((1, POOL_WIDTH), const),
        ],
        out_specs=[
            pl.BlockSpec((tm, ATTN_WIDTH), row_blk),
            pl.BlockSpec((tm, KV_WIDTH), row_blk),
            pl.BlockSpec((tm, KV_WIDTH), row_blk),
            pl.BlockSpec((tm, KV_WIDTH), row_blk),
            pl.BlockSpec((tm, KV_WIDTH), row_blk),
            pl.BlockSpec((tm, POOL_WIDTH), row_blk),
        ],
        out_shape=[
            jax.ShapeDtypeStruct((n, ATTN_WIDTH), BF16),
            jax.ShapeDtypeStruct((n, KV_WIDTH), BF16),
            jax.ShapeDtypeStruct((n, KV_WIDTH), BF16),
            jax.ShapeDtypeStruct((n, KV_WIDTH), BF16),
            jax.ShapeDtypeStruct((n, KV_WIDTH), BF16),
            jax.ShapeDtypeStruct((n, POOL_WIDTH), BF16),
        ],
        scratch_shapes=[pltpu.VMEM((HALO + tm, POOL_WIDTH), F32)],
        compiler_params=_cparams(),
        name="inproj",
    )(xf, norm_mix[0].reshape(1, d), wq, wk, wv, wu, qn, kn, cc, s_up, s_dn,
      e_mat, w_pool_b, pscale)

    nb = seq // BLOCK
    nq = nb // Q_BLOCKS
    qrows = Q_BLOCKS * BLOCK
    cur = lambda b, j, *_: (b * nq + j, 0)
    prv = lambda b, j, *_: (b * nb + jnp.maximum(Q_BLOCKS * j - 1, 0), 0)
    kv_specs = [pl.BlockSpec((qrows, KV_WIDTH), cur), pl.BlockSpec((BLOCK, KV_WIDTH), prv)]
    attn = pl.pallas_call(
        _attn_kernel,
        grid_spec=pltpu.PrefetchScalarGridSpec(
            num_scalar_prefetch=1,
            grid=(bsz, nq),
            in_specs=[pl.BlockSpec((qrows, ATTN_WIDTH), cur)] + kv_specs * 4,
            out_specs=pl.BlockSpec((qrows, ATTN_WIDTH), cur),
        ),
        out_shape=jax.ShapeDtypeStruct((n, ATTN_WIDTH), BF16),
        compiler_params=_cparams(2),
        name="swa_attn",
    )(sinks[0].astype(F32), q, k, k, ksw, ksw, v, v, vsw, vsw)

    h, hpk, route, cnt = pl.pallas_call(
        _outproj_router_kernel,
        grid=(n_tiles,),
        in_specs=[
            pl.BlockSpec((tm, ATTN_WIDTH), row_blk),
            pl.BlockSpec((tm, POOL_WIDTH), row_blk),
            pl.BlockSpec((tm, d), row_blk),
            pl.BlockSpec((ATTN_WIDTH, d), const),
            pl.BlockSpec((POOL_WIDTH, d), const),
            pl.BlockSpec((1, d), const),
            pl.BlockSpec((d, 2 * LANES), const),
            pl.BlockSpec((1, LANES), const),
            pl.BlockSpec((SUB_TILE, SUB_TILE), const),
        ],
        out_specs=[
            pl.BlockSpec((tm, d), row_blk),
            pl.BlockSpec((tm * PK_ROWS, LANES), row_blk),
            pl.BlockSpec((SUBLANES, tm), lambda i: (0, i)),
            pl.BlockSpec((SUBLANES, LANES), row_blk),
        ],
        out_shape=[
            jax.ShapeDtypeStruct((n, d), F32),
            jax.ShapeDtypeStruct((n * PK_ROWS, LANES), jnp.int32),
            jax.ShapeDtypeStruct((SUBLANES, n), F32),
            jax.ShapeDtypeStruct((n_tiles * SUBLANES, LANES), F32),
        ],
        compiler_params=_cparams(),
        name="outproj_router",
    )(attn, y, xf, wo_attn, wo_pool, norm_ffn[0].reshape(1, d), w_r2, b_r, tri)

    assert tm == LOCAL_TILE
    gates = route[0:TOP_K].reshape(TOP_K * n)
    lp = route[TOP_K:2 * TOP_K].astype(jnp.int32).reshape(TOP_K * n)
    tile_cnt = cnt.reshape(n_tiles, SUBLANES, LANES)[:, 0, :N_EXPERTS].astype(jnp.int32)
    te = EXPERT_TILE
    gch, csrc, blk_eid, first, next_eid, n_act, n_rows = _plan(tile_cnt, n)
    n_blocks = n_rows // te
    n_local = n // LOCAL_TILE

    xs = pl.pallas_call(
        _local_sort_kernel,
        grid_spec=pltpu.PrefetchScalarGridSpec(
            num_scalar_prefetch=1,
            grid=(n_local,),
            in_specs=[pl.BlockSpec((LOCAL_TILE * PK_ROWS, LANES), lambda i, *_: (i, 0))],
            out_specs=pl.BlockSpec((LT_MAX * PK_ROWS, LANES), lambda i, *_: (i, 0)),
        ),
        out_shape=jax.ShapeDtypeStruct((n_local * LT_MAX * PK_ROWS, LANES), jnp.int32),
        compiler_params=_cparams(),
        name="local_sort",
    )(lp * PK_ROWS, hpk)

    ys = pl.pallas_call(
        _expert_kernel,
        grid_spec=pltpu.PrefetchScalarGridSpec(
            num_scalar_prefetch=5,
            grid=(n_blocks,),
            in_specs=[pl.BlockSpec(memory_space=pl.ANY)] * 4,
            out_specs=pl.BlockSpec((te * PK_ROWS, LANES), lambda i, *_: (i, 0)),
            scratch_shapes=[
                pltpu.VMEM((2, te * PK_ROWS, LANES), jnp.int32),
                pltpu.VMEM((d, D_EXPERT), F32),
                pltpu.VMEM((d, D_EXPERT), F32),
                pltpu.VMEM((D_EXPERT, d), F32),
                pltpu.VMEM((d, D_EXPERT), BF16),
                pltpu.VMEM((d, D_EXPERT), BF16),
                pltpu.VMEM((D_EXPERT, d), BF16),
                pltpu.SemaphoreType.DMA((2,)),
                pltpu.SemaphoreType.DMA((3,)),
            ],
        ),
        out_shape=jax.ShapeDtypeStruct((n_rows * PK_ROWS, LANES), jnp.int32),
        compiler_params=_cparams(),
        name="experts",
    )(blk_eid, first, n_act, csrc * (CHUNK * PK_ROWS), next_eid, xs,
      w_gate[0], w_up[0], w_down[0])

    tt = LOCAL_TILE
    out = pl.pallas_call(
        _combine_kernel,
        grid_spec=pltpu.PrefetchScalarGridSpec(
            num_scalar_prefetch=3,
            grid=(n_local,),
            in_specs=[
                pl.BlockSpec((tt, d), lambda i, *_: (i, 0)),
                pl.BlockSpec(memory_space=pl.ANY),
            ],
            out_specs=pl.BlockSpec((tt, d), lambda i, *_: (i, 0)),
            scratch_shapes=[
                pltpu.VMEM((2, LT_MAX * PK_ROWS, LANES), jnp.int32),
                pltpu.VMEM((2, tt * PK_ROWS, LANES), F32),
                pltpu.SemaphoreType.DMA((2,)),
            ],
        ),
        out_shape=jax.ShapeDtypeStruct((n, d), F32),
        compiler_params=_cparams(),
        name="combine",
    )(gch * (CHUNK * PK_ROWS), lp * PK_ROWS, gates, h, ys)
    return out.reshape(bsz, seq, d)
```

```python
import functools

import jax
import jax.numpy as jnp
from jax import lax
from jax.experimental import pallas as pl
from jax.experimental.pallas import tpu as pltpu

F32 = jnp.float32
BF16 = jnp.bfloat16

D_MODEL = 2048
N_HEADS = 16
N_KV_HEADS = 4
HEAD_DIM = 64
GROUP = N_HEADS // N_KV_HEADS
ROT_DIM = HEAD_DIM // 4
ROPE_THETA = 500000.0
WINDOW = 128
BLOCK = 128
Q_BLOCKS = 2
ATTN_WIDTH = N_HEADS * HEAD_DIM
KV_WIDTH = N_KV_HEADS * HEAD_DIM
POOL_WINDOWS = (2, 4, 8, 16)
POOL_WIDTH = D_MODEL // 2
POOL_GROUP_DIM = POOL_WIDTH // len(POOL_WINDOWS)
N_EXPERT_GROUPS = 4
EXPERTS_PER_GROUP = 8
N_EXPERTS = N_EXPERT_GROUPS * EXPERTS_PER_GROUP
TOP_K = 2
D_EXPERT = 512
EPS = 1e-6

LANES = 128
SUBLANES = 8
MXU_DIM = 256
HALO = 16
NEG_BIG = -1e30
HALF = D_MODEL // 2
PK_ROWS = HALF // LANES
Y_ROWS = D_MODEL // LANES

ROW_TILE = 512
SUB_TILE = 512
EXPERT_TILE = 512
LOCAL_TILE = 512
CHUNK = 8
LT_MAX = TOP_K * LOCAL_TILE + N_EXPERTS * (CHUNK - 1)
VMEM_LIMIT = 56 * 1024 * 1024


def _cparams(n_axes=1):
    return pltpu.CompilerParams(
        dimension_semantics=("arbitrary",) * n_axes,
        vmem_limit_bytes=VMEM_LIMIT,
    )


def _pack_halves(t):
    return pltpu.pack_elementwise([t[:, :HALF], t[:, HALF:]], packed_dtype=BF16)


def _unpack_halves(w):
    return tuple(pltpu.unpack_elementwise(w, index=i, packed_dtype=BF16, unpacked_dtype=F32)
                 for i in range(2))


def _head_sumsq(t, e_ref):
    t2 = (t * t).astype(BF16)
    e = e_ref[...]
    outs = []
    for c in range(t.shape[1] // MXU_DIM):
        sl = slice(c * MXU_DIM, (c + 1) * MXU_DIM)
        outs.append(jnp.dot(t2[:, sl], e, preferred_element_type=F32))
    return outs[0] if len(outs) == 1 else jnp.concatenate(outs, axis=1)


def _norm_rope(t, gain, e_ref, cc, s_up, s_dn, out_scale):
    w = t.shape[1]
    ss = _head_sumsq(t, e_ref)
    tn = t * lax.rsqrt(ss * (1.0 / HEAD_DIM) + EPS) * gain
    reps = w // LANES
    up = pltpu.roll(tn, w - ROT_DIM // 2, axis=1)
    dn = pltpu.roll(tn, ROT_DIM // 2, axis=1)
    c = jnp.concatenate([cc] * reps, axis=1)
    su = jnp.concatenate([s_up] * reps, axis=1)
    sd = jnp.concatenate([s_dn] * reps, axis=1)
    return (tn * c + up * su + dn * sd) * out_scale


def _swap_head_pairs(t):
    w = t.shape[1]
    lane = lax.broadcasted_iota(jnp.int32, t.shape, 1)
    from_up = pltpu.roll(t, w - HEAD_DIM, axis=1)
    from_dn = pltpu.roll(t, HEAD_DIM, axis=1)
    return jnp.where((lane & (LANES - 1)) < HEAD_DIM, from_up, from_dn)


def _inproj_kernel(x_ref, g_ref, wq_ref, wk_ref, wv_ref, wu_ref, qn_ref, kn_ref,
                   cc_ref, su_ref, sd_ref, e_ref, wpool_ref, pscale_ref,
                   q_out, k_out, v_out, ksw_out, vsw_out, y_out, ubuf, *, tiles_per_seq):
    i = pl.program_id(0)
    tm = x_ref.shape[0]

    @pl.when(i % tiles_per_seq == 0)
    def _():
        ubuf[0:HALO, :] = jnp.zeros((HALO, POOL_WIDTH), F32)

    for r0 in range(0, tm, SUB_TILE):
        rows = slice(r0, r0 + SUB_TILE)
        x = x_ref[rows, :]
        ms = jnp.mean(x * x, axis=-1, keepdims=True)
        hn = (x * lax.rsqrt(ms + EPS) * g_ref[...]).astype(BF16)

        cc, s_up, s_dn = cc_ref[rows, :], su_ref[rows, :], sd_ref[rows, :]
        q = jnp.dot(hn, wq_ref[...], preferred_element_type=F32)
        q_out[rows, :] = _norm_rope(q, qn_ref[...], e_ref, cc, s_up, s_dn,
                                    HEAD_DIM ** -0.5).astype(BF16)
        k = jnp.dot(hn, wk_ref[...], preferred_element_type=F32)
        k = _norm_rope(k, kn_ref[...], e_ref, cc, s_up, s_dn, 1.0)
        v = jnp.dot(hn, wv_ref[...], preferred_element_type=F32)
        k_out[rows, :] = k.astype(BF16)
        v_out[rows, :] = v.astype(BF16)
        ksw_out[rows, :] = _swap_head_pairs(k).astype(BF16)
        vsw_out[rows, :] = _swap_head_pairs(v).astype(BF16)

        base = HALO + r0
        ubuf[base:base + SUB_TILE, :] = jnp.dot(hn, wu_ref[...], preferred_element_type=F32)
        pos = (i % tiles_per_seq) * tm + r0 + lax.broadcasted_iota(
            jnp.int32, (SUB_TILE, POOL_GROUP_DIM), 0)
        for g, w in enumerate(POOL_WINDOWS):
            cols = slice(g * POOL_GROUP_DIM, (g + 1) * POOL_GROUP_DIM)
            assert w & (w - 1) == 0 and w - 1 < HALO
            acc = ubuf[base - HALO:base + SUB_TILE, cols]
            shift = 1
            while shift < w:
                acc = acc + pltpu.roll(acc, shift, axis=0)
                shift *= 2
            acc = acc[HALO:, :]
            u_g = ubuf[base:base + SUB_TILE, cols]
            cnt = jnp.minimum(pos + 1, w).astype(F32)
            d = (acc / cnt - u_g).astype(BF16)
            y = jnp.dot(d, wpool_ref[g], preferred_element_type=F32)
            y_out[rows, cols] = (y * pscale_ref[:, cols]).astype(BF16)
    ubuf[0:HALO, :] = ubuf[tm:tm + HALO, :]


def _attn_kernel(sinks_ref, q_ref, kc_ref, kp_ref, kswc_ref, kswp_ref,
                 vc_ref, vp_ref, vswc_ref, vswp_ref, o_ref):
    j = pl.program_id(1)
    two = 2 * BLOCK
    row = lax.broadcasted_iota(jnp.int32, (two, two), 0) & (BLOCK - 1)
    col = lax.broadcasted_iota(jnp.int32, (two, two), 1)
    diff = row + BLOCK - col
    local = (diff >= 0) & (diff < WINDOW)
    lo_lanes = lax.broadcasted_iota(jnp.int32, (two, LANES), 1) < HEAD_DIM
    top_rows = lax.broadcasted_iota(jnp.int32, (two, 1), 0) < BLOCK
    zero = jnp.zeros((two, LANES), BF16)

    for sb in range(q_ref.shape[0] // BLOCK):
        rows = slice(sb * BLOCK, (sb + 1) * BLOCK)
        if sb == 0:
            def band(cur_ref, prev_ref):
                return jnp.concatenate([prev_ref[...], cur_ref[0:BLOCK, :]], axis=0)
            valid = local & ((col >= BLOCK) | (j > 0))
        else:
            def band(cur_ref, prev_ref, sb=sb):
                return cur_ref[(sb - 1) * BLOCK:(sb + 1) * BLOCK, :]
            valid = local
        k_nat, k_swp = band(kc_ref, kp_ref), band(kswc_ref, kswp_ref)
        v_nat, v_swp = band(vc_ref, vp_ref), band(vswc_ref, vswp_ref)

        for g in range(N_KV_HEADS):
            c0 = (g // 2) * LANES
            kcols = (k_nat[:, c0:c0 + LANES], k_swp[:, c0:c0 + LANES])
            vcols = (v_nat[:, c0:c0 + LANES], v_swp[:, c0:c0 + LANES])
            in_lo, in_hi = (0, 1) if g % 2 == 0 else (1, 0)
            q0 = g * GROUP * HEAD_DIM
            qq = jnp.concatenate([q_ref[rows, q0:q0 + LANES],
                                  q_ref[rows, q0 + LANES:q0 + 2 * LANES]], axis=0)
            halves = []
            for half, src in ((0, in_lo), (1, in_hi)):
                qm = jnp.where(lo_lanes, qq, zero) if half == 0 else jnp.where(lo_lanes, zero, qq)
                s = lax.dot_general(qm, kcols[src], (((1,), (1,)), ((), ())),
                                    preferred_element_type=F32)
                s = jnp.where(valid, s, NEG_BIG)
                h_top = g * GROUP + half
                sink = jnp.where(top_rows, sinks_ref[h_top], sinks_ref[h_top + 2])
                m = jnp.maximum(jnp.max(s, axis=-1, keepdims=True), sink)
                p = jnp.exp(s - m)
                denom = jnp.sum(p, axis=-1, keepdims=True) + jnp.exp(sink - m)
                o = jnp.dot(p.astype(BF16), vcols[src], preferred_element_type=F32)
                halves.append(o / denom)
            o_pair = jnp.where(lo_lanes, halves[0], halves[1]).astype(BF16)
            o_ref[rows, q0:q0 + LANES] = o_pair[0:BLOCK, :]
            o_ref[rows, q0 + LANES:q0 + 2 * LANES] = o_pair[BLOCK:, :]


def _outproj_router_kernel(o_ref, y_ref, x_ref, woa_ref, wop_ref, g_ref, wr_ref,
                           br_ref, tri_ref, h_out, hpk_out, route_out, cnt_out):
    n_sub = x_ref.shape[0] // SUB_TILE
    picks = [_outproj_router_subtile(sub, o_ref, y_ref, x_ref, woa_ref, wop_ref, g_ref,
                                     wr_ref, br_ref, tri_ref, h_out, hpk_out)
             for sub in range(n_sub)]

    lane = lax.broadcasted_iota(jnp.int32, (SUB_TILE, LANES), 1)
    lane_f = lane.astype(F32)
    cnt_tile = sum(pk[5] for pk in picks)
    cnt_pad = jnp.floor((cnt_tile + (CHUNK - 1)) * (1.0 / CHUNK)) * CHUNK
    run_start = lax.dot_general(
        jnp.broadcast_to(cnt_pad, (SUBLANES, LANES)).astype(BF16),
        tri_ref[0:LANES, 0:LANES], (((1,), (1,)), ((), ())),
        preferred_element_type=F32)[0:1, :]
    before = jnp.zeros((1, LANES), F32)
    for sub, (e1, e2, g1, g2, prefix, cnt_sub) in enumerate(picks):
        table = prefix + (run_start + before)
        lp1 = jnp.sum(jnp.where(lane_f == e1, table, 0.0), axis=-1, keepdims=True)
        lp2 = jnp.sum(jnp.where(lane_f == e2, table, 0.0), axis=-1, keepdims=True)
        cols = jnp.where(lane == 0, g1,
                         jnp.where(lane == 1, g2,
                                   jnp.where(lane == 2, lp1,
                                             jnp.where(lane == 3, lp2, 0.0))))
        pick = (lax.broadcasted_iota(jnp.int32, (SUBLANES, LANES), 0)
                == lax.broadcasted_iota(jnp.int32, (SUBLANES, LANES), 1)).astype(F32)
        route_out[:, sub * SUB_TILE:(sub + 1) * SUB_TILE] = lax.dot_general(
            pick, cols, (((1,), (1,)), ((), ())), precision=lax.Precision.HIGHEST,
            preferred_element_type=F32)
        before = before + cnt_sub
    cnt_out[...] = jnp.broadcast_to(cnt_tile, cnt_out.shape)


def _outproj_router_subtile(sub, o_ref, y_ref, x_ref, woa_ref, wop_ref, g_ref, wr_ref,
                            br_ref, tri_ref, h_out, hpk_out):
    tm = SUB_TILE
    r0 = sub * SUB_TILE
    rows = slice(r0, r0 + SUB_TILE)
    h = (x_ref[rows, :]
         + jnp.dot(o_ref[rows, :], woa_ref[...], preferred_element_type=F32)
         + jnp.dot(y_ref[rows, :], wop_ref[...], preferred_element_type=F32))
    h_out[rows, :] = h
    ms = jnp.mean(h * h, axis=-1, keepdims=True)
    hn = h * lax.rsqrt(ms + EPS) * g_ref[...]
    hb = hn.astype(BF16)
    packed = _pack_halves(hn)
    for s in range(PK_ROWS):
        hpk_out[pl.ds(r0 * PK_ROWS + s, tm, stride=PK_ROWS), :] = (
            packed[:, s * LANES:(s + 1) * LANES])

    both = jnp.dot(hb, wr_ref[...], preferred_element_type=F32)
    logits = both[:, :LANES] + both[:, LANES:] + br_ref[...]
    lane = lax.broadcasted_iota(jnp.int32, (tm, LANES), 1)
    lane_f = lane.astype(F32)

    def first_lane_of(mask):
        return jnp.min(jnp.where(mask, lane_f, float(LANES)), axis=-1, keepdims=True)

    def row_max(t):
        return jnp.max(t, axis=-1, keepdims=True)

    coarse = jnp.where(lane < N_EXPERT_GROUPS, logits, NEG_BIG)
    cmax = row_max(coarse)
    grp = first_lane_of(coarse == cmax)
    csum = jnp.sum(jnp.exp(coarse - cmax), axis=-1, keepdims=True)
    flo = N_EXPERT_GROUPS + EXPERTS_PER_GROUP * grp
    fine = jnp.where((lane_f >= flo) & (lane_f < flo + EXPERTS_PER_GROUP), logits, NEG_BIG)
    f1 = row_max(fine)
    i1 = first_lane_of(fine == f1)
    rest = jnp.where(lane_f == i1, NEG_BIG, fine)
    f2 = row_max(rest)
    i2 = first_lane_of(rest == f2)
    ratio = jnp.exp(f2 - f1)
    scale = (1.0 / csum) / (1.0 + ratio)
    g1 = scale
    g2 = scale * ratio
    e1 = i1 - N_EXPERT_GROUPS
    e2 = i2 - N_EXPERT_GROUPS

    oh = jnp.where(lane_f == e1, 1.0, 0.0) + jnp.where(lane_f == e2, 1.0, 0.0)
    prefix = jnp.dot(tri_ref[...], oh.astype(BF16), preferred_element_type=F32)
    return e1, e2, g1, g2, prefix, jnp.sum(oh, axis=0, keepdims=True)


def _expert_kernel(eid_ref, first_ref, nact_ref, off_ref, next_ref, used_ref,
                   hpk_hbm, wg_hbm, wu_hbm, wd_hbm, ys_out,
                   xbuf, wg_f, wu_f, wd_f, wg_s, wu_s, wd_s, sems, wsems):
    i = pl.program_id(0)
    tm = ys_out.shape[0] // PK_ROWS
    n_act = nact_ref[0]

    def weight_copies(e):
        return (pltpu.make_async_copy(wg_hbm.at[e], wg_f, wsems.at[0]),
                pltpu.make_async_copy(wu_hbm.at[e], wu_f, wsems.at[1]),
                pltpu.make_async_copy(wd_hbm.at[e], wd_f, wsems.at[2]))

    ch_rows = CHUNK * PK_ROWS
    n_ch = tm // CHUNK

    def gather(blk, slot):
        for c in range(n_ch):
            src = pl.multiple_of(off_ref[blk * n_ch + c], ch_rows)
            pltpu.make_async_copy(hpk_hbm.at[pl.ds(src, ch_rows)],
                                  xbuf.at[slot, pl.ds(c * ch_rows, ch_rows)],
                                  sems.at[slot]).start()

    @pl.when(i == 0)
    def _():
        for cp in weight_copies(eid_ref[0]):
            cp.start()
        gather(0, 0)

    @pl.when(i + 1 < n_act)
    def _():
        gather(i + 1, (i + 1) % 2)

    @pl.when(i < n_act)
    def _():
        slot = i % 2

        @pl.when(first_ref[i] == 1)
        def _():
            for cp in weight_copies(eid_ref[i]):
                cp.wait()
            wg_s[...] = wg_f[...].astype(BF16)
            wu_s[...] = wu_f[...].astype(BF16)
            wd_s[...] = wd_f[...].astype(BF16)

            @pl.when(next_ref[i] >= 0)
            def _():
                for cp in weight_copies(next_ref[i]):
                    cp.start()

        pltpu.make_async_copy(hpk_hbm.at[pl.ds(0, tm * PK_ROWS)], xbuf.at[slot],
                              sems.at[slot]).wait()

        def swiglu(rows):
            xu = jnp.concatenate(
                [xbuf[slot, pl.ds(s, rows, stride=PK_ROWS), :] for s in range(PK_ROWS)],
                axis=1)
            x_lo, x_hi = (v.astype(BF16) for v in _unpack_halves(xu))
            a = (jnp.dot(x_lo, wg_s[0:HALF, :], preferred_element_type=F32)
                 + jnp.dot(x_hi, wg_s[HALF:, :], preferred_element_type=F32))
            b = (jnp.dot(x_lo, wu_s[0:HALF, :], preferred_element_type=F32)
                 + jnp.dot(x_hi, wu_s[HALF:, :], preferred_element_type=F32))
            mid = (a * jax.nn.sigmoid(a) * b).astype(BF16)
            y = jnp.dot(mid, wd_s[...], preferred_element_type=F32)
            yw = _pack_halves(y)
            for s in range(PK_ROWS):
                ys_out[pl.ds(s, rows, stride=PK_ROWS), :] = yw[:, s * LANES:(s + 1) * LANES]
            if rows < tm:
                ys_out[rows * PK_ROWS:, :] = jnp.zeros(((tm - rows) * PK_ROWS, LANES),
                                                       ys_out.dtype)

        half_rows = tm // 2

        @pl.when(used_ref[i] > half_rows)
        def _():
            swiglu(tm)

        @pl.when(used_ref[i] <= half_rows)
        def _():
            swiglu(half_rows)

    @pl.when(i >= n_act)
    def _():
        ys_out[...] = jnp.zeros(ys_out.shape, ys_out.dtype)


def _local_sort_kernel(lp_ref, hpk_ref, xs_out):
    i = pl.program_id(0)
    t_tile = hpk_ref.shape[0] // PK_ROWS
    n_tok = pl.num_programs(0) * t_tile
    xs_out[...] = jnp.zeros(xs_out.shape, xs_out.dtype)

    def body(q, carry):
        for u in range(SUBLANES):
            t = q * SUBLANES + u
            row = hpk_ref[pl.ds(pl.multiple_of(t * PK_ROWS, PK_ROWS), PK_ROWS), :]
            for k in range(TOP_K):
                dst = pl.multiple_of(lp_ref[k * n_tok + i * t_tile + t], PK_ROWS)
                xs_out[pl.ds(dst, PK_ROWS), :] = row
        return carry
    lax.fori_loop(0, t_tile // SUBLANES, body, 0)


def _combine_kernel(gch_ref, lp_ref, gate_ref, h_ref, ys_hbm, out_ref, ylocal, acc, sems):
    i = pl.program_id(0)
    n = pl.num_programs(0)
    t_tile = h_ref.shape[0]
    n_tok = n * t_tile
    ch_rows = CHUNK * PK_ROWS
    n_ch = ylocal.shape[1] // ch_rows

    def chunk_copy(tile, slot, lc):
        src = pl.multiple_of(gch_ref[tile * n_ch + lc], ch_rows)
        dst = pl.multiple_of(lc * ch_rows, ch_rows)
        pltpu.make_async_copy(ys_hbm.at[pl.ds(src, ch_rows)],
                              ylocal.at[slot, pl.ds(dst, ch_rows)],
                              sems.at[slot]).start()

    def wait_slot(slot):
        pltpu.make_async_copy(ys_hbm.at[pl.ds(0, n_ch * ch_rows)], ylocal.at[slot],
                              sems.at[slot]).wait()

    @pl.when(i == 0)
    def _():
        def body(lc, carry):
            chunk_copy(0, 0, lc)
            return carry
        lax.fori_loop(0, n_ch, body, 0)

    slot = i % 2
    wait_slot(slot)

    nxt = jnp.minimum(i + 1, n - 1)
    for lc in range(n_ch):
        chunk_copy(nxt, 1 - slot, lc)

    tok_unroll = 8

    def tok_body(q, carry):
        for u in range(tok_unroll):
            t = q * tok_unroll + u
            a0 = i * t_tile + t
            a1 = n_tok + a0
            p0 = pl.multiple_of(lp_ref[a0], PK_ROWS)
            p1 = pl.multiple_of(lp_ref[a1], PK_ROWS)
            lo0, hi0 = _unpack_halves(ylocal[slot, pl.ds(p0, PK_ROWS), :])
            lo1, hi1 = _unpack_halves(ylocal[slot, pl.ds(p1, PK_ROWS), :])
            g0 = gate_ref[a0]
            g1 = gate_ref[a1]
            dst = pl.multiple_of(t * PK_ROWS, PK_ROWS)
            acc[0, pl.ds(dst, PK_ROWS), :] = g0 * lo0 + g1 * lo1
            acc[1, pl.ds(dst, PK_ROWS), :] = g0 * hi0 + g1 * hi1
        return carry
    lax.fori_loop(0, t_tile // tok_unroll, tok_body, 0)

    @pl.when(i == n - 1)
    def _():
        wait_slot(1 - slot)

    for half in range(2):
        for s in range(PK_ROWS):
            c0 = half * HALF + s * LANES
            out_ref[:, c0:c0 + LANES] = (h_ref[:, c0:c0 + LANES]
                                         + acc[half, pl.ds(s, t_tile, stride=PK_ROWS), :])


def _rope_tables(seq):
    pos = jnp.arange(seq, dtype=F32)
    inv_freq = ROPE_THETA ** (-jnp.arange(0, ROT_DIM, 2, dtype=F32) / ROT_DIM)
    ang = pos[:, None] * inv_freq[None, :]
    cos, sin = jnp.cos(ang), jnp.sin(ang)
    half = ROT_DIM // 2
    ones = jnp.ones((seq, HEAD_DIM - ROT_DIM), F32)
    zeros = jnp.zeros((seq, HEAD_DIM - ROT_DIM), F32)
    zh = jnp.zeros((seq, half), F32)
    cc = jnp.concatenate([cos, cos, ones], axis=1)
    s_up = jnp.concatenate([-sin, zh, zeros], axis=1)
    s_dn = jnp.concatenate([zh, sin, zeros], axis=1)
    rep = LANES // HEAD_DIM
    return tuple(jnp.tile(t, (1, rep)) for t in (cc, s_up, s_dn))


def _plan(cnt, n_tokens):
    nt = n_tokens // LOCAL_TILE
    experts = jnp.arange(N_EXPERTS, dtype=jnp.int32)
    cntp = ((cnt + CHUNK - 1) // CHUNK) * CHUNK
    loff_end = jnp.cumsum(cntp, axis=1)
    loff = loff_end - cntp
    rows_e = jnp.sum(cntp, axis=0)
    rows_pad = ((rows_e + EXPERT_TILE - 1) // EXPERT_TILE) * EXPERT_TILE
    g_end = jnp.cumsum(rows_pad)
    g_start = g_end - rows_pad
    gpos = g_start[None, :] + jnp.cumsum(cntp, axis=0) - cntp

    lrow = jnp.arange(LT_MAX // CHUNK, dtype=jnp.int32) * CHUNK
    e_lc = jnp.sum((lrow[None, :, None] >= loff_end[:, None, :]).astype(jnp.int32), axis=-1)
    sel = e_lc[:, :, None] == experts
    delta = jnp.sum(jnp.where(sel, (gpos - loff)[:, None, :], 0), axis=-1)
    gch = jnp.where(e_lc < N_EXPERTS, (delta + lrow[None, :]) // CHUNK, 0).reshape(-1)

    n_rows = TOP_K * n_tokens + nt * N_EXPERTS * (CHUNK - 1) + N_EXPERTS * (EXPERT_TILE - CHUNK)
    n_blocks = -(-n_rows // EXPERT_TILE)
    n_rows = n_blocks * EXPERT_TILE
    blk_start = jnp.arange(n_blocks, dtype=jnp.int32) * EXPERT_TILE
    blk_eid = jnp.minimum(
        jnp.sum((blk_start[:, None] >= g_end[None, :]).astype(jnp.int32), axis=-1),
        N_EXPERTS - 1)
    prev = jnp.concatenate([jnp.full((1,), -1, jnp.int32), blk_eid[:-1]])
    first = (blk_eid != prev).astype(jnp.int32)
    n_act = (g_end[-1] // EXPERT_TILE).astype(jnp.int32).reshape(1)
    later = (blk_eid[None, :] > blk_eid[:, None]) & (jnp.arange(n_blocks)[None, :] < n_act)
    next_eid = jnp.min(jnp.where(later, blk_eid[None, :], N_EXPERTS), axis=1)
    next_eid = jnp.where(next_eid < N_EXPERTS, next_eid, -1).astype(jnp.int32)
    of_blk = blk_eid[:, None] == experts
    blk_end = jnp.sum(jnp.where(of_blk, (g_start + rows_e)[None, :], 0), axis=1)
    used = jnp.clip(blk_end - blk_start, 0, EXPERT_TILE).astype(jnp.int32)

    cpb = EXPERT_TILE // CHUNK
    run_start = gpos.T // CHUNK
    run_len = cntp.T // CHUNK
    run_src = (jnp.arange(nt, dtype=jnp.int32)[:, None] * (LT_MAX // CHUNK) + loff // CHUNK).T
    tabs = jnp.stack([run_start, run_len, run_src], axis=0)
    dtabs = tabs - jnp.concatenate(
        [jnp.zeros((3, N_EXPERTS, 1), jnp.int32), tabs[:, :, :-1]], axis=2)
    blk_sel = (blk_eid[:, None] == experts)[:, :, None]
    blk_start_t = jnp.sum(jnp.where(blk_sel, run_start[None], 0), axis=1)
    blk_dtabs = jnp.sum(jnp.where(blk_sel[None], dtabs[:, None], 0), axis=2)
    gc = jnp.arange(n_rows // CHUNK, dtype=jnp.int32).reshape(n_blocks, cpb)
    started = blk_start_t[:, None, :] <= gc[:, :, None]
    picked = jnp.sum(jnp.where(started[None], blk_dtabs[:, :, None, :], 0), axis=-1)
    within = gc - picked[0]
    csrc = jnp.where(within < picked[1], picked[2] + within, 0).reshape(-1)
    return (gch.astype(jnp.int32), csrc.astype(jnp.int32), blk_eid, first, next_eid, used,
            n_act, n_rows)


def kernel(x, norm_mix, w_in, q_norm, k_norm, sinks, w_pool, pool_scale, w_out,
           norm_ffn, w_coarse, b_coarse, w_fine, b_fine, w_gate, w_up, w_down):
    bsz, seq, d = x.shape
    n = bsz * seq
    assert d == D_MODEL and seq % ROW_TILE == 0 and seq % BLOCK == 0
    assert norm_mix.shape[0] == 1, "single-layer problem"
    xf = x.reshape(n, d)

    w_in_b = w_in[0].astype(BF16)
    wq = w_in_b[:, :ATTN_WIDTH]
    wk = w_in_b[:, ATTN_WIDTH:ATTN_WIDTH + KV_WIDTH]
    wv = w_in_b[:, ATTN_WIDTH + KV_WIDTH:ATTN_WIDTH + 2 * KV_WIDTH]
    wu = w_in_b[:, ATTN_WIDTH + 2 * KV_WIDTH:]
    qn = jnp.tile(q_norm[0], N_HEADS).reshape(1, ATTN_WIDTH)
    kn = jnp.tile(k_norm[0], N_KV_HEADS).reshape(1, KV_WIDTH)
    cc, s_up, s_dn = _rope_tables(seq)
    lane_head = jnp.arange(MXU_DIM) // HEAD_DIM
    e_mat = (lane_head[:, None] == lane_head[None, :]).astype(BF16)
    w_pool_b = w_pool[0].astype(BF16)
    pscale = pool_scale[0].reshape(1, POOL_WIDTH)
    w_out_b = w_out[0].astype(BF16)
    wo_attn, wo_pool = w_out_b[:ATTN_WIDTH], w_out_b[ATTN_WIDTH:]
    n_router = N_EXPERT_GROUPS + N_EXPERTS
    w_r = jnp.concatenate([w_coarse[0], w_fine[0]], axis=1)
    w_r = jnp.pad(w_r, ((0, 0), (0, LANES - n_router)))
    w_r_hi = w_r.astype(BF16)
    w_r_lo = (w_r - w_r_hi.astype(F32)).astype(BF16)
    w_r2 = jnp.concatenate([w_r_hi, w_r_lo], axis=1)
    b_r = jnp.pad(jnp.concatenate([b_coarse[0], b_fine[0]]),
                  (0, LANES - n_router)).reshape(1, LANES)

    tm = ROW_TILE
    tiles_per_seq = seq // tm
    n_tiles = n // tm
    idx = jnp.arange(SUB_TILE)
    tri = (idx[None, :] < idx[:, None]).astype(BF16)
    const = lambda *_: (0, 0)
    row_blk = lambda i: (i, 0)

    q, k, v, ksw, vsw, y = pl.pallas_call(
        functools.partial(_inproj_kernel, tiles_per_seq=tiles_per_seq),
        grid=(n_tiles,),
        in_specs=[
            pl.BlockSpec((tm, d), row_blk),
            pl.BlockSpec((1, d), const),
            pl.BlockSpec((d, ATTN_WIDTH), const),
            pl.BlockSpec((d, KV_WIDTH), const),
            pl.BlockSpec((d, KV_WIDTH), const),
            pl.BlockSpec((d, POOL_WIDTH), const),
            pl.BlockSpec((1, ATTN_WIDTH), const),
            pl.BlockSpec((1, KV_WIDTH), const),
            pl.BlockSpec((tm, LANES), lambda i: (i % tiles_per_seq, 0)),
            pl.BlockSpec((tm, LANES), lambda i: (i % tiles_per_seq, 0)),
            pl.BlockSpec((tm, LANES), lambda i: (i % tiles_per_seq, 0)),
            pl.BlockSpec((MXU_DIM, MXU_DIM), const),
            pl.BlockSpec((len(POOL_WINDOWS), POOL_GROUP_DIM, POOL_GROUP_DIM),
                         lambda i: (0, 0, 0)),
            pl.BlockSpec((1, POOL_WIDTH), const),
        ],
        out_specs=[
            pl.BlockSpec((tm, ATTN_WIDTH), row_blk),
            pl.BlockSpec((tm, KV_WIDTH), row_blk),
            pl.BlockSpec((tm, KV_WIDTH), row_blk),
            pl.BlockSpec((tm, KV_WIDTH), row_blk),
            pl.BlockSpec((tm, KV_WIDTH), row_blk),
            pl.BlockSpec((tm, POOL_WIDTH), row_blk),
        ],
        out_shape=[
            jax.ShapeDtypeStruct((n, ATTN_WIDTH), BF16),
            jax.ShapeDtypeStruct((n, KV_WIDTH), BF16),
            jax.ShapeDtypeStruct((n, KV_WIDTH), BF16),
            jax.ShapeDtypeStruct((n, KV_WIDTH), BF16),
            jax.ShapeDtypeStruct((n, KV_WIDTH), BF16),
            jax.ShapeDtypeStruct((n, POOL_WIDTH), BF16),
        ],
        scratch_shapes=[pltpu.VMEM((HALO + tm, POOL_WIDTH), F32)],
        compiler_params=_cparams(),
        name="inproj",
    )(xf, norm_mix[0].reshape(1, d), wq, wk, wv, wu, qn, kn, cc, s_up, s_dn,
      e_mat, w_pool_b, pscale)

    nb = seq // BLOCK
    nq = nb // Q_BLOCKS
    qrows = Q_BLOCKS * BLOCK
    cur = lambda b, j, *_: (b * nq + j, 0)
    prv = lambda b, j, *_: (b * nb + jnp.maximum(Q_BLOCKS * j - 1, 0), 0)
    kv_specs = [pl.BlockSpec((qrows, KV_WIDTH), cur), pl.BlockSpec((BLOCK, KV_WIDTH), prv)]
    attn = pl.pallas_call(
        _attn_kernel,
        grid_spec=pltpu.PrefetchScalarGridSpec(
            num_scalar_prefetch=1,
            grid=(bsz, nq),
            in_specs=[pl.BlockSpec((qrows, ATTN_WIDTH), cur)] + kv_specs * 4,
            out_specs=pl.BlockSpec((qrows, ATTN_WIDTH), cur),
        ),
        out_shape=jax.ShapeDtypeStruct((n, ATTN_WIDTH), BF16),
        compiler_params=_cparams(2),
        name="swa_attn",
    )(sinks[0].astype(F32), q, k, k, ksw, ksw, v, v, vsw, vsw)

    h, hpk, route, cnt = pl.pallas_call(
        _outproj_router_kernel,
        grid=(n_tiles,),
        in_specs=[
            pl.BlockSpec((tm, ATTN_WIDTH), row_blk),
            pl.BlockSpec((tm, POOL_WIDTH), row_blk),
            pl.BlockSpec((tm, d), row_blk),
            pl.BlockSpec((ATTN_WIDTH, d), const),
            pl.BlockSpec((POOL_WIDTH, d), const),
            pl.BlockSpec((1, d), const),
            pl.BlockSpec((d, 2 * LANES), const),
            pl.BlockSpec((1, LANES), const),
            pl.BlockSpec((SUB_TILE, SUB_TILE), const),
        ],
        out_specs=[
            pl.BlockSpec((tm, d), row_blk),
            pl.BlockSpec((tm * PK_ROWS, LANES), row_blk),
            pl.BlockSpec((SUBLANES, tm), lambda i: (0, i)),
            pl.BlockSpec((SUBLANES, LANES), row_blk),
        ],
        out_shape=[
            jax.ShapeDtypeStruct((n, d), F32),
            jax.ShapeDtypeStruct((n * PK_ROWS, LANES), jnp.int32),
            jax.ShapeDtypeStruct((SUBLANES, n), F32),
            jax.ShapeDtypeStruct((n_tiles * SUBLANES, LANES), F32),
        ],
        compiler_params=_cparams(),
        name="outproj_router",
    )(attn, y, xf, wo_attn, wo_pool, norm_ffn[0].reshape(1, d), w_r2, b_r, tri)

    assert tm == LOCAL_TILE
    gates = route[0:TOP_K].reshape(TOP_K * n)
    lp = route[TOP_K:2 * TOP_K].astype(jnp.int32).reshape(TOP_K * n)
    tile_cnt = cnt.reshape(n_tiles, SUBLANES, LANES)[:, 0, :N_EXPERTS].astype(jnp.int32)
    te = EXPERT_TILE
    gch, csrc, blk_eid, first, next_eid, used, n_act, n_rows = _plan(tile_cnt, n)
    n_blocks = n_rows // te
    n_local = n // LOCAL_TILE

    xs = pl.pallas_call(
        _local_sort_kernel,
        grid_spec=pltpu.PrefetchScalarGridSpec(
            num_scalar_prefetch=1,
            grid=(n_local,),
            in_specs=[pl.BlockSpec((LOCAL_TILE * PK_ROWS, LANES), lambda i, *_: (i, 0))],
            out_specs=pl.BlockSpec((LT_MAX * PK_ROWS, LANES), lambda i, *_: (i, 0)),
        ),
        out_shape=jax.ShapeDtypeStruct((n_local * LT_MAX * PK_ROWS, LANES), jnp.int32),
        compiler_params=_cparams(),
        name="local_sort",
    )(lp * PK_ROWS, hpk)

    ys = pl.pallas_call(
        _expert_kernel,
        grid_spec=pltpu.PrefetchScalarGridSpec(
            num_scalar_prefetch=6,
            grid=(n_blocks,),
            in_specs=[pl.BlockSpec(memory_space=pl.ANY)] * 4,
            out_specs=pl.BlockSpec((te * PK_ROWS, LANES), lambda i, *_: (i, 0)),
            scratch_shapes=[
                pltpu.VMEM((2, te * PK_ROWS, LANES), jnp.int32),
                pltpu.VMEM((d, D_EXPERT), F32),
                pltpu.VMEM((d, D_EXPERT), F32),
                pltpu.VMEM((D_EXPERT, d), F32),
                pltpu.VMEM((d, D_EXPERT), BF16),
                pltpu.VMEM((d, D_EXPERT), BF16),
                pltpu.VMEM((D_EXPERT, d), BF16),
                pltpu.SemaphoreType.DMA((2,)),
                pltpu.SemaphoreType.DMA((3,)),
            ],
        ),
        out_shape=jax.ShapeDtypeStruct((n_rows * PK_ROWS, LANES), jnp.int32),
        compiler_params=_cparams(),
        name="experts",
    )(blk_eid, first, n_act, csrc * (CHUNK * PK_ROWS), next_eid, used, xs,
      w_gate[0], w_up[0], w_down[0])

    tt = LOCAL_TILE
    out = pl.pallas_call(
        _combine_kernel,
        grid_spec=pltpu.PrefetchScalarGridSpec(
            num_scalar_prefetch=3,
            grid=(n_local,),
            in_specs=[
                pl.BlockSpec((tt, d), lambda i, *_: (i, 0)),
                pl.BlockSpec(memory_space=pl.ANY),
            ],
            out_specs=pl.BlockSpec((tt, d), lambda i, *_: (i, 0)),
            scratch_shapes=[
                pltpu.VMEM((2, LT_MAX * PK_ROWS, LANES), jnp.int32),
                pltpu.VMEM((2, tt * PK_ROWS, LANES), F32),
                pltpu.SemaphoreType.DMA((2,)),
            ],
        ),
        out_shape=jax.ShapeDtypeStruct((n, d), F32),
        compiler_params=_cparams(),
        name="combine",
    )(gch * (CHUNK * PK_ROWS), lp * PK_ROWS, gates, h, ys)
    return out.reshape(bsz, seq, d)
```

```python
import functools

import jax
import jax.numpy as jnp
from jax import lax
from jax.experimental import pallas as pl
from jax.experimental.pallas import tpu as pltpu

F32 = jnp.float32
BF16 = jnp.bfloat16

D_MODEL = 2048
N_HEADS = 16
N_KV_HEADS = 4
HEAD_DIM = 64
GROUP = N_HEADS // N_KV_HEADS
ROT_DIM = HEAD_DIM // 4
ROPE_THETA = 500000.0
WINDOW = 128
BLOCK = 128
Q_BLOCKS = 2
ATTN_WIDTH = N_HEADS * HEAD_DIM
KV_WIDTH = N_KV_HEADS * HEAD_DIM
POOL_WINDOWS = (2, 4, 8, 16)
POOL_WIDTH = D_MODEL // 2
POOL_GROUP_DIM = POOL_WIDTH // len(POOL_WINDOWS)
N_EXPERT_GROUPS = 4
EXPERTS_PER_GROUP = 8
N_EXPERTS = N_EXPERT_GROUPS * EXPERTS_PER_GROUP
TOP_K = 2
D_EXPERT = 512
EPS = 1e-6

LANES = 128
SUBLANES = 8
MXU_DIM = 256
HALO = 16
NEG_BIG = -1e30
HALF = D_MODEL // 2
PK_ROWS = HALF // LANES
Y_ROWS = D_MODEL // LANES

ROW_TILE = 512
SUB_TILE = 512
EXPERT_TILE = 512
LOCAL_TILE = 512
CHUNK = 8
LT_MAX = TOP_K * LOCAL_TILE + N_EXPERTS * (CHUNK - 1)
VMEM_LIMIT = 56 * 1024 * 1024


def _cparams(n_axes=1):
    return pltpu.CompilerParams(
        dimension_semantics=("arbitrary",) * n_axes,
        vmem_limit_bytes=VMEM_LIMIT,
    )


def _pack_halves(t):
    return pltpu.pack_elementwise([t[:, :HALF], t[:, HALF:]], packed_dtype=BF16)


def _unpack_halves(w):
    return tuple(pltpu.unpack_elementwise(w, index=i, packed_dtype=BF16, unpacked_dtype=F32)
                 for i in range(2))


def _head_sumsq(t, e_ref):
    t2 = (t * t).astype(BF16)
    e = e_ref[...]
    outs = []
    for c in range(t.shape[1] // MXU_DIM):
        sl = slice(c * MXU_DIM, (c + 1) * MXU_DIM)
        outs.append(jnp.dot(t2[:, sl], e, preferred_element_type=F32))
    return outs[0] if len(outs) == 1 else jnp.concatenate(outs, axis=1)


def _norm_rope(t, gain, e_ref, cc, s_up, s_dn, out_scale):
    w = t.shape[1]
    ss = _head_sumsq(t, e_ref)
    tn = t * lax.rsqrt(ss * (1.0 / HEAD_DIM) + EPS) * gain
    reps = w // LANES
    up = pltpu.roll(tn, w - ROT_DIM // 2, axis=1)
    dn = pltpu.roll(tn, ROT_DIM // 2, axis=1)
    c = jnp.concatenate([cc] * reps, axis=1)
    su = jnp.concatenate([s_up] * reps, axis=1)
    sd = jnp.concatenate([s_dn] * reps, axis=1)
    return (tn * c + up * su + dn * sd) * out_scale


def _swap_head_pairs(t):
    w = t.shape[1]
    lane = lax.broadcasted_iota(jnp.int32, t.shape, 1)
    from_up = pltpu.roll(t, w - HEAD_DIM, axis=1)
    from_dn = pltpu.roll(t, HEAD_DIM, axis=1)
    return jnp.where((lane & (LANES - 1)) < HEAD_DIM, from_up, from_dn)


def _inproj_kernel(x_ref, g_ref, wq_ref, wk_ref, wv_ref, wu_ref, qn_ref, kn_ref,
                   cc_ref, su_ref, sd_ref, e_ref, wpool_ref, pscale_ref,
                   q_out, k_out, v_out, ksw_out, vsw_out, y_out, ubuf, *, tiles_per_seq):
    i = pl.program_id(0)
    tm = x_ref.shape[0]

    @pl.when(i % tiles_per_seq == 0)
    def _():
        ubuf[0:HALO, :] = jnp.zeros((HALO, POOL_WIDTH), F32)

    for r0 in range(0, tm, SUB_TILE):
        rows = slice(r0, r0 + SUB_TILE)
        x = x_ref[rows, :]
        ms = jnp.mean(x * x, axis=-1, keepdims=True)
        hn = (x * lax.rsqrt(ms + EPS) * g_ref[...]).astype(BF16)

        cc, s_up, s_dn = cc_ref[rows, :], su_ref[rows, :], sd_ref[rows, :]
        q = jnp.dot(hn, wq_ref[...], preferred_element_type=F32)
        q_out[rows, :] = _norm_rope(q, qn_ref[...], e_ref, cc, s_up, s_dn,
                                    HEAD_DIM ** -0.5).astype(BF16)
        k = jnp.dot(hn, wk_ref[...], preferred_element_type=F32)
        k = _norm_rope(k, kn_ref[...], e_ref, cc, s_up, s_dn, 1.0)
        v = jnp.dot(hn, wv_ref[...], preferred_element_type=F32)
        k_out[rows, :] = k.astype(BF16)
        v_out[rows, :] = v.astype(BF16)
        ksw_out[rows, :] = _swap_head_pairs(k).astype(BF16)
        vsw_out[rows, :] = _swap_head_pairs(v).astype(BF16)

        base = HALO + r0
        ubuf[base:base + SUB_TILE, :] = jnp.dot(hn, wu_ref[...], preferred_element_type=F32)
        pos = (i % tiles_per_seq) * tm + r0 + lax.broadcasted_iota(
            jnp.int32, (SUB_TILE, POOL_GROUP_DIM), 0)
        for g, w in enumerate(POOL_WINDOWS):
            cols = slice(g * POOL_GROUP_DIM, (g + 1) * POOL_GROUP_DIM)
            assert w & (w - 1) == 0 and w - 1 < HALO
            acc = ubuf[base - HALO:base + SUB_TILE, cols]
            shift = 1
            while shift < w:
                acc = acc + pltpu.roll(acc, shift, axis=0)
                shift *= 2
            acc = acc[HALO:, :]
            u_g = ubuf[base:base + SUB_TILE, cols]
            cnt = jnp.minimum(pos + 1, w).astype(F32)
            d = (acc / cnt - u_g).astype(BF16)
            y = jnp.dot(d, wpool_ref[g], preferred_element_type=F32)
            y_out[rows, cols] = (y * pscale_ref[:, cols]).astype(BF16)
    ubuf[0:HALO, :] = ubuf[tm:tm + HALO, :]


def _attn_kernel(sinks_ref, q_ref, kc_ref, kp_ref, kswc_ref, kswp_ref,
                 vc_ref, vp_ref, vswc_ref, vswp_ref, o_ref):
    j = pl.program_id(1)
    two = 2 * BLOCK
    row = lax.broadcasted_iota(jnp.int32, (two, two), 0) & (BLOCK - 1)
    col = lax.broadcasted_iota(jnp.int32, (two, two), 1)
    diff = row + BLOCK - col
    local = (diff >= 0) & (diff < WINDOW)
    lo_lanes = lax.broadcasted_iota(jnp.int32, (two, LANES), 1) < HEAD_DIM
    top_rows = lax.broadcasted_iota(jnp.int32, (two, 1), 0) < BLOCK
    zero = jnp.zeros((two, LANES), BF16)

    for sb in range(q_ref.shape[0] // BLOCK):
        rows = slice(sb * BLOCK, (sb + 1) * BLOCK)
        if sb == 0:
            def band(cur_ref, prev_ref):
                return jnp.concatenate([prev_ref[...], cur_ref[0:BLOCK, :]], axis=0)
            valid = local & ((col >= BLOCK) | (j > 0))
        else:
            def band(cur_ref, prev_ref, sb=sb):
                return cur_ref[(sb - 1) * BLOCK:(sb + 1) * BLOCK, :]
            valid = local
        k_nat, k_swp = band(kc_ref, kp_ref), band(kswc_ref, kswp_ref)
        v_nat, v_swp = band(vc_ref, vp_ref), band(vswc_ref, vswp_ref)

        for g in range(N_KV_HEADS):
            c0 = (g // 2) * LANES
            kcols = (k_nat[:, c0:c0 + LANES], k_swp[:, c0:c0 + LANES])
            vcols = (v_nat[:, c0:c0 + LANES], v_swp[:, c0:c0 + LANES])
            in_lo, in_hi = (0, 1) if g % 2 == 0 else (1, 0)
            q0 = g * GROUP * HEAD_DIM
            qq = jnp.concatenate([q_ref[rows, q0:q0 + LANES],
                                  q_ref[rows, q0 + LANES:q0 + 2 * LANES]], axis=0)
            halves = []
            for half, src in ((0, in_lo), (1, in_hi)):
                qm = jnp.where(lo_lanes, qq, zero) if half == 0 else jnp.where(lo_lanes, zero, qq)
                s = lax.dot_general(qm, kcols[src], (((1,), (1,)), ((), ())),
                                    preferred_element_type=F32)
                s = jnp.where(valid, s, NEG_BIG)
                h_top = g * GROUP + half
                sink = jnp.where(top_rows, sinks_ref[h_top], sinks_ref[h_top + 2])
                m = jnp.maximum(jnp.max(s, axis=-1, keepdims=True), sink)
                p = jnp.exp(s - m)
                denom = jnp.sum(p, axis=-1, keepdims=True) + jnp.exp(sink - m)
                o = jnp.dot(p.astype(BF16), vcols[src], preferred_element_type=F32)
                halves.append(o / denom)
            o_pair = jnp.where(lo_lanes, halves[0], halves[1]).astype(BF16)
            o_ref[rows, q0:q0 + LANES] = o_pair[0:BLOCK, :]
            o_ref[rows, q0 + LANES:q0 + 2 * LANES] = o_pair[BLOCK:, :]


def _outproj_router_kernel(o_ref, y_ref, x_ref, woa_ref, wop_ref, g_ref, wr_ref,
                           br_ref, tri_ref, h_out, hpk_out, route_out, cnt_out):
    n_sub = x_ref.shape[0] // SUB_TILE
    picks = [_outproj_router_subtile(sub, o_ref, y_ref, x_ref, woa_ref, wop_ref, g_ref,
                                     wr_ref, br_ref, tri_ref, h_out, hpk_out)
             for sub in range(n_sub)]

    lane = lax.broadcasted_iota(jnp.int32, (SUB_TILE, LANES), 1)
    lane_f = lane.astype(F32)
    cnt_tile = sum(pk[5] for pk in picks)
    cnt_pad = jnp.floor((cnt_tile + (CHUNK - 1)) * (1.0 / CHUNK)) * CHUNK
    run_start = lax.dot_general(
        jnp.broadcast_to(cnt_pad, (SUBLANES, LANES)).astype(BF16),
        tri_ref[0:LANES, 0:LANES], (((1,), (1,)), ((), ())),
        preferred_element_type=F32)[0:1, :]
    before = jnp.zeros((1, LANES), F32)
    for sub, (e1, e2, g1, g2, prefix, cnt_sub) in enumerate(picks):
        table = prefix + (run_start + before)
        lp1 = jnp.sum(jnp.where(lane_f == e1, table, 0.0), axis=-1, keepdims=True)
        lp2 = jnp.sum(jnp.where(lane_f == e2, table, 0.0), axis=-1, keepdims=True)
        cols = jnp.where(lane == 0, g1,
                         jnp.where(lane == 1, g2,
                                   jnp.where(lane == 2, lp1,
                                             jnp.where(lane == 3, lp2, 0.0))))
        pick = (lax.broadcasted_iota(jnp.int32, (SUBLANES, LANES), 0)
                == lax.broadcasted_iota(jnp.int32, (SUBLANES, LANES), 1)).astype(F32)
        route_out[:, sub * SUB_TILE:(sub + 1) * SUB_TILE] = lax.dot_general(
            pick, cols, (((1,), (1,)), ((), ())), precision=lax.Precision.HIGHEST,
            preferred_element_type=F32)
        before = before + cnt_sub
    cnt_out[...] = jnp.broadcast_to(cnt_tile, cnt_out.shape)


def _outproj_router_subtile(sub, o_ref, y_ref, x_ref, woa_ref, wop_ref, g_ref, wr_ref,
                            br_ref, tri_ref, h_out, hpk_out):
    tm = SUB_TILE
    r0 = sub * SUB_TILE
    rows = slice(r0, r0 + SUB_TILE)
    h = (x_ref[rows, :]
         + jnp.dot(o_ref[rows, :], woa_ref[...], preferred_element_type=F32)
         + jnp.dot(y_ref[rows, :], wop_ref[...], preferred_element_type=F32))
    h_out[rows, :] = h
    ms = jnp.mean(h * h, axis=-1, keepdims=True)
    hn = h * lax.rsqrt(ms + EPS) * g_ref[...]
    hb = hn.astype(BF16)
    packed = _pack_halves(hn)
    for s in range(PK_ROWS):
        hpk_out[pl.ds(r0 * PK_ROWS + s, tm, stride=PK_ROWS), :] = (
            packed[:, s * LANES:(s + 1) * LANES])

    both = jnp.dot(hb, wr_ref[...], preferred_element_type=F32)
    logits = both[:, :LANES] + both[:, LANES:] + br_ref[...]
    lane = lax.broadcasted_iota(jnp.int32, (tm, LANES), 1)
    lane_f = lane.astype(F32)

    def first_lane_of(mask):
        return jnp.min(jnp.where(mask, lane_f, float(LANES)), axis=-1, keepdims=True)

    def row_max(t):
        return jnp.max(t, axis=-1, keepdims=True)

    coarse = jnp.where(lane < N_EXPERT_GROUPS, logits, NEG_BIG)
    cmax = row_max(coarse)
    grp = first_lane_of(coarse == cmax)
    csum = jnp.sum(jnp.exp(coarse - cmax), axis=-1, keepdims=True)
    flo = N_EXPERT_GROUPS + EXPERTS_PER_GROUP * grp
    fine = jnp.where((lane_f >= flo) & (lane_f < flo + EXPERTS_PER_GROUP), logits, NEG_BIG)
    f1 = row_max(fine)
    i1 = first_lane_of(fine == f1)
    rest = jnp.where(lane_f == i1, NEG_BIG, fine)
    f2 = row_max(rest)
    i2 = first_lane_of(rest == f2)
    ratio = jnp.exp(f2 - f1)
    scale = (1.0 / csum) / (1.0 + ratio)
    g1 = scale
    g2 = scale * ratio
    e1 = i1 - N_EXPERT_GROUPS
    e2 = i2 - N_EXPERT_GROUPS

    oh = jnp.where(lane_f == e1, 1.0, 0.0) + jnp.where(lane_f == e2, 1.0, 0.0)
    prefix = jnp.dot(tri_ref[...], oh.astype(BF16), preferred_element_type=F32)
    return e1, e2, g1, g2, prefix, jnp.sum(oh, axis=0, keepdims=True)


def _expert_kernel(eid_ref, first_ref, nact_ref, off_ref, next_ref, used_ref,
                   hpk_hbm, wg_hbm, wu_hbm, wd_hbm, ys_out,
                   xbuf, wg_f, wu_f, wd_f, wg_s, wu_s, wd_s, sems, wsems):
    i = pl.program_id(0)
    tm = ys_out.shape[0] // PK_ROWS
    n_act = nact_ref[0]

    def weight_copies(e):
        return (pltpu.make_async_copy(wg_hbm.at[e], wg_f, wsems.at[0]),
                pltpu.make_async_copy(wu_hbm.at[e], wu_f, wsems.at[1]),
                pltpu.make_async_copy(wd_hbm.at[e], wd_f, wsems.at[2]))

    ch_rows = CHUNK * PK_ROWS
    n_ch = tm // CHUNK

    def gather(blk, slot):
        for c in range(n_ch):
            src = pl.multiple_of(off_ref[blk * n_ch + c], ch_rows)
            pltpu.make_async_copy(hpk_hbm.at[pl.ds(src, ch_rows)],
                                  xbuf.at[slot, pl.ds(c * ch_rows, ch_rows)],
                                  sems.at[slot]).start()

    @pl.when(i == 0)
    def _():
        for cp in weight_copies(eid_ref[0]):
            cp.start()
        gather(0, 0)

    @pl.when(i + 1 < n_act)
    def _():
        gather(i + 1, (i + 1) % 2)

    @pl.when(i < n_act)
    def _():
        slot = i % 2

        @pl.when(first_ref[i] == 1)
        def _():
            for cp in weight_copies(eid_ref[i]):
                cp.wait()
            wg_s[...] = wg_f[...].astype(BF16)
            wu_s[...] = wu_f[...].astype(BF16)
            wd_s[...] = wd_f[...].astype(BF16)

            @pl.when(next_ref[i] >= 0)
            def _():
                for cp in weight_copies(next_ref[i]):
                    cp.start()

        pltpu.make_async_copy(hpk_hbm.at[pl.ds(0, tm * PK_ROWS)], xbuf.at[slot],
                              sems.at[slot]).wait()

        def swiglu(rows):
            xu = jnp.concatenate(
                [xbuf[slot, pl.ds(s, rows, stride=PK_ROWS), :] for s in range(PK_ROWS)],
                axis=1)
            x_lo, x_hi = (v.astype(BF16) for v in _unpack_halves(xu))
            a = (jnp.dot(x_lo, wg_s[0:HALF, :], preferred_element_type=F32)
                 + jnp.dot(x_hi, wg_s[HALF:, :], preferred_element_type=F32))
            b = (jnp.dot(x_lo, wu_s[0:HALF, :], preferred_element_type=F32)
                 + jnp.dot(x_hi, wu_s[HALF:, :], preferred_element_type=F32))
            mid = (a * jax.nn.sigmoid(a) * b).astype(BF16)
            y = jnp.dot(mid, wd_s[...], preferred_element_type=F32)
            yw = _pack_halves(y)
            for s in range(PK_ROWS):
                ys_out[pl.ds(s, rows, stride=PK_ROWS), :] = yw[:, s * LANES:(s + 1) * LANES]
            if rows < tm:
                ys_out[rows * PK_ROWS:, :] = jnp.zeros(((tm - rows) * PK_ROWS, LANES),
                                                       ys_out.dtype)

        half_rows = tm // 2

        @pl.when(used_ref[i] > half_rows)
        def _():
            swiglu(tm)

        @pl.when(used_ref[i] <= half_rows)
        def _():
            swiglu(half_rows)

    @pl.when(i >= n_act)
    def _():
        ys_out[...] = jnp.zeros(ys_out.shape, ys_out.dtype)


def _local_sort_kernel(lp_ref, hpk_ref, xs_out):
    i = pl.program_id(0)
    t_tile = hpk_ref.shape[0] // PK_ROWS
    n_tok = pl.num_programs(0) * t_tile
    xs_out[...] = jnp.zeros(xs_out.shape, xs_out.dtype)

    def body(q, carry):
        for u in range(SUBLANES):
            t = q * SUBLANES + u
            row = hpk_ref[pl.ds(pl.multiple_of(t * PK_ROWS, PK_ROWS), PK_ROWS), :]
            for k in range(TOP_K):
                dst = pl.multiple_of(lp_ref[k * n_tok + i * t_tile + t], PK_ROWS)
                xs_out[pl.ds(dst, PK_ROWS), :] = row
        return carry
    lax.fori_loop(0, t_tile // SUBLANES, body, 0)


def _combine_kernel(gch_ref, lp_ref, gate_ref, h_ref, ys_hbm, out_ref, ylocal, acc, sems):
    i = pl.program_id(0)
    n = pl.num_programs(0)
    t_tile = h_ref.shape[0]
    n_tok = n * t_tile
    ch_rows = CHUNK * PK_ROWS
    n_ch = ylocal.shape[1] // ch_rows

    def chunk_copy(tile, slot, lc):
        src = pl.multiple_of(gch_ref[tile * n_ch + lc], ch_rows)
        dst = pl.multiple_of(lc * ch_rows, ch_rows)
        pltpu.make_async_copy(ys_hbm.at[pl.ds(src, ch_rows)],
                              ylocal.at[slot, pl.ds(dst, ch_rows)],
                              sems.at[slot]).start()

    def wait_slot(slot):
        pltpu.make_async_copy(ys_hbm.at[pl.ds(0, n_ch * ch_rows)], ylocal.at[slot],
                              sems.at[slot]).wait()

    @pl.when(i == 0)
    def _():
        def body(lc, carry):
            chunk_copy(0, 0, lc)
            return carry
        lax.fori_loop(0, n_ch, body, 0)

    slot = i % 2
    wait_slot(slot)

    nxt = jnp.minimum(i + 1, n - 1)
    tok_unroll = 4
    n_groups = t_tile // tok_unroll
    per_group = -(-n_ch // n_groups)
    issue_groups = n_ch // per_group
    assert issue_groups * per_group == n_ch and issue_groups <= n_groups

    def tok_body(q, carry, issue):
        if issue:
            for u in range(per_group):
                chunk_copy(nxt, 1 - slot, q * per_group + u)
        for u in range(tok_unroll):
            t = q * tok_unroll + u
            a0 = i * t_tile + t
            a1 = n_tok + a0
            p0 = pl.multiple_of(lp_ref[a0], PK_ROWS)
            p1 = pl.multiple_of(lp_ref[a1], PK_ROWS)
            lo0, hi0 = _unpack_halves(ylocal[slot, pl.ds(p0, PK_ROWS), :])
            lo1, hi1 = _unpack_halves(ylocal[slot, pl.ds(p1, PK_ROWS), :])
            g0 = gate_ref[a0]
            g1 = gate_ref[a1]
            dst = pl.multiple_of(t * PK_ROWS, PK_ROWS)
            acc[0, pl.ds(dst, PK_ROWS), :] = g0 * lo0 + g1 * lo1
            acc[1, pl.ds(dst, PK_ROWS), :] = g0 * hi0 + g1 * hi1
        return carry
    lax.fori_loop(0, issue_groups, functools.partial(tok_body, issue=True), 0)
    lax.fori_loop(issue_groups, n_groups, functools.partial(tok_body, issue=False), 0)

    @pl.when(i == n - 1)
    def _():
        wait_slot(1 - slot)

    for half in range(2):
        for s in range(PK_ROWS):
            c0 = half * HALF + s * LANES
            out_ref[:, c0:c0 + LANES] = (h_ref[:, c0:c0 + LANES]
                                         + acc[half, pl.ds(s, t_tile, stride=PK_ROWS), :])


def _rope_tables(seq):
    pos = jnp.arange(seq, dtype=F32)
    inv_freq = ROPE_THETA ** (-jnp.arange(0, ROT_DIM, 2, dtype=F32) / ROT_DIM)
    ang = pos[:, None] * inv_freq[None, :]
    cos, sin = jnp.cos(ang), jnp.sin(ang)
    half = ROT_DIM // 2
    ones = jnp.ones((seq, HEAD_DIM - ROT_DIM), F32)
    zeros = jnp.zeros((seq, HEAD_DIM - ROT_DIM), F32)
    zh = jnp.zeros((seq, half), F32)
    cc = jnp.concatenate([cos, cos, ones], axis=1)
    s_up = jnp.concatenate([-sin, zh, zeros], axis=1)
    s_dn = jnp.concatenate([zh, sin, zeros], axis=1)
    rep = LANES // HEAD_DIM
    return tuple(jnp.tile(t, (1, rep)) for t in (cc, s_up, s_dn))


def _plan(cnt, n_tokens):
    nt = n_tokens // LOCAL_TILE
    experts = jnp.arange(N_EXPERTS, dtype=jnp.int32)
    cntp = ((cnt + CHUNK - 1) // CHUNK) * CHUNK
    loff_end = jnp.cumsum(cntp, axis=1)
    loff = loff_end - cntp
    rows_e = jnp.sum(cntp, axis=0)
    rows_pad = ((rows_e + EXPERT_TILE - 1) // EXPERT_TILE) * EXPERT_TILE
    g_end = jnp.cumsum(rows_pad)
    g_start = g_end - rows_pad
    gpos = g_start[None, :] + jnp.cumsum(cntp, axis=0) - cntp

    lrow = jnp.arange(LT_MAX // CHUNK, dtype=jnp.int32) * CHUNK
    e_lc = jnp.sum((lrow[None, :, None] >= loff_end[:, None, :]).astype(jnp.int32), axis=-1)
    sel = e_lc[:, :, None] == experts
    delta = jnp.sum(jnp.where(sel, (gpos - loff)[:, None, :], 0), axis=-1)
    gch = jnp.where(e_lc < N_EXPERTS, (delta + lrow[None, :]) // CHUNK, 0).reshape(-1)

    n_rows = TOP_K * n_tokens + nt * N_EXPERTS * (CHUNK - 1) + N_EXPERTS * (EXPERT_TILE - CHUNK)
    n_blocks = -(-n_rows // EXPERT_TILE)
    n_rows = n_blocks * EXPERT_TILE
    blk_start = jnp.arange(n_blocks, dtype=jnp.int32) * EXPERT_TILE
    blk_eid = jnp.minimum(
        jnp.sum((blk_start[:, None] >= g_end[None, :]).astype(jnp.int32), axis=-1),
        N_EXPERTS - 1)
    prev = jnp.concatenate([jnp.full((1,), -1, jnp.int32), blk_eid[:-1]])
    first = (blk_eid != prev).astype(jnp.int32)
    n_act = (g_end[-1] // EXPERT_TILE).astype(jnp.int32).reshape(1)
    later = (blk_eid[None, :] > blk_eid[:, None]) & (jnp.arange(n_blocks)[None, :] < n_act)
    next_eid = jnp.min(jnp.where(later, blk_eid[None, :], N_EXPERTS), axis=1)
    next_eid = jnp.where(next_eid < N_EXPERTS, next_eid, -1).astype(jnp.int32)
    of_blk = blk_eid[:, None] == experts
    blk_end = jnp.sum(jnp.where(of_blk, (g_start + rows_e)[None, :], 0), axis=1)
    used = jnp.clip(blk_end - blk_start, 0, EXPERT_TILE).astype(jnp.int32)

    cpb = EXPERT_TILE // CHUNK
    run_start = gpos.T // CHUNK
    run_len = cntp.T // CHUNK
    run_src = (jnp.arange(nt, dtype=jnp.int32)[:, None] * (LT_MAX // CHUNK) + loff // CHUNK).T
    tabs = jnp.stack([run_start, run_len, run_src], axis=0)
    dtabs = tabs - jnp.concatenate(
        [jnp.zeros((3, N_EXPERTS, 1), jnp.int32), tabs[:, :, :-1]], axis=2)
    blk_sel = (blk_eid[:, None] == experts)[:, :, None]
    blk_start_t = jnp.sum(jnp.where(blk_sel, run_start[None], 0), axis=1)
    blk_dtabs = jnp.sum(jnp.where(blk_sel[None], dtabs[:, None], 0), axis=2)
    gc = jnp.arange(n_rows // CHUNK, dtype=jnp.int32).reshape(n_blocks, cpb)
    started = blk_start_t[:, None, :] <= gc[:, :, None]
    picked = jnp.sum(jnp.where(started[None], blk_dtabs[:, :, None, :], 0), axis=-1)
    within = gc - picked[0]
    csrc = jnp.where(within < picked[1], picked[2] + within, 0).reshape(-1)
    return (gch.astype(jnp.int32), csrc.astype(jnp.int32), blk_eid, first, next_eid, used,
            n_act, n_rows)


def kernel(x, norm_mix, w_in, q_norm, k_norm, sinks, w_pool, pool_scale, w_out,
           norm_ffn, w_coarse, b_coarse, w_fine, b_fine, w_gate, w_up, w_down):
    bsz, seq, d = x.shape
    n = bsz * seq
    assert d == D_MODEL and seq % ROW_TILE == 0 and seq % BLOCK == 0
    assert norm_mix.shape[0] == 1, "single-layer problem"
    xf = x.reshape(n, d)

    w_in_b = w_in[0].astype(BF16)
    wq = w_in_b[:, :ATTN_WIDTH]
    wk = w_in_b[:, ATTN_WIDTH:ATTN_WIDTH + KV_WIDTH]
    wv = w_in_b[:, ATTN_WIDTH + KV_WIDTH:ATTN_WIDTH + 2 * KV_WIDTH]
    wu = w_in_b[:, ATTN_WIDTH + 2 * KV_WIDTH:]
    qn = jnp.tile(q_norm[0], N_HEADS).reshape(1, ATTN_WIDTH)
    kn = jnp.tile(k_norm[0], N_KV_HEADS).reshape(1, KV_WIDTH)
    cc, s_up, s_dn = _rope_tables(seq)
    lane_head = jnp.arange(MXU_DIM) // HEAD_DIM
    e_mat = (lane_head[:, None] == lane_head[None, :]).astype(BF16)
    w_pool_b = w_pool[0].astype(BF16)
    pscale = pool_scale[0].reshape(1, POOL_WIDTH)
    w_out_b = w_out[0].astype(BF16)
    wo_attn, wo_pool = w_out_b[:ATTN_WIDTH], w_out_b[ATTN_WIDTH:]
    n_router = N_EXPERT_GROUPS + N_EXPERTS
    w_r = jnp.concatenate([w_coarse[0], w_fine[0]], axis=1)
    w_r = jnp.pad(w_r, ((0, 0), (0, LANES - n_router)))
    w_r_hi = w_r.astype(BF16)
    w_r_lo = (w_r - w_r_hi.astype(F32)).astype(BF16)
    w_r2 = jnp.concatenate([w_r_hi, w_r_lo], axis=1)
    b_r = jnp.pad(jnp.concatenate([b_coarse[0], b_fine[0]]),
                  (0, LANES - n_router)).reshape(1, LANES)

    tm = ROW_TILE
    tiles_per_seq = seq // tm
    n_tiles = n // tm
    idx = jnp.arange(SUB_TILE)
    tri = (idx[None, :] < idx[:, None]).astype(BF16)
    const = lambda *_: (0, 0)
    row_blk = lambda i: (i, 0)

    q, k, v, ksw, vsw, y = pl.pallas_call(
        functools.partial(_inproj_kernel, tiles_per_seq=tiles_per_seq),
        grid=(n_tiles,),
        in_specs=[
            pl.BlockSpec((tm, d), row_blk),
            pl.BlockSpec((1, d), const),
            pl.BlockSpec((d, ATTN_WIDTH), const),
            pl.BlockSpec((d, KV_WIDTH), const),
            pl.BlockSpec((d, KV_WIDTH), const),
            pl.BlockSpec((d, POOL_WIDTH), const),
            pl.BlockSpec((1, ATTN_WIDTH), const),
            pl.BlockSpec((1, KV_WIDTH), const),
            pl.BlockSpec((tm, LANES), lambda i: (i % tiles_per_seq, 0)),
            pl.BlockSpec((tm, LANES), lambda i: (i % tiles_per_seq, 0)),
            pl.BlockSpec((tm, LANES), lambda i: (i % tiles_per_seq, 0)),
            pl.BlockSpec((MXU_DIM, MXU_DIM), const),
            pl.BlockSpec((len(POOL_WINDOWS), POOL_GROUP_DIM, POOL_GROUP_DIM),
                         lambda i: (0, 0, 0)),
            pl.BlockSpec((1, POOL_WIDTH), const),
        ],
        out_specs=[
            pl.BlockSpec((tm, ATTN_WIDTH), row_blk),
            pl.BlockSpec((tm, KV_WIDTH), row_blk),
            pl.BlockSpec((tm, KV_WIDTH), row_blk),
            pl.BlockSpec((tm, KV_WIDTH), row_blk),
            pl.BlockSpec((tm, KV_WIDTH), row_blk),
            pl.BlockSpec((tm, POOL_WIDTH), row_blk),
        ],
        out_shape=[
            jax.ShapeDtypeStruct((n, ATTN_WIDTH), BF16),
            jax.ShapeDtypeStruct((n, KV_WIDTH), BF16),
            jax.ShapeDtypeStruct((n, KV_WIDTH), BF16),
            jax.ShapeDtypeStruct((n, KV_WIDTH), BF16),
            jax.ShapeDtypeStruct((n, KV_WIDTH), BF16),
            jax.ShapeDtypeStruct((n, POOL_WIDTH), BF16),
        ],
        scratch_shapes=[pltpu.VMEM((HALO + tm, POOL_WIDTH), F32)],
        compiler_params=_cparams(),
        name="inproj",
    )(xf, norm_mix[0].reshape(1, d), wq, wk, wv, wu, qn, kn, cc, s_up, s_dn,
      e_mat, w_pool_b, pscale)

    nb = seq // BLOCK
    nq = nb // Q_BLOCKS
    qrows = Q_BLOCKS * BLOCK
    cur = lambda b, j, *_: (b * nq + j, 0)
    prv = lambda b, j, *_: (b * nb + jnp.maximum(Q_BLOCKS * j - 1, 0), 0)
    kv_specs = [pl.BlockSpec((qrows, KV_WIDTH), cur), pl.BlockSpec((BLOCK, KV_WIDTH), prv)]
    attn = pl.pallas_call(
        _attn_kernel,
        grid_spec=pltpu.PrefetchScalarGridSpec(
            num_scalar_prefetch=1,
            grid=(bsz, nq),
            in_specs=[pl.BlockSpec((qrows, ATTN_WIDTH), cur)] + kv_specs * 4,
            out_specs=pl.BlockSpec((qrows, ATTN_WIDTH), cur),
        ),
        out_shape=jax.ShapeDtypeStruct((n, ATTN_WIDTH), BF16),
        compiler_params=_cparams(2),
        name="swa_attn",
    )(sinks[0].astype(F32), q, k, k, ksw, ksw, v, v, vsw, vsw)

    h, hpk, route, cnt = pl.pallas_call(
        _outproj_router_kernel,
        grid=(n_tiles,),
        in_specs=[
            pl.BlockSpec((tm, ATTN_WIDTH), row_blk),
            pl.BlockSpec((tm, POOL_WIDTH), row_blk),
            pl.BlockSpec((tm, d), row_blk),
            pl.BlockSpec((ATTN_WIDTH, d), const),
            pl.BlockSpec((POOL_WIDTH, d), const),
            pl.BlockSpec((1, d), const),
            pl.BlockSpec((d, 2 * LANES), const),
            pl.BlockSpec((1, LANES), const),
            pl.BlockSpec((SUB_TILE, SUB_TILE), const),
        ],
        out_specs=[
            pl.BlockSpec((tm, d), row_blk),
            pl.BlockSpec((tm * PK_ROWS, LANES), row_blk),
            pl.BlockSpec((SUBLANES, tm), lambda i: (0, i)),
            pl.BlockSpec((SUBLANES, LANES), row_blk),
        ],
        out_shape=[
            jax.ShapeDtypeStruct((n, d), F32),
            jax.ShapeDtypeStruct((n * PK_ROWS, LANES), jnp.int32),
            jax.ShapeDtypeStruct((SUBLANES, n), F32),
            jax.ShapeDtypeStruct((n_tiles * SUBLANES, LANES), F32),
        ],
        compiler_params=_cparams(),
        name="outproj_router",
    )(attn, y, xf, wo_attn, wo_pool, norm_ffn[0].reshape(1, d), w_r2, b_r, tri)

    assert tm == LOCAL_TILE
    gates = route[0:TOP_K].reshape(TOP_K * n)
    lp = route[TOP_K:2 * TOP_K].astype(jnp.int32).reshape(TOP_K * n)
    tile_cnt = cnt.reshape(n_tiles, SUBLANES, LANES)[:, 0, :N_EXPERTS].astype(jnp.int32)
    te = EXPERT_TILE
    gch, csrc, blk_eid, first, next_eid, used, n_act, n_rows = _plan(tile_cnt, n)
    n_blocks = n_rows // te
    n_local = n // LOCAL_TILE

    xs = pl.pallas_call(
        _local_sort_kernel,
        grid_spec=pltpu.PrefetchScalarGridSpec(
            num_scalar_prefetch=1,
            grid=(n_local,),
            in_specs=[pl.BlockSpec((LOCAL_TILE * PK_ROWS, LANES), lambda i, *_: (i, 0))],
            out_specs=pl.BlockSpec((LT_MAX * PK_ROWS, LANES), lambda i, *_: (i, 0)),
        ),
        out_shape=jax.ShapeDtypeStruct((n_local * LT_MAX * PK_ROWS, LANES), jnp.int32),
        compiler_params=_cparams(),
        name="local_sort",
    )(lp * PK_ROWS, hpk)

    ys = pl.pallas_call(
        _expert_kernel,
        grid_spec=pltpu.PrefetchScalarGridSpec(
            num_scalar_prefetch=6,
            grid=(n_blocks,),
            in_specs=[pl.BlockSpec(memory_space=pl.ANY)] * 4,
            out_specs=pl.BlockSpec((te * PK_ROWS, LANES), lambda i, *_: (i, 0)),
            scratch_shapes=[
                pltpu.VMEM((2, te * PK_ROWS, LANES), jnp.int32),
                pltpu.VMEM((d, D_EXPERT), F32),
                pltpu.VMEM((d, D_EXPERT), F32),
                pltpu.VMEM((D_EXPERT, d), F32),
                pltpu.VMEM((d, D_EXPERT), BF16),
                pltpu.VMEM((d, D_EXPERT), BF16),
                pltpu.VMEM((D_EXPERT, d), BF16),
                pltpu.SemaphoreType.DMA((2,)),
                pltpu.SemaphoreType.DMA((3,)),
            ],
        ),
        out_shape=jax.ShapeDtypeStruct((n_rows * PK_ROWS, LANES), jnp.int32),
        compiler_params=_cparams(),
        name="experts",
    )(blk_eid, first, n_act, csrc * (CHUNK * PK_ROWS), next_eid, used, xs,
      w_gate[0], w_up[0], w_down[0])

    tt = LOCAL_TILE
    out = pl.pallas_call(
        _combine_kernel,
        grid_spec=pltpu.PrefetchScalarGridSpec(
            num_scalar_prefetch=3,
            grid=(n_local,),
            in_specs=[
                pl.BlockSpec((tt, d), lambda i, *_: (i, 0)),
                pl.BlockSpec(memory_space=pl.ANY),
            ],
            out_specs=pl.BlockSpec((tt, d), lambda i, *_: (i, 0)),
            scratch_shapes=[
                pltpu.VMEM((2, LT_MAX * PK_ROWS, LANES), jnp.int32),
                pltpu.VMEM((2, tt * PK_ROWS, LANES), F32),
                pltpu.SemaphoreType.DMA((2,)),
            ],
        ),
        out_shape=jax.ShapeDtypeStruct((n, d), F32),
        compiler_params=_cparams(),
        name="combine",
    )(gch * (CHUNK * PK_ROWS), lp * PK_ROWS, gates, h, ys)
    return out.reshape(bsz, seq, d)
```

```python
import functools

import jax
import jax.numpy as jnp
from jax import lax
from jax.experimental import pallas as pl
from jax.experimental.pallas import tpu as pltpu

F32 = jnp.float32
BF16 = jnp.bfloat16

D_MODEL = 2048
N_HEADS = 16
N_KV_HEADS = 4
HEAD_DIM = 64
GROUP = N_HEADS // N_KV_HEADS
ROT_DIM = HEAD_DIM // 4
ROPE_THETA = 500000.0
WINDOW = 128
BLOCK = 128
Q_BLOCKS = 4
ATTN_WIDTH = N_HEADS * HEAD_DIM
KV_WIDTH = N_KV_HEADS * HEAD_DIM
POOL_WINDOWS = (2, 4, 8, 16)
POOL_WIDTH = D_MODEL // 2
POOL_GROUP_DIM = POOL_WIDTH // len(POOL_WINDOWS)
N_EXPERT_GROUPS = 4
EXPERTS_PER_GROUP = 8
N_EXPERTS = N_EXPERT_GROUPS * EXPERTS_PER_GROUP
TOP_K = 2
D_EXPERT = 512
EPS = 1e-6

LANES = 128
SUBLANES = 8
MXU_DIM = 256
HALO = 16
NEG_BIG = -1e30
HALF = D_MODEL // 2
PK_ROWS = HALF // LANES
Y_ROWS = D_MODEL // LANES

ROW_TILE = 512
SUB_TILE = 512
EXPERT_TILE = 512
LOCAL_TILE = 512
CHUNK = 8
LT_MAX = TOP_K * LOCAL_TILE + N_EXPERTS * (CHUNK - 1)
VMEM_LIMIT = 56 * 1024 * 1024


def _cparams(n_axes=1):
    return pltpu.CompilerParams(
        dimension_semantics=("arbitrary",) * n_axes,
        vmem_limit_bytes=VMEM_LIMIT,
    )


def _pack_halves(t):
    return pltpu.pack_elementwise([t[:, :HALF], t[:, HALF:]], packed_dtype=BF16)


def _unpack_halves(w):
    return tuple(pltpu.unpack_elementwise(w, index=i, packed_dtype=BF16, unpacked_dtype=F32)
                 for i in range(2))


def _head_sumsq(t, e_ref):
    t2 = (t * t).astype(BF16)
    e = e_ref[...]
    outs = []
    for c in range(t.shape[1] // MXU_DIM):
        sl = slice(c * MXU_DIM, (c + 1) * MXU_DIM)
        outs.append(jnp.dot(t2[:, sl], e, preferred_element_type=F32))
    return outs[0] if len(outs) == 1 else jnp.concatenate(outs, axis=1)


def _norm_rope(t, gain, e_ref, cc, s_up, s_dn, out_scale):
    w = t.shape[1]
    ss = _head_sumsq(t, e_ref)
    tn = t * lax.rsqrt(ss * (1.0 / HEAD_DIM) + EPS) * gain
    reps = w // LANES
    up = pltpu.roll(tn, w - ROT_DIM // 2, axis=1)
    dn = pltpu.roll(tn, ROT_DIM // 2, axis=1)
    c = jnp.concatenate([cc] * reps, axis=1)
    su = jnp.concatenate([s_up] * reps, axis=1)
    sd = jnp.concatenate([s_dn] * reps, axis=1)
    return (tn * c + up * su + dn * sd) * out_scale


def _swap_head_pairs(t):
    w = t.shape[1]
    lane = lax.broadcasted_iota(jnp.int32, t.shape, 1)
    from_up = pltpu.roll(t, w - HEAD_DIM, axis=1)
    from_dn = pltpu.roll(t, HEAD_DIM, axis=1)
    return jnp.where((lane & (LANES - 1)) < HEAD_DIM, from_up, from_dn)


def _inproj_kernel(x_ref, g_ref, wq_ref, wk_ref, wv_ref, wu_ref, qn_ref, kn_ref,
                   cc_ref, su_ref, sd_ref, e_ref, wpool_ref, pscale_ref,
                   q_out, k_out, v_out, ksw_out, vsw_out, y_out, ubuf, *, tiles_per_seq):
    i = pl.program_id(0)
    tm = x_ref.shape[0]

    @pl.when(i % tiles_per_seq == 0)
    def _():
        ubuf[0:HALO, :] = jnp.zeros((HALO, POOL_WIDTH), F32)

    for r0 in range(0, tm, SUB_TILE):
        rows = slice(r0, r0 + SUB_TILE)
        x = x_ref[rows, :]
        ms = jnp.mean(x * x, axis=-1, keepdims=True)
        hn = (x * lax.rsqrt(ms + EPS) * g_ref[...]).astype(BF16)

        cc, s_up, s_dn = cc_ref[rows, :], su_ref[rows, :], sd_ref[rows, :]
        q = jnp.dot(hn, wq_ref[...], preferred_element_type=F32)
        q_out[rows, :] = _norm_rope(q, qn_ref[...], e_ref, cc, s_up, s_dn,
                                    HEAD_DIM ** -0.5).astype(BF16)
        k = jnp.dot(hn, wk_ref[...], preferred_element_type=F32)
        k = _norm_rope(k, kn_ref[...], e_ref, cc, s_up, s_dn, 1.0)
        v = jnp.dot(hn, wv_ref[...], preferred_element_type=F32)
        k_out[rows, :] = k.astype(BF16)
        v_out[rows, :] = v.astype(BF16)
        ksw_out[rows, :] = _swap_head_pairs(k).astype(BF16)
        vsw_out[rows, :] = _swap_head_pairs(v).astype(BF16)

        base = HALO + r0
        ubuf[base:base + SUB_TILE, :] = jnp.dot(hn, wu_ref[...], preferred_element_type=F32)
        pos = (i % tiles_per_seq) * tm + r0 + lax.broadcasted_iota(
            jnp.int32, (SUB_TILE, POOL_GROUP_DIM), 0)
        for g, w in enumerate(POOL_WINDOWS):
            cols = slice(g * POOL_GROUP_DIM, (g + 1) * POOL_GROUP_DIM)
            assert w & (w - 1) == 0 and w - 1 < HALO
            acc = ubuf[base - HALO:base + SUB_TILE, cols]
            shift = 1
            while shift < w:
                acc = acc + pltpu.roll(acc, shift, axis=0)
                shift *= 2
            acc = acc[HALO:, :]
            u_g = ubuf[base:base + SUB_TILE, cols]
            cnt = jnp.minimum(pos + 1, w).astype(F32)
            d = (acc / cnt - u_g).astype(BF16)
            y = jnp.dot(d, wpool_ref[g], preferred_element_type=F32)
            y_out[rows, cols] = (y * pscale_ref[:, cols]).astype(BF16)
    ubuf[0:HALO, :] = ubuf[tm:tm + HALO, :]


def _attn_kernel(sinks_ref, q_ref, kc_ref, kp_ref, kswc_ref, kswp_ref,
                 vc_ref, vp_ref, vswc_ref, vswp_ref, o_ref):
    j = pl.program_id(1)
    two = 2 * BLOCK
    row = lax.broadcasted_iota(jnp.int32, (two, two), 0) & (BLOCK - 1)
    col = lax.broadcasted_iota(jnp.int32, (two, two), 1)
    diff = row + BLOCK - col
    local = (diff >= 0) & (diff < WINDOW)
    lo_lanes = lax.broadcasted_iota(jnp.int32, (two, LANES), 1) < HEAD_DIM
    top_rows = lax.broadcasted_iota(jnp.int32, (two, 1), 0) < BLOCK
    zero = jnp.zeros((two, LANES), BF16)

    for sb in range(q_ref.shape[0] // BLOCK):
        rows = slice(sb * BLOCK, (sb + 1) * BLOCK)
        if sb == 0:
            def band(cur_ref, prev_ref):
                return jnp.concatenate([prev_ref[...], cur_ref[0:BLOCK, :]], axis=0)
            valid = local & ((col >= BLOCK) | (j > 0))
        else:
            def band(cur_ref, prev_ref, sb=sb):
                return cur_ref[(sb - 1) * BLOCK:(sb + 1) * BLOCK, :]
            valid = local
        k_nat, k_swp = band(kc_ref, kp_ref), band(kswc_ref, kswp_ref)
        v_nat, v_swp = band(vc_ref, vp_ref), band(vswc_ref, vswp_ref)

        for g in range(N_KV_HEADS):
            c0 = (g // 2) * LANES
            kcols = (k_nat[:, c0:c0 + LANES], k_swp[:, c0:c0 + LANES])
            vcols = (v_nat[:, c0:c0 + LANES], v_swp[:, c0:c0 + LANES])
            in_lo, in_hi = (0, 1) if g % 2 == 0 else (1, 0)
            q0 = g * GROUP * HEAD_DIM
            qq = jnp.concatenate([q_ref[rows, q0:q0 + LANES],
                                  q_ref[rows, q0 + LANES:q0 + 2 * LANES]], axis=0)
            halves = []
            for half, src in ((0, in_lo), (1, in_hi)):
                qm = jnp.where(lo_lanes, qq, zero) if half == 0 else jnp.where(lo_lanes, zero, qq)
                s = lax.dot_general(qm, kcols[src], (((1,), (1,)), ((), ())),
                                    preferred_element_type=F32)
                s = jnp.where(valid, s, NEG_BIG)
                h_top = g * GROUP + half
                sink = jnp.where(top_rows, sinks_ref[h_top], sinks_ref[h_top + 2])
                m = jnp.maximum(jnp.max(s, axis=-1, keepdims=True), sink)
                p = jnp.exp(s - m)
                denom = jnp.sum(p, axis=-1, keepdims=True) + jnp.exp(sink - m)
                o = jnp.dot(p.astype(BF16), vcols[src], preferred_element_type=F32)
                halves.append(o / denom)
            o_pair = jnp.where(lo_lanes, halves[0], halves[1]).astype(BF16)
            o_ref[rows, q0:q0 + LANES] = o_pair[0:BLOCK, :]
            o_ref[rows, q0 + LANES:q0 + 2 * LANES] = o_pair[BLOCK:, :]


def _outproj_router_kernel(o_ref, y_ref, x_ref, woa_ref, wop_ref, g_ref, wr_ref,
                           br_ref, tri_ref, h_out, hpk_out, route_out, cnt_out):
    n_sub = x_ref.shape[0] // SUB_TILE
    picks = [_outproj_router_subtile(sub, o_ref, y_ref, x_ref, woa_ref, wop_ref, g_ref,
                                     wr_ref, br_ref, tri_ref, h_out, hpk_out)
             for sub in range(n_sub)]

    lane = lax.broadcasted_iota(jnp.int32, (SUB_TILE, LANES), 1)
    lane_f = lane.astype(F32)
    cnt_tile = sum(pk[5] for pk in picks)
    cnt_pad = jnp.floor((cnt_tile + (CHUNK - 1)) * (1.0 / CHUNK)) * CHUNK
    run_start = lax.dot_general(
        jnp.broadcast_to(cnt_pad, (SUBLANES, LANES)).astype(BF16),
        tri_ref[0:LANES, 0:LANES], (((1,), (1,)), ((), ())),
        preferred_element_type=F32)[0:1, :]
    before = jnp.zeros((1, LANES), F32)
    for sub, (e1, e2, g1, g2, prefix, cnt_sub) in enumerate(picks):
        table = prefix + (run_start + before)
        lp1 = jnp.sum(jnp.where(lane_f == e1, table, 0.0), axis=-1, keepdims=True)
        lp2 = jnp.sum(jnp.where(lane_f == e2, table, 0.0), axis=-1, keepdims=True)
        cols = jnp.where(lane == 0, g1,
                         jnp.where(lane == 1, g2,
                                   jnp.where(lane == 2, lp1,
                                             jnp.where(lane == 3, lp2, 0.0))))
        pick = (lax.broadcasted_iota(jnp.int32, (SUBLANES, LANES), 0)
                == lax.broadcasted_iota(jnp.int32, (SUBLANES, LANES), 1)).astype(F32)
        route_out[:, sub * SUB_TILE:(sub + 1) * SUB_TILE] = lax.dot_general(
            pick, cols, (((1,), (1,)), ((), ())), precision=lax.Precision.HIGHEST,
            preferred_element_type=F32)
        before = before + cnt_sub
    cnt_out[...] = jnp.broadcast_to(cnt_tile, cnt_out.shape)


def _outproj_router_subtile(sub, o_ref, y_ref, x_ref, woa_ref, wop_ref, g_ref, wr_ref,
                            br_ref, tri_ref, h_out, hpk_out):
    tm = SUB_TILE
    r0 = sub * SUB_TILE
    rows = slice(r0, r0 + SUB_TILE)
    h = (x_ref[rows, :]
         + jnp.dot(o_ref[rows, :], woa_ref[...], preferred_element_type=F32)
         + jnp.dot(y_ref[rows, :], wop_ref[...], preferred_element_type=F32))
    h_out[rows, :] = h
    ms = jnp.mean(h * h, axis=-1, keepdims=True)
    hn = h * lax.rsqrt(ms + EPS) * g_ref[...]
    hb = hn.astype(BF16)
    packed = _pack_halves(hn)
    for s in range(PK_ROWS):
        hpk_out[pl.ds(r0 * PK_ROWS + s, tm, stride=PK_ROWS), :] = (
            packed[:, s * LANES:(s + 1) * LANES])

    both = jnp.dot(hb, wr_ref[...], preferred_element_type=F32)
    logits = both[:, :LANES] + both[:, LANES:] + br_ref[...]
    lane = lax.broadcasted_iota(jnp.int32, (tm, LANES), 1)
    lane_f = lane.astype(F32)

    def first_lane_of(mask):
        return jnp.min(jnp.where(mask, lane_f, float(LANES)), axis=-1, keepdims=True)

    def row_max(t):
        return jnp.max(t, axis=-1, keepdims=True)

    coarse = jnp.where(lane < N_EXPERT_GROUPS, logits, NEG_BIG)
    cmax = row_max(coarse)
    grp = first_lane_of(coarse == cmax)
    csum = jnp.sum(jnp.exp(coarse - cmax), axis=-1, keepdims=True)
    flo = N_EXPERT_GROUPS + EXPERTS_PER_GROUP * grp
    fine = jnp.where((lane_f >= flo) & (lane_f < flo + EXPERTS_PER_GROUP), logits, NEG_BIG)
    f1 = row_max(fine)
    i1 = first_lane_of(fine == f1)
    rest = jnp.where(lane_f == i1, NEG_BIG, fine)
    f2 = row_max(rest)
    i2 = first_lane_of(rest == f2)
    ratio = jnp.exp(f2 - f1)
    scale = (1.0 / csum) / (1.0 + ratio)
    g1 = scale
    g2 = scale * ratio
    e1 = i1 - N_EXPERT_GROUPS
    e2 = i2 - N_EXPERT_GROUPS

    oh = jnp.where(lane_f == e1, 1.0, 0.0) + jnp.where(lane_f == e2, 1.0, 0.0)
    prefix = jnp.dot(tri_ref[...], oh.astype(BF16), preferred_element_type=F32)
    return e1, e2, g1, g2, prefix, jnp.sum(oh, axis=0, keepdims=True)


def _expert_kernel(eid_ref, first_ref, nact_ref, off_ref, next_ref, used_ref,
                   hpk_hbm, wg_hbm, wu_hbm, wd_hbm, ys_out,
                   xbuf, wg_f, wu_f, wd_f, wg_s, wu_s, wd_s, sems, wsems):
    i = pl.program_id(0)
    tm = ys_out.shape[0] // PK_ROWS
    n_act = nact_ref[0]

    def weight_copies(e):
        return (pltpu.make_async_copy(wg_hbm.at[e], wg_f, wsems.at[0]),
                pltpu.make_async_copy(wu_hbm.at[e], wu_f, wsems.at[1]),
                pltpu.make_async_copy(wd_hbm.at[e], wd_f, wsems.at[2]))

    ch_rows = CHUNK * PK_ROWS
    n_ch = tm // CHUNK

    def gather(blk, slot):
        for c in range(n_ch):
            src = pl.multiple_of(off_ref[blk * n_ch + c], ch_rows)
            pltpu.make_async_copy(hpk_hbm.at[pl.ds(src, ch_rows)],
                                  xbuf.at[slot, pl.ds(c * ch_rows, ch_rows)],
                                  sems.at[slot]).start()

    @pl.when(i == 0)
    def _():
        for cp in weight_copies(eid_ref[0]):
            cp.start()
        gather(0, 0)

    @pl.when(i + 1 < n_act)
    def _():
        gather(i + 1, (i + 1) % 2)

    @pl.when(i < n_act)
    def _():
        slot = i % 2

        @pl.when(first_ref[i] == 1)
        def _():
            for cp in weight_copies(eid_ref[i]):
                cp.wait()
            wg_s[...] = wg_f[...].astype(BF16)
            wu_s[...] = wu_f[...].astype(BF16)
            wd_s[...] = wd_f[...].astype(BF16)

            @pl.when(next_ref[i] >= 0)
            def _():
                for cp in weight_copies(next_ref[i]):
                    cp.start()

        pltpu.make_async_copy(hpk_hbm.at[pl.ds(0, tm * PK_ROWS)], xbuf.at[slot],
                              sems.at[slot]).wait()

        def swiglu(rows):
            xu = jnp.concatenate(
                [xbuf[slot, pl.ds(s, rows, stride=PK_ROWS), :] for s in range(PK_ROWS)],
                axis=1)
            x_lo, x_hi = (v.astype(BF16) for v in _unpack_halves(xu))
            a = (jnp.dot(x_lo, wg_s[0:HALF, :], preferred_element_type=F32)
                 + jnp.dot(x_hi, wg_s[HALF:, :], preferred_element_type=F32))
            b = (jnp.dot(x_lo, wu_s[0:HALF, :], preferred_element_type=F32)
                 + jnp.dot(x_hi, wu_s[HALF:, :], preferred_element_type=F32))
            mid = (a * jax.nn.sigmoid(a) * b).astype(BF16)
            y = jnp.dot(mid, wd_s[...], preferred_element_type=F32)
            yw = _pack_halves(y)
            for s in range(PK_ROWS):
                ys_out[pl.ds(s, rows, stride=PK_ROWS), :] = yw[:, s * LANES:(s + 1) * LANES]
            if rows < tm:
                ys_out[rows * PK_ROWS:, :] = jnp.zeros(((tm - rows) * PK_ROWS, LANES),
                                                       ys_out.dtype)

        quarter = tm // 4
        for rows in range(quarter, tm + 1, quarter):
            @pl.when((used_ref[i] > rows - quarter) & (used_ref[i] <= rows))
            def _(rows=rows):
                swiglu(rows)

    @pl.when(i >= n_act)
    def _():
        ys_out[...] = jnp.zeros(ys_out.shape, ys_out.dtype)


def _local_sort_kernel(lp_ref, hpk_ref, xs_out):
    i = pl.program_id(0)
    t_tile = hpk_ref.shape[0] // PK_ROWS
    n_tok = pl.num_programs(0) * t_tile
    xs_out[...] = jnp.zeros(xs_out.shape, xs_out.dtype)

    def body(q, carry):
        for u in range(SUBLANES):
            t = q * SUBLANES + u
            row = hpk_ref[pl.ds(pl.multiple_of(t * PK_ROWS, PK_ROWS), PK_ROWS), :]
            for k in range(TOP_K):
                dst = pl.multiple_of(lp_ref[k * n_tok + i * t_tile + t], PK_ROWS)
                xs_out[pl.ds(dst, PK_ROWS), :] = row
        return carry
    lax.fori_loop(0, t_tile // SUBLANES, body, 0)


def _combine_kernel(gch_ref, lp_ref, gate_ref, h_ref, ys_hbm, out_ref, ylocal, acc, sems):
    i = pl.program_id(0)
    n = pl.num_programs(0)
    t_tile = h_ref.shape[0]
    n_tok = n * t_tile
    ch_rows = CHUNK * PK_ROWS
    n_ch = ylocal.shape[1] // ch_rows

    def chunk_copy(tile, slot, lc):
        src = pl.multiple_of(gch_ref[tile * n_ch + lc], ch_rows)
        dst = pl.multiple_of(lc * ch_rows, ch_rows)
        pltpu.make_async_copy(ys_hbm.at[pl.ds(src, ch_rows)],
                              ylocal.at[slot, pl.ds(dst, ch_rows)],
                              sems.at[slot]).start()

    def wait_slot(slot):
        pltpu.make_async_copy(ys_hbm.at[pl.ds(0, n_ch * ch_rows)], ylocal.at[slot],
                              sems.at[slot]).wait()

    @pl.when(i == 0)
    def _():
        def body(lc, carry):
            chunk_copy(0, 0, lc)
            return carry
        lax.fori_loop(0, n_ch, body, 0)

    slot = i % 2
    wait_slot(slot)

    nxt = jnp.minimum(i + 1, n - 1)
    tok_unroll = 4
    n_groups = t_tile // tok_unroll
    per_group = -(-n_ch // n_groups)
    issue_groups = n_ch // per_group
    assert issue_groups * per_group == n_ch and issue_groups <= n_groups

    def tok_body(q, carry, issue):
        if issue:
            for u in range(per_group):
                chunk_copy(nxt, 1 - slot, q * per_group + u)
        for u in range(tok_unroll):
            t = q * tok_unroll + u
            a0 = i * t_tile + t
            a1 = n_tok + a0
            p0 = pl.multiple_of(lp_ref[a0], PK_ROWS)
            p1 = pl.multiple_of(lp_ref[a1], PK_ROWS)
            lo0, hi0 = _unpack_halves(ylocal[slot, pl.ds(p0, PK_ROWS), :])
            lo1, hi1 = _unpack_halves(ylocal[slot, pl.ds(p1, PK_ROWS), :])
            g0 = gate_ref[a0]
            g1 = gate_ref[a1]
            dst = pl.multiple_of(t * PK_ROWS, PK_ROWS)
            acc[0, pl.ds(dst, PK_ROWS), :] = g0 * lo0 + g1 * lo1
            acc[1, pl.ds(dst, PK_ROWS), :] = g0 * hi0 + g1 * hi1
        return carry
    lax.fori_loop(0, issue_groups, functools.partial(tok_body, issue=True), 0)
    lax.fori_loop(issue_groups, n_groups, functools.partial(tok_body, issue=False), 0)

    @pl.when(i == n - 1)
    def _():
        wait_slot(1 - slot)

    for half in range(2):
        for s in range(PK_ROWS):
            c0 = half * HALF + s * LANES
            out_ref[:, c0:c0 + LANES] = (h_ref[:, c0:c0 + LANES]
                                         + acc[half, pl.ds(s, t_tile, stride=PK_ROWS), :])


def _rope_tables(seq):
    pos = jnp.arange(seq, dtype=F32)
    inv_freq = ROPE_THETA ** (-jnp.arange(0, ROT_DIM, 2, dtype=F32) / ROT_DIM)
    ang = pos[:, None] * inv_freq[None, :]
    cos, sin = jnp.cos(ang), jnp.sin(ang)
    half = ROT_DIM // 2
    ones = jnp.ones((seq, HEAD_DIM - ROT_DIM), F32)
    zeros = jnp.zeros((seq, HEAD_DIM - ROT_DIM), F32)
    zh = jnp.zeros((seq, half), F32)
    cc = jnp.concatenate([cos, cos, ones], axis=1)
    s_up = jnp.concatenate([-sin, zh, zeros], axis=1)
    s_dn = jnp.concatenate([zh, sin, zeros], axis=1)
    rep = LANES // HEAD_DIM
    return tuple(jnp.tile(t, (1, rep)) for t in (cc, s_up, s_dn))


def _plan(cnt, n_tokens):
    nt = n_tokens // LOCAL_TILE
    experts = jnp.arange(N_EXPERTS, dtype=jnp.int32)
    cntp = ((cnt + CHUNK - 1) // CHUNK) * CHUNK
    loff_end = jnp.cumsum(cntp, axis=1)
    loff = loff_end - cntp
    rows_e = jnp.sum(cntp, axis=0)
    rows_pad = ((rows_e + EXPERT_TILE - 1) // EXPERT_TILE) * EXPERT_TILE
    g_end = jnp.cumsum(rows_pad)
    g_start = g_end - rows_pad
    gpos = g_start[None, :] + jnp.cumsum(cntp, axis=0) - cntp

    lrow = jnp.arange(LT_MAX // CHUNK, dtype=jnp.int32) * CHUNK
    e_lc = jnp.sum((lrow[None, :, None] >= loff_end[:, None, :]).astype(jnp.int32), axis=-1)
    sel = e_lc[:, :, None] == experts
    delta = jnp.sum(jnp.where(sel, (gpos - loff)[:, None, :], 0), axis=-1)
    gch = jnp.where(e_lc < N_EXPERTS, (delta + lrow[None, :]) // CHUNK, 0).reshape(-1)

    n_rows = TOP_K * n_tokens + nt * N_EXPERTS * (CHUNK - 1) + N_EXPERTS * (EXPERT_TILE - CHUNK)
    n_blocks = -(-n_rows // EXPERT_TILE)
    n_rows = n_blocks * EXPERT_TILE
    blk_start = jnp.arange(n_blocks, dtype=jnp.int32) * EXPERT_TILE
    blk_eid = jnp.minimum(
        jnp.sum((blk_start[:, None] >= g_end[None, :]).astype(jnp.int32), axis=-1),
        N_EXPERTS - 1)
    prev = jnp.concatenate([jnp.full((1,), -1, jnp.int32), blk_eid[:-1]])
    first = (blk_eid != prev).astype(jnp.int32)
    n_act = (g_end[-1] // EXPERT_TILE).astype(jnp.int32).reshape(1)
    later = (blk_eid[None, :] > blk_eid[:, None]) & (jnp.arange(n_blocks)[None, :] < n_act)
    next_eid = jnp.min(jnp.where(later, blk_eid[None, :], N_EXPERTS), axis=1)
    next_eid = jnp.where(next_eid < N_EXPERTS, next_eid, -1).astype(jnp.int32)
    of_blk = blk_eid[:, None] == experts
    blk_end = jnp.sum(jnp.where(of_blk, (g_start + rows_e)[None, :], 0), axis=1)
    used = jnp.clip(blk_end - blk_start, 0, EXPERT_TILE).astype(jnp.int32)

    cpb = EXPERT_TILE // CHUNK
    run_start = gpos.T // CHUNK
    run_len = cntp.T // CHUNK
    run_src = (jnp.arange(nt, dtype=jnp.int32)[:, None] * (LT_MAX // CHUNK) + loff // CHUNK).T
    tabs = jnp.stack([run_start, run_len, run_src], axis=0)
    dtabs = tabs - jnp.concatenate(
        [jnp.zeros((3, N_EXPERTS, 1), jnp.int32), tabs[:, :, :-1]], axis=2)
    blk_sel = (blk_eid[:, None] == experts)[:, :, None]
    blk_start_t = jnp.sum(jnp.where(blk_sel, run_start[None], 0), axis=1)
    blk_dtabs = jnp.sum(jnp.where(blk_sel[None], dtabs[:, None], 0), axis=2)
    gc = jnp.arange(n_rows // CHUNK, dtype=jnp.int32).reshape(n_blocks, cpb)
    started = blk_start_t[:, None, :] <= gc[:, :, None]
    picked = jnp.sum(jnp.where(started[None], blk_dtabs[:, :, None, :], 0), axis=-1)
    within = gc - picked[0]
    csrc = jnp.where(within < picked[1], picked[2] + within, 0).reshape(-1)
    return (gch.astype(jnp.int32), csrc.astype(jnp.int32), blk_eid, first, next_eid, used,
            n_act, n_rows)


def kernel(x, norm_mix, w_in, q_norm, k_norm, sinks, w_pool, pool_scale, w_out,
           norm_ffn, w_coarse, b_coarse, w_fine, b_fine, w_gate, w_up, w_down):
    bsz, seq, d = x.shape
    n = bsz * seq
    assert d == D_MODEL and seq % ROW_TILE == 0 and seq % BLOCK == 0
    assert norm_mix.shape[0] == 1, "single-layer problem"
    xf = x.reshape(n, d)

    w_in_b = w_in[0].astype(BF16)
    wq = w_in_b[:, :ATTN_WIDTH]
    wk = w_in_b[:, ATTN_WIDTH:ATTN_WIDTH + KV_WIDTH]
    wv = w_in_b[:, ATTN_WIDTH + KV_WIDTH:ATTN_WIDTH + 2 * KV_WIDTH]
    wu = w_in_b[:, ATTN_WIDTH + 2 * KV_WIDTH:]
    qn = jnp.tile(q_norm[0], N_HEADS).reshape(1, ATTN_WIDTH)
    kn = jnp.tile(k_norm[0], N_KV_HEADS).reshape(1, KV_WIDTH)
    cc, s_up, s_dn = _rope_tables(seq)
    lane_head = jnp.arange(MXU_DIM) // HEAD_DIM
    e_mat = (lane_head[:, None] == lane_head[None, :]).astype(BF16)
    w_pool_b = w_pool[0].astype(BF16)
    pscale = pool_scale[0].reshape(1, POOL_WIDTH)
    w_out_b = w_out[0].astype(BF16)
    wo_attn, wo_pool = w_out_b[:ATTN_WIDTH], w_out_b[ATTN_WIDTH:]
    n_router = N_EXPERT_GROUPS + N_EXPERTS
    w_r = jnp.concatenate([w_coarse[0], w_fine[0]], axis=1)
    w_r = jnp.pad(w_r, ((0, 0), (0, LANES - n_router)))
    w_r_hi = w_r.astype(BF16)
    w_r_lo = (w_r - w_r_hi.astype(F32)).astype(BF16)
    w_r2 = jnp.concatenate([w_r_hi, w_r_lo], axis=1)
    b_r = jnp.pad(jnp.concatenate([b_coarse[0], b_fine[0]]),
                  (0, LANES - n_router)).reshape(1, LANES)

    tm = ROW_TILE
    tiles_per_seq = seq // tm
    n_tiles = n // tm
    idx = jnp.arange(SUB_TILE)
    tri = (idx[None, :] < idx[:, None]).astype(BF16)
    const = lambda *_: (0, 0)
    row_blk = lambda i: (i, 0)

    q, k, v, ksw, vsw, y = pl.pallas_call(
        functools.partial(_inproj_kernel, tiles_per_seq=tiles_per_seq),
        grid=(n_tiles,),
        in_specs=[
            pl.BlockSpec((tm, d), row_blk),
            pl.BlockSpec((1, d), const),
            pl.BlockSpec((d, ATTN_WIDTH), const),
            pl.BlockSpec((d, KV_WIDTH), const),
            pl.BlockSpec((d, KV_WIDTH), const),
            pl.BlockSpec((d, POOL_WIDTH), const),
            pl.BlockSpec((1, ATTN_WIDTH), const),
            pl.BlockSpec((1, KV_WIDTH), const),
            pl.BlockSpec((tm, LANES), lambda i: (i % tiles_per_seq, 0)),
            pl.BlockSpec((tm, LANES), lambda i: (i % tiles_per_seq, 0)),
            pl.BlockSpec((tm, LANES), lambda i: (i % tiles_per_seq, 0)),
            pl.BlockSpec((MXU_DIM, MXU_DIM), const),
            pl.BlockSpec((len(POOL_WINDOWS), POOL_GROUP_DIM, POOL_GROUP_DIM),
                         lambda i: (0, 0, 0)),
            pl.BlockSpec((1, POOL_WIDTH), const),
        ],
        out_specs=[
            pl.BlockSpec((tm, ATTN_WIDTH), row_blk),
            pl.BlockSpec((tm, KV_WIDTH), row_blk),
            pl.BlockSpec((tm, KV_WIDTH), row_blk),
            pl.BlockSpec((tm, KV_WIDTH), row_blk),
            pl.BlockSpec((tm, KV_WIDTH), row_blk),
            pl.BlockSpec((tm, POOL_WIDTH), row_blk),
        ],
        out_shape=[
            jax.ShapeDtypeStruct((n, ATTN_WIDTH), BF16),
            jax.ShapeDtypeStruct((n, KV_WIDTH), BF16),
            jax.ShapeDtypeStruct((n, KV_WIDTH), BF16),
            jax.ShapeDtypeStruct((n, KV_WIDTH), BF16),
            jax.ShapeDtypeStruct((n, KV_WIDTH), BF16),
            jax.ShapeDtypeStruct((n, POOL_WIDTH), BF16),
        ],
        scratch_shapes=[pltpu.VMEM((HALO + tm, POOL_WIDTH), F32)],
        compiler_params=_cparams(),
        name="inproj",
    )(xf, norm_mix[0].reshape(1, d), wq, wk, wv, wu, qn, kn, cc, s_up, s_dn,
      e_mat, w_pool_b, pscale)

    nb = seq // BLOCK
    nq = nb // Q_BLOCKS
    qrows = Q_BLOCKS * BLOCK
    cur = lambda b, j, *_: (b * nq + j, 0)
    prv = lambda b, j, *_: (b * nb + jnp.maximum(Q_BLOCKS * j - 1, 0), 0)
    kv_specs = [pl.BlockSpec((qrows, KV_WIDTH), cur), pl.BlockSpec((BLOCK, KV_WIDTH), prv)]
    attn = pl.pallas_call(
        _attn_kernel,
        grid_spec=pltpu.PrefetchScalarGridSpec(
            num_scalar_prefetch=1,
            grid=(bsz, nq),
            in_specs=[pl.BlockSpec((qrows, ATTN_WIDTH), cur)] + kv_specs * 4,
            out_specs=pl.BlockSpec((qrows, ATTN_WIDTH), cur),
        ),
        out_shape=jax.ShapeDtypeStruct((n, ATTN_WIDTH), BF16),
        compiler_params=_cparams(2),
        name="swa_attn",
    )(sinks[0].astype(F32), q, k, k, ksw, ksw, v, v, vsw, vsw)

    h, hpk, route, cnt = pl.pallas_call(
        _outproj_router_kernel,
        grid=(n_tiles,),
        in_specs=[
            pl.BlockSpec((tm, ATTN_WIDTH), row_blk),
            pl.BlockSpec((tm, POOL_WIDTH), row_blk),
            pl.BlockSpec((tm, d), row_blk),
            pl.BlockSpec((ATTN_WIDTH, d), const),
            pl.BlockSpec((POOL_WIDTH, d), const),
            pl.BlockSpec((1, d), const),
            pl.BlockSpec((d, 2 * LANES), const),
            pl.BlockSpec((1, LANES), const),
            pl.BlockSpec((SUB_TILE, SUB_TILE), const),
        ],
        out_specs=[
            pl.BlockSpec((tm, d), row_blk),
            pl.BlockSpec((tm * PK_ROWS, LANES), row_blk),
            pl.BlockSpec((SUBLANES, tm), lambda i: (0, i)),
            pl.BlockSpec((SUBLANES, LANES), row_blk),
        ],
        out_shape=[
            jax.ShapeDtypeStruct((n, d), F32),
            jax.ShapeDtypeStruct((n * PK_ROWS, LANES), jnp.int32),
            jax.ShapeDtypeStruct((SUBLANES, n), F32),
            jax.ShapeDtypeStruct((n_tiles * SUBLANES, LANES), F32),
        ],
        compiler_params=_cparams(),
        name="outproj_router",
    )(attn, y, xf, wo_attn, wo_pool, norm_ffn[0].reshape(1, d), w_r2, b_r, tri)

    assert tm == LOCAL_TILE
    gates = route[0:TOP_K].reshape(TOP_K * n)
    lp = route[TOP_K:2 * TOP_K].astype(jnp.int32).reshape(TOP_K * n)
    tile_cnt = cnt.reshape(n_tiles, SUBLANES, LANES)[:, 0, :N_EXPERTS].astype(jnp.int32)
    te = EXPERT_TILE
    gch, csrc, blk_eid, first, next_eid, used, n_act, n_rows = _plan(tile_cnt, n)
    n_blocks = n_rows // te
    n_local = n // LOCAL_TILE

    xs = pl.pallas_call(
        _local_sort_kernel,
        grid_spec=pltpu.PrefetchScalarGridSpec(
            num_scalar_prefetch=1,
            grid=(n_local,),
            in_specs=[pl.BlockSpec((LOCAL_TILE * PK_ROWS, LANES), lambda i, *_: (i, 0))],
            out_specs=pl.BlockSpec((LT_MAX * PK_ROWS, LANES), lambda i, *_: (i, 0)),
        ),
        out_shape=jax.ShapeDtypeStruct((n_local * LT_MAX * PK_ROWS, LANES), jnp.int32),
        compiler_params=_cparams(),
        name="local_sort",
    )(lp * PK_ROWS, hpk)

    ys = pl.pallas_call(
        _expert_kernel,
        grid_spec=pltpu.PrefetchScalarGridSpec(
            num_scalar_prefetch=6,
            grid=(n_blocks,),
            in_specs=[pl.BlockSpec(memory_space=pl.ANY)] * 4,
            out_specs=pl.BlockSpec((te * PK_ROWS, LANES), lambda i, *_: (i, 0)),
            scratch_shapes=[
                pltpu.VMEM((2, te * PK_ROWS, LANES), jnp.int32),
                pltpu.VMEM((d, D_EXPERT), F32),
                pltpu.VMEM((d, D_EXPERT), F32),
                pltpu.VMEM((D_EXPERT, d), F32),
                pltpu.VMEM((d, D_EXPERT), BF16),
                pltpu.VMEM((d, D_EXPERT), BF16),
                pltpu.VMEM((D_EXPERT, d), BF16),
                pltpu.SemaphoreType.DMA((2,)),
                pltpu.SemaphoreType.DMA((3,)),
            ],
        ),
        out_shape=jax.ShapeDtypeStruct((n_rows * PK_ROWS, LANES), jnp.int32),
        compiler_params=_cparams(),
        name="experts",
    )(blk_eid, first, n_act, csrc * (CHUNK * PK_ROWS), next_eid, used, xs,
      w_gate[0], w_up[0], w_down[0])

    tt = LOCAL_TILE
    out = pl.pallas_call(
        _combine_kernel,
        grid_spec=pltpu.PrefetchScalarGridSpec(
            num_scalar_prefetch=3,
            grid=(n_local,),
            in_specs=[
                pl.BlockSpec((tt, d), lambda i, *_: (i, 0)),
                pl.BlockSpec(memory_space=pl.ANY),
            ],
            out_specs=pl.BlockSpec((tt, d), lambda i, *_: (i, 0)),
            scratch_shapes=[
                pltpu.VMEM((2, LT_MAX * PK_ROWS, LANES), jnp.int32),
                pltpu.VMEM((2, tt * PK_ROWS, LANES), F32),
                pltpu.SemaphoreType.DMA((2,)),
            ],
        ),
        out_shape=jax.ShapeDtypeStruct((n, d), F32),
        compiler_params=_cparams(),
        name="combine",
    )(gch * (CHUNK * PK_ROWS), lp * PK_ROWS, gates, h, ys)
    return out.reshape(bsz, seq, d)
```

```python
import functools

import jax
import jax.numpy as jnp
from jax import lax
from jax.experimental import pallas as pl
from jax.experimental.pallas import tpu as pltpu

F32 = jnp.float32
BF16 = jnp.bfloat16

D_MODEL = 2048
N_HEADS = 16
N_KV_HEADS = 4
HEAD_DIM = 64
GROUP = N_HEADS // N_KV_HEADS
ROT_DIM = HEAD_DIM // 4
ROPE_THETA = 500000.0
WINDOW = 128
BLOCK = 128
Q_BLOCKS = 4
ATTN_WIDTH = N_HEADS * HEAD_DIM
KV_WIDTH = N_KV_HEADS * HEAD_DIM
POOL_WINDOWS = (2, 4, 8, 16)
POOL_WIDTH = D_MODEL // 2
POOL_GROUP_DIM = POOL_WIDTH // len(POOL_WINDOWS)
N_EXPERT_GROUPS = 4
EXPERTS_PER_GROUP = 8
N_EXPERTS = N_EXPERT_GROUPS * EXPERTS_PER_GROUP
TOP_K = 2
D_EXPERT = 512
EPS = 1e-6

LANES = 128
SUBLANES = 8
MXU_DIM = 256
HALO = 16
NEG_BIG = -1e30
HALF = D_MODEL // 2
PK_ROWS = HALF // LANES
Y_ROWS = D_MODEL // LANES

ROW_TILE = 512
SUB_TILE = 512
EXPERT_TILE = 512
LOCAL_TILE = 512
CHUNK = 8
LT_MAX = TOP_K * LOCAL_TILE + N_EXPERTS * (CHUNK - 1)
VMEM_LIMIT = 56 * 1024 * 1024


def _cparams(n_axes=1):
    return pltpu.CompilerParams(
        dimension_semantics=("arbitrary",) * n_axes,
        vmem_limit_bytes=VMEM_LIMIT,
    )


def _pack_halves(t):
    return pltpu.pack_elementwise([t[:, :HALF], t[:, HALF:]], packed_dtype=BF16)


def _unpack_halves(w):
    return tuple(pltpu.unpack_elementwise(w, index=i, packed_dtype=BF16, unpacked_dtype=F32)
                 for i in range(2))


def _head_sumsq(t, e_ref):
    t2 = (t * t).astype(BF16)
    e = e_ref[...]
    outs = []
    for c in range(t.shape[1] // MXU_DIM):
        sl = slice(c * MXU_DIM, (c + 1) * MXU_DIM)
        outs.append(jnp.dot(t2[:, sl], e, preferred_element_type=F32))
    return outs[0] if len(outs) == 1 else jnp.concatenate(outs, axis=1)


def _norm_rope(t, gain, e_ref, cc, s_up, s_dn, out_scale):
    w = t.shape[1]
    ss = _head_sumsq(t, e_ref)
    tn = t * lax.rsqrt(ss * (1.0 / HEAD_DIM) + EPS) * gain
    reps = w // LANES
    up = pltpu.roll(tn, w - ROT_DIM // 2, axis=1)
    dn = pltpu.roll(tn, ROT_DIM // 2, axis=1)
    c = jnp.concatenate([cc] * reps, axis=1)
    su = jnp.concatenate([s_up] * reps, axis=1)
    sd = jnp.concatenate([s_dn] * reps, axis=1)
    return (tn * c + up * su + dn * sd) * out_scale


def _swap_head_pairs(t):
    w = t.shape[1]
    lane = lax.broadcasted_iota(jnp.int32, t.shape, 1)
    from_up = pltpu.roll(t, w - HEAD_DIM, axis=1)
    from_dn = pltpu.roll(t, HEAD_DIM, axis=1)
    return jnp.where((lane & (LANES - 1)) < HEAD_DIM, from_up, from_dn)


def _inproj_kernel(x_ref, g_ref, wq_ref, wk_ref, wv_ref, wu_ref, qn_ref, kn_ref,
                   cc_ref, su_ref, sd_ref, e_ref, wpool_ref, pscale_ref,
                   q_out, k_out, v_out, ksw_out, vsw_out, y_out, ubuf, *, tiles_per_seq):
    i = pl.program_id(0)
    tm = x_ref.shape[0]

    @pl.when(i % tiles_per_seq == 0)
    def _():
        ubuf[0:HALO, :] = jnp.zeros((HALO, POOL_WIDTH), F32)

    for r0 in range(0, tm, SUB_TILE):
        rows = slice(r0, r0 + SUB_TILE)
        x = x_ref[rows, :]
        ms = jnp.mean(x * x, axis=-1, keepdims=True)
        hn = (x * lax.rsqrt(ms + EPS) * g_ref[...]).astype(BF16)

        cc, s_up, s_dn = cc_ref[rows, :], su_ref[rows, :], sd_ref[rows, :]
        q = jnp.dot(hn, wq_ref[...], preferred_element_type=F32)
        q_out[rows, :] = _norm_rope(q, qn_ref[...], e_ref, cc, s_up, s_dn,
                                    HEAD_DIM ** -0.5).astype(BF16)
        k = jnp.dot(hn, wk_ref[...], preferred_element_type=F32)
        k = _norm_rope(k, kn_ref[...], e_ref, cc, s_up, s_dn, 1.0)
        v = jnp.dot(hn, wv_ref[...], preferred_element_type=F32)
        k_out[rows, :] = k.astype(BF16)
        v_out[rows, :] = v.astype(BF16)
        ksw_out[rows, :] = _swap_head_pairs(k).astype(BF16)
        vsw_out[rows, :] = _swap_head_pairs(v).astype(BF16)

        base = HALO + r0
        ubuf[base:base + SUB_TILE, :] = jnp.dot(hn, wu_ref[...], preferred_element_type=F32)
        pos = (i % tiles_per_seq) * tm + r0 + lax.broadcasted_iota(
            jnp.int32, (SUB_TILE, POOL_GROUP_DIM), 0)
        for g, w in enumerate(POOL_WINDOWS):
            cols = slice(g * POOL_GROUP_DIM, (g + 1) * POOL_GROUP_DIM)
            assert w & (w - 1) == 0 and w - 1 < HALO
            acc = ubuf[base - HALO:base + SUB_TILE, cols]
            shift = 1
            while shift < w:
                acc = acc + pltpu.roll(acc, shift, axis=0)
                shift *= 2
            acc = acc[HALO:, :]
            u_g = ubuf[base:base + SUB_TILE, cols]
            cnt = jnp.minimum(pos + 1, w).astype(F32)
            d = (acc / cnt - u_g).astype(BF16)
            y = jnp.dot(d, wpool_ref[g], preferred_element_type=F32)
            y_out[rows, cols] = (y * pscale_ref[:, cols]).astype(BF16)
    ubuf[0:HALO, :] = ubuf[tm:tm + HALO, :]


def _attn_kernel(sinks_ref, q_ref, kc_ref, kp_ref, kswc_ref, kswp_ref,
                 vc_ref, vp_ref, vswc_ref, vswp_ref, o_ref):
    j = pl.program_id(1)
    two = 2 * BLOCK
    row = lax.broadcasted_iota(jnp.int32, (two, two), 0) & (BLOCK - 1)
    col = lax.broadcasted_iota(jnp.int32, (two, two), 1)
    diff = row + BLOCK - col
    local = (diff >= 0) & (diff < WINDOW)
    lo_lanes = lax.broadcasted_iota(jnp.int32, (two, LANES), 1) < HEAD_DIM
    row_p = lax.broadcasted_iota(jnp.int32, (two, BLOCK), 0)
    diag = (row_p & (BLOCK - 1)) == lax.broadcasted_iota(jnp.int32, (two, BLOCK), 1)
    top_rows = row_p < BLOCK
    diag_top = diag & top_rows
    diag_bot = diag & jnp.logical_not(top_rows)
    zero = jnp.zeros((two, LANES), BF16)

    for sb in range(q_ref.shape[0] // BLOCK):
        rows = slice(sb * BLOCK, (sb + 1) * BLOCK)
        if sb == 0:
            def band(cur_ref, prev_ref):
                return jnp.concatenate([prev_ref[...], cur_ref[0:BLOCK, :]], axis=0)
            valid = local & ((col >= BLOCK) | (j > 0))
        else:
            def band(cur_ref, prev_ref, sb=sb):
                return cur_ref[(sb - 1) * BLOCK:(sb + 1) * BLOCK, :]
            valid = local
        k_nat, k_swp = band(kc_ref, kp_ref), band(kswc_ref, kswp_ref)
        v_nat, v_swp = band(vc_ref, vp_ref), band(vswc_ref, vswp_ref)

        for g in range(N_KV_HEADS):
            c0 = (g // 2) * LANES
            kcols = (k_nat[:, c0:c0 + LANES], k_swp[:, c0:c0 + LANES])
            vcols = (v_nat[:, c0:c0 + LANES], v_swp[:, c0:c0 + LANES])
            in_lo, in_hi = (0, 1) if g % 2 == 0 else (1, 0)
            q0 = g * GROUP * HEAD_DIM
            qq = jnp.concatenate([q_ref[rows, q0:q0 + LANES],
                                  q_ref[rows, q0 + LANES:q0 + 2 * LANES]], axis=0)
            halves = []
            for half, src in ((0, in_lo), (1, in_hi)):
                qm = jnp.where(lo_lanes, qq, zero) if half == 0 else jnp.where(lo_lanes, zero, qq)
                s = lax.dot_general(qm, kcols[src], (((1,), (1,)), ((), ())),
                                    preferred_element_type=F32)
                s = jnp.where(valid, s, NEG_BIG)
                h_top = g * GROUP + half
                s_prev = jnp.where(diag_top, sinks_ref[h_top],
                                   jnp.where(diag_bot, sinks_ref[h_top + 2], s[:, :BLOCK]))
                s_cur = s[:, BLOCK:]
                m = jnp.max(jnp.maximum(s_prev, s_cur), axis=-1, keepdims=True)
                p_prev = jnp.exp(s_prev - m)
                p_cur = jnp.exp(s_cur - m)
                denom = jnp.sum(p_prev + p_cur, axis=-1, keepdims=True)
                p = jnp.concatenate([jnp.where(diag, 0.0, p_prev), p_cur], axis=1)
                o = jnp.dot(p.astype(BF16), vcols[src], preferred_element_type=F32)
                halves.append(o / denom)
            o_pair = jnp.where(lo_lanes, halves[0], halves[1]).astype(BF16)
            o_ref[rows, q0:q0 + LANES] = o_pair[0:BLOCK, :]
            o_ref[rows, q0 + LANES:q0 + 2 * LANES] = o_pair[BLOCK:, :]


def _outproj_router_kernel(o_ref, y_ref, x_ref, woa_ref, wop_ref, g_ref, wr_ref,
                           br_ref, tri_ref, h_out, hpk_out, route_out, cnt_out):
    n_sub = x_ref.shape[0] // SUB_TILE
    picks = [_outproj_router_subtile(sub, o_ref, y_ref, x_ref, woa_ref, wop_ref, g_ref,
                                     wr_ref, br_ref, tri_ref, h_out, hpk_out)
             for sub in range(n_sub)]

    lane = lax.broadcasted_iota(jnp.int32, (SUB_TILE, LANES), 1)
    lane_f = lane.astype(F32)
    cnt_tile = sum(pk[5] for pk in picks)
    cnt_pad = jnp.floor((cnt_tile + (CHUNK - 1)) * (1.0 / CHUNK)) * CHUNK
    run_start = lax.dot_general(
        jnp.broadcast_to(cnt_pad, (SUBLANES, LANES)).astype(BF16),
        tri_ref[0:LANES, 0:LANES], (((1,), (1,)), ((), ())),
        preferred_element_type=F32)[0:1, :]
    before = jnp.zeros((1, LANES), F32)
    for sub, (e1, e2, g1, g2, prefix, cnt_sub) in enumerate(picks):
        table = prefix + (run_start + before)
        lp1 = jnp.sum(jnp.where(lane_f == e1, table, 0.0), axis=-1, keepdims=True)
        lp2 = jnp.sum(jnp.where(lane_f == e2, table, 0.0), axis=-1, keepdims=True)
        cols = jnp.where(lane == 0, g1,
                         jnp.where(lane == 1, g2,
                                   jnp.where(lane == 2, lp1,
                                             jnp.where(lane == 3, lp2, 0.0))))
        pick = (lax.broadcasted_iota(jnp.int32, (SUBLANES, LANES), 0)
                == lax.broadcasted_iota(jnp.int32, (SUBLANES, LANES), 1)).astype(F32)
        route_out[:, sub * SUB_TILE:(sub + 1) * SUB_TILE] = lax.dot_general(
            pick, cols, (((1,), (1,)), ((), ())), precision=lax.Precision.HIGHEST,
            preferred_element_type=F32)
        before = before + cnt_sub
    cnt_out[...] = jnp.broadcast_to(cnt_tile, cnt_out.shape)


def _outproj_router_subtile(sub, o_ref, y_ref, x_ref, woa_ref, wop_ref, g_ref, wr_ref,
                            br_ref, tri_ref, h_out, hpk_out):
    tm = SUB_TILE
    r0 = sub * SUB_TILE
    rows = slice(r0, r0 + SUB_TILE)
    h = (x_ref[rows, :]
         + jnp.dot(o_ref[rows, :], woa_ref[...], preferred_element_type=F32)
         + jnp.dot(y_ref[rows, :], wop_ref[...], preferred_element_type=F32))
    h_out[rows, :] = h
    ms = jnp.mean(h * h, axis=-1, keepdims=True)
    hn = h * lax.rsqrt(ms + EPS) * g_ref[...]
    hb = hn.astype(BF16)
    packed = _pack_halves(hn)
    for s in range(PK_ROWS):
        hpk_out[pl.ds(r0 * PK_ROWS + s, tm, stride=PK_ROWS), :] = (
            packed[:, s * LANES:(s + 1) * LANES])

    both = jnp.dot(hb, wr_ref[...], preferred_element_type=F32)
    logits = both[:, :LANES] + both[:, LANES:] + br_ref[...]
    lane = lax.broadcasted_iota(jnp.int32, (tm, LANES), 1)
    lane_f = lane.astype(F32)

    def first_lane_of(mask):
        return jnp.min(jnp.where(mask, lane_f, float(LANES)), axis=-1, keepdims=True)

    def row_max(t):
        return jnp.max(t, axis=-1, keepdims=True)

    coarse = jnp.where(lane < N_EXPERT_GROUPS, logits, NEG_BIG)
    cmax = row_max(coarse)
    grp = first_lane_of(coarse == cmax)
    csum = jnp.sum(jnp.exp(coarse - cmax), axis=-1, keepdims=True)
    flo = N_EXPERT_GROUPS + EXPERTS_PER_GROUP * grp
    fine = jnp.where((lane_f >= flo) & (lane_f < flo + EXPERTS_PER_GROUP), logits, NEG_BIG)
    f1 = row_max(fine)
    i1 = first_lane_of(fine == f1)
    rest = jnp.where(lane_f == i1, NEG_BIG, fine)
    f2 = row_max(rest)
    i2 = first_lane_of(rest == f2)
    ratio = jnp.exp(f2 - f1)
    scale = (1.0 / csum) / (1.0 + ratio)
    g1 = scale
    g2 = scale * ratio
    e1 = i1 - N_EXPERT_GROUPS
    e2 = i2 - N_EXPERT_GROUPS

    oh = jnp.where(lane_f == e1, 1.0, 0.0) + jnp.where(lane_f == e2, 1.0, 0.0)
    prefix = jnp.dot(tri_ref[...], oh.astype(BF16), preferred_element_type=F32)
    return e1, e2, g1, g2, prefix, jnp.sum(oh, axis=0, keepdims=True)


def _expert_kernel(eid_ref, first_ref, nact_ref, off_ref, next_ref, used_ref,
                   hpk_hbm, wg_hbm, wu_hbm, wd_hbm, ys_out,
                   xbuf, wg_f, wu_f, wd_f, wg_s, wu_s, wd_s, sems, wsems):
    i = pl.program_id(0)
    tm = ys_out.shape[0] // PK_ROWS
    n_act = nact_ref[0]

    def weight_copies(e):
        return (pltpu.make_async_copy(wg_hbm.at[e], wg_f, wsems.at[0]),
                pltpu.make_async_copy(wu_hbm.at[e], wu_f, wsems.at[1]),
                pltpu.make_async_copy(wd_hbm.at[e], wd_f, wsems.at[2]))

    ch_rows = CHUNK * PK_ROWS
    n_ch = tm // CHUNK

    def gather(blk, slot):
        for c in range(n_ch):
            src = pl.multiple_of(off_ref[blk * n_ch + c], ch_rows)
            pltpu.make_async_copy(hpk_hbm.at[pl.ds(src, ch_rows)],
                                  xbuf.at[slot, pl.ds(c * ch_rows, ch_rows)],
                                  sems.at[slot]).start()

    @pl.when(i == 0)
    def _():
        for cp in weight_copies(eid_ref[0]):
            cp.start()
        gather(0, 0)

    @pl.when(i + 1 < n_act)
    def _():
        gather(i + 1, (i + 1) % 2)

    @pl.when(i < n_act)
    def _():
        slot = i % 2

        @pl.when(first_ref[i] == 1)
        def _():
            for cp in weight_copies(eid_ref[i]):
                cp.wait()
            wg_s[...] = wg_f[...].astype(BF16)
            wu_s[...] = wu_f[...].astype(BF16)
            wd_s[...] = wd_f[...].astype(BF16)

            @pl.when(next_ref[i] >= 0)
            def _():
                for cp in weight_copies(next_ref[i]):
                    cp.start()

        pltpu.make_async_copy(hpk_hbm.at[pl.ds(0, tm * PK_ROWS)], xbuf.at[slot],
                              sems.at[slot]).wait()

        def swiglu(rows):
            xu = jnp.concatenate(
                [xbuf[slot, pl.ds(s, rows, stride=PK_ROWS), :] for s in range(PK_ROWS)],
                axis=1)
            x_lo, x_hi = (v.astype(BF16) for v in _unpack_halves(xu))
            a = (jnp.dot(x_lo, wg_s[0:HALF, :], preferred_element_type=F32)
                 + jnp.dot(x_hi, wg_s[HALF:, :], preferred_element_type=F32))
            b = (jnp.dot(x_lo, wu_s[0:HALF, :], preferred_element_type=F32)
                 + jnp.dot(x_hi, wu_s[HALF:, :], preferred_element_type=F32))
            mid = (a * jax.nn.sigmoid(a) * b).astype(BF16)
            y = jnp.dot(mid, wd_s[...], preferred_element_type=F32)
            yw = _pack_halves(y)
            for s in range(PK_ROWS):
                ys_out[pl.ds(s, rows, stride=PK_ROWS), :] = yw[:, s * LANES:(s + 1) * LANES]
            if rows < tm:
                ys_out[rows * PK_ROWS:, :] = jnp.zeros(((tm - rows) * PK_ROWS, LANES),
                                                       ys_out.dtype)

        quarter = tm // 4
        for rows in range(quarter, tm + 1, quarter):
            @pl.when((used_ref[i] > rows - quarter) & (used_ref[i] <= rows))
            def _(rows=rows):
                swiglu(rows)

    @pl.when(i >= n_act)
    def _():
        ys_out[...] = jnp.zeros(ys_out.shape, ys_out.dtype)


def _local_sort_kernel(lp_ref, hpk_ref, xs_out):
    i = pl.program_id(0)
    t_tile = hpk_ref.shape[0] // PK_ROWS
    n_tok = pl.num_programs(0) * t_tile
    xs_out[...] = jnp.zeros(xs_out.shape, xs_out.dtype)

    def body(q, carry):
        for u in range(SUBLANES):
            t = q * SUBLANES + u
            row = hpk_ref[pl.ds(pl.multiple_of(t * PK_ROWS, PK_ROWS), PK_ROWS), :]
            for k in range(TOP_K):
                dst = pl.multiple_of(lp_ref[k * n_tok + i * t_tile + t], PK_ROWS)
                xs_out[pl.ds(dst, PK_ROWS), :] = row
        return carry
    lax.fori_loop(0, t_tile // SUBLANES, body, 0)


def _combine_kernel(gch_ref, lp_ref, gate_ref, h_ref, ys_hbm, out_ref, ylocal, acc, sems):
    i = pl.program_id(0)
    n = pl.num_programs(0)
    t_tile = h_ref.shape[0]
    n_tok = n * t_tile
    ch_rows = CHUNK * PK_ROWS
    n_ch = ylocal.shape[1] // ch_rows

    def chunk_copy(tile, slot, lc):
        src = pl.multiple_of(gch_ref[tile * n_ch + lc], ch_rows)
        dst = pl.multiple_of(lc * ch_rows, ch_rows)
        pltpu.make_async_copy(ys_hbm.at[pl.ds(src, ch_rows)],
                              ylocal.at[slot, pl.ds(dst, ch_rows)],
                              sems.at[slot]).start()

    def wait_slot(slot):
        pltpu.make_async_copy(ys_hbm.at[pl.ds(0, n_ch * ch_rows)], ylocal.at[slot],
                              sems.at[slot]).wait()

    @pl.when(i == 0)
    def _():
        def body(lc, carry):
            chunk_copy(0, 0, lc)
            return carry
        lax.fori_loop(0, n_ch, body, 0)

    slot = i % 2
    wait_slot(slot)

    nxt = jnp.minimum(i + 1, n - 1)
    tok_unroll = 4
    n_groups = t_tile // tok_unroll
    per_group = -(-n_ch // n_groups)
    issue_groups = n_ch // per_group
    assert issue_groups * per_group == n_ch and issue_groups <= n_groups

    def tok_body(q, carry, issue):
        if issue:
            for u in range(per_group):
                chunk_copy(nxt, 1 - slot, q * per_group + u)
        for u in range(tok_unroll):
            t = q * tok_unroll + u
            a0 = i * t_tile + t
            a1 = n_tok + a0
            p0 = pl.multiple_of(lp_ref[a0], PK_ROWS)
            p1 = pl.multiple_of(lp_ref[a1], PK_ROWS)
            lo0, hi0 = _unpack_halves(ylocal[slot, pl.ds(p0, PK_ROWS), :])
            lo1, hi1 = _unpack_halves(ylocal[slot, pl.ds(p1, PK_ROWS), :])
            g0 = gate_ref[a0]
            g1 = gate_ref[a1]
            dst = pl.multiple_of(t * PK_ROWS, PK_ROWS)
            acc[0, pl.ds(dst, PK_ROWS), :] = g0 * lo0 + g1 * lo1
            acc[1, pl.ds(dst, PK_ROWS), :] = g0 * hi0 + g1 * hi1
        return carry
    lax.fori_loop(0, issue_groups, functools.partial(tok_body, issue=True), 0)
    lax.fori_loop(issue_groups, n_groups, functools.partial(tok_body, issue=False), 0)

    @pl.when(i == n - 1)
    def _():
        wait_slot(1 - slot)

    for half in range(2):
        for s in range(PK_ROWS):
            c0 = half * HALF + s * LANES
            out_ref[:, c0:c0 + LANES] = (h_ref[:, c0:c0 + LANES]
                                         + acc[half, pl.ds(s, t_tile, stride=PK_ROWS), :])


def _rope_tables(seq):
    pos = jnp.arange(seq, dtype=F32)
    inv_freq = ROPE_THETA ** (-jnp.arange(0, ROT_DIM, 2, dtype=F32) / ROT_DIM)
    ang = pos[:, None] * inv_freq[None, :]
    cos, sin = jnp.cos(ang), jnp.sin(ang)
    half = ROT_DIM // 2
    ones = jnp.ones((seq, HEAD_DIM - ROT_DIM), F32)
    zeros = jnp.zeros((seq, HEAD_DIM - ROT_DIM), F32)
    zh = jnp.zeros((seq, half), F32)
    cc = jnp.concatenate([cos, cos, ones], axis=1)
    s_up = jnp.concatenate([-sin, zh, zeros], axis=1)
    s_dn = jnp.concatenate([zh, sin, zeros], axis=1)
    rep = LANES // HEAD_DIM
    return tuple(jnp.tile(t, (1, rep)) for t in (cc, s_up, s_dn))


def _plan(cnt, n_tokens):
    nt = n_tokens // LOCAL_TILE
    experts = jnp.arange(N_EXPERTS, dtype=jnp.int32)
    cntp = ((cnt + CHUNK - 1) // CHUNK) * CHUNK
    loff_end = jnp.cumsum(cntp, axis=1)
    loff = loff_end - cntp
    rows_e = jnp.sum(cntp, axis=0)
    rows_pad = ((rows_e + EXPERT_TILE - 1) // EXPERT_TILE) * EXPERT_TILE
    g_end = jnp.cumsum(rows_pad)
    g_start = g_end - rows_pad
    gpos = g_start[None, :] + jnp.cumsum(cntp, axis=0) - cntp

    lrow = jnp.arange(LT_MAX // CHUNK, dtype=jnp.int32) * CHUNK
    e_lc = jnp.sum((lrow[None, :, None] >= loff_end[:, None, :]).astype(jnp.int32), axis=-1)
    sel = e_lc[:, :, None] == experts
    delta = jnp.sum(jnp.where(sel, (gpos - loff)[:, None, :], 0), axis=-1)
    gch = jnp.where(e_lc < N_EXPERTS, (delta + lrow[None, :]) // CHUNK, 0).reshape(-1)

    n_rows = TOP_K * n_tokens + nt * N_EXPERTS * (CHUNK - 1) + N_EXPERTS * (EXPERT_TILE - CHUNK)
    n_blocks = -(-n_rows // EXPERT_TILE)
    n_rows = n_blocks * EXPERT_TILE
    blk_start = jnp.arange(n_blocks, dtype=jnp.int32) * EXPERT_TILE
    blk_eid = jnp.minimum(
        jnp.sum((blk_start[:, None] >= g_end[None, :]).astype(jnp.int32), axis=-1),
        N_EXPERTS - 1)
    prev = jnp.concatenate([jnp.full((1,), -1, jnp.int32), blk_eid[:-1]])
    first = (blk_eid != prev).astype(jnp.int32)
    n_act = (g_end[-1] // EXPERT_TILE).astype(jnp.int32).reshape(1)
    later = (blk_eid[None, :] > blk_eid[:, None]) & (jnp.arange(n_blocks)[None, :] < n_act)
    next_eid = jnp.min(jnp.where(later, blk_eid[None, :], N_EXPERTS), axis=1)
    next_eid = jnp.where(next_eid < N_EXPERTS, next_eid, -1).astype(jnp.int32)
    of_blk = blk_eid[:, None] == experts
    blk_end = jnp.sum(jnp.where(of_blk, (g_start + rows_e)[None, :], 0), axis=1)
    used = jnp.clip(blk_end - blk_start, 0, EXPERT_TILE).astype(jnp.int32)

    cpb = EXPERT_TILE // CHUNK
    run_start = gpos.T // CHUNK
    run_len = cntp.T // CHUNK
    run_src = (jnp.arange(nt, dtype=jnp.int32)[:, None] * (LT_MAX // CHUNK) + loff // CHUNK).T
    tabs = jnp.stack([run_start, run_len, run_src], axis=0)
    dtabs = tabs - jnp.concatenate(
        [jnp.zeros((3, N_EXPERTS, 1), jnp.int32), tabs[:, :, :-1]], axis=2)
    blk_sel = (blk_eid[:, None] == experts)[:, :, None]
    blk_start_t = jnp.sum(jnp.where(blk_sel, run_start[None], 0), axis=1)
    blk_dtabs = jnp.sum(jnp.where(blk_sel[None], dtabs[:, None], 0), axis=2)
    gc = jnp.arange(n_rows // CHUNK, dtype=jnp.int32).reshape(n_blocks, cpb)
    started = blk_start_t[:, None, :] <= gc[:, :, None]
    picked = jnp.sum(jnp.where(started[None], blk_dtabs[:, :, None, :], 0), axis=-1)
    within = gc - picked[0]
    csrc = jnp.where(within < picked[1], picked[2] + within, 0).reshape(-1)
    return (gch.astype(jnp.int32), csrc.astype(jnp.int32), blk_eid, first, next_eid, used,
            n_act, n_rows)


def kernel(x, norm_mix, w_in, q_norm, k_norm, sinks, w_pool, pool_scale, w_out,
           norm_ffn, w_coarse, b_coarse, w_fine, b_fine, w_gate, w_up, w_down):
    bsz, seq, d = x.shape
    n = bsz * seq
    assert d == D_MODEL and seq % ROW_TILE == 0 and seq % BLOCK == 0
    assert norm_mix.shape[0] == 1, "single-layer problem"
    xf = x.reshape(n, d)

    w_in_b = w_in[0].astype(BF16)
    wq = w_in_b[:, :ATTN_WIDTH]
    wk = w_in_b[:, ATTN_WIDTH:ATTN_WIDTH + KV_WIDTH]
    wv = w_in_b[:, ATTN_WIDTH + KV_WIDTH:ATTN_WIDTH + 2 * KV_WIDTH]
    wu = w_in_b[:, ATTN_WIDTH + 2 * KV_WIDTH:]
    qn = jnp.tile(q_norm[0], N_HEADS).reshape(1, ATTN_WIDTH)
    kn = jnp.tile(k_norm[0], N_KV_HEADS).reshape(1, KV_WIDTH)
    cc, s_up, s_dn = _rope_tables(seq)
    lane_head = jnp.arange(MXU_DIM) // HEAD_DIM
    e_mat = (lane_head[:, None] == lane_head[None, :]).astype(BF16)
    w_pool_b = w_pool[0].astype(BF16)
    pscale = pool_scale[0].reshape(1, POOL_WIDTH)
    w_out_b = w_out[0].astype(BF16)
    wo_attn, wo_pool = w_out_b[:ATTN_WIDTH], w_out_b[ATTN_WIDTH:]
    n_router = N_EXPERT_GROUPS + N_EXPERTS
    w_r = jnp.concatenate([w_coarse[0], w_fine[0]], axis=1)
    w_r = jnp.pad(w_r, ((0, 0), (0, LANES - n_router)))
    w_r_hi = w_r.astype(BF16)
    w_r_lo = (w_r - w_r_hi.astype(F32)).astype(BF16)
    w_r2 = jnp.concatenate([w_r_hi, w_r_lo], axis=1)
    b_r = jnp.pad(jnp.concatenate([b_coarse[0], b_fine[0]]),
                  (0, LANES - n_router)).reshape(1, LANES)

    tm = ROW_TILE
    tiles_per_seq = seq // tm
    n_tiles = n // tm
    idx = jnp.arange(SUB_TILE)
    tri = (idx[None, :] < idx[:, None]).astype(BF16)
    const = lambda *_: (0, 0)
    row_blk = lambda i: (i, 0)

    q, k, v, ksw, vsw, y = pl.pallas_call(
        functools.partial(_inproj_kernel, tiles_per_seq=tiles_per_seq),
        grid=(n_tiles,),
        in_specs=[
            pl.BlockSpec((tm, d), row_blk),
            pl.BlockSpec((1, d), const),
            pl.BlockSpec((d, ATTN_WIDTH), const),
            pl.BlockSpec((d, KV_WIDTH), const),
            pl.BlockSpec((d, KV_WIDTH), const),
            pl.BlockSpec((d, POOL_WIDTH), const),
            pl.BlockSpec((1, ATTN_WIDTH), const),
            pl.BlockSpec((1, KV_WIDTH), const),
            pl.BlockSpec((tm, LANES), lambda i: (i % tiles_per_seq, 0)),
            pl.BlockSpec((tm, LANES), lambda i: (i % tiles_per_seq, 0)),
            pl.BlockSpec((tm, LANES), lambda i: (i % tiles_per_seq, 0)),
            pl.BlockSpec((MXU_DIM, MXU_DIM), const),
            pl.BlockSpec((len(POOL_WINDOWS), POOL_GROUP_DIM, POOL_GROUP_DIM),
                         lambda i: (0, 0, 0)),
            pl.BlockSpec((1, POOL_WIDTH), const),
        ],
        out_specs=[
            pl.BlockSpec((tm, ATTN_WIDTH), row_blk),
            pl.BlockSpec((tm, KV_WIDTH), row_blk),
            pl.BlockSpec((tm, KV_WIDTH), row_blk),
            pl.BlockSpec((tm, KV_WIDTH), row_blk),
            pl.BlockSpec((tm, KV_WIDTH), row_blk),
            pl.BlockSpec((tm, POOL_WIDTH), row_blk),
        ],
        out_shape=[
            jax.ShapeDtypeStruct((n, ATTN_WIDTH), BF16),
            jax.ShapeDtypeStruct((n, KV_WIDTH), BF16),
            jax.ShapeDtypeStruct((n, KV_WIDTH), BF16),
            jax.ShapeDtypeStruct((n, KV_WIDTH), BF16),
            jax.ShapeDtypeStruct((n, KV_WIDTH), BF16),
            jax.ShapeDtypeStruct((n, POOL_WIDTH), BF16),
        ],
        scratch_shapes=[pltpu.VMEM((HALO + tm, POOL_WIDTH), F32)],
        compiler_params=_cparams(),
        name="inproj",
    )(xf, norm_mix[0].reshape(1, d), wq, wk, wv, wu, qn, kn, cc, s_up, s_dn,
      e_mat, w_pool_b, pscale)

    nb = seq // BLOCK
    nq = nb // Q_BLOCKS
    qrows = Q_BLOCKS * BLOCK
    cur = lambda b, j, *_: (b * nq + j, 0)
    prv = lambda b, j, *_: (b * nb + jnp.maximum(Q_BLOCKS * j - 1, 0), 0)
    kv_specs = [pl.BlockSpec((qrows, KV_WIDTH), cur), pl.BlockSpec((BLOCK, KV_WIDTH), prv)]
    attn = pl.pallas_call(
        _attn_kernel,
        grid_spec=pltpu.PrefetchScalarGridSpec(
            num_scalar_prefetch=1,
            grid=(bsz, nq),
            in_specs=[pl.BlockSpec((qrows, ATTN_WIDTH), cur)] + kv_specs * 4,
            out_specs=pl.BlockSpec((qrows, ATTN_WIDTH), cur),
        ),
        out_shape=jax.ShapeDtypeStruct((n, ATTN_WIDTH), BF16),
        compiler_params=_cparams(2),
        name="swa_attn",
    )(sinks[0].astype(F32), q, k, k, ksw, ksw, v, v, vsw, vsw)

    h, hpk, route, cnt = pl.pallas_call(
        _outproj_router_kernel,
        grid=(n_tiles,),
        in_specs=[
            pl.BlockSpec((tm, ATTN_WIDTH), row_blk),
            pl.BlockSpec((tm, POOL_WIDTH), row_blk),
            pl.BlockSpec((tm, d), row_blk),
            pl.BlockSpec((ATTN_WIDTH, d), const),
            pl.BlockSpec((POOL_WIDTH, d), const),
            pl.BlockSpec((1, d), const),
            pl.BlockSpec((d, 2 * LANES), const),
            pl.BlockSpec((1, LANES), const),
            pl.BlockSpec((SUB_TILE, SUB_TILE), const),
        ],
        out_specs=[
            pl.BlockSpec((tm, d), row_blk),
            pl.BlockSpec((tm * PK_ROWS, LANES), row_blk),
            pl.BlockSpec((SUBLANES, tm), lambda i: (0, i)),
            pl.BlockSpec((SUBLANES, LANES), row_blk),
        ],
        out_shape=[
            jax.ShapeDtypeStruct((n, d), F32),
            jax.ShapeDtypeStruct((n * PK_ROWS, LANES), jnp.int32),
            jax.ShapeDtypeStruct((SUBLANES, n), F32),
            jax.ShapeDtypeStruct((n_tiles * SUBLANES, LANES), F32),
        ],
        compiler_params=_cparams(),
        name="outproj_router",
    )(attn, y, xf, wo_attn, wo_pool, norm_ffn[0].reshape(1, d), w_r2, b_r, tri)

    assert tm == LOCAL_TILE
    gates = route[0:TOP_K].reshape(TOP_K * n)
    lp = route[TOP_K:2 * TOP_K].astype(jnp.int32).reshape(TOP_K * n)
    tile_cnt = cnt.reshape(n_tiles, SUBLANES, LANES)[:, 0, :N_EXPERTS].astype(jnp.int32)
    te = EXPERT_TILE
    gch, csrc, blk_eid, first, next_eid, used, n_act, n_rows = _plan(tile_cnt, n)
    n_blocks = n_rows // te
    n_local = n // LOCAL_TILE

    xs = pl.pallas_call(
        _local_sort_kernel,
        grid_spec=pltpu.PrefetchScalarGridSpec(
            num_scalar_prefetch=1,
            grid=(n_local,),
            in_specs=[pl.BlockSpec((LOCAL_TILE * PK_ROWS, LANES), lambda i, *_: (i, 0))],
            out_specs=pl.BlockSpec((LT_MAX * PK_ROWS, LANES), lambda i, *_: (i, 0)),
        ),
        out_shape=jax.ShapeDtypeStruct((n_local * LT_MAX * PK_ROWS, LANES), jnp.int32),
        compiler_params=_cparams(),
        name="local_sort",
    )(lp * PK_ROWS, hpk)

    ys = pl.pallas_call(
        _expert_kernel,
        grid_spec=pltpu.PrefetchScalarGridSpec(
            num_scalar_prefetch=6,
            grid=(n_blocks,),
            in_specs=[pl.BlockSpec(memory_space=pl.ANY)] * 4,
            out_specs=pl.BlockSpec((te * PK_ROWS, LANES), lambda i, *_: (i, 0)),
            scratch_shapes=[
                pltpu.VMEM((2, te * PK_ROWS, LANES), jnp.int32),
                pltpu.VMEM((d, D_EXPERT), F32),
                pltpu.VMEM((d, D_EXPERT), F32),
                pltpu.VMEM((D_EXPERT, d), F32),
                pltpu.VMEM((d, D_EXPERT), BF16),
                pltpu.VMEM((d, D_EXPERT), BF16),
                pltpu.VMEM((D_EXPERT, d), BF16),
                pltpu.SemaphoreType.DMA((2,)),
                pltpu.SemaphoreType.DMA((3,)),
            ],
        ),
        out_shape=jax.ShapeDtypeStruct((n_rows * PK_ROWS, LANES), jnp.int32),
        compiler_params=_cparams(),
        name="experts",
    )(blk_eid, first, n_act, csrc * (CHUNK * PK_ROWS), next_eid, used, xs,
      w_gate[0], w_up[0], w_down[0])

    tt = LOCAL_TILE
    out = pl.pallas_call(
        _combine_kernel,
        grid_spec=pltpu.PrefetchScalarGridSpec(
            num_scalar_prefetch=3,
            grid=(n_local,),
            in_specs=[
                pl.BlockSpec((tt, d), lambda i, *_: (i, 0)),
                pl.BlockSpec(memory_space=pl.ANY),
            ],
            out_specs=pl.BlockSpec((tt, d), lambda i, *_: (i, 0)),
            scratch_shapes=[
                pltpu.VMEM((2, LT_MAX * PK_ROWS, LANES), jnp.int32),
                pltpu.VMEM((2, tt * PK_ROWS, LANES), F32),
                pltpu.SemaphoreType.DMA((2,)),
            ],
        ),
        out_shape=jax.ShapeDtypeStruct((n, d), F32),
        compiler_params=_cparams(),
        name="combine",
    )(gch * (CHUNK * PK_ROWS), lp * PK_ROWS, gates, h, ys)
    return out.reshape(bsz, seq, d)
```

```python
import functools

import jax
import jax.numpy as jnp
from jax import lax
from jax.experimental import pallas as pl
from jax.experimental.pallas import tpu as pltpu

F32 = jnp.float32
BF16 = jnp.bfloat16

D_MODEL = 2048
N_HEADS = 16
N_KV_HEADS = 4
HEAD_DIM = 64
GROUP = N_HEADS // N_KV_HEADS
ROT_DIM = HEAD_DIM // 4
ROPE_THETA = 500000.0
WINDOW = 128
BLOCK = 128
Q_BLOCKS = 4
ATTN_WIDTH = N_HEADS * HEAD_DIM
KV_WIDTH = N_KV_HEADS * HEAD_DIM
POOL_WINDOWS = (2, 4, 8, 16)
POOL_WIDTH = D_MODEL // 2
POOL_GROUP_DIM = POOL_WIDTH // len(POOL_WINDOWS)
N_EXPERT_GROUPS = 4
EXPERTS_PER_GROUP = 8
N_EXPERTS = N_EXPERT_GROUPS * EXPERTS_PER_GROUP
TOP_K = 2
D_EXPERT = 512
EPS = 1e-6

LANES = 128
SUBLANES = 8
MXU_DIM = 256
HALO = 16
NEG_BIG = -1e30
ROUTER_FINE0 = 8
HALF = D_MODEL // 2
PK_ROWS = HALF // LANES
Y_ROWS = D_MODEL // LANES

ROW_TILE = 512
SUB_TILE = 512
EXPERT_TILE = 512
LOCAL_TILE = 512
CHUNK = 8
LT_MAX = TOP_K * LOCAL_TILE + N_EXPERTS * (CHUNK - 1)
VMEM_LIMIT = 56 * 1024 * 1024


def _cparams(n_axes=1):
    return pltpu.CompilerParams(
        dimension_semantics=("arbitrary",) * n_axes,
        vmem_limit_bytes=VMEM_LIMIT,
    )


def _pack_halves(t):
    return pltpu.pack_elementwise([t[:, :HALF], t[:, HALF:]], packed_dtype=BF16)


def _unpack_halves(w):
    return tuple(pltpu.unpack_elementwise(w, index=i, packed_dtype=BF16, unpacked_dtype=F32)
                 for i in range(2))


def _head_sumsq(t, e_ref):
    t2 = (t * t).astype(BF16)
    e = e_ref[...]
    outs = []
    for c in range(t.shape[1] // MXU_DIM):
        sl = slice(c * MXU_DIM, (c + 1) * MXU_DIM)
        outs.append(jnp.dot(t2[:, sl], e, preferred_element_type=F32))
    return outs[0] if len(outs) == 1 else jnp.concatenate(outs, axis=1)


def _norm_rope(t, gain, e_ref, cc, s_up, s_dn, out_scale):
    w = t.shape[1]
    ss = _head_sumsq(t, e_ref)
    tn = t * lax.rsqrt(ss * (1.0 / HEAD_DIM) + EPS) * gain
    reps = w // LANES
    up = pltpu.roll(tn, w - ROT_DIM // 2, axis=1)
    dn = pltpu.roll(tn, ROT_DIM // 2, axis=1)
    c = jnp.concatenate([cc] * reps, axis=1)
    su = jnp.concatenate([s_up] * reps, axis=1)
    sd = jnp.concatenate([s_dn] * reps, axis=1)
    return (tn * c + up * su + dn * sd) * out_scale


def _swap_head_pairs(t):
    w = t.shape[1]
    lane = lax.broadcasted_iota(jnp.int32, t.shape, 1)
    from_up = pltpu.roll(t, w - HEAD_DIM, axis=1)
    from_dn = pltpu.roll(t, HEAD_DIM, axis=1)
    return jnp.where((lane & (LANES - 1)) < HEAD_DIM, from_up, from_dn)


def _inproj_kernel(x_ref, g_ref, wq_ref, wk_ref, wv_ref, wu_ref, qn_ref, kn_ref,
                   cc_ref, su_ref, sd_ref, e_ref, wpool_ref, pscale_ref,
                   q_out, k_out, v_out, ksw_out, vsw_out, y_out, ubuf, *, tiles_per_seq):
    i = pl.program_id(0)
    tm = x_ref.shape[0]

    @pl.when(i % tiles_per_seq == 0)
    def _():
        ubuf[0:HALO, :] = jnp.zeros((HALO, POOL_WIDTH), F32)

    for r0 in range(0, tm, SUB_TILE):
        rows = slice(r0, r0 + SUB_TILE)
        x = x_ref[rows, :]
        ms = jnp.mean(x * x, axis=-1, keepdims=True)
        hn = (x * lax.rsqrt(ms + EPS) * g_ref[...]).astype(BF16)

        cc, s_up, s_dn = cc_ref[rows, :], su_ref[rows, :], sd_ref[rows, :]
        q = jnp.dot(hn, wq_ref[...], preferred_element_type=F32)
        q_out[rows, :] = _norm_rope(q, qn_ref[...], e_ref, cc, s_up, s_dn,
                                    HEAD_DIM ** -0.5).astype(BF16)
        k = jnp.dot(hn, wk_ref[...], preferred_element_type=F32)
        k = _norm_rope(k, kn_ref[...], e_ref, cc, s_up, s_dn, 1.0)
        v = jnp.dot(hn, wv_ref[...], preferred_element_type=F32)
        k_out[rows, :] = k.astype(BF16)
        v_out[rows, :] = v.astype(BF16)
        ksw_out[rows, :] = _swap_head_pairs(k).astype(BF16)
        vsw_out[rows, :] = _swap_head_pairs(v).astype(BF16)

        base = HALO + r0
        ubuf[base:base + SUB_TILE, :] = jnp.dot(hn, wu_ref[...], preferred_element_type=F32)
        pos = (i % tiles_per_seq) * tm + r0 + lax.broadcasted_iota(
            jnp.int32, (SUB_TILE, POOL_GROUP_DIM), 0)
        for g, w in enumerate(POOL_WINDOWS):
            cols = slice(g * POOL_GROUP_DIM, (g + 1) * POOL_GROUP_DIM)
            assert w & (w - 1) == 0 and w - 1 < HALO
            acc = ubuf[base - HALO:base + SUB_TILE, cols]
            shift = 1
            while shift < w:
                acc = acc + pltpu.roll(acc, shift, axis=0)
                shift *= 2
            acc = acc[HALO:, :]
            u_g = ubuf[base:base + SUB_TILE, cols]
            cnt = jnp.minimum(pos + 1, w).astype(F32)
            d = (acc / cnt - u_g).astype(BF16)
            y = jnp.dot(d, wpool_ref[g], preferred_element_type=F32)
            y_out[rows, cols] = (y * pscale_ref[:, cols]).astype(BF16)
    ubuf[0:HALO, :] = ubuf[tm:tm + HALO, :]


def _attn_kernel(sinks_ref, q_ref, kc_ref, kp_ref, kswc_ref, kswp_ref,
                 vc_ref, vp_ref, vswc_ref, vswp_ref, o_ref):
    j = pl.program_id(1)
    two = 2 * BLOCK
    row = lax.broadcasted_iota(jnp.int32, (two, two), 0) & (BLOCK - 1)
    col = lax.broadcasted_iota(jnp.int32, (two, two), 1)
    diff = row + BLOCK - col
    local = (diff >= 0) & (diff < WINDOW)
    lo_lanes = lax.broadcasted_iota(jnp.int32, (two, LANES), 1) < HEAD_DIM
    row_p = lax.broadcasted_iota(jnp.int32, (two, BLOCK), 0)
    diag = (row_p & (BLOCK - 1)) == lax.broadcasted_iota(jnp.int32, (two, BLOCK), 1)
    top_rows = row_p < BLOCK
    diag_top = diag & top_rows
    diag_bot = diag & jnp.logical_not(top_rows)
    zero = jnp.zeros((two, LANES), BF16)

    for sb in range(q_ref.shape[0] // BLOCK):
        rows = slice(sb * BLOCK, (sb + 1) * BLOCK)
        if sb == 0:
            def band(cur_ref, prev_ref):
                return jnp.concatenate([prev_ref[...], cur_ref[0:BLOCK, :]], axis=0)
            valid = local & ((col >= BLOCK) | (j > 0))
        else:
            def band(cur_ref, prev_ref, sb=sb):
                return cur_ref[(sb - 1) * BLOCK:(sb + 1) * BLOCK, :]
            valid = local
        k_nat, k_swp = band(kc_ref, kp_ref), band(kswc_ref, kswp_ref)
        v_nat, v_swp = band(vc_ref, vp_ref), band(vswc_ref, vswp_ref)

        for g in range(N_KV_HEADS):
            c0 = (g // 2) * LANES
            kcols = (k_nat[:, c0:c0 + LANES], k_swp[:, c0:c0 + LANES])
            vcols = (v_nat[:, c0:c0 + LANES], v_swp[:, c0:c0 + LANES])
            in_lo, in_hi = (0, 1) if g % 2 == 0 else (1, 0)
            q0 = g * GROUP * HEAD_DIM
            qq = jnp.concatenate([q_ref[rows, q0:q0 + LANES],
                                  q_ref[rows, q0 + LANES:q0 + 2 * LANES]], axis=0)
            halves = []
            for half, src in ((0, in_lo), (1, in_hi)):
                qm = jnp.where(lo_lanes, qq, zero) if half == 0 else jnp.where(lo_lanes, zero, qq)
                s = lax.dot_general(qm, kcols[src], (((1,), (1,)), ((), ())),
                                    preferred_element_type=F32)
                s = jnp.where(valid, s, NEG_BIG)
                h_top = g * GROUP + half
                s_prev = jnp.where(diag_top, sinks_ref[h_top],
                                   jnp.where(diag_bot, sinks_ref[h_top + 2], s[:, :BLOCK]))
                s_cur = s[:, BLOCK:]
                m = jnp.max(jnp.maximum(s_prev, s_cur), axis=-1, keepdims=True)
                p_prev = jnp.exp(s_prev - m)
                p_cur = jnp.exp(s_cur - m)
                denom = jnp.sum(p_prev + p_cur, axis=-1, keepdims=True)
                p = jnp.concatenate([jnp.where(diag, 0.0, p_prev), p_cur], axis=1)
                o = jnp.dot(p.astype(BF16), vcols[src], preferred_element_type=F32)
                halves.append(o / denom)
            o_pair = jnp.where(lo_lanes, halves[0], halves[1]).astype(BF16)
            o_ref[rows, q0:q0 + LANES] = o_pair[0:BLOCK, :]
            o_ref[rows, q0 + LANES:q0 + 2 * LANES] = o_pair[BLOCK:, :]


def _outproj_router_kernel(o_ref, y_ref, x_ref, woa_ref, wop_ref, g_ref, wr_ref,
                           br_ref, tri_ref, h_out, hpk_out, route_out, cnt_out):
    tm = x_ref.shape[0]
    h = (x_ref[...]
         + jnp.dot(o_ref[...], woa_ref[...], preferred_element_type=F32)
         + jnp.dot(y_ref[...], wop_ref[...], preferred_element_type=F32))
    h_out[...] = h
    ms = jnp.mean(h * h, axis=-1, keepdims=True)
    hn = h * lax.rsqrt(ms + EPS) * g_ref[...]
    hb = hn.astype(BF16)
    packed = _pack_halves(hn)
    for s in range(PK_ROWS):
        hpk_out[pl.ds(s, tm, stride=PK_ROWS), :] = packed[:, s * LANES:(s + 1) * LANES]

    both = jnp.dot(hb, wr_ref[...], preferred_element_type=F32)
    logits = (both[:, :LANES] + both[:, LANES:] + br_ref[...]).T
    assert EXPERTS_PER_GROUP == SUBLANES and ROUTER_FINE0 % SUBLANES == 0
    slot = lax.broadcasted_iota(jnp.int32, (SUBLANES, tm), 0).astype(F32)

    def col_max(t):
        return jnp.max(t, axis=0, keepdims=True)

    def first_slot_of(mask):
        return jnp.min(jnp.where(mask, slot, float(SUBLANES)), axis=0, keepdims=True)

    coarse = jnp.where(slot < N_EXPERT_GROUPS, logits[0:SUBLANES, :], NEG_BIG)
    cmax = col_max(coarse)
    grp = first_slot_of(coarse == cmax)
    csum = jnp.sum(jnp.exp(coarse - cmax), axis=0, keepdims=True)

    def fine_of(g):
        return logits[ROUTER_FINE0 + g * SUBLANES:ROUTER_FINE0 + (g + 1) * SUBLANES, :]

    fine = fine_of(N_EXPERT_GROUPS - 1)
    for g in range(N_EXPERT_GROUPS - 2, -1, -1):
        fine = jnp.where(grp == float(g), fine_of(g), fine)
    f1 = col_max(fine)
    i1 = first_slot_of(fine == f1)
    rest = jnp.where(slot == i1, NEG_BIG, fine)
    f2 = col_max(rest)
    i2 = first_slot_of(rest == f2)
    ratio = jnp.exp(f2 - f1)
    g1 = (1.0 / csum) / (1.0 + ratio)
    g2 = g1 * ratio
    e1 = grp * EXPERTS_PER_GROUP + i1
    e2 = grp * EXPERTS_PER_GROUP + i2

    expert = lax.broadcasted_iota(jnp.int32, (N_EXPERTS, tm), 0).astype(F32)
    oh1 = expert == e1
    oh2 = expert == e2
    oh = jnp.where(oh1, 1.0, 0.0) + jnp.where(oh2, 1.0, 0.0)
    rank = lax.dot_general(oh.astype(BF16), tri_ref[...], (((1,), (1,)), ((), ())),
                           preferred_element_type=F32)
    cnt = jnp.sum(oh, axis=1, keepdims=True)
    cnt_pad = jnp.floor((cnt + (CHUNK - 1)) * (1.0 / CHUNK)) * CHUNK
    padded_cols = jnp.concatenate(
        [jnp.broadcast_to(cnt_pad, (N_EXPERTS, LANES)),
         jnp.zeros((LANES - N_EXPERTS, LANES), F32)], axis=0).astype(BF16)
    run_start = jnp.dot(tri_ref[0:N_EXPERTS, 0:LANES], padded_cols,
                        preferred_element_type=F32)[:, 0:1]
    table = rank + run_start
    lp1 = jnp.sum(jnp.where(oh1, table, 0.0), axis=0, keepdims=True)
    lp2 = jnp.sum(jnp.where(oh2, table, 0.0), axis=0, keepdims=True)
    route_out[...] = jnp.where(slot == 0.0, g1,
                               jnp.where(slot == 1.0, g2,
                                         jnp.where(slot == 2.0, lp1,
                                                   jnp.where(slot == 3.0, lp2, 0.0))))
    cnt_out[...] = jnp.broadcast_to(cnt, cnt_out.shape)


def _expert_kernel(eid_ref, first_ref, nact_ref, off_ref, next_ref, used_ref,
                   hpk_hbm, wg_hbm, wu_hbm, wd_hbm, ys_out,
                   xbuf, wg_f, wu_f, wd_f, wg_s, wu_s, wd_s, sems, wsems):
    i = pl.program_id(0)
    tm = ys_out.shape[0] // PK_ROWS
    n_act = nact_ref[0]

    def weight_copies(e):
        return (pltpu.make_async_copy(wg_hbm.at[e], wg_f, wsems.at[0]),
                pltpu.make_async_copy(wu_hbm.at[e], wu_f, wsems.at[1]),
                pltpu.make_async_copy(wd_hbm.at[e], wd_f, wsems.at[2]))

    ch_rows = CHUNK * PK_ROWS
    n_ch = tm // CHUNK

    def gather(blk, slot):
        for c in range(n_ch):
            src = pl.multiple_of(off_ref[blk * n_ch + c], ch_rows)
            pltpu.make_async_copy(hpk_hbm.at[pl.ds(src, ch_rows)],
                                  xbuf.at[slot, pl.ds(c * ch_rows, ch_rows)],
                                  sems.at[slot]).start()

    @pl.when(i == 0)
    def _():
        for cp in weight_copies(eid_ref[0]):
            cp.start()
        gather(0, 0)

    @pl.when(i + 1 < n_act)
    def _():
        gather(i + 1, (i + 1) % 2)

    @pl.when(i < n_act)
    def _():
        slot = i % 2

        @pl.when(first_ref[i] == 1)
        def _():
            for cp in weight_copies(eid_ref[i]):
                cp.wait()
            wg_s[...] = wg_f[...].astype(BF16)
            wu_s[...] = wu_f[...].astype(BF16)
            wd_s[...] = wd_f[...].astype(BF16)

            @pl.when(next_ref[i] >= 0)
            def _():
                for cp in weight_copies(next_ref[i]):
                    cp.start()

        pltpu.make_async_copy(hpk_hbm.at[pl.ds(0, tm * PK_ROWS)], xbuf.at[slot],
                              sems.at[slot]).wait()

        def swiglu(rows):
            xu = jnp.concatenate(
                [xbuf[slot, pl.ds(s, rows, stride=PK_ROWS), :] for s in range(PK_ROWS)],
                axis=1)
            x_lo, x_hi = (v.astype(BF16) for v in _unpack_halves(xu))
            a = (jnp.dot(x_lo, wg_s[0:HALF, :], preferred_element_type=F32)
                 + jnp.dot(x_hi, wg_s[HALF:, :], preferred_element_type=F32))
            b = (jnp.dot(x_lo, wu_s[0:HALF, :], preferred_element_type=F32)
                 + jnp.dot(x_hi, wu_s[HALF:, :], preferred_element_type=F32))
            mid = (a * jax.nn.sigmoid(a) * b).astype(BF16)
            y = jnp.dot(mid, wd_s[...], preferred_element_type=F32)
            yw = _pack_halves(y)
            for s in range(PK_ROWS):
                ys_out[pl.ds(s, rows, stride=PK_ROWS), :] = yw[:, s * LANES:(s + 1) * LANES]
            if rows < tm:
                ys_out[rows * PK_ROWS:, :] = jnp.zeros(((tm - rows) * PK_ROWS, LANES),
                                                       ys_out.dtype)

        quarter = tm // 4
        for rows in range(quarter, tm + 1, quarter):
            @pl.when((used_ref[i] > rows - quarter) & (used_ref[i] <= rows))
            def _(rows=rows):
                swiglu(rows)

    @pl.when(i >= n_act)
    def _():
        ys_out[...] = jnp.zeros(ys_out.shape, ys_out.dtype)


def _local_sort_kernel(lp_ref, hpk_ref, xs_out):
    i = pl.program_id(0)
    t_tile = hpk_ref.shape[0] // PK_ROWS
    n_tok = pl.num_programs(0) * t_tile
    xs_out[...] = jnp.zeros(xs_out.shape, xs_out.dtype)

    def body(q, carry):
        for u in range(SUBLANES):
            t = q * SUBLANES + u
            row = hpk_ref[pl.ds(pl.multiple_of(t * PK_ROWS, PK_ROWS), PK_ROWS), :]
            for k in range(TOP_K):
                dst = pl.multiple_of(lp_ref[k * n_tok + i * t_tile + t], PK_ROWS)
                xs_out[pl.ds(dst, PK_ROWS), :] = row
        return carry
    lax.fori_loop(0, t_tile // SUBLANES, body, 0)


def _combine_kernel(gch_ref, lp_ref, gate_ref, h_ref, ys_hbm, out_ref, ylocal, acc, sems):
    i = pl.program_id(0)
    n = pl.num_programs(0)
    t_tile = h_ref.shape[0]
    n_tok = n * t_tile
    ch_rows = CHUNK * PK_ROWS
    n_ch = ylocal.shape[1] // ch_rows

    def chunk_copy(tile, slot, lc):
        src = pl.multiple_of(gch_ref[tile * n_ch + lc], ch_rows)
        dst = pl.multiple_of(lc * ch_rows, ch_rows)
        pltpu.make_async_copy(ys_hbm.at[pl.ds(src, ch_rows)],
                              ylocal.at[slot, pl.ds(dst, ch_rows)],
                              sems.at[slot]).start()

    def wait_slot(slot):
        pltpu.make_async_copy(ys_hbm.at[pl.ds(0, n_ch * ch_rows)], ylocal.at[slot],
                              sems.at[slot]).wait()

    @pl.when(i == 0)
    def _():
        def body(lc, carry):
            chunk_copy(0, 0, lc)
            return carry
        lax.fori_loop(0, n_ch, body, 0)

    slot = i % 2
    wait_slot(slot)

    nxt = jnp.minimum(i + 1, n - 1)
    tok_unroll = 4
    n_groups = t_tile // tok_unroll
    per_group = -(-n_ch // n_groups)
    issue_groups = n_ch // per_group
    assert issue_groups * per_group == n_ch and issue_groups <= n_groups

    def tok_body(q, carry, issue):
        if issue:
            for u in range(per_group):
                chunk_copy(nxt, 1 - slot, q * per_group + u)
        for u in range(tok_unroll):
            t = q * tok_unroll + u
            a0 = i * t_tile + t
            a1 = n_tok + a0
            p0 = pl.multiple_of(lp_ref[a0], PK_ROWS)
            p1 = pl.multiple_of(lp_ref[a1], PK_ROWS)
            lo0, hi0 = _unpack_halves(ylocal[slot, pl.ds(p0, PK_ROWS), :])
            lo1, hi1 = _unpack_halves(ylocal[slot, pl.ds(p1, PK_ROWS), :])
            g0 = gate_ref[a0]
            g1 = gate_ref[a1]
            dst = pl.multiple_of(t * PK_ROWS, PK_ROWS)
            acc[0, pl.ds(dst, PK_ROWS), :] = g0 * lo0 + g1 * lo1
            acc[1, pl.ds(dst, PK_ROWS), :] = g0 * hi0 + g1 * hi1
        return carry
    lax.fori_loop(0, issue_groups, functools.partial(tok_body, issue=True), 0)
    lax.fori_loop(issue_groups, n_groups, functools.partial(tok_body, issue=False), 0)

    @pl.when(i == n - 1)
    def _():
        wait_slot(1 - slot)

    for half in range(2):
        for s in range(PK_ROWS):
            c0 = half * HALF + s * LANES
            out_ref[:, c0:c0 + LANES] = (h_ref[:, c0:c0 + LANES]
                                         + acc[half, pl.ds(s, t_tile, stride=PK_ROWS), :])


def _rope_tables(seq):
    pos = jnp.arange(seq, dtype=F32)
    inv_freq = ROPE_THETA ** (-jnp.arange(0, ROT_DIM, 2, dtype=F32) / ROT_DIM)
    ang = pos[:, None] * inv_freq[None, :]
    cos, sin = jnp.cos(ang), jnp.sin(ang)
    half = ROT_DIM // 2
    ones = jnp.ones((seq, HEAD_DIM - ROT_DIM), F32)
    zeros = jnp.zeros((seq, HEAD_DIM - ROT_DIM), F32)
    zh = jnp.zeros((seq, half), F32)
    cc = jnp.concatenate([cos, cos, ones], axis=1)
    s_up = jnp.concatenate([-sin, zh, zeros], axis=1)
    s_dn = jnp.concatenate([zh, sin, zeros], axis=1)
    rep = LANES // HEAD_DIM
    return tuple(jnp.tile(t, (1, rep)) for t in (cc, s_up, s_dn))


def _plan(cnt, n_tokens):
    nt = n_tokens // LOCAL_TILE
    experts = jnp.arange(N_EXPERTS, dtype=jnp.int32)
    cntp = ((cnt + CHUNK - 1) // CHUNK) * CHUNK
    loff_end = jnp.cumsum(cntp, axis=1)
    loff = loff_end - cntp
    rows_e = jnp.sum(cntp, axis=0)
    rows_pad = ((rows_e + EXPERT_TILE - 1) // EXPERT_TILE) * EXPERT_TILE
    g_end = jnp.cumsum(rows_pad)
    g_start = g_end - rows_pad
    gpos = g_start[None, :] + jnp.cumsum(cntp, axis=0) - cntp

    lrow = jnp.arange(LT_MAX // CHUNK, dtype=jnp.int32) * CHUNK
    e_lc = jnp.sum((lrow[None, :, None] >= loff_end[:, None, :]).astype(jnp.int32), axis=-1)
    sel = e_lc[:, :, None] == experts
    delta = jnp.sum(jnp.where(sel, (gpos - loff)[:, None, :], 0), axis=-1)
    gch = jnp.where(e_lc < N_EXPERTS, (delta + lrow[None, :]) // CHUNK, 0).reshape(-1)

    n_rows = TOP_K * n_tokens + nt * N_EXPERTS * (CHUNK - 1) + N_EXPERTS * (EXPERT_TILE - CHUNK)
    n_blocks = -(-n_rows // EXPERT_TILE)
    n_rows = n_blocks * EXPERT_TILE
    blk_start = jnp.arange(n_blocks, dtype=jnp.int32) * EXPERT_TILE
    blk_eid = jnp.minimum(
        jnp.sum((blk_start[:, None] >= g_end[None, :]).astype(jnp.int32), axis=-1),
        N_EXPERTS - 1)
    prev = jnp.concatenate([jnp.full((1,), -1, jnp.int32), blk_eid[:-1]])
    first = (blk_eid != prev).astype(jnp.int32)
    n_act = (g_end[-1] // EXPERT_TILE).astype(jnp.int32).reshape(1)
    later = (blk_eid[None, :] > blk_eid[:, None]) & (jnp.arange(n_blocks)[None, :] < n_act)
    next_eid = jnp.min(jnp.where(later, blk_eid[None, :], N_EXPERTS), axis=1)
    next_eid = jnp.where(next_eid < N_EXPERTS, next_eid, -1).astype(jnp.int32)
    of_blk = blk_eid[:, None] == experts
    blk_end = jnp.sum(jnp.where(of_blk, (g_start + rows_e)[None, :], 0), axis=1)
    used = jnp.clip(blk_end - blk_start, 0, EXPERT_TILE).astype(jnp.int32)

    cpb = EXPERT_TILE // CHUNK
    run_start = gpos.T // CHUNK
    run_len = cntp.T // CHUNK
    run_src = (jnp.arange(nt, dtype=jnp.int32)[:, None] * (LT_MAX // CHUNK) + loff // CHUNK).T
    tabs = jnp.stack([run_start, run_len, run_src], axis=0)
    dtabs = tabs - jnp.concatenate(
        [jnp.zeros((3, N_EXPERTS, 1), jnp.int32), tabs[:, :, :-1]], axis=2)
    blk_sel = (blk_eid[:, None] == experts)[:, :, None]
    blk_start_t = jnp.sum(jnp.where(blk_sel, run_start[None], 0), axis=1)
    blk_dtabs = jnp.sum(jnp.where(blk_sel[None], dtabs[:, None], 0), axis=2)
    gc = jnp.arange(n_rows // CHUNK, dtype=jnp.int32).reshape(n_blocks, cpb)
    started = blk_start_t[:, None, :] <= gc[:, :, None]
    picked = jnp.sum(jnp.where(started[None], blk_dtabs[:, :, None, :], 0), axis=-1)
    within = gc - picked[0]
    csrc = jnp.where(within < picked[1], picked[2] + within, 0).reshape(-1)
    return (gch.astype(jnp.int32), csrc.astype(jnp.int32), blk_eid, first, next_eid, used,
            n_act, n_rows)


def kernel(x, norm_mix, w_in, q_norm, k_norm, sinks, w_pool, pool_scale, w_out,
           norm_ffn, w_coarse, b_coarse, w_fine, b_fine, w_gate, w_up, w_down):
    bsz, seq, d = x.shape
    n = bsz * seq
    assert d == D_MODEL and seq % ROW_TILE == 0 and seq % BLOCK == 0
    assert norm_mix.shape[0] == 1, "single-layer problem"
    xf = x.reshape(n, d)

    w_in_b = w_in[0].astype(BF16)
    wq = w_in_b[:, :ATTN_WIDTH]
    wk = w_in_b[:, ATTN_WIDTH:ATTN_WIDTH + KV_WIDTH]
    wv = w_in_b[:, ATTN_WIDTH + KV_WIDTH:ATTN_WIDTH + 2 * KV_WIDTH]
    wu = w_in_b[:, ATTN_WIDTH + 2 * KV_WIDTH:]
    qn = jnp.tile(q_norm[0], N_HEADS).reshape(1, ATTN_WIDTH)
    kn = jnp.tile(k_norm[0], N_KV_HEADS).reshape(1, KV_WIDTH)
    cc, s_up, s_dn = _rope_tables(seq)
    lane_head = jnp.arange(MXU_DIM) // HEAD_DIM
    e_mat = (lane_head[:, None] == lane_head[None, :]).astype(BF16)
    w_pool_b = w_pool[0].astype(BF16)
    pscale = pool_scale[0].reshape(1, POOL_WIDTH)
    w_out_b = w_out[0].astype(BF16)
    wo_attn, wo_pool = w_out_b[:ATTN_WIDTH], w_out_b[ATTN_WIDTH:]
    gap = ROUTER_FINE0 - N_EXPERT_GROUPS
    tail = LANES - ROUTER_FINE0 - N_EXPERTS
    w_r = jnp.concatenate([w_coarse[0], jnp.zeros((d, gap), F32), w_fine[0],
                           jnp.zeros((d, tail), F32)], axis=1)
    w_r_hi = w_r.astype(BF16)
    w_r_lo = (w_r - w_r_hi.astype(F32)).astype(BF16)
    w_r2 = jnp.concatenate([w_r_hi, w_r_lo], axis=1)
    b_r = jnp.concatenate([b_coarse[0], jnp.zeros((gap,), F32), b_fine[0],
                           jnp.zeros((tail,), F32)]).reshape(1, LANES)

    tm = ROW_TILE
    tiles_per_seq = seq // tm
    n_tiles = n // tm
    idx = jnp.arange(SUB_TILE)
    tri = (idx[None, :] < idx[:, None]).astype(BF16)
    const = lambda *_: (0, 0)
    row_blk = lambda i: (i, 0)

    q, k, v, ksw, vsw, y = pl.pallas_call(
        functools.partial(_inproj_kernel, tiles_per_seq=tiles_per_seq),
        grid=(n_tiles,),
        in_specs=[
            pl.BlockSpec((tm, d), row_blk),
            pl.BlockSpec((1, d), const),
            pl.BlockSpec((d, ATTN_WIDTH), const),
            pl.BlockSpec((d, KV_WIDTH), const),
            pl.BlockSpec((d, KV_WIDTH), const),
            pl.BlockSpec((d, POOL_WIDTH), const),
            pl.BlockSpec((1, ATTN_WIDTH), const),
            pl.BlockSpec((1, KV_WIDTH), const),
            pl.BlockSpec((tm, LANES), lambda i: (i % tiles_per_seq, 0)),
            pl.BlockSpec((tm, LANES), lambda i: (i % tiles_per_seq, 0)),
            pl.BlockSpec((tm, LANES), lambda i: (i % tiles_per_seq, 0)),
            pl.BlockSpec((MXU_DIM, MXU_DIM), const),
            pl.BlockSpec((len(POOL_WINDOWS), POOL_GROUP_DIM, POOL_GROUP_DIM),
                         lambda i: (0, 0, 0)),
            pl.BlockSpec((1, POOL_WIDTH), const),
        ],
        out_specs=[
            pl.BlockSpec((tm, ATTN_WIDTH), row_blk),
            pl.BlockSpec((tm, KV_WIDTH), row_blk),
            pl.BlockSpec((tm, KV_WIDTH), row_blk),
            pl.BlockSpec((tm, KV_WIDTH), row_blk),
            pl.BlockSpec((tm, KV_WIDTH), row_blk),
            pl.BlockSpec((tm, POOL_WIDTH), row_blk),
        ],
        out_shape=[
            jax.ShapeDtypeStruct((n, ATTN_WIDTH), BF16),
            jax.ShapeDtypeStruct((n, KV_WIDTH), BF16),
            jax.ShapeDtypeStruct((n, KV_WIDTH), BF16),
            jax.ShapeDtypeStruct((n, KV_WIDTH), BF16),
            jax.ShapeDtypeStruct((n, KV_WIDTH), BF16),
            jax.ShapeDtypeStruct((n, POOL_WIDTH), BF16),
        ],
        scratch_shapes=[pltpu.VMEM((HALO + tm, POOL_WIDTH), F32)],
        compiler_params=_cparams(),
        name="inproj",
    )(xf, norm_mix[0].reshape(1, d), wq, wk, wv, wu, qn, kn, cc, s_up, s_dn,
      e_mat, w_pool_b, pscale)

    nb = seq // BLOCK
    nq = nb // Q_BLOCKS
    qrows = Q_BLOCKS * BLOCK
    cur = lambda b, j, *_: (b * nq + j, 0)
    prv = lambda b, j, *_: (b * nb + jnp.maximum(Q_BLOCKS * j - 1, 0), 0)
    kv_specs = [pl.BlockSpec((qrows, KV_WIDTH), cur), pl.BlockSpec((BLOCK, KV_WIDTH), prv)]
    attn = pl.pallas_call(
        _attn_kernel,
        grid_spec=pltpu.PrefetchScalarGridSpec(
            num_scalar_prefetch=1,
            grid=(bsz, nq),
            in_specs=[pl.BlockSpec((qrows, ATTN_WIDTH), cur)] + kv_specs * 4,
            out_specs=pl.BlockSpec((qrows, ATTN_WIDTH), cur),
        ),
        out_shape=jax.ShapeDtypeStruct((n, ATTN_WIDTH), BF16),
        compiler_params=_cparams(2),
        name="swa_attn",
    )(sinks[0].astype(F32), q, k, k, ksw, ksw, v, v, vsw, vsw)

    h, hpk, route, cnt = pl.pallas_call(
        _outproj_router_kernel,
        grid=(n_tiles,),
        in_specs=[
            pl.BlockSpec((tm, ATTN_WIDTH), row_blk),
            pl.BlockSpec((tm, POOL_WIDTH), row_blk),
            pl.BlockSpec((tm, d), row_blk),
            pl.BlockSpec((ATTN_WIDTH, d), const),
            pl.BlockSpec((POOL_WIDTH, d), const),
            pl.BlockSpec((1, d), const),
            pl.BlockSpec((d, 2 * LANES), const),
            pl.BlockSpec((1, LANES), const),
            pl.BlockSpec((SUB_TILE, SUB_TILE), const),
        ],
        out_specs=[
            pl.BlockSpec((tm, d), row_blk),
            pl.BlockSpec((tm * PK_ROWS, LANES), row_blk),
            pl.BlockSpec((SUBLANES, tm), lambda i: (0, i)),
            pl.BlockSpec((N_EXPERTS, LANES), row_blk),
        ],
        out_shape=[
            jax.ShapeDtypeStruct((n, d), F32),
            jax.ShapeDtypeStruct((n * PK_ROWS, LANES), jnp.int32),
            jax.ShapeDtypeStruct((SUBLANES, n), F32),
            jax.ShapeDtypeStruct((n_tiles * N_EXPERTS, LANES), F32),
        ],
        compiler_params=_cparams(),
        name="outproj_router",
    )(attn, y, xf, wo_attn, wo_pool, norm_ffn[0].reshape(1, d), w_r2, b_r, tri)

    assert tm == LOCAL_TILE
    gates = route[0:TOP_K].reshape(TOP_K * n)
    lp = route[TOP_K:2 * TOP_K].astype(jnp.int32).reshape(TOP_K * n)
    tile_cnt = cnt.reshape(n_tiles, N_EXPERTS, LANES)[:, :, 0].astype(jnp.int32)
    te = EXPERT_TILE
    gch, csrc, blk_eid, first, next_eid, used, n_act, n_rows = _plan(tile_cnt, n)
    n_blocks = n_rows // te
    n_local = n // LOCAL_TILE

    xs = pl.pallas_call(
        _local_sort_kernel,
        grid_spec=pltpu.PrefetchScalarGridSpec(
            num_scalar_prefetch=1,
            grid=(n_local,),
            in_specs=[pl.BlockSpec((LOCAL_TILE * PK_ROWS, LANES), lambda i, *_: (i, 0))],
            out_specs=pl.BlockSpec((LT_MAX * PK_ROWS, LANES), lambda i, *_: (i, 0)),
        ),
        out_shape=jax.ShapeDtypeStruct((n_local * LT_MAX * PK_ROWS, LANES), jnp.int32),
        compiler_params=_cparams(),
        name="local_sort",
    )(lp * PK_ROWS, hpk)

    ys = pl.pallas_call(
        _expert_kernel,
        grid_spec=pltpu.PrefetchScalarGridSpec(
            num_scalar_prefetch=6,
            grid=(n_blocks,),
            in_specs=[pl.BlockSpec(memory_space=pl.ANY)] * 4,
            out_specs=pl.BlockSpec((te * PK_ROWS, LANES), lambda i, *_: (i, 0)),
            scratch_shapes=[
                pltpu.VMEM((2, te * PK_ROWS, LANES), jnp.int32),
                pltpu.VMEM((d, D_EXPERT), F32),
                pltpu.VMEM((d, D_EXPERT), F32),
                pltpu.VMEM((D_EXPERT, d), F32),
                pltpu.VMEM((d, D_EXPERT), BF16),
                pltpu.VMEM((d, D_EXPERT), BF16),
                pltpu.VMEM((D_EXPERT, d), BF16),
                pltpu.SemaphoreType.DMA((2,)),
                pltpu.SemaphoreType.DMA((3,)),
            ],
        ),
        out_shape=jax.ShapeDtypeStruct((n_rows * PK_ROWS, LANES), jnp.int32),
        compiler_params=_cparams(),
        name="experts",
    )(blk_eid, first, n_act, csrc * (CHUNK * PK_ROWS), next_eid, used, xs,
      w_gate[0], w_up[0], w_down[0])

    tt = LOCAL_TILE
    out = pl.pallas_call(
        _combine_kernel,
        grid_spec=pltpu.PrefetchScalarGridSpec(
            num_scalar_prefetch=3,
            grid=(n_local,),
            in_specs=[
                pl.BlockSpec((tt, d), lambda i, *_: (i, 0)),
                pl.BlockSpec(memory_space=pl.ANY),
            ],
            out_specs=pl.BlockSpec((tt, d), lambda i, *_: (i, 0)),
            scratch_shapes=[
                pltpu.VMEM((2, LT_MAX * PK_ROWS, LANES), jnp.int32),
                pltpu.VMEM((2, tt * PK_ROWS, LANES), F32),
                pltpu.SemaphoreType.DMA((2,)),
            ],
        ),
        out_shape=jax.ShapeDtypeStruct((n, d), F32),
        compiler_params=_cparams(),
        name="combine",
    )(gch * (CHUNK * PK_ROWS), lp * PK_ROWS, gates, h, ys)
    return out.reshape(bsz, seq, d)
```

```python
import functools

import jax
import jax.numpy as jnp
from jax import lax
from jax.experimental import pallas as pl
from jax.experimental.pallas import tpu as pltpu

F32 = jnp.float32
BF16 = jnp.bfloat16

D_MODEL = 2048
N_HEADS = 16
N_KV_HEADS = 4
HEAD_DIM = 64
GROUP = N_HEADS // N_KV_HEADS
ROT_DIM = HEAD_DIM // 4
ROPE_THETA = 500000.0
WINDOW = 128
BLOCK = 128
Q_BLOCKS = 8
ATTN_WIDTH = N_HEADS * HEAD_DIM
KV_WIDTH = N_KV_HEADS * HEAD_DIM
POOL_WINDOWS = (2, 4, 8, 16)
POOL_WIDTH = D_MODEL // 2
POOL_GROUP_DIM = POOL_WIDTH // len(POOL_WINDOWS)
N_EXPERT_GROUPS = 4
EXPERTS_PER_GROUP = 8
N_EXPERTS = N_EXPERT_GROUPS * EXPERTS_PER_GROUP
TOP_K = 2
D_EXPERT = 512
EPS = 1e-6

LANES = 128
SUBLANES = 8
MXU_DIM = 256
HALO = 16
NEG_BIG = -1e30
ROUTER_FINE0 = 8
HALF = D_MODEL // 2
PK_ROWS = HALF // LANES

ROW_TILE = 512
SUB_TILE = 512
EXPERT_TILE = 512
LOCAL_TILE = 512
CHUNK = 8
LT_MAX = TOP_K * LOCAL_TILE + N_EXPERTS * (CHUNK - 1)
VMEM_LIMIT = 56 * 1024 * 1024


def _cparams(n_axes=1):
    return pltpu.CompilerParams(
        dimension_semantics=("arbitrary",) * n_axes,
        vmem_limit_bytes=VMEM_LIMIT,
    )


def _pack_halves(t):
    return pltpu.pack_elementwise([t[:, :HALF], t[:, HALF:]], packed_dtype=BF16)


def _unpack_halves(w):
    return tuple(pltpu.unpack_elementwise(w, index=i, packed_dtype=BF16, unpacked_dtype=F32)
                 for i in range(2))


def _head_sumsq(t, e_ref):
    t2 = (t * t).astype(BF16)
    e = e_ref[...]
    outs = []
    for c in range(t.shape[1] // MXU_DIM):
        sl = slice(c * MXU_DIM, (c + 1) * MXU_DIM)
        outs.append(jnp.dot(t2[:, sl], e, preferred_element_type=F32))
    return outs[0] if len(outs) == 1 else jnp.concatenate(outs, axis=1)


def _norm_rope(t, gain, e_ref, cc, s_up, s_dn, out_scale):
    w = t.shape[1]
    ss = _head_sumsq(t, e_ref)
    tn = t * lax.rsqrt(ss * (1.0 / HEAD_DIM) + EPS) * gain
    reps = w // LANES
    up = pltpu.roll(tn, w - ROT_DIM // 2, axis=1)
    dn = pltpu.roll(tn, ROT_DIM // 2, axis=1)
    c = jnp.concatenate([cc] * reps, axis=1)
    su = jnp.concatenate([s_up] * reps, axis=1)
    sd = jnp.concatenate([s_dn] * reps, axis=1)
    return (tn * c + up * su + dn * sd) * out_scale


def _swap_head_pairs(t):
    w = t.shape[1]
    lane = lax.broadcasted_iota(jnp.int32, t.shape, 1)
    from_up = pltpu.roll(t, w - HEAD_DIM, axis=1)
    from_dn = pltpu.roll(t, HEAD_DIM, axis=1)
    return jnp.where((lane & (LANES - 1)) < HEAD_DIM, from_up, from_dn)


def _inproj_kernel(x_ref, g_ref, wq_ref, wk_ref, wv_ref, wu_ref, qn_ref, kn_ref,
                   cc_ref, su_ref, sd_ref, e_ref, wpool_ref, pscale_ref,
                   q_out, k_out, v_out, ksw_out, vsw_out, y_out, ubuf, *, tiles_per_seq):
    i = pl.program_id(0)
    tm = x_ref.shape[0]

    @pl.when(i % tiles_per_seq == 0)
    def _():
        ubuf[0:HALO, :] = jnp.zeros((HALO, POOL_WIDTH), F32)

    for r0 in range(0, tm, SUB_TILE):
        rows = slice(r0, r0 + SUB_TILE)
        x = x_ref[rows, :]
        ms = jnp.mean(x * x, axis=-1, keepdims=True)
        hn = (x * lax.rsqrt(ms + EPS) * g_ref[...]).astype(BF16)

        cc, s_up, s_dn = cc_ref[rows, :], su_ref[rows, :], sd_ref[rows, :]
        q = jnp.dot(hn, wq_ref[...], preferred_element_type=F32)
        q_out[rows, :] = _norm_rope(q, qn_ref[...], e_ref, cc, s_up, s_dn,
                                    HEAD_DIM ** -0.5).astype(BF16)
        k = jnp.dot(hn, wk_ref[...], preferred_element_type=F32)
        k = _norm_rope(k, kn_ref[...], e_ref, cc, s_up, s_dn, 1.0)
        v = jnp.dot(hn, wv_ref[...], preferred_element_type=F32)
        k_out[rows, :] = k.astype(BF16)
        v_out[rows, :] = v.astype(BF16)
        ksw_out[rows, :] = _swap_head_pairs(k).astype(BF16)
        vsw_out[rows, :] = _swap_head_pairs(v).astype(BF16)

        base = HALO + r0
        ubuf[base:base + SUB_TILE, :] = jnp.dot(hn, wu_ref[...], preferred_element_type=F32)
        pos = (i % tiles_per_seq) * tm + r0 + lax.broadcasted_iota(
            jnp.int32, (SUB_TILE, POOL_GROUP_DIM), 0)
        for g, w in enumerate(POOL_WINDOWS):
            cols = slice(g * POOL_GROUP_DIM, (g + 1) * POOL_GROUP_DIM)
            assert w & (w - 1) == 0 and w - 1 < HALO
            acc = ubuf[base - HALO:base + SUB_TILE, cols]
            shift = 1
            while shift < w:
                acc = acc + pltpu.roll(acc, shift, axis=0)
                shift *= 2
            acc = acc[HALO:, :]
            u_g = ubuf[base:base + SUB_TILE, cols]
            cnt = jnp.minimum(pos + 1, w).astype(F32)
            d = (acc / cnt - u_g).astype(BF16)
            y = jnp.dot(d, wpool_ref[g], preferred_element_type=F32)
            y_out[rows, cols] = (y * pscale_ref[:, cols]).astype(BF16)
    ubuf[0:HALO, :] = ubuf[tm:tm + HALO, :]


def _attn_kernel(sinks_ref, q_ref, kc_ref, kp_ref, kswc_ref, kswp_ref,
                 vc_ref, vp_ref, vswc_ref, vswp_ref, o_ref):
    j = pl.program_id(1)
    two = 2 * BLOCK
    row = lax.broadcasted_iota(jnp.int32, (two, two), 0) & (BLOCK - 1)
    col = lax.broadcasted_iota(jnp.int32, (two, two), 1)
    diff = row + BLOCK - col
    local = (diff >= 0) & (diff < WINDOW)
    lo_lanes = lax.broadcasted_iota(jnp.int32, (two, LANES), 1) < HEAD_DIM
    row_p = lax.broadcasted_iota(jnp.int32, (two, BLOCK), 0)
    diag = (row_p & (BLOCK - 1)) == lax.broadcasted_iota(jnp.int32, (two, BLOCK), 1)
    top_rows = row_p < BLOCK
    diag_top = diag & top_rows
    diag_bot = diag & jnp.logical_not(top_rows)
    zero = jnp.zeros((two, LANES), BF16)

    for sb in range(q_ref.shape[0] // BLOCK):
        rows = slice(sb * BLOCK, (sb + 1) * BLOCK)
        if sb == 0:
            def band(cur_ref, prev_ref):
                return jnp.concatenate([prev_ref[...], cur_ref[0:BLOCK, :]], axis=0)
            valid = local & ((col >= BLOCK) | (j > 0))
        else:
            def band(cur_ref, prev_ref, sb=sb):
                return cur_ref[(sb - 1) * BLOCK:(sb + 1) * BLOCK, :]
            valid = local
        k_nat, k_swp = band(kc_ref, kp_ref), band(kswc_ref, kswp_ref)
        v_nat, v_swp = band(vc_ref, vp_ref), band(vswc_ref, vswp_ref)

        for g in range(N_KV_HEADS):
            c0 = (g // 2) * LANES
            kcols = (k_nat[:, c0:c0 + LANES], k_swp[:, c0:c0 + LANES])
            vcols = (v_nat[:, c0:c0 + LANES], v_swp[:, c0:c0 + LANES])
            in_lo, in_hi = (0, 1) if g % 2 == 0 else (1, 0)
            q0 = g * GROUP * HEAD_DIM
            qq = jnp.concatenate([q_ref[rows, q0:q0 + LANES],
                                  q_ref[rows, q0 + LANES:q0 + 2 * LANES]], axis=0)
            halves = []
            for half, src in ((0, in_lo), (1, in_hi)):
                qm = jnp.where(lo_lanes, qq, zero) if half == 0 else jnp.where(lo_lanes, zero, qq)
                s = lax.dot_general(qm, kcols[src], (((1,), (1,)), ((), ())),
                                    preferred_element_type=F32)
                s = jnp.where(valid, s, NEG_BIG)
                h_top = g * GROUP + half
                s_prev = jnp.where(diag_top, sinks_ref[h_top],
                                   jnp.where(diag_bot, sinks_ref[h_top + 2], s[:, :BLOCK]))
                s_cur = s[:, BLOCK:]
                m = jnp.max(jnp.maximum(s_prev, s_cur), axis=-1, keepdims=True)
                p_prev = jnp.exp(s_prev - m)
                p_cur = jnp.exp(s_cur - m)
                denom = jnp.sum(p_prev + p_cur, axis=-1, keepdims=True)
                p = jnp.concatenate([jnp.where(diag, 0.0, p_prev), p_cur], axis=1)
                o = jnp.dot(p.astype(BF16), vcols[src], preferred_element_type=F32)
                halves.append(o / denom)
            o_pair = jnp.where(lo_lanes, halves[0], halves[1]).astype(BF16)
            o_ref[rows, q0:q0 + LANES] = o_pair[0:BLOCK, :]
            o_ref[rows, q0 + LANES:q0 + 2 * LANES] = o_pair[BLOCK:, :]


def _outproj_router_kernel(o_ref, y_ref, x_ref, woa_ref, wop_ref, g_ref, wr_ref,
                           br_ref, tri_ref, h_out, hpk_out, route_out, cnt_out):
    tm = x_ref.shape[0]
    h = (x_ref[...]
         + jnp.dot(o_ref[...], woa_ref[...], preferred_element_type=F32)
         + jnp.dot(y_ref[...], wop_ref[...], preferred_element_type=F32))
    h_out[...] = h
    ms = jnp.mean(h * h, axis=-1, keepdims=True)
    hn = h * lax.rsqrt(ms + EPS) * g_ref[...]
    hb = hn.astype(BF16)
    packed = _pack_halves(hn)
    for s in range(PK_ROWS):
        hpk_out[pl.ds(s, tm, stride=PK_ROWS), :] = packed[:, s * LANES:(s + 1) * LANES]

    both = jnp.dot(hb, wr_ref[...], preferred_element_type=F32)
    logits = (both[:, :LANES] + both[:, LANES:] + br_ref[...]).T
    assert EXPERTS_PER_GROUP == SUBLANES and ROUTER_FINE0 % SUBLANES == 0
    slot = lax.broadcasted_iota(jnp.int32, (SUBLANES, tm), 0).astype(F32)

    def col_max(t):
        return jnp.max(t, axis=0, keepdims=True)

    def first_slot_of(mask):
        return jnp.min(jnp.where(mask, slot, float(SUBLANES)), axis=0, keepdims=True)

    coarse = jnp.where(slot < N_EXPERT_GROUPS, logits[0:SUBLANES, :], NEG_BIG)
    cmax = col_max(coarse)
    grp = first_slot_of(coarse == cmax)
    csum = jnp.sum(jnp.exp(coarse - cmax), axis=0, keepdims=True)

    def fine_of(g):
        return logits[ROUTER_FINE0 + g * SUBLANES:ROUTER_FINE0 + (g + 1) * SUBLANES, :]

    fine = fine_of(N_EXPERT_GROUPS - 1)
    for g in range(N_EXPERT_GROUPS - 2, -1, -1):
        fine = jnp.where(grp == float(g), fine_of(g), fine)
    f1 = col_max(fine)
    i1 = first_slot_of(fine == f1)
    rest = jnp.where(slot == i1, NEG_BIG, fine)
    f2 = col_max(rest)
    i2 = first_slot_of(rest == f2)
    ratio = jnp.exp(f2 - f1)
    g1 = (1.0 / csum) / (1.0 + ratio)
    g2 = g1 * ratio
    e1 = grp * EXPERTS_PER_GROUP + i1
    e2 = grp * EXPERTS_PER_GROUP + i2

    expert = lax.broadcasted_iota(jnp.int32, (N_EXPERTS, tm), 0).astype(F32)
    oh1 = expert == e1
    oh2 = expert == e2
    oh = jnp.where(oh1, 1.0, 0.0) + jnp.where(oh2, 1.0, 0.0)
    rank = lax.dot_general(oh.astype(BF16), tri_ref[...], (((1,), (1,)), ((), ())),
                           preferred_element_type=F32)
    cnt = jnp.sum(oh, axis=1, keepdims=True)
    cnt_pad = jnp.floor((cnt + (CHUNK - 1)) * (1.0 / CHUNK)) * CHUNK
    padded_cols = jnp.concatenate(
        [jnp.broadcast_to(cnt_pad, (N_EXPERTS, LANES)),
         jnp.zeros((LANES - N_EXPERTS, LANES), F32)], axis=0).astype(BF16)
    run_start = jnp.dot(tri_ref[0:N_EXPERTS, 0:LANES], padded_cols,
                        preferred_element_type=F32)[:, 0:1]
    table = rank + run_start
    lp1 = jnp.sum(jnp.where(oh1, table, 0.0), axis=0, keepdims=True)
    lp2 = jnp.sum(jnp.where(oh2, table, 0.0), axis=0, keepdims=True)
    route_out[...] = jnp.where(slot == 0.0, g1,
                               jnp.where(slot == 1.0, g2,
                                         jnp.where(slot == 2.0, lp1,
                                                   jnp.where(slot == 3.0, lp2, 0.0))))
    cnt_out[...] = jnp.broadcast_to(cnt, cnt_out.shape)


def _expert_kernel(eid_ref, first_ref, nact_ref, off_ref, next_ref, used_ref,
                   hpk_hbm, wg_hbm, wu_hbm, wd_hbm, ys_out,
                   xbuf, wg_f, wu_f, wd_f, wg_s, wu_s, wd_s, sems, wsems):
    i = pl.program_id(0)
    tm = ys_out.shape[0] // PK_ROWS
    n_act = nact_ref[0]

    def weight_copies(e):
        return (pltpu.make_async_copy(wg_hbm.at[e], wg_f, wsems.at[0]),
                pltpu.make_async_copy(wu_hbm.at[e], wu_f, wsems.at[1]),
                pltpu.make_async_copy(wd_hbm.at[e], wd_f, wsems.at[2]))

    ch_rows = CHUNK * PK_ROWS
    n_ch = tm // CHUNK

    def gather(blk, slot):
        for c in range(n_ch):
            src = pl.multiple_of(off_ref[blk * n_ch + c], ch_rows)
            pltpu.make_async_copy(hpk_hbm.at[pl.ds(src, ch_rows)],
                                  xbuf.at[slot, pl.ds(c * ch_rows, ch_rows)],
                                  sems.at[slot]).start()

    @pl.when(i == 0)
    def _():
        for cp in weight_copies(eid_ref[0]):
            cp.start()
        gather(0, 0)

    @pl.when(i + 1 < n_act)
    def _():
        gather(i + 1, (i + 1) % 2)

    @pl.when(i < n_act)
    def _():
        slot = i % 2

        @pl.when(first_ref[i] == 1)
        def _():
            for cp in weight_copies(eid_ref[i]):
                cp.wait()
            wg_s[...] = wg_f[...].astype(BF16)
            wu_s[...] = wu_f[...].astype(BF16)
            wd_s[...] = wd_f[...].astype(BF16)

            @pl.when(next_ref[i] >= 0)
            def _():
                for cp in weight_copies(next_ref[i]):
                    cp.start()

        pltpu.make_async_copy(hpk_hbm.at[pl.ds(0, tm * PK_ROWS)], xbuf.at[slot],
                              sems.at[slot]).wait()

        def swiglu(rows):
            xu = jnp.concatenate(
                [xbuf[slot, pl.ds(s, rows, stride=PK_ROWS), :] for s in range(PK_ROWS)],
                axis=1)
            x_lo, x_hi = (v.astype(BF16) for v in _unpack_halves(xu))
            a = (jnp.dot(x_lo, wg_s[0:HALF, :], preferred_element_type=F32)
                 + jnp.dot(x_hi, wg_s[HALF:, :], preferred_element_type=F32))
            b = (jnp.dot(x_lo, wu_s[0:HALF, :], preferred_element_type=F32)
                 + jnp.dot(x_hi, wu_s[HALF:, :], preferred_element_type=F32))
            mid = (a * jax.nn.sigmoid(a) * b).astype(BF16)
            y = jnp.dot(mid, wd_s[...], preferred_element_type=F32)
            yw = _pack_halves(y)
            for s in range(PK_ROWS):
                ys_out[pl.ds(s, rows, stride=PK_ROWS), :] = yw[:, s * LANES:(s + 1) * LANES]
            if rows < tm:
                ys_out[rows * PK_ROWS:, :] = jnp.zeros(((tm - rows) * PK_ROWS, LANES),
                                                       ys_out.dtype)

        quarter = tm // 4
        for rows in range(quarter, tm + 1, quarter):
            @pl.when((used_ref[i] > rows - quarter) & (used_ref[i] <= rows))
            def _(rows=rows):
                swiglu(rows)

    @pl.when(i >= n_act)
    def _():
        ys_out[...] = jnp.zeros(ys_out.shape, ys_out.dtype)


def _local_sort_kernel(lp_ref, hpk_ref, xs_out):
    i = pl.program_id(0)
    t_tile = hpk_ref.shape[0] // PK_ROWS
    n_tok = pl.num_programs(0) * t_tile
    xs_out[...] = jnp.zeros(xs_out.shape, xs_out.dtype)

    def body(q, carry):
        for u in range(SUBLANES):
            t = q * SUBLANES + u
            row = hpk_ref[pl.ds(pl.multiple_of(t * PK_ROWS, PK_ROWS), PK_ROWS), :]
            for k in range(TOP_K):
                dst = pl.multiple_of(lp_ref[k * n_tok + i * t_tile + t], PK_ROWS)
                xs_out[pl.ds(dst, PK_ROWS), :] = row
        return carry
    lax.fori_loop(0, t_tile // SUBLANES, body, 0)


def _combine_kernel(gch_ref, lp_ref, gate_ref, h_ref, ys_hbm, out_ref, ylocal, acc, sems):
    i = pl.program_id(0)
    n = pl.num_programs(0)
    t_tile = h_ref.shape[0]
    n_tok = n * t_tile
    ch_rows = CHUNK * PK_ROWS
    n_ch = ylocal.shape[1] // ch_rows

    def chunk_copy(tile, slot, lc):
        src = pl.multiple_of(gch_ref[tile * n_ch + lc], ch_rows)
        dst = pl.multiple_of(lc * ch_rows, ch_rows)
        pltpu.make_async_copy(ys_hbm.at[pl.ds(src, ch_rows)],
                              ylocal.at[slot, pl.ds(dst, ch_rows)],
                              sems.at[slot]).start()

    def wait_slot(slot):
        pltpu.make_async_copy(ys_hbm.at[pl.ds(0, n_ch * ch_rows)], ylocal.at[slot],
                              sems.at[slot]).wait()

    @pl.when(i == 0)
    def _():
        def body(lc, carry):
            chunk_copy(0, 0, lc)
            return carry
        lax.fori_loop(0, n_ch, body, 0)

    slot = i % 2
    wait_slot(slot)

    nxt = jnp.minimum(i + 1, n - 1)
    tok_unroll = 4
    n_groups = t_tile // tok_unroll
    per_group = -(-n_ch // n_groups)
    issue_groups = n_ch // per_group
    assert issue_groups * per_group == n_ch and issue_groups <= n_groups

    def tok_body(q, carry, issue):
        if issue:
            for u in range(per_group):
                chunk_copy(nxt, 1 - slot, q * per_group + u)
        for u in range(tok_unroll):
            t = q * tok_unroll + u
            a0 = i * t_tile + t
            a1 = n_tok + a0
            p0 = pl.multiple_of(lp_ref[a0], PK_ROWS)
            p1 = pl.multiple_of(lp_ref[a1], PK_ROWS)
            lo0, hi0 = _unpack_halves(ylocal[slot, pl.ds(p0, PK_ROWS), :])
            lo1, hi1 = _unpack_halves(ylocal[slot, pl.ds(p1, PK_ROWS), :])
            g0 = gate_ref[a0]
            g1 = gate_ref[a1]
            dst = pl.multiple_of(t * PK_ROWS, PK_ROWS)
            acc[0, pl.ds(dst, PK_ROWS), :] = g0 * lo0 + g1 * lo1
            acc[1, pl.ds(dst, PK_ROWS), :] = g0 * hi0 + g1 * hi1
        return carry
    lax.fori_loop(0, issue_groups, functools.partial(tok_body, issue=True), 0)
    lax.fori_loop(issue_groups, n_groups, functools.partial(tok_body, issue=False), 0)

    @pl.when(i == n - 1)
    def _():
        wait_slot(1 - slot)

    for half in range(2):
        for s in range(PK_ROWS):
            c0 = half * HALF + s * LANES
            out_ref[:, c0:c0 + LANES] = (h_ref[:, c0:c0 + LANES]
                                         + acc[half, pl.ds(s, t_tile, stride=PK_ROWS), :])


def _rope_tables(seq):
    pos = jnp.arange(seq, dtype=F32)
    inv_freq = ROPE_THETA ** (-jnp.arange(0, ROT_DIM, 2, dtype=F32) / ROT_DIM)
    ang = pos[:, None] * inv_freq[None, :]
    cos, sin = jnp.cos(ang), jnp.sin(ang)
    half = ROT_DIM // 2
    ones = jnp.ones((seq, HEAD_DIM - ROT_DIM), F32)
    zeros = jnp.zeros((seq, HEAD_DIM - ROT_DIM), F32)
    zh = jnp.zeros((seq, half), F32)
    cc = jnp.concatenate([cos, cos, ones], axis=1)
    s_up = jnp.concatenate([-sin, zh, zeros], axis=1)
    s_dn = jnp.concatenate([zh, sin, zeros], axis=1)
    rep = LANES // HEAD_DIM
    return tuple(jnp.tile(t, (1, rep)) for t in (cc, s_up, s_dn))


def _plan(cnt, n_tokens):
    nt = n_tokens // LOCAL_TILE
    experts = jnp.arange(N_EXPERTS, dtype=jnp.int32)
    cntp = ((cnt + CHUNK - 1) // CHUNK) * CHUNK
    loff_end = jnp.cumsum(cntp, axis=1)
    loff = loff_end - cntp
    rows_e = jnp.sum(cntp, axis=0)
    rows_pad = ((rows_e + EXPERT_TILE - 1) // EXPERT_TILE) * EXPERT_TILE
    g_end = jnp.cumsum(rows_pad)
    g_start = g_end - rows_pad
    gpos = g_start[None, :] + jnp.cumsum(cntp, axis=0) - cntp

    lrow = jnp.arange(LT_MAX // CHUNK, dtype=jnp.int32) * CHUNK
    e_lc = jnp.sum((lrow[None, :, None] >= loff_end[:, None, :]).astype(jnp.int32), axis=-1)
    sel = e_lc[:, :, None] == experts
    delta = jnp.sum(jnp.where(sel, (gpos - loff)[:, None, :], 0), axis=-1)
    gch = jnp.where(e_lc < N_EXPERTS, (delta + lrow[None, :]) // CHUNK, 0).reshape(-1)

    n_rows = TOP_K * n_tokens + nt * N_EXPERTS * (CHUNK - 1) + N_EXPERTS * (EXPERT_TILE - CHUNK)
    n_blocks = -(-n_rows // EXPERT_TILE)
    n_rows = n_blocks * EXPERT_TILE
    blk_start = jnp.arange(n_blocks, dtype=jnp.int32) * EXPERT_TILE
    blk_eid = jnp.minimum(
        jnp.sum((blk_start[:, None] >= g_end[None, :]).astype(jnp.int32), axis=-1),
        N_EXPERTS - 1)
    prev = jnp.concatenate([jnp.full((1,), -1, jnp.int32), blk_eid[:-1]])
    first = (blk_eid != prev).astype(jnp.int32)
    n_act = (g_end[-1] // EXPERT_TILE).astype(jnp.int32).reshape(1)
    later = (blk_eid[None, :] > blk_eid[:, None]) & (jnp.arange(n_blocks)[None, :] < n_act)
    next_eid = jnp.min(jnp.where(later, blk_eid[None, :], N_EXPERTS), axis=1)
    next_eid = jnp.where(next_eid < N_EXPERTS, next_eid, -1).astype(jnp.int32)
    of_blk = blk_eid[:, None] == experts
    blk_end = jnp.sum(jnp.where(of_blk, (g_start + rows_e)[None, :], 0), axis=1)
    used = jnp.clip(blk_end - blk_start, 0, EXPERT_TILE).astype(jnp.int32)

    cpb = EXPERT_TILE // CHUNK
    run_start = gpos.T // CHUNK
    run_len = cntp.T // CHUNK
    run_src = (jnp.arange(nt, dtype=jnp.int32)[:, None] * (LT_MAX // CHUNK) + loff // CHUNK).T
    tabs = jnp.stack([run_start, run_len, run_src], axis=0)
    dtabs = tabs - jnp.concatenate(
        [jnp.zeros((3, N_EXPERTS, 1), jnp.int32), tabs[:, :, :-1]], axis=2)
    blk_sel = (blk_eid[:, None] == experts)[:, :, None]
    blk_start_t = jnp.sum(jnp.where(blk_sel, run_start[None], 0), axis=1)
    blk_dtabs = jnp.sum(jnp.where(blk_sel[None], dtabs[:, None], 0), axis=2)
    gc = jnp.arange(n_rows // CHUNK, dtype=jnp.int32).reshape(n_blocks, cpb)
    started = blk_start_t[:, None, :] <= gc[:, :, None]
    picked = jnp.sum(jnp.where(started[None], blk_dtabs[:, :, None, :], 0), axis=-1)
    within = gc - picked[0]
    csrc = jnp.where(within < picked[1], picked[2] + within, 0).reshape(-1)
    return (gch.astype(jnp.int32), csrc.astype(jnp.int32), blk_eid, first, next_eid, used,
            n_act, n_rows)


def kernel(x, norm_mix, w_in, q_norm, k_norm, sinks, w_pool, pool_scale, w_out,
           norm_ffn, w_coarse, b_coarse, w_fine, b_fine, w_gate, w_up, w_down):
    bsz, seq, d = x.shape
    n = bsz * seq
    assert d == D_MODEL and seq % ROW_TILE == 0 and seq % BLOCK == 0
    assert norm_mix.shape[0] == 1, "single-layer problem"
    xf = x.reshape(n, d)

    w_in_b = w_in[0].astype(BF16)
    wq = w_in_b[:, :ATTN_WIDTH]
    wk = w_in_b[:, ATTN_WIDTH:ATTN_WIDTH + KV_WIDTH]
    wv = w_in_b[:, ATTN_WIDTH + KV_WIDTH:ATTN_WIDTH + 2 * KV_WIDTH]
    wu = w_in_b[:, ATTN_WIDTH + 2 * KV_WIDTH:]
    qn = jnp.tile(q_norm[0], N_HEADS).reshape(1, ATTN_WIDTH)
    kn = jnp.tile(k_norm[0], N_KV_HEADS).reshape(1, KV_WIDTH)
    cc, s_up, s_dn = _rope_tables(seq)
    lane_head = jnp.arange(MXU_DIM) // HEAD_DIM
    e_mat = (lane_head[:, None] == lane_head[None, :]).astype(BF16)
    w_pool_b = w_pool[0].astype(BF16)
    pscale = pool_scale[0].reshape(1, POOL_WIDTH)
    w_out_b = w_out[0].astype(BF16)
    wo_attn, wo_pool = w_out_b[:ATTN_WIDTH], w_out_b[ATTN_WIDTH:]
    gap = ROUTER_FINE0 - N_EXPERT_GROUPS
    tail = LANES - ROUTER_FINE0 - N_EXPERTS
    w_r = jnp.concatenate([w_coarse[0], jnp.zeros((d, gap), F32), w_fine[0],
                           jnp.zeros((d, tail), F32)], axis=1)
    w_r_hi = w_r.astype(BF16)
    w_r_lo = (w_r - w_r_hi.astype(F32)).astype(BF16)
    w_r2 = jnp.concatenate([w_r_hi, w_r_lo], axis=1)
    b_r = jnp.concatenate([b_coarse[0], jnp.zeros((gap,), F32), b_fine[0],
                           jnp.zeros((tail,), F32)]).reshape(1, LANES)

    tm = ROW_TILE
    tiles_per_seq = seq // tm
    n_tiles = n // tm
    idx = jnp.arange(SUB_TILE)
    tri = (idx[None, :] < idx[:, None]).astype(BF16)
    const = lambda *_: (0, 0)
    row_blk = lambda i: (i, 0)

    q, k, v, ksw, vsw, y = pl.pallas_call(
        functools.partial(_inproj_kernel, tiles_per_seq=tiles_per_seq),
        grid=(n_tiles,),
        in_specs=[
            pl.BlockSpec((tm, d), row_blk),
            pl.BlockSpec((1, d), const),
            pl.BlockSpec((d, ATTN_WIDTH), const),
            pl.BlockSpec((d, KV_WIDTH), const),
            pl.BlockSpec((d, KV_WIDTH), const),
            pl.BlockSpec((d, POOL_WIDTH), const),
            pl.BlockSpec((1, ATTN_WIDTH), const),
            pl.BlockSpec((1, KV_WIDTH), const),
            pl.BlockSpec((tm, LANES), lambda i: (i % tiles_per_seq, 0)),
            pl.BlockSpec((tm, LANES), lambda i: (i % tiles_per_seq, 0)),
            pl.BlockSpec((tm, LANES), lambda i: (i % tiles_per_seq, 0)),
            pl.BlockSpec((MXU_DIM, MXU_DIM), const),
            pl.BlockSpec((len(POOL_WINDOWS), POOL_GROUP_DIM, POOL_GROUP_DIM),
                         lambda i: (0, 0, 0)),
            pl.BlockSpec((1, POOL_WIDTH), const),
        ],
        out_specs=[
            pl.BlockSpec((tm, ATTN_WIDTH), row_blk),
            pl.BlockSpec((tm, KV_WIDTH), row_blk),
            pl.BlockSpec((tm, KV_WIDTH), row_blk),
            pl.BlockSpec((tm, KV_WIDTH), row_blk),
            pl.BlockSpec((tm, KV_WIDTH), row_blk),
            pl.BlockSpec((tm, POOL_WIDTH), row_blk),
        ],
        out_shape=[
            jax.ShapeDtypeStruct((n, ATTN_WIDTH), BF16),
            jax.ShapeDtypeStruct((n, KV_WIDTH), BF16),
            jax.ShapeDtypeStruct((n, KV_WIDTH), BF16),
            jax.ShapeDtypeStruct((n, KV_WIDTH), BF16),
            jax.ShapeDtypeStruct((n, KV_WIDTH), BF16),
            jax.ShapeDtypeStruct((n, POOL_WIDTH), BF16),
        ],
        scratch_shapes=[pltpu.VMEM((HALO + tm, POOL_WIDTH), F32)],
        compiler_params=_cparams(),
        name="inproj",
    )(xf, norm_mix[0].reshape(1, d), wq, wk, wv, wu, qn, kn, cc, s_up, s_dn,
      e_mat, w_pool_b, pscale)

    nb = seq // BLOCK
    nq = nb // Q_BLOCKS
    qrows = Q_BLOCKS * BLOCK
    cur = lambda b, j, *_: (b * nq + j, 0)
    prv = lambda b, j, *_: (b * nb + jnp.maximum(Q_BLOCKS * j - 1, 0), 0)
    kv_specs = [pl.BlockSpec((qrows, KV_WIDTH), cur), pl.BlockSpec((BLOCK, KV_WIDTH), prv)]
    attn = pl.pallas_call(
        _attn_kernel,
        grid_spec=pltpu.PrefetchScalarGridSpec(
            num_scalar_prefetch=1,
            grid=(bsz, nq),
            in_specs=[pl.BlockSpec((qrows, ATTN_WIDTH), cur)] + kv_specs * 4,
            out_specs=pl.BlockSpec((qrows, ATTN_WIDTH), cur),
        ),
        out_shape=jax.ShapeDtypeStruct((n, ATTN_WIDTH), BF16),
        compiler_params=_cparams(2),
        name="swa_attn",
    )(sinks[0].astype(F32), q, k, k, ksw, ksw, v, v, vsw, vsw)

    h, hpk, route, cnt = pl.pallas_call(
        _outproj_router_kernel,
        grid=(n_tiles,),
        in_specs=[
            pl.BlockSpec((tm, ATTN_WIDTH), row_blk),
            pl.BlockSpec((tm, POOL_WIDTH), row_blk),
            pl.BlockSpec((tm, d), row_blk),
            pl.BlockSpec((ATTN_WIDTH, d), const),
            pl.BlockSpec((POOL_WIDTH, d), const),
            pl.BlockSpec((1, d), const),
            pl.BlockSpec((d, 2 * LANES), const),
            pl.BlockSpec((1, LANES), const),
            pl.BlockSpec((SUB_TILE, SUB_TILE), const),
        ],
        out_specs=[
            pl.BlockSpec((tm, d), row_blk),
            pl.BlockSpec((tm * PK_ROWS, LANES), row_blk),
            pl.BlockSpec((SUBLANES, tm), lambda i: (0, i)),
            pl.BlockSpec((N_EXPERTS, LANES), row_blk),
        ],
        out_shape=[
            jax.ShapeDtypeStruct((n, d), F32),
            jax.ShapeDtypeStruct((n * PK_ROWS, LANES), jnp.int32),
            jax.ShapeDtypeStruct((SUBLANES, n), F32),
            jax.ShapeDtypeStruct((n_tiles * N_EXPERTS, LANES), F32),
        ],
        compiler_params=_cparams(),
        name="outproj_router",
    )(attn, y, xf, wo_attn, wo_pool, norm_ffn[0].reshape(1, d), w_r2, b_r, tri)

    assert tm == LOCAL_TILE
    gates = route[0:TOP_K].reshape(TOP_K * n)
    lp = route[TOP_K:2 * TOP_K].astype(jnp.int32).reshape(TOP_K * n)
    tile_cnt = cnt.reshape(n_tiles, N_EXPERTS, LANES)[:, :, 0].astype(jnp.int32)
    te = EXPERT_TILE
    gch, csrc, blk_eid, first, next_eid, used, n_act, n_rows = _plan(tile_cnt, n)
    n_blocks = n_rows // te
    n_local = n // LOCAL_TILE

    xs = pl.pallas_call(
        _local_sort_kernel,
        grid_spec=pltpu.PrefetchScalarGridSpec(
            num_scalar_prefetch=1,
            grid=(n_local,),
            in_specs=[pl.BlockSpec((LOCAL_TILE * PK_ROWS, LANES), lambda i, *_: (i, 0))],
            out_specs=pl.BlockSpec((LT_MAX * PK_ROWS, LANES), lambda i, *_: (i, 0)),
        ),
        out_shape=jax.ShapeDtypeStruct((n_local * LT_MAX * PK_ROWS, LANES), jnp.int32),
        compiler_params=_cparams(),
        name="local_sort",
    )(lp * PK_ROWS, hpk)

    ys = pl.pallas_call(
        _expert_kernel,
        grid_spec=pltpu.PrefetchScalarGridSpec(
            num_scalar_prefetch=6,
            grid=(n_blocks,),
            in_specs=[pl.BlockSpec(memory_space=pl.ANY)] * 4,
            out_specs=pl.BlockSpec((te * PK_ROWS, LANES), lambda i, *_: (i, 0)),
            scratch_shapes=[
                pltpu.VMEM((2, te * PK_ROWS, LANES), jnp.int32),
                pltpu.VMEM((d, D_EXPERT), F32),
                pltpu.VMEM((d, D_EXPERT), F32),
                pltpu.VMEM((D_EXPERT, d), F32),
                pltpu.VMEM((d, D_EXPERT), BF16),
                pltpu.VMEM((d, D_EXPERT), BF16),
                pltpu.VMEM((D_EXPERT, d), BF16),
                pltpu.SemaphoreType.DMA((2,)),
                pltpu.SemaphoreType.DMA((3,)),
            ],
        ),
        out_shape=jax.ShapeDtypeStruct((n_rows * PK_ROWS, LANES), jnp.int32),
        compiler_params=_cparams(),
        name="experts",
    )(blk_eid, first, n_act, csrc * (CHUNK * PK_ROWS), next_eid, used, xs,
      w_gate[0], w_up[0], w_down[0])

    tt = LOCAL_TILE
    out = pl.pallas_call(
        _combine_kernel,
        grid_spec=pltpu.PrefetchScalarGridSpec(
            num_scalar_prefetch=3,
            grid=(n_local,),
            in_specs=[
                pl.BlockSpec((tt, d), lambda i, *_: (i, 0)),
                pl.BlockSpec(memory_space=pl.ANY),
            ],
            out_specs=pl.BlockSpec((tt, d), lambda i, *_: (i, 0)),
            scratch_shapes=[
                pltpu.VMEM((2, LT_MAX * PK_ROWS, LANES), jnp.int32),
                pltpu.VMEM((2, tt * PK_ROWS, LANES), F32),
                pltpu.SemaphoreType.DMA((2,)),
            ],
        ),
        out_shape=jax.ShapeDtypeStruct((n, d), F32),
        compiler_params=_cparams(),
        name="combine",
    )(gch * (CHUNK * PK_ROWS), lp * PK_ROWS, gates, h, ys)
    return out.reshape(bsz, seq, d)
```

```python
import functools

import jax
import jax.numpy as jnp
from jax import lax
from jax.experimental import pallas as pl
from jax.experimental.pallas import tpu as pltpu

F32 = jnp.float32
BF16 = jnp.bfloat16

D_MODEL = 2048
N_HEADS = 16
N_KV_HEADS = 4
HEAD_DIM = 64
GROUP = N_HEADS // N_KV_HEADS
ROT_DIM = HEAD_DIM // 4
ROPE_THETA = 500000.0
WINDOW = 128
BLOCK = 128
Q_BLOCKS = 8
ATTN_WIDTH = N_HEADS * HEAD_DIM
KV_WIDTH = N_KV_HEADS * HEAD_DIM
POOL_WINDOWS = (2, 4, 8, 16)
POOL_WIDTH = D_MODEL // 2
POOL_GROUP_DIM = POOL_WIDTH // len(POOL_WINDOWS)
N_EXPERT_GROUPS = 4
EXPERTS_PER_GROUP = 8
N_EXPERTS = N_EXPERT_GROUPS * EXPERTS_PER_GROUP
TOP_K = 2
D_EXPERT = 512
EPS = 1e-6

LANES = 128
SUBLANES = 8
MXU_DIM = 256
HALO = 16
NEG_BIG = -1e30
ROUTER_FINE0 = 8
HALF = D_MODEL // 2
PK_ROWS = HALF // LANES

ROW_TILE = 512
SUB_TILE = 512
EXPERT_TILE = 512
LOCAL_TILE = 512
CHUNK = 8
LT_MAX = TOP_K * LOCAL_TILE + N_EXPERTS * (CHUNK - 1)
VMEM_LIMIT = 56 * 1024 * 1024


def _cparams(n_axes=1):
    return pltpu.CompilerParams(
        dimension_semantics=("arbitrary",) * n_axes,
        vmem_limit_bytes=VMEM_LIMIT,
    )


def _pack_halves(t):
    return pltpu.pack_elementwise([t[:, :HALF], t[:, HALF:]], packed_dtype=BF16)


def _unpack_halves(w):
    return tuple(pltpu.unpack_elementwise(w, index=i, packed_dtype=BF16, unpacked_dtype=F32)
                 for i in range(2))


def _head_sumsq(t, e_ref):
    t2 = (t * t).astype(BF16)
    e = e_ref[...]
    outs = []
    for c in range(t.shape[1] // MXU_DIM):
        sl = slice(c * MXU_DIM, (c + 1) * MXU_DIM)
        outs.append(jnp.dot(t2[:, sl], e, preferred_element_type=F32))
    return outs[0] if len(outs) == 1 else jnp.concatenate(outs, axis=1)


def _norm_rope(t, gain, e_ref, cc, s_up, s_dn, out_scale):
    w = t.shape[1]
    ss = _head_sumsq(t, e_ref)
    tn = t * lax.rsqrt(ss * (1.0 / HEAD_DIM) + EPS) * gain
    reps = w // LANES
    up = pltpu.roll(tn, w - ROT_DIM // 2, axis=1)
    dn = pltpu.roll(tn, ROT_DIM // 2, axis=1)
    c = jnp.concatenate([cc] * reps, axis=1)
    su = jnp.concatenate([s_up] * reps, axis=1)
    sd = jnp.concatenate([s_dn] * reps, axis=1)
    return (tn * c + up * su + dn * sd) * out_scale


def _swap_head_pairs(t):
    w = t.shape[1]
    lane = lax.broadcasted_iota(jnp.int32, t.shape, 1)
    from_up = pltpu.roll(t, w - HEAD_DIM, axis=1)
    from_dn = pltpu.roll(t, HEAD_DIM, axis=1)
    return jnp.where((lane & (LANES - 1)) < HEAD_DIM, from_up, from_dn)


def _inproj_kernel(x_ref, g_ref, wq_ref, wk_ref, wv_ref, wu_ref, qn_ref, kn_ref,
                   cc_ref, su_ref, sd_ref, e_ref, wpool_ref, pscale_ref,
                   q_out, k_out, v_out, ksw_out, vsw_out, y_out, ubuf, *, tiles_per_seq):
    i = pl.program_id(0)
    tm = x_ref.shape[0]

    @pl.when(i % tiles_per_seq == 0)
    def _():
        ubuf[0:HALO, :] = jnp.zeros((HALO, POOL_WIDTH), F32)

    for r0 in range(0, tm, SUB_TILE):
        rows = slice(r0, r0 + SUB_TILE)
        x = x_ref[rows, :]
        ms = jnp.mean(x * x, axis=-1, keepdims=True)
        hn = (x * lax.rsqrt(ms + EPS) * g_ref[...]).astype(BF16)

        cc, s_up, s_dn = cc_ref[rows, :], su_ref[rows, :], sd_ref[rows, :]
        q = jnp.dot(hn, wq_ref[...], preferred_element_type=F32)
        q_out[rows, :] = _norm_rope(q, qn_ref[...], e_ref, cc, s_up, s_dn,
                                    HEAD_DIM ** -0.5).astype(BF16)
        k = jnp.dot(hn, wk_ref[...], preferred_element_type=F32)
        k = _norm_rope(k, kn_ref[...], e_ref, cc, s_up, s_dn, 1.0)
        v = jnp.dot(hn, wv_ref[...], preferred_element_type=F32)
        k_out[rows, :] = k.astype(BF16)
        v_out[rows, :] = v.astype(BF16)
        ksw_out[rows, :] = _swap_head_pairs(k).astype(BF16)
        vsw_out[rows, :] = _swap_head_pairs(v).astype(BF16)

        base = HALO + r0
        ubuf[base:base + SUB_TILE, :] = jnp.dot(hn, wu_ref[...], preferred_element_type=F32)
        pos = (i % tiles_per_seq) * tm + r0 + lax.broadcasted_iota(
            jnp.int32, (SUB_TILE, POOL_GROUP_DIM), 0)
        for g, w in enumerate(POOL_WINDOWS):
            cols = slice(g * POOL_GROUP_DIM, (g + 1) * POOL_GROUP_DIM)
            assert w & (w - 1) == 0 and w - 1 < HALO
            acc = ubuf[base - HALO:base + SUB_TILE, cols]
            shift = 1
            while shift < w:
                acc = acc + pltpu.roll(acc, shift, axis=0)
                shift *= 2
            acc = acc[HALO:, :]
            u_g = ubuf[base:base + SUB_TILE, cols]
            cnt = jnp.minimum(pos + 1, w).astype(F32)
            d = (acc / cnt - u_g).astype(BF16)
            y = jnp.dot(d, wpool_ref[g], preferred_element_type=F32)
            y_out[rows, cols] = (y * pscale_ref[:, cols]).astype(BF16)
    ubuf[0:HALO, :] = ubuf[tm:tm + HALO, :]


def _attn_kernel(sinks_ref, q_ref, kc_ref, kp_ref, kswc_ref, kswp_ref,
                 vc_ref, vp_ref, vswc_ref, vswp_ref, o_ref):
    j = pl.program_id(1)
    two = 2 * BLOCK
    row = lax.broadcasted_iota(jnp.int32, (two, two), 0) & (BLOCK - 1)
    col = lax.broadcasted_iota(jnp.int32, (two, two), 1)
    diff = row + BLOCK - col
    local = (diff >= 0) & (diff < WINDOW)
    lo_lanes = lax.broadcasted_iota(jnp.int32, (two, LANES), 1) < HEAD_DIM
    row_p = lax.broadcasted_iota(jnp.int32, (two, BLOCK), 0)
    diag = (row_p & (BLOCK - 1)) == lax.broadcasted_iota(jnp.int32, (two, BLOCK), 1)
    top_rows = row_p < BLOCK
    diag_top = diag & top_rows
    diag_bot = diag & jnp.logical_not(top_rows)
    zero = jnp.zeros((two, LANES), BF16)

    for sb in range(q_ref.shape[0] // BLOCK):
        rows = slice(sb * BLOCK, (sb + 1) * BLOCK)
        if sb == 0:
            def band(cur_ref, prev_ref):
                return jnp.concatenate([prev_ref[...], cur_ref[0:BLOCK, :]], axis=0)
            valid = local & ((col >= BLOCK) | (j > 0))
        else:
            def band(cur_ref, prev_ref, sb=sb):
                return cur_ref[(sb - 1) * BLOCK:(sb + 1) * BLOCK, :]
            valid = local
        k_nat, k_swp = band(kc_ref, kp_ref), band(kswc_ref, kswp_ref)
        v_nat, v_swp = band(vc_ref, vp_ref), band(vswc_ref, vswp_ref)

        for g in range(N_KV_HEADS):
            c0 = (g // 2) * LANES
            kcols = (k_nat[:, c0:c0 + LANES], k_swp[:, c0:c0 + LANES])
            vcols = (v_nat[:, c0:c0 + LANES], v_swp[:, c0:c0 + LANES])
            in_lo, in_hi = (0, 1) if g % 2 == 0 else (1, 0)
            q0 = g * GROUP * HEAD_DIM
            qq = jnp.concatenate([q_ref[rows, q0:q0 + LANES],
                                  q_ref[rows, q0 + LANES:q0 + 2 * LANES]], axis=0)
            halves = []
            for half, src in ((0, in_lo), (1, in_hi)):
                qm = jnp.where(lo_lanes, qq, zero) if half == 0 else jnp.where(lo_lanes, zero, qq)
                s = lax.dot_general(qm, kcols[src], (((1,), (1,)), ((), ())),
                                    preferred_element_type=F32)
                s = jnp.where(valid, s, NEG_BIG)
                h_top = g * GROUP + half
                s_prev = jnp.where(diag_top, sinks_ref[h_top],
                                   jnp.where(diag_bot, sinks_ref[h_top + 2], s[:, :BLOCK]))
                s_cur = s[:, BLOCK:]
                m = jnp.max(jnp.maximum(s_prev, s_cur), axis=-1, keepdims=True)
                p_prev = jnp.exp(s_prev - m)
                p_cur = jnp.exp(s_cur - m)
                denom = jnp.sum(p_prev + p_cur, axis=-1, keepdims=True)
                p = jnp.concatenate([jnp.where(diag, 0.0, p_prev), p_cur], axis=1)
                o = jnp.dot(p.astype(BF16), vcols[src], preferred_element_type=F32)
                halves.append(o / denom)
            o_pair = jnp.where(lo_lanes, halves[0], halves[1]).astype(BF16)
            o_ref[rows, q0:q0 + LANES] = o_pair[0:BLOCK, :]
            o_ref[rows, q0 + LANES:q0 + 2 * LANES] = o_pair[BLOCK:, :]


def _outproj_router_kernel(o_ref, y_ref, x_ref, woa_ref, wop_ref, g_ref, wr_ref,
                           br_ref, tri_ref, h_out, hpk_out, route_out, cnt_out):
    tm = x_ref.shape[0]
    h = (x_ref[...]
         + jnp.dot(o_ref[...], woa_ref[...], preferred_element_type=F32)
         + jnp.dot(y_ref[...], wop_ref[...], preferred_element_type=F32))
    h_out[...] = h
    ms = jnp.mean(h * h, axis=-1, keepdims=True)
    hn = h * lax.rsqrt(ms + EPS) * g_ref[...]
    hb = hn.astype(BF16)
    packed = _pack_halves(hn)
    for s in range(PK_ROWS):
        hpk_out[pl.ds(s, tm, stride=PK_ROWS), :] = packed[:, s * LANES:(s + 1) * LANES]

    both = jnp.dot(hb, wr_ref[...], preferred_element_type=F32)
    logits = (both[:, :LANES] + both[:, LANES:] + br_ref[...]).T
    assert EXPERTS_PER_GROUP == SUBLANES and ROUTER_FINE0 % SUBLANES == 0
    slot = lax.broadcasted_iota(jnp.int32, (SUBLANES, tm), 0).astype(F32)

    def col_max(t):
        return jnp.max(t, axis=0, keepdims=True)

    def first_slot_of(mask):
        return jnp.min(jnp.where(mask, slot, float(SUBLANES)), axis=0, keepdims=True)

    coarse = jnp.where(slot < N_EXPERT_GROUPS, logits[0:SUBLANES, :], NEG_BIG)
    cmax = col_max(coarse)
    grp = first_slot_of(coarse == cmax)
    csum = jnp.sum(jnp.exp(coarse - cmax), axis=0, keepdims=True)

    def fine_of(g):
        return logits[ROUTER_FINE0 + g * SUBLANES:ROUTER_FINE0 + (g + 1) * SUBLANES, :]

    fine = fine_of(N_EXPERT_GROUPS - 1)
    for g in range(N_EXPERT_GROUPS - 2, -1, -1):
        fine = jnp.where(grp == float(g), fine_of(g), fine)
    f1 = col_max(fine)
    i1 = first_slot_of(fine == f1)
    rest = jnp.where(slot == i1, NEG_BIG, fine)
    f2 = col_max(rest)
    i2 = first_slot_of(rest == f2)
    ratio = jnp.exp(f2 - f1)
    g1 = (1.0 / csum) / (1.0 + ratio)
    g2 = g1 * ratio
    e1 = grp * EXPERTS_PER_GROUP + i1
    e2 = grp * EXPERTS_PER_GROUP + i2

    expert = lax.broadcasted_iota(jnp.int32, (N_EXPERTS, tm), 0).astype(F32)
    oh1 = expert == e1
    oh2 = expert == e2
    oh = jnp.where(oh1, 1.0, 0.0) + jnp.where(oh2, 1.0, 0.0)
    rank = lax.dot_general(oh.astype(BF16), tri_ref[...], (((1,), (1,)), ((), ())),
                           preferred_element_type=F32)
    cnt = jnp.sum(oh, axis=1, keepdims=True)
    cnt_pad = jnp.floor((cnt + (CHUNK - 1)) * (1.0 / CHUNK)) * CHUNK
    padded_cols = jnp.concatenate(
        [jnp.broadcast_to(cnt_pad, (N_EXPERTS, LANES)),
         jnp.zeros((LANES - N_EXPERTS, LANES), F32)], axis=0).astype(BF16)
    run_start = jnp.dot(tri_ref[0:N_EXPERTS, 0:LANES], padded_cols,
                        preferred_element_type=F32)[:, 0:1]
    table = rank + run_start
    lp1 = jnp.sum(jnp.where(oh1, table, 0.0), axis=0, keepdims=True)
    lp2 = jnp.sum(jnp.where(oh2, table, 0.0), axis=0, keepdims=True)
    route_out[...] = jnp.where(slot == 0.0, g1,
                               jnp.where(slot == 1.0, g2,
                                         jnp.where(slot == 2.0, lp1,
                                                   jnp.where(slot == 3.0, lp2, 0.0))))
    cnt_out[...] = jnp.broadcast_to(cnt, cnt_out.shape)


def _expert_kernel(eid_ref, first_ref, nact_ref, off_ref, next_ref, used_ref,
                   hpk_hbm, wg_hbm, wu_hbm, wd_hbm, ys_out,
                   xbuf, wg_f, wu_f, wd_f, wg_s, wu_s, wd_s, sems, wsems):
    i = pl.program_id(0)
    tm = ys_out.shape[0] // PK_ROWS
    n_act = nact_ref[0]

    def weight_copies(e):
        return (pltpu.make_async_copy(wg_hbm.at[e], wg_f, wsems.at[0]),
                pltpu.make_async_copy(wu_hbm.at[e], wu_f, wsems.at[1]),
                pltpu.make_async_copy(wd_hbm.at[e], wd_f, wsems.at[2]))

    ch_rows = CHUNK * PK_ROWS
    n_ch = tm // CHUNK

    def gather(blk, slot):
        for c in range(n_ch):
            src = pl.multiple_of(off_ref[blk * n_ch + c], ch_rows)
            pltpu.make_async_copy(hpk_hbm.at[pl.ds(src, ch_rows)],
                                  xbuf.at[slot, pl.ds(c * ch_rows, ch_rows)],
                                  sems.at[slot]).start()

    @pl.when(i == 0)
    def _():
        for cp in weight_copies(eid_ref[0]):
            cp.start()
        gather(0, 0)

    @pl.when(i + 1 < n_act)
    def _():
        gather(i + 1, (i + 1) % 2)

    @pl.when(i < n_act)
    def _():
        slot = i % 2

        @pl.when(first_ref[i] == 1)
        def _():
            for cp in weight_copies(eid_ref[i]):
                cp.wait()
            wg_s[...] = wg_f[...].astype(BF16)
            wu_s[...] = wu_f[...].astype(BF16)
            wd_s[...] = wd_f[...].astype(BF16)

            @pl.when(next_ref[i] >= 0)
            def _():
                for cp in weight_copies(next_ref[i]):
                    cp.start()

        pltpu.make_async_copy(hpk_hbm.at[pl.ds(0, tm * PK_ROWS)], xbuf.at[slot],
                              sems.at[slot]).wait()

        def swiglu(rows):
            xu = jnp.concatenate(
                [xbuf[slot, pl.ds(s, rows, stride=PK_ROWS), :] for s in range(PK_ROWS)],
                axis=1)
            x_lo, x_hi = (v.astype(BF16) for v in _unpack_halves(xu))
            a = (jnp.dot(x_lo, wg_s[0:HALF, :], preferred_element_type=F32)
                 + jnp.dot(x_hi, wg_s[HALF:, :], preferred_element_type=F32))
            b = (jnp.dot(x_lo, wu_s[0:HALF, :], preferred_element_type=F32)
                 + jnp.dot(x_hi, wu_s[HALF:, :], preferred_element_type=F32))
            mid = (a * jax.nn.sigmoid(a) * b).astype(BF16)
            y = jnp.dot(mid, wd_s[...], preferred_element_type=F32)
            yw = _pack_halves(y)
            for s in range(PK_ROWS):
                ys_out[pl.ds(s, rows, stride=PK_ROWS), :] = yw[:, s * LANES:(s + 1) * LANES]
            if rows < tm:
                ys_out[rows * PK_ROWS:, :] = jnp.zeros(((tm - rows) * PK_ROWS, LANES),
                                                       ys_out.dtype)

        quarter = tm // 4
        for rows in range(quarter, tm + 1, quarter):
            @pl.when((used_ref[i] > rows - quarter) & (used_ref[i] <= rows))
            def _(rows=rows):
                swiglu(rows)

    @pl.when(i >= n_act)
    def _():
        ys_out[...] = jnp.zeros(ys_out.shape, ys_out.dtype)


def _local_sort_kernel(lp_ref, hpk_ref, xs_out):
    i = pl.program_id(0)
    t_tile = hpk_ref.shape[0] // PK_ROWS
    n_tok = pl.num_programs(0) * t_tile
    xs_out[...] = jnp.zeros(xs_out.shape, xs_out.dtype)

    def body(q, carry):
        for u in range(SUBLANES):
            t = q * SUBLANES + u
            row = hpk_ref[pl.ds(pl.multiple_of(t * PK_ROWS, PK_ROWS), PK_ROWS), :]
            for k in range(TOP_K):
                dst = pl.multiple_of(lp_ref[k * n_tok + i * t_tile + t], PK_ROWS)
                xs_out[pl.ds(dst, PK_ROWS), :] = row
        return carry
    lax.fori_loop(0, t_tile // SUBLANES, body, 0)


def _combine_kernel(gch_ref, lp_ref, gate_ref, h_ref, ys_hbm, out_ref, ylocal, acc, sems):
    i = pl.program_id(0)
    n = pl.num_programs(0)
    t_tile = h_ref.shape[0]
    n_tok = n * t_tile
    ch_rows = CHUNK * PK_ROWS
    n_ch = ylocal.shape[1] // ch_rows

    def copy_rows(src, slot, dst, rows):
        pltpu.make_async_copy(ys_hbm.at[pl.ds(src, rows)],
                              ylocal.at[slot, pl.ds(dst, rows)],
                              sems.at[slot]).start()

    def chunk_copy(tile, slot, lc):
        src = pl.multiple_of(gch_ref[tile * n_ch + lc], ch_rows)
        copy_rows(src, slot, pl.multiple_of(lc * ch_rows, ch_rows), ch_rows)

    def chunk_pair_copy(tile, slot, lc):
        src0 = pl.multiple_of(gch_ref[tile * n_ch + lc], ch_rows)
        src1 = pl.multiple_of(gch_ref[tile * n_ch + lc + 1], ch_rows)
        dst = pl.multiple_of(lc * ch_rows, ch_rows)
        joined = src1 == src0 + ch_rows

        @pl.when(joined)
        def _():
            copy_rows(src0, slot, dst, 2 * ch_rows)

        @pl.when(jnp.logical_not(joined))
        def _():
            copy_rows(src0, slot, dst, ch_rows)
            copy_rows(src1, slot, dst + ch_rows, ch_rows)

    def wait_slot(slot):
        pltpu.make_async_copy(ys_hbm.at[pl.ds(0, n_ch * ch_rows)], ylocal.at[slot],
                              sems.at[slot]).wait()

    @pl.when(i == 0)
    def _():
        def body(lc, carry):
            chunk_copy(0, 0, lc)
            return carry
        lax.fori_loop(0, n_ch, body, 0)

    slot = i % 2
    wait_slot(slot)

    nxt = jnp.minimum(i + 1, n - 1)
    tok_unroll = 4
    n_groups = t_tile // tok_unroll
    issue_groups = n_ch // 2
    assert issue_groups * 2 == n_ch and issue_groups <= n_groups

    def tok_body(q, carry, issue):
        if issue:
            chunk_pair_copy(nxt, 1 - slot, 2 * q)
        for u in range(tok_unroll):
            t = q * tok_unroll + u
            a0 = i * t_tile + t
            a1 = n_tok + a0
            p0 = pl.multiple_of(lp_ref[a0], PK_ROWS)
            p1 = pl.multiple_of(lp_ref[a1], PK_ROWS)
            lo0, hi0 = _unpack_halves(ylocal[slot, pl.ds(p0, PK_ROWS), :])
            lo1, hi1 = _unpack_halves(ylocal[slot, pl.ds(p1, PK_ROWS), :])
            g0 = gate_ref[a0]
            g1 = gate_ref[a1]
            dst = pl.multiple_of(t * PK_ROWS, PK_ROWS)
            acc[0, pl.ds(dst, PK_ROWS), :] = g0 * lo0 + g1 * lo1
            acc[1, pl.ds(dst, PK_ROWS), :] = g0 * hi0 + g1 * hi1
        return carry
    lax.fori_loop(0, issue_groups, functools.partial(tok_body, issue=True), 0)
    lax.fori_loop(issue_groups, n_groups, functools.partial(tok_body, issue=False), 0)

    @pl.when(i == n - 1)
    def _():
        wait_slot(1 - slot)

    for half in range(2):
        for s in range(PK_ROWS):
            c0 = half * HALF + s * LANES
            out_ref[:, c0:c0 + LANES] = (h_ref[:, c0:c0 + LANES]
                                         + acc[half, pl.ds(s, t_tile, stride=PK_ROWS), :])


def _rope_tables(seq):
    pos = jnp.arange(seq, dtype=F32)
    inv_freq = ROPE_THETA ** (-jnp.arange(0, ROT_DIM, 2, dtype=F32) / ROT_DIM)
    ang = pos[:, None] * inv_freq[None, :]
    cos, sin = jnp.cos(ang), jnp.sin(ang)
    half = ROT_DIM // 2
    ones = jnp.ones((seq, HEAD_DIM - ROT_DIM), F32)
    zeros = jnp.zeros((seq, HEAD_DIM - ROT_DIM), F32)
    zh = jnp.zeros((seq, half), F32)
    cc = jnp.concatenate([cos, cos, ones], axis=1)
    s_up = jnp.concatenate([-sin, zh, zeros], axis=1)
    s_dn = jnp.concatenate([zh, sin, zeros], axis=1)
    rep = LANES // HEAD_DIM
    return tuple(jnp.tile(t, (1, rep)) for t in (cc, s_up, s_dn))


def _plan(cnt, n_tokens):
    nt = n_tokens // LOCAL_TILE
    experts = jnp.arange(N_EXPERTS, dtype=jnp.int32)
    cntp = ((cnt + CHUNK - 1) // CHUNK) * CHUNK
    loff_end = jnp.cumsum(cntp, axis=1)
    loff = loff_end - cntp
    rows_e = jnp.sum(cntp, axis=0)
    rows_pad = ((rows_e + EXPERT_TILE - 1) // EXPERT_TILE) * EXPERT_TILE
    g_end = jnp.cumsum(rows_pad)
    g_start = g_end - rows_pad
    gpos = g_start[None, :] + jnp.cumsum(cntp, axis=0) - cntp

    lrow = jnp.arange(LT_MAX // CHUNK, dtype=jnp.int32) * CHUNK
    e_lc = jnp.sum((lrow[None, :, None] >= loff_end[:, None, :]).astype(jnp.int32), axis=-1)
    sel = e_lc[:, :, None] == experts
    delta = jnp.sum(jnp.where(sel, (gpos - loff)[:, None, :], 0), axis=-1)
    gch = jnp.where(e_lc < N_EXPERTS, (delta + lrow[None, :]) // CHUNK, 0).reshape(-1)

    n_rows = TOP_K * n_tokens + nt * N_EXPERTS * (CHUNK - 1) + N_EXPERTS * (EXPERT_TILE - CHUNK)
    n_blocks = -(-n_rows // EXPERT_TILE)
    n_rows = n_blocks * EXPERT_TILE
    blk_start = jnp.arange(n_blocks, dtype=jnp.int32) * EXPERT_TILE
    blk_eid = jnp.minimum(
        jnp.sum((blk_start[:, None] >= g_end[None, :]).astype(jnp.int32), axis=-1),
        N_EXPERTS - 1)
    prev = jnp.concatenate([jnp.full((1,), -1, jnp.int32), blk_eid[:-1]])
    first = (blk_eid != prev).astype(jnp.int32)
    n_act = (g_end[-1] // EXPERT_TILE).astype(jnp.int32).reshape(1)
    later = (blk_eid[None, :] > blk_eid[:, None]) & (jnp.arange(n_blocks)[None, :] < n_act)
    next_eid = jnp.min(jnp.where(later, blk_eid[None, :], N_EXPERTS), axis=1)
    next_eid = jnp.where(next_eid < N_EXPERTS, next_eid, -1).astype(jnp.int32)
    of_blk = blk_eid[:, None] == experts
    blk_end = jnp.sum(jnp.where(of_blk, (g_start + rows_e)[None, :], 0), axis=1)
    used = jnp.clip(blk_end - blk_start, 0, EXPERT_TILE).astype(jnp.int32)

    cpb = EXPERT_TILE // CHUNK
    run_start = gpos.T // CHUNK
    run_len = cntp.T // CHUNK
    run_src = (jnp.arange(nt, dtype=jnp.int32)[:, None] * (LT_MAX // CHUNK) + loff // CHUNK).T
    tabs = jnp.stack([run_start, run_len, run_src], axis=0)
    dtabs = tabs - jnp.concatenate(
        [jnp.zeros((3, N_EXPERTS, 1), jnp.int32), tabs[:, :, :-1]], axis=2)
    blk_sel = (blk_eid[:, None] == experts)[:, :, None]
    blk_start_t = jnp.sum(jnp.where(blk_sel, run_start[None], 0), axis=1)
    blk_dtabs = jnp.sum(jnp.where(blk_sel[None], dtabs[:, None], 0), axis=2)
    gc = jnp.arange(n_rows // CHUNK, dtype=jnp.int32).reshape(n_blocks, cpb)
    started = blk_start_t[:, None, :] <= gc[:, :, None]
    picked = jnp.sum(jnp.where(started[None], blk_dtabs[:, :, None, :], 0), axis=-1)
    within = gc - picked[0]
    csrc = jnp.where(within < picked[1], picked[2] + within, 0).reshape(-1)
    return (gch.astype(jnp.int32), csrc.astype(jnp.int32), blk_eid, first, next_eid, used,
            n_act, n_rows)


def kernel(x, norm_mix, w_in, q_norm, k_norm, sinks, w_pool, pool_scale, w_out,
           norm_ffn, w_coarse, b_coarse, w_fine, b_fine, w_gate, w_up, w_down):
    bsz, seq, d = x.shape
    n = bsz * seq
    assert d == D_MODEL and seq % ROW_TILE == 0 and seq % BLOCK == 0
    assert norm_mix.shape[0] == 1, "single-layer problem"
    xf = x.reshape(n, d)

    w_in_b = w_in[0].astype(BF16)
    wq = w_in_b[:, :ATTN_WIDTH]
    wk = w_in_b[:, ATTN_WIDTH:ATTN_WIDTH + KV_WIDTH]
    wv = w_in_b[:, ATTN_WIDTH + KV_WIDTH:ATTN_WIDTH + 2 * KV_WIDTH]
    wu = w_in_b[:, ATTN_WIDTH + 2 * KV_WIDTH:]
    qn = jnp.tile(q_norm[0], N_HEADS).reshape(1, ATTN_WIDTH)
    kn = jnp.tile(k_norm[0], N_KV_HEADS).reshape(1, KV_WIDTH)
    cc, s_up, s_dn = _rope_tables(seq)
    lane_head = jnp.arange(MXU_DIM) // HEAD_DIM
    e_mat = (lane_head[:, None] == lane_head[None, :]).astype(BF16)
    w_pool_b = w_pool[0].astype(BF16)
    pscale = pool_scale[0].reshape(1, POOL_WIDTH)
    w_out_b = w_out[0].astype(BF16)
    wo_attn, wo_pool = w_out_b[:ATTN_WIDTH], w_out_b[ATTN_WIDTH:]
    gap = ROUTER_FINE0 - N_EXPERT_GROUPS
    tail = LANES - ROUTER_FINE0 - N_EXPERTS
    w_r = jnp.concatenate([w_coarse[0], jnp.zeros((d, gap), F32), w_fine[0],
                           jnp.zeros((d, tail), F32)], axis=1)
    w_r_hi = w_r.astype(BF16)
    w_r_lo = (w_r - w_r_hi.astype(F32)).astype(BF16)
    w_r2 = jnp.concatenate([w_r_hi, w_r_lo], axis=1)
    b_r = jnp.concatenate([b_coarse[0], jnp.zeros((gap,), F32), b_fine[0],
                           jnp.zeros((tail,), F32)]).reshape(1, LANES)

    tm = ROW_TILE
    tiles_per_seq = seq // tm
    n_tiles = n // tm
    idx = jnp.arange(SUB_TILE)
    tri = (idx[None, :] < idx[:, None]).astype(BF16)
    const = lambda *_: (0, 0)
    row_blk = lambda i: (i, 0)

    q, k, v, ksw, vsw, y = pl.pallas_call(
        functools.partial(_inproj_kernel, tiles_per_seq=tiles_per_seq),
        grid=(n_tiles,),
        in_specs=[
            pl.BlockSpec((tm, d), row_blk),
            pl.BlockSpec((1, d), const),
            pl.BlockSpec((d, ATTN_WIDTH), const),
            pl.BlockSpec((d, KV_WIDTH), const),
            pl.BlockSpec((d, KV_WIDTH), const),
            pl.BlockSpec((d, POOL_WIDTH), const),
            pl.BlockSpec((1, ATTN_WIDTH), const),
            pl.BlockSpec((1, KV_WIDTH), const),
            pl.BlockSpec((tm, LANES), lambda i: (i % tiles_per_seq, 0)),
            pl.BlockSpec((tm, LANES), lambda i: (i % tiles_per_seq, 0)),
            pl.BlockSpec((tm, LANES), lambda i: (i % tiles_per_seq, 0)),
            pl.BlockSpec((MXU_DIM, MXU_DIM), const),
            pl.BlockSpec((len(POOL_WINDOWS), POOL_GROUP_DIM, POOL_GROUP_DIM),
                         lambda i: (0, 0, 0)),
            pl.BlockSpec((1, POOL_WIDTH), const),
        ],
        out_specs=[
            pl.BlockSpec((tm, ATTN_WIDTH), row_blk),
            pl.BlockSpec((tm, KV_WIDTH), row_blk),
            pl.BlockSpec((tm, KV_WIDTH), row_blk),
            pl.BlockSpec((tm, KV_WIDTH), row_blk),
            pl.BlockSpec((tm, KV_WIDTH), row_blk),
            pl.BlockSpec((tm, POOL_WIDTH), row_blk),
        ],
        out_shape=[
            jax.ShapeDtypeStruct((n, ATTN_WIDTH), BF16),
            jax.ShapeDtypeStruct((n, KV_WIDTH), BF16),
            jax.ShapeDtypeStruct((n, KV_WIDTH), BF16),
            jax.ShapeDtypeStruct((n, KV_WIDTH), BF16),
            jax.ShapeDtypeStruct((n, KV_WIDTH), BF16),
            jax.ShapeDtypeStruct((n, POOL_WIDTH), BF16),
        ],
        scratch_shapes=[pltpu.VMEM((HALO + tm, POOL_WIDTH), F32)],
        compiler_params=_cparams(),
        name="inproj",
    )(xf, norm_mix[0].reshape(1, d), wq, wk, wv, wu, qn, kn, cc, s_up, s_dn,
      e_mat, w_pool_b, pscale)

    nb = seq // BLOCK
    nq = nb // Q_BLOCKS
    qrows = Q_BLOCKS * BLOCK
    cur = lambda b, j, *_: (b * nq + j, 0)
    prv = lambda b, j, *_: (b * nb + jnp.maximum(Q_BLOCKS * j - 1, 0), 0)
    kv_specs = [pl.BlockSpec((qrows, KV_WIDTH), cur), pl.BlockSpec((BLOCK, KV_WIDTH), prv)]
    attn = pl.pallas_call(
        _attn_kernel,
        grid_spec=pltpu.PrefetchScalarGridSpec(
            num_scalar_prefetch=1,
            grid=(bsz, nq),
            in_specs=[pl.BlockSpec((qrows, ATTN_WIDTH), cur)] + kv_specs * 4,
            out_specs=pl.BlockSpec((qrows, ATTN_WIDTH), cur),
        ),
        out_shape=jax.ShapeDtypeStruct((n, ATTN_WIDTH), BF16),
        compiler_params=_cparams(2),
        name="swa_attn",
    )(sinks[0].astype(F32), q, k, k, ksw, ksw, v, v, vsw, vsw)

    h, hpk, route, cnt = pl.pallas_call(
        _outproj_router_kernel,
        grid=(n_tiles,),
        in_specs=[
            pl.BlockSpec((tm, ATTN_WIDTH), row_blk),
            pl.BlockSpec((tm, POOL_WIDTH), row_blk),
            pl.BlockSpec((tm, d), row_blk),
            pl.BlockSpec((ATTN_WIDTH, d), const),
            pl.BlockSpec((POOL_WIDTH, d), const),
            pl.BlockSpec((1, d), const),
            pl.BlockSpec((d, 2 * LANES), const),
            pl.BlockSpec((1, LANES), const),
            pl.BlockSpec((SUB_TILE, SUB_TILE), const),
        ],
        out_specs=[
            pl.BlockSpec((tm, d), row_blk),
            pl.BlockSpec((tm * PK_ROWS, LANES), row_blk),
            pl.BlockSpec((SUBLANES, tm), lambda i: (0, i)),
            pl.BlockSpec((N_EXPERTS, LANES), row_blk),
        ],
        out_shape=[
            jax.ShapeDtypeStruct((n, d), F32),
            jax.ShapeDtypeStruct((n * PK_ROWS, LANES), jnp.int32),
            jax.ShapeDtypeStruct((SUBLANES, n), F32),
            jax.ShapeDtypeStruct((n_tiles * N_EXPERTS, LANES), F32),
        ],
        compiler_params=_cparams(),
        name="outproj_router",
    )(attn, y, xf, wo_attn, wo_pool, norm_ffn[0].reshape(1, d), w_r2, b_r, tri)

    assert tm == LOCAL_TILE
    gates = route[0:TOP_K].reshape(TOP_K * n)
    lp = route[TOP_K:2 * TOP_K].astype(jnp.int32).reshape(TOP_K * n)
    tile_cnt = cnt.reshape(n_tiles, N_EXPERTS, LANES)[:, :, 0].astype(jnp.int32)
    te = EXPERT_TILE
    gch, csrc, blk_eid, first, next_eid, used, n_act, n_rows = _plan(tile_cnt, n)
    n_blocks = n_rows // te
    n_local = n // LOCAL_TILE

    xs = pl.pallas_call(
        _local_sort_kernel,
        grid_spec=pltpu.PrefetchScalarGridSpec(
            num_scalar_prefetch=1,
            grid=(n_local,),
            in_specs=[pl.BlockSpec((LOCAL_TILE * PK_ROWS, LANES), lambda i, *_: (i, 0))],
            out_specs=pl.BlockSpec((LT_MAX * PK_ROWS, LANES), lambda i, *_: (i, 0)),
        ),
        out_shape=jax.ShapeDtypeStruct((n_local * LT_MAX * PK_ROWS, LANES), jnp.int32),
        compiler_params=_cparams(),
        name="local_sort",
    )(lp * PK_ROWS, hpk)

    ys = pl.pallas_call(
        _expert_kernel,
        grid_spec=pltpu.PrefetchScalarGridSpec(
            num_scalar_prefetch=6,
            grid=(n_blocks,),
            in_specs=[pl.BlockSpec(memory_space=pl.ANY)] * 4,
            out_specs=pl.BlockSpec((te * PK_ROWS, LANES), lambda i, *_: (i, 0)),
            scratch_shapes=[
                pltpu.VMEM((2, te * PK_ROWS, LANES), jnp.int32),
                pltpu.VMEM((d, D_EXPERT), F32),
                pltpu.VMEM((d, D_EXPERT), F32),
                pltpu.VMEM((D_EXPERT, d), F32),
                pltpu.VMEM((d, D_EXPERT), BF16),
                pltpu.VMEM((d, D_EXPERT), BF16),
                pltpu.VMEM((D_EXPERT, d), BF16),
                pltpu.SemaphoreType.DMA((2,)),
                pltpu.SemaphoreType.DMA((3,)),
            ],
        ),
        out_shape=jax.ShapeDtypeStruct((n_rows * PK_ROWS, LANES), jnp.int32),
        compiler_params=_cparams(),
        name="experts",
    )(blk_eid, first, n_act, csrc * (CHUNK * PK_ROWS), next_eid, used, xs,
      w_gate[0], w_up[0], w_down[0])

    tt = LOCAL_TILE
    out = pl.pallas_call(
        _combine_kernel,
        grid_spec=pltpu.PrefetchScalarGridSpec(
            num_scalar_prefetch=3,
            grid=(n_local,),
            in_specs=[
                pl.BlockSpec((tt, d), lambda i, *_: (i, 0)),
                pl.BlockSpec(memory_space=pl.ANY),
            ],
            out_specs=pl.BlockSpec((tt, d), lambda i, *_: (i, 0)),
            scratch_shapes=[
                pltpu.VMEM((2, LT_MAX * PK_ROWS, LANES), jnp.int32),
                pltpu.VMEM((2, tt * PK_ROWS, LANES), F32),
                pltpu.SemaphoreType.DMA((2,)),
            ],
        ),
        out_shape=jax.ShapeDtypeStruct((n, d), F32),
        compiler_params=_cparams(),
        name="combine",
    )(gch * (CHUNK * PK_ROWS), lp * PK_ROWS, gates, h, ys)
    return out.reshape(bsz, seq, d)
```

```python
import functools

import jax
import jax.numpy as jnp
from jax import lax
from jax.experimental import pallas as pl
from jax.experimental.pallas import tpu as pltpu

F32 = jnp.float32
BF16 = jnp.bfloat16

D_MODEL = 2048
N_HEADS = 16
N_KV_HEADS = 4
HEAD_DIM = 64
GROUP = N_HEADS // N_KV_HEADS
ROT_DIM = HEAD_DIM // 4
ROPE_THETA = 500000.0
WINDOW = 128
BLOCK = 128
Q_BLOCKS = 16
ATTN_WIDTH = N_HEADS * HEAD_DIM
KV_WIDTH = N_KV_HEADS * HEAD_DIM
POOL_WINDOWS = (2, 4, 8, 16)
POOL_WIDTH = D_MODEL // 2
POOL_GROUP_DIM = POOL_WIDTH // len(POOL_WINDOWS)
N_EXPERT_GROUPS = 4
EXPERTS_PER_GROUP = 8
N_EXPERTS = N_EXPERT_GROUPS * EXPERTS_PER_GROUP
TOP_K = 2
D_EXPERT = 512
EPS = 1e-6

LANES = 128
SUBLANES = 8
MXU_DIM = 256
HALO = 16
NEG_BIG = -1e30
ROUTER_FINE0 = 8
HALF = D_MODEL // 2
PK_ROWS = HALF // LANES

ROW_TILE = 512
SUB_TILE = 512
EXPERT_TILE = 512
LOCAL_TILE = 512
CHUNK = 8
LT_MAX = TOP_K * LOCAL_TILE + N_EXPERTS * (CHUNK - 1)
VMEM_LIMIT = 56 * 1024 * 1024


def _cparams(n_axes=1):
    return pltpu.CompilerParams(
        dimension_semantics=("arbitrary",) * n_axes,
        vmem_limit_bytes=VMEM_LIMIT,
    )


def _pack_halves(t):
    return pltpu.pack_elementwise([t[:, :HALF], t[:, HALF:]], packed_dtype=BF16)


def _unpack_halves(w):
    return tuple(pltpu.unpack_elementwise(w, index=i, packed_dtype=BF16, unpacked_dtype=F32)
                 for i in range(2))


def _head_sumsq(t, e_ref):
    t2 = (t * t).astype(BF16)
    e = e_ref[...]
    outs = []
    for c in range(t.shape[1] // MXU_DIM):
        sl = slice(c * MXU_DIM, (c + 1) * MXU_DIM)
        outs.append(jnp.dot(t2[:, sl], e, preferred_element_type=F32))
    return outs[0] if len(outs) == 1 else jnp.concatenate(outs, axis=1)


def _norm_rope(t, gain, e_ref, cc, s_up, s_dn, out_scale):
    w = t.shape[1]
    ss = _head_sumsq(t, e_ref)
    tn = t * lax.rsqrt(ss * (1.0 / HEAD_DIM) + EPS) * gain
    reps = w // LANES
    up = pltpu.roll(tn, w - ROT_DIM // 2, axis=1)
    dn = pltpu.roll(tn, ROT_DIM // 2, axis=1)
    c = jnp.concatenate([cc] * reps, axis=1)
    su = jnp.concatenate([s_up] * reps, axis=1)
    sd = jnp.concatenate([s_dn] * reps, axis=1)
    return (tn * c + up * su + dn * sd) * out_scale


def _swap_head_pairs(t):
    w = t.shape[1]
    lane = lax.broadcasted_iota(jnp.int32, t.shape, 1)
    from_up = pltpu.roll(t, w - HEAD_DIM, axis=1)
    from_dn = pltpu.roll(t, HEAD_DIM, axis=1)
    return jnp.where((lane & (LANES - 1)) < HEAD_DIM, from_up, from_dn)


def _inproj_kernel(x_ref, g_ref, wq_ref, wk_ref, wv_ref, wu_ref, qn_ref, kn_ref,
                   cc_ref, su_ref, sd_ref, e_ref, wpool_ref, pscale_ref,
                   q_out, k_out, v_out, ksw_out, vsw_out, y_out, ubuf, *, tiles_per_seq):
    i = pl.program_id(0)
    tm = x_ref.shape[0]

    @pl.when(i % tiles_per_seq == 0)
    def _():
        ubuf[0:HALO, :] = jnp.zeros((HALO, POOL_WIDTH), F32)

    for r0 in range(0, tm, SUB_TILE):
        rows = slice(r0, r0 + SUB_TILE)
        x = x_ref[rows, :]
        ms = jnp.mean(x * x, axis=-1, keepdims=True)
        hn = (x * lax.rsqrt(ms + EPS) * g_ref[...]).astype(BF16)

        cc, s_up, s_dn = cc_ref[rows, :], su_ref[rows, :], sd_ref[rows, :]
        q = jnp.dot(hn, wq_ref[...], preferred_element_type=F32)
        q_out[rows, :] = _norm_rope(q, qn_ref[...], e_ref, cc, s_up, s_dn,
                                    HEAD_DIM ** -0.5).astype(BF16)
        k = jnp.dot(hn, wk_ref[...], preferred_element_type=F32)
        k = _norm_rope(k, kn_ref[...], e_ref, cc, s_up, s_dn, 1.0)
        v = jnp.dot(hn, wv_ref[...], preferred_element_type=F32)
        k_out[rows, :] = k.astype(BF16)
        v_out[rows, :] = v.astype(BF16)
        ksw_out[rows, :] = _swap_head_pairs(k).astype(BF16)
        vsw_out[rows, :] = _swap_head_pairs(v).astype(BF16)

        base = HALO + r0
        ubuf[base:base + SUB_TILE, :] = jnp.dot(hn, wu_ref[...], preferred_element_type=F32)
        pos = (i % tiles_per_seq) * tm + r0 + lax.broadcasted_iota(
            jnp.int32, (SUB_TILE, POOL_GROUP_DIM), 0)
        for g, w in enumerate(POOL_WINDOWS):
            cols = slice(g * POOL_GROUP_DIM, (g + 1) * POOL_GROUP_DIM)
            assert w & (w - 1) == 0 and w - 1 < HALO
            acc = ubuf[base - HALO:base + SUB_TILE, cols]
            shift = 1
            while shift < w:
                acc = acc + pltpu.roll(acc, shift, axis=0)
                shift *= 2
            acc = acc[HALO:, :]
            u_g = ubuf[base:base + SUB_TILE, cols]
            cnt = jnp.minimum(pos + 1, w).astype(F32)
            d = (acc / cnt - u_g).astype(BF16)
            y = jnp.dot(d, wpool_ref[g], preferred_element_type=F32)
            y_out[rows, cols] = (y * pscale_ref[:, cols]).astype(BF16)
    ubuf[0:HALO, :] = ubuf[tm:tm + HALO, :]


def _attn_kernel(sinks_ref, q_ref, kc_ref, kp_ref, kswc_ref, kswp_ref,
                 vc_ref, vp_ref, vswc_ref, vswp_ref, o_ref):
    j = pl.program_id(1)
    two = 2 * BLOCK
    row = lax.broadcasted_iota(jnp.int32, (two, two), 0) & (BLOCK - 1)
    col = lax.broadcasted_iota(jnp.int32, (two, two), 1)
    diff = row + BLOCK - col
    local = (diff >= 0) & (diff < WINDOW)
    lo_lanes = lax.broadcasted_iota(jnp.int32, (two, LANES), 1) < HEAD_DIM
    row_p = lax.broadcasted_iota(jnp.int32, (two, BLOCK), 0)
    diag = (row_p & (BLOCK - 1)) == lax.broadcasted_iota(jnp.int32, (two, BLOCK), 1)
    top_rows = row_p < BLOCK
    diag_top = diag & top_rows
    diag_bot = diag & jnp.logical_not(top_rows)
    zero = jnp.zeros((two, LANES), BF16)

    for sb in range(q_ref.shape[0] // BLOCK):
        rows = slice(sb * BLOCK, (sb + 1) * BLOCK)
        if sb == 0:
            def band(cur_ref, prev_ref):
                return jnp.concatenate([prev_ref[...], cur_ref[0:BLOCK, :]], axis=0)
            valid = local & ((col >= BLOCK) | (j > 0))
        else:
            def band(cur_ref, prev_ref, sb=sb):
                return cur_ref[(sb - 1) * BLOCK:(sb + 1) * BLOCK, :]
            valid = local
        k_nat, k_swp = band(kc_ref, kp_ref), band(kswc_ref, kswp_ref)
        v_nat, v_swp = band(vc_ref, vp_ref), band(vswc_ref, vswp_ref)

        for g in range(N_KV_HEADS):
            c0 = (g // 2) * LANES
            kcols = (k_nat[:, c0:c0 + LANES], k_swp[:, c0:c0 + LANES])
            vcols = (v_nat[:, c0:c0 + LANES], v_swp[:, c0:c0 + LANES])
            in_lo, in_hi = (0, 1) if g % 2 == 0 else (1, 0)
            q0 = g * GROUP * HEAD_DIM
            qq = jnp.concatenate([q_ref[rows, q0:q0 + LANES],
                                  q_ref[rows, q0 + LANES:q0 + 2 * LANES]], axis=0)
            halves = []
            for half, src in ((0, in_lo), (1, in_hi)):
                qm = jnp.where(lo_lanes, qq, zero) if half == 0 else jnp.where(lo_lanes, zero, qq)
                s = lax.dot_general(qm, kcols[src], (((1,), (1,)), ((), ())),
                                    preferred_element_type=F32)
                s = jnp.where(valid, s, NEG_BIG)
                h_top = g * GROUP + half
                s_prev = jnp.where(diag_top, sinks_ref[h_top],
                                   jnp.where(diag_bot, sinks_ref[h_top + 2], s[:, :BLOCK]))
                s_cur = s[:, BLOCK:]
                m = jnp.max(jnp.maximum(s_prev, s_cur), axis=-1, keepdims=True)
                p_prev = jnp.exp(s_prev - m)
                p_cur = jnp.exp(s_cur - m)
                denom = jnp.sum(p_prev + p_cur, axis=-1, keepdims=True)
                p = jnp.concatenate([jnp.where(diag, 0.0, p_prev), p_cur], axis=1)
                o = jnp.dot(p.astype(BF16), vcols[src], preferred_element_type=F32)
                halves.append(o / denom)
            o_pair = jnp.where(lo_lanes, halves[0], halves[1]).astype(BF16)
            o_ref[rows, q0:q0 + LANES] = o_pair[0:BLOCK, :]
            o_ref[rows, q0 + LANES:q0 + 2 * LANES] = o_pair[BLOCK:, :]


def _outproj_router_kernel(o_ref, y_ref, x_ref, woa_ref, wop_ref, g_ref, wr_ref,
                           br_ref, tri_ref, h_out, hpk_out, route_out, cnt_out):
    tm = x_ref.shape[0]
    h = (x_ref[...]
         + jnp.dot(o_ref[...], woa_ref[...], preferred_element_type=F32)
         + jnp.dot(y_ref[...], wop_ref[...], preferred_element_type=F32))
    h_out[...] = h
    ms = jnp.mean(h * h, axis=-1, keepdims=True)
    hn = h * lax.rsqrt(ms + EPS) * g_ref[...]
    hb = hn.astype(BF16)
    packed = _pack_halves(hn)
    for s in range(PK_ROWS):
        hpk_out[pl.ds(s, tm, stride=PK_ROWS), :] = packed[:, s * LANES:(s + 1) * LANES]

    both = jnp.dot(hb, wr_ref[...], preferred_element_type=F32)
    logits = (both[:, :LANES] + both[:, LANES:] + br_ref[...]).T
    assert EXPERTS_PER_GROUP == SUBLANES and ROUTER_FINE0 % SUBLANES == 0
    slot = lax.broadcasted_iota(jnp.int32, (SUBLANES, tm), 0).astype(F32)

    def col_max(t):
        return jnp.max(t, axis=0, keepdims=True)

    def first_slot_of(mask):
        return jnp.min(jnp.where(mask, slot, float(SUBLANES)), axis=0, keepdims=True)

    coarse = jnp.where(slot < N_EXPERT_GROUPS, logits[0:SUBLANES, :], NEG_BIG)
    cmax = col_max(coarse)
    grp = first_slot_of(coarse == cmax)
    csum = jnp.sum(jnp.exp(coarse - cmax), axis=0, keepdims=True)

    def fine_of(g):
        return logits[ROUTER_FINE0 + g * SUBLANES:ROUTER_FINE0 + (g + 1) * SUBLANES, :]

    fine = fine_of(N_EXPERT_GROUPS - 1)
    for g in range(N_EXPERT_GROUPS - 2, -1, -1):
        fine = jnp.where(grp == float(g), fine_of(g), fine)
    f1 = col_max(fine)
    i1 = first_slot_of(fine == f1)
    rest = jnp.where(slot == i1, NEG_BIG, fine)
    f2 = col_max(rest)
    i2 = first_slot_of(rest == f2)
    ratio = jnp.exp(f2 - f1)
    g1 = (1.0 / csum) / (1.0 + ratio)
    g2 = g1 * ratio
    e1 = grp * EXPERTS_PER_GROUP + i1
    e2 = grp * EXPERTS_PER_GROUP + i2

    expert = lax.broadcasted_iota(jnp.int32, (N_EXPERTS, tm), 0).astype(F32)
    oh1 = expert == e1
    oh2 = expert == e2
    oh = jnp.where(oh1, 1.0, 0.0) + jnp.where(oh2, 1.0, 0.0)
    rank = lax.dot_general(oh.astype(BF16), tri_ref[...], (((1,), (1,)), ((), ())),
                           preferred_element_type=F32)
    cnt = jnp.sum(oh, axis=1, keepdims=True)
    cnt_pad = jnp.floor((cnt + (CHUNK - 1)) * (1.0 / CHUNK)) * CHUNK
    padded_cols = jnp.concatenate(
        [jnp.broadcast_to(cnt_pad, (N_EXPERTS, LANES)),
         jnp.zeros((LANES - N_EXPERTS, LANES), F32)], axis=0).astype(BF16)
    run_start = jnp.dot(tri_ref[0:N_EXPERTS, 0:LANES], padded_cols,
                        preferred_element_type=F32)[:, 0:1]
    table = rank + run_start
    lp1 = jnp.sum(jnp.where(oh1, table, 0.0), axis=0, keepdims=True)
    lp2 = jnp.sum(jnp.where(oh2, table, 0.0), axis=0, keepdims=True)
    route_out[...] = jnp.where(slot == 0.0, g1,
                               jnp.where(slot == 1.0, g2,
                                         jnp.where(slot == 2.0, lp1,
                                                   jnp.where(slot == 3.0, lp2, 0.0))))
    cnt_out[...] = jnp.broadcast_to(cnt, cnt_out.shape)


def _expert_kernel(eid_ref, first_ref, nact_ref, off_ref, next_ref, used_ref,
                   hpk_hbm, wg_hbm, wu_hbm, wd_hbm, ys_out,
                   xbuf, wg_f, wu_f, wd_f, wg_s, wu_s, wd_s, sems, wsems):
    i = pl.program_id(0)
    tm = ys_out.shape[0] // PK_ROWS
    n_act = nact_ref[0]

    def weight_copies(e):
        return (pltpu.make_async_copy(wg_hbm.at[e], wg_f, wsems.at[0]),
                pltpu.make_async_copy(wu_hbm.at[e], wu_f, wsems.at[1]),
                pltpu.make_async_copy(wd_hbm.at[e], wd_f, wsems.at[2]))

    ch_rows = CHUNK * PK_ROWS
    n_ch = tm // CHUNK

    def gather(blk, slot):
        for c in range(n_ch):
            src = pl.multiple_of(off_ref[blk * n_ch + c], ch_rows)
            pltpu.make_async_copy(hpk_hbm.at[pl.ds(src, ch_rows)],
                                  xbuf.at[slot, pl.ds(c * ch_rows, ch_rows)],
                                  sems.at[slot]).start()

    @pl.when(i == 0)
    def _():
        for cp in weight_copies(eid_ref[0]):
            cp.start()
        gather(0, 0)

    @pl.when(i + 1 < n_act)
    def _():
        gather(i + 1, (i + 1) % 2)

    @pl.when(i < n_act)
    def _():
        slot = i % 2

        @pl.when(first_ref[i] == 1)
        def _():
            for cp in weight_copies(eid_ref[i]):
                cp.wait()
            wg_s[...] = wg_f[...].astype(BF16)
            wu_s[...] = wu_f[...].astype(BF16)
            wd_s[...] = wd_f[...].astype(BF16)

            @pl.when(next_ref[i] >= 0)
            def _():
                for cp in weight_copies(next_ref[i]):
                    cp.start()

        pltpu.make_async_copy(hpk_hbm.at[pl.ds(0, tm * PK_ROWS)], xbuf.at[slot],
                              sems.at[slot]).wait()

        def swiglu(rows):
            xu = jnp.concatenate(
                [xbuf[slot, pl.ds(s, rows, stride=PK_ROWS), :] for s in range(PK_ROWS)],
                axis=1)
            x_lo, x_hi = (v.astype(BF16) for v in _unpack_halves(xu))
            a = (jnp.dot(x_lo, wg_s[0:HALF, :], preferred_element_type=F32)
                 + jnp.dot(x_hi, wg_s[HALF:, :], preferred_element_type=F32))
            b = (jnp.dot(x_lo, wu_s[0:HALF, :], preferred_element_type=F32)
                 + jnp.dot(x_hi, wu_s[HALF:, :], preferred_element_type=F32))
            mid = (a * jax.nn.sigmoid(a) * b).astype(BF16)
            y = jnp.dot(mid, wd_s[...], preferred_element_type=F32)
            yw = _pack_halves(y)
            for s in range(PK_ROWS):
                ys_out[pl.ds(s, rows, stride=PK_ROWS), :] = yw[:, s * LANES:(s + 1) * LANES]
            if rows < tm:
                ys_out[rows * PK_ROWS:, :] = jnp.zeros(((tm - rows) * PK_ROWS, LANES),
                                                       ys_out.dtype)

        quarter = tm // 4
        for rows in range(quarter, tm + 1, quarter):
            @pl.when((used_ref[i] > rows - quarter) & (used_ref[i] <= rows))
            def _(rows=rows):
                swiglu(rows)

    @pl.when(i >= n_act)
    def _():
        ys_out[...] = jnp.zeros(ys_out.shape, ys_out.dtype)


def _local_sort_kernel(lp_ref, hpk_ref, xs_out):
    i = pl.program_id(0)
    t_tile = hpk_ref.shape[0] // PK_ROWS
    n_tok = pl.num_programs(0) * t_tile
    xs_out[...] = jnp.zeros(xs_out.shape, xs_out.dtype)

    def body(q, carry):
        for u in range(SUBLANES):
            t = q * SUBLANES + u
            row = hpk_ref[pl.ds(pl.multiple_of(t * PK_ROWS, PK_ROWS), PK_ROWS), :]
            for k in range(TOP_K):
                dst = pl.multiple_of(lp_ref[k * n_tok + i * t_tile + t], PK_ROWS)
                xs_out[pl.ds(dst, PK_ROWS), :] = row
        return carry
    lax.fori_loop(0, t_tile // SUBLANES, body, 0)


def _combine_kernel(gch_ref, lp_ref, gate_ref, h_ref, ys_hbm, out_ref, ylocal, acc, sems):
    i = pl.program_id(0)
    n = pl.num_programs(0)
    t_tile = h_ref.shape[0]
    n_tok = n * t_tile
    ch_rows = CHUNK * PK_ROWS
    n_ch = ylocal.shape[1] // ch_rows

    def copy_rows(src, slot, dst, rows):
        pltpu.make_async_copy(ys_hbm.at[pl.ds(src, rows)],
                              ylocal.at[slot, pl.ds(dst, rows)],
                              sems.at[slot]).start()

    def chunk_copy(tile, slot, lc):
        src = pl.multiple_of(gch_ref[tile * n_ch + lc], ch_rows)
        copy_rows(src, slot, pl.multiple_of(lc * ch_rows, ch_rows), ch_rows)

    def chunk_pair_copy(tile, slot, lc):
        src0 = pl.multiple_of(gch_ref[tile * n_ch + lc], ch_rows)
        src1 = pl.multiple_of(gch_ref[tile * n_ch + lc + 1], ch_rows)
        dst = pl.multiple_of(lc * ch_rows, ch_rows)
        joined = src1 == src0 + ch_rows

        @pl.when(joined)
        def _():
            copy_rows(src0, slot, dst, 2 * ch_rows)

        @pl.when(jnp.logical_not(joined))
        def _():
            copy_rows(src0, slot, dst, ch_rows)
            copy_rows(src1, slot, dst + ch_rows, ch_rows)

    def wait_slot(slot):
        pltpu.make_async_copy(ys_hbm.at[pl.ds(0, n_ch * ch_rows)], ylocal.at[slot],
                              sems.at[slot]).wait()

    @pl.when(i == 0)
    def _():
        def body(lc, carry):
            chunk_copy(0, 0, lc)
            return carry
        lax.fori_loop(0, n_ch, body, 0)

    slot = i % 2
    wait_slot(slot)

    nxt = jnp.minimum(i + 1, n - 1)
    tok_unroll = 8
    pairs_per_group = 2
    n_groups = t_tile // tok_unroll
    issue_groups = n_ch // (2 * pairs_per_group)
    assert issue_groups * 2 * pairs_per_group == n_ch and issue_groups <= n_groups

    def tok_body(q, carry, issue):
        if issue:
            for u in range(pairs_per_group):
                chunk_pair_copy(nxt, 1 - slot, 2 * (q * pairs_per_group + u))
        for u in range(tok_unroll):
            t = q * tok_unroll + u
            a0 = i * t_tile + t
            a1 = n_tok + a0
            p0 = pl.multiple_of(lp_ref[a0], PK_ROWS)
            p1 = pl.multiple_of(lp_ref[a1], PK_ROWS)
            lo0, hi0 = _unpack_halves(ylocal[slot, pl.ds(p0, PK_ROWS), :])
            lo1, hi1 = _unpack_halves(ylocal[slot, pl.ds(p1, PK_ROWS), :])
            g0 = gate_ref[a0]
            g1 = gate_ref[a1]
            dst = pl.multiple_of(t * PK_ROWS, PK_ROWS)
            acc[0, pl.ds(dst, PK_ROWS), :] = g0 * lo0 + g1 * lo1
            acc[1, pl.ds(dst, PK_ROWS), :] = g0 * hi0 + g1 * hi1
        return carry
    lax.fori_loop(0, issue_groups, functools.partial(tok_body, issue=True), 0)
    lax.fori_loop(issue_groups, n_groups, functools.partial(tok_body, issue=False), 0)

    @pl.when(i == n - 1)
    def _():
        wait_slot(1 - slot)

    for half in range(2):
        for s in range(PK_ROWS):
            c0 = half * HALF + s * LANES
            out_ref[:, c0:c0 + LANES] = (h_ref[:, c0:c0 + LANES]
                                         + acc[half, pl.ds(s, t_tile, stride=PK_ROWS), :])


def _rope_tables(seq):
    pos = jnp.arange(seq, dtype=F32)
    inv_freq = ROPE_THETA ** (-jnp.arange(0, ROT_DIM, 2, dtype=F32) / ROT_DIM)
    ang = pos[:, None] * inv_freq[None, :]
    cos, sin = jnp.cos(ang), jnp.sin(ang)
    half = ROT_DIM // 2
    ones = jnp.ones((seq, HEAD_DIM - ROT_DIM), F32)
    zeros = jnp.zeros((seq, HEAD_DIM - ROT_DIM), F32)
    zh = jnp.zeros((seq, half), F32)
    cc = jnp.concatenate([cos, cos, ones], axis=1)
    s_up = jnp.concatenate([-sin, zh, zeros], axis=1)
    s_dn = jnp.concatenate([zh, sin, zeros], axis=1)
    rep = LANES // HEAD_DIM
    return tuple(jnp.tile(t, (1, rep)) for t in (cc, s_up, s_dn))


def _plan(cnt, n_tokens):
    nt = n_tokens // LOCAL_TILE
    experts = jnp.arange(N_EXPERTS, dtype=jnp.int32)
    cntp = ((cnt + CHUNK - 1) // CHUNK) * CHUNK
    loff_end = jnp.cumsum(cntp, axis=1)
    loff = loff_end - cntp
    rows_e = jnp.sum(cntp, axis=0)
    rows_pad = ((rows_e + EXPERT_TILE - 1) // EXPERT_TILE) * EXPERT_TILE
    g_end = jnp.cumsum(rows_pad)
    g_start = g_end - rows_pad
    gpos = g_start[None, :] + jnp.cumsum(cntp, axis=0) - cntp

    lrow = jnp.arange(LT_MAX // CHUNK, dtype=jnp.int32) * CHUNK
    e_lc = jnp.sum((lrow[None, :, None] >= loff_end[:, None, :]).astype(jnp.int32), axis=-1)
    sel = e_lc[:, :, None] == experts
    delta = jnp.sum(jnp.where(sel, (gpos - loff)[:, None, :], 0), axis=-1)
    gch = jnp.where(e_lc < N_EXPERTS, (delta + lrow[None, :]) // CHUNK, 0).reshape(-1)

    n_rows = TOP_K * n_tokens + nt * N_EXPERTS * (CHUNK - 1) + N_EXPERTS * (EXPERT_TILE - CHUNK)
    n_blocks = -(-n_rows // EXPERT_TILE)
    n_rows = n_blocks * EXPERT_TILE
    blk_start = jnp.arange(n_blocks, dtype=jnp.int32) * EXPERT_TILE
    blk_eid = jnp.minimum(
        jnp.sum((blk_start[:, None] >= g_end[None, :]).astype(jnp.int32), axis=-1),
        N_EXPERTS - 1)
    prev = jnp.concatenate([jnp.full((1,), -1, jnp.int32), blk_eid[:-1]])
    first = (blk_eid != prev).astype(jnp.int32)
    n_act = (g_end[-1] // EXPERT_TILE).astype(jnp.int32).reshape(1)
    later = (blk_eid[None, :] > blk_eid[:, None]) & (jnp.arange(n_blocks)[None, :] < n_act)
    next_eid = jnp.min(jnp.where(later, blk_eid[None, :], N_EXPERTS), axis=1)
    next_eid = jnp.where(next_eid < N_EXPERTS, next_eid, -1).astype(jnp.int32)
    of_blk = blk_eid[:, None] == experts
    blk_end = jnp.sum(jnp.where(of_blk, (g_start + rows_e)[None, :], 0), axis=1)
    used = jnp.clip(blk_end - blk_start, 0, EXPERT_TILE).astype(jnp.int32)

    cpb = EXPERT_TILE // CHUNK
    run_start = gpos.T // CHUNK
    run_len = cntp.T // CHUNK
    run_src = (jnp.arange(nt, dtype=jnp.int32)[:, None] * (LT_MAX // CHUNK) + loff // CHUNK).T
    tabs = jnp.stack([run_start, run_len, run_src], axis=0)
    dtabs = tabs - jnp.concatenate(
        [jnp.zeros((3, N_EXPERTS, 1), jnp.int32), tabs[:, :, :-1]], axis=2)
    blk_sel = (blk_eid[:, None] == experts)[:, :, None]
    blk_start_t = jnp.sum(jnp.where(blk_sel, run_start[None], 0), axis=1)
    blk_dtabs = jnp.sum(jnp.where(blk_sel[None], dtabs[:, None], 0), axis=2)
    gc = jnp.arange(n_rows // CHUNK, dtype=jnp.int32).reshape(n_blocks, cpb)
    started = blk_start_t[:, None, :] <= gc[:, :, None]
    picked = jnp.sum(jnp.where(started[None], blk_dtabs[:, :, None, :], 0), axis=-1)
    within = gc - picked[0]
    csrc = jnp.where(within < picked[1], picked[2] + within, 0).reshape(-1)
    return (gch.astype(jnp.int32), csrc.astype(jnp.int32), blk_eid, first, next_eid, used,
            n_act, n_rows)


def kernel(x, norm_mix, w_in, q_norm, k_norm, sinks, w_pool, pool_scale, w_out,
           norm_ffn, w_coarse, b_coarse, w_fine, b_fine, w_gate, w_up, w_down):
    bsz, seq, d = x.shape
    n = bsz * seq
    assert d == D_MODEL and seq % ROW_TILE == 0 and seq % BLOCK == 0
    assert norm_mix.shape[0] == 1, "single-layer problem"
    xf = x.reshape(n, d)

    w_in_b = w_in[0].astype(BF16)
    wq = w_in_b[:, :ATTN_WIDTH]
    wk = w_in_b[:, ATTN_WIDTH:ATTN_WIDTH + KV_WIDTH]
    wv = w_in_b[:, ATTN_WIDTH + KV_WIDTH:ATTN_WIDTH + 2 * KV_WIDTH]
    wu = w_in_b[:, ATTN_WIDTH + 2 * KV_WIDTH:]
    qn = jnp.tile(q_norm[0], N_HEADS).reshape(1, ATTN_WIDTH)
    kn = jnp.tile(k_norm[0], N_KV_HEADS).reshape(1, KV_WIDTH)
    cc, s_up, s_dn = _rope_tables(seq)
    lane_head = jnp.arange(MXU_DIM) // HEAD_DIM
    e_mat = (lane_head[:, None] == lane_head[None, :]).astype(BF16)
    w_pool_b = w_pool[0].astype(BF16)
    pscale = pool_scale[0].reshape(1, POOL_WIDTH)
    w_out_b = w_out[0].astype(BF16)
    wo_attn, wo_pool = w_out_b[:ATTN_WIDTH], w_out_b[ATTN_WIDTH:]
    gap = ROUTER_FINE0 - N_EXPERT_GROUPS
    tail = LANES - ROUTER_FINE0 - N_EXPERTS
    w_r = jnp.concatenate([w_coarse[0], jnp.zeros((d, gap), F32), w_fine[0],
                           jnp.zeros((d, tail), F32)], axis=1)
    w_r_hi = w_r.astype(BF16)
    w_r_lo = (w_r - w_r_hi.astype(F32)).astype(BF16)
    w_r2 = jnp.concatenate([w_r_hi, w_r_lo], axis=1)
    b_r = jnp.concatenate([b_coarse[0], jnp.zeros((gap,), F32), b_fine[0],
                           jnp.zeros((tail,), F32)]).reshape(1, LANES)

    tm = ROW_TILE
    tiles_per_seq = seq // tm
    n_tiles = n // tm
    idx = jnp.arange(SUB_TILE)
    tri = (idx[None, :] < idx[:, None]).astype(BF16)
    const = lambda *_: (0, 0)
    row_blk = lambda i: (i, 0)

    q, k, v, ksw, vsw, y = pl.pallas_call(
        functools.partial(_inproj_kernel, tiles_per_seq=tiles_per_seq),
        grid=(n_tiles,),
        in_specs=[
            pl.BlockSpec((tm, d), row_blk),
            pl.BlockSpec((1, d), const),
            pl.BlockSpec((d, ATTN_WIDTH), const),
            pl.BlockSpec((d, KV_WIDTH), const),
            pl.BlockSpec((d, KV_WIDTH), const),
            pl.BlockSpec((d, POOL_WIDTH), const),
            pl.BlockSpec((1, ATTN_WIDTH), const),
            pl.BlockSpec((1, KV_WIDTH), const),
            pl.BlockSpec((tm, LANES), lambda i: (i % tiles_per_seq, 0)),
            pl.BlockSpec((tm, LANES), lambda i: (i % tiles_per_seq, 0)),
            pl.BlockSpec((tm, LANES), lambda i: (i % tiles_per_seq, 0)),
            pl.BlockSpec((MXU_DIM, MXU_DIM), const),
            pl.BlockSpec((len(POOL_WINDOWS), POOL_GROUP_DIM, POOL_GROUP_DIM),
                         lambda i: (0, 0, 0)),
            pl.BlockSpec((1, POOL_WIDTH), const),
        ],
        out_specs=[
            pl.BlockSpec((tm, ATTN_WIDTH), row_blk),
            pl.BlockSpec((tm, KV_WIDTH), row_blk),
            pl.BlockSpec((tm, KV_WIDTH), row_blk),
            pl.BlockSpec((tm, KV_WIDTH), row_blk),
            pl.BlockSpec((tm, KV_WIDTH), row_blk),
            pl.BlockSpec((tm, POOL_WIDTH), row_blk),
        ],
        out_shape=[
            jax.ShapeDtypeStruct((n, ATTN_WIDTH), BF16),
            jax.ShapeDtypeStruct((n, KV_WIDTH), BF16),
            jax.ShapeDtypeStruct((n, KV_WIDTH), BF16),
            jax.ShapeDtypeStruct((n, KV_WIDTH), BF16),
            jax.ShapeDtypeStruct((n, KV_WIDTH), BF16),
            jax.ShapeDtypeStruct((n, POOL_WIDTH), BF16),
        ],
        scratch_shapes=[pltpu.VMEM((HALO + tm, POOL_WIDTH), F32)],
        compiler_params=_cparams(),
        name="inproj",
    )(xf, norm_mix[0].reshape(1, d), wq, wk, wv, wu, qn, kn, cc, s_up, s_dn,
      e_mat, w_pool_b, pscale)

    nb = seq // BLOCK
    nq = nb // Q_BLOCKS
    qrows = Q_BLOCKS * BLOCK
    cur = lambda b, j, *_: (b * nq + j, 0)
    prv = lambda b, j, *_: (b * nb + jnp.maximum(Q_BLOCKS * j - 1, 0), 0)
    kv_specs = [pl.BlockSpec((qrows, KV_WIDTH), cur), pl.BlockSpec((BLOCK, KV_WIDTH), prv)]
    attn = pl.pallas_call(
        _attn_kernel,
        grid_spec=pltpu.PrefetchScalarGridSpec(
            num_scalar_prefetch=1,
            grid=(bsz, nq),
            in_specs=[pl.BlockSpec((qrows, ATTN_WIDTH), cur)] + kv_specs * 4,
            out_specs=pl.BlockSpec((qrows, ATTN_WIDTH), cur),
        ),
        out_shape=jax.ShapeDtypeStruct((n, ATTN_WIDTH), BF16),
        compiler_params=_cparams(2),
        name="swa_attn",
    )(sinks[0].astype(F32), q, k, k, ksw, ksw, v, v, vsw, vsw)

    h, hpk, route, cnt = pl.pallas_call(
        _outproj_router_kernel,
        grid=(n_tiles,),
        in_specs=[
            pl.BlockSpec((tm, ATTN_WIDTH), row_blk),
            pl.BlockSpec((tm, POOL_WIDTH), row_blk),
            pl.BlockSpec((tm, d), row_blk),
            pl.BlockSpec((ATTN_WIDTH, d), const),
            pl.BlockSpec((POOL_WIDTH, d), const),
            pl.BlockSpec((1, d), const),
            pl.BlockSpec((d, 2 * LANES), const),
            pl.BlockSpec((1, LANES), const),
            pl.BlockSpec((SUB_TILE, SUB_TILE), const),
        ],
        out_specs=[
            pl.BlockSpec((tm, d), row_blk),
            pl.BlockSpec((tm * PK_ROWS, LANES), row_blk),
            pl.BlockSpec((SUBLANES, tm), lambda i: (0, i)),
            pl.BlockSpec((N_EXPERTS, LANES), row_blk),
        ],
        out_shape=[
            jax.ShapeDtypeStruct((n, d), F32),
            jax.ShapeDtypeStruct((n * PK_ROWS, LANES), jnp.int32),
            jax.ShapeDtypeStruct((SUBLANES, n), F32),
            jax.ShapeDtypeStruct((n_tiles * N_EXPERTS, LANES), F32),
        ],
        compiler_params=_cparams(),
        name="outproj_router",
    )(attn, y, xf, wo_attn, wo_pool, norm_ffn[0].reshape(1, d), w_r2, b_r, tri)

    assert tm == LOCAL_TILE
    gates = route[0:TOP_K].reshape(TOP_K * n)
    lp = route[TOP_K:2 * TOP_K].astype(jnp.int32).reshape(TOP_K * n)
    tile_cnt = cnt.reshape(n_tiles, N_EXPERTS, LANES)[:, :, 0].astype(jnp.int32)
    te = EXPERT_TILE
    gch, csrc, blk_eid, first, next_eid, used, n_act, n_rows = _plan(tile_cnt, n)
    n_blocks = n_rows // te
    n_local = n // LOCAL_TILE

    xs = pl.pallas_call(
        _local_sort_kernel,
        grid_spec=pltpu.PrefetchScalarGridSpec(
            num_scalar_prefetch=1,
            grid=(n_local,),
            in_specs=[pl.BlockSpec((LOCAL_TILE * PK_ROWS, LANES), lambda i, *_: (i, 0))],
            out_specs=pl.BlockSpec((LT_MAX * PK_ROWS, LANES), lambda i, *_: (i, 0)),
        ),
        out_shape=jax.ShapeDtypeStruct((n_local * LT_MAX * PK_ROWS, LANES), jnp.int32),
        compiler_params=_cparams(),
        name="local_sort",
    )(lp * PK_ROWS, hpk)

    ys = pl.pallas_call(
        _expert_kernel,
        grid_spec=pltpu.PrefetchScalarGridSpec(
            num_scalar_prefetch=6,
            grid=(n_blocks,),
            in_specs=[pl.BlockSpec(memory_space=pl.ANY)] * 4,
            out_specs=pl.BlockSpec((te * PK_ROWS, LANES), lambda i, *_: (i, 0)),
            scratch_shapes=[
                pltpu.VMEM((2, te * PK_ROWS, LANES), jnp.int32),
                pltpu.VMEM((d, D_EXPERT), F32),
                pltpu.VMEM((d, D_EXPERT), F32),
                pltpu.VMEM((D_EXPERT, d), F32),
                pltpu.VMEM((d, D_EXPERT), BF16),
                pltpu.VMEM((d, D_EXPERT), BF16),
                pltpu.VMEM((D_EXPERT, d), BF16),
                pltpu.SemaphoreType.DMA((2,)),
                pltpu.SemaphoreType.DMA((3,)),
            ],
        ),
        out_shape=jax.ShapeDtypeStruct((n_rows * PK_ROWS, LANES), jnp.int32),
        compiler_params=_cparams(),
        name="experts",
    )(blk_eid, first, n_act, csrc * (CHUNK * PK_ROWS), next_eid, used, xs,
      w_gate[0], w_up[0], w_down[0])

    tt = LOCAL_TILE
    out = pl.pallas_call(
        _combine_kernel,
        grid_spec=pltpu.PrefetchScalarGridSpec(
            num_scalar_prefetch=3,
            grid=(n_local,),
            in_specs=[
                pl.BlockSpec((tt, d), lambda i, *_: (i, 0)),
                pl.BlockSpec(memory_space=pl.ANY),
            ],
            out_specs=pl.BlockSpec((tt, d), lambda i, *_: (i, 0)),
            scratch_shapes=[
                pltpu.VMEM((2, LT_MAX * PK_ROWS, LANES), jnp.int32),
                pltpu.VMEM((2, tt * PK_ROWS, LANES), F32),
                pltpu.SemaphoreType.DMA((2,)),
            ],
        ),
        out_shape=jax.ShapeDtypeStruct((n, d), F32),
        compiler_params=_cparams(),
        name="combine",
    )(gch * (CHUNK * PK_ROWS), lp * PK_ROWS, gates, h, ys)
    return out.reshape(bsz, seq, d)
```

```python
import functools

import jax
import jax.numpy as jnp
from jax import lax
from jax.experimental import pallas as pl
from jax.experimental.pallas import tpu as pltpu

F32 = jnp.float32
BF16 = jnp.bfloat16

D_MODEL = 2048
N_HEADS = 16
N_KV_HEADS = 4
HEAD_DIM = 64
GROUP = N_HEADS // N_KV_HEADS
ROT_DIM = HEAD_DIM // 4
ROPE_THETA = 500000.0
WINDOW = 128
BLOCK = 128
Q_BLOCKS = 16
ATTN_WIDTH = N_HEADS * HEAD_DIM
KV_WIDTH = N_KV_HEADS * HEAD_DIM
POOL_WINDOWS = (2, 4, 8, 16)
POOL_WIDTH = D_MODEL // 2
POOL_GROUP_DIM = POOL_WIDTH // len(POOL_WINDOWS)
N_EXPERT_GROUPS = 4
EXPERTS_PER_GROUP = 8
N_EXPERTS = N_EXPERT_GROUPS * EXPERTS_PER_GROUP
TOP_K = 2
D_EXPERT = 512
EPS = 1e-6

LANES = 128
SUBLANES = 8
MXU_DIM = 256
HALO = 16
NEG_BIG = -1e30
ROUTER_FINE0 = 8
HALF = D_MODEL // 2
PK_ROWS = HALF // LANES

ROW_TILE = 512
SUB_TILE = 512
EXPERT_TILE = 512
LOCAL_TILE = 512
CHUNK = 8
LT_MAX = TOP_K * LOCAL_TILE + N_EXPERTS * (CHUNK - 1)
VMEM_LIMIT = 56 * 1024 * 1024


def _cparams(n_axes=1):
    return pltpu.CompilerParams(
        dimension_semantics=("arbitrary",) * n_axes,
        vmem_limit_bytes=VMEM_LIMIT,
    )


def _pack_halves(t):
    return pltpu.pack_elementwise([t[:, :HALF], t[:, HALF:]], packed_dtype=BF16)


def _unpack_halves(w):
    return tuple(pltpu.unpack_elementwise(w, index=i, packed_dtype=BF16, unpacked_dtype=F32)
                 for i in range(2))


def _head_sumsq(t, e_ref):
    t2 = (t * t).astype(BF16)
    e = e_ref[...]
    outs = []
    for c in range(t.shape[1] // MXU_DIM):
        sl = slice(c * MXU_DIM, (c + 1) * MXU_DIM)
        outs.append(jnp.dot(t2[:, sl], e, preferred_element_type=F32))
    return outs[0] if len(outs) == 1 else jnp.concatenate(outs, axis=1)


def _norm_rope(t, gain, e_ref, cc, s_up, s_dn, out_scale):
    w = t.shape[1]
    ss = _head_sumsq(t, e_ref)
    tn = t * lax.rsqrt(ss * (1.0 / HEAD_DIM) + EPS) * gain
    reps = w // LANES
    up = pltpu.roll(tn, w - ROT_DIM // 2, axis=1)
    dn = pltpu.roll(tn, ROT_DIM // 2, axis=1)
    c = jnp.concatenate([cc] * reps, axis=1)
    su = jnp.concatenate([s_up] * reps, axis=1)
    sd = jnp.concatenate([s_dn] * reps, axis=1)
    return (tn * c + up * su + dn * sd) * out_scale


def _swap_head_pairs(t):
    w = t.shape[1]
    lane = lax.broadcasted_iota(jnp.int32, t.shape, 1)
    from_up = pltpu.roll(t, w - HEAD_DIM, axis=1)
    from_dn = pltpu.roll(t, HEAD_DIM, axis=1)
    return jnp.where((lane & (LANES - 1)) < HEAD_DIM, from_up, from_dn)


def _inproj_kernel(x_ref, g_ref, wq_ref, wk_ref, wv_ref, wu_ref, qn_ref, kn_ref,
                   cc_ref, su_ref, sd_ref, e_ref, wpool_ref, pscale_ref,
                   q_out, k_out, v_out, ksw_out, vsw_out, y_out, ubuf, *, tiles_per_seq):
    i = pl.program_id(0)
    tm = x_ref.shape[0]

    @pl.when(i % tiles_per_seq == 0)
    def _():
        ubuf[0:HALO, :] = jnp.zeros((HALO, POOL_WIDTH), F32)

    for r0 in range(0, tm, SUB_TILE):
        rows = slice(r0, r0 + SUB_TILE)
        x = x_ref[rows, :]
        ms = jnp.mean(x * x, axis=-1, keepdims=True)
        hn = (x * lax.rsqrt(ms + EPS) * g_ref[...]).astype(BF16)

        cc, s_up, s_dn = cc_ref[rows, :], su_ref[rows, :], sd_ref[rows, :]
        q = jnp.dot(hn, wq_ref[...], preferred_element_type=F32)
        q_out[rows, :] = _norm_rope(q, qn_ref[...], e_ref, cc, s_up, s_dn,
                                    HEAD_DIM ** -0.5).astype(BF16)
        k = jnp.dot(hn, wk_ref[...], preferred_element_type=F32)
        k = _norm_rope(k, kn_ref[...], e_ref, cc, s_up, s_dn, 1.0)
        v = jnp.dot(hn, wv_ref[...], preferred_element_type=F32)
        k_out[rows, :] = k.astype(BF16)
        v_out[rows, :] = v.astype(BF16)
        ksw_out[rows, :] = _swap_head_pairs(k).astype(BF16)
        vsw_out[rows, :] = _swap_head_pairs(v).astype(BF16)

        base = HALO + r0
        ubuf[base:base + SUB_TILE, :] = jnp.dot(hn, wu_ref[...], preferred_element_type=F32)
        pos = (i % tiles_per_seq) * tm + r0 + lax.broadcasted_iota(
            jnp.int32, (SUB_TILE, POOL_GROUP_DIM), 0)
        for g, w in enumerate(POOL_WINDOWS):
            cols = slice(g * POOL_GROUP_DIM, (g + 1) * POOL_GROUP_DIM)
            assert w & (w - 1) == 0 and w - 1 < HALO
            acc = ubuf[base - HALO:base + SUB_TILE, cols]
            shift = 1
            while shift < w:
                acc = acc + pltpu.roll(acc, shift, axis=0)
                shift *= 2
            acc = acc[HALO:, :]
            u_g = ubuf[base:base + SUB_TILE, cols]
            cnt = jnp.minimum(pos + 1, w).astype(F32)
            d = (acc / cnt - u_g).astype(BF16)
            y = jnp.dot(d, wpool_ref[g], preferred_element_type=F32)
            y_out[rows, cols] = (y * pscale_ref[:, cols]).astype(BF16)
    ubuf[0:HALO, :] = ubuf[tm:tm + HALO, :]


def _attn_kernel(sinks_ref, q_ref, kc_ref, kp_ref, kswc_ref, kswp_ref,
                 vc_ref, vp_ref, vswc_ref, vswp_ref, o_ref):
    j = pl.program_id(1)
    two = 2 * BLOCK
    row = lax.broadcasted_iota(jnp.int32, (two, two), 0) & (BLOCK - 1)
    col = lax.broadcasted_iota(jnp.int32, (two, two), 1)
    diff = row + BLOCK - col
    local = (diff >= 0) & (diff < WINDOW)
    lo_lanes = lax.broadcasted_iota(jnp.int32, (two, LANES), 1) < HEAD_DIM
    row_p = lax.broadcasted_iota(jnp.int32, (two, BLOCK), 0)
    diag = (row_p & (BLOCK - 1)) == lax.broadcasted_iota(jnp.int32, (two, BLOCK), 1)
    top_rows = row_p < BLOCK
    diag_top = diag & top_rows
    diag_bot = diag & jnp.logical_not(top_rows)
    zero = jnp.zeros((two, LANES), BF16)

    for sb in range(q_ref.shape[0] // BLOCK):
        rows = slice(sb * BLOCK, (sb + 1) * BLOCK)
        if sb == 0:
            def band(cur_ref, prev_ref):
                return jnp.concatenate([prev_ref[...], cur_ref[0:BLOCK, :]], axis=0)
            valid = local & ((col >= BLOCK) | (j > 0))
        else:
            def band(cur_ref, prev_ref, sb=sb):
                return cur_ref[(sb - 1) * BLOCK:(sb + 1) * BLOCK, :]
            valid = local
        k_nat, k_swp = band(kc_ref, kp_ref), band(kswc_ref, kswp_ref)
        v_nat, v_swp = band(vc_ref, vp_ref), band(vswc_ref, vswp_ref)

        for g in range(N_KV_HEADS):
            c0 = (g // 2) * LANES
            kcols = (k_nat[:, c0:c0 + LANES], k_swp[:, c0:c0 + LANES])
            vcols = (v_nat[:, c0:c0 + LANES], v_swp[:, c0:c0 + LANES])
            in_lo, in_hi = (0, 1) if g % 2 == 0 else (1, 0)
            q0 = g * GROUP * HEAD_DIM
            qq = jnp.concatenate([q_ref[rows, q0:q0 + LANES],
                                  q_ref[rows, q0 + LANES:q0 + 2 * LANES]], axis=0)
            halves = []
            for half, src in ((0, in_lo), (1, in_hi)):
                qm = jnp.where(lo_lanes, qq, zero) if half == 0 else jnp.where(lo_lanes, zero, qq)
                s = lax.dot_general(qm, kcols[src], (((1,), (1,)), ((), ())),
                                    preferred_element_type=F32)
                s = jnp.where(valid, s, NEG_BIG)
                h_top = g * GROUP + half
                s_prev = jnp.where(diag_top, sinks_ref[h_top],
                                   jnp.where(diag_bot, sinks_ref[h_top + 2], s[:, :BLOCK]))
                s_cur = s[:, BLOCK:]
                m = jnp.max(jnp.maximum(s_prev, s_cur), axis=-1, keepdims=True)
                p_prev = jnp.exp(s_prev - m)
                p_cur = jnp.exp(s_cur - m)
                denom = jnp.sum(p_prev + p_cur, axis=-1, keepdims=True)
                p = jnp.concatenate([jnp.where(diag, 0.0, p_prev), p_cur], axis=1)
                o = jnp.dot(p.astype(BF16), vcols[src], preferred_element_type=F32)
                halves.append(o / denom)
            o_pair = jnp.where(lo_lanes, halves[0], halves[1]).astype(BF16)
            o_ref[rows, q0:q0 + LANES] = o_pair[0:BLOCK, :]
            o_ref[rows, q0 + LANES:q0 + 2 * LANES] = o_pair[BLOCK:, :]


def _outproj_router_kernel(o_ref, y_ref, x_ref, woa_ref, wop_ref, g_ref, wr_ref,
                           br_ref, tri_ref, h_out, hpk_out, route_out, cnt_out):
    tm = x_ref.shape[0]
    h = (x_ref[...]
         + jnp.dot(o_ref[...], woa_ref[...], preferred_element_type=F32)
         + jnp.dot(y_ref[...], wop_ref[...], preferred_element_type=F32))
    h_out[...] = h
    ms = jnp.mean(h * h, axis=-1, keepdims=True)
    hn = h * lax.rsqrt(ms + EPS) * g_ref[...]
    hb = hn.astype(BF16)
    packed = _pack_halves(hn)
    for s in range(PK_ROWS):
        hpk_out[pl.ds(s, tm, stride=PK_ROWS), :] = packed[:, s * LANES:(s + 1) * LANES]

    both = jnp.dot(hb, wr_ref[...], preferred_element_type=F32)
    logits = (both[:, :LANES] + both[:, LANES:] + br_ref[...]).T
    assert EXPERTS_PER_GROUP == SUBLANES and ROUTER_FINE0 % SUBLANES == 0
    slot = lax.broadcasted_iota(jnp.int32, (SUBLANES, tm), 0).astype(F32)

    def col_max(t):
        return jnp.max(t, axis=0, keepdims=True)

    def first_slot_of(mask):
        return jnp.min(jnp.where(mask, slot, float(SUBLANES)), axis=0, keepdims=True)

    coarse = jnp.where(slot < N_EXPERT_GROUPS, logits[0:SUBLANES, :], NEG_BIG)
    cmax = col_max(coarse)
    grp = first_slot_of(coarse == cmax)
    csum = jnp.sum(jnp.exp(coarse - cmax), axis=0, keepdims=True)

    def fine_of(g):
        return logits[ROUTER_FINE0 + g * SUBLANES:ROUTER_FINE0 + (g + 1) * SUBLANES, :]

    fine = fine_of(N_EXPERT_GROUPS - 1)
    for g in range(N_EXPERT_GROUPS - 2, -1, -1):
        fine = jnp.where(grp == float(g), fine_of(g), fine)
    f1 = col_max(fine)
    i1 = first_slot_of(fine == f1)
    rest = jnp.where(slot == i1, NEG_BIG, fine)
    f2 = col_max(rest)
    i2 = first_slot_of(rest == f2)
    ratio = jnp.exp(f2 - f1)
    g1 = (1.0 / csum) / (1.0 + ratio)
    g2 = g1 * ratio
    e1 = grp * EXPERTS_PER_GROUP + i1
    e2 = grp * EXPERTS_PER_GROUP + i2

    expert = lax.broadcasted_iota(jnp.int32, (N_EXPERTS, tm), 0).astype(F32)
    oh1 = expert == e1
    oh2 = expert == e2
    oh = jnp.where(oh1, 1.0, 0.0) + jnp.where(oh2, 1.0, 0.0)
    rank = lax.dot_general(oh.astype(BF16), tri_ref[...], (((1,), (1,)), ((), ())),
                           preferred_element_type=F32)
    cnt = jnp.sum(oh, axis=1, keepdims=True)
    cnt_pad = jnp.floor((cnt + (CHUNK - 1)) * (1.0 / CHUNK)) * CHUNK
    padded_cols = jnp.concatenate(
        [jnp.broadcast_to(cnt_pad, (N_EXPERTS, LANES)),
         jnp.zeros((LANES - N_EXPERTS, LANES), F32)], axis=0).astype(BF16)
    run_start = jnp.dot(tri_ref[0:N_EXPERTS, 0:LANES], padded_cols,
                        preferred_element_type=F32)[:, 0:1]
    table = rank + run_start
    lp1 = jnp.sum(jnp.where(oh1, table, 0.0), axis=0, keepdims=True)
    lp2 = jnp.sum(jnp.where(oh2, table, 0.0), axis=0, keepdims=True)
    route_out[...] = jnp.where(slot == 0.0, g1,
                               jnp.where(slot == 1.0, g2,
                                         jnp.where(slot == 2.0, lp1,
                                                   jnp.where(slot == 3.0, lp2, 0.0))))
    cnt_out[...] = jnp.broadcast_to(cnt, cnt_out.shape)


def _expert_kernel(eid_ref, first_ref, nact_ref, off_ref, next_ref, used_ref,
                   hpk_hbm, wg_hbm, wu_hbm, wd_hbm, ys_out,
                   xbuf, wg_f, wu_f, wd_f, wg_s, wu_s, wd_s, sems, wsems):
    i = pl.program_id(0)
    tm = ys_out.shape[0] // PK_ROWS
    n_act = nact_ref[0]

    def weight_copies(e):
        return (pltpu.make_async_copy(wg_hbm.at[e], wg_f, wsems.at[0]),
                pltpu.make_async_copy(wu_hbm.at[e], wu_f, wsems.at[1]),
                pltpu.make_async_copy(wd_hbm.at[e], wd_f, wsems.at[2]))

    ch_rows = CHUNK * PK_ROWS
    n_ch = tm // CHUNK

    def gather(blk, slot):
        for c in range(n_ch):
            src = pl.multiple_of(off_ref[blk * n_ch + c], ch_rows)
            pltpu.make_async_copy(hpk_hbm.at[pl.ds(src, ch_rows)],
                                  xbuf.at[slot, pl.ds(c * ch_rows, ch_rows)],
                                  sems.at[slot]).start()

    @pl.when(i == 0)
    def _():
        for cp in weight_copies(eid_ref[0]):
            cp.start()
        gather(0, 0)

    @pl.when(i + 1 < n_act)
    def _():
        gather(i + 1, (i + 1) % 2)

    @pl.when(i < n_act)
    def _():
        slot = i % 2

        @pl.when(first_ref[i] == 1)
        def _():
            for cp in weight_copies(eid_ref[i]):
                cp.wait()
            wg_s[...] = wg_f[...].astype(BF16)
            wu_s[...] = wu_f[...].astype(BF16)
            wd_s[...] = wd_f[...].astype(BF16)

            @pl.when(next_ref[i] >= 0)
            def _():
                for cp in weight_copies(next_ref[i]):
                    cp.start()

        pltpu.make_async_copy(hpk_hbm.at[pl.ds(0, tm * PK_ROWS)], xbuf.at[slot],
                              sems.at[slot]).wait()

        def swiglu(rows):
            xu = jnp.concatenate(
                [xbuf[slot, pl.ds(s, rows, stride=PK_ROWS), :] for s in range(PK_ROWS)],
                axis=1)
            x_lo, x_hi = (v.astype(BF16) for v in _unpack_halves(xu))
            a = (jnp.dot(x_lo, wg_s[0:HALF, :], preferred_element_type=F32)
                 + jnp.dot(x_hi, wg_s[HALF:, :], preferred_element_type=F32))
            b = (jnp.dot(x_lo, wu_s[0:HALF, :], preferred_element_type=F32)
                 + jnp.dot(x_hi, wu_s[HALF:, :], preferred_element_type=F32))
            mid = (a * jax.nn.sigmoid(a) * b).astype(BF16)
            y = jnp.dot(mid, wd_s[...], preferred_element_type=F32)
            yw = _pack_halves(y)
            for s in range(PK_ROWS):
                ys_out[pl.ds(s, rows, stride=PK_ROWS), :] = yw[:, s * LANES:(s + 1) * LANES]
            if rows < tm:
                ys_out[rows * PK_ROWS:, :] = jnp.zeros(((tm - rows) * PK_ROWS, LANES),
                                                       ys_out.dtype)

        quarter = tm // 4
        for rows in range(quarter, tm + 1, quarter):
            @pl.when((used_ref[i] > rows - quarter) & (used_ref[i] <= rows))
            def _(rows=rows):
                swiglu(rows)

    @pl.when(i >= n_act)
    def _():
        ys_out[...] = jnp.zeros(ys_out.shape, ys_out.dtype)


def _local_sort_kernel(lp_ref, hpk_ref, xs_out):
    i = pl.program_id(0)
    t_tile = hpk_ref.shape[0] // PK_ROWS
    n_tok = pl.num_programs(0) * t_tile
    xs_out[...] = jnp.zeros(xs_out.shape, xs_out.dtype)

    def body(q, carry):
        for u in range(SUBLANES):
            t = q * SUBLANES + u
            row = hpk_ref[pl.ds(pl.multiple_of(t * PK_ROWS, PK_ROWS), PK_ROWS), :]
            for k in range(TOP_K):
                dst = pl.multiple_of(lp_ref[k * n_tok + i * t_tile + t], PK_ROWS)
                xs_out[pl.ds(dst, PK_ROWS), :] = row
        return carry
    lax.fori_loop(0, t_tile // SUBLANES, body, 0)


def _combine_kernel(gch_ref, lp_ref, gate_ref, h_ref, ys_hbm, out_ref, ylocal, acc, sems):
    i = pl.program_id(0)
    n = pl.num_programs(0)
    t_tile = h_ref.shape[0]
    n_tok = n * t_tile
    ch_rows = CHUNK * PK_ROWS
    n_ch = ylocal.shape[1] // ch_rows

    def copy_rows(src, slot, dst, rows):
        pltpu.make_async_copy(ys_hbm.at[pl.ds(src, rows)],
                              ylocal.at[slot, pl.ds(dst, rows)],
                              sems.at[slot]).start()

    def chunk_copy(tile, slot, lc):
        src = pl.multiple_of(gch_ref[tile * n_ch + lc], ch_rows)
        copy_rows(src, slot, pl.multiple_of(lc * ch_rows, ch_rows), ch_rows)

    def chunk_pair_copy(tile, slot, lc):
        src0 = pl.multiple_of(gch_ref[tile * n_ch + lc], ch_rows)
        src1 = pl.multiple_of(gch_ref[tile * n_ch + lc + 1], ch_rows)
        dst = pl.multiple_of(lc * ch_rows, ch_rows)
        joined = src1 == src0 + ch_rows

        @pl.when(joined)
        def _():
            copy_rows(src0, slot, dst, 2 * ch_rows)

        @pl.when(jnp.logical_not(joined))
        def _():
            copy_rows(src0, slot, dst, ch_rows)
            copy_rows(src1, slot, dst + ch_rows, ch_rows)

    def wait_slot(slot):
        pltpu.make_async_copy(ys_hbm.at[pl.ds(0, n_ch * ch_rows)], ylocal.at[slot],
                              sems.at[slot]).wait()

    @pl.when(i == 0)
    def _():
        def body(lc, carry):
            chunk_copy(0, 0, lc)
            return carry
        lax.fori_loop(0, n_ch, body, 0)

    slot = i % 2
    wait_slot(slot)

    nxt = jnp.minimum(i + 1, n - 1)
    tok_unroll = 4
    pairs_per_group = 1
    n_groups = t_tile // tok_unroll
    issue_groups = n_ch // (2 * pairs_per_group)
    assert issue_groups * 2 * pairs_per_group == n_ch and issue_groups <= n_groups

    def tok_body(q, carry, issue):
        if issue:
            for u in range(pairs_per_group):
                chunk_pair_copy(nxt, 1 - slot, 2 * (q * pairs_per_group + u))
        for u in range(tok_unroll):
            t = q * tok_unroll + u
            a0 = i * t_tile + t
            a1 = n_tok + a0
            p0 = pl.multiple_of(lp_ref[a0], PK_ROWS)
            p1 = pl.multiple_of(lp_ref[a1], PK_ROWS)
            lo0, hi0 = _unpack_halves(ylocal[slot, pl.ds(p0, PK_ROWS), :])
            lo1, hi1 = _unpack_halves(ylocal[slot, pl.ds(p1, PK_ROWS), :])
            g0 = gate_ref[a0]
            g1 = gate_ref[a1]
            dst = pl.multiple_of(t * PK_ROWS, PK_ROWS)
            acc[0, pl.ds(dst, PK_ROWS), :] = g0 * lo0 + g1 * lo1
            acc[1, pl.ds(dst, PK_ROWS), :] = g0 * hi0 + g1 * hi1
        return carry
    lax.fori_loop(0, issue_groups, functools.partial(tok_body, issue=True), 0)
    lax.fori_loop(issue_groups, n_groups, functools.partial(tok_body, issue=False), 0)

    @pl.when(i == n - 1)
    def _():
        wait_slot(1 - slot)

    for half in range(2):
        for s in range(PK_ROWS):
            c0 = half * HALF + s * LANES
            out_ref[:, c0:c0 + LANES] = (h_ref[:, c0:c0 + LANES]
                                         + acc[half, pl.ds(s, t_tile, stride=PK_ROWS), :])


def _rope_tables(seq):
    pos = jnp.arange(seq, dtype=F32)
    inv_freq = ROPE_THETA ** (-jnp.arange(0, ROT_DIM, 2, dtype=F32) / ROT_DIM)
    ang = pos[:, None] * inv_freq[None, :]
    cos, sin = jnp.cos(ang), jnp.sin(ang)
    half = ROT_DIM // 2
    ones = jnp.ones((seq, HEAD_DIM - ROT_DIM), F32)
    zeros = jnp.zeros((seq, HEAD_DIM - ROT_DIM), F32)
    zh = jnp.zeros((seq, half), F32)
    cc = jnp.concatenate([cos, cos, ones], axis=1)
    s_up = jnp.concatenate([-sin, zh, zeros], axis=1)
    s_dn = jnp.concatenate([zh, sin, zeros], axis=1)
    rep = LANES // HEAD_DIM
    return tuple(jnp.tile(t, (1, rep)) for t in (cc, s_up, s_dn))


def _plan(cnt, n_tokens):
    nt = n_tokens // LOCAL_TILE
    experts = jnp.arange(N_EXPERTS, dtype=jnp.int32)
    cntp = ((cnt + CHUNK - 1) // CHUNK) * CHUNK
    loff_end = jnp.cumsum(cntp, axis=1)
    loff = loff_end - cntp
    rows_e = jnp.sum(cntp, axis=0)
    rows_pad = ((rows_e + EXPERT_TILE - 1) // EXPERT_TILE) * EXPERT_TILE
    g_end = jnp.cumsum(rows_pad)
    g_start = g_end - rows_pad
    gpos = g_start[None, :] + jnp.cumsum(cntp, axis=0) - cntp

    lrow = jnp.arange(LT_MAX // CHUNK, dtype=jnp.int32) * CHUNK
    e_lc = jnp.sum((lrow[None, :, None] >= loff_end[:, None, :]).astype(jnp.int32), axis=-1)
    sel = e_lc[:, :, None] == experts
    delta = jnp.sum(jnp.where(sel, (gpos - loff)[:, None, :], 0), axis=-1)
    gch = jnp.where(e_lc < N_EXPERTS, (delta + lrow[None, :]) // CHUNK, 0).reshape(-1)

    n_rows = TOP_K * n_tokens + nt * N_EXPERTS * (CHUNK - 1) + N_EXPERTS * (EXPERT_TILE - CHUNK)
    n_blocks = -(-n_rows // EXPERT_TILE)
    n_rows = n_blocks * EXPERT_TILE
    blk_start = jnp.arange(n_blocks, dtype=jnp.int32) * EXPERT_TILE
    blk_eid = jnp.minimum(
        jnp.sum((blk_start[:, None] >= g_end[None, :]).astype(jnp.int32), axis=-1),
        N_EXPERTS - 1)
    prev = jnp.concatenate([jnp.full((1,), -1, jnp.int32), blk_eid[:-1]])
    first = (blk_eid != prev).astype(jnp.int32)
    n_act = (g_end[-1] // EXPERT_TILE).astype(jnp.int32).reshape(1)
    later = (blk_eid[None, :] > blk_eid[:, None]) & (jnp.arange(n_blocks)[None, :] < n_act)
    next_eid = jnp.min(jnp.where(later, blk_eid[None, :], N_EXPERTS), axis=1)
    next_eid = jnp.where(next_eid < N_EXPERTS, next_eid, -1).astype(jnp.int32)
    of_blk = blk_eid[:, None] == experts
    blk_end = jnp.sum(jnp.where(of_blk, (g_start + rows_e)[None, :], 0), axis=1)
    used = jnp.clip(blk_end - blk_start, 0, EXPERT_TILE).astype(jnp.int32)

    cpb = EXPERT_TILE // CHUNK
    run_start = gpos.T // CHUNK
    run_len = cntp.T // CHUNK
    run_src = (jnp.arange(nt, dtype=jnp.int32)[:, None] * (LT_MAX // CHUNK) + loff // CHUNK).T
    tabs = jnp.stack([run_start, run_len, run_src], axis=0)
    dtabs = tabs - jnp.concatenate(
        [jnp.zeros((3, N_EXPERTS, 1), jnp.int32), tabs[:, :, :-1]], axis=2)
    blk_sel = (blk_eid[:, None] == experts)[:, :, None]
    blk_start_t = jnp.sum(jnp.where(blk_sel, run_start[None], 0), axis=1)
    blk_dtabs = jnp.sum(jnp.where(blk_sel[None], dtabs[:, None], 0), axis=2)
    gc = jnp.arange(n_rows // CHUNK, dtype=jnp.int32).reshape(n_blocks, cpb)
    started = blk_start_t[:, None, :] <= gc[:, :, None]
    picked = jnp.sum(jnp.where(started[None], blk_dtabs[:, :, None, :], 0), axis=-1)
    within = gc - picked[0]
    csrc = jnp.where(within < picked[1], picked[2] + within, 0).reshape(-1)
    return (gch.astype(jnp.int32), csrc.astype(jnp.int32), blk_eid, first, next_eid, used,
            n_act, n_rows)


def kernel(x, norm_mix, w_in, q_norm, k_norm, sinks, w_pool, pool_scale, w_out,
           norm_ffn, w_coarse, b_coarse, w_fine, b_fine, w_gate, w_up, w_down):
    bsz, seq, d = x.shape
    n = bsz * seq
    assert d == D_MODEL and seq % ROW_TILE == 0 and seq % BLOCK == 0
    assert norm_mix.shape[0] == 1, "single-layer problem"
    xf = x.reshape(n, d)

    w_in_b = w_in[0].astype(BF16)
    wq = w_in_b[:, :ATTN_WIDTH]
    wk = w_in_b[:, ATTN_WIDTH:ATTN_WIDTH + KV_WIDTH]
    wv = w_in_b[:, ATTN_WIDTH + KV_WIDTH:ATTN_WIDTH + 2 * KV_WIDTH]
    wu = w_in_b[:, ATTN_WIDTH + 2 * KV_WIDTH:]
    qn = jnp.tile(q_norm[0], N_HEADS).reshape(1, ATTN_WIDTH)
    kn = jnp.tile(k_norm[0], N_KV_HEADS).reshape(1, KV_WIDTH)
    cc, s_up, s_dn = _rope_tables(seq)
    lane_head = jnp.arange(MXU_DIM) // HEAD_DIM
    e_mat = (lane_head[:, None] == lane_head[None, :]).astype(BF16)
    w_pool_b = w_pool[0].astype(BF16)
    pscale = pool_scale[0].reshape(1, POOL_WIDTH)
    w_out_b = w_out[0].astype(BF16)
    wo_attn, wo_pool = w_out_b[:ATTN_WIDTH], w_out_b[ATTN_WIDTH:]
    gap = ROUTER_FINE0 - N_EXPERT_GROUPS
    tail = LANES - ROUTER_FINE0 - N_EXPERTS
    w_r = jnp.concatenate([w_coarse[0], jnp.zeros((d, gap), F32), w_fine[0],
                           jnp.zeros((d, tail), F32)], axis=1)
    w_r_hi = w_r.astype(BF16)
    w_r_lo = (w_r - w_r_hi.astype(F32)).astype(BF16)
    w_r2 = jnp.concatenate([w_r_hi, w_r_lo], axis=1)
    b_r = jnp.concatenate([b_coarse[0], jnp.zeros((gap,), F32), b_fine[0],
                           jnp.zeros((tail,), F32)]).reshape(1, LANES)

    tm = ROW_TILE
    tiles_per_seq = seq // tm
    n_tiles = n // tm
    idx = jnp.arange(SUB_TILE)
    tri = (idx[None, :] < idx[:, None]).astype(BF16)
    const = lambda *_: (0, 0)
    row_blk = lambda i: (i, 0)

    q, k, v, ksw, vsw, y = pl.pallas_call(
        functools.partial(_inproj_kernel, tiles_per_seq=tiles_per_seq),
        grid=(n_tiles,),
        in_specs=[
            pl.BlockSpec((tm, d), row_blk),
            pl.BlockSpec((1, d), const),
            pl.BlockSpec((d, ATTN_WIDTH), const),
            pl.BlockSpec((d, KV_WIDTH), const),
            pl.BlockSpec((d, KV_WIDTH), const),
            pl.BlockSpec((d, POOL_WIDTH), const),
            pl.BlockSpec((1, ATTN_WIDTH), const),
            pl.BlockSpec((1, KV_WIDTH), const),
            pl.BlockSpec((tm, LANES), lambda i: (i % tiles_per_seq, 0)),
            pl.BlockSpec((tm, LANES), lambda i: (i % tiles_per_seq, 0)),
            pl.BlockSpec((tm, LANES), lambda i: (i % tiles_per_seq, 0)),
            pl.BlockSpec((MXU_DIM, MXU_DIM), const),
            pl.BlockSpec((len(POOL_WINDOWS), POOL_GROUP_DIM, POOL_GROUP_DIM),
                         lambda i: (0, 0, 0)),
            pl.BlockSpec((1, POOL_WIDTH), const),
        ],
        out_specs=[
            pl.BlockSpec((tm, ATTN_WIDTH), row_blk),
            pl.BlockSpec((tm, KV_WIDTH), row_blk),
            pl.BlockSpec((tm, KV_WIDTH), row_blk),
            pl.BlockSpec((tm, KV_WIDTH), row_blk),
            pl.BlockSpec((tm, KV_WIDTH), row_blk),
            pl.BlockSpec((tm, POOL_WIDTH), row_blk),
        ],
        out_shape=[
            jax.ShapeDtypeStruct((n, ATTN_WIDTH), BF16),
            jax.ShapeDtypeStruct((n, KV_WIDTH), BF16),
            jax.ShapeDtypeStruct((n, KV_WIDTH), BF16),
            jax.ShapeDtypeStruct((n, KV_WIDTH), BF16),
            jax.ShapeDtypeStruct((n, KV_WIDTH), BF16),
            jax.ShapeDtypeStruct((n, POOL_WIDTH), BF16),
        ],
        scratch_shapes=[pltpu.VMEM((HALO + tm, POOL_WIDTH), F32)],
        compiler_params=_cparams(),
        name="inproj",
    )(xf, norm_mix[0].reshape(1, d), wq, wk, wv, wu, qn, kn, cc, s_up, s_dn,
      e_mat, w_pool_b, pscale)

    nb = seq // BLOCK
    nq = nb // Q_BLOCKS
    qrows = Q_BLOCKS * BLOCK
    cur = lambda b, j, *_: (b * nq + j, 0)
    prv = lambda b, j, *_: (b * nb + jnp.maximum(Q_BLOCKS * j - 1, 0), 0)
    kv_specs = [pl.BlockSpec((qrows, KV_WIDTH), cur), pl.BlockSpec((BLOCK, KV_WIDTH), prv)]
    attn = pl.pallas_call(
        _attn_kernel,
        grid_spec=pltpu.PrefetchScalarGridSpec(
            num_scalar_prefetch=1,
            grid=(bsz, nq),
            in_specs=[pl.BlockSpec((qrows, ATTN_WIDTH), cur)] + kv_specs * 4,
            out_specs=pl.BlockSpec((qrows, ATTN_WIDTH), cur),
        ),
        out_shape=jax.ShapeDtypeStruct((n, ATTN_WIDTH), BF16),
        compiler_params=_cparams(2),
        name="swa_attn",
    )(sinks[0].astype(F32), q, k, k, ksw, ksw, v, v, vsw, vsw)

    h, hpk, route, cnt = pl.pallas_call(
        _outproj_router_kernel,
        grid=(n_tiles,),
        in_specs=[
            pl.BlockSpec((tm, ATTN_WIDTH), row_blk),
            pl.BlockSpec((tm, POOL_WIDTH), row_blk),
            pl.BlockSpec((tm, d), row_blk),
            pl.BlockSpec((ATTN_WIDTH, d), const),
            pl.BlockSpec((POOL_WIDTH, d), const),
            pl.BlockSpec((1, d), const),
            pl.BlockSpec((d, 2 * LANES), const),
            pl.BlockSpec((1, LANES), const),
            pl.BlockSpec((SUB_TILE, SUB_TILE), const),
        ],
        out_specs=[
            pl.BlockSpec((tm, d), row_blk),
            pl.BlockSpec((tm * PK_ROWS, LANES), row_blk),
            pl.BlockSpec((SUBLANES, tm), lambda i: (0, i)),
            pl.BlockSpec((N_EXPERTS, LANES), row_blk),
        ],
        out_shape=[
            jax.ShapeDtypeStruct((n, d), F32),
            jax.ShapeDtypeStruct((n * PK_ROWS, LANES), jnp.int32),
            jax.ShapeDtypeStruct((SUBLANES, n), F32),
            jax.ShapeDtypeStruct((n_tiles * N_EXPERTS, LANES), F32),
        ],
        compiler_params=_cparams(),
        name="outproj_router",
    )(attn, y, xf, wo_attn, wo_pool, norm_ffn[0].reshape(1, d), w_r2, b_r, tri)

    assert tm == LOCAL_TILE
    gates = route[0:TOP_K].reshape(TOP_K * n)
    lp = route[TOP_K:2 * TOP_K].astype(jnp.int32).reshape(TOP_K * n)
    tile_cnt = cnt.reshape(n_tiles, N_EXPERTS, LANES)[:, :, 0].astype(jnp.int32)
    te = EXPERT_TILE
    gch, csrc, blk_eid, first, next_eid, used, n_act, n_rows = _plan(tile_cnt, n)
    n_blocks = n_rows // te
    n_local = n // LOCAL_TILE

    xs = pl.pallas_call(
        _local_sort_kernel,
        grid_spec=pltpu.PrefetchScalarGridSpec(
            num_scalar_prefetch=1,
            grid=(n_local,),
            in_specs=[pl.BlockSpec((LOCAL_TILE * PK_ROWS, LANES), lambda i, *_: (i, 0))],
            out_specs=pl.BlockSpec((LT_MAX * PK_ROWS, LANES), lambda i, *_: (i, 0)),
        ),
        out_shape=jax.ShapeDtypeStruct((n_local * LT_MAX * PK_ROWS, LANES), jnp.int32),
        compiler_params=_cparams(),
        name="local_sort",
    )(lp * PK_ROWS, hpk)

    ys = pl.pallas_call(
        _expert_kernel,
        grid_spec=pltpu.PrefetchScalarGridSpec(
            num_scalar_prefetch=6,
            grid=(n_blocks,),
            in_specs=[pl.BlockSpec(memory_space=pl.ANY)] * 4,
            out_specs=pl.BlockSpec((te * PK_ROWS, LANES), lambda i, *_: (i, 0)),
            scratch_shapes=[
                pltpu.VMEM((2, te * PK_ROWS, LANES), jnp.int32),
                pltpu.VMEM((d, D_EXPERT), F32),
                pltpu.VMEM((d, D_EXPERT), F32),
                pltpu.VMEM((D_EXPERT, d), F32),
                pltpu.VMEM((d, D_EXPERT), BF16),
                pltpu.VMEM((d, D_EXPERT), BF16),
                pltpu.VMEM((D_EXPERT, d), BF16),
                pltpu.SemaphoreType.DMA((2,)),
                pltpu.SemaphoreType.DMA((3,)),
            ],
        ),
        out_shape=jax.ShapeDtypeStruct((n_rows * PK_ROWS, LANES), jnp.int32),
        compiler_params=_cparams(),
        name="experts",
    )(blk_eid, first, n_act, csrc * (CHUNK * PK_ROWS), next_eid, used, xs,
      w_gate[0], w_up[0], w_down[0])

    tt = LOCAL_TILE
    out = pl.pallas_call(
        _combine_kernel,
        grid_spec=pltpu.PrefetchScalarGridSpec(
            num_scalar_prefetch=3,
            grid=(n_local,),
            in_specs=[
                pl.BlockSpec((tt, d), lambda i, *_: (i, 0)),
                pl.BlockSpec(memory_space=pl.ANY),
            ],
            out_specs=pl.BlockSpec((tt, d), lambda i, *_: (i, 0)),
            scratch_shapes=[
                pltpu.VMEM((2, LT_MAX * PK_ROWS, LANES), jnp.int32),
                pltpu.VMEM((2, tt * PK_ROWS, LANES), F32),
                pltpu.SemaphoreType.DMA((2,)),
            ],
        ),
        out_shape=jax.ShapeDtypeStruct((n, d), F32),
        compiler_params=_cparams(),
        name="combine",
    )(gch * (CHUNK * PK_ROWS), lp * PK_ROWS, gates, h, ys)
    return out.reshape(bsz, seq, d)
```

```python
import functools

import jax
import jax.numpy as jnp
from jax import lax
from jax.experimental import pallas as pl
from jax.experimental.pallas import tpu as pltpu

F32 = jnp.float32
BF16 = jnp.bfloat16

D_MODEL = 2048
N_HEADS = 16
N_KV_HEADS = 4
HEAD_DIM = 64
GROUP = N_HEADS // N_KV_HEADS
ROT_DIM = HEAD_DIM // 4
ROPE_THETA = 500000.0
WINDOW = 128
BLOCK = 128
Q_BLOCKS = 16
ATTN_WIDTH = N_HEADS * HEAD_DIM
KV_WIDTH = N_KV_HEADS * HEAD_DIM
POOL_WINDOWS = (2, 4, 8, 16)
POOL_WIDTH = D_MODEL // 2
POOL_GROUP_DIM = POOL_WIDTH // len(POOL_WINDOWS)
N_EXPERT_GROUPS = 4
EXPERTS_PER_GROUP = 8
N_EXPERTS = N_EXPERT_GROUPS * EXPERTS_PER_GROUP
TOP_K = 2
D_EXPERT = 512
EPS = 1e-6

LANES = 128
SUBLANES = 8
MXU_DIM = 256
HALO = 16
NEG_BIG = -1e30
ROUTER_FINE0 = 8
HALF = D_MODEL // 2
PK_ROWS = HALF // LANES

ROW_TILE = 512
SUB_TILE = 512
EXPERT_TILE = 512
LOCAL_TILE = 512
CHUNK = 8
LT_MAX = TOP_K * LOCAL_TILE + N_EXPERTS * (CHUNK - 1)
VMEM_LIMIT = 56 * 1024 * 1024


def _cparams(n_axes=1):
    return pltpu.CompilerParams(
        dimension_semantics=("arbitrary",) * n_axes,
        vmem_limit_bytes=VMEM_LIMIT,
    )


def _pack_halves(t):
    return pltpu.pack_elementwise([t[:, :HALF], t[:, HALF:]], packed_dtype=BF16)


def _unpack_halves(w):
    return tuple(pltpu.unpack_elementwise(w, index=i, packed_dtype=BF16, unpacked_dtype=F32)
                 for i in range(2))


def _head_sumsq(t, e_ref):
    t2 = (t * t).astype(BF16)
    e = e_ref[...]
    outs = []
    for c in range(t.shape[1] // MXU_DIM):
        sl = slice(c * MXU_DIM, (c + 1) * MXU_DIM)
        outs.append(jnp.dot(t2[:, sl], e, preferred_element_type=F32))
    return outs[0] if len(outs) == 1 else jnp.concatenate(outs, axis=1)


def _norm_rope(t, gain, e_ref, cc, s_up, s_dn, out_scale):
    w = t.shape[1]
    ss = _head_sumsq(t, e_ref)
    tn = t * lax.rsqrt(ss * (1.0 / HEAD_DIM) + EPS) * gain
    reps = w // LANES
    up = pltpu.roll(tn, w - ROT_DIM // 2, axis=1)
    dn = pltpu.roll(tn, ROT_DIM // 2, axis=1)
    c = jnp.concatenate([cc] * reps, axis=1)
    su = jnp.concatenate([s_up] * reps, axis=1)
    sd = jnp.concatenate([s_dn] * reps, axis=1)
    return (tn * c + up * su + dn * sd) * out_scale


def _swap_head_pairs(t):
    w = t.shape[1]
    lane = lax.broadcasted_iota(jnp.int32, t.shape, 1)
    from_up = pltpu.roll(t, w - HEAD_DIM, axis=1)
    from_dn = pltpu.roll(t, HEAD_DIM, axis=1)
    return jnp.where((lane & (LANES - 1)) < HEAD_DIM, from_up, from_dn)


def _inproj_kernel(x_ref, g_ref, wq_ref, wk_ref, wv_ref, wu_ref, qn_ref, kn_ref,
                   cc_ref, su_ref, sd_ref, e_ref, wpool_ref, pscale_ref,
                   q_out, k_out, v_out, ksw_out, vsw_out, y_out, ubuf, *, tiles_per_seq):
    i = pl.program_id(0)
    tm = x_ref.shape[0]

    @pl.when(i % tiles_per_seq == 0)
    def _():
        ubuf[0:HALO, :] = jnp.zeros((HALO, POOL_WIDTH), F32)

    for r0 in range(0, tm, SUB_TILE):
        rows = slice(r0, r0 + SUB_TILE)
        x = x_ref[rows, :]
        ms = jnp.mean(x * x, axis=-1, keepdims=True)
        hn = (x * lax.rsqrt(ms + EPS) * g_ref[...]).astype(BF16)

        cc, s_up, s_dn = cc_ref[rows, :], su_ref[rows, :], sd_ref[rows, :]
        q = jnp.dot(hn, wq_ref[...], preferred_element_type=F32)
        q_out[rows, :] = _norm_rope(q, qn_ref[...], e_ref, cc, s_up, s_dn,
                                    HEAD_DIM ** -0.5).astype(BF16)
        k = jnp.dot(hn, wk_ref[...], preferred_element_type=F32)
        k = _norm_rope(k, kn_ref[...], e_ref, cc, s_up, s_dn, 1.0)
        v = jnp.dot(hn, wv_ref[...], preferred_element_type=F32)
        k_out[rows, :] = k.astype(BF16)
        v_out[rows, :] = v.astype(BF16)
        ksw_out[rows, :] = _swap_head_pairs(k).astype(BF16)
        vsw_out[rows, :] = _swap_head_pairs(v).astype(BF16)

        base = HALO + r0
        ubuf[base:base + SUB_TILE, :] = jnp.dot(hn, wu_ref[...], preferred_element_type=F32)
        pos = (i % tiles_per_seq) * tm + r0 + lax.broadcasted_iota(
            jnp.int32, (SUB_TILE, POOL_GROUP_DIM), 0)
        for g, w in enumerate(POOL_WINDOWS):
            cols = slice(g * POOL_GROUP_DIM, (g + 1) * POOL_GROUP_DIM)
            assert w & (w - 1) == 0 and w - 1 < HALO
            acc = ubuf[base - HALO:base + SUB_TILE, cols]
            shift = 1
            while shift < w:
                acc = acc + pltpu.roll(acc, shift, axis=0)
                shift *= 2
            acc = acc[HALO:, :]
            u_g = ubuf[base:base + SUB_TILE, cols]
            cnt = jnp.minimum(pos + 1, w).astype(F32)
            d = (acc / cnt - u_g).astype(BF16)
            y = jnp.dot(d, wpool_ref[g], preferred_element_type=F32)
            y_out[rows, cols] = (y * pscale_ref[:, cols]).astype(BF16)
    ubuf[0:HALO, :] = ubuf[tm:tm + HALO, :]


def _attn_kernel(sinks_ref, q_ref, kc_ref, kp_ref, kswc_ref, kswp_ref,
                 vc_ref, vp_ref, vswc_ref, vswp_ref, o_ref):
    j = pl.program_id(1)
    two = 2 * BLOCK
    row = lax.broadcasted_iota(jnp.int32, (two, two), 0) & (BLOCK - 1)
    col = lax.broadcasted_iota(jnp.int32, (two, two), 1)
    diff = row + BLOCK - col
    local = (diff >= 0) & (diff < WINDOW)
    lo_lanes = lax.broadcasted_iota(jnp.int32, (two, LANES), 1) < HEAD_DIM
    row_p = lax.broadcasted_iota(jnp.int32, (two, BLOCK), 0)
    diag = (row_p & (BLOCK - 1)) == lax.broadcasted_iota(jnp.int32, (two, BLOCK), 1)
    top_rows = row_p < BLOCK
    diag_top = diag & top_rows
    diag_bot = diag & jnp.logical_not(top_rows)
    zero = jnp.zeros((two, LANES), BF16)

    for sb in range(q_ref.shape[0] // BLOCK):
        rows = slice(sb * BLOCK, (sb + 1) * BLOCK)
        if sb == 0:
            def band(cur_ref, prev_ref):
                return jnp.concatenate([prev_ref[...], cur_ref[0:BLOCK, :]], axis=0)
            valid = local & ((col >= BLOCK) | (j > 0))
        else:
            def band(cur_ref, prev_ref, sb=sb):
                return cur_ref[(sb - 1) * BLOCK:(sb + 1) * BLOCK, :]
            valid = local
        k_nat, k_swp = band(kc_ref, kp_ref), band(kswc_ref, kswp_ref)
        v_nat, v_swp = band(vc_ref, vp_ref), band(vswc_ref, vswp_ref)

        for g in range(N_KV_HEADS):
            c0 = (g // 2) * LANES
            kcols = (k_nat[:, c0:c0 + LANES], k_swp[:, c0:c0 + LANES])
            vcols = (v_nat[:, c0:c0 + LANES], v_swp[:, c0:c0 + LANES])
            in_lo, in_hi = (0, 1) if g % 2 == 0 else (1, 0)
            q0 = g * GROUP * HEAD_DIM
            qq = jnp.concatenate([q_ref[rows, q0:q0 + LANES],
                                  q_ref[rows, q0 + LANES:q0 + 2 * LANES]], axis=0)
            halves = []
            for half, src in ((0, in_lo), (1, in_hi)):
                qm = jnp.where(lo_lanes, qq, zero) if half == 0 else jnp.where(lo_lanes, zero, qq)
                s = lax.dot_general(qm, kcols[src], (((1,), (1,)), ((), ())),
                                    preferred_element_type=F32)
                s = jnp.where(valid, s, NEG_BIG)
                h_top = g * GROUP + half
                s_prev = jnp.where(diag_top, sinks_ref[h_top],
                                   jnp.where(diag_bot, sinks_ref[h_top + 2], s[:, :BLOCK]))
                s_cur = s[:, BLOCK:]
                m = jnp.max(jnp.maximum(s_prev, s_cur), axis=-1, keepdims=True)
                p_prev = jnp.exp(s_prev - m)
                p_cur = jnp.exp(s_cur - m)
                denom = jnp.sum(p_prev + p_cur, axis=-1, keepdims=True)
                p = jnp.concatenate([jnp.where(diag, 0.0, p_prev), p_cur], axis=1)
                o = jnp.dot(p.astype(BF16), vcols[src], preferred_element_type=F32)
                halves.append(o / denom)
            o_pair = jnp.where(lo_lanes, halves[0], halves[1]).astype(BF16)
            o_ref[rows, q0:q0 + LANES] = o_pair[0:BLOCK, :]
            o_ref[rows, q0 + LANES:q0 + 2 * LANES] = o_pair[BLOCK:, :]


def _outproj_router_kernel(o_ref, y_ref, x_ref, woa_ref, wop_ref, g_ref, wr_ref,
                           br_ref, tri_ref, h_out, hpk_out, route_out, cnt_out):
    tm = x_ref.shape[0]
    h = (x_ref[...]
         + jnp.dot(o_ref[...], woa_ref[...], preferred_element_type=F32)
         + jnp.dot(y_ref[...], wop_ref[...], preferred_element_type=F32))
    h_out[...] = h
    ms = jnp.mean(h * h, axis=-1, keepdims=True)
    hn = h * lax.rsqrt(ms + EPS) * g_ref[...]
    hb = hn.astype(BF16)
    packed = _pack_halves(hn)
    for s in range(PK_ROWS):
        hpk_out[pl.ds(s, tm, stride=PK_ROWS), :] = packed[:, s * LANES:(s + 1) * LANES]

    both = jnp.dot(hb, wr_ref[...], preferred_element_type=F32)
    logits = (both[:, :LANES] + both[:, LANES:] + br_ref[...]).T
    assert EXPERTS_PER_GROUP == SUBLANES and ROUTER_FINE0 % SUBLANES == 0
    slot = lax.broadcasted_iota(jnp.int32, (SUBLANES, tm), 0).astype(F32)

    def col_max(t):
        return jnp.max(t, axis=0, keepdims=True)

    def first_slot_of(mask):
        return jnp.min(jnp.where(mask, slot, float(SUBLANES)), axis=0, keepdims=True)

    coarse = jnp.where(slot < N_EXPERT_GROUPS, logits[0:SUBLANES, :], NEG_BIG)
    cmax = col_max(coarse)
    grp = first_slot_of(coarse == cmax)
    csum = jnp.sum(jnp.exp(coarse - cmax), axis=0, keepdims=True)

    def fine_of(g):
        return logits[ROUTER_FINE0 + g * SUBLANES:ROUTER_FINE0 + (g + 1) * SUBLANES, :]

    fine = fine_of(N_EXPERT_GROUPS - 1)
    for g in range(N_EXPERT_GROUPS - 2, -1, -1):
        fine = jnp.where(grp == float(g), fine_of(g), fine)
    f1 = col_max(fine)
    i1 = first_slot_of(fine == f1)
    rest = jnp.where(slot == i1, NEG_BIG, fine)
    f2 = col_max(rest)
    i2 = first_slot_of(rest == f2)
    ratio = jnp.exp(f2 - f1)
    g1 = (1.0 / csum) / (1.0 + ratio)
    g2 = g1 * ratio
    e1 = grp * EXPERTS_PER_GROUP + i1
    e2 = grp * EXPERTS_PER_GROUP + i2

    expert = lax.broadcasted_iota(jnp.int32, (N_EXPERTS, tm), 0).astype(F32)
    oh1 = expert == e1
    oh2 = expert == e2
    oh = jnp.where(oh1, 1.0, 0.0) + jnp.where(oh2, 1.0, 0.0)
    rank = lax.dot_general(oh.astype(BF16), tri_ref[...], (((1,), (1,)), ((), ())),
                           preferred_element_type=F32)
    cnt = jnp.sum(oh, axis=1, keepdims=True)
    cnt_pad = jnp.floor((cnt + (CHUNK - 1)) * (1.0 / CHUNK)) * CHUNK
    padded_cols = jnp.concatenate(
        [jnp.broadcast_to(cnt_pad, (N_EXPERTS, LANES)),
         jnp.zeros((LANES - N_EXPERTS, LANES), F32)], axis=0).astype(BF16)
    run_start = jnp.dot(tri_ref[0:N_EXPERTS, 0:LANES], padded_cols,
                        preferred_element_type=F32)[:, 0:1]
    table = rank + run_start
    lp1 = jnp.sum(jnp.where(oh1, table, 0.0), axis=0, keepdims=True)
    lp2 = jnp.sum(jnp.where(oh2, table, 0.0), axis=0, keepdims=True)
    route_out[...] = jnp.where(slot == 0.0, g1,
                               jnp.where(slot == 1.0, g2,
                                         jnp.where(slot == 2.0, lp1,
                                                   jnp.where(slot == 3.0, lp2, 0.0))))
    cnt_out[...] = jnp.broadcast_to(cnt, cnt_out.shape)


def _expert_kernel(eid_ref, first_ref, nact_ref, off_ref, next_ref, used_ref,
                   hpk_hbm, wg_hbm, wu_hbm, wd_hbm, ys_out,
                   xbuf, wg_f, wu_f, wd_f, wgu_s, wd_s, sems, wsems):
    i = pl.program_id(0)
    tm = ys_out.shape[0] // PK_ROWS
    n_act = nact_ref[0]

    def weight_copies(e):
        return (pltpu.make_async_copy(wg_hbm.at[e], wg_f, wsems.at[0]),
                pltpu.make_async_copy(wu_hbm.at[e], wu_f, wsems.at[1]),
                pltpu.make_async_copy(wd_hbm.at[e], wd_f, wsems.at[2]))

    ch_rows = CHUNK * PK_ROWS
    n_ch = tm // CHUNK

    def gather(blk, slot):
        for c in range(n_ch):
            src = pl.multiple_of(off_ref[blk * n_ch + c], ch_rows)
            pltpu.make_async_copy(hpk_hbm.at[pl.ds(src, ch_rows)],
                                  xbuf.at[slot, pl.ds(c * ch_rows, ch_rows)],
                                  sems.at[slot]).start()

    @pl.when(i == 0)
    def _():
        for cp in weight_copies(eid_ref[0]):
            cp.start()
        gather(0, 0)

    @pl.when(i + 1 < n_act)
    def _():
        gather(i + 1, (i + 1) % 2)

    @pl.when(i < n_act)
    def _():
        slot = i % 2

        @pl.when(first_ref[i] == 1)
        def _():
            for cp in weight_copies(eid_ref[i]):
                cp.wait()
            wgu_s[:, :D_EXPERT] = wg_f[...].astype(BF16)
            wgu_s[:, D_EXPERT:] = wu_f[...].astype(BF16)
            wd_s[...] = wd_f[...].astype(BF16)

            @pl.when(next_ref[i] >= 0)
            def _():
                for cp in weight_copies(next_ref[i]):
                    cp.start()

        pltpu.make_async_copy(hpk_hbm.at[pl.ds(0, tm * PK_ROWS)], xbuf.at[slot],
                              sems.at[slot]).wait()

        def swiglu(rows):
            xu = jnp.concatenate(
                [xbuf[slot, pl.ds(s, rows, stride=PK_ROWS), :] for s in range(PK_ROWS)],
                axis=1)
            x = jnp.concatenate([v.astype(BF16) for v in _unpack_halves(xu)], axis=1)
            ab = jnp.dot(x, wgu_s[...], preferred_element_type=F32)
            a, b = ab[:, :D_EXPERT], ab[:, D_EXPERT:]
            mid = (a * jax.nn.sigmoid(a) * b).astype(BF16)
            y = jnp.dot(mid, wd_s[...], preferred_element_type=F32)
            yw = _pack_halves(y)
            for s in range(PK_ROWS):
                ys_out[pl.ds(s, rows, stride=PK_ROWS), :] = yw[:, s * LANES:(s + 1) * LANES]
            if rows < tm:
                ys_out[rows * PK_ROWS:, :] = jnp.zeros(((tm - rows) * PK_ROWS, LANES),
                                                       ys_out.dtype)

        quarter = tm // 4
        for rows in range(quarter, tm + 1, quarter):
            @pl.when((used_ref[i] > rows - quarter) & (used_ref[i] <= rows))
            def _(rows=rows):
                swiglu(rows)

    @pl.when(i >= n_act)
    def _():
        ys_out[...] = jnp.zeros(ys_out.shape, ys_out.dtype)


def _local_sort_kernel(lp_ref, hpk_ref, xs_out):
    i = pl.program_id(0)
    t_tile = hpk_ref.shape[0] // PK_ROWS
    n_tok = pl.num_programs(0) * t_tile
    xs_out[...] = jnp.zeros(xs_out.shape, xs_out.dtype)

    def body(q, carry):
        for u in range(SUBLANES):
            t = q * SUBLANES + u
            row = hpk_ref[pl.ds(pl.multiple_of(t * PK_ROWS, PK_ROWS), PK_ROWS), :]
            for k in range(TOP_K):
                dst = pl.multiple_of(lp_ref[k * n_tok + i * t_tile + t], PK_ROWS)
                xs_out[pl.ds(dst, PK_ROWS), :] = row
        return carry
    lax.fori_loop(0, t_tile // SUBLANES, body, 0)


def _combine_kernel(gch_ref, lp_ref, gate_ref, h_ref, ys_hbm, out_ref, ylocal, acc, sems):
    i = pl.program_id(0)
    n = pl.num_programs(0)
    t_tile = h_ref.shape[0]
    n_tok = n * t_tile
    ch_rows = CHUNK * PK_ROWS
    n_ch = ylocal.shape[1] // ch_rows

    def copy_rows(src, slot, dst, rows):
        pltpu.make_async_copy(ys_hbm.at[pl.ds(src, rows)],
                              ylocal.at[slot, pl.ds(dst, rows)],
                              sems.at[slot]).start()

    def chunk_copy(tile, slot, lc):
        src = pl.multiple_of(gch_ref[tile * n_ch + lc], ch_rows)
        copy_rows(src, slot, pl.multiple_of(lc * ch_rows, ch_rows), ch_rows)

    def chunk_pair_copy(tile, slot, lc):
        src0 = pl.multiple_of(gch_ref[tile * n_ch + lc], ch_rows)
        src1 = pl.multiple_of(gch_ref[tile * n_ch + lc + 1], ch_rows)
        dst = pl.multiple_of(lc * ch_rows, ch_rows)
        joined = src1 == src0 + ch_rows

        @pl.when(joined)
        def _():
            copy_rows(src0, slot, dst, 2 * ch_rows)

        @pl.when(jnp.logical_not(joined))
        def _():
            copy_rows(src0, slot, dst, ch_rows)
            copy_rows(src1, slot, dst + ch_rows, ch_rows)

    def wait_slot(slot):
        pltpu.make_async_copy(ys_hbm.at[pl.ds(0, n_ch * ch_rows)], ylocal.at[slot],
                              sems.at[slot]).wait()

    @pl.when(i == 0)
    def _():
        def body(lc, carry):
            chunk_copy(0, 0, lc)
            return carry
        lax.fori_loop(0, n_ch, body, 0)

    slot = i % 2
    wait_slot(slot)

    nxt = jnp.minimum(i + 1, n - 1)
    tok_unroll = 4
    pairs_per_group = 1
    n_groups = t_tile // tok_unroll
    issue_groups = n_ch // (2 * pairs_per_group)
    assert issue_groups * 2 * pairs_per_group == n_ch and issue_groups <= n_groups

    def tok_body(q, carry, issue):
        if issue:
            for u in range(pairs_per_group):
                chunk_pair_copy(nxt, 1 - slot, 2 * (q * pairs_per_group + u))
        for u in range(tok_unroll):
            t = q * tok_unroll + u
            a0 = i * t_tile + t
            a1 = n_tok + a0
            p0 = pl.multiple_of(lp_ref[a0], PK_ROWS)
            p1 = pl.multiple_of(lp_ref[a1], PK_ROWS)
            lo0, hi0 = _unpack_halves(ylocal[slot, pl.ds(p0, PK_ROWS), :])
            lo1, hi1 = _unpack_halves(ylocal[slot, pl.ds(p1, PK_ROWS), :])
            g0 = gate_ref[a0]
            g1 = gate_ref[a1]
            dst = pl.multiple_of(t * PK_ROWS, PK_ROWS)
            acc[0, pl.ds(dst, PK_ROWS), :] = g0 * lo0 + g1 * lo1
            acc[1, pl.ds(dst, PK_ROWS), :] = g0 * hi0 + g1 * hi1
        return carry
    lax.fori_loop(0, issue_groups, functools.partial(tok_body, issue=True), 0)
    lax.fori_loop(issue_groups, n_groups, functools.partial(tok_body, issue=False), 0)

    @pl.when(i == n - 1)
    def _():
        wait_slot(1 - slot)

    for half in range(2):
        for s in range(PK_ROWS):
            c0 = half * HALF + s * LANES
            out_ref[:, c0:c0 + LANES] = (h_ref[:, c0:c0 + LANES]
                                         + acc[half, pl.ds(s, t_tile, stride=PK_ROWS), :])


def _rope_tables(seq):
    pos = jnp.arange(seq, dtype=F32)
    inv_freq = ROPE_THETA ** (-jnp.arange(0, ROT_DIM, 2, dtype=F32) / ROT_DIM)
    ang = pos[:, None] * inv_freq[None, :]
    cos, sin = jnp.cos(ang), jnp.sin(ang)
    half = ROT_DIM // 2
    ones = jnp.ones((seq, HEAD_DIM - ROT_DIM), F32)
    zeros = jnp.zeros((seq, HEAD_DIM - ROT_DIM), F32)
    zh = jnp.zeros((seq, half), F32)
    cc = jnp.concatenate([cos, cos, ones], axis=1)
    s_up = jnp.concatenate([-sin, zh, zeros], axis=1)
    s_dn = jnp.concatenate([zh, sin, zeros], axis=1)
    rep = LANES // HEAD_DIM
    return tuple(jnp.tile(t, (1, rep)) for t in (cc, s_up, s_dn))


def _plan(cnt, n_tokens):
    nt = n_tokens // LOCAL_TILE
    experts = jnp.arange(N_EXPERTS, dtype=jnp.int32)
    cntp = ((cnt + CHUNK - 1) // CHUNK) * CHUNK
    loff_end = jnp.cumsum(cntp, axis=1)
    loff = loff_end - cntp
    rows_e = jnp.sum(cntp, axis=0)
    rows_pad = ((rows_e + EXPERT_TILE - 1) // EXPERT_TILE) * EXPERT_TILE
    g_end = jnp.cumsum(rows_pad)
    g_start = g_end - rows_pad
    gpos = g_start[None, :] + jnp.cumsum(cntp, axis=0) - cntp

    lrow = jnp.arange(LT_MAX // CHUNK, dtype=jnp.int32) * CHUNK
    e_lc = jnp.sum((lrow[None, :, None] >= loff_end[:, None, :]).astype(jnp.int32), axis=-1)
    sel = e_lc[:, :, None] == experts
    delta = jnp.sum(jnp.where(sel, (gpos - loff)[:, None, :], 0), axis=-1)
    gch = jnp.where(e_lc < N_EXPERTS, (delta + lrow[None, :]) // CHUNK, 0).reshape(-1)

    n_rows = TOP_K * n_tokens + nt * N_EXPERTS * (CHUNK - 1) + N_EXPERTS * (EXPERT_TILE - CHUNK)
    n_blocks = -(-n_rows // EXPERT_TILE)
    n_rows = n_blocks * EXPERT_TILE
    blk_start = jnp.arange(n_blocks, dtype=jnp.int32) * EXPERT_TILE
    blk_eid = jnp.minimum(
        jnp.sum((blk_start[:, None] >= g_end[None, :]).astype(jnp.int32), axis=-1),
        N_EXPERTS - 1)
    prev = jnp.concatenate([jnp.full((1,), -1, jnp.int32), blk_eid[:-1]])
    first = (blk_eid != prev).astype(jnp.int32)
    n_act = (g_end[-1] // EXPERT_TILE).astype(jnp.int32).reshape(1)
    later = (blk_eid[None, :] > blk_eid[:, None]) & (jnp.arange(n_blocks)[None, :] < n_act)
    next_eid = jnp.min(jnp.where(later, blk_eid[None, :], N_EXPERTS), axis=1)
    next_eid = jnp.where(next_eid < N_EXPERTS, next_eid, -1).astype(jnp.int32)
    of_blk = blk_eid[:, None] == experts
    blk_end = jnp.sum(jnp.where(of_blk, (g_start + rows_e)[None, :], 0), axis=1)
    used = jnp.clip(blk_end - blk_start, 0, EXPERT_TILE).astype(jnp.int32)

    cpb = EXPERT_TILE // CHUNK
    run_start = gpos.T // CHUNK
    run_len = cntp.T // CHUNK
    run_src = (jnp.arange(nt, dtype=jnp.int32)[:, None] * (LT_MAX // CHUNK) + loff // CHUNK).T
    tabs = jnp.stack([run_start, run_len, run_src], axis=0)
    dtabs = tabs - jnp.concatenate(
        [jnp.zeros((3, N_EXPERTS, 1), jnp.int32), tabs[:, :, :-1]], axis=2)
    blk_sel = (blk_eid[:, None] == experts)[:, :, None]
    blk_start_t = jnp.sum(jnp.where(blk_sel, run_start[None], 0), axis=1)
    blk_dtabs = jnp.sum(jnp.where(blk_sel[None], dtabs[:, None], 0), axis=2)
    gc = jnp.arange(n_rows // CHUNK, dtype=jnp.int32).reshape(n_blocks, cpb)
    started = blk_start_t[:, None, :] <= gc[:, :, None]
    picked = jnp.sum(jnp.where(started[None], blk_dtabs[:, :, None, :], 0), axis=-1)
    within = gc - picked[0]
    csrc = jnp.where(within < picked[1], picked[2] + within, 0).reshape(-1)
    return (gch.astype(jnp.int32), csrc.astype(jnp.int32), blk_eid, first, next_eid, used,
            n_act, n_rows)


def kernel(x, norm_mix, w_in, q_norm, k_norm, sinks, w_pool, pool_scale, w_out,
           norm_ffn, w_coarse, b_coarse, w_fine, b_fine, w_gate, w_up, w_down):
    bsz, seq, d = x.shape
    n = bsz * seq
    assert d == D_MODEL and seq % ROW_TILE == 0 and seq % BLOCK == 0
    assert norm_mix.shape[0] == 1, "single-layer problem"
    xf = x.reshape(n, d)

    w_in_b = w_in[0].astype(BF16)
    wq = w_in_b[:, :ATTN_WIDTH]
    wk = w_in_b[:, ATTN_WIDTH:ATTN_WIDTH + KV_WIDTH]
    wv = w_in_b[:, ATTN_WIDTH + KV_WIDTH:ATTN_WIDTH + 2 * KV_WIDTH]
    wu = w_in_b[:, ATTN_WIDTH + 2 * KV_WIDTH:]
    qn = jnp.tile(q_norm[0], N_HEADS).reshape(1, ATTN_WIDTH)
    kn = jnp.tile(k_norm[0], N_KV_HEADS).reshape(1, KV_WIDTH)
    cc, s_up, s_dn = _rope_tables(seq)
    lane_head = jnp.arange(MXU_DIM) // HEAD_DIM
    e_mat = (lane_head[:, None] == lane_head[None, :]).astype(BF16)
    w_pool_b = w_pool[0].astype(BF16)
    pscale = pool_scale[0].reshape(1, POOL_WIDTH)
    w_out_b = w_out[0].astype(BF16)
    wo_attn, wo_pool = w_out_b[:ATTN_WIDTH], w_out_b[ATTN_WIDTH:]
    gap = ROUTER_FINE0 - N_EXPERT_GROUPS
    tail = LANES - ROUTER_FINE0 - N_EXPERTS
    w_r = jnp.concatenate([w_coarse[0], jnp.zeros((d, gap), F32), w_fine[0],
                           jnp.zeros((d, tail), F32)], axis=1)
    w_r_hi = w_r.astype(BF16)
    w_r_lo = (w_r - w_r_hi.astype(F32)).astype(BF16)
    w_r2 = jnp.concatenate([w_r_hi, w_r_lo], axis=1)
    b_r = jnp.concatenate([b_coarse[0], jnp.zeros((gap,), F32), b_fine[0],
                           jnp.zeros((tail,), F32)]).reshape(1, LANES)

    tm = ROW_TILE
    tiles_per_seq = seq // tm
    n_tiles = n // tm
    idx = jnp.arange(SUB_TILE)
    tri = (idx[None, :] < idx[:, None]).astype(BF16)
    const = lambda *_: (0, 0)
    row_blk = lambda i: (i, 0)

    q, k, v, ksw, vsw, y = pl.pallas_call(
        functools.partial(_inproj_kernel, tiles_per_seq=tiles_per_seq),
        grid=(n_tiles,),
        in_specs=[
            pl.BlockSpec((tm, d), row_blk),
            pl.BlockSpec((1, d), const),
            pl.BlockSpec((d, ATTN_WIDTH), const),
            pl.BlockSpec((d, KV_WIDTH), const),
            pl.BlockSpec((d, KV_WIDTH), const),
            pl.BlockSpec((d, POOL_WIDTH), const),
            pl.BlockSpec((1, ATTN_WIDTH), const),
            pl.BlockSpec((1, KV_WIDTH), const),
            pl.BlockSpec((tm, LANES), lambda i: (i % tiles_per_seq, 0)),
            pl.BlockSpec((tm, LANES), lambda i: (i % tiles_per_seq, 0)),
            pl.BlockSpec((tm, LANES), lambda i: (i % tiles_per_seq, 0)),
            pl.BlockSpec((MXU_DIM, MXU_DIM), const),
            pl.BlockSpec((len(POOL_WINDOWS), POOL_GROUP_DIM, POOL_GROUP_DIM),
                         lambda i: (0, 0, 0)),
            pl.BlockSpec((1, POOL_WIDTH), const),
        ],
        out_specs=[
            pl.BlockSpec((tm, ATTN_WIDTH), row_blk),
            pl.BlockSpec((tm, KV_WIDTH), row_blk),
            pl.BlockSpec((tm, KV_WIDTH), row_blk),
            pl.BlockSpec((tm, KV_WIDTH), row_blk),
            pl.BlockSpec((tm, KV_WIDTH), row_blk),
            pl.BlockSpec((tm, POOL_WIDTH), row_blk),
        ],
        out_shape=[
            jax.ShapeDtypeStruct((n, ATTN_WIDTH), BF16),
            jax.ShapeDtypeStruct((n, KV_WIDTH), BF16),
            jax.ShapeDtypeStruct((n, KV_WIDTH), BF16),
            jax.ShapeDtypeStruct((n, KV_WIDTH), BF16),
            jax.ShapeDtypeStruct((n, KV_WIDTH), BF16),
            jax.ShapeDtypeStruct((n, POOL_WIDTH), BF16),
        ],
        scratch_shapes=[pltpu.VMEM((HALO + tm, POOL_WIDTH), F32)],
        compiler_params=_cparams(),
        name="inproj",
    )(xf, norm_mix[0].reshape(1, d), wq, wk, wv, wu, qn, kn, cc, s_up, s_dn,
      e_mat, w_pool_b, pscale)

    nb = seq // BLOCK
    nq = nb // Q_BLOCKS
    qrows = Q_BLOCKS * BLOCK
    cur = lambda b, j, *_: (b * nq + j, 0)
    prv = lambda b, j, *_: (b * nb + jnp.maximum(Q_BLOCKS * j - 1, 0), 0)
    kv_specs = [pl.BlockSpec((qrows, KV_WIDTH), cur), pl.BlockSpec((BLOCK, KV_WIDTH), prv)]
    attn = pl.pallas_call(
        _attn_kernel,
        grid_spec=pltpu.PrefetchScalarGridSpec(
            num_scalar_prefetch=1,
            grid=(bsz, nq),
            in_specs=[pl.BlockSpec((qrows, ATTN_WIDTH), cur)] + kv_specs * 4,
            out_specs=pl.BlockSpec((qrows, ATTN_WIDTH), cur),
        ),
        out_shape=jax.ShapeDtypeStruct((n, ATTN_WIDTH), BF16),
        compiler_params=_cparams(2),
        name="swa_attn",
    )(sinks[0].astype(F32), q, k, k, ksw, ksw, v, v, vsw, vsw)

    h, hpk, route, cnt = pl.pallas_call(
        _outproj_router_kernel,
        grid=(n_tiles,),
        in_specs=[
            pl.BlockSpec((tm, ATTN_WIDTH), row_blk),
            pl.BlockSpec((tm, POOL_WIDTH), row_blk),
            pl.BlockSpec((tm, d), row_blk),
            pl.BlockSpec((ATTN_WIDTH, d), const),
            pl.BlockSpec((POOL_WIDTH, d), const),
            pl.BlockSpec((1, d), const),
            pl.BlockSpec((d, 2 * LANES), const),
            pl.BlockSpec((1, LANES), const),
            pl.BlockSpec((SUB_TILE, SUB_TILE), const),
        ],
        out_specs=[
            pl.BlockSpec((tm, d), row_blk),
            pl.BlockSpec((tm * PK_ROWS, LANES), row_blk),
            pl.BlockSpec((SUBLANES, tm), lambda i: (0, i)),
            pl.BlockSpec((N_EXPERTS, LANES), row_blk),
        ],
        out_shape=[
            jax.ShapeDtypeStruct((n, d), F32),
            jax.ShapeDtypeStruct((n * PK_ROWS, LANES), jnp.int32),
            jax.ShapeDtypeStruct((SUBLANES, n), F32),
            jax.ShapeDtypeStruct((n_tiles * N_EXPERTS, LANES), F32),
        ],
        compiler_params=_cparams(),
        name="outproj_router",
    )(attn, y, xf, wo_attn, wo_pool, norm_ffn[0].reshape(1, d), w_r2, b_r, tri)

    assert tm == LOCAL_TILE
    gates = route[0:TOP_K].reshape(TOP_K * n)
    lp = route[TOP_K:2 * TOP_K].astype(jnp.int32).reshape(TOP_K * n)
    tile_cnt = cnt.reshape(n_tiles, N_EXPERTS, LANES)[:, :, 0].astype(jnp.int32)
    te = EXPERT_TILE
    gch, csrc, blk_eid, first, next_eid, used, n_act, n_rows = _plan(tile_cnt, n)
    n_blocks = n_rows // te
    n_local = n // LOCAL_TILE

    xs = pl.pallas_call(
        _local_sort_kernel,
        grid_spec=pltpu.PrefetchScalarGridSpec(
            num_scalar_prefetch=1,
            grid=(n_local,),
            in_specs=[pl.BlockSpec((LOCAL_TILE * PK_ROWS, LANES), lambda i, *_: (i, 0))],
            out_specs=pl.BlockSpec((LT_MAX * PK_ROWS, LANES), lambda i, *_: (i, 0)),
        ),
        out_shape=jax.ShapeDtypeStruct((n_local * LT_MAX * PK_ROWS, LANES), jnp.int32),
        compiler_params=_cparams(),
        name="local_sort",
    )(lp * PK_ROWS, hpk)

    ys = pl.pallas_call(
        _expert_kernel,
        grid_spec=pltpu.PrefetchScalarGridSpec(
            num_scalar_prefetch=6,
            grid=(n_blocks,),
            in_specs=[pl.BlockSpec(memory_space=pl.ANY)] * 4,
            out_specs=pl.BlockSpec((te * PK_ROWS, LANES), lambda i, *_: (i, 0)),
            scratch_shapes=[
                pltpu.VMEM((2, te * PK_ROWS, LANES), jnp.int32),
                pltpu.VMEM((d, D_EXPERT), F32),
                pltpu.VMEM((d, D_EXPERT), F32),
                pltpu.VMEM((D_EXPERT, d), F32),
                pltpu.VMEM((d, 2 * D_EXPERT), BF16),
                pltpu.VMEM((D_EXPERT, d), BF16),
                pltpu.SemaphoreType.DMA((2,)),
                pltpu.SemaphoreType.DMA((3,)),
            ],
        ),
        out_shape=jax.ShapeDtypeStruct((n_rows * PK_ROWS, LANES), jnp.int32),
        compiler_params=_cparams(),
        name="experts",
    )(blk_eid, first, n_act, csrc * (CHUNK * PK_ROWS), next_eid, used, xs,
      w_gate[0], w_up[0], w_down[0])

    tt = LOCAL_TILE
    out = pl.pallas_call(
        _combine_kernel,
        grid_spec=pltpu.PrefetchScalarGridSpec(
            num_scalar_prefetch=3,
            grid=(n_local,),
            in_specs=[
                pl.BlockSpec((tt, d), lambda i, *_: (i, 0)),
                pl.BlockSpec(memory_space=pl.ANY),
            ],
            out_specs=pl.BlockSpec((tt, d), lambda i, *_: (i, 0)),
            scratch_shapes=[
                pltpu.VMEM((2, LT_MAX * PK_ROWS, LANES), jnp.int32),
                pltpu.VMEM((2, tt * PK_ROWS, LANES), F32),
                pltpu.SemaphoreType.DMA((2,)),
            ],
        ),
        out_shape=jax.ShapeDtypeStruct((n, d), F32),
        compiler_params=_cparams(),
        name="combine",
    )(gch * (CHUNK * PK_ROWS), lp * PK_ROWS, gates, h, ys)
    return out.reshape(bsz, seq, d)
```

```python
import functools

import jax
import jax.numpy as jnp
from jax import lax
from jax.experimental import pallas as pl
from jax.experimental.pallas import tpu as pltpu

F32 = jnp.float32
BF16 = jnp.bfloat16

D_MODEL = 2048
N_HEADS = 16
N_KV_HEADS = 4
HEAD_DIM = 64
GROUP = N_HEADS // N_KV_HEADS
ROT_DIM = HEAD_DIM // 4
ROPE_THETA = 500000.0
WINDOW = 128
BLOCK = 128
Q_BLOCKS = 16
ATTN_WIDTH = N_HEADS * HEAD_DIM
KV_WIDTH = N_KV_HEADS * HEAD_DIM
POOL_WINDOWS = (2, 4, 8, 16)
POOL_WIDTH = D_MODEL // 2
POOL_GROUP_DIM = POOL_WIDTH // len(POOL_WINDOWS)
N_EXPERT_GROUPS = 4
EXPERTS_PER_GROUP = 8
N_EXPERTS = N_EXPERT_GROUPS * EXPERTS_PER_GROUP
TOP_K = 2
D_EXPERT = 512
EPS = 1e-6

LANES = 128
SUBLANES = 8
MXU_DIM = 256
HALO = 16
NEG_BIG = -1e30
ROUTER_FINE0 = 8
HALF = D_MODEL // 2
PK_ROWS = HALF // LANES

ROW_TILE = 512
SUB_TILE = 512
EXPERT_TILE = 512
LOCAL_TILE = 512
CHUNK = 8
LT_MAX = TOP_K * LOCAL_TILE + N_EXPERTS * (CHUNK - 1)
VMEM_LIMIT = 56 * 1024 * 1024


def _cparams(n_axes=1):
    return pltpu.CompilerParams(
        dimension_semantics=("arbitrary",) * n_axes,
        vmem_limit_bytes=VMEM_LIMIT,
    )


def _pack_halves(t):
    return pltpu.pack_elementwise([t[:, :HALF], t[:, HALF:]], packed_dtype=BF16)


def _unpack_halves(w):
    return tuple(pltpu.unpack_elementwise(w, index=i, packed_dtype=BF16, unpacked_dtype=F32)
                 for i in range(2))


def _head_sumsq(ts, e_ref):
    rows = ts[0].shape[0]
    chunks = []
    for t in ts:
        t2 = (t * t).astype(BF16)
        chunks += [t2[:, c * MXU_DIM:(c + 1) * MXU_DIM] for c in range(t.shape[1] // MXU_DIM)]
    sums = jnp.dot(jnp.concatenate(chunks, axis=0), e_ref[...], preferred_element_type=F32)
    outs, first = [], 0
    for t in ts:
        n = t.shape[1] // MXU_DIM
        parts = [sums[(first + c) * rows:(first + c + 1) * rows, :] for c in range(n)]
        outs.append(parts[0] if n == 1 else jnp.concatenate(parts, axis=1))
        first += n
    return outs


def _norm_rope(t, ss, gain, cc, s_up, s_dn, out_scale):
    w = t.shape[1]
    tn = t * lax.rsqrt(ss * (1.0 / HEAD_DIM) + EPS) * gain
    reps = w // LANES
    up = pltpu.roll(tn, w - ROT_DIM // 2, axis=1)
    dn = pltpu.roll(tn, ROT_DIM // 2, axis=1)
    c = jnp.concatenate([cc] * reps, axis=1)
    su = jnp.concatenate([s_up] * reps, axis=1)
    sd = jnp.concatenate([s_dn] * reps, axis=1)
    return (tn * c + up * su + dn * sd) * out_scale


def _swap_head_pairs(t):
    w = t.shape[1]
    lane = lax.broadcasted_iota(jnp.int32, t.shape, 1)
    from_up = pltpu.roll(t, w - HEAD_DIM, axis=1)
    from_dn = pltpu.roll(t, HEAD_DIM, axis=1)
    return jnp.where((lane & (LANES - 1)) < HEAD_DIM, from_up, from_dn)


def _inproj_kernel(x_ref, g_ref, wq_ref, wk_ref, wv_ref, wu_ref, qn_ref, kn_ref,
                   cc_ref, su_ref, sd_ref, e_ref, wpool_ref, pscale_ref,
                   q_out, k_out, v_out, ksw_out, vsw_out, y_out, ubuf, *, tiles_per_seq):
    i = pl.program_id(0)
    tm = x_ref.shape[0]

    @pl.when(i % tiles_per_seq == 0)
    def _():
        ubuf[0:HALO, :] = jnp.zeros((HALO, POOL_WIDTH), F32)

    for r0 in range(0, tm, SUB_TILE):
        rows = slice(r0, r0 + SUB_TILE)
        x = x_ref[rows, :]
        ms = jnp.mean(x * x, axis=-1, keepdims=True)
        hn = (x * lax.rsqrt(ms + EPS) * g_ref[...]).astype(BF16)

        cc, s_up, s_dn = cc_ref[rows, :], su_ref[rows, :], sd_ref[rows, :]
        q = jnp.dot(hn, wq_ref[...], preferred_element_type=F32)
        k = jnp.dot(hn, wk_ref[...], preferred_element_type=F32)
        ss_q, ss_k = _head_sumsq([q, k], e_ref)
        q_out[rows, :] = _norm_rope(q, ss_q, qn_ref[...], cc, s_up, s_dn,
                                    HEAD_DIM ** -0.5).astype(BF16)
        k = _norm_rope(k, ss_k, kn_ref[...], cc, s_up, s_dn, 1.0)
        v = jnp.dot(hn, wv_ref[...], preferred_element_type=F32)
        k_out[rows, :] = k.astype(BF16)
        v_out[rows, :] = v.astype(BF16)
        ksw_out[rows, :] = _swap_head_pairs(k).astype(BF16)
        vsw_out[rows, :] = _swap_head_pairs(v).astype(BF16)

        base = HALO + r0
        ubuf[base:base + SUB_TILE, :] = jnp.dot(hn, wu_ref[...], preferred_element_type=F32)
        pos = (i % tiles_per_seq) * tm + r0 + lax.broadcasted_iota(
            jnp.int32, (SUB_TILE, POOL_GROUP_DIM), 0)
        for g, w in enumerate(POOL_WINDOWS):
            cols = slice(g * POOL_GROUP_DIM, (g + 1) * POOL_GROUP_DIM)
            assert w & (w - 1) == 0 and w - 1 < HALO
            acc = ubuf[base - HALO:base + SUB_TILE, cols]
            shift = 1
            while shift < w:
                acc = acc + pltpu.roll(acc, shift, axis=0)
                shift *= 2
            acc = acc[HALO:, :]
            u_g = ubuf[base:base + SUB_TILE, cols]
            cnt = jnp.minimum(pos + 1, w).astype(F32)
            d = (acc / cnt - u_g).astype(BF16)
            y = jnp.dot(d, wpool_ref[g], preferred_element_type=F32)
            y_out[rows, cols] = (y * pscale_ref[:, cols]).astype(BF16)
    ubuf[0:HALO, :] = ubuf[tm:tm + HALO, :]


def _attn_kernel(sinks_ref, q_ref, kc_ref, kp_ref, kswc_ref, kswp_ref,
                 vc_ref, vp_ref, vswc_ref, vswp_ref, o_ref):
    j = pl.program_id(1)
    two = 2 * BLOCK
    row = lax.broadcasted_iota(jnp.int32, (two, two), 0) & (BLOCK - 1)
    col = lax.broadcasted_iota(jnp.int32, (two, two), 1)
    diff = row + BLOCK - col
    local = (diff >= 0) & (diff < WINDOW)
    lo_lanes = lax.broadcasted_iota(jnp.int32, (two, LANES), 1) < HEAD_DIM
    row_p = lax.broadcasted_iota(jnp.int32, (two, BLOCK), 0)
    diag = (row_p & (BLOCK - 1)) == lax.broadcasted_iota(jnp.int32, (two, BLOCK), 1)
    top_rows = row_p < BLOCK
    diag_top = diag & top_rows
    diag_bot = diag & jnp.logical_not(top_rows)
    zero = jnp.zeros((two, LANES), BF16)

    for sb in range(q_ref.shape[0] // BLOCK):
        rows = slice(sb * BLOCK, (sb + 1) * BLOCK)
        if sb == 0:
            def band(cur_ref, prev_ref):
                return jnp.concatenate([prev_ref[...], cur_ref[0:BLOCK, :]], axis=0)
            valid = local & ((col >= BLOCK) | (j > 0))
        else:
            def band(cur_ref, prev_ref, sb=sb):
                return cur_ref[(sb - 1) * BLOCK:(sb + 1) * BLOCK, :]
            valid = local
        k_nat, k_swp = band(kc_ref, kp_ref), band(kswc_ref, kswp_ref)
        v_nat, v_swp = band(vc_ref, vp_ref), band(vswc_ref, vswp_ref)

        for g in range(N_KV_HEADS):
            c0 = (g // 2) * LANES
            kcols = (k_nat[:, c0:c0 + LANES], k_swp[:, c0:c0 + LANES])
            vcols = (v_nat[:, c0:c0 + LANES], v_swp[:, c0:c0 + LANES])
            in_lo, in_hi = (0, 1) if g % 2 == 0 else (1, 0)
            q0 = g * GROUP * HEAD_DIM
            qq = jnp.concatenate([q_ref[rows, q0:q0 + LANES],
                                  q_ref[rows, q0 + LANES:q0 + 2 * LANES]], axis=0)
            halves = []
            for half, src in ((0, in_lo), (1, in_hi)):
                qm = jnp.where(lo_lanes, qq, zero) if half == 0 else jnp.where(lo_lanes, zero, qq)
                s = lax.dot_general(qm, kcols[src], (((1,), (1,)), ((), ())),
                                    preferred_element_type=F32)
                s = jnp.where(valid, s, NEG_BIG)
                h_top = g * GROUP + half
                s_prev = jnp.where(diag_top, sinks_ref[h_top],
                                   jnp.where(diag_bot, sinks_ref[h_top + 2], s[:, :BLOCK]))
                s_cur = s[:, BLOCK:]
                m = jnp.max(jnp.maximum(s_prev, s_cur), axis=-1, keepdims=True)
                p_prev = jnp.exp(s_prev - m)
                p_cur = jnp.exp(s_cur - m)
                denom = jnp.sum(p_prev + p_cur, axis=-1, keepdims=True)
                p = jnp.concatenate([jnp.where(diag, 0.0, p_prev), p_cur], axis=1)
                o = jnp.dot(p.astype(BF16), vcols[src], preferred_element_type=F32)
                halves.append(o / denom)
            o_pair = jnp.where(lo_lanes, halves[0], halves[1]).astype(BF16)
            o_ref[rows, q0:q0 + LANES] = o_pair[0:BLOCK, :]
            o_ref[rows, q0 + LANES:q0 + 2 * LANES] = o_pair[BLOCK:, :]


def _outproj_router_kernel(o_ref, y_ref, x_ref, woa_ref, wop_ref, g_ref, wr_ref,
                           br_ref, tri_ref, h_out, hpk_out, route_out, cnt_out):
    tm = x_ref.shape[0]
    h = (x_ref[...]
         + jnp.dot(o_ref[...], woa_ref[...], preferred_element_type=F32)
         + jnp.dot(y_ref[...], wop_ref[...], preferred_element_type=F32))
    h_out[...] = h
    ms = jnp.mean(h * h, axis=-1, keepdims=True)
    hn = h * lax.rsqrt(ms + EPS) * g_ref[...]
    hb = hn.astype(BF16)
    packed = _pack_halves(hn)
    for s in range(PK_ROWS):
        hpk_out[pl.ds(s, tm, stride=PK_ROWS), :] = packed[:, s * LANES:(s + 1) * LANES]

    both = jnp.dot(hb, wr_ref[...], preferred_element_type=F32)
    logits = (both[:, :LANES] + both[:, LANES:] + br_ref[...]).T
    assert EXPERTS_PER_GROUP == SUBLANES and ROUTER_FINE0 % SUBLANES == 0
    slot = lax.broadcasted_iota(jnp.int32, (SUBLANES, tm), 0).astype(F32)

    def col_max(t):
        return jnp.max(t, axis=0, keepdims=True)

    def first_slot_of(mask):
        return jnp.min(jnp.where(mask, slot, float(SUBLANES)), axis=0, keepdims=True)

    coarse = jnp.where(slot < N_EXPERT_GROUPS, logits[0:SUBLANES, :], NEG_BIG)
    cmax = col_max(coarse)
    grp = first_slot_of(coarse == cmax)
    csum = jnp.sum(jnp.exp(coarse - cmax), axis=0, keepdims=True)

    def fine_of(g):
        return logits[ROUTER_FINE0 + g * SUBLANES:ROUTER_FINE0 + (g + 1) * SUBLANES, :]

    fine = fine_of(N_EXPERT_GROUPS - 1)
    for g in range(N_EXPERT_GROUPS - 2, -1, -1):
        fine = jnp.where(grp == float(g), fine_of(g), fine)
    f1 = col_max(fine)
    i1 = first_slot_of(fine == f1)
    rest = jnp.where(slot == i1, NEG_BIG, fine)
    f2 = col_max(rest)
    i2 = first_slot_of(rest == f2)
    ratio = jnp.exp(f2 - f1)
    g1 = (1.0 / csum) / (1.0 + ratio)
    g2 = g1 * ratio
    e1 = grp * EXPERTS_PER_GROUP + i1
    e2 = grp * EXPERTS_PER_GROUP + i2

    expert = lax.broadcasted_iota(jnp.int32, (N_EXPERTS, tm), 0).astype(F32)
    oh1 = expert == e1
    oh2 = expert == e2
    oh = jnp.where(oh1, 1.0, 0.0) + jnp.where(oh2, 1.0, 0.0)
    rank = lax.dot_general(oh.astype(BF16), tri_ref[...], (((1,), (1,)), ((), ())),
                           preferred_element_type=F32)
    cnt = jnp.sum(oh, axis=1, keepdims=True)
    cnt_pad = jnp.floor((cnt + (CHUNK - 1)) * (1.0 / CHUNK)) * CHUNK
    padded_cols = jnp.concatenate(
        [jnp.broadcast_to(cnt_pad, (N_EXPERTS, LANES)),
         jnp.zeros((LANES - N_EXPERTS, LANES), F32)], axis=0).astype(BF16)
    run_start = jnp.dot(tri_ref[0:N_EXPERTS, 0:LANES], padded_cols,
                        preferred_element_type=F32)[:, 0:1]
    table = rank + run_start
    lp1 = jnp.sum(jnp.where(oh1, table, 0.0), axis=0, keepdims=True)
    lp2 = jnp.sum(jnp.where(oh2, table, 0.0), axis=0, keepdims=True)
    route_out[...] = jnp.where(slot == 0.0, g1,
                               jnp.where(slot == 1.0, g2,
                                         jnp.where(slot == 2.0, lp1,
                                                   jnp.where(slot == 3.0, lp2, 0.0))))
    cnt_out[...] = jnp.broadcast_to(cnt, cnt_out.shape)


def _expert_kernel(eid_ref, first_ref, nact_ref, off_ref, next_ref, used_ref,
                   hpk_hbm, wg_hbm, wu_hbm, wd_hbm, ys_out,
                   xbuf, wg_f, wu_f, wd_f, wgu_s, wd_s, sems, wsems):
    i = pl.program_id(0)
    tm = ys_out.shape[0] // PK_ROWS
    n_act = nact_ref[0]

    def weight_copies(e):
        return (pltpu.make_async_copy(wg_hbm.at[e], wg_f, wsems.at[0]),
                pltpu.make_async_copy(wu_hbm.at[e], wu_f, wsems.at[1]),
                pltpu.make_async_copy(wd_hbm.at[e], wd_f, wsems.at[2]))

    ch_rows = CHUNK * PK_ROWS
    n_ch = tm // CHUNK

    def gather(blk, slot):
        for c in range(n_ch):
            src = pl.multiple_of(off_ref[blk * n_ch + c], ch_rows)
            pltpu.make_async_copy(hpk_hbm.at[pl.ds(src, ch_rows)],
                                  xbuf.at[slot, pl.ds(c * ch_rows, ch_rows)],
                                  sems.at[slot]).start()

    @pl.when(i == 0)
    def _():
        for cp in weight_copies(eid_ref[0]):
            cp.start()
        gather(0, 0)

    @pl.when(i + 1 < n_act)
    def _():
        gather(i + 1, (i + 1) % 2)

    @pl.when(i < n_act)
    def _():
        slot = i % 2

        @pl.when(first_ref[i] == 1)
        def _():
            for cp in weight_copies(eid_ref[i]):
                cp.wait()
            wgu_s[:, :D_EXPERT] = wg_f[...].astype(BF16)
            wgu_s[:, D_EXPERT:] = wu_f[...].astype(BF16)
            wd_s[...] = wd_f[...].astype(BF16)

            @pl.when(next_ref[i] >= 0)
            def _():
                for cp in weight_copies(next_ref[i]):
                    cp.start()

        pltpu.make_async_copy(hpk_hbm.at[pl.ds(0, tm * PK_ROWS)], xbuf.at[slot],
                              sems.at[slot]).wait()

        def swiglu(rows):
            xu = jnp.concatenate(
                [xbuf[slot, pl.ds(s, rows, stride=PK_ROWS), :] for s in range(PK_ROWS)],
                axis=1)
            x = jnp.concatenate([v.astype(BF16) for v in _unpack_halves(xu)], axis=1)
            ab = jnp.dot(x, wgu_s[...], preferred_element_type=F32)
            a, b = ab[:, :D_EXPERT], ab[:, D_EXPERT:]
            mid = (a * jax.nn.sigmoid(a) * b).astype(BF16)
            y = jnp.dot(mid, wd_s[...], preferred_element_type=F32)
            yw = _pack_halves(y)
            for s in range(PK_ROWS):
                ys_out[pl.ds(s, rows, stride=PK_ROWS), :] = yw[:, s * LANES:(s + 1) * LANES]
            if rows < tm:
                ys_out[rows * PK_ROWS:, :] = jnp.zeros(((tm - rows) * PK_ROWS, LANES),
                                                       ys_out.dtype)

        quarter = tm // 4
        for rows in range(quarter, tm + 1, quarter):
            @pl.when((used_ref[i] > rows - quarter) & (used_ref[i] <= rows))
            def _(rows=rows):
                swiglu(rows)

    @pl.when(i >= n_act)
    def _():
        ys_out[...] = jnp.zeros(ys_out.shape, ys_out.dtype)


def _local_sort_kernel(lp_ref, hpk_ref, xs_out):
    i = pl.program_id(0)
    t_tile = hpk_ref.shape[0] // PK_ROWS
    n_tok = pl.num_programs(0) * t_tile
    xs_out[...] = jnp.zeros(xs_out.shape, xs_out.dtype)

    def body(q, carry):
        for u in range(SUBLANES):
            t = q * SUBLANES + u
            row = hpk_ref[pl.ds(pl.multiple_of(t * PK_ROWS, PK_ROWS), PK_ROWS), :]
            for k in range(TOP_K):
                dst = pl.multiple_of(lp_ref[k * n_tok + i * t_tile + t], PK_ROWS)
                xs_out[pl.ds(dst, PK_ROWS), :] = row
        return carry
    lax.fori_loop(0, t_tile // SUBLANES, body, 0)


def _combine_kernel(gch_ref, lp_ref, gate_ref, h_ref, ys_hbm, out_ref, ylocal, acc, sems):
    i = pl.program_id(0)
    n = pl.num_programs(0)
    t_tile = h_ref.shape[0]
    n_tok = n * t_tile
    ch_rows = CHUNK * PK_ROWS
    n_ch = ylocal.shape[1] // ch_rows

    def copy_rows(src, slot, dst, rows):
        pltpu.make_async_copy(ys_hbm.at[pl.ds(src, rows)],
                              ylocal.at[slot, pl.ds(dst, rows)],
                              sems.at[slot]).start()

    def chunk_copy(tile, slot, lc):
        src = pl.multiple_of(gch_ref[tile * n_ch + lc], ch_rows)
        copy_rows(src, slot, pl.multiple_of(lc * ch_rows, ch_rows), ch_rows)

    def chunk_pair_copy(tile, slot, lc):
        src0 = pl.multiple_of(gch_ref[tile * n_ch + lc], ch_rows)
        src1 = pl.multiple_of(gch_ref[tile * n_ch + lc + 1], ch_rows)
        dst = pl.multiple_of(lc * ch_rows, ch_rows)
        joined = src1 == src0 + ch_rows

        @pl.when(joined)
        def _():
            copy_rows(src0, slot, dst, 2 * ch_rows)

        @pl.when(jnp.logical_not(joined))
        def _():
            copy_rows(src0, slot, dst, ch_rows)
            copy_rows(src1, slot, dst + ch_rows, ch_rows)

    def wait_slot(slot):
        pltpu.make_async_copy(ys_hbm.at[pl.ds(0, n_ch * ch_rows)], ylocal.at[slot],
                              sems.at[slot]).wait()

    @pl.when(i == 0)
    def _():
        def body(lc, carry):
            chunk_copy(0, 0, lc)
            return carry
        lax.fori_loop(0, n_ch, body, 0)

    slot = i % 2
    wait_slot(slot)

    nxt = jnp.minimum(i + 1, n - 1)
    tok_unroll = 4
    pairs_per_group = 1
    n_groups = t_tile // tok_unroll
    issue_groups = n_ch // (2 * pairs_per_group)
    assert issue_groups * 2 * pairs_per_group == n_ch and issue_groups <= n_groups

    def tok_body(q, carry, issue):
        if issue:
            for u in range(pairs_per_group):
                chunk_pair_copy(nxt, 1 - slot, 2 * (q * pairs_per_group + u))
        for u in range(tok_unroll):
            t = q * tok_unroll + u
            a0 = i * t_tile + t
            a1 = n_tok + a0
            p0 = pl.multiple_of(lp_ref[a0], PK_ROWS)
            p1 = pl.multiple_of(lp_ref[a1], PK_ROWS)
            lo0, hi0 = _unpack_halves(ylocal[slot, pl.ds(p0, PK_ROWS), :])
            lo1, hi1 = _unpack_halves(ylocal[slot, pl.ds(p1, PK_ROWS), :])
            g0 = gate_ref[a0]
            g1 = gate_ref[a1]
            dst = pl.multiple_of(t * PK_ROWS, PK_ROWS)
            acc[0, pl.ds(dst, PK_ROWS), :] = g0 * lo0 + g1 * lo1
            acc[1, pl.ds(dst, PK_ROWS), :] = g0 * hi0 + g1 * hi1
        return carry
    lax.fori_loop(0, issue_groups, functools.partial(tok_body, issue=True), 0)
    lax.fori_loop(issue_groups, n_groups, functools.partial(tok_body, issue=False), 0)

    @pl.when(i == n - 1)
    def _():
        wait_slot(1 - slot)

    for half in range(2):
        for s in range(PK_ROWS):
            c0 = half * HALF + s * LANES
            out_ref[:, c0:c0 + LANES] = (h_ref[:, c0:c0 + LANES]
                                         + acc[half, pl.ds(s, t_tile, stride=PK_ROWS), :])


def _rope_tables(seq):
    pos = jnp.arange(seq, dtype=F32)
    inv_freq = ROPE_THETA ** (-jnp.arange(0, ROT_DIM, 2, dtype=F32) / ROT_DIM)
    ang = pos[:, None] * inv_freq[None, :]
    cos, sin = jnp.cos(ang), jnp.sin(ang)
    half = ROT_DIM // 2
    ones = jnp.ones((seq, HEAD_DIM - ROT_DIM), F32)
    zeros = jnp.zeros((seq, HEAD_DIM - ROT_DIM), F32)
    zh = jnp.zeros((seq, half), F32)
    cc = jnp.concatenate([cos, cos, ones], axis=1)
    s_up = jnp.concatenate([-sin, zh, zeros], axis=1)
    s_dn = jnp.concatenate([zh, sin, zeros], axis=1)
    rep = LANES // HEAD_DIM
    return tuple(jnp.tile(t, (1, rep)) for t in (cc, s_up, s_dn))


def _plan(cnt, n_tokens):
    nt = n_tokens // LOCAL_TILE
    experts = jnp.arange(N_EXPERTS, dtype=jnp.int32)
    cntp = ((cnt + CHUNK - 1) // CHUNK) * CHUNK
    loff_end = jnp.cumsum(cntp, axis=1)
    loff = loff_end - cntp
    rows_e = jnp.sum(cntp, axis=0)
    rows_pad = ((rows_e + EXPERT_TILE - 1) // EXPERT_TILE) * EXPERT_TILE
    g_end = jnp.cumsum(rows_pad)
    g_start = g_end - rows_pad
    gpos = g_start[None, :] + jnp.cumsum(cntp, axis=0) - cntp

    lrow = jnp.arange(LT_MAX // CHUNK, dtype=jnp.int32) * CHUNK
    e_lc = jnp.sum((lrow[None, :, None] >= loff_end[:, None, :]).astype(jnp.int32), axis=-1)
    sel = e_lc[:, :, None] == experts
    delta = jnp.sum(jnp.where(sel, (gpos - loff)[:, None, :], 0), axis=-1)
    gch = jnp.where(e_lc < N_EXPERTS, (delta + lrow[None, :]) // CHUNK, 0).reshape(-1)

    n_rows = TOP_K * n_tokens + nt * N_EXPERTS * (CHUNK - 1) + N_EXPERTS * (EXPERT_TILE - CHUNK)
    n_blocks = -(-n_rows // EXPERT_TILE)
    n_rows = n_blocks * EXPERT_TILE
    blk_start = jnp.arange(n_blocks, dtype=jnp.int32) * EXPERT_TILE
    blk_eid = jnp.minimum(
        jnp.sum((blk_start[:, None] >= g_end[None, :]).astype(jnp.int32), axis=-1),
        N_EXPERTS - 1)
    prev = jnp.concatenate([jnp.full((1,), -1, jnp.int32), blk_eid[:-1]])
    first = (blk_eid != prev).astype(jnp.int32)
    n_act = (g_end[-1] // EXPERT_TILE).astype(jnp.int32).reshape(1)
    later = (blk_eid[None, :] > blk_eid[:, None]) & (jnp.arange(n_blocks)[None, :] < n_act)
    next_eid = jnp.min(jnp.where(later, blk_eid[None, :], N_EXPERTS), axis=1)
    next_eid = jnp.where(next_eid < N_EXPERTS, next_eid, -1).astype(jnp.int32)
    of_blk = blk_eid[:, None] == experts
    blk_end = jnp.sum(jnp.where(of_blk, (g_start + rows_e)[None, :], 0), axis=1)
    used = jnp.clip(blk_end - blk_start, 0, EXPERT_TILE).astype(jnp.int32)

    cpb = EXPERT_TILE // CHUNK
    run_start = gpos.T // CHUNK
    run_len = cntp.T // CHUNK
    run_src = (jnp.arange(nt, dtype=jnp.int32)[:, None] * (LT_MAX // CHUNK) + loff // CHUNK).T
    tabs = jnp.stack([run_start, run_len, run_src], axis=0)
    dtabs = tabs - jnp.concatenate(
        [jnp.zeros((3, N_EXPERTS, 1), jnp.int32), tabs[:, :, :-1]], axis=2)
    blk_sel = (blk_eid[:, None] == experts)[:, :, None]
    blk_start_t = jnp.sum(jnp.where(blk_sel, run_start[None], 0), axis=1)
    blk_dtabs = jnp.sum(jnp.where(blk_sel[None], dtabs[:, None], 0), axis=2)
    gc = jnp.arange(n_rows // CHUNK, dtype=jnp.int32).reshape(n_blocks, cpb)
    started = blk_start_t[:, None, :] <= gc[:, :, None]
    picked = jnp.sum(jnp.where(started[None], blk_dtabs[:, :, None, :], 0), axis=-1)
    within = gc - picked[0]
    csrc = jnp.where(within < picked[1], picked[2] + within, 0).reshape(-1)
    return (gch.astype(jnp.int32), csrc.astype(jnp.int32), blk_eid, first, next_eid, used,
            n_act, n_rows)


def kernel(x, norm_mix, w_in, q_norm, k_norm, sinks, w_pool, pool_scale, w_out,
           norm_ffn, w_coarse, b_coarse, w_fine, b_fine, w_gate, w_up, w_down):
    bsz, seq, d = x.shape
    n = bsz * seq
    assert d == D_MODEL and seq % ROW_TILE == 0 and seq % BLOCK == 0
    assert norm_mix.shape[0] == 1, "single-layer problem"
    xf = x.reshape(n, d)

    w_in_b = w_in[0].astype(BF16)
    wq = w_in_b[:, :ATTN_WIDTH]
    wk = w_in_b[:, ATTN_WIDTH:ATTN_WIDTH + KV_WIDTH]
    wv = w_in_b[:, ATTN_WIDTH + KV_WIDTH:ATTN_WIDTH + 2 * KV_WIDTH]
    wu = w_in_b[:, ATTN_WIDTH + 2 * KV_WIDTH:]
    qn = jnp.tile(q_norm[0], N_HEADS).reshape(1, ATTN_WIDTH)
    kn = jnp.tile(k_norm[0], N_KV_HEADS).reshape(1, KV_WIDTH)
    cc, s_up, s_dn = _rope_tables(seq)
    lane_head = jnp.arange(MXU_DIM) // HEAD_DIM
    e_mat = (lane_head[:, None] == lane_head[None, :]).astype(BF16)
    w_pool_b = w_pool[0].astype(BF16)
    pscale = pool_scale[0].reshape(1, POOL_WIDTH)
    w_out_b = w_out[0].astype(BF16)
    wo_attn, wo_pool = w_out_b[:ATTN_WIDTH], w_out_b[ATTN_WIDTH:]
    gap = ROUTER_FINE0 - N_EXPERT_GROUPS
    tail = LANES - ROUTER_FINE0 - N_EXPERTS
    w_r = jnp.concatenate([w_coarse[0], jnp.zeros((d, gap), F32), w_fine[0],
                           jnp.zeros((d, tail), F32)], axis=1)
    w_r_hi = w_r.astype(BF16)
    w_r_lo = (w_r - w_r_hi.astype(F32)).astype(BF16)
    w_r2 = jnp.concatenate([w_r_hi, w_r_lo], axis=1)
    b_r = jnp.concatenate([b_coarse[0], jnp.zeros((gap,), F32), b_fine[0],
                           jnp.zeros((tail,), F32)]).reshape(1, LANES)

    tm = ROW_TILE
    tiles_per_seq = seq // tm
    n_tiles = n // tm
    idx = jnp.arange(SUB_TILE)
    tri = (idx[None, :] < idx[:, None]).astype(BF16)
    const = lambda *_: (0, 0)
    row_blk = lambda i: (i, 0)

    q, k, v, ksw, vsw, y = pl.pallas_call(
        functools.partial(_inproj_kernel, tiles_per_seq=tiles_per_seq),
        grid=(n_tiles,),
        in_specs=[
            pl.BlockSpec((tm, d), row_blk),
            pl.BlockSpec((1, d), const),
            pl.BlockSpec((d, ATTN_WIDTH), const),
            pl.BlockSpec((d, KV_WIDTH), const),
            pl.BlockSpec((d, KV_WIDTH), const),
            pl.BlockSpec((d, POOL_WIDTH), const),
            pl.BlockSpec((1, ATTN_WIDTH), const),
            pl.BlockSpec((1, KV_WIDTH), const),
            pl.BlockSpec((tm, LANES), lambda i: (i % tiles_per_seq, 0)),
            pl.BlockSpec((tm, LANES), lambda i: (i % tiles_per_seq, 0)),
            pl.BlockSpec((tm, LANES), lambda i: (i % tiles_per_seq, 0)),
            pl.BlockSpec((MXU_DIM, MXU_DIM), const),
            pl.BlockSpec((len(POOL_WINDOWS), POOL_GROUP_DIM, POOL_GROUP_DIM),
                         lambda i: (0, 0, 0)),
            pl.BlockSpec((1, POOL_WIDTH), const),
        ],
        out_specs=[
            pl.BlockSpec((tm, ATTN_WIDTH), row_blk),
            pl.BlockSpec((tm, KV_WIDTH), row_blk),
            pl.BlockSpec((tm, KV_WIDTH), row_blk),
            pl.BlockSpec((tm, KV_WIDTH), row_blk),
            pl.BlockSpec((tm, KV_WIDTH), row_blk),
            pl.BlockSpec((tm, POOL_WIDTH), row_blk),
        ],
        out_shape=[
            jax.ShapeDtypeStruct((n, ATTN_WIDTH), BF16),
            jax.ShapeDtypeStruct((n, KV_WIDTH), BF16),
            jax.ShapeDtypeStruct((n, KV_WIDTH), BF16),
            jax.ShapeDtypeStruct((n, KV_WIDTH), BF16),
            jax.ShapeDtypeStruct((n, KV_WIDTH), BF16),
            jax.ShapeDtypeStruct((n, POOL_WIDTH), BF16),
        ],
        scratch_shapes=[pltpu.VMEM((HALO + tm, POOL_WIDTH), F32)],
        compiler_params=_cparams(),
        name="inproj",
    )(xf, norm_mix[0].reshape(1, d), wq, wk, wv, wu, qn, kn, cc, s_up, s_dn,
      e_mat, w_pool_b, pscale)

    nb = seq // BLOCK
    nq = nb // Q_BLOCKS
    qrows = Q_BLOCKS * BLOCK
    cur = lambda b, j, *_: (b * nq + j, 0)
    prv = lambda b, j, *_: (b * nb + jnp.maximum(Q_BLOCKS * j - 1, 0), 0)
    kv_specs = [pl.BlockSpec((qrows, KV_WIDTH), cur), pl.BlockSpec((BLOCK, KV_WIDTH), prv)]
    attn = pl.pallas_call(
        _attn_kernel,
        grid_spec=pltpu.PrefetchScalarGridSpec(
            num_scalar_prefetch=1,
            grid=(bsz, nq),
            in_specs=[pl.BlockSpec((qrows, ATTN_WIDTH), cur)] + kv_specs * 4,
            out_specs=pl.BlockSpec((qrows, ATTN_WIDTH), cur),
        ),
        out_shape=jax.ShapeDtypeStruct((n, ATTN_WIDTH), BF16),
        compiler_params=_cparams(2),
        name="swa_attn",
    )(sinks[0].astype(F32), q, k, k, ksw, ksw, v, v, vsw, vsw)

    h, hpk, route, cnt = pl.pallas_call(
        _outproj_router_kernel,
        grid=(n_tiles,),
        in_specs=[
            pl.BlockSpec((tm, ATTN_WIDTH), row_blk),
            pl.BlockSpec((tm, POOL_WIDTH), row_blk),
            pl.BlockSpec((tm, d), row_blk),
            pl.BlockSpec((ATTN_WIDTH, d), const),
            pl.BlockSpec((POOL_WIDTH, d), const),
            pl.BlockSpec((1, d), const),
            pl.BlockSpec((d, 2 * LANES), const),
            pl.BlockSpec((1, LANES), const),
            pl.BlockSpec((SUB_TILE, SUB_TILE), const),
        ],
        out_specs=[
            pl.BlockSpec((tm, d), row_blk),
            pl.BlockSpec((tm * PK_ROWS, LANES), row_blk),
            pl.BlockSpec((SUBLANES, tm), lambda i: (0, i)),
            pl.BlockSpec((N_EXPERTS, LANES), row_blk),
        ],
        out_shape=[
            jax.ShapeDtypeStruct((n, d), F32),
            jax.ShapeDtypeStruct((n * PK_ROWS, LANES), jnp.int32),
            jax.ShapeDtypeStruct((SUBLANES, n), F32),
            jax.ShapeDtypeStruct((n_tiles * N_EXPERTS, LANES), F32),
        ],
        compiler_params=_cparams(),
        name="outproj_router",
    )(attn, y, xf, wo_attn, wo_pool, norm_ffn[0].reshape(1, d), w_r2, b_r, tri)

    assert tm == LOCAL_TILE
    gates = route[0:TOP_K].reshape(TOP_K * n)
    lp = route[TOP_K:2 * TOP_K].astype(jnp.int32).reshape(TOP_K * n)
    tile_cnt = cnt.reshape(n_tiles, N_EXPERTS, LANES)[:, :, 0].astype(jnp.int32)
    te = EXPERT_TILE
    gch, csrc, blk_eid, first, next_eid, used, n_act, n_rows = _plan(tile_cnt, n)
    n_blocks = n_rows // te
    n_local = n // LOCAL_TILE

    xs = pl.pallas_call(
        _local_sort_kernel,
        grid_spec=pltpu.PrefetchScalarGridSpec(
            num_scalar_prefetch=1,
            grid=(n_local,),
            in_specs=[pl.BlockSpec((LOCAL_TILE * PK_ROWS, LANES), lambda i, *_: (i, 0))],
            out_specs=pl.BlockSpec((LT_MAX * PK_ROWS, LANES), lambda i, *_: (i, 0)),
        ),
        out_shape=jax.ShapeDtypeStruct((n_local * LT_MAX * PK_ROWS, LANES), jnp.int32),
        compiler_params=_cparams(),
        name="local_sort",
    )(lp * PK_ROWS, hpk)

    ys = pl.pallas_call(
        _expert_kernel,
        grid_spec=pltpu.PrefetchScalarGridSpec(
            num_scalar_prefetch=6,
            grid=(n_blocks,),
            in_specs=[pl.BlockSpec(memory_space=pl.ANY)] * 4,
            out_specs=pl.BlockSpec((te * PK_ROWS, LANES), lambda i, *_: (i, 0)),
            scratch_shapes=[
                pltpu.VMEM((2, te * PK_ROWS, LANES), jnp.int32),
                pltpu.VMEM((d, D_EXPERT), F32),
                pltpu.VMEM((d, D_EXPERT), F32),
                pltpu.VMEM((D_EXPERT, d), F32),
                pltpu.VMEM((d, 2 * D_EXPERT), BF16),
                pltpu.VMEM((D_EXPERT, d), BF16),
                pltpu.SemaphoreType.DMA((2,)),
                pltpu.SemaphoreType.DMA((3,)),
            ],
        ),
        out_shape=jax.ShapeDtypeStruct((n_rows * PK_ROWS, LANES), jnp.int32),
        compiler_params=_cparams(),
        name="experts",
    )(blk_eid, first, n_act, csrc * (CHUNK * PK_ROWS), next_eid, used, xs,
      w_gate[0], w_up[0], w_down[0])

    tt = LOCAL_TILE
    out = pl.pallas_call(
        _combine_kernel,
        grid_spec=pltpu.PrefetchScalarGridSpec(
            num_scalar_prefetch=3,
            grid=(n_local,),
            in_specs=[
                pl.BlockSpec((tt, d), lambda i, *_: (i, 0)),
                pl.BlockSpec(memory_space=pl.ANY),
            ],
            out_specs=pl.BlockSpec((tt, d), lambda i, *_: (i, 0)),
            scratch_shapes=[
                pltpu.VMEM((2, LT_MAX * PK_ROWS, LANES), jnp.int32),
                pltpu.VMEM((2, tt * PK_ROWS, LANES), F32),
                pltpu.SemaphoreType.DMA((2,)),
            ],
        ),
        out_shape=jax.ShapeDtypeStruct((n, d), F32),
        compiler_params=_cparams(),
        name="combine",
    )(gch * (CHUNK * PK_ROWS), lp * PK_ROWS, gates, h, ys)
    return out.reshape(bsz, seq, d)
```

```python
import functools

import jax
import jax.numpy as jnp
from jax import lax
from jax.experimental import pallas as pl
from jax.experimental.pallas import tpu as pltpu

F32 = jnp.float32
BF16 = jnp.bfloat16

D_MODEL = 2048
N_HEADS = 16
N_KV_HEADS = 4
HEAD_DIM = 64
GROUP = N_HEADS // N_KV_HEADS
ROT_DIM = HEAD_DIM // 4
ROPE_THETA = 500000.0
WINDOW = 128
BLOCK = 128
Q_BLOCKS = 16
ATTN_WIDTH = N_HEADS * HEAD_DIM
KV_WIDTH = N_KV_HEADS * HEAD_DIM
POOL_WINDOWS = (2, 4, 8, 16)
POOL_WIDTH = D_MODEL // 2
POOL_GROUP_DIM = POOL_WIDTH // len(POOL_WINDOWS)
N_EXPERT_GROUPS = 4
EXPERTS_PER_GROUP = 8
N_EXPERTS = N_EXPERT_GROUPS * EXPERTS_PER_GROUP
TOP_K = 2
D_EXPERT = 512
EPS = 1e-6

LANES = 128
SUBLANES = 8
MXU_DIM = 256
HALO = 16
NEG_BIG = -1e30
ROUTER_FINE0 = 8
HALF = D_MODEL // 2
PK_ROWS = HALF // LANES

ROW_TILE = 512
SUB_TILE = 512
EXPERT_TILE = 512
LOCAL_TILE = 512
CHUNK = 8
LT_MAX = TOP_K * LOCAL_TILE + N_EXPERTS * (CHUNK - 1)
VMEM_LIMIT = 56 * 1024 * 1024


def _cparams(n_axes=1):
    return pltpu.CompilerParams(
        dimension_semantics=("arbitrary",) * n_axes,
        vmem_limit_bytes=VMEM_LIMIT,
    )


def _pack_halves(t):
    return pltpu.pack_elementwise([t[:, :HALF], t[:, HALF:]], packed_dtype=BF16)


def _unpack_halves(w):
    return tuple(pltpu.unpack_elementwise(w, index=i, packed_dtype=BF16, unpacked_dtype=F32)
                 for i in range(2))


def _head_sumsq(t, e_ref):
    t2 = (t * t).astype(BF16)
    e = e_ref[...]
    outs = []
    for c in range(t.shape[1] // MXU_DIM):
        sl = slice(c * MXU_DIM, (c + 1) * MXU_DIM)
        outs.append(jnp.dot(t2[:, sl], e, preferred_element_type=F32))
    return outs[0] if len(outs) == 1 else jnp.concatenate(outs, axis=1)


def _norm_rope(t, gain, e_ref, cc, s_up, s_dn, out_scale):
    w = t.shape[1]
    ss = _head_sumsq(t, e_ref)
    tn = t * lax.rsqrt(ss * (1.0 / HEAD_DIM) + EPS) * gain
    reps = w // LANES
    up = pltpu.roll(tn, w - ROT_DIM // 2, axis=1)
    dn = pltpu.roll(tn, ROT_DIM // 2, axis=1)
    c = jnp.concatenate([cc] * reps, axis=1)
    su = jnp.concatenate([s_up] * reps, axis=1)
    sd = jnp.concatenate([s_dn] * reps, axis=1)
    return (tn * c + up * su + dn * sd) * out_scale


def _swap_head_pairs(t):
    w = t.shape[1]
    lane = lax.broadcasted_iota(jnp.int32, t.shape, 1)
    from_up = pltpu.roll(t, w - HEAD_DIM, axis=1)
    from_dn = pltpu.roll(t, HEAD_DIM, axis=1)
    return jnp.where((lane & (LANES - 1)) < HEAD_DIM, from_up, from_dn)


def _inproj_kernel(x_ref, g_ref, wq_ref, wk_ref, wv_ref, wu_ref, qn_ref, kn_ref,
                   cc_ref, su_ref, sd_ref, e_ref, wpool_ref, pscale_ref,
                   q_out, k_out, v_out, ksw_out, vsw_out, y_out, ubuf, *, tiles_per_seq):
    i = pl.program_id(0)
    tm = x_ref.shape[0]

    @pl.when(i % tiles_per_seq == 0)
    def _():
        ubuf[0:HALO, :] = jnp.zeros((HALO, POOL_WIDTH), F32)

    for r0 in range(0, tm, SUB_TILE):
        rows = slice(r0, r0 + SUB_TILE)
        x = x_ref[rows, :]
        ms = jnp.mean(x * x, axis=-1, keepdims=True)
        hn = (x * lax.rsqrt(ms + EPS) * g_ref[...]).astype(BF16)

        cc, s_up, s_dn = cc_ref[rows, :], su_ref[rows, :], sd_ref[rows, :]
        q = jnp.dot(hn, wq_ref[...], preferred_element_type=F32)
        q_out[rows, :] = _norm_rope(q, qn_ref[...], e_ref, cc, s_up, s_dn,
                                    HEAD_DIM ** -0.5).astype(BF16)
        k = jnp.dot(hn, wk_ref[...], preferred_element_type=F32)
        k = _norm_rope(k, kn_ref[...], e_ref, cc, s_up, s_dn, 1.0)
        v = jnp.dot(hn, wv_ref[...], preferred_element_type=F32)
        k_out[rows, :] = k.astype(BF16)
        v_out[rows, :] = v.astype(BF16)
        ksw_out[rows, :] = _swap_head_pairs(k).astype(BF16)
        vsw_out[rows, :] = _swap_head_pairs(v).astype(BF16)

        base = HALO + r0
        ubuf[base:base + SUB_TILE, :] = jnp.dot(hn, wu_ref[...], preferred_element_type=F32)
        pos = (i % tiles_per_seq) * tm + r0 + lax.broadcasted_iota(
            jnp.int32, (SUB_TILE, POOL_GROUP_DIM), 0)
        for g, w in enumerate(POOL_WINDOWS):
            cols = slice(g * POOL_GROUP_DIM, (g + 1) * POOL_GROUP_DIM)
            assert w & (w - 1) == 0 and w - 1 < HALO
            acc = ubuf[base - HALO:base + SUB_TILE, cols]
            shift = 1
            while shift < w:
                acc = acc + pltpu.roll(acc, shift, axis=0)
                shift *= 2
            acc = acc[HALO:, :]
            u_g = ubuf[base:base + SUB_TILE, cols]
            cnt = jnp.minimum(pos + 1, w).astype(F32)
            d = (acc / cnt - u_g).astype(BF16)
            y = jnp.dot(d, wpool_ref[g], preferred_element_type=F32)
            y_out[rows, cols] = (y * pscale_ref[:, cols]).astype(BF16)
    ubuf[0:HALO, :] = ubuf[tm:tm + HALO, :]


def _attn_kernel(sinks_ref, q_ref, kc_ref, kp_ref, kswc_ref, kswp_ref,
                 vc_ref, vp_ref, vswc_ref, vswp_ref, o_ref):
    j = pl.program_id(1)
    two = 2 * BLOCK
    row = lax.broadcasted_iota(jnp.int32, (two, two), 0) & (BLOCK - 1)
    col = lax.broadcasted_iota(jnp.int32, (two, two), 1)
    diff = row + BLOCK - col
    local = (diff >= 0) & (diff < WINDOW)
    lo_lanes = lax.broadcasted_iota(jnp.int32, (two, LANES), 1) < HEAD_DIM
    row_p = lax.broadcasted_iota(jnp.int32, (two, BLOCK), 0)
    diag = (row_p & (BLOCK - 1)) == lax.broadcasted_iota(jnp.int32, (two, BLOCK), 1)
    top_rows = row_p < BLOCK
    diag_top = diag & top_rows
    diag_bot = diag & jnp.logical_not(top_rows)
    zero = jnp.zeros((two, LANES), BF16)

    for sb in range(q_ref.shape[0] // BLOCK):
        rows = slice(sb * BLOCK, (sb + 1) * BLOCK)
        if sb == 0:
            def band(cur_ref, prev_ref):
                return jnp.concatenate([prev_ref[...], cur_ref[0:BLOCK, :]], axis=0)
            valid = local & ((col >= BLOCK) | (j > 0))
        else:
            def band(cur_ref, prev_ref, sb=sb):
                return cur_ref[(sb - 1) * BLOCK:(sb + 1) * BLOCK, :]
            valid = local
        k_nat, k_swp = band(kc_ref, kp_ref), band(kswc_ref, kswp_ref)
        v_nat, v_swp = band(vc_ref, vp_ref), band(vswc_ref, vswp_ref)

        for g in range(N_KV_HEADS):
            c0 = (g // 2) * LANES
            kcols = (k_nat[:, c0:c0 + LANES], k_swp[:, c0:c0 + LANES])
            vcols = (v_nat[:, c0:c0 + LANES], v_swp[:, c0:c0 + LANES])
            in_lo, in_hi = (0, 1) if g % 2 == 0 else (1, 0)
            q0 = g * GROUP * HEAD_DIM
            qq = jnp.concatenate([q_ref[rows, q0:q0 + LANES],
                                  q_ref[rows, q0 + LANES:q0 + 2 * LANES]], axis=0)
            halves = []
            for half, src in ((0, in_lo), (1, in_hi)):
                qm = jnp.where(lo_lanes, qq, zero) if half == 0 else jnp.where(lo_lanes, zero, qq)
                s = lax.dot_general(qm, kcols[src], (((1,), (1,)), ((), ())),
                                    preferred_element_type=F32)
                s = jnp.where(valid, s, NEG_BIG)
                h_top = g * GROUP + half
                s_prev = jnp.where(diag_top, sinks_ref[h_top],
                                   jnp.where(diag_bot, sinks_ref[h_top + 2], s[:, :BLOCK]))
                s_cur = s[:, BLOCK:]
                m = jnp.max(jnp.maximum(s_prev, s_cur), axis=-1, keepdims=True)
                p_prev = jnp.exp(s_prev - m)
                p_cur = jnp.exp(s_cur - m)
                denom = jnp.sum(p_prev + p_cur, axis=-1, keepdims=True)
                p = jnp.concatenate([jnp.where(diag, 0.0, p_prev), p_cur], axis=1)
                o = jnp.dot(p.astype(BF16), vcols[src], preferred_element_type=F32)
                halves.append(o / denom)
            o_pair = jnp.where(lo_lanes, halves[0], halves[1]).astype(BF16)
            o_ref[rows, q0:q0 + LANES] = o_pair[0:BLOCK, :]
            o_ref[rows, q0 + LANES:q0 + 2 * LANES] = o_pair[BLOCK:, :]


def _outproj_router_kernel(o_ref, y_ref, x_ref, woa_ref, wop_ref, g_ref, wr_ref,
                           br_ref, tri_ref, h_out, hpk_out, route_out, cnt_out):
    tm = x_ref.shape[0]
    h = (x_ref[...]
         + jnp.dot(o_ref[...], woa_ref[...], preferred_element_type=F32)
         + jnp.dot(y_ref[...], wop_ref[...], preferred_element_type=F32))
    h_out[...] = h
    ms = jnp.mean(h * h, axis=-1, keepdims=True)
    hn = h * lax.rsqrt(ms + EPS) * g_ref[...]
    hb = hn.astype(BF16)
    packed = _pack_halves(hn)
    for s in range(PK_ROWS):
        hpk_out[pl.ds(s, tm, stride=PK_ROWS), :] = packed[:, s * LANES:(s + 1) * LANES]

    both = jnp.dot(hb, wr_ref[...], preferred_element_type=F32)
    logits = (both[:, :LANES] + both[:, LANES:] + br_ref[...]).T
    assert EXPERTS_PER_GROUP == SUBLANES and ROUTER_FINE0 % SUBLANES == 0
    slot = lax.broadcasted_iota(jnp.int32, (SUBLANES, tm), 0).astype(F32)

    def col_max(t):
        return jnp.max(t, axis=0, keepdims=True)

    def first_slot_of(mask):
        return jnp.min(jnp.where(mask, slot, float(SUBLANES)), axis=0, keepdims=True)

    coarse = jnp.where(slot < N_EXPERT_GROUPS, logits[0:SUBLANES, :], NEG_BIG)
    cmax = col_max(coarse)
    grp = first_slot_of(coarse == cmax)
    csum = jnp.sum(jnp.exp(coarse - cmax), axis=0, keepdims=True)

    def fine_of(g):
        return logits[ROUTER_FINE0 + g * SUBLANES:ROUTER_FINE0 + (g + 1) * SUBLANES, :]

    fine = fine_of(N_EXPERT_GROUPS - 1)
    for g in range(N_EXPERT_GROUPS - 2, -1, -1):
        fine = jnp.where(grp == float(g), fine_of(g), fine)
    f1 = col_max(fine)
    i1 = first_slot_of(fine == f1)
    rest = jnp.where(slot == i1, NEG_BIG, fine)
    f2 = col_max(rest)
    i2 = first_slot_of(rest == f2)
    ratio = jnp.exp(f2 - f1)
    g1 = (1.0 / csum) / (1.0 + ratio)
    g2 = g1 * ratio
    e1 = grp * EXPERTS_PER_GROUP + i1
    e2 = grp * EXPERTS_PER_GROUP + i2

    expert = lax.broadcasted_iota(jnp.int32, (N_EXPERTS, tm), 0).astype(F32)
    oh1 = expert == e1
    oh2 = expert == e2
    oh = jnp.where(oh1, 1.0, 0.0) + jnp.where(oh2, 1.0, 0.0)
    rank = lax.dot_general(oh.astype(BF16), tri_ref[...], (((1,), (1,)), ((), ())),
                           preferred_element_type=F32)
    cnt = jnp.sum(oh, axis=1, keepdims=True)
    cnt_pad = jnp.floor((cnt + (CHUNK - 1)) * (1.0 / CHUNK)) * CHUNK
    padded_cols = jnp.concatenate(
        [jnp.broadcast_to(cnt_pad, (N_EXPERTS, LANES)),
         jnp.zeros((LANES - N_EXPERTS, LANES), F32)], axis=0).astype(BF16)
    run_start = jnp.dot(tri_ref[0:N_EXPERTS, 0:LANES], padded_cols,
                        preferred_element_type=F32)[:, 0:1]
    table = rank + run_start
    lp1 = jnp.sum(jnp.where(oh1, table, 0.0), axis=0, keepdims=True)
    lp2 = jnp.sum(jnp.where(oh2, table, 0.0), axis=0, keepdims=True)
    route_out[...] = jnp.where(slot == 0.0, g1,
                               jnp.where(slot == 1.0, g2,
                                         jnp.where(slot == 2.0, lp1,
                                                   jnp.where(slot == 3.0, lp2, 0.0))))
    cnt_out[...] = jnp.broadcast_to(cnt, cnt_out.shape)


def _expert_kernel(eid_ref, first_ref, nact_ref, off_ref, next_ref, used_ref,
                   hpk_hbm, wg_hbm, wu_hbm, wd_hbm, ys_out,
                   xbuf, wg_f, wu_f, wd_f, wgu_s, wd_s, sems, wsems):
    i = pl.program_id(0)
    tm = ys_out.shape[0] // PK_ROWS
    n_act = nact_ref[0]

    def weight_copies(e):
        return (pltpu.make_async_copy(wg_hbm.at[e], wg_f, wsems.at[0]),
                pltpu.make_async_copy(wu_hbm.at[e], wu_f, wsems.at[1]),
                pltpu.make_async_copy(wd_hbm.at[e], wd_f, wsems.at[2]))

    ch_rows = CHUNK * PK_ROWS
    n_ch = tm // CHUNK

    def gather(blk, slot):
        for c in range(n_ch):
            src = pl.multiple_of(off_ref[blk * n_ch + c], ch_rows)
            pltpu.make_async_copy(hpk_hbm.at[pl.ds(src, ch_rows)],
                                  xbuf.at[slot, pl.ds(c * ch_rows, ch_rows)],
                                  sems.at[slot]).start()

    @pl.when(i == 0)
    def _():
        for cp in weight_copies(eid_ref[0]):
            cp.start()
        gather(0, 0)

    @pl.when(i + 1 < n_act)
    def _():
        gather(i + 1, (i + 1) % 2)

    @pl.when(i < n_act)
    def _():
        slot = i % 2

        @pl.when(first_ref[i] == 1)
        def _():
            for cp in weight_copies(eid_ref[i]):
                cp.wait()
            wgu_s[:, :D_EXPERT] = wg_f[...].astype(BF16)
            wgu_s[:, D_EXPERT:] = wu_f[...].astype(BF16)
            wd_s[...] = wd_f[...].astype(BF16)

            @pl.when(next_ref[i] >= 0)
            def _():
                for cp in weight_copies(next_ref[i]):
                    cp.start()

        pltpu.make_async_copy(hpk_hbm.at[pl.ds(0, tm * PK_ROWS)], xbuf.at[slot],
                              sems.at[slot]).wait()

        def swiglu(rows):
            xu = jnp.concatenate(
                [xbuf[slot, pl.ds(s, rows, stride=PK_ROWS), :] for s in range(PK_ROWS)],
                axis=1)
            x = jnp.concatenate([v.astype(BF16) for v in _unpack_halves(xu)], axis=1)
            ab = jnp.dot(x, wgu_s[...], preferred_element_type=F32)
            a, b = ab[:, :D_EXPERT], ab[:, D_EXPERT:]
            mid = (a * jax.nn.sigmoid(a) * b).astype(BF16)
            y = jnp.dot(mid, wd_s[...], preferred_element_type=F32)
            yw = _pack_halves(y)
            for s in range(PK_ROWS):
                ys_out[pl.ds(s, rows, stride=PK_ROWS), :] = yw[:, s * LANES:(s + 1) * LANES]
            if rows < tm:
                ys_out[rows * PK_ROWS:, :] = jnp.zeros(((tm - rows) * PK_ROWS, LANES),
                                                       ys_out.dtype)

        eighth = tm // 8
        for rows in range(eighth, tm + 1, eighth):
            @pl.when((used_ref[i] > rows - eighth) & (used_ref[i] <= rows))
            def _(rows=rows):
                swiglu(rows)

    @pl.when(i >= n_act)
    def _():
        ys_out[...] = jnp.zeros(ys_out.shape, ys_out.dtype)


def _local_sort_kernel(lp_ref, hpk_ref, xs_out):
    i = pl.program_id(0)
    t_tile = hpk_ref.shape[0] // PK_ROWS
    n_tok = pl.num_programs(0) * t_tile
    xs_out[...] = jnp.zeros(xs_out.shape, xs_out.dtype)

    def body(q, carry):
        for u in range(SUBLANES):
            t = q * SUBLANES + u
            row = hpk_ref[pl.ds(pl.multiple_of(t * PK_ROWS, PK_ROWS), PK_ROWS), :]
            for k in range(TOP_K):
                dst = pl.multiple_of(lp_ref[k * n_tok + i * t_tile + t], PK_ROWS)
                xs_out[pl.ds(dst, PK_ROWS), :] = row
        return carry
    lax.fori_loop(0, t_tile // SUBLANES, body, 0)


def _combine_kernel(gch_ref, lp_ref, gate_ref, h_ref, ys_hbm, out_ref, ylocal, acc, sems):
    i = pl.program_id(0)
    n = pl.num_programs(0)
    t_tile = h_ref.shape[0]
    n_tok = n * t_tile
    ch_rows = CHUNK * PK_ROWS
    n_ch = ylocal.shape[1] // ch_rows

    def copy_rows(src, slot, dst, rows):
        pltpu.make_async_copy(ys_hbm.at[pl.ds(src, rows)],
                              ylocal.at[slot, pl.ds(dst, rows)],
                              sems.at[slot]).start()

    def chunk_copy(tile, slot, lc):
        src = pl.multiple_of(gch_ref[tile * n_ch + lc], ch_rows)
        copy_rows(src, slot, pl.multiple_of(lc * ch_rows, ch_rows), ch_rows)

    def chunk_pair_copy(tile, slot, lc):
        src0 = pl.multiple_of(gch_ref[tile * n_ch + lc], ch_rows)
        src1 = pl.multiple_of(gch_ref[tile * n_ch + lc + 1], ch_rows)
        dst = pl.multiple_of(lc * ch_rows, ch_rows)
        joined = src1 == src0 + ch_rows

        @pl.when(joined)
        def _():
            copy_rows(src0, slot, dst, 2 * ch_rows)

        @pl.when(jnp.logical_not(joined))
        def _():
            copy_rows(src0, slot, dst, ch_rows)
            copy_rows(src1, slot, dst + ch_rows, ch_rows)

    def wait_slot(slot):
        pltpu.make_async_copy(ys_hbm.at[pl.ds(0, n_ch * ch_rows)], ylocal.at[slot],
                              sems.at[slot]).wait()

    @pl.when(i == 0)
    def _():
        def body(lc, carry):
            chunk_copy(0, 0, lc)
            return carry
        lax.fori_loop(0, n_ch, body, 0)

    slot = i % 2
    wait_slot(slot)

    nxt = jnp.minimum(i + 1, n - 1)
    tok_unroll = 4
    pairs_per_group = 1
    n_groups = t_tile // tok_unroll
    issue_groups = n_ch // (2 * pairs_per_group)
    assert issue_groups * 2 * pairs_per_group == n_ch and issue_groups <= n_groups

    def tok_body(q, carry, issue):
        if issue:
            for u in range(pairs_per_group):
                chunk_pair_copy(nxt, 1 - slot, 2 * (q * pairs_per_group + u))
        for u in range(tok_unroll):
            t = q * tok_unroll + u
            a0 = i * t_tile + t
            a1 = n_tok + a0
            p0 = pl.multiple_of(lp_ref[a0], PK_ROWS)
            p1 = pl.multiple_of(lp_ref[a1], PK_ROWS)
            lo0, hi0 = _unpack_halves(ylocal[slot, pl.ds(p0, PK_ROWS), :])
            lo1, hi1 = _unpack_halves(ylocal[slot, pl.ds(p1, PK_ROWS), :])
            g0 = gate_ref[a0]
            g1 = gate_ref[a1]
            dst = pl.multiple_of(t * PK_ROWS, PK_ROWS)
            acc[0, pl.ds(dst, PK_ROWS), :] = g0 * lo0 + g1 * lo1
            acc[1, pl.ds(dst, PK_ROWS), :] = g0 * hi0 + g1 * hi1
        return carry
    lax.fori_loop(0, issue_groups, functools.partial(tok_body, issue=True), 0)
    lax.fori_loop(issue_groups, n_groups, functools.partial(tok_body, issue=False), 0)

    @pl.when(i == n - 1)
    def _():
        wait_slot(1 - slot)

    for half in range(2):
        for s in range(PK_ROWS):
            c0 = half * HALF + s * LANES
            out_ref[:, c0:c0 + LANES] = (h_ref[:, c0:c0 + LANES]
                                         + acc[half, pl.ds(s, t_tile, stride=PK_ROWS), :])


def _rope_tables(seq):
    pos = jnp.arange(seq, dtype=F32)
    inv_freq = ROPE_THETA ** (-jnp.arange(0, ROT_DIM, 2, dtype=F32) / ROT_DIM)
    ang = pos[:, None] * inv_freq[None, :]
    cos, sin = jnp.cos(ang), jnp.sin(ang)
    half = ROT_DIM // 2
    ones = jnp.ones((seq, HEAD_DIM - ROT_DIM), F32)
    zeros = jnp.zeros((seq, HEAD_DIM - ROT_DIM), F32)
    zh = jnp.zeros((seq, half), F32)
    cc = jnp.concatenate([cos, cos, ones], axis=1)
    s_up = jnp.concatenate([-sin, zh, zeros], axis=1)
    s_dn = jnp.concatenate([zh, sin, zeros], axis=1)
    rep = LANES // HEAD_DIM
    return tuple(jnp.tile(t, (1, rep)) for t in (cc, s_up, s_dn))


def _plan(cnt, n_tokens):
    nt = n_tokens // LOCAL_TILE
    experts = jnp.arange(N_EXPERTS, dtype=jnp.int32)
    cntp = ((cnt + CHUNK - 1) // CHUNK) * CHUNK
    loff_end = jnp.cumsum(cntp, axis=1)
    loff = loff_end - cntp
    rows_e = jnp.sum(cntp, axis=0)
    rows_pad = ((rows_e + EXPERT_TILE - 1) // EXPERT_TILE) * EXPERT_TILE
    g_end = jnp.cumsum(rows_pad)
    g_start = g_end - rows_pad
    gpos = g_start[None, :] + jnp.cumsum(cntp, axis=0) - cntp

    lrow = jnp.arange(LT_MAX // CHUNK, dtype=jnp.int32) * CHUNK
    e_lc = jnp.sum((lrow[None, :, None] >= loff_end[:, None, :]).astype(jnp.int32), axis=-1)
    sel = e_lc[:, :, None] == experts
    delta = jnp.sum(jnp.where(sel, (gpos - loff)[:, None, :], 0), axis=-1)
    gch = jnp.where(e_lc < N_EXPERTS, (delta + lrow[None, :]) // CHUNK, 0).reshape(-1)

    n_rows = TOP_K * n_tokens + nt * N_EXPERTS * (CHUNK - 1) + N_EXPERTS * (EXPERT_TILE - CHUNK)
    n_blocks = -(-n_rows // EXPERT_TILE)
    n_rows = n_blocks * EXPERT_TILE
    blk_start = jnp.arange(n_blocks, dtype=jnp.int32) * EXPERT_TILE
    blk_eid = jnp.minimum(
        jnp.sum((blk_start[:, None] >= g_end[None, :]).astype(jnp.int32), axis=-1),
        N_EXPERTS - 1)
    prev = jnp.concatenate([jnp.full((1,), -1, jnp.int32), blk_eid[:-1]])
    first = (blk_eid != prev).astype(jnp.int32)
    n_act = (g_end[-1] // EXPERT_TILE).astype(jnp.int32).reshape(1)
    later = (blk_eid[None, :] > blk_eid[:, None]) & (jnp.arange(n_blocks)[None, :] < n_act)
    next_eid = jnp.min(jnp.where(later, blk_eid[None, :], N_EXPERTS), axis=1)
    next_eid = jnp.where(next_eid < N_EXPERTS, next_eid, -1).astype(jnp.int32)
    of_blk = blk_eid[:, None] == experts
    blk_end = jnp.sum(jnp.where(of_blk, (g_start + rows_e)[None, :], 0), axis=1)
    used = jnp.clip(blk_end - blk_start, 0, EXPERT_TILE).astype(jnp.int32)

    cpb = EXPERT_TILE // CHUNK
    run_start = gpos.T // CHUNK
    run_len = cntp.T // CHUNK
    run_src = (jnp.arange(nt, dtype=jnp.int32)[:, None] * (LT_MAX // CHUNK) + loff // CHUNK).T
    tabs = jnp.stack([run_start, run_len, run_src], axis=0)
    dtabs = tabs - jnp.concatenate(
        [jnp.zeros((3, N_EXPERTS, 1), jnp.int32), tabs[:, :, :-1]], axis=2)
    blk_sel = (blk_eid[:, None] == experts)[:, :, None]
    blk_start_t = jnp.sum(jnp.where(blk_sel, run_start[None], 0), axis=1)
    blk_dtabs = jnp.sum(jnp.where(blk_sel[None], dtabs[:, None], 0), axis=2)
    gc = jnp.arange(n_rows // CHUNK, dtype=jnp.int32).reshape(n_blocks, cpb)
    started = blk_start_t[:, None, :] <= gc[:, :, None]
    picked = jnp.sum(jnp.where(started[None], blk_dtabs[:, :, None, :], 0), axis=-1)
    within = gc - picked[0]
    csrc = jnp.where(within < picked[1], picked[2] + within, 0).reshape(-1)
    return (gch.astype(jnp.int32), csrc.astype(jnp.int32), blk_eid, first, next_eid, used,
            n_act, n_rows)


def kernel(x, norm_mix, w_in, q_norm, k_norm, sinks, w_pool, pool_scale, w_out,
           norm_ffn, w_coarse, b_coarse, w_fine, b_fine, w_gate, w_up, w_down):
    bsz, seq, d = x.shape
    n = bsz * seq
    assert d == D_MODEL and seq % ROW_TILE == 0 and seq % BLOCK == 0
    assert norm_mix.shape[0] == 1, "single-layer problem"
    xf = x.reshape(n, d)

    w_in_b = w_in[0].astype(BF16)
    wq = w_in_b[:, :ATTN_WIDTH]
    wk = w_in_b[:, ATTN_WIDTH:ATTN_WIDTH + KV_WIDTH]
    wv = w_in_b[:, ATTN_WIDTH + KV_WIDTH:ATTN_WIDTH + 2 * KV_WIDTH]
    wu = w_in_b[:, ATTN_WIDTH + 2 * KV_WIDTH:]
    qn = jnp.tile(q_norm[0], N_HEADS).reshape(1, ATTN_WIDTH)
    kn = jnp.tile(k_norm[0], N_KV_HEADS).reshape(1, KV_WIDTH)
    cc, s_up, s_dn = _rope_tables(seq)
    lane_head = jnp.arange(MXU_DIM) // HEAD_DIM
    e_mat = (lane_head[:, None] == lane_head[None, :]).astype(BF16)
    w_pool_b = w_pool[0].astype(BF16)
    pscale = pool_scale[0].reshape(1, POOL_WIDTH)
    w_out_b = w_out[0].astype(BF16)
    wo_attn, wo_pool = w_out_b[:ATTN_WIDTH], w_out_b[ATTN_WIDTH:]
    gap = ROUTER_FINE0 - N_EXPERT_GROUPS
    tail = LANES - ROUTER_FINE0 - N_EXPERTS
    w_r = jnp.concatenate([w_coarse[0], jnp.zeros((d, gap), F32), w_fine[0],
                           jnp.zeros((d, tail), F32)], axis=1)
    w_r_hi = w_r.astype(BF16)
    w_r_lo = (w_r - w_r_hi.astype(F32)).astype(BF16)
    w_r2 = jnp.concatenate([w_r_hi, w_r_lo], axis=1)
    b_r = jnp.concatenate([b_coarse[0], jnp.zeros((gap,), F32), b_fine[0],
                           jnp.zeros((tail,), F32)]).reshape(1, LANES)

    tm = ROW_TILE
    tiles_per_seq = seq // tm
    n_tiles = n // tm
    idx = jnp.arange(SUB_TILE)
    tri = (idx[None, :] < idx[:, None]).astype(BF16)
    const = lambda *_: (0, 0)
    row_blk = lambda i: (i, 0)

    q, k, v, ksw, vsw, y = pl.pallas_call(
        functools.partial(_inproj_kernel, tiles_per_seq=tiles_per_seq),
        grid=(n_tiles,),
        in_specs=[
            pl.BlockSpec((tm, d), row_blk),
            pl.BlockSpec((1, d), const),
            pl.BlockSpec((d, ATTN_WIDTH), const),
            pl.BlockSpec((d, KV_WIDTH), const),
            pl.BlockSpec((d, KV_WIDTH), const),
            pl.BlockSpec((d, POOL_WIDTH), const),
            pl.BlockSpec((1, ATTN_WIDTH), const),
            pl.BlockSpec((1, KV_WIDTH), const),
            pl.BlockSpec((tm, LANES), lambda i: (i % tiles_per_seq, 0)),
            pl.BlockSpec((tm, LANES), lambda i: (i % tiles_per_seq, 0)),
            pl.BlockSpec((tm, LANES), lambda i: (i % tiles_per_seq, 0)),
            pl.BlockSpec((MXU_DIM, MXU_DIM), const),
            pl.BlockSpec((len(POOL_WINDOWS), POOL_GROUP_DIM, POOL_GROUP_DIM),
                         lambda i: (0, 0, 0)),
            pl.BlockSpec((1, POOL_WIDTH), const),
        ],
        out_specs=[
            pl.BlockSpec((tm, ATTN_WIDTH), row_blk),
            pl.BlockSpec((tm, KV_WIDTH), row_blk),
            pl.BlockSpec((tm, KV_WIDTH), row_blk),
            pl.BlockSpec((tm, KV_WIDTH), row_blk),
            pl.BlockSpec((tm, KV_WIDTH), row_blk),
            pl.BlockSpec((tm, POOL_WIDTH), row_blk),
        ],
        out_shape=[
            jax.ShapeDtypeStruct((n, ATTN_WIDTH), BF16),
            jax.ShapeDtypeStruct((n, KV_WIDTH), BF16),
            jax.ShapeDtypeStruct((n, KV_WIDTH), BF16),
            jax.ShapeDtypeStruct((n, KV_WIDTH), BF16),
            jax.ShapeDtypeStruct((n, KV_WIDTH), BF16),
            jax.ShapeDtypeStruct((n, POOL_WIDTH), BF16),
        ],
        scratch_shapes=[pltpu.VMEM((HALO + tm, POOL_WIDTH), F32)],
        compiler_params=_cparams(),
        name="inproj",
    )(xf, norm_mix[0].reshape(1, d), wq, wk, wv, wu, qn, kn, cc, s_up, s_dn,
      e_mat, w_pool_b, pscale)

    nb = seq // BLOCK
    nq = nb // Q_BLOCKS
    qrows = Q_BLOCKS * BLOCK
    cur = lambda b, j, *_: (b * nq + j, 0)
    prv = lambda b, j, *_: (b * nb + jnp.maximum(Q_BLOCKS * j - 1, 0), 0)
    kv_specs = [pl.BlockSpec((qrows, KV_WIDTH), cur), pl.BlockSpec((BLOCK, KV_WIDTH), prv)]
    attn = pl.pallas_call(
        _attn_kernel,
        grid_spec=pltpu.PrefetchScalarGridSpec(
            num_scalar_prefetch=1,
            grid=(bsz, nq),
            in_specs=[pl.BlockSpec((qrows, ATTN_WIDTH), cur)] + kv_specs * 4,
            out_specs=pl.BlockSpec((qrows, ATTN_WIDTH), cur),
        ),
        out_shape=jax.ShapeDtypeStruct((n, ATTN_WIDTH), BF16),
        compiler_params=_cparams(2),
        name="swa_attn",
    )(sinks[0].astype(F32), q, k, k, ksw, ksw, v, v, vsw, vsw)

    h, hpk, route, cnt = pl.pallas_call(
        _outproj_router_kernel,
        grid=(n_tiles,),
        in_specs=[
            pl.BlockSpec((tm, ATTN_WIDTH), row_blk),
            pl.BlockSpec((tm, POOL_WIDTH), row_blk),
            pl.BlockSpec((tm, d), row_blk),
            pl.BlockSpec((ATTN_WIDTH, d), const),
            pl.BlockSpec((POOL_WIDTH, d), const),
            pl.BlockSpec((1, d), const),
            pl.BlockSpec((d, 2 * LANES), const),
            pl.BlockSpec((1, LANES), const),
            pl.BlockSpec((SUB_TILE, SUB_TILE), const),
        ],
        out_specs=[
            pl.BlockSpec((tm, d), row_blk),
            pl.BlockSpec((tm * PK_ROWS, LANES), row_blk),
            pl.BlockSpec((SUBLANES, tm), lambda i: (0, i)),
            pl.BlockSpec((N_EXPERTS, LANES), row_blk),
        ],
        out_shape=[
            jax.ShapeDtypeStruct((n, d), F32),
            jax.ShapeDtypeStruct((n * PK_ROWS, LANES), jnp.int32),
            jax.ShapeDtypeStruct((SUBLANES, n), F32),
            jax.ShapeDtypeStruct((n_tiles * N_EXPERTS, LANES), F32),
        ],
        compiler_params=_cparams(),
        name="outproj_router",
    )(attn, y, xf, wo_attn, wo_pool, norm_ffn[0].reshape(1, d), w_r2, b_r, tri)

    assert tm == LOCAL_TILE
    gates = route[0:TOP_K].reshape(TOP_K * n)
    lp = route[TOP_K:2 * TOP_K].astype(jnp.int32).reshape(TOP_K * n)
    tile_cnt = cnt.reshape(n_tiles, N_EXPERTS, LANES)[:, :, 0].astype(jnp.int32)
    te = EXPERT_TILE
    gch, csrc, blk_eid, first, next_eid, used, n_act, n_rows = _plan(tile_cnt, n)
    n_blocks = n_rows // te
    n_local = n // LOCAL_TILE

    xs = pl.pallas_call(
        _local_sort_kernel,
        grid_spec=pltpu.PrefetchScalarGridSpec(
            num_scalar_prefetch=1,
            grid=(n_local,),
            in_specs=[pl.BlockSpec((LOCAL_TILE * PK_ROWS, LANES), lambda i, *_: (i, 0))],
            out_specs=pl.BlockSpec((LT_MAX * PK_ROWS, LANES), lambda i, *_: (i, 0)),
        ),
        out_shape=jax.ShapeDtypeStruct((n_local * LT_MAX * PK_ROWS, LANES), jnp.int32),
        compiler_params=_cparams(),
        name="local_sort",
    )(lp * PK_ROWS, hpk)

    ys = pl.pallas_call(
        _expert_kernel,
        grid_spec=pltpu.PrefetchScalarGridSpec(
            num_scalar_prefetch=6,
            grid=(n_blocks,),
            in_specs=[pl.BlockSpec(memory_space=pl.ANY)] * 4,
            out_specs=pl.BlockSpec((te * PK_ROWS, LANES), lambda i, *_: (i, 0)),
            scratch_shapes=[
                pltpu.VMEM((2, te * PK_ROWS, LANES), jnp.int32),
                pltpu.VMEM((d, D_EXPERT), F32),
                pltpu.VMEM((d, D_EXPERT), F32),
                pltpu.VMEM((D_EXPERT, d), F32),
                pltpu.VMEM((d, 2 * D_EXPERT), BF16),
                pltpu.VMEM((D_EXPERT, d), BF16),
                pltpu.SemaphoreType.DMA((2,)),
                pltpu.SemaphoreType.DMA((3,)),
            ],
        ),
        out_shape=jax.ShapeDtypeStruct((n_rows * PK_ROWS, LANES), jnp.int32),
        compiler_params=_cparams(),
        name="experts",
    )(blk_eid, first, n_act, csrc * (CHUNK * PK_ROWS), next_eid, used, xs,
      w_gate[0], w_up[0], w_down[0])

    tt = LOCAL_TILE
    out = pl.pallas_call(
        _combine_kernel,
        grid_spec=pltpu.PrefetchScalarGridSpec(
            num_scalar_prefetch=3,
            grid=(n_local,),
            in_specs=[
                pl.BlockSpec((tt, d), lambda i, *_: (i, 0)),
                pl.BlockSpec(memory_space=pl.ANY),
            ],
            out_specs=pl.BlockSpec((tt, d), lambda i, *_: (i, 0)),
            scratch_shapes=[
                pltpu.VMEM((2, LT_MAX * PK_ROWS, LANES), jnp.int32),
                pltpu.VMEM((2, tt * PK_ROWS, LANES), F32),
                pltpu.SemaphoreType.DMA((2,)),
            ],
        ),
        out_shape=jax.ShapeDtypeStruct((n, d), F32),
        compiler_params=_cparams(),
        name="combine",
    )(gch * (CHUNK * PK_ROWS), lp * PK_ROWS, gates, h, ys)
    return out.reshape(bsz, seq, d)
```
